```python
import math
import jax, jax.numpy as jnp
from jax import lax
import numpy as np

D_MODEL = 1024
BATCH = 16
SEQ = 256
DEPTH = 2
DEC_BATCH = 8
DEC_SEQ = 1024
PAST_LEN = 256

GRID_W = 64
ROPE_BASE = 10000.0
S5_WIDTH = 512
S5_GROUP = 16
S5_GROUPS = S5_WIDTH // S5_GROUP
S5_STATE = 64
GLA_HEADS = 4
GLA_DK = 64
GLA_DV = 128
GLA_QK = GLA_HEADS * GLA_DK
GLA_V = GLA_HEADS * GLA_DV
GLA_RANK = 16
GLA_NORMALIZER = 16.0
GLA_CHUNK = 32
ATT_HEADS = 8
ATT_KV_HEADS = 2
HEAD_DIM = 64
Q_PER_KV = ATT_HEADS // ATT_KV_HEADS
ATT_Q = ATT_HEADS * HEAD_DIM
ATT_KV = ATT_KV_HEADS * HEAD_DIM
WINDOW = 128
ATT_BLOCK = 128
N_BRANCH = 3
BRANCH_W = 512
IN_SIZES = (S5_WIDTH, GLA_QK, GLA_QK, GLA_V, GLA_V, GLA_RANK, GLA_RANK, ATT_Q, ATT_KV, ATT_KV, N_BRANCH * D_MODEL)
D_IN = sum(IN_SIZES)
FFN_HIDDEN = -(-8 * D_MODEL // (3 * 256)) * 256
RMS_EPS = 1e-6

kernel_name = 'hybrid_diffusion_s5_gla_swa_prefix_step'


def rmsnorm(x, g):
    xf = x.astype(jnp.float32)
    y = xf * lax.rsqrt(jnp.mean(xf * xf, axis=-1, keepdims=True) + RMS_EPS)
    return (y * g.astype(jnp.float32)).astype(x.dtype)


def axial_rope(x):
    L = x.shape[1]
    rows = L // GRID_W
    row = jnp.repeat(jnp.arange(rows, dtype=jnp.float32), GRID_W)
    col = jnp.tile(jnp.arange(GRID_W, dtype=jnp.float32), rows)
    half = HEAD_DIM // 2
    quarter = half // 2
    inv = ROPE_BASE ** (-jnp.arange(quarter, dtype=jnp.float32) / quarter)

    def rot(xa, pos):
        ang = pos[:, None] * inv[None, :]
        cos = jnp.cos(ang)[None, :, None, :]
        sin = jnp.sin(ang)[None, :, None, :]
        x1, x2 = xa[..., :quarter], xa[..., quarter:]
        return jnp.concatenate([x1 * cos - x2 * sin, x1 * sin + x2 * cos], axis=-1)

    return jnp.concatenate([rot(x[..., :half], row), rot(x[..., half:], col)], axis=-1)


def s5_discretize(lam_re, lam_im, log_step, b_re, b_im):
    lam_re = lam_re.astype(jnp.float32)
    lam_im = lam_im.astype(jnp.float32)
    dt = jnp.exp(log_step.astype(jnp.float32))
    mag = jnp.exp(lam_re * dt)
    ar = mag * jnp.cos(lam_im * dt)
    ai = mag * jnp.sin(lam_im * dt)
    nr, ni = ar - 1.0, ai
    den = lam_re * lam_re + lam_im * lam_im
    fr = (nr * lam_re + ni * lam_im) / den
    fi = (ni * lam_re - nr * lam_im) / den
    b_re = b_re.astype(jnp.float32)
    b_im = b_im.astype(jnp.float32)
    br = fr[..., None] * b_re - fi[..., None] * b_im
    bi = fr[..., None] * b_im + fi[..., None] * b_re
    return ar, ai, br, bi


def s5_scan(u, ar, ai, br, bi, h0_re, h0_im):
    bu_re = jnp.einsum('blgs,gps->blgp', u, br)
    bu_im = jnp.einsum('blgs,gps->blgp', u, bi)
    bu_re = bu_re.at[:, 0].add(ar * h0_re - ai * h0_im)
    bu_im = bu_im.at[:, 0].add(ar * h0_im + ai * h0_re)
    a_re = jnp.broadcast_to(ar, bu_re.shape)
    a_im = jnp.broadcast_to(ai, bu_im.shape)

    def combine(e1, e2):
        a1r, a1i, b1r, b1i = e1
        a2r, a2i, b2r, b2i = e2
        return (a1r * a2r - a1i * a2i, a1r * a2i + a1i * a2r,
                a2r * b1r - a2i * b1i + b2r, a2r * b1i + a2i * b1r + b2i)

    _, _, hr, hi = lax.associative_scan(combine, (a_re, a_im, bu_re, bu_im), axis=1)
    return hr, hi


def s5_branch(u_flat, h0, lp, want_state):
    B_, L, _ = u_flat.shape
    u = u_flat.astype(jnp.float32).reshape(B_, L, S5_GROUPS, S5_GROUP)
    h0 = h0.astype(jnp.float32)
    y = u * lp['s5_d'].astype(jnp.float32).reshape(S5_GROUPS, S5_GROUP)
    finals = []
    for d in range(2):
        ud = u if d == 0 else u[:, ::-1]
        ar, ai, br, bi = s5_discretize(lp['s5_lam_re'][d], lp['s5_lam_im'][d], lp['s5_log_step'][d],
                                       lp['s5_b_re'][d], lp['s5_b_im'][d])
        hr, hi = s5_scan(ud, ar, ai, br, bi, h0[:, d, :, :, 0], h0[:, d, :, :, 1])
        yd = (jnp.einsum('blgp,gsp->blgs', hr, lp['s5_c_re'][d].astype(jnp.float32))
              - jnp.einsum('blgp,gsp->blgs', hi, lp['s5_c_im'][d].astype(jnp.float32)))
        y = y + (yd if d == 0 else yd[:, ::-1])
        if want_state:
            finals.append(jnp.stack([hr[:, -1], hi[:, -1]], axis=-1))
    y = jax.nn.gelu(y.reshape(B_, L, S5_WIDTH))
    a, g = jnp.split(y @ lp['w_glu'].astype(jnp.float32), 2, axis=-1)
    out = (a * jax.nn.sigmoid(g)).astype(u_flat.dtype)
    fin = jnp.stack(finals, axis=1) if want_state else None
    return out, fin


def gla_chunked(q, k, v, gk, s0):
    B_, L, H, _ = q.shape
    n = L // GLA_CHUNK

    def chunk(t):
        return t.reshape(B_, n, GLA_CHUNK, H, t.shape[-1]).transpose(1, 0, 3, 2, 4)

    qc, kc, vc, gc = chunk(q), chunk(k), chunk(v), chunk(gk)
    bcum = jnp.cumsum(gc, axis=3)
    mask = jnp.tril(jnp.ones((GLA_CHUNK, GLA_CHUNK), dtype=bool))
    diff = bcum[..., :, None, :] - bcum[..., None, :, :]
    decay = jnp.exp(jnp.where(mask[:, :, None], diff, -jnp.inf))
    att = jnp.einsum('nbhid,nbhjd,nbhijd->nbhij', qc, kc, decay)
    o_intra = jnp.einsum('nbhij,nbhjv->nbhiv', att, vc)
    blast = bcum[..., -1:, :]
    q_in = qc * jnp.exp(bcum)
    k_out = kc * jnp.exp(blast - bcum)
    d_last = jnp.exp(blast[..., 0, :])

    def step(s, inp):
        qi, ko, vi, dl = inp
        o = jnp.einsum('bhcd,bhdv->bhcv', qi, s)
        s = s * dl[..., None] + jnp.einsum('bhcd,bhcv->bhdv', ko, vi)
        return s, o

    s_fin, o_inter = lax.scan(step, s0, (q_in, k_out, vc, d_last))
    o = (o_intra + o_inter).transpose(1, 0, 3, 2, 4).reshape(B_, L, H, v.shape[-1])
    return o, s_fin


def gla_branch(q_b, k_b, v_b, g_b, lr_f, lr_b, s0, lp, want_state):
    B_, L, _ = q_b.shape
    q = q_b.astype(jnp.float32).reshape(B_, L, GLA_HEADS, GLA_DK) * (GLA_DK ** -0.5)
    k = k_b.astype(jnp.float32).reshape(B_, L, GLA_HEADS, GLA_DK)
    v = v_b.astype(jnp.float32).reshape(B_, L, GLA_HEADS, GLA_DV)
    s0 = s0.astype(jnp.float32)
    o = None
    finals = []
    for d, lr in enumerate((lr_f, lr_b)):
        gk = jax.nn.log_sigmoid(lr.astype(jnp.float32) @ lp['gla_w_gk'][d].astype(jnp.float32)
                                + lp['gla_b_gk'][d].astype(jnp.float32)) / GLA_NORMALIZER
        gk = gk.reshape(B_, L, GLA_HEADS, GLA_DK)
        if d == 0:
            od, sd = gla_chunked(q, k, v, gk, s0[:, 0])
        else:
            od, sd = gla_chunked(q[:, ::-1], k[:, ::-1], v[:, ::-1], gk[:, ::-1], s0[:, 1])
            od = od[:, ::-1]
        o = od if o is None else o + od
        if want_state:
            finals.append(sd)
    o = rmsnorm(o, lp['gla_norm_g']).reshape(B_, L, GLA_V) * jax.nn.silu(g_b.astype(jnp.float32))
    fin = jnp.stack(finals, axis=1) if want_state else None
    return o.astype(q_b.dtype), fin


def context_attention(q, k, v, sink):
    B_, L = q.shape[:2]
    nb = L // ATT_BLOCK
    scale = HEAD_DIM ** -0.5
    qb = q.reshape(B_, nb, ATT_BLOCK, ATT_KV_HEADS, Q_PER_KV, HEAD_DIM).transpose(1, 0, 2, 3, 4, 5)
    sk = sink.astype(jnp.float32).reshape(ATT_KV_HEADS, Q_PER_KV)[None, :, :, None, None]

    def block(qblk):
        s = jnp.einsum('bqkgd,bskd->bkgqs', qblk, k) * scale
        m = jnp.maximum(jnp.max(s, axis=-1, keepdims=True), sk)
        p = jnp.exp(s - m)
        den = jnp.sum(p, axis=-1, keepdims=True) + jnp.exp(sk - m)
        o = jnp.einsum('bkgqs,bskd->bqkgd', p / den, v)
        return o

    o = lax.map(block, qb)
    return o.transpose(1, 0, 2, 3, 4, 5).reshape(B_, L, ATT_Q)


def latent_attention(q, k, v, ck, cv, sink):
    B_, L = q.shape[:2]
    nb = L // ATT_BLOCK
    scale = HEAD_DIM ** -0.5
    qb = q.reshape(B_, nb, ATT_BLOCK, ATT_KV_HEADS, Q_PER_KV, HEAD_DIM)

    def band(t):
        tb = t.reshape(B_, nb, ATT_BLOCK, ATT_KV_HEADS, HEAD_DIM)
        pad = jnp.pad(tb, ((0, 0), (1, 1), (0, 0), (0, 0), (0, 0)))
        return jnp.concatenate([pad[:, :-2], pad[:, 1:-1], pad[:, 2:]], axis=2)

    kw, vw = band(k), band(v)
    qpos = jnp.arange(ATT_BLOCK)[:, None]
    kpos = jnp.arange(3 * ATT_BLOCK)[None, :] - ATT_BLOCK
    abs_k = jnp.arange(nb)[:, None, None] * ATT_BLOCK + kpos[None]
    valid = (jnp.abs(kpos - qpos)[None] <= WINDOW) & (abs_k >= 0) & (abs_k < L)
    s_w = jnp.einsum('bnqkgd,bnskd->bnkgqs', qb, kw) * scale
    s_w = jnp.where(valid[None, :, None, None], s_w, -jnp.inf)
    s_c = jnp.einsum('bnqkgd,bskd->bnkgqs', qb, ck) * scale
    sk = sink.astype(jnp.float32).reshape(ATT_KV_HEADS, Q_PER_KV)[None, None, :, :, None, None]
    m = jnp.maximum(jnp.maximum(jnp.max(s_w, axis=-1, keepdims=True), jnp.max(s_c, axis=-1, keepdims=True)), sk)
    p_w = jnp.exp(s_w - m)
    p_c = jnp.exp(s_c - m)
    inv = 1.0 / (jnp.sum(p_w, axis=-1, keepdims=True) + jnp.sum(p_c, axis=-1, keepdims=True) + jnp.exp(sk - m))
    o = (jnp.einsum('bnkgqs,bnskd->bnqkgd', p_w * inv, vw)
         + jnp.einsum('bnkgqs,bskd->bnqkgd', p_c * inv, cv))
    return o.reshape(B_, L, ATT_Q)


def token_mix(h, lp, ctx):
    B_, L, _ = h.shape
    is_ctx = ctx is None
    z = h @ lp['w_in']
    offs = [int(o) for o in np.cumsum(IN_SIZES)[:-1]]
    u_a, q_b, k_b, v_b, g_b, lr_f, lr_b, q_c, k_c, v_c, gate_logits = jnp.split(z, offs, axis=-1)
    if is_ctx:
        s5_h0 = jnp.zeros((B_, 2, S5_GROUPS, S5_STATE, 2), jnp.float32)
        gla_s0 = jnp.zeros((B_, 2, GLA_HEADS, GLA_DK, GLA_DV), jnp.float32)
    else:
        ck, cv, s5_h0, gla_s0 = ctx
    y_a, s5_fin = s5_branch(u_a, s5_h0, lp, is_ctx)
    y_b, gla_fin = gla_branch(q_b, k_b, v_b, g_b, lr_f, lr_b, gla_s0, lp, is_ctx)
    k_heads = k_c.reshape(B_, L, ATT_KV_HEADS, HEAD_DIM)
    v_heads = v_c.reshape(B_, L, ATT_KV_HEADS, HEAD_DIM)
    q = q_c.astype(jnp.float32).reshape(B_, L, ATT_HEADS, HEAD_DIM)
    k = k_heads.astype(jnp.float32)
    v = v_heads.astype(jnp.float32)
    if is_ctx:
        y_c = context_attention(q, k, v, lp['att_sink'])
        new_ctx = (k_heads, v_heads, s5_fin, gla_fin)
    else:
        y_c = latent_attention(axial_rope(q), axial_rope(k), v, ck.astype(jnp.float32),
                               cv.astype(jnp.float32), lp['att_sink'])
        new_ctx = None
    ys = jnp.stack([y_a, y_b, y_c.astype(h.dtype)], axis=2)
    proj = jnp.einsum('blnc,ncd->blnd', ys, lp['w_branch'])
    gates = jax.nn.sigmoid(gate_logits.reshape(B_, L, N_BRANCH, D_MODEL))
    merged = jnp.sum(gates * proj, axis=2)
    return merged @ lp['w_out'], new_ctx


def swiglu(h, w1, w2):
    a, b = jnp.split(h @ w1, 2, axis=-1)
    return (jax.nn.silu(a) * b) @ w2


def trunk_layer(x, mod, lp, ctx):
    sh1, sc1, g1, sh2, sc2, g2 = jnp.split(mod[:, None, :], 6, axis=-1)
    h = rmsnorm(x, lp['norm_g'][0]) * (1 + sc1) + sh1
    mixed, new_ctx = token_mix(h, lp, ctx)
    x = x + g1 * rmsnorm(mixed, lp['norm_g'][1])
    h = rmsnorm(x, lp['norm_g'][2]) * (1 + sc2) + sh2
    x = x + g2 * rmsnorm(swiglu(h, lp['w_ffn_in'], lp['w_ffn_out']), lp['norm_g'][3])
    return x, new_ctx


def setup_inputs(seed: int = 0) -> dict:
    key = jax.random.key(seed)
    k = jax.random.split(key, 32)
    f32 = jnp.float32

    def nrm(i, shape, scale=1.0):
        return jax.random.normal(k[i], shape, f32) * scale

    s5_shape = (DEPTH, 2, S5_GROUPS, S5_STATE)
    n_idx = jnp.arange(S5_STATE, dtype=f32)
    return {
        'x_prompt': nrm(0, (BATCH, SEQ, D_MODEL)),
        'x_sample': nrm(1, (DEC_BATCH, DEC_SEQ, D_MODEL)),
        'cache_k': nrm(2, (DEC_BATCH, DEPTH, PAST_LEN, ATT_KV_HEADS, HEAD_DIM)),
        'cache_v': nrm(3, (DEC_BATCH, DEPTH, PAST_LEN, ATT_KV_HEADS, HEAD_DIM)),
        'state_s5': nrm(4, (DEC_BATCH, DEPTH, 2, S5_GROUPS, S5_STATE, 2), 0.3),
        'state_gla': nrm(5, (DEC_BATCH, DEPTH, 2, GLA_HEADS, GLA_DK, GLA_DV), 0.5),
        'c': nrm(6, (DEC_BATCH, D_MODEL)),
        'c_ctx': nrm(7, (D_MODEL,)),
        'w_mod': nrm(8, (DEPTH, D_MODEL, 6 * D_MODEL), 0.5 * D_MODEL ** -0.5),
        'b_mod': nrm(9, (DEPTH, 6 * D_MODEL), 0.02),
        'norm_g': 1.0 + nrm(10, (DEPTH, 4, D_MODEL), 0.02),
        'w_in': nrm(11, (DEPTH, D_MODEL, D_IN), D_MODEL ** -0.5),
        's5_lam_re': -0.5 + nrm(12, s5_shape, 0.01),
        's5_lam_im': math.pi * n_idx + nrm(13, s5_shape, 0.01),
        's5_log_step': jax.random.uniform(k[14], s5_shape, f32, math.log(1e-3), math.log(1e-1)),
        's5_b_re': nrm(15, s5_shape + (S5_GROUP,), (2 * S5_GROUP) ** -0.5),
        's5_b_im': nrm(16, s5_shape + (S5_GROUP,), (2 * S5_GROUP) ** -0.5),
        's5_c_re': nrm(17, (DEPTH, 2, S5_GROUPS, S5_GROUP, S5_STATE), (2 * S5_STATE) ** -0.5),
        's5_c_im': nrm(18, (DEPTH, 2, S5_GROUPS, S5_GROUP, S5_STATE), (2 * S5_STATE) ** -0.5),
        's5_d': nrm(19, (DEPTH, S5_WIDTH)),
        'w_glu': nrm(20, (DEPTH, S5_WIDTH, 2 * S5_WIDTH), S5_WIDTH ** -0.5),
        'gla_w_gk': nrm(21, (DEPTH, 2, GLA_RANK, GLA_QK), GLA_RANK ** -0.5),
        'gla_b_gk': nrm(22, (DEPTH, 2, GLA_QK), 0.1),
        'gla_norm_g': 1.0 + nrm(23, (DEPTH, GLA_DV), 0.02),
        'att_sink': nrm(24, (DEPTH, ATT_HEADS), 0.5),
        'w_branch': nrm(25, (DEPTH, N_BRANCH, BRANCH_W, D_MODEL), BRANCH_W ** -0.5),
        'w_out': nrm(26, (DEPTH, D_MODEL, D_MODEL), D_MODEL ** -0.5),
        'w_ffn_in': nrm(27, (DEPTH, D_MODEL, 2 * FFN_HIDDEN), D_MODEL ** -0.5),
        'w_ffn_out': nrm(28, (DEPTH, FFN_HIDDEN, D_MODEL), FFN_HIDDEN ** -0.5),
    }


def reference(x_prompt, x_sample, cache_k, cache_v, state_s5, state_gla, c, c_ctx,
              w_mod, b_mod, norm_g, w_in, s5_lam_re, s5_lam_im, s5_log_step, s5_b_re, s5_b_im,
              s5_c_re, s5_c_im, s5_d, w_glu, gla_w_gk, gla_b_gk, gla_norm_g, att_sink,
              w_branch, w_out, w_ffn_in, w_ffn_out):
    cond_ctx = jax.nn.silu(c_ctx)[None, :]
    cond_lat = jax.nn.silu(c)
    xp, xs = x_prompt, x_sample
    new_k, new_v, new_s5, new_gla = [], [], [], []
    for i in range(DEPTH):
        lp = {
            'norm_g': norm_g[i], 'w_in': w_in[i],
            's5_lam_re': s5_lam_re[i], 's5_lam_im': s5_lam_im[i], 's5_log_step': s5_log_step[i],
            's5_b_re': s5_b_re[i], 's5_b_im': s5_b_im[i], 's5_c_re': s5_c_re[i], 's5_c_im': s5_c_im[i],
            's5_d': s5_d[i], 'w_glu': w_glu[i],
            'gla_w_gk': gla_w_gk[i], 'gla_b_gk': gla_b_gk[i], 'gla_norm_g': gla_norm_g[i],
            'att_sink': att_sink[i], 'w_branch': w_branch[i], 'w_out': w_out[i],
            'w_ffn_in': w_ffn_in[i], 'w_ffn_out': w_ffn_out[i],
        }
        xp, (k_i, v_i, s5_i, gla_i) = trunk_layer(xp, cond_ctx @ w_mod[i] + b_mod[i], lp, None)
        new_k.append(k_i)
        new_v.append(v_i)
        new_s5.append(s5_i)
        new_gla.append(gla_i)
        xs, _ = trunk_layer(xs, cond_lat @ w_mod[i] + b_mod[i], lp,
                            (cache_k[:, i], cache_v[:, i], state_s5[:, i], state_gla[:, i]))
    return (xp, xs, jnp.stack(new_k, axis=1), jnp.stack(new_v, axis=1),
            jnp.stack(new_s5, axis=1), jnp.stack(new_gla, axis=1))
```

```python
import functools
import math

import numpy as np
import jax
import jax.numpy as jnp
from jax import lax
from jax.experimental import pallas as pl
from jax.experimental.pallas import tpu as pltpu

F32 = jnp.float32
BF16 = jnp.bfloat16

D_MODEL = 1024
BATCH = 16
SEQ = 256
DEPTH = 2
DEC_BATCH = 8
DEC_SEQ = 1024
PAST_LEN = 256
GRID_W = 64
ROPE_BASE = 10000.0
S5_WIDTH = 512
S5_GROUP = 16
S5_GROUPS = 32
S5_STATE = 64
GLA_HEADS = 4
GLA_DK = 64
GLA_DV = 128
GLA_QK = 256
GLA_V = 512
GLA_RANK = 16
GLA_NORMALIZER = 16.0
ATT_HEADS = 8
ATT_KV_HEADS = 2
HEAD_DIM = 64
ATT_Q = 512
ATT_KV = 128
WINDOW = 128
ATT_BLOCK = 128
N_BRANCH = 3
BRANCH_W = 512
FFN_HIDDEN = 2816
RMS_EPS = 1e-6

NTOK_C = BATCH * SEQ
NTOK_L = DEC_BATCH * DEC_SEQ
NTOK = NTOK_C + NTOK_L
TM = 256
N_MOD_ROWS = 16

W_IN_COLS = 6016
S5_CHUNK = 16
S5_ROWS_C = (SEQ // S5_CHUNK) * BATCH
S5_ROWS_L = (DEC_SEQ // S5_CHUNK) * DEC_BATCH
GLA_BLK = 256
GLA_LEVELS = 8
VMEM_LIMIT = 56 * 1024 * 1024


def _dot(a, b):
    return jnp.dot(a, b, preferred_element_type=F32)


def _dot_nt(a, b):
    return lax.dot_general(a, b, (((1,), (1,)), ((), ())), preferred_element_type=F32)


def _dot_tn(a, b):
    return lax.dot_general(a, b, (((0,), (0,)), ((), ())), preferred_element_type=F32)


def _rms(x, g):
    return x * lax.rsqrt(jnp.mean(x * x, axis=-1, keepdims=True) + RMS_EPS) * g


def _sigmoid(x):
    return 1.0 / (1.0 + jnp.exp(-x))


def _mod_row(i):
    nct = NTOK_C // TM
    return jnp.where(i < nct, 0, 1 + (i - nct) // (DEC_SEQ // TM))


def _mod_kernel(c_ref, w_ref, b_ref, o_ref):
    c = c_ref[...]
    s = (c * _sigmoid(c)).astype(BF16)
    o_ref[...] = _dot(s, w_ref[...].astype(BF16)) + b_ref[...]


def _modulation(cond, w_mod, b_mod):
    tn = 1024
    return pl.pallas_call(
        _mod_kernel,
        grid=(DEPTH, 6 * D_MODEL // tn),
        in_specs=[
            pl.BlockSpec((N_MOD_ROWS, D_MODEL), lambda l, n: (0, 0)),
            pl.BlockSpec((None, D_MODEL, tn), lambda l, n: (l, 0, n)),
            pl.BlockSpec((None, 1, tn), lambda l, n: (l, 0, n)),
        ],
        out_specs=pl.BlockSpec((None, N_MOD_ROWS, tn), lambda l, n: (l, 0, n)),
        out_shape=jax.ShapeDtypeStruct((DEPTH, N_MOD_ROWS, 6 * D_MODEL), F32),
        name="modulation",
    )(cond, w_mod, b_mod.reshape(DEPTH, 1, 6 * D_MODEL))


_IN_SLABS = ((0, 512), (512, 1536), (2048, 768), (2816, 3072), (5888, 128))


def _inproj_kernel(x_ref, mod_ref, g_ref, w_ref, u_ref, b_ref, c_ref, gate_ref, lr_ref):
    x = x_ref[...]
    mod = mod_ref[...]
    h = _rms(x, g_ref[...]) * (1.0 + mod[:, D_MODEL:2 * D_MODEL]) + mod[:, 0:D_MODEL]
    h = h.astype(BF16)
    for (off, width), o_ref in zip(_IN_SLABS, (u_ref, b_ref, c_ref, gate_ref, lr_ref)):
        o_ref[...] = _dot(h, w_ref[:, off:off + width])


def _inproj(x, mod, g, w):
    return pl.pallas_call(
        _inproj_kernel,
        grid=(NTOK // TM,),
        in_specs=[
            pl.BlockSpec((TM, D_MODEL), lambda i: (i, 0)),
            pl.BlockSpec((None, 1, 6 * D_MODEL), lambda i: (_mod_row(i), 0, 0)),
            pl.BlockSpec((1, D_MODEL), lambda i: (0, 0)),
            pl.BlockSpec((D_MODEL, W_IN_COLS), lambda i: (0, 0)),
        ],
        out_specs=[pl.BlockSpec((TM, width), lambda i: (i, 0)) for _, width in _IN_SLABS],
        out_shape=[jax.ShapeDtypeStruct((NTOK, width), F32) for _, width in _IN_SLABS],
        compiler_params=pltpu.CompilerParams(vmem_limit_bytes=VMEM_LIMIT),
        name="inproj",
    )(x, mod, g, w)


def _s5_prep(lam_re, lam_im, log_step, b_re, b_im, c_re, c_im, d_skip):
    hp = lax.Precision.HIGHEST
    n = S5_CHUNK
    lam_re = lam_re.astype(F32)
    lam_im = lam_im.astype(F32)
    dt = jnp.exp(log_step.astype(F32))
    mag = jnp.exp(lam_re * dt)
    ar = mag * jnp.cos(lam_im * dt)
    ai = mag * jnp.sin(lam_im * dt)
    nr, ni = ar - 1.0, ai
    den = lam_re * lam_re + lam_im * lam_im
    fr = (nr * lam_re + ni * lam_im) / den
    fi = (ni * lam_re - nr * lam_im) / den
    b_re = b_re.astype(F32)
    b_im = b_im.astype(F32)
    br = fr[..., None] * b_re - fi[..., None] * b_im
    bi = fr[..., None] * b_im + fi[..., None] * b_re
    cr = c_re.astype(F32)
    ci = c_im.astype(F32)
    pr = [jnp.ones_like(ar)]
    pi = [jnp.zeros_like(ar)]
    for _ in range(n):
        pr.append(pr[-1] * ar - pi[-1] * ai)
        pi.append(pr[-2] * ai + pi[-1] * ar)
    pr = jnp.stack(pr)
    pi = jnp.stack(pi)
    wr = cr[None] * pr[:, :, :, None, :] - ci[None] * pi[:, :, :, None, :]
    wi = cr[None] * pi[:, :, :, None, :] + ci[None] * pr[:, :, :, None, :]
    kern = (jnp.einsum('ndgop,dgpi->ndgoi', wr, br, precision=hp)
            - jnp.einsum('ndgop,dgpi->ndgoi', wi, bi, precision=hp))
    s_idx = np.arange(n)[:, None]
    t_idx = np.arange(n)[None, :]
    lag_f = np.clip(t_idx - s_idx, 0, n)
    lag_b = np.clip(s_idx - t_idx, 0, n)
    tf = jnp.where((t_idx >= s_idx)[:, :, None, None, None], kern[lag_f, 0], 0.0)
    tb = jnp.where((s_idx >= t_idx)[:, :, None, None, None], kern[lag_b, 1], 0.0)
    toep = (tf + tb).transpose(2, 0, 4, 1, 3).reshape(S5_GROUPS, n * S5_GROUP, n * S5_GROUP)
    pf_r, pf_i = pr[n - 1 - np.arange(n), 0], pi[n - 1 - np.arange(n), 0]
    pb_r, pb_i = pr[np.arange(n), 1], pi[np.arange(n), 1]

    def eb_part(p_r, p_i, b_r, b_i):
        re = p_r[:, :, :, None] * b_r[None] - p_i[:, :, :, None] * b_i[None]
        im = p_r[:, :, :, None] * b_i[None] + p_i[:, :, :, None] * b_r[None]
        to_rows = lambda a: a.transpose(1, 0, 3, 2).reshape(S5_GROUPS, n * S5_GROUP, S5_STATE)
        return to_rows(re), to_rows(im)

    ef_r, ef_i = eb_part(pf_r, pf_i, br[0], bi[0])
    eb_r, eb_i = eb_part(pb_r, pb_i, br[1], bi[1])
    eb = jnp.concatenate([ef_r, eb_r, ef_i, eb_i], axis=-1)
    def ca_part(w_r, w_i):
        to_cols = lambda a: a.transpose(1, 3, 0, 2).reshape(S5_GROUPS, S5_STATE, n * S5_GROUP)
        return to_cols(w_r), to_cols(-w_i)

    cf_r, cf_i = ca_part(wr[1 + np.arange(n), 0], wi[1 + np.arange(n), 0])
    cb_r, cb_i = ca_part(wr[n - np.arange(n), 1], wi[n - np.arange(n), 1])
    ca = jnp.concatenate([cf_r, cb_r, cf_i, cb_i], axis=1)
    a16 = jnp.concatenate([pr[n, 0], pr[n, 1], pi[n, 0], pi[n, 1]], axis=-1)[:, None, :]
    dtile = jnp.tile(d_skip.astype(F32).reshape(S5_GROUPS, 1, S5_GROUP), (1, 1, n))
    return toep.astype(BF16), eb.astype(BF16), ca.astype(BF16), a16, dtile


def _s5_kernel(u_ref, t_ref, eb_ref, ca_ref, a16_ref, d_ref, h0c_ref, h0l_ref,
               y_ref, finc_ref, finl_ref, sloc, hin_f, hin_b):
    u = u_ref[...]
    ub = u.astype(BF16)
    sloc[...] = _dot(ub, eb_ref[...])
    a_re = a16_ref[:, 0:128]
    a_im = a16_ref[:, 128:256]

    def scan(row0, nc, nb, h0_ref, fin_ref):
        is_f = lax.broadcasted_iota(jnp.int32, (nb, 128), 1) < S5_STATE

        def body(kk, carry):
            h_re, h_im = carry
            rf = pl.multiple_of(row0 + kk * nb, 8)
            rb = pl.multiple_of(row0 + (nc - 1 - kk) * nb, 8)
            hin_f[pl.ds(rf, nb), 0:128] = h_re
            hin_f[pl.ds(rf, nb), 128:256] = h_im
            hin_b[pl.ds(rb, nb), 0:128] = h_re
            hin_b[pl.ds(rb, nb), 128:256] = h_im
            s_re = jnp.where(is_f, sloc[pl.ds(rf, nb), 0:128], sloc[pl.ds(rb, nb), 0:128])
            s_im = jnp.where(is_f, sloc[pl.ds(rf, nb), 128:256], sloc[pl.ds(rb, nb), 128:256])
            return (a_re * h_re - a_im * h_im + s_re, a_re * h_im + a_im * h_re + s_im)

        h_re, h_im = lax.fori_loop(0, nc, body, (h0_ref[:, 0:128], h0_ref[:, 128:256]))
        fin_ref[:, 0:128] = h_re
        fin_ref[:, 128:256] = h_im

    scan(0, SEQ // S5_CHUNK, BATCH, h0c_ref, finc_ref)
    scan(S5_ROWS_C, DEC_SEQ // S5_CHUNK, DEC_BATCH, h0l_ref, finl_ref)
    rows = S5_ROWS_C + S5_ROWS_L
    fwd_cols = (lax.broadcasted_iota(jnp.int32, (rows, 256), 1) & 127) < S5_STATE
    hin = jnp.where(fwd_cols, hin_f[...], hin_b[...]).astype(BF16)
    y_ref[...] = _dot(ub, t_ref[...]) + _dot(hin, ca_ref[...]) + u * d_ref[...]


def _s5_mix(ug, toep, eb, ca, a16, dtile, h0c, h0l):
    rows = S5_ROWS_C + S5_ROWS_L
    mat = pl.BlockSpec((None, 256, 256), lambda g: (g, 0, 0))
    vec = pl.BlockSpec((None, 1, 256), lambda g: (g, 0, 0))
    return pl.pallas_call(
        _s5_kernel,
        grid=(S5_GROUPS,),
        in_specs=[
            pl.BlockSpec((None, rows, 256), lambda g: (g, 0, 0)),
            mat, mat, mat, vec, vec,
            pl.BlockSpec((None, BATCH, 256), lambda g: (g, 0, 0)),
            pl.BlockSpec((None, DEC_BATCH, 256), lambda g: (g, 0, 0)),
        ],
        out_specs=[
            pl.BlockSpec((None, rows, 256), lambda g: (g, 0, 0)),
            pl.BlockSpec((None, BATCH, 256), lambda g: (g, 0, 0)),
            pl.BlockSpec((None, DEC_BATCH, 256), lambda g: (g, 0, 0)),
        ],
        out_shape=[
            jax.ShapeDtypeStruct((S5_GROUPS, rows, 256), F32),
            jax.ShapeDtypeStruct((S5_GROUPS, BATCH, 256), F32),
            jax.ShapeDtypeStruct((S5_GROUPS, DEC_BATCH, 256), F32),
        ],
        scratch_shapes=[pltpu.VMEM((rows, 256), F32)] * 3,
        name="s5_mix",
    )(ug, toep, eb, ca, a16, dtile, h0c, h0l)


def _s5_to_groups(u):
    n = S5_CHUNK
    uc = u[:NTOK_C].reshape(BATCH, SEQ // n, n, S5_GROUPS, S5_GROUP)
    ul = u[NTOK_C:].reshape(DEC_BATCH, DEC_SEQ // n, n, S5_GROUPS, S5_GROUP)
    uc = uc.transpose(3, 1, 0, 2, 4).reshape(S5_GROUPS, S5_ROWS_C, n * S5_GROUP)
    ul = ul.transpose(3, 1, 0, 2, 4).reshape(S5_GROUPS, S5_ROWS_L, n * S5_GROUP)
    return jnp.concatenate([uc, ul], axis=1)


def _s5_from_groups(y):
    n = S5_CHUNK
    yc = y[:, :S5_ROWS_C].reshape(S5_GROUPS, SEQ // n, BATCH, n, S5_GROUP)
    yl = y[:, S5_ROWS_C:].reshape(S5_GROUPS, DEC_SEQ // n, DEC_BATCH, n, S5_GROUP)
    yc = yc.transpose(2, 1, 3, 0, 4).reshape(NTOK_C, S5_WIDTH)
    yl = yl.transpose(2, 1, 3, 0, 4).reshape(NTOK_L, S5_WIDTH)
    return jnp.concatenate([yc, yl], axis=0)


@functools.lru_cache(maxsize=None)
def _gla_consts():
    n = GLA_BLK
    nl = GLA_LEVELS
    r = np.arange(n)
    seg = np.zeros((nl + 2, n, n), np.float32)
    up = np.zeros((n, 128), np.int32)
    for l in range(nl):
        for row in range(n):
            if (row >> l) & 1:
                seg[l, row, (row >> l) << l:row + 1] = 1.0
            else:
                seg[l, row, row + 1:(row | ((1 << l) - 1)) + 1] = 1.0
        up[:, l] = (r >> l) & 1
    for row in range(n):
        seg[nl, row, :row + 1] = 1.0
        seg[nl + 1, row, row + 1:] = 1.0
    i = r[:, None]
    j = r[None, :]
    x = np.maximum(i ^ j, 1)
    lev = np.where(j < i, np.floor(np.log2(x)).astype(np.int32), np.where(i == j, nl, -1)).astype(np.int32)
    seg2 = np.stack([seg, seg[:, ::-1, ::-1]]).reshape(2, (nl + 2) * n, n)
    up2 = np.stack([up, up[::-1]])
    lev2 = np.stack([lev, lev[::-1, ::-1]])
    return seg2, up2, lev2


@functools.lru_cache(maxsize=None)
def _gla_tables():
    rowblk, seq, first, last = [], [], [], []
    for d in range(2):
        rb, sq, fi, la = [], [], [], []
        for s in range(BATCH + DEC_BATCH):
            nblk = 1 if s < BATCH else DEC_SEQ // GLA_BLK
            base = s if s < BATCH else NTOK_C // GLA_BLK + (s - BATCH) * nblk
            order = range(nblk) if d == 0 else range(nblk - 1, -1, -1)
            for pos, b in enumerate(order):
                rb.append(base + b)
                sq.append(s)
                fi.append(int(pos == 0))
                la.append(int(pos == nblk - 1))
        rowblk.append(rb); seq.append(sq); first.append(fi); last.append(la)
    as_np = lambda a: np.asarray(a, np.int32)
    return as_np(rowblk), as_np(seq), as_np(first), as_np(last)


def _gla_kernel(rowblk_ref, seq_ref, first_ref, last_ref,
                q_ref, k_ref, v_ref, lr_ref, wgk_ref, bgk_ref, seg_ref, up_ref, lev_ref, s0_ref,
                o_ref, fin_ref, x_scr, z_scr, st_scr):
    d = pl.program_id(0)
    n = pl.program_id(1)
    nl = GLA_LEVELS
    blk = GLA_BLK

    @pl.when(first_ref[d, n] == 1)
    def _():
        st_scr[...] = jnp.zeros_like(st_scr)
        for h in range(GLA_HEADS):
            st_scr[h * GLA_DK:(h + 1) * GLA_DK, h * GLA_DV:(h + 1) * GLA_DV] = s0_ref[h]

    q = q_ref[...] * (GLA_DK ** -0.5)
    k = k_ref[...]
    vb = v_ref[...].astype(BF16)
    x = _dot(lr_ref[...].astype(BF16), wgk_ref[...]) + bgk_ref[...]
    gk = (jnp.minimum(x, 0.0) - jnp.log1p(jnp.exp(-jnp.abs(x)))) * (1.0 / GLA_NORMALIZER)
    g_hi = gk.astype(BF16)
    g_lo = (gk - g_hi.astype(F32)).astype(BF16)
    x_scr[...] = _dot(seg_ref[...], g_hi) + _dot(seg_ref[...], g_lo)
    ones = jnp.ones((blk, 128), BF16)
    tot = _dot_tn(g_hi, ones) + _dot_tn(g_lo, ones)

    for l in range(nl):
        e = jnp.exp(x_scr[l * blk:(l + 1) * blk, :])
        z_scr[l] = (jnp.where(up_ref[:, l:l + 1] != 0, q, k) * e).astype(BF16)
    z_scr[nl] = q.astype(BF16)
    kb = k.astype(BF16)

    lev = lev_ref[...]
    lane = lax.broadcasted_iota(jnp.int32, (blk, GLA_QK), 1)
    for h in range(GLA_HEADS):
        in_head = (lane >= h * GLA_DK) & (lane < (h + 1) * GLA_DK)
        att = jnp.zeros((blk, blk), F32)
        for l in range(nl + 1):
            lhs = z_scr[l]
            rhs = jnp.where(in_head, kb if l == nl else lhs, jnp.zeros_like(lhs))
            att = jnp.where(lev == l, _dot_nt(lhs, rhs), att)
        o_ref[:, h * GLA_DV:(h + 1) * GLA_DV] = _dot(att.astype(BF16), vb[:, h * GLA_DV:(h + 1) * GLA_DV])

    st = st_scr[...]
    q_in = (q * jnp.exp(x_scr[nl * blk:(nl + 1) * blk, :])).astype(BF16)
    o_ref[...] += _dot(q_in, st.astype(BF16))
    k_out = (k * jnp.exp(x_scr[(nl + 1) * blk:(nl + 2) * blk, :])).astype(BF16)
    kv = _dot_tn(k_out, vb)
    row = lax.broadcasted_iota(jnp.int32, (GLA_QK, GLA_V), 0)
    col = lax.broadcasted_iota(jnp.int32, (GLA_QK, GLA_V), 1)
    same_head = (row >> 6) == (col >> 7)
    decay = jnp.exp(tot)
    decay = jnp.concatenate([decay] * GLA_HEADS, axis=1)
    st_new = decay * st + jnp.where(same_head, kv, 0.0)
    st_scr[...] = st_new

    @pl.when(last_ref[d, n] == 1)
    def _():
        for h in range(GLA_HEADS):
            fin_ref[h] = st_scr[h * GLA_DK:(h + 1) * GLA_DK, h * GLA_DV:(h + 1) * GLA_DV]


def _gla_mix(bslab, lr, wgk, bgk, s0):
    seg, up, lev = _gla_consts()
    rowblk, seq, first, last = _gla_tables()
    nsteps = rowblk.shape[1]
    nseq = BATCH + DEC_BATCH
    nl = GLA_LEVELS
    grid_spec = pltpu.PrefetchScalarGridSpec(
        num_scalar_prefetch=4,
        grid=(2, nsteps),
        in_specs=[
            pl.BlockSpec((GLA_BLK, GLA_QK), lambda d, n, rb, sq, fi, la: (rb[d, n], 0)),
            pl.BlockSpec((GLA_BLK, GLA_QK), lambda d, n, rb, sq, fi, la: (rb[d, n], 1)),
            pl.BlockSpec((GLA_BLK, GLA_V), lambda d, n, rb, sq, fi, la: (rb[d, n], 1)),
            pl.BlockSpec((GLA_BLK, 128), lambda d, n, rb, sq, fi, la: (rb[d, n], 0)),
            pl.BlockSpec((None, 128, GLA_QK), lambda d, n, rb, sq, fi, la: (d, 0, 0)),
            pl.BlockSpec((None, 1, GLA_QK), lambda d, n, rb, sq, fi, la: (d, 0, 0)),
            pl.BlockSpec((None, (nl + 2) * GLA_BLK, GLA_BLK), lambda d, n, rb, sq, fi, la: (d, 0, 0)),
            pl.BlockSpec((None, GLA_BLK, 128), lambda d, n, rb, sq, fi, la: (d, 0, 0)),
            pl.BlockSpec((None, GLA_BLK, GLA_BLK), lambda d, n, rb, sq, fi, la: (d, 0, 0)),
            pl.BlockSpec((None, None, GLA_HEADS, GLA_DK, GLA_DV),
                         lambda d, n, rb, sq, fi, la: (sq[d, n], d, 0, 0, 0)),
        ],
        out_specs=[
            pl.BlockSpec((None, GLA_BLK, GLA_V), lambda d, n, rb, sq, fi, la: (d, rb[d, n], 0)),
            pl.BlockSpec((None, None, GLA_HEADS, GLA_DK, GLA_DV),
                         lambda d, n, rb, sq, fi, la: (sq[d, n], d, 0, 0, 0)),
        ],
        scratch_shapes=[
            pltpu.VMEM(((nl + 2) * GLA_BLK, GLA_QK), F32),
            pltpu.VMEM((nl + 1, GLA_BLK, GLA_QK), BF16),
            pltpu.VMEM((GLA_QK, GLA_V), F32),
        ],
    )
    return pl.pallas_call(
        _gla_kernel,
        grid_spec=grid_spec,
        out_shape=[
            jax.ShapeDtypeStruct((2, NTOK, GLA_V), F32),
            jax.ShapeDtypeStruct((nseq, 2, GLA_HEADS, GLA_DK, GLA_DV), F32),
        ],
        compiler_params=pltpu.CompilerParams(vmem_limit_bytes=VMEM_LIMIT),
        name="gla_mix",
    )(jnp.asarray(rowblk), jnp.asarray(seq), jnp.asarray(first), jnp.asarray(last),
      bslab, bslab, bslab, lr, wgk, bgk,
      jnp.asarray(seg, BF16), jnp.asarray(up), jnp.asarray(lev), s0)


def _softmax_head(s_list, v_list, sink):
    m = sink
    for s in s_list:
        m = jnp.maximum(m, jnp.max(s, axis=-1, keepdims=True))
    den = jnp.exp(sink - m)
    acc = None
    for s, v in zip(s_list, v_list):
        p = jnp.exp(s - m)
        den = den + jnp.sum(p, axis=-1, keepdims=True)
        pv = _dot(p.astype(BF16), v)
        acc = pv if acc is None else acc + pv
    return acc / den


def _attn_ctx_kernel(sink_ref, q_ref, k_ref, v_ref, o_ref):
    k = k_ref[...]
    v = v_ref[...]
    ks = (k.astype(BF16), pltpu.roll(k, 64, 1).astype(BF16))
    vs = (v.astype(BF16), pltpu.roll(v, 64, 1).astype(BF16))
    lo = lax.broadcasted_iota(jnp.int32, (SEQ, 128), 1) < HEAD_DIM
    for t in range(ATT_HEADS // 2):
        qt = q_ref[:, t * 128:(t + 1) * 128] * (HEAD_DIM ** -0.5)
        kvh = t // 2
        outs = []
        for p in range(2):
            sel = lo if p == 0 else jnp.logical_not(lo)
            qm = jnp.where(sel, qt, 0.0).astype(BF16)
            which = 0 if p == kvh else 1
            s = _dot_nt(qm, ks[which])
            outs.append(_softmax_head([s], [vs[which]], sink_ref[2 * t + p]))
        o_ref[:, t * 128:(t + 1) * 128] = jnp.where(lo, outs[0], outs[1])


def _attn_ctx(sink, cslab):
    return pl.pallas_call(
        _attn_ctx_kernel,
        grid=(BATCH,),
        in_specs=[
            pl.BlockSpec(memory_space=pltpu.SMEM),
            pl.BlockSpec((SEQ, ATT_Q), lambda b: (b, 0)),
            pl.BlockSpec((SEQ, ATT_KV), lambda b: (b, 4)),
            pl.BlockSpec((SEQ, ATT_KV), lambda b: (b, 5)),
        ],
        out_specs=pl.BlockSpec((SEQ, ATT_Q), lambda b: (b, 0)),
        out_shape=jax.ShapeDtypeStruct((NTOK, ATT_Q), F32),
        name="attn_ctx",
    )(sink, cslab, cslab, cslab)


def _attn_lat_kernel(sink_ref, q_ref, kp_ref, kc_ref, kn_ref, vp_ref, vc_ref, vn_ref,
                     ck_ref, cv_ref, cos_ref, sin_ref, prev_ref, o_ref):
    del prev_ref
    j = pl.program_id(1)
    nb = DEC_SEQ // ATT_BLOCK
    lane = lax.broadcasted_iota(jnp.int32, (ATT_BLOCK, 128), 1)
    lo = lane < HEAD_DIM
    first16 = (lane & 31) < 16

    def rope(x, blk_idx):
        r0 = pl.multiple_of(blk_idx * ATT_BLOCK, ATT_BLOCK)
        c = cos_ref[pl.ds(r0, ATT_BLOCK), :]
        s = sin_ref[pl.ds(r0, ATT_BLOCK), :]
        xs = jnp.where(first16, pltpu.roll(x, 112, 1), pltpu.roll(x, 16, 1))
        return x * c + xs * s

    kw = jnp.concatenate([rope(kp_ref[...], jnp.maximum(j - 1, 0)), rope(kc_ref[...], j),
                          rope(kn_ref[...], jnp.minimum(j + 1, nb - 1))], axis=0)
    vw = jnp.concatenate([vp_ref[...], vc_ref[...], vn_ref[...]], axis=0)
    ck = ck_ref[...]
    cv = cv_ref[...]
    kws = (kw.astype(BF16), pltpu.roll(kw, 64, 1).astype(BF16))
    vws = (vw.astype(BF16), pltpu.roll(vw, 64, 1).astype(BF16))
    cks = (ck.astype(BF16), pltpu.roll(ck, 64, 1).astype(BF16))
    cvs = (cv.astype(BF16), pltpu.roll(cv, 64, 1).astype(BF16))
    qi = lax.broadcasted_iota(jnp.int32, (ATT_BLOCK, 3 * ATT_BLOCK), 0)
    kc = lax.broadcasted_iota(jnp.int32, (ATT_BLOCK, 3 * ATT_BLOCK), 1)
    valid = jnp.abs(kc - ATT_BLOCK - qi) <= WINDOW
    valid = valid & ((j > 0) | (kc >= ATT_BLOCK)) & ((j < nb - 1) | (kc < 2 * ATT_BLOCK))
    for t in range(ATT_HEADS // 2):
        qt = rope(q_ref[:, t * 128:(t + 1) * 128], j) * (HEAD_DIM ** -0.5)
        kvh = t // 2
        outs = []
        for p in range(2):
            sel = lo if p == 0 else jnp.logical_not(lo)
            qm = jnp.where(sel, qt, 0.0).astype(BF16)
            which = 0 if p == kvh else 1
            s_w = jnp.where(valid, _dot_nt(qm, kws[which]), -1e30)
            s_c = _dot_nt(qm, cks[which])
            outs.append(_softmax_head([s_w, s_c], [vws[which], cvs[which]], sink_ref[2 * t + p]))
        o_ref[:, t * 128:(t + 1) * 128] = jnp.where(lo, outs[0], outs[1])


def _attn_lat(sink, cslab, ck, cv, cos_t, sin_t, y_prev):
    nb = DEC_SEQ // ATT_BLOCK
    base = NTOK_C // ATT_BLOCK
    cur = lambda b, j: base + b * nb + j
    prv = lambda b, j: base + b * nb + jnp.maximum(j - 1, 0)
    nxt = lambda b, j: base + b * nb + jnp.minimum(j + 1, nb - 1)
    kv_spec = lambda row, col: pl.BlockSpec((ATT_BLOCK, ATT_KV), lambda b, j: (row(b, j), col))
    return pl.pallas_call(
        _attn_lat_kernel,
        grid=(DEC_BATCH, nb),
        in_specs=[
            pl.BlockSpec(memory_space=pltpu.SMEM),
            pl.BlockSpec((ATT_BLOCK, ATT_Q), lambda b, j: (cur(b, j), 0)),
            kv_spec(prv, 4), kv_spec(cur, 4), kv_spec(nxt, 4),
            kv_spec(prv, 5), kv_spec(cur, 5), kv_spec(nxt, 5),
            pl.BlockSpec((None, PAST_LEN, ATT_KV), lambda b, j: (b, 0, 0)),
            pl.BlockSpec((None, PAST_LEN, ATT_KV), lambda b, j: (b, 0, 0)),
            pl.BlockSpec((DEC_SEQ, 128), lambda b, j: (0, 0)),
            pl.BlockSpec((DEC_SEQ, 128), lambda b, j: (0, 0)),
            pl.BlockSpec(memory_space=pl.ANY),
        ],
        out_specs=pl.BlockSpec((ATT_BLOCK, ATT_Q), lambda b, j: (cur(b, j), 0)),
        out_shape=jax.ShapeDtypeStruct((NTOK, ATT_Q), F32),
        input_output_aliases={12: 0},
        name="attn_lat",
    )(sink, cslab, cslab, cslab, cslab, cslab, cslab, cslab, ck, cv, cos_t, sin_t, y_prev)


def _rope_tables():
    rows = DEC_SEQ // GRID_W
    row = np.repeat(np.arange(rows, dtype=np.float32), GRID_W)
    col = np.tile(np.arange(GRID_W, dtype=np.float32), rows)
    quarter = HEAD_DIM // 4
    inv = jnp.asarray(ROPE_BASE, F32) ** (-jnp.arange(quarter, dtype=F32) / quarter)
    lane = np.arange(128)
    use_row = (lane % HEAD_DIM) < HEAD_DIM // 2
    pos = jnp.where(use_row[None, :], jnp.asarray(row)[:, None], jnp.asarray(col)[:, None])
    ang = pos * inv[lane % quarter][None, :]
    sign = np.where((lane % 32) < 16, -1.0, 1.0).astype(np.float32)
    return jnp.cos(ang), jnp.sin(ang) * sign[None, :]


def _merge_kernel(x_ref, mod_ref, g_ref, ys5_ref, ogf_ref, ogb_ref, gb_ref, yc_ref, gate_ref,
                  gng_ref, wglu_ref, wbr_ref, wout_ref, o_ref):
    y = ys5_ref[...]
    y = 0.5 * y * (1.0 + jnp.tanh(math.sqrt(2.0 / math.pi) * (y + 0.044715 * (y * y * y))))
    ag = _dot(y.astype(BF16), wglu_ref[...])
    y_a = ag[:, :S5_WIDTH] * _sigmoid(ag[:, S5_WIDTH:])
    gng = gng_ref[...]
    gb = gb_ref[...]
    parts = []
    for h in range(GLA_HEADS):
        sl = slice(h * GLA_DV, (h + 1) * GLA_DV)
        o = ogf_ref[:, sl] + ogb_ref[:, sl]
        g = gb[:, sl]
        parts.append(_rms(o, gng) * (g * _sigmoid(g)))
    y_b = jnp.concatenate(parts, axis=1)
    merged = None
    for n, yn in enumerate((y_a, y_b, yc_ref[...])):
        proj = _dot(yn.astype(BF16), wbr_ref[n])
        term = _sigmoid(gate_ref[:, n * D_MODEL:(n + 1) * D_MODEL]) * proj
        merged = term if merged is None else merged + term
    mixed = _dot(merged.astype(BF16), wout_ref[...])
    g1 = mod_ref[:, 2 * D_MODEL:3 * D_MODEL]
    o_ref[...] = x_ref[...] + g1 * _rms(mixed, g_ref[...])


def _merge(x, mod, g, ys5, og, bslab, yc, gates, gng, wglu, wbr, wout):
    tok = lambda width, col=0: pl.BlockSpec((TM, width), lambda i: (i, col))
    full = lambda shape: pl.BlockSpec(shape, lambda i: (0,) * len(shape))
    return pl.pallas_call(
        _merge_kernel,
        grid=(NTOK // TM,),
        in_specs=[
            tok(D_MODEL),
            pl.BlockSpec((None, 1, 6 * D_MODEL), lambda i: (_mod_row(i), 0, 0)),
            full((1, D_MODEL)),
            tok(S5_WIDTH),
            pl.BlockSpec((None, TM, GLA_V), lambda i: (0, i, 0)),
            pl.BlockSpec((None, TM, GLA_V), lambda i: (1, i, 0)),
            tok(GLA_V, 2),
            tok(ATT_Q),
            tok(N_BRANCH * D_MODEL),
            full((1, GLA_DV)),
            full((S5_WIDTH, 2 * S5_WIDTH)),
            full((N_BRANCH, BRANCH_W, D_MODEL)),
            full((D_MODEL, D_MODEL)),
        ],
        out_specs=tok(D_MODEL),
        out_shape=jax.ShapeDtypeStruct((NTOK, D_MODEL), F32),
        compiler_params=pltpu.CompilerParams(vmem_limit_bytes=VMEM_LIMIT),
        name="merge",
    )(x, mod, g, ys5, og, og, bslab, yc, gates, gng, wglu, wbr, wout)


FFN_SPLIT = 2


def _ffn_kernel(x_ref, mod_ref, gin_ref, gout_ref, w1_ref, w2_ref, o_ref):
    x = x_ref[...]
    sh = mod_ref[:, 3 * D_MODEL:4 * D_MODEL]
    sc = mod_ref[:, 4 * D_MODEL:5 * D_MODEL]
    g2 = mod_ref[:, 5 * D_MODEL:6 * D_MODEL]
    h = (_rms(x, gin_ref[...]) * (1.0 + sc) + sh).astype(BF16)
    ck = FFN_HIDDEN // FFN_SPLIT
    acc = None
    for c in range(FFN_SPLIT):
        a = _dot(h, w1_ref[:, c * ck:(c + 1) * ck])
        b = _dot(h, w1_ref[:, FFN_HIDDEN + c * ck:FFN_HIDDEN + (c + 1) * ck])
        act = (a * _sigmoid(a) * b).astype(BF16)
        part = _dot(act, w2_ref[c * ck:(c + 1) * ck, :])
        acc = part if acc is None else acc + part
    o_ref[...] = x + g2 * _rms(acc, gout_ref[...])


def _ffn(x, mod, gin, gout, w1, w2):
    full = lambda shape: pl.BlockSpec(shape, lambda i: (0,) * len(shape))
    return pl.pallas_call(
        _ffn_kernel,
        grid=(NTOK // TM,),
        in_specs=[
            pl.BlockSpec((TM, D_MODEL), lambda i: (i, 0)),
            pl.BlockSpec((None, 1, 6 * D_MODEL), lambda i: (_mod_row(i), 0, 0)),
            full((1, D_MODEL)),
            full((1, D_MODEL)),
            full((D_MODEL, 2 * FFN_HIDDEN)),
            full((FFN_HIDDEN, D_MODEL)),
        ],
        out_specs=pl.BlockSpec((TM, D_MODEL), lambda i: (i, 0)),
        out_shape=jax.ShapeDtypeStruct((NTOK, D_MODEL), F32),
        compiler_params=pltpu.CompilerParams(vmem_limit_bytes=VMEM_LIMIT),
        name="ffn",
    )(x, mod, gin, gout, w1, w2)


def kernel(x_prompt, x_sample, cache_k, cache_v, state_s5, state_gla, c, c_ctx, w_mod, b_mod, norm_g, w_in,
           s5_lam_re, s5_lam_im, s5_log_step, s5_b_re, s5_b_im, s5_c_re, s5_c_im, s5_d, w_glu, gla_w_gk,
           gla_b_gk, gla_norm_g, att_sink, w_branch, w_out, w_ffn_in, w_ffn_out):
    cond = jnp.concatenate([c_ctx[None, :], c, jnp.zeros((N_MOD_ROWS - 1 - DEC_BATCH, D_MODEL), F32)], axis=0)
    mod_all = _modulation(cond, w_mod, b_mod).reshape(DEPTH, N_MOD_ROWS, 1, 6 * D_MODEL)
    cos_t, sin_t = _rope_tables()
    x = jnp.concatenate([x_prompt.reshape(NTOK_C, D_MODEL), x_sample.reshape(NTOK_L, D_MODEL)], axis=0)
    new_k, new_v, new_s5, new_gla = [], [], [], []
    for i in range(DEPTH):
        mod = mod_all[i]
        wi = w_in[i]
        w_in_p = jnp.concatenate(
            [wi[:, :2048], wi[:, 2080:], wi[:, 2048:2080], jnp.zeros((D_MODEL, W_IN_COLS - 5920), F32)],
            axis=1).astype(BF16)
        u, bslab, cslab, gates, lr = _inproj(x, mod, norm_g[i, 0][None, :], w_in_p)

        toep, eb, ca, a16, dtile = _s5_prep(s5_lam_re[i], s5_lam_im[i], s5_log_step[i], s5_b_re[i], s5_b_im[i],
                                            s5_c_re[i], s5_c_im[i], s5_d[i])
        h0l = state_s5[:, i].astype(F32).transpose(2, 0, 4, 1, 3).reshape(S5_GROUPS, DEC_BATCH, 256)
        h0c = jnp.zeros((S5_GROUPS, BATCH, 256), F32)
        yg, finc, _ = _s5_mix(_s5_to_groups(u), toep, eb, ca, a16, dtile, h0c, h0l)
        ys5 = _s5_from_groups(yg)
        new_s5.append(finc.reshape(S5_GROUPS, BATCH, 2, 2, S5_STATE).transpose(1, 3, 0, 4, 2))

        wgk = jnp.zeros((2, 128, GLA_QK), F32)
        wgk = wgk.at[0, 0:GLA_RANK].set(gla_w_gk[i, 0]).at[1, GLA_RANK:2 * GLA_RANK].set(gla_w_gk[i, 1])
        s0 = jnp.concatenate([jnp.zeros((BATCH, 2, GLA_HEADS, GLA_DK, GLA_DV), F32),
                              state_gla[:, i].astype(F32)], axis=0)
        og, gla_fin = _gla_mix(bslab, lr, wgk.astype(BF16), gla_b_gk[i][:, None, :].astype(F32), s0)
        new_gla.append(gla_fin[:BATCH])

        sink = att_sink[i].astype(F32)
        yc = _attn_ctx(sink, cslab)
        yc = _attn_lat(sink, cslab, cache_k[:, i].reshape(DEC_BATCH, PAST_LEN, ATT_KV).astype(F32),
                       cache_v[:, i].reshape(DEC_BATCH, PAST_LEN, ATT_KV).astype(F32), cos_t, sin_t, yc)
        new_k.append(cslab[:NTOK_C, ATT_Q:ATT_Q + ATT_KV].reshape(BATCH, SEQ, ATT_KV_HEADS, HEAD_DIM))
        new_v.append(cslab[:NTOK_C, ATT_Q + ATT_KV:].reshape(BATCH, SEQ, ATT_KV_HEADS, HEAD_DIM))

        x = _merge(x, mod, norm_g[i, 1][None, :], ys5, og, bslab, yc, gates, gla_norm_g[i][None, :],
                   w_glu[i].astype(BF16), w_branch[i].astype(BF16), w_out[i].astype(BF16))
        x = _ffn(x, mod, norm_g[i, 2][None, :], norm_g[i, 3][None, :],
                 w_ffn_in[i].astype(BF16), w_ffn_out[i].astype(BF16))

    return (x[:NTOK_C].reshape(BATCH, SEQ, D_MODEL), x[NTOK_C:].reshape(DEC_BATCH, DEC_SEQ, D_MODEL),
            jnp.stack(new_k, axis=1), jnp.stack(new_v, axis=1),
            jnp.stack(new_s5, axis=1), jnp.stack(new_gla, axis=1))
```

```python
import functools
import math

import numpy as np
import jax
import jax.numpy as jnp
from jax import lax
from jax.experimental import pallas as pl
from jax.experimental.pallas import tpu as pltpu

F32 = jnp.float32
BF16 = jnp.bfloat16

D_MODEL = 1024
BATCH = 16
SEQ = 256
DEPTH = 2
DEC_BATCH = 8
DEC_SEQ = 1024
PAST_LEN = 256
GRID_W = 64
ROPE_BASE = 10000.0
S5_WIDTH = 512
S5_GROUP = 16
S5_GROUPS = 32
S5_STATE = 64
GLA_HEADS = 4
GLA_DK = 64
GLA_DV = 128
GLA_QK = 256
GLA_V = 512
GLA_RANK = 16
GLA_NORMALIZER = 16.0
ATT_HEADS = 8
ATT_KV_HEADS = 2
HEAD_DIM = 64
ATT_Q = 512
ATT_KV = 128
WINDOW = 128
ATT_BLOCK = 128
N_BRANCH = 3
BRANCH_W = 512
FFN_HIDDEN = 2816
RMS_EPS = 1e-6

NTOK_C = BATCH * SEQ
NTOK_L = DEC_BATCH * DEC_SEQ
NTOK = NTOK_C + NTOK_L
TM = 256
N_MOD_ROWS = 16

W_IN_COLS = 6016
S5_CHUNK = 16
S5_SLABS = S5_WIDTH // 128
S5_SLAB_W = S5_CHUNK * 128
S5_ROWS_C = NTOK_C // S5_CHUNK
S5_ROWS = NTOK // S5_CHUNK
S5_ROW_TILE = 256
GLA_BLK = 256
GLA_LEVELS = 8
VMEM_LIMIT = 56 * 1024 * 1024


def _dot(a, b):
    return jnp.dot(a, b, preferred_element_type=F32)


def _dot_nt(a, b):
    return lax.dot_general(a, b, (((1,), (1,)), ((), ())), preferred_element_type=F32)


def _dot_tn(a, b):
    return lax.dot_general(a, b, (((0,), (0,)), ((), ())), preferred_element_type=F32)


def _rms(x, g):
    return x * lax.rsqrt(jnp.mean(x * x, axis=-1, keepdims=True) + RMS_EPS) * g


def _sigmoid(x):
    return 1.0 / (1.0 + jnp.exp(-x))


def _mod_row(i):
    nct = NTOK_C // TM
    return jnp.where(i < nct, 0, 1 + (i - nct) // (DEC_SEQ // TM))


def _mod_kernel(c_ref, w_ref, b_ref, o_ref):
    c = c_ref[...]
    s = (c * _sigmoid(c)).astype(BF16)
    o_ref[...] = _dot(s, w_ref[...].astype(BF16)) + b_ref[...]


def _modulation(cond, w_mod, b_mod):
    tn = 1024
    return pl.pallas_call(
        _mod_kernel,
        grid=(DEPTH, 6 * D_MODEL // tn),
        in_specs=[
            pl.BlockSpec((N_MOD_ROWS, D_MODEL), lambda l, n: (0, 0)),
            pl.BlockSpec((None, D_MODEL, tn), lambda l, n: (l, 0, n)),
            pl.BlockSpec((None, 1, tn), lambda l, n: (l, 0, n)),
        ],
        out_specs=pl.BlockSpec((None, N_MOD_ROWS, tn), lambda l, n: (l, 0, n)),
        out_shape=jax.ShapeDtypeStruct((DEPTH, N_MOD_ROWS, 6 * D_MODEL), F32),
        name="modulation",
    )(cond, w_mod, b_mod.reshape(DEPTH, 1, 6 * D_MODEL))


_IN_SLABS = ((0, 512), (512, 1536), (2048, 768), (2816, 3072), (5888, 128))


def _inproj_kernel(x_ref, mod_ref, g_ref, w_ref, u_ref, b_ref, c_ref, gate_ref, lr_ref):
    x = x_ref[...]
    mod = mod_ref[...]
    h = _rms(x, g_ref[...]) * (1.0 + mod[:, D_MODEL:2 * D_MODEL]) + mod[:, 0:D_MODEL]
    h = h.astype(BF16)
    for j in range(S5_SLABS):
        u_ref[j] = _dot(h, w_ref[:, j * 128:(j + 1) * 128])
    for (off, width), o_ref in zip(_IN_SLABS[1:], (b_ref, c_ref, gate_ref, lr_ref)):
        o_ref[...] = _dot(h, w_ref[:, off:off + width])


def _inproj(x, mod, g, w):
    return pl.pallas_call(
        _inproj_kernel,
        grid=(NTOK // TM,),
        in_specs=[
            pl.BlockSpec((TM, D_MODEL), lambda i: (i, 0)),
            pl.BlockSpec((None, 1, 6 * D_MODEL), lambda i: (_mod_row(i), 0, 0)),
            pl.BlockSpec((1, D_MODEL), lambda i: (0, 0)),
            pl.BlockSpec((D_MODEL, W_IN_COLS), lambda i: (0, 0)),
        ],
        out_specs=[pl.BlockSpec((S5_SLABS, TM, 128), lambda i: (0, i, 0))]
        + [pl.BlockSpec((TM, width), lambda i: (i, 0)) for _, width in _IN_SLABS[1:]],
        out_shape=[jax.ShapeDtypeStruct((S5_SLABS, NTOK, 128), F32)]
        + [jax.ShapeDtypeStruct((NTOK, width), F32) for _, width in _IN_SLABS[1:]],
        compiler_params=pltpu.CompilerParams(vmem_limit_bytes=VMEM_LIMIT),
        name="inproj",
    )(x, mod, g, w)


@functools.lru_cache(maxsize=None)
def _s5_expanders():
    seg = 8
    spread = np.zeros((seg, 256, S5_SLAB_W), np.float32)
    place = np.zeros((seg, 256, S5_SLAB_W), np.float32)
    col = np.arange(256)
    for gl in range(seg):
        spread[gl, col, (col // S5_GROUP) * 128 + gl * S5_GROUP + col % S5_GROUP] = 1.0
        place[gl, col, gl * 256 + col] = 1.0
    return spread, place


def _s5_prep_kernel(par_ref, bre_ref, bim_ref, cre_ref, cim_ref, spread_ref, place_ref,
                    wt_ref, web_ref, wca_ref, a16_ref):
    n = S5_CHUNK
    lam_re = par_ref[0:1, :]
    lam_im = par_ref[1:2, :]
    dt = jnp.exp(par_ref[2:3, :])
    lr = lam_re * dt
    li = lam_im * dt
    mag = jnp.exp(lr)
    ar = mag * jnp.cos(li)
    ai = mag * jnp.sin(li)
    nr = ar - 1.0
    den = lam_re * lam_re + lam_im * lam_im
    fr = (nr * lam_re + ai * lam_im) / den
    fi = (ai * lam_re - nr * lam_im) / den
    b_re = bre_ref[...]
    b_im = bim_ref[...]
    br = fr * b_re - fi * b_im
    bi = fr * b_im + fi * b_re
    c_re = cre_ref[...]
    c_im = cim_ref[...]

    def lo_half(shape):
        return lax.broadcasted_iota(jnp.int32, shape, 1) < S5_STATE

    def tile_rows(a):
        return jnp.concatenate([a] * n, axis=0)

    def powers(nmat, lr_, li_):
        m = jnp.exp(nmat * lr_)
        return m * jnp.cos(nmat * li_), m * jnp.sin(nmat * li_)

    blk = lax.broadcasted_iota(jnp.int32, (n * S5_GROUP, 128), 0) >> 4
    fwd = lo_half((n * S5_GROUP, 128))
    brt, bit, crt, cit = tile_rows(br), tile_rows(bi), tile_rows(c_re), tile_rows(c_im)

    per, pei = powers(jnp.where(fwd, n - 1 - blk, blk).astype(F32), lr, li)
    eb = jnp.concatenate([brt * per - bit * pei, brt * pei + bit * per], axis=1)
    pcr, pci = powers(jnp.where(fwd, blk + 1, n - blk).astype(F32), lr, li)
    ca = jnp.concatenate([(crt * pcr - cit * pci).T, (-(crt * pci + cit * pcr)).T], axis=0)

    def one_dir(x, d):
        sw = pltpu.roll(x, S5_STATE, 1)
        lo = lo_half(x.shape)
        return jnp.where(lo, x, sw) if d == 0 else jnp.where(lo, sw, x)

    klag = []
    for d in range(2):
        lhs = jnp.where(lo_half(br.shape), one_dir(br, d), -one_dir(bi, d))
        crd, cid = tile_rows(one_dir(c_re, d)), tile_rows(one_dir(c_im, d))
        lag = blk if d == 0 else n - 1 - blk
        lr_d = one_dir(jnp.broadcast_to(lr, (8, 128)), d)[0:1]
        li_d = one_dir(jnp.broadcast_to(li, (8, 128)), d)[0:1]
        pr, pi = powers(lag.astype(F32), lr_d, li_d)
        rhs_t = jnp.where(fwd, crd * pr - cid * pi, crd * pi + cid * pr)
        klag.append(lax.dot_general(lhs, rhs_t, (((1,), (1,)), ((), ())),
                                    precision=lax.Precision.HIGHEST, preferred_element_type=F32))
    lane = lax.broadcasted_iota(jnp.int32, (S5_GROUP, n * S5_GROUP), 1)
    rows = []
    for s in range(n):
        f = klag[0] if s == 0 else jnp.where(lane >= S5_GROUP * s, pltpu.roll(klag[0], S5_GROUP * s, 1), 0.0)
        sh = (n * S5_GROUP - S5_GROUP * (n - 1 - s)) % (n * S5_GROUP)
        b = klag[1] if sh == 0 else pltpu.roll(klag[1], sh, 1)
        rows.append(f + jnp.where(lane < S5_GROUP * (s + 1), b, 0.0))
    toep = jnp.concatenate(rows, axis=0)

    spread = spread_ref[...]
    wt_ref[...] = _dot(toep.astype(BF16), spread).astype(BF16).reshape(n, S5_GROUP, S5_SLAB_W)
    web_ref[...] = _dot(eb.astype(BF16), place_ref[...]).astype(BF16).reshape(n, S5_GROUP, S5_SLAB_W)
    wca_ref[...] = _dot(ca.astype(BF16), spread).astype(BF16)
    m16 = jnp.exp(n * lr)
    a16_ref[0:1, :] = m16 * jnp.cos(n * li)
    a16_ref[1:2, :] = m16 * jnp.sin(n * li)


def _s5_prep(lam_re, lam_im, log_step, b_re, b_im, c_re, c_im, d_skip):
    seg = 8
    par = jnp.stack([lam_re, lam_im, log_step]).astype(F32).transpose(2, 0, 1, 3).reshape(S5_GROUPS, 3, 128)
    par = jnp.concatenate([par, jnp.zeros((S5_GROUPS, 5, 128), F32)], axis=1)
    b_t = lambda b: b.astype(F32).transpose(1, 3, 0, 2).reshape(S5_GROUPS, S5_GROUP, 128)
    c_t = lambda c: c.astype(F32).transpose(1, 2, 0, 3).reshape(S5_GROUPS, S5_GROUP, 128)
    spread, place = _s5_expanders()
    vec = pl.BlockSpec((None, S5_GROUP, 128), lambda g: (g, 0, 0))
    exp_spec = pl.BlockSpec((None, 256, S5_SLAB_W), lambda g: (g % seg, 0, 0))
    rows_spec = pl.BlockSpec((None, S5_CHUNK, None, S5_GROUP, S5_SLAB_W), lambda g: (g // seg, 0, g % seg, 0, 0))
    wt, web, wca, a16 = pl.pallas_call(
        _s5_prep_kernel,
        grid=(S5_GROUPS,),
        in_specs=[pl.BlockSpec((None, 8, 128), lambda g: (g, 0, 0)), vec, vec, vec, vec, exp_spec, exp_spec],
        out_specs=[
            rows_spec, rows_spec,
            pl.BlockSpec((None, None, 256, S5_SLAB_W), lambda g: (g // seg, g % seg, 0, 0)),
            pl.BlockSpec((None, 2, 128), lambda g: (g, 0, 0)),
        ],
        out_shape=[
            jax.ShapeDtypeStruct((S5_SLABS, S5_CHUNK, seg, S5_GROUP, S5_SLAB_W), BF16),
            jax.ShapeDtypeStruct((S5_SLABS, S5_CHUNK, seg, S5_GROUP, S5_SLAB_W), BF16),
            jax.ShapeDtypeStruct((S5_SLABS, seg, 256, S5_SLAB_W), BF16),
            jax.ShapeDtypeStruct((S5_GROUPS, 2, 128), F32),
        ],
        name="s5_prep",
    )(par, b_t(b_re), b_t(b_im), c_t(c_re), c_t(c_im), jnp.asarray(spread, BF16), jnp.asarray(place, BF16))
    mat = (S5_SLABS, S5_SLAB_W, S5_SLAB_W)
    dj = jnp.tile(d_skip.astype(F32).reshape(S5_SLABS, 1, 128), (1, 1, S5_CHUNK))
    return wt.reshape(mat), web.reshape(mat), wca.reshape(mat), a16.reshape(1, S5_SLABS * S5_SLAB_W), dj


S5_STATE_COLS = S5_SLABS * S5_SLAB_W // 128
S5_SLAB_COLS = S5_SLAB_W // 128


def _s5_state_kernel(u_ref, w_ref, o_ref):
    s = _dot(u_ref[...].astype(BF16), w_ref[...])
    for k in range(S5_SLAB_COLS):
        o_ref[k] = s[:, k * 128:(k + 1) * 128]


def _s5_state(uj, web):
    return pl.pallas_call(
        _s5_state_kernel,
        grid=(S5_SLABS, S5_ROWS // S5_ROW_TILE),
        in_specs=[
            pl.BlockSpec((None, S5_ROW_TILE, S5_SLAB_W), lambda j, p: (j, p, 0)),
            pl.BlockSpec((None, S5_SLAB_W, S5_SLAB_W), lambda j, p: (j, 0, 0)),
        ],
        out_specs=pl.BlockSpec((S5_SLAB_COLS, S5_ROW_TILE, 128), lambda j, p: (j, p, 0)),
        out_shape=jax.ShapeDtypeStruct((S5_STATE_COLS, S5_ROWS, 128), F32),
        compiler_params=pltpu.CompilerParams(vmem_limit_bytes=VMEM_LIMIT),
        name="s5_state",
    )(uj, web)


S5_SCAN_COLS = 8


def _s5_scan_kernel(s_ref, a_ref, h0_ref, hin_ref, fin_ref, hf, hb):
    ncol = S5_SCAN_COLS

    def scan(row0, nc, nb, h0):
        is_f = lax.broadcasted_iota(jnp.int32, (nb, 128), 1) < S5_STATE

        def body(c, hs):
            rf = row0 + c
            rb = row0 + (nc - 1 - c)
            new = []
            for m in range(ncol // 2):
                h_re, h_im = hs[2 * m], hs[2 * m + 1]
                a_re = a_ref[:, (2 * m) * 128:(2 * m + 1) * 128]
                a_im = a_ref[:, (2 * m + 1) * 128:(2 * m + 2) * 128]
                loc = []
                for k, h in ((2 * m, h_re), (2 * m + 1, h_im)):
                    hf[k, pl.ds(rf, nb, stride=nc), :] = h
                    hb[k, pl.ds(rb, nb, stride=nc), :] = h
                    loc.append(jnp.where(is_f, s_ref[k, pl.ds(rf, nb, stride=nc), :],
                                         s_ref[k, pl.ds(rb, nb, stride=nc), :]))
                new.append(a_re * h_re - a_im * h_im + loc[0])
                new.append(a_re * h_im + a_im * h_re + loc[1])
            return tuple(new)

        return lax.fori_loop(0, nc, body, h0)

    fin = scan(0, SEQ // S5_CHUNK, BATCH, tuple(jnp.zeros((BATCH, 128), F32) for _ in range(ncol)))
    for k in range(ncol):
        fin_ref[:, k * 128:(k + 1) * 128] = fin[k]
    scan(S5_ROWS_C, DEC_SEQ // S5_CHUNK, DEC_BATCH,
         tuple(h0_ref[:, k * 128:(k + 1) * 128] for k in range(ncol)))
    fwd = lax.broadcasted_iota(jnp.int32, (ncol, S5_ROWS, 128), 2) < S5_STATE
    hin_ref[...] = jnp.where(fwd, hf[...], hb[...])


def _s5_scan(sloc, a16, h0l):
    ncol = S5_SCAN_COLS
    w = ncol * 128
    return pl.pallas_call(
        _s5_scan_kernel,
        grid=(S5_STATE_COLS // ncol,),
        in_specs=[
            pl.BlockSpec((ncol, S5_ROWS, 128), lambda k: (k, 0, 0)),
            pl.BlockSpec((1, w), lambda k: (0, k)),
            pl.BlockSpec((DEC_BATCH, w), lambda k: (0, k)),
        ],
        out_specs=[
            pl.BlockSpec((ncol, S5_ROWS, 128), lambda k: (k, 0, 0)),
            pl.BlockSpec((BATCH, w), lambda k: (0, k)),
        ],
        out_shape=[
            jax.ShapeDtypeStruct((S5_STATE_COLS, S5_ROWS, 128), F32),
            jax.ShapeDtypeStruct((BATCH, S5_STATE_COLS * 128), F32),
        ],
        scratch_shapes=[pltpu.VMEM((ncol, S5_ROWS, 128), F32)] * 2,
        name="s5_scan",
    )(sloc, a16, h0l)


def _s5_out_kernel(u_ref, hin_ref, wt_ref, wca_ref, d_ref, y_ref):
    u = u_ref[...]
    hin = jnp.concatenate([hin_ref[k] for k in range(S5_SLAB_COLS)], axis=1).astype(BF16)
    y_ref[...] = _dot(u.astype(BF16), wt_ref[...]) + _dot(hin, wca_ref[...]) + u * d_ref[...]


def _s5_out(uj, hin, wt, wca, dj):
    return pl.pallas_call(
        _s5_out_kernel,
        grid=(S5_SLABS, S5_ROWS // S5_ROW_TILE),
        in_specs=[
            pl.BlockSpec((None, S5_ROW_TILE, S5_SLAB_W), lambda j, p: (j, p, 0)),
            pl.BlockSpec((S5_SLAB_COLS, S5_ROW_TILE, 128), lambda j, p: (j, p, 0)),
            pl.BlockSpec((None, S5_SLAB_W, S5_SLAB_W), lambda j, p: (j, 0, 0)),
            pl.BlockSpec((None, S5_SLAB_W, S5_SLAB_W), lambda j, p: (j, 0, 0)),
            pl.BlockSpec((None, 1, S5_SLAB_W), lambda j, p: (j, 0, 0)),
        ],
        out_specs=pl.BlockSpec((None, S5_ROW_TILE, S5_SLAB_W), lambda j, p: (j, p, 0)),
        out_shape=jax.ShapeDtypeStruct((S5_SLABS, S5_ROWS, S5_SLAB_W), F32),
        compiler_params=pltpu.CompilerParams(vmem_limit_bytes=VMEM_LIMIT),
        name="s5_out",
    )(uj, hin, wt, wca, dj)


@functools.lru_cache(maxsize=None)
def _gla_consts():
    n = GLA_BLK
    nl = GLA_LEVELS
    r = np.arange(n)
    seg = np.zeros((nl + 2, n, n), np.float32)
    up = np.zeros((n, 128), np.int32)
    for l in range(nl):
        for row in range(n):
            if (row >> l) & 1:
                seg[l, row, (row >> l) << l:row + 1] = 1.0
            else:
                seg[l, row, row + 1:(row | ((1 << l) - 1)) + 1] = 1.0
        up[:, l] = (r >> l) & 1
    for row in range(n):
        seg[nl, row, :row + 1] = 1.0
        seg[nl + 1, row, row + 1:] = 1.0
    i = r[:, None]
    j = r[None, :]
    x = np.maximum(i ^ j, 1)
    lev = np.where(j < i, np.floor(np.log2(x)).astype(np.int32), np.where(i == j, nl, -1)).astype(np.int32)
    seg2 = np.stack([seg, seg[:, ::-1, ::-1]]).reshape(2, (nl + 2) * n, n)
    up2 = np.stack([up, up[::-1]])
    lev2 = np.stack([lev, lev[::-1, ::-1]])
    return seg2, up2, lev2


@functools.lru_cache(maxsize=None)
def _gla_tables():
    rowblk, seq, first, last = [], [], [], []
    for d in range(2):
        rb, sq, fi, la = [], [], [], []
        for s in range(BATCH + DEC_BATCH):
            nblk = 1 if s < BATCH else DEC_SEQ // GLA_BLK
            base = s if s < BATCH else NTOK_C // GLA_BLK + (s - BATCH) * nblk
            order = range(nblk) if d == 0 else range(nblk - 1, -1, -1)
            for pos, b in enumerate(order):
                rb.append(base + b)
                sq.append(s)
                fi.append(int(pos == 0))
                la.append(int(pos == nblk - 1))
        rowblk.append(rb); seq.append(sq); first.append(fi); last.append(la)
    as_np = lambda a: np.asarray(a, np.int32)
    return as_np(rowblk), as_np(seq), as_np(first), as_np(last)


def _gla_kernel(rowblk_ref, seq_ref, first_ref, last_ref,
                q_ref, k_ref, v_ref, lr_ref, wgk_ref, bgk_ref, seg_ref, up_ref, lev_ref, s0_ref,
                o_ref, fin_ref, x_scr, z_scr, st_scr):
    d = pl.program_id(0)
    n = pl.program_id(1)
    nl = GLA_LEVELS
    blk = GLA_BLK

    @pl.when(first_ref[d, n] == 1)
    def _():
        st_scr[...] = jnp.zeros_like(st_scr)
        for h in range(GLA_HEADS):
            st_scr[h * GLA_DK:(h + 1) * GLA_DK, h * GLA_DV:(h + 1) * GLA_DV] = s0_ref[h]

    q = q_ref[...] * (GLA_DK ** -0.5)
    k = k_ref[...]
    vb = v_ref[...].astype(BF16)
    x = _dot(lr_ref[...].astype(BF16), wgk_ref[...]) + bgk_ref[...]
    gk = (jnp.minimum(x, 0.0) - jnp.log1p(jnp.exp(-jnp.abs(x)))) * (1.0 / GLA_NORMALIZER)
    g_hi = gk.astype(BF16)
    g_lo = (gk - g_hi.astype(F32)).astype(BF16)
    x_scr[...] = _dot(seg_ref[...], g_hi) + _dot(seg_ref[...], g_lo)
    ones = jnp.ones((blk, 128), BF16)
    tot = _dot_tn(g_hi, ones) + _dot_tn(g_lo, ones)

    for l in range(nl):
        e = jnp.exp(x_scr[l * blk:(l + 1) * blk, :])
        z_scr[l] = (jnp.where(up_ref[:, l:l + 1] != 0, q, k) * e).astype(BF16)
    z_scr[nl] = q.astype(BF16)
    kb = k.astype(BF16)

    lev = lev_ref[...]
    lane = lax.broadcasted_iota(jnp.int32, (blk, GLA_QK), 1)
    for h in range(GLA_HEADS):
        in_head = (lane >= h * GLA_DK) & (lane < (h + 1) * GLA_DK)
        att = jnp.zeros((blk, blk), F32)
        for l in range(nl + 1):
            lhs = z_scr[l]
            rhs = jnp.where(in_head, kb if l == nl else lhs, jnp.zeros_like(lhs))
            att = jnp.where(lev == l, _dot_nt(lhs, rhs), att)
        o_ref[:, h * GLA_DV:(h + 1) * GLA_DV] = _dot(att.astype(BF16), vb[:, h * GLA_DV:(h + 1) * GLA_DV])

    st = st_scr[...]
    q_in = (q * jnp.exp(x_scr[nl * blk:(nl + 1) * blk, :])).astype(BF16)
    o_ref[...] += _dot(q_in, st.astype(BF16))
    k_out = (k * jnp.exp(x_scr[(nl + 1) * blk:(nl + 2) * blk, :])).astype(BF16)
    kv = _dot_tn(k_out, vb)
    row = lax.broadcasted_iota(jnp.int32, (GLA_QK, GLA_V), 0)
    col = lax.broadcasted_iota(jnp.int32, (GLA_QK, GLA_V), 1)
    same_head = (row >> 6) == (col >> 7)
    decay = jnp.exp(tot)
    decay = jnp.concatenate([decay] * GLA_HEADS, axis=1)
    st_new = decay * st + jnp.where(same_head, kv, 0.0)
    st_scr[...] = st_new

    @pl.when(last_ref[d, n] == 1)
    def _():
        for h in range(GLA_HEADS):
            fin_ref[h] = st_scr[h * GLA_DK:(h + 1) * GLA_DK, h * GLA_DV:(h + 1) * GLA_DV]


def _gla_mix(bslab, lr, wgk, bgk, s0):
    seg, up, lev = _gla_consts()
    rowblk, seq, first, last = _gla_tables()
    nsteps = rowblk.shape[1]
    nseq = BATCH + DEC_BATCH
    nl = GLA_LEVELS
    grid_spec = pltpu.PrefetchScalarGridSpec(
        num_scalar_prefetch=4,
        grid=(2, nsteps),
        in_specs=[
            pl.BlockSpec((GLA_BLK, GLA_QK), lambda d, n, rb, sq, fi, la: (rb[d, n], 0)),
            pl.BlockSpec((GLA_BLK, GLA_QK), lambda d, n, rb, sq, fi, la: (rb[d, n], 1)),
            pl.BlockSpec((GLA_BLK, GLA_V), lambda d, n, rb, sq, fi, la: (rb[d, n], 1)),
            pl.BlockSpec((GLA_BLK, 128), lambda d, n, rb, sq, fi, la: (rb[d, n], 0)),
            pl.BlockSpec((None, 128, GLA_QK), lambda d, n, rb, sq, fi, la: (d, 0, 0)),
            pl.BlockSpec((None, 1, GLA_QK), lambda d, n, rb, sq, fi, la: (d, 0, 0)),
            pl.BlockSpec((None, (nl + 2) * GLA_BLK, GLA_BLK), lambda d, n, rb, sq, fi, la: (d, 0, 0)),
            pl.BlockSpec((None, GLA_BLK, 128), lambda d, n, rb, sq, fi, la: (d, 0, 0)),
            pl.BlockSpec((None, GLA_BLK, GLA_BLK), lambda d, n, rb, sq, fi, la: (d, 0, 0)),
            pl.BlockSpec((None, None, GLA_HEADS, GLA_DK, GLA_DV),
                         lambda d, n, rb, sq, fi, la: (sq[d, n], d, 0, 0, 0)),
        ],
        out_specs=[
            pl.BlockSpec((None, GLA_BLK, GLA_V), lambda d, n, rb, sq, fi, la: (d, rb[d, n], 0)),
            pl.BlockSpec((None, None, GLA_HEADS, GLA_DK, GLA_DV),
                         lambda d, n, rb, sq, fi, la: (sq[d, n], d, 0, 0, 0)),
        ],
        scratch_shapes=[
            pltpu.VMEM(((nl + 2) * GLA_BLK, GLA_QK), F32),
            pltpu.VMEM((nl + 1, GLA_BLK, GLA_QK), BF16),
            pltpu.VMEM((GLA_QK, GLA_V), F32),
        ],
    )
    return pl.pallas_call(
        _gla_kernel,
        grid_spec=grid_spec,
        out_shape=[
            jax.ShapeDtypeStruct((2, NTOK, GLA_V), F32),
            jax.ShapeDtypeStruct((nseq, 2, GLA_HEADS, GLA_DK, GLA_DV), F32),
        ],
        compiler_params=pltpu.CompilerParams(vmem_limit_bytes=VMEM_LIMIT),
        name="gla_mix",
    )(jnp.asarray(rowblk), jnp.asarray(seq), jnp.asarray(first), jnp.asarray(last),
      bslab, bslab, bslab, lr, wgk, bgk,
      jnp.asarray(seg, BF16), jnp.asarray(up), jnp.asarray(lev), s0)


def _softmax_head(s_list, v_list, sink):
    m = sink
    for s in s_list:
        m = jnp.maximum(m, jnp.max(s, axis=-1, keepdims=True))
    den = jnp.exp(sink - m)
    acc = None
    for s, v in zip(s_list, v_list):
        p = jnp.exp(s - m)
        den = den + jnp.sum(p, axis=-1, keepdims=True)
        pv = _dot(p.astype(BF16), v)
        acc = pv if acc is None else acc + pv
    return acc / den


def _attn_ctx_kernel(sink_ref, q_ref, k_ref, v_ref, o_ref):
    k = k_ref[...]
    v = v_ref[...]
    ks = (k.astype(BF16), pltpu.roll(k, 64, 1).astype(BF16))
    vs = (v.astype(BF16), pltpu.roll(v, 64, 1).astype(BF16))
    lo = lax.broadcasted_iota(jnp.int32, (SEQ, 128), 1) < HEAD_DIM
    for t in range(ATT_HEADS // 2):
        qt = q_ref[:, t * 128:(t + 1) * 128] * (HEAD_DIM ** -0.5)
        kvh = t // 2
        outs = []
        for p in range(2):
            sel = lo if p == 0 else jnp.logical_not(lo)
            qm = jnp.where(sel, qt, 0.0).astype(BF16)
            which = 0 if p == kvh else 1
            s = _dot_nt(qm, ks[which])
            outs.append(_softmax_head([s], [vs[which]], sink_ref[2 * t + p]))
        o_ref[:, t * 128:(t + 1) * 128] = jnp.where(lo, outs[0], outs[1])


def _attn_ctx(sink, cslab):
    return pl.pallas_call(
        _attn_ctx_kernel,
        grid=(BATCH,),
        in_specs=[
            pl.BlockSpec(memory_space=pltpu.SMEM),
            pl.BlockSpec((SEQ, ATT_Q), lambda b: (b, 0)),
            pl.BlockSpec((SEQ, ATT_KV), lambda b: (b, 4)),
            pl.BlockSpec((SEQ, ATT_KV), lambda b: (b, 5)),
        ],
        out_specs=pl.BlockSpec((SEQ, ATT_Q), lambda b: (b, 0)),
        out_shape=jax.ShapeDtypeStruct((NTOK, ATT_Q), F32),
        name="attn_ctx",
    )(sink, cslab, cslab, cslab)


def _attn_lat_kernel(sink_ref, q_ref, kp_ref, kc_ref, kn_ref, vp_ref, vc_ref, vn_ref,
                     ck_ref, cv_ref, cos_ref, sin_ref, prev_ref, o_ref):
    del prev_ref
    j = pl.program_id(1)
    nb = DEC_SEQ // ATT_BLOCK
    lane = lax.broadcasted_iota(jnp.int32, (ATT_BLOCK, 128), 1)
    lo = lane < HEAD_DIM
    first16 = (lane & 31) < 16

    def rope(x, blk_idx):
        r0 = pl.multiple_of(blk_idx * ATT_BLOCK, ATT_BLOCK)
        c = cos_ref[pl.ds(r0, ATT_BLOCK), :]
        s = sin_ref[pl.ds(r0, ATT_BLOCK), :]
        xs = jnp.where(first16, pltpu.roll(x, 112, 1), pltpu.roll(x, 16, 1))
        return x * c + xs * s

    kw = jnp.concatenate([rope(kp_ref[...], jnp.maximum(j - 1, 0)), rope(kc_ref[...], j),
                          rope(kn_ref[...], jnp.minimum(j + 1, nb - 1))], axis=0)
    vw = jnp.concatenate([vp_ref[...], vc_ref[...], vn_ref[...]], axis=0)
    ck = ck_ref[...]
    cv = cv_ref[...]
    kws = (kw.astype(BF16), pltpu.roll(kw, 64, 1).astype(BF16))
    vws = (vw.astype(BF16), pltpu.roll(vw, 64, 1).astype(BF16))
    cks = (ck.astype(BF16), pltpu.roll(ck, 64, 1).astype(BF16))
    cvs = (cv.astype(BF16), pltpu.roll(cv, 64, 1).astype(BF16))
    qi = lax.broadcasted_iota(jnp.int32, (ATT_BLOCK, 3 * ATT_BLOCK), 0)
    kc = lax.broadcasted_iota(jnp.int32, (ATT_BLOCK, 3 * ATT_BLOCK), 1)
    valid = jnp.abs(kc - ATT_BLOCK - qi) <= WINDOW
    valid = valid & ((j > 0) | (kc >= ATT_BLOCK)) & ((j < nb - 1) | (kc < 2 * ATT_BLOCK))
    for t in range(ATT_HEADS // 2):
        qt = rope(q_ref[:, t * 128:(t + 1) * 128], j) * (HEAD_DIM ** -0.5)
        kvh = t // 2
        outs = []
        for p in range(2):
            sel = lo if p == 0 else jnp.logical_not(lo)
            qm = jnp.where(sel, qt, 0.0).astype(BF16)
            which = 0 if p == kvh else 1
            s_w = jnp.where(valid, _dot_nt(qm, kws[which]), -1e30)
            s_c = _dot_nt(qm, cks[which])
            outs.append(_softmax_head([s_w, s_c], [vws[which], cvs[which]], sink_ref[2 * t + p]))
        o_ref[:, t * 128:(t + 1) * 128] = jnp.where(lo, outs[0], outs[1])


def _attn_lat(sink, cslab, ck, cv, cos_t, sin_t, y_prev):
    nb = DEC_SEQ // ATT_BLOCK
    base = NTOK_C // ATT_BLOCK
    cur = lambda b, j: base + b * nb + j
    prv = lambda b, j: base + b * nb + jnp.maximum(j - 1, 0)
    nxt = lambda b, j: base + b * nb + jnp.minimum(j + 1, nb - 1)
    kv_spec = lambda row, col: pl.BlockSpec((ATT_BLOCK, ATT_KV), lambda b, j: (row(b, j), col))
    return pl.pallas_call(
        _attn_lat_kernel,
        grid=(DEC_BATCH, nb),
        in_specs=[
            pl.BlockSpec(memory_space=pltpu.SMEM),
            pl.BlockSpec((ATT_BLOCK, ATT_Q), lambda b, j: (cur(b, j), 0)),
            kv_spec(prv, 4), kv_spec(cur, 4), kv_spec(nxt, 4),
            kv_spec(prv, 5), kv_spec(cur, 5), kv_spec(nxt, 5),
            pl.BlockSpec((None, PAST_LEN, ATT_KV), lambda b, j: (b, 0, 0)),
            pl.BlockSpec((None, PAST_LEN, ATT_KV), lambda b, j: (b, 0, 0)),
            pl.BlockSpec((DEC_SEQ, 128), lambda b, j: (0, 0)),
            pl.BlockSpec((DEC_SEQ, 128), lambda b, j: (0, 0)),
            pl.BlockSpec(memory_space=pl.ANY),
        ],
        out_specs=pl.BlockSpec((ATT_BLOCK, ATT_Q), lambda b, j: (cur(b, j), 0)),
        out_shape=jax.ShapeDtypeStruct((NTOK, ATT_Q), F32),
        input_output_aliases={12: 0},
        name="attn_lat",
    )(sink, cslab, cslab, cslab, cslab, cslab, cslab, cslab, ck, cv, cos_t, sin_t, y_prev)


def _rope_tables():
    rows = DEC_SEQ // GRID_W
    row = np.repeat(np.arange(rows, dtype=np.float32), GRID_W)
    col = np.tile(np.arange(GRID_W, dtype=np.float32), rows)
    quarter = HEAD_DIM // 4
    inv = jnp.asarray(ROPE_BASE, F32) ** (-jnp.arange(quarter, dtype=F32) / quarter)
    lane = np.arange(128)
    use_row = (lane % HEAD_DIM) < HEAD_DIM // 2
    pos = jnp.where(use_row[None, :], jnp.asarray(row)[:, None], jnp.asarray(col)[:, None])
    ang = pos * inv[lane % quarter][None, :]
    sign = np.where((lane % 32) < 16, -1.0, 1.0).astype(np.float32)
    return jnp.cos(ang), jnp.sin(ang) * sign[None, :]


def _merge_kernel(x_ref, mod_ref, g_ref, ys5_ref, ogf_ref, ogb_ref, gb_ref, yc_ref, gate_ref,
                  gng_ref, wglu_ref, wbr_ref, wout_ref, o_ref):
    y = jnp.concatenate([ys5_ref[j] for j in range(S5_SLABS)], axis=1)
    y = 0.5 * y * (1.0 + jnp.tanh(math.sqrt(2.0 / math.pi) * (y + 0.044715 * (y * y * y))))
    ag = _dot(y.astype(BF16), wglu_ref[...])
    y_a = ag[:, :S5_WIDTH] * _sigmoid(ag[:, S5_WIDTH:])
    gng = gng_ref[...]
    gb = gb_ref[...]
    parts = []
    for h in range(GLA_HEADS):
        sl = slice(h * GLA_DV, (h + 1) * GLA_DV)
        o = ogf_ref[:, sl] + ogb_ref[:, sl]
        g = gb[:, sl]
        parts.append(_rms(o, gng) * (g * _sigmoid(g)))
    y_b = jnp.concatenate(parts, axis=1)
    merged = None
    for n, yn in enumerate((y_a, y_b, yc_ref[...])):
        proj = _dot(yn.astype(BF16), wbr_ref[n])
        term = _sigmoid(gate_ref[:, n * D_MODEL:(n + 1) * D_MODEL]) * proj
        merged = term if merged is None else merged + term
    mixed = _dot(merged.astype(BF16), wout_ref[...])
    g1 = mod_ref[:, 2 * D_MODEL:3 * D_MODEL]
    o_ref[...] = x_ref[...] + g1 * _rms(mixed, g_ref[...])


def _merge(x, mod, g, ys5, og, bslab, yc, gates, gng, wglu, wbr, wout):
    tok = lambda width, col=0: pl.BlockSpec((TM, width), lambda i: (i, col))
    full = lambda shape: pl.BlockSpec(shape, lambda i: (0,) * len(shape))
    return pl.pallas_call(
        _merge_kernel,
        grid=(NTOK // TM,),
        in_specs=[
            tok(D_MODEL),
            pl.BlockSpec((None, 1, 6 * D_MODEL), lambda i: (_mod_row(i), 0, 0)),
            full((1, D_MODEL)),
            pl.BlockSpec((S5_SLABS, TM, 128), lambda i: (0, i, 0)),
            pl.BlockSpec((None, TM, GLA_V), lambda i: (0, i, 0)),
            pl.BlockSpec((None, TM, GLA_V), lambda i: (1, i, 0)),
            tok(GLA_V, 2),
            tok(ATT_Q),
            tok(N_BRANCH * D_MODEL),
            full((1, GLA_DV)),
            full((S5_WIDTH, 2 * S5_WIDTH)),
            full((N_BRANCH, BRANCH_W, D_MODEL)),
            full((D_MODEL, D_MODEL)),
        ],
        out_specs=tok(D_MODEL),
        out_shape=jax.ShapeDtypeStruct((NTOK, D_MODEL), F32),
        compiler_params=pltpu.CompilerParams(vmem_limit_bytes=VMEM_LIMIT),
        name="merge",
    )(x, mod, g, ys5, og, og, bslab, yc, gates, gng, wglu, wbr, wout)


FFN_SPLIT = 2


def _ffn_kernel(x_ref, mod_ref, gin_ref, gout_ref, w1_ref, w2_ref, o_ref):
    x = x_ref[...]
    sh = mod_ref[:, 3 * D_MODEL:4 * D_MODEL]
    sc = mod_ref[:, 4 * D_MODEL:5 * D_MODEL]
    g2 = mod_ref[:, 5 * D_MODEL:6 * D_MODEL]
    h = (_rms(x, gin_ref[...]) * (1.0 + sc) + sh).astype(BF16)
    ck = FFN_HIDDEN // FFN_SPLIT
    acc = None
    for c in range(FFN_SPLIT):
        a = _dot(h, w1_ref[:, c * ck:(c + 1) * ck])
        b = _dot(h, w1_ref[:, FFN_HIDDEN + c * ck:FFN_HIDDEN + (c + 1) * ck])
        act = (a * _sigmoid(a) * b).astype(BF16)
        part = _dot(act, w2_ref[c * ck:(c + 1) * ck, :])
        acc = part if acc is None else acc + part
    o_ref[...] = x + g2 * _rms(acc, gout_ref[...])


def _ffn(x, mod, gin, gout, w1, w2):
    full = lambda shape: pl.BlockSpec(shape, lambda i: (0,) * len(shape))
    return pl.pallas_call(
        _ffn_kernel,
        grid=(NTOK // TM,),
        in_specs=[
            pl.BlockSpec((TM, D_MODEL), lambda i: (i, 0)),
            pl.BlockSpec((None, 1, 6 * D_MODEL), lambda i: (_mod_row(i), 0, 0)),
            full((1, D_MODEL)),
            full((1, D_MODEL)),
            full((D_MODEL, 2 * FFN_HIDDEN)),
            full((FFN_HIDDEN, D_MODEL)),
        ],
        out_specs=pl.BlockSpec((TM, D_MODEL), lambda i: (i, 0)),
        out_shape=jax.ShapeDtypeStruct((NTOK, D_MODEL), F32),
        compiler_params=pltpu.CompilerParams(vmem_limit_bytes=VMEM_LIMIT),
        name="ffn",
    )(x, mod, gin, gout, w1, w2)


def kernel(x_prompt, x_sample, cache_k, cache_v, state_s5, state_gla, c, c_ctx, w_mod, b_mod, norm_g, w_in,
           s5_lam_re, s5_lam_im, s5_log_step, s5_b_re, s5_b_im, s5_c_re, s5_c_im, s5_d, w_glu, gla_w_gk,
           gla_b_gk, gla_norm_g, att_sink, w_branch, w_out, w_ffn_in, w_ffn_out):
    cond = jnp.concatenate([c_ctx[None, :], c, jnp.zeros((N_MOD_ROWS - 1 - DEC_BATCH, D_MODEL), F32)], axis=0)
    mod_all = _modulation(cond, w_mod, b_mod).reshape(DEPTH, N_MOD_ROWS, 1, 6 * D_MODEL)
    cos_t, sin_t = _rope_tables()
    x = jnp.concatenate([x_prompt.reshape(NTOK_C, D_MODEL), x_sample.reshape(NTOK_L, D_MODEL)], axis=0)
    new_k, new_v, new_s5, new_gla = [], [], [], []
    for i in range(DEPTH):
        mod = mod_all[i]
        wi = w_in[i]
        w_in_p = jnp.concatenate(
            [wi[:, :2048], wi[:, 2080:], wi[:, 2048:2080], jnp.zeros((D_MODEL, W_IN_COLS - 5920), F32)],
            axis=1).astype(BF16)
        u, bslab, cslab, gates, lr = _inproj(x, mod, norm_g[i, 0][None, :], w_in_p)

        wt, web, wca, a16, dj = _s5_prep(s5_lam_re[i], s5_lam_im[i], s5_log_step[i], s5_b_re[i], s5_b_im[i],
                                         s5_c_re[i], s5_c_im[i], s5_d[i])
        uj = u.reshape(S5_SLABS, S5_ROWS, S5_SLAB_W)
        h0l = state_s5[:, i].astype(F32).transpose(0, 2, 4, 1, 3).reshape(DEC_BATCH, S5_GROUPS * 256)
        hin, finc = _s5_scan(_s5_state(uj, web), a16, h0l)
        ys5 = _s5_out(uj, hin, wt, wca, dj).reshape(S5_SLABS, NTOK, 128)
        new_s5.append(finc.reshape(BATCH, S5_GROUPS, 2, 2, S5_STATE).transpose(0, 3, 1, 4, 2))

        wgk = jnp.zeros((2, 128, GLA_QK), F32)
        wgk = wgk.at[0, 0:GLA_RANK].set(gla_w_gk[i, 0]).at[1, GLA_RANK:2 * GLA_RANK].set(gla_w_gk[i, 1])
        s0 = jnp.concatenate([jnp.zeros((BATCH, 2, GLA_HEADS, GLA_DK, GLA_DV), F32),
                              state_gla[:, i].astype(F32)], axis=0)
        og, gla_fin = _gla_mix(bslab, lr, wgk.astype(BF16), gla_b_gk[i][:, None, :].astype(F32), s0)
        new_gla.append(gla_fin[:BATCH])

        sink = att_sink[i].astype(F32)
        yc = _attn_ctx(sink, cslab)
        yc = _attn_lat(sink, cslab, cache_k[:, i].reshape(DEC_BATCH, PAST_LEN, ATT_KV).astype(F32),
                       cache_v[:, i].reshape(DEC_BATCH, PAST_LEN, ATT_KV).astype(F32), cos_t, sin_t, yc)
        new_k.append(cslab[:NTOK_C, ATT_Q:ATT_Q + ATT_KV].reshape(BATCH, SEQ, ATT_KV_HEADS, HEAD_DIM))
        new_v.append(cslab[:NTOK_C, ATT_Q + ATT_KV:].reshape(BATCH, SEQ, ATT_KV_HEADS, HEAD_DIM))

        x = _merge(x, mod, norm_g[i, 1][None, :], ys5, og, bslab, yc, gates, gla_norm_g[i][None, :],
                   w_glu[i].astype(BF16), w_branch[i].astype(BF16), w_out[i].astype(BF16))
        x = _ffn(x, mod, norm_g[i, 2][None, :], norm_g[i, 3][None, :],
                 w_ffn_in[i].astype(BF16), w_ffn_out[i].astype(BF16))

    return (x[:NTOK_C].reshape(BATCH, SEQ, D_MODEL), x[NTOK_C:].reshape(DEC_BATCH, DEC_SEQ, D_MODEL),
            jnp.stack(new_k, axis=1), jnp.stack(new_v, axis=1),
            jnp.stack(new_s5, axis=1), jnp.stack(new_gla, axis=1))
```

```python
import functools
import math

import numpy as np
import jax
import jax.numpy as jnp
from jax import lax
from jax.experimental import pallas as pl
from jax.experimental.pallas import tpu as pltpu

F32 = jnp.float32
BF16 = jnp.bfloat16

D_MODEL = 1024
BATCH = 16
SEQ = 256
DEPTH = 2
DEC_BATCH = 8
DEC_SEQ = 1024
PAST_LEN = 256
GRID_W = 64
ROPE_BASE = 10000.0
S5_WIDTH = 512
S5_GROUP = 16
S5_GROUPS = 32
S5_STATE = 64
GLA_HEADS = 4
GLA_DK = 64
GLA_DV = 128
GLA_QK = 256
GLA_V = 512
GLA_RANK = 16
GLA_NORMALIZER = 16.0
ATT_HEADS = 8
ATT_KV_HEADS = 2
HEAD_DIM = 64
ATT_Q = 512
ATT_KV = 128
WINDOW = 128
ATT_BLOCK = 128
N_BRANCH = 3
BRANCH_W = 512
FFN_HIDDEN = 2816
RMS_EPS = 1e-6

NTOK_C = BATCH * SEQ
NTOK_L = DEC_BATCH * DEC_SEQ
NTOK = NTOK_C + NTOK_L
TM = 512
N_MOD_ROWS = 16

W_IN_COLS = 6016
S5_CHUNK = 16
S5_SLABS = S5_WIDTH // 128
S5_SLAB_W = S5_CHUNK * 128
S5_ROWS_C = NTOK_C // S5_CHUNK
S5_ROWS = NTOK // S5_CHUNK
S5_ROW_TILE = 256
GLA_BLK = 256
GLA_LEVELS = 8
VMEM_LIMIT = 56 * 1024 * 1024


def _dot(a, b):
    return jnp.dot(a, b, preferred_element_type=F32)


def _dot_nt(a, b):
    return lax.dot_general(a, b, (((1,), (1,)), ((), ())), preferred_element_type=F32)


def _dot_tn(a, b):
    return lax.dot_general(a, b, (((0,), (0,)), ((), ())), preferred_element_type=F32)


def _rms(x, g):
    return x * lax.rsqrt(jnp.mean(x * x, axis=-1, keepdims=True) + RMS_EPS) * g


def _sigmoid(x):
    return 1.0 / (1.0 + jnp.exp(-x))


def _mod_row(i):
    nct = NTOK_C // TM
    return jnp.where(i < nct, 0, 1 + (i - nct) // (DEC_SEQ // TM))


def _mod_kernel(c_ref, w_ref, b_ref, o_ref):
    c = c_ref[...]
    s = (c * _sigmoid(c)).astype(BF16)
    o_ref[...] = _dot(s, w_ref[...].astype(BF16)) + b_ref[...]


def _modulation(cond, w_mod, b_mod):
    tn = 1024
    return pl.pallas_call(
        _mod_kernel,
        grid=(DEPTH, 6 * D_MODEL // tn),
        in_specs=[
            pl.BlockSpec((N_MOD_ROWS, D_MODEL), lambda l, n: (0, 0)),
            pl.BlockSpec((None, D_MODEL, tn), lambda l, n: (l, 0, n)),
            pl.BlockSpec((None, 1, tn), lambda l, n: (l, 0, n)),
        ],
        out_specs=pl.BlockSpec((None, N_MOD_ROWS, tn), lambda l, n: (l, 0, n)),
        out_shape=jax.ShapeDtypeStruct((DEPTH, N_MOD_ROWS, 6 * D_MODEL), F32),
        name="modulation",
    )(cond, w_mod, b_mod.reshape(DEPTH, 1, 6 * D_MODEL))


_IN_SLABS = ((0, 512), (512, 1536), (2048, 768), (2816, 3072), (5888, 128))


def _inproj_kernel(x_ref, mod_ref, g_ref, w_ref, u_ref, b_ref, c_ref, gate_ref, lr_ref):
    x = x_ref[...]
    mod = mod_ref[...]
    h = _rms(x, g_ref[...]) * (1.0 + mod[:, D_MODEL:2 * D_MODEL]) + mod[:, 0:D_MODEL]
    h = h.astype(BF16)
    for j in range(S5_SLABS):
        u_ref[j] = _dot(h, w_ref[:, j * 128:(j + 1) * 128])
    for (off, width), o_ref in zip(_IN_SLABS[1:], (b_ref, c_ref, gate_ref, lr_ref)):
        o_ref[...] = _dot(h, w_ref[:, off:off + width])


def _inproj(x, mod, g, w):
    return pl.pallas_call(
        _inproj_kernel,
        grid=(NTOK // TM,),
        in_specs=[
            pl.BlockSpec((TM, D_MODEL), lambda i: (i, 0)),
            pl.BlockSpec((None, 1, 6 * D_MODEL), lambda i: (_mod_row(i), 0, 0)),
            pl.BlockSpec((1, D_MODEL), lambda i: (0, 0)),
            pl.BlockSpec((D_MODEL, W_IN_COLS), lambda i: (0, 0), pipeline_mode=pl.Buffered(1)),
        ],
        out_specs=[pl.BlockSpec((S5_SLABS, TM, 128), lambda i: (0, i, 0))]
        + [pl.BlockSpec((TM, width), lambda i: (i, 0)) for _, width in _IN_SLABS[1:]],
        out_shape=[jax.ShapeDtypeStruct((S5_SLABS, NTOK, 128), F32)]
        + [jax.ShapeDtypeStruct((NTOK, width), F32) for _, width in _IN_SLABS[1:]],
        compiler_params=pltpu.CompilerParams(vmem_limit_bytes=VMEM_LIMIT),
        name="inproj",
    )(x, mod, g, w)


@functools.lru_cache(maxsize=None)
def _s5_expanders():
    seg = 8
    spread = np.zeros((seg, 256, S5_SLAB_W), np.float32)
    place = np.zeros((seg, 256, S5_SLAB_W), np.float32)
    col = np.arange(256)
    for gl in range(seg):
        spread[gl, col, (col // S5_GROUP) * 128 + gl * S5_GROUP + col % S5_GROUP] = 1.0
        place[gl, col, gl * 256 + col] = 1.0
    return spread, place


def _s5_prep_kernel(par_ref, bre_ref, bim_ref, cre_ref, cim_ref, spread_ref, place_ref,
                    wt_ref, web_ref, wca_ref, a16_ref):
    n = S5_CHUNK
    lam_re = par_ref[0:1, :]
    lam_im = par_ref[1:2, :]
    dt = jnp.exp(par_ref[2:3, :])
    lr = lam_re * dt
    li = lam_im * dt
    krow = lax.broadcasted_iota(jnp.int32, (24, 128), 0).astype(F32)
    tab_mag = jnp.exp(krow * lr)
    tab_re = tab_mag * jnp.cos(krow * li)
    tab_im = tab_mag * jnp.sin(krow * li)
    ar = tab_re[1:2, :]
    ai = tab_im[1:2, :]
    nr = ar - 1.0
    den = lam_re * lam_re + lam_im * lam_im
    fr = (nr * lam_re + ai * lam_im) / den
    fi = (ai * lam_re - nr * lam_im) / den
    b_re = bre_ref[...]
    b_im = bim_ref[...]
    br = fr * b_re - fi * b_im
    bi = fr * b_im + fi * b_re
    c_re = cre_ref[...]
    c_im = cim_ref[...]

    def lo_half(shape):
        return lax.broadcasted_iota(jnp.int32, shape, 1) < S5_STATE

    def tile_rows(a):
        return jnp.concatenate([a] * n, axis=0)

    fwd16 = lo_half((S5_GROUP, 128))

    def powers(t_re, t_im, k_fwd, k_bwd):
        def pick(t, b):
            kf, kb = k_fwd(b), k_bwd(b)
            return jnp.where(fwd16, jnp.broadcast_to(t[kf:kf + 1, :], (S5_GROUP, 128)),
                             jnp.broadcast_to(t[kb:kb + 1, :], (S5_GROUP, 128)))
        return (jnp.concatenate([pick(t_re, b) for b in range(n)], axis=0),
                jnp.concatenate([pick(t_im, b) for b in range(n)], axis=0))

    fwd = lo_half((n * S5_GROUP, 128))
    brt, bit, crt, cit = tile_rows(br), tile_rows(bi), tile_rows(c_re), tile_rows(c_im)

    per, pei = powers(tab_re, tab_im, lambda s: n - 1 - s, lambda s: s)
    eb = jnp.concatenate([brt * per - bit * pei, brt * pei + bit * per], axis=1)
    pcr, pci = powers(tab_re, tab_im, lambda t: t + 1, lambda t: n - t)
    ca = jnp.concatenate([(crt * pcr - cit * pci).T, (-(crt * pci + cit * pcr)).T], axis=0)

    def one_dir(x, d):
        sw = pltpu.roll(x, S5_STATE, 1)
        lo = lo_half(x.shape)
        return jnp.where(lo, x, sw) if d == 0 else jnp.where(lo, sw, x)

    klag = []
    for d in range(2):
        lhs = jnp.where(lo_half(br.shape), one_dir(br, d), -one_dir(bi, d))
        crd, cid = tile_rows(one_dir(c_re, d)), tile_rows(one_dir(c_im, d))
        lag = (lambda b: b) if d == 0 else (lambda b: n - 1 - b)
        pr, pi = powers(one_dir(tab_re, d), one_dir(tab_im, d), lag, lag)
        rhs_t = jnp.where(fwd, crd * pr - cid * pi, crd * pi + cid * pr)
        klag.append(lax.dot_general(lhs, rhs_t, (((1,), (1,)), ((), ())),
                                    precision=lax.Precision.HIGHEST, preferred_element_type=F32))
    lane = lax.broadcasted_iota(jnp.int32, (S5_GROUP, n * S5_GROUP), 1)
    rows = []
    for s in range(n):
        f = klag[0] if s == 0 else jnp.where(lane >= S5_GROUP * s, pltpu.roll(klag[0], S5_GROUP * s, 1), 0.0)
        sh = (n * S5_GROUP - S5_GROUP * (n - 1 - s)) % (n * S5_GROUP)
        b = klag[1] if sh == 0 else pltpu.roll(klag[1], sh, 1)
        rows.append(f + jnp.where(lane < S5_GROUP * (s + 1), b, 0.0))
    toep = jnp.concatenate(rows, axis=0)

    spread = spread_ref[...]
    wt_ref[...] = _dot(toep.astype(BF16), spread).astype(BF16).reshape(n, S5_GROUP, S5_SLAB_W)
    web_ref[...] = _dot(eb.astype(BF16), place_ref[...]).astype(BF16).reshape(n, S5_GROUP, S5_SLAB_W)
    wca_ref[...] = _dot(ca.astype(BF16), spread).astype(BF16)
    a16_ref[0:1, :] = tab_re[n:n + 1, :]
    a16_ref[1:2, :] = tab_im[n:n + 1, :]


def _s5_prep(lam_re, lam_im, log_step, b_re, b_im, c_re, c_im, d_skip):
    seg = 8
    par = jnp.stack([lam_re, lam_im, log_step]).astype(F32).transpose(2, 0, 1, 3).reshape(S5_GROUPS, 3, 128)
    par = jnp.concatenate([par, jnp.zeros((S5_GROUPS, 5, 128), F32)], axis=1)
    b_t = lambda b: b.astype(F32).transpose(1, 3, 0, 2).reshape(S5_GROUPS, S5_GROUP, 128)
    c_t = lambda c: c.astype(F32).transpose(1, 2, 0, 3).reshape(S5_GROUPS, S5_GROUP, 128)
    spread, place = _s5_expanders()
    vec = pl.BlockSpec((None, S5_GROUP, 128), lambda g: (g, 0, 0))
    exp_spec = pl.BlockSpec((None, 256, S5_SLAB_W), lambda g: (g % seg, 0, 0))
    rows_spec = pl.BlockSpec((None, S5_CHUNK, None, S5_GROUP, S5_SLAB_W), lambda g: (g // seg, 0, g % seg, 0, 0))
    wt, web, wca, a16 = pl.pallas_call(
        _s5_prep_kernel,
        grid=(S5_GROUPS,),
        in_specs=[pl.BlockSpec((None, 8, 128), lambda g: (g, 0, 0)), vec, vec, vec, vec, exp_spec, exp_spec],
        out_specs=[
            rows_spec, rows_spec,
            pl.BlockSpec((None, None, 256, S5_SLAB_W), lambda g: (g // seg, g % seg, 0, 0)),
            pl.BlockSpec((None, 2, 128), lambda g: (g, 0, 0)),
        ],
        out_shape=[
            jax.ShapeDtypeStruct((S5_SLABS, S5_CHUNK, seg, S5_GROUP, S5_SLAB_W), BF16),
            jax.ShapeDtypeStruct((S5_SLABS, S5_CHUNK, seg, S5_GROUP, S5_SLAB_W), BF16),
            jax.ShapeDtypeStruct((S5_SLABS, seg, 256, S5_SLAB_W), BF16),
            jax.ShapeDtypeStruct((S5_GROUPS, 2, 128), F32),
        ],
        name="s5_prep",
    )(par, b_t(b_re), b_t(b_im), c_t(c_re), c_t(c_im), jnp.asarray(spread, BF16), jnp.asarray(place, BF16))
    mat = (S5_SLABS, S5_SLAB_W, S5_SLAB_W)
    dj = jnp.tile(d_skip.astype(F32).reshape(S5_SLABS, 1, 128), (1, 1, S5_CHUNK))
    return wt.reshape(mat), web.reshape(mat), wca.reshape(mat), a16.reshape(1, S5_SLABS * S5_SLAB_W), dj


S5_STATE_COLS = S5_SLABS * S5_SLAB_W // 128
S5_SLAB_COLS = S5_SLAB_W // 128


def _s5_state_kernel(u_ref, w_ref, o_ref):
    s = _dot(u_ref[...].astype(BF16), w_ref[...])
    for k in range(S5_SLAB_COLS):
        o_ref[k] = s[:, k * 128:(k + 1) * 128]


def _s5_state(uj, web):
    return pl.pallas_call(
        _s5_state_kernel,
        grid=(S5_SLABS, S5_ROWS // S5_ROW_TILE),
        in_specs=[
            pl.BlockSpec((None, S5_ROW_TILE, S5_SLAB_W), lambda j, p: (j, p, 0)),
            pl.BlockSpec((None, S5_SLAB_W, S5_SLAB_W), lambda j, p: (j, 0, 0)),
        ],
        out_specs=pl.BlockSpec((S5_SLAB_COLS, S5_ROW_TILE, 128), lambda j, p: (j, p, 0)),
        out_shape=jax.ShapeDtypeStruct((S5_STATE_COLS, S5_ROWS, 128), F32),
        compiler_params=pltpu.CompilerParams(vmem_limit_bytes=VMEM_LIMIT),
        name="s5_state",
    )(uj, web)


S5_SCAN_COLS = 8


def _s5_scan_kernel(s_ref, a_ref, h0_ref, hin_ref, fin_ref, sg, hf, hb):
    ncol = S5_SCAN_COLS

    def scan(row0, nc, nb, h0):
        is_f = lax.broadcasted_iota(jnp.int32, (nb, 128), 1) < S5_STATE
        chunk_rows = lambda c: pl.ds(pl.multiple_of(row0 + c * nb, 8), nb)

        def gather(c, carry):
            for k in range(ncol):
                sg[k, chunk_rows(c), :] = s_ref[k, pl.ds(row0 + c, nb, stride=nc), :]
            return carry

        lax.fori_loop(0, nc, gather, 0)

        def body(c, hs):
            rf = chunk_rows(c)
            rb = chunk_rows(nc - 1 - c)
            new = []
            for m in range(ncol // 2):
                h_re, h_im = hs[2 * m], hs[2 * m + 1]
                a_re = a_ref[:, (2 * m) * 128:(2 * m + 1) * 128]
                a_im = a_ref[:, (2 * m + 1) * 128:(2 * m + 2) * 128]
                loc = []
                for k, h in ((2 * m, h_re), (2 * m + 1, h_im)):
                    hf[k, rf, :] = h
                    hb[k, rb, :] = h
                    loc.append(jnp.where(is_f, sg[k, rf, :], sg[k, rb, :]))
                new.append(a_re * h_re - a_im * h_im + loc[0])
                new.append(a_re * h_im + a_im * h_re + loc[1])
            return tuple(new)

        fin = lax.fori_loop(0, nc, body, h0)

        def scatter(c, carry):
            for k in range(ncol):
                hin_ref[k, pl.ds(row0 + c, nb, stride=nc), :] = jnp.where(
                    is_f, hf[k, chunk_rows(c), :], hb[k, chunk_rows(c), :])
            return carry

        lax.fori_loop(0, nc, scatter, 0)
        return fin

    fin = scan(0, SEQ // S5_CHUNK, BATCH, tuple(jnp.zeros((BATCH, 128), F32) for _ in range(ncol)))
    for k in range(ncol):
        fin_ref[:, k * 128:(k + 1) * 128] = fin[k]
    scan(S5_ROWS_C, DEC_SEQ // S5_CHUNK, DEC_BATCH,
         tuple(h0_ref[:, k * 128:(k + 1) * 128] for k in range(ncol)))


def _s5_scan(sloc, a16, h0l):
    ncol = S5_SCAN_COLS
    w = ncol * 128
    return pl.pallas_call(
        _s5_scan_kernel,
        grid=(S5_STATE_COLS // ncol,),
        in_specs=[
            pl.BlockSpec((ncol, S5_ROWS, 128), lambda k: (k, 0, 0)),
            pl.BlockSpec((1, w), lambda k: (0, k)),
            pl.BlockSpec((DEC_BATCH, w), lambda k: (0, k)),
        ],
        out_specs=[
            pl.BlockSpec((ncol, S5_ROWS, 128), lambda k: (k, 0, 0)),
            pl.BlockSpec((BATCH, w), lambda k: (0, k)),
        ],
        out_shape=[
            jax.ShapeDtypeStruct((S5_STATE_COLS, S5_ROWS, 128), F32),
            jax.ShapeDtypeStruct((BATCH, S5_STATE_COLS * 128), F32),
        ],
        scratch_shapes=[pltpu.VMEM((ncol, S5_ROWS, 128), F32)] * 3,
        name="s5_scan",
    )(sloc, a16, h0l)


def _s5_out_kernel(u_ref, hin_ref, wt_ref, wca_ref, d_ref, y_ref):
    u = u_ref[...]
    hin = jnp.concatenate([hin_ref[k] for k in range(S5_SLAB_COLS)], axis=1).astype(BF16)
    y_ref[...] = _dot(u.astype(BF16), wt_ref[...]) + _dot(hin, wca_ref[...]) + u * d_ref[...]


def _s5_out(uj, hin, wt, wca, dj):
    return pl.pallas_call(
        _s5_out_kernel,
        grid=(S5_SLABS, S5_ROWS // S5_ROW_TILE),
        in_specs=[
            pl.BlockSpec((None, S5_ROW_TILE, S5_SLAB_W), lambda j, p: (j, p, 0)),
            pl.BlockSpec((S5_SLAB_COLS, S5_ROW_TILE, 128), lambda j, p: (j, p, 0)),
            pl.BlockSpec((None, S5_SLAB_W, S5_SLAB_W), lambda j, p: (j, 0, 0)),
            pl.BlockSpec((None, S5_SLAB_W, S5_SLAB_W), lambda j, p: (j, 0, 0)),
            pl.BlockSpec((None, 1, S5_SLAB_W), lambda j, p: (j, 0, 0)),
        ],
        out_specs=pl.BlockSpec((None, S5_ROW_TILE, S5_SLAB_W), lambda j, p: (j, p, 0)),
        out_shape=jax.ShapeDtypeStruct((S5_SLABS, S5_ROWS, S5_SLAB_W), F32),
        compiler_params=pltpu.CompilerParams(vmem_limit_bytes=VMEM_LIMIT),
        name="s5_out",
    )(uj, hin, wt, wca, dj)


@functools.lru_cache(maxsize=None)
def _gla_consts():
    n = GLA_BLK
    nl = GLA_LEVELS
    r = np.arange(n)
    seg = np.zeros((nl + 2, n, n), np.float32)
    up = np.zeros((n, 128), np.int32)
    for l in range(nl):
        for row in range(n):
            if (row >> l) & 1:
                seg[l, row, (row >> l) << l:row + 1] = 1.0
            else:
                seg[l, row, row + 1:(row | ((1 << l) - 1)) + 1] = 1.0
        up[:, l] = (r >> l) & 1
    for row in range(n):
        seg[nl, row, :row + 1] = 1.0
        seg[nl + 1, row, row + 1:] = 1.0
    i = r[:, None]
    j = r[None, :]
    x = np.maximum(i ^ j, 1)
    lev = np.where(j < i, np.floor(np.log2(x)).astype(np.int32), np.where(i == j, nl, -1)).astype(np.int32)
    seg2 = np.stack([seg, seg[:, ::-1, ::-1]]).reshape(2, (nl + 2) * n, n)
    up2 = np.stack([up, up[::-1]])
    lev2 = np.stack([lev, lev[::-1, ::-1]])
    return seg2, up2, lev2


@functools.lru_cache(maxsize=None)
def _gla_tables():
    rowblk, seq, first, last = [], [], [], []
    for d in range(2):
        rb, sq, fi, la = [], [], [], []
        for s in range(BATCH + DEC_BATCH):
            nblk = 1 if s < BATCH else DEC_SEQ // GLA_BLK
            base = s if s < BATCH else NTOK_C // GLA_BLK + (s - BATCH) * nblk
            order = range(nblk) if d == 0 else range(nblk - 1, -1, -1)
            for pos, b in enumerate(order):
                rb.append(base + b)
                sq.append(s)
                fi.append(int(pos == 0))
                la.append(int(pos == nblk - 1))
        rowblk.append(rb); seq.append(sq); first.append(fi); last.append(la)
    as_np = lambda a: np.asarray(a, np.int32)
    return as_np(rowblk), as_np(seq), as_np(first), as_np(last)


def _gla_kernel(rowblk_ref, seq_ref, first_ref, last_ref,
                q_ref, k_ref, v_ref, lr_ref, wgk_ref, bgk_ref, seg_ref, up_ref, lev_ref, s0_ref,
                o_ref, fin_ref, x_scr, z_scr, st_scr):
    d = pl.program_id(0)
    n = pl.program_id(1)
    nl = GLA_LEVELS
    blk = GLA_BLK

    @pl.when(first_ref[d, n] == 1)
    def _():
        st_scr[...] = jnp.zeros_like(st_scr)
        for h in range(GLA_HEADS):
            st_scr[h * GLA_DK:(h + 1) * GLA_DK, h * GLA_DV:(h + 1) * GLA_DV] = s0_ref[h]

    q = q_ref[...] * (GLA_DK ** -0.5)
    k = k_ref[...]
    vb = v_ref[...].astype(BF16)
    x = _dot(lr_ref[...].astype(BF16), wgk_ref[...]) + bgk_ref[...]
    gk = (jnp.minimum(x, 0.0) - jnp.log1p(jnp.exp(-jnp.abs(x)))) * (1.0 / GLA_NORMALIZER)
    g_hi = gk.astype(BF16)
    g_lo = (gk - g_hi.astype(F32)).astype(BF16)
    x_scr[...] = _dot(seg_ref[...], g_hi) + _dot(seg_ref[...], g_lo)
    ones = jnp.ones((blk, 128), BF16)
    tot = _dot_tn(g_hi, ones) + _dot_tn(g_lo, ones)

    for l in range(nl):
        e = jnp.exp(x_scr[l * blk:(l + 1) * blk, :])
        z_scr[l] = (jnp.where(up_ref[:, l:l + 1] != 0, q, k) * e).astype(BF16)
    z_scr[nl] = q.astype(BF16)
    kb = k.astype(BF16)

    lev = lev_ref[...]
    lane = lax.broadcasted_iota(jnp.int32, (blk, GLA_QK), 1)
    for h in range(GLA_HEADS):
        in_head = (lane >= h * GLA_DK) & (lane < (h + 1) * GLA_DK)
        att = jnp.zeros((blk, blk), F32)
        for l in range(nl + 1):
            lhs = z_scr[l]
            rhs = jnp.where(in_head, kb if l == nl else lhs, jnp.zeros_like(lhs))
            att = jnp.where(lev == l, _dot_nt(lhs, rhs), att)
        o_ref[:, h * GLA_DV:(h + 1) * GLA_DV] = _dot(att.astype(BF16), vb[:, h * GLA_DV:(h + 1) * GLA_DV])

    st = st_scr[...]
    q_in = (q * jnp.exp(x_scr[nl * blk:(nl + 1) * blk, :])).astype(BF16)
    o_ref[...] += _dot(q_in, st.astype(BF16))
    k_out = (k * jnp.exp(x_scr[(nl + 1) * blk:(nl + 2) * blk, :])).astype(BF16)
    kv = _dot_tn(k_out, vb)
    row = lax.broadcasted_iota(jnp.int32, (GLA_QK, GLA_V), 0)
    col = lax.broadcasted_iota(jnp.int32, (GLA_QK, GLA_V), 1)
    same_head = (row >> 6) == (col >> 7)
    decay = jnp.exp(tot)
    decay = jnp.concatenate([decay] * GLA_HEADS, axis=1)
    st_new = decay * st + jnp.where(same_head, kv, 0.0)
    st_scr[...] = st_new

    @pl.when(last_ref[d, n] == 1)
    def _():
        for h in range(GLA_HEADS):
            fin_ref[h] = st_scr[h * GLA_DK:(h + 1) * GLA_DK, h * GLA_DV:(h + 1) * GLA_DV]


def _gla_mix(bslab, lr, wgk, bgk, s0):
    seg, up, lev = _gla_consts()
    rowblk, seq, first, last = _gla_tables()
    nsteps = rowblk.shape[1]
    nseq = BATCH + DEC_BATCH
    nl = GLA_LEVELS
    grid_spec = pltpu.PrefetchScalarGridSpec(
        num_scalar_prefetch=4,
        grid=(2, nsteps),
        in_specs=[
            pl.BlockSpec((GLA_BLK, GLA_QK), lambda d, n, rb, sq, fi, la: (rb[d, n], 0)),
            pl.BlockSpec((GLA_BLK, GLA_QK), lambda d, n, rb, sq, fi, la: (rb[d, n], 1)),
            pl.BlockSpec((GLA_BLK, GLA_V), lambda d, n, rb, sq, fi, la: (rb[d, n], 1)),
            pl.BlockSpec((GLA_BLK, 128), lambda d, n, rb, sq, fi, la: (rb[d, n], 0)),
            pl.BlockSpec((None, 128, GLA_QK), lambda d, n, rb, sq, fi, la: (d, 0, 0)),
            pl.BlockSpec((None, 1, GLA_QK), lambda d, n, rb, sq, fi, la: (d, 0, 0)),
            pl.BlockSpec((None, (nl + 2) * GLA_BLK, GLA_BLK), lambda d, n, rb, sq, fi, la: (d, 0, 0)),
            pl.BlockSpec((None, GLA_BLK, 128), lambda d, n, rb, sq, fi, la: (d, 0, 0)),
            pl.BlockSpec((None, GLA_BLK, GLA_BLK), lambda d, n, rb, sq, fi, la: (d, 0, 0)),
            pl.BlockSpec((None, None, GLA_HEADS, GLA_DK, GLA_DV),
                         lambda d, n, rb, sq, fi, la: (sq[d, n], d, 0, 0, 0)),
        ],
        out_specs=[
            pl.BlockSpec((None, GLA_BLK, GLA_V), lambda d, n, rb, sq, fi, la: (d, rb[d, n], 0)),
            pl.BlockSpec((None, None, GLA_HEADS, GLA_DK, GLA_DV),
                         lambda d, n, rb, sq, fi, la: (sq[d, n], d, 0, 0, 0)),
        ],
        scratch_shapes=[
            pltpu.VMEM(((nl + 2) * GLA_BLK, GLA_QK), F32),
            pltpu.VMEM((nl + 1, GLA_BLK, GLA_QK), BF16),
            pltpu.VMEM((GLA_QK, GLA_V), F32),
        ],
    )
    return pl.pallas_call(
        _gla_kernel,
        grid_spec=grid_spec,
        out_shape=[
            jax.ShapeDtypeStruct((2, NTOK, GLA_V), F32),
            jax.ShapeDtypeStruct((nseq, 2, GLA_HEADS, GLA_DK, GLA_DV), F32),
        ],
        compiler_params=pltpu.CompilerParams(vmem_limit_bytes=VMEM_LIMIT),
        name="gla_mix",
    )(jnp.asarray(rowblk), jnp.asarray(seq), jnp.asarray(first), jnp.asarray(last),
      bslab, bslab, bslab, lr, wgk, bgk,
      jnp.asarray(seg, BF16), jnp.asarray(up), jnp.asarray(lev), s0)


def _softmax_head(s_list, v_list, sink):
    m = sink
    for s in s_list:
        m = jnp.maximum(m, jnp.max(s, axis=-1, keepdims=True))
    den = jnp.exp(sink - m)
    acc = None
    for s, v in zip(s_list, v_list):
        p = jnp.exp(s - m)
        den = den + jnp.sum(p, axis=-1, keepdims=True)
        pv = _dot(p.astype(BF16), v)
        acc = pv if acc is None else acc + pv
    return acc / den


def _attn_ctx_kernel(sink_ref, q_ref, k_ref, v_ref, o_ref):
    k = k_ref[...]
    v = v_ref[...]
    ks = (k.astype(BF16), pltpu.roll(k, 64, 1).astype(BF16))
    vs = (v.astype(BF16), pltpu.roll(v, 64, 1).astype(BF16))
    lo = lax.broadcasted_iota(jnp.int32, (SEQ, 128), 1) < HEAD_DIM
    for t in range(ATT_HEADS // 2):
        qt = q_ref[:, t * 128:(t + 1) * 128] * (HEAD_DIM ** -0.5)
        kvh = t // 2
        outs = []
        for p in range(2):
            sel = lo if p == 0 else jnp.logical_not(lo)
            qm = jnp.where(sel, qt, 0.0).astype(BF16)
            which = 0 if p == kvh else 1
            s = _dot_nt(qm, ks[which])
            outs.append(_softmax_head([s], [vs[which]], sink_ref[2 * t + p]))
        o_ref[:, t * 128:(t + 1) * 128] = jnp.where(lo, outs[0], outs[1])


def _attn_ctx(sink, cslab):
    return pl.pallas_call(
        _attn_ctx_kernel,
        grid=(BATCH,),
        in_specs=[
            pl.BlockSpec(memory_space=pltpu.SMEM),
            pl.BlockSpec((SEQ, ATT_Q), lambda b: (b, 0)),
            pl.BlockSpec((SEQ, ATT_KV), lambda b: (b, 4)),
            pl.BlockSpec((SEQ, ATT_KV), lambda b: (b, 5)),
        ],
        out_specs=pl.BlockSpec((SEQ, ATT_Q), lambda b: (b, 0)),
        out_shape=jax.ShapeDtypeStruct((NTOK, ATT_Q), F32),
        name="attn_ctx",
    )(sink, cslab, cslab, cslab)


def _attn_lat_kernel(sink_ref, q_ref, kp_ref, kc_ref, kn_ref, vp_ref, vc_ref, vn_ref,
                     ck_ref, cv_ref, cos_ref, sin_ref, prev_ref, o_ref):
    del prev_ref
    j = pl.program_id(1)
    nb = DEC_SEQ // ATT_BLOCK
    lane = lax.broadcasted_iota(jnp.int32, (ATT_BLOCK, 128), 1)
    lo = lane < HEAD_DIM
    first16 = (lane & 31) < 16

    def rope(x, blk_idx):
        r0 = pl.multiple_of(blk_idx * ATT_BLOCK, ATT_BLOCK)
        c = cos_ref[pl.ds(r0, ATT_BLOCK), :]
        s = sin_ref[pl.ds(r0, ATT_BLOCK), :]
        xs = jnp.where(first16, pltpu.roll(x, 112, 1), pltpu.roll(x, 16, 1))
        return x * c + xs * s

    kw = jnp.concatenate([rope(kp_ref[...], jnp.maximum(j - 1, 0)), rope(kc_ref[...], j),
                          rope(kn_ref[...], jnp.minimum(j + 1, nb - 1))], axis=0)
    vw = jnp.concatenate([vp_ref[...], vc_ref[...], vn_ref[...]], axis=0)
    ck = ck_ref[...]
    cv = cv_ref[...]
    kws = (kw.astype(BF16), pltpu.roll(kw, 64, 1).astype(BF16))
    vws = (vw.astype(BF16), pltpu.roll(vw, 64, 1).astype(BF16))
    cks = (ck.astype(BF16), pltpu.roll(ck, 64, 1).astype(BF16))
    cvs = (cv.astype(BF16), pltpu.roll(cv, 64, 1).astype(BF16))
    qi = lax.broadcasted_iota(jnp.int32, (ATT_BLOCK, 3 * ATT_BLOCK), 0)
    kc = lax.broadcasted_iota(jnp.int32, (ATT_BLOCK, 3 * ATT_BLOCK), 1)
    valid = jnp.abs(kc - ATT_BLOCK - qi) <= WINDOW
    valid = valid & ((j > 0) | (kc >= ATT_BLOCK)) & ((j < nb - 1) | (kc < 2 * ATT_BLOCK))
    for t in range(ATT_HEADS // 2):
        qt = rope(q_ref[:, t * 128:(t + 1) * 128], j) * (HEAD_DIM ** -0.5)
        kvh = t // 2
        outs = []
        for p in range(2):
            sel = lo if p == 0 else jnp.logical_not(lo)
            qm = jnp.where(sel, qt, 0.0).astype(BF16)
            which = 0 if p == kvh else 1
            s_w = jnp.where(valid, _dot_nt(qm, kws[which]), -1e30)
            s_c = _dot_nt(qm, cks[which])
            outs.append(_softmax_head([s_w, s_c], [vws[which], cvs[which]], sink_ref[2 * t + p]))
        o_ref[:, t * 128:(t + 1) * 128] = jnp.where(lo, outs[0], outs[1])


def _attn_lat(sink, cslab, ck, cv, cos_t, sin_t, y_prev):
    nb = DEC_SEQ // ATT_BLOCK
    base = NTOK_C // ATT_BLOCK
    cur = lambda b, j: base + b * nb + j
    prv = lambda b, j: base + b * nb + jnp.maximum(j - 1, 0)
    nxt = lambda b, j: base + b * nb + jnp.minimum(j + 1, nb - 1)
    kv_spec = lambda row, col: pl.BlockSpec((ATT_BLOCK, ATT_KV), lambda b, j: (row(b, j), col))
    return pl.pallas_call(
        _attn_lat_kernel,
        grid=(DEC_BATCH, nb),
        in_specs=[
            pl.BlockSpec(memory_space=pltpu.SMEM),
            pl.BlockSpec((ATT_BLOCK, ATT_Q), lambda b, j: (cur(b, j), 0)),
            kv_spec(prv, 4), kv_spec(cur, 4), kv_spec(nxt, 4),
            kv_spec(prv, 5), kv_spec(cur, 5), kv_spec(nxt, 5),
            pl.BlockSpec((None, PAST_LEN, ATT_KV), lambda b, j: (b, 0, 0)),
            pl.BlockSpec((None, PAST_LEN, ATT_KV), lambda b, j: (b, 0, 0)),
            pl.BlockSpec((DEC_SEQ, 128), lambda b, j: (0, 0)),
            pl.BlockSpec((DEC_SEQ, 128), lambda b, j: (0, 0)),
            pl.BlockSpec(memory_space=pl.ANY),
        ],
        out_specs=pl.BlockSpec((ATT_BLOCK, ATT_Q), lambda b, j: (cur(b, j), 0)),
        out_shape=jax.ShapeDtypeStruct((NTOK, ATT_Q), F32),
        input_output_aliases={12: 0},
        name="attn_lat",
    )(sink, cslab, cslab, cslab, cslab, cslab, cslab, cslab, ck, cv, cos_t, sin_t, y_prev)


def _rope_tables():
    rows = DEC_SEQ // GRID_W
    row = np.repeat(np.arange(rows, dtype=np.float32), GRID_W)
    col = np.tile(np.arange(GRID_W, dtype=np.float32), rows)
    quarter = HEAD_DIM // 4
    inv = jnp.asarray(ROPE_BASE, F32) ** (-jnp.arange(quarter, dtype=F32) / quarter)
    lane = np.arange(128)
    use_row = (lane % HEAD_DIM) < HEAD_DIM // 2
    pos = jnp.where(use_row[None, :], jnp.asarray(row)[:, None], jnp.asarray(col)[:, None])
    ang = pos * inv[lane % quarter][None, :]
    sign = np.where((lane % 32) < 16, -1.0, 1.0).astype(np.float32)
    return jnp.cos(ang), jnp.sin(ang) * sign[None, :]


def _merge_kernel(x_ref, mod_ref, g_ref, ys5_ref, ogf_ref, ogb_ref, gb_ref, yc_ref, gate_ref,
                  gng_ref, wglu_ref, wbr_ref, wout_ref, o_ref):
    y = jnp.concatenate([ys5_ref[j] for j in range(S5_SLABS)], axis=1)
    y = 0.5 * y * (1.0 + jnp.tanh(math.sqrt(2.0 / math.pi) * (y + 0.044715 * (y * y * y))))
    ag = _dot(y.astype(BF16), wglu_ref[...])
    y_a = ag[:, :S5_WIDTH] * _sigmoid(ag[:, S5_WIDTH:])
    gng = gng_ref[...]
    gb = gb_ref[...]
    parts = []
    for h in range(GLA_HEADS):
        sl = slice(h * GLA_DV, (h + 1) * GLA_DV)
        o = ogf_ref[:, sl] + ogb_ref[:, sl]
        g = gb[:, sl]
        parts.append(_rms(o, gng) * (g * _sigmoid(g)))
    y_b = jnp.concatenate(parts, axis=1)
    merged = None
    for n, yn in enumerate((y_a, y_b, yc_ref[...])):
        proj = _dot(yn.astype(BF16), wbr_ref[n])
        term = _sigmoid(gate_ref[:, n * D_MODEL:(n + 1) * D_MODEL]) * proj
        merged = term if merged is None else merged + term
    mixed = _dot(merged.astype(BF16), wout_ref[...])
    g1 = mod_ref[:, 2 * D_MODEL:3 * D_MODEL]
    o_ref[...] = x_ref[...] + g1 * _rms(mixed, g_ref[...])


def _merge(x, mod, g, ys5, og, bslab, yc, gates, gng, wglu, wbr, wout):
    tok = lambda width, col=0: pl.BlockSpec((TM, width), lambda i: (i, col))
    full = lambda shape: pl.BlockSpec(shape, lambda i: (0,) * len(shape), pipeline_mode=pl.Buffered(1))
    return pl.pallas_call(
        _merge_kernel,
        grid=(NTOK // TM,),
        in_specs=[
            tok(D_MODEL),
            pl.BlockSpec((None, 1, 6 * D_MODEL), lambda i: (_mod_row(i), 0, 0)),
            full((1, D_MODEL)),
            pl.BlockSpec((S5_SLABS, TM, 128), lambda i: (0, i, 0)),
            pl.BlockSpec((None, TM, GLA_V), lambda i: (0, i, 0)),
            pl.BlockSpec((None, TM, GLA_V), lambda i: (1, i, 0)),
            tok(GLA_V, 2),
            tok(ATT_Q),
            tok(N_BRANCH * D_MODEL),
            full((1, GLA_DV)),
            full((S5_WIDTH, 2 * S5_WIDTH)),
            full((N_BRANCH, BRANCH_W, D_MODEL)),
            full((D_MODEL, D_MODEL)),
        ],
        out_specs=tok(D_MODEL),
        out_shape=jax.ShapeDtypeStruct((NTOK, D_MODEL), F32),
        compiler_params=pltpu.CompilerParams(vmem_limit_bytes=VMEM_LIMIT),
        name="merge",
    )(x, mod, g, ys5, og, og, bslab, yc, gates, gng, wglu, wbr, wout)


FFN_SPLIT = 2


def _ffn_kernel(x_ref, mod_ref, gin_ref, gout_ref, w1_ref, w2_ref, o_ref):
    x = x_ref[...]
    sh = mod_ref[:, 3 * D_MODEL:4 * D_MODEL]
    sc = mod_ref[:, 4 * D_MODEL:5 * D_MODEL]
    g2 = mod_ref[:, 5 * D_MODEL:6 * D_MODEL]
    h = (_rms(x, gin_ref[...]) * (1.0 + sc) + sh).astype(BF16)
    ck = FFN_HIDDEN // FFN_SPLIT
    acc = None
    for c in range(FFN_SPLIT):
        a = _dot(h, w1_ref[:, c * ck:(c + 1) * ck])
        b = _dot(h, w1_ref[:, FFN_HIDDEN + c * ck:FFN_HIDDEN + (c + 1) * ck])
        act = (a * _sigmoid(a) * b).astype(BF16)
        part = _dot(act, w2_ref[c * ck:(c + 1) * ck, :])
        acc = part if acc is None else acc + part
    o_ref[...] = x + g2 * _rms(acc, gout_ref[...])


def _ffn(x, mod, gin, gout, w1, w2):
    full = lambda shape: pl.BlockSpec(shape, lambda i: (0,) * len(shape), pipeline_mode=pl.Buffered(1))
    return pl.pallas_call(
        _ffn_kernel,
        grid=(NTOK // TM,),
        in_specs=[
            pl.BlockSpec((TM, D_MODEL), lambda i: (i, 0)),
            pl.BlockSpec((None, 1, 6 * D_MODEL), lambda i: (_mod_row(i), 0, 0)),
            full((1, D_MODEL)),
            full((1, D_MODEL)),
            full((D_MODEL, 2 * FFN_HIDDEN)),
            full((FFN_HIDDEN, D_MODEL)),
        ],
        out_specs=pl.BlockSpec((TM, D_MODEL), lambda i: (i, 0)),
        out_shape=jax.ShapeDtypeStruct((NTOK, D_MODEL), F32),
        compiler_params=pltpu.CompilerParams(vmem_limit_bytes=VMEM_LIMIT),
        name="ffn",
    )(x, mod, gin, gout, w1, w2)


def kernel(x_prompt, x_sample, cache_k, cache_v, state_s5, state_gla, c, c_ctx, w_mod, b_mod, norm_g, w_in,
           s5_lam_re, s5_lam_im, s5_log_step, s5_b_re, s5_b_im, s5_c_re, s5_c_im, s5_d, w_glu, gla_w_gk,
           gla_b_gk, gla_norm_g, att_sink, w_branch, w_out, w_ffn_in, w_ffn_out):
    cond = jnp.concatenate([c_ctx[None, :], c, jnp.zeros((N_MOD_ROWS - 1 - DEC_BATCH, D_MODEL), F32)], axis=0)
    mod_all = _modulation(cond, w_mod, b_mod).reshape(DEPTH, N_MOD_ROWS, 1, 6 * D_MODEL)
    cos_t, sin_t = _rope_tables()
    x = jnp.concatenate([x_prompt.reshape(NTOK_C, D_MODEL), x_sample.reshape(NTOK_L, D_MODEL)], axis=0)
    new_k, new_v, new_s5, new_gla = [], [], [], []
    for i in range(DEPTH):
        mod = mod_all[i]
        wi = w_in[i]
        w_in_p = jnp.concatenate(
            [wi[:, :2048], wi[:, 2080:], wi[:, 2048:2080], jnp.zeros((D_MODEL, W_IN_COLS - 5920), F32)],
            axis=1).astype(BF16)
        u, bslab, cslab, gates, lr = _inproj(x, mod, norm_g[i, 0][None, :], w_in_p)

        wt, web, wca, a16, dj = _s5_prep(s5_lam_re[i], s5_lam_im[i], s5_log_step[i], s5_b_re[i], s5_b_im[i],
                                         s5_c_re[i], s5_c_im[i], s5_d[i])
        uj = u.reshape(S5_SLABS, S5_ROWS, S5_SLAB_W)
        h0l = state_s5[:, i].astype(F32).transpose(0, 2, 4, 1, 3).reshape(DEC_BATCH, S5_GROUPS * 256)
        hin, finc = _s5_scan(_s5_state(uj, web), a16, h0l)
        ys5 = _s5_out(uj, hin, wt, wca, dj).reshape(S5_SLABS, NTOK, 128)
        new_s5.append(finc.reshape(BATCH, S5_GROUPS, 2, 2, S5_STATE).transpose(0, 3, 1, 4, 2))

        wgk = jnp.zeros((2, 128, GLA_QK), F32)
        wgk = wgk.at[0, 0:GLA_RANK].set(gla_w_gk[i, 0]).at[1, GLA_RANK:2 * GLA_RANK].set(gla_w_gk[i, 1])
        s0 = jnp.concatenate([jnp.zeros((BATCH, 2, GLA_HEADS, GLA_DK, GLA_DV), F32),
                              state_gla[:, i].astype(F32)], axis=0)
        og, gla_fin = _gla_mix(bslab, lr, wgk.astype(BF16), gla_b_gk[i][:, None, :].astype(F32), s0)
        new_gla.append(gla_fin[:BATCH])

        sink = att_sink[i].astype(F32)
        yc = _attn_ctx(sink, cslab)
        yc = _attn_lat(sink, cslab, cache_k[:, i].reshape(DEC_BATCH, PAST_LEN, ATT_KV).astype(F32),
                       cache_v[:, i].reshape(DEC_BATCH, PAST_LEN, ATT_KV).astype(F32), cos_t, sin_t, yc)
        new_k.append(cslab[:NTOK_C, ATT_Q:ATT_Q + ATT_KV].reshape(BATCH, SEQ, ATT_KV_HEADS, HEAD_DIM))
        new_v.append(cslab[:NTOK_C, ATT_Q + ATT_KV:].reshape(BATCH, SEQ, ATT_KV_HEADS, HEAD_DIM))

        x = _merge(x, mod, norm_g[i, 1][None, :], ys5, og, bslab, yc, gates, gla_norm_g[i][None, :],
                   w_glu[i].astype(BF16), w_branch[i].astype(BF16), w_out[i].astype(BF16))
        x = _ffn(x, mod, norm_g[i, 2][None, :], norm_g[i, 3][None, :],
                 w_ffn_in[i].astype(BF16), w_ffn_out[i].astype(BF16))

    return (x[:NTOK_C].reshape(BATCH, SEQ, D_MODEL), x[NTOK_C:].reshape(DEC_BATCH, DEC_SEQ, D_MODEL),
            jnp.stack(new_k, axis=1), jnp.stack(new_v, axis=1),
            jnp.stack(new_s5, axis=1), jnp.stack(new_gla, axis=1))
```

```python
import functools
import math

import numpy as np
import jax
import jax.numpy as jnp
from jax import lax
from jax.experimental import pallas as pl
from jax.experimental.pallas import tpu as pltpu

F32 = jnp.float32
BF16 = jnp.bfloat16

D_MODEL = 1024
BATCH = 16
SEQ = 256
DEPTH = 2
DEC_BATCH = 8
DEC_SEQ = 1024
PAST_LEN = 256
GRID_W = 64
ROPE_BASE = 10000.0
S5_WIDTH = 512
S5_GROUP = 16
S5_GROUPS = 32
S5_STATE = 64
GLA_HEADS = 4
GLA_DK = 64
GLA_DV = 128
GLA_QK = 256
GLA_V = 512
GLA_RANK = 16
GLA_NORMALIZER = 16.0
ATT_HEADS = 8
ATT_KV_HEADS = 2
HEAD_DIM = 64
ATT_Q = 512
ATT_KV = 128
WINDOW = 128
ATT_BLOCK = 128
N_BRANCH = 3
BRANCH_W = 512
FFN_HIDDEN = 2816
RMS_EPS = 1e-6

NTOK_C = BATCH * SEQ
NTOK_L = DEC_BATCH * DEC_SEQ
NTOK = NTOK_C + NTOK_L
TM = 512
N_MOD_ROWS = 16

W_IN_COLS = 6016
S5_CHUNK = 16
S5_SLABS = S5_WIDTH // 128
S5_SLAB_W = S5_CHUNK * 128
S5_ROWS_C = NTOK_C // S5_CHUNK
S5_ROWS = NTOK // S5_CHUNK
S5_ROW_TILE = 256
GLA_BLK = 256
GLA_LEVELS = 8
VMEM_LIMIT = 56 * 1024 * 1024


def _dot(a, b):
    return jnp.dot(a, b, preferred_element_type=F32)


def _dot_nt(a, b):
    return lax.dot_general(a, b, (((1,), (1,)), ((), ())), preferred_element_type=F32)


def _dot_tn(a, b):
    return lax.dot_general(a, b, (((0,), (0,)), ((), ())), preferred_element_type=F32)


def _rms(x, g):
    return x * lax.rsqrt(jnp.mean(x * x, axis=-1, keepdims=True) + RMS_EPS) * g


def _sigmoid(x):
    return 1.0 / (1.0 + jnp.exp(-x))


def _mod_row(i):
    nct = NTOK_C // TM
    return jnp.where(i < nct, 0, 1 + (i - nct) // (DEC_SEQ // TM))


def _mod_kernel(c_ref, w_ref, b_ref, o_ref):
    c = c_ref[...]
    s = (c * _sigmoid(c)).astype(BF16)
    o_ref[...] = _dot(s, w_ref[...].astype(BF16)) + b_ref[...]


def _modulation(cond, w_mod, b_mod):
    tn = 1024
    return pl.pallas_call(
        _mod_kernel,
        grid=(DEPTH, 6 * D_MODEL // tn),
        in_specs=[
            pl.BlockSpec((N_MOD_ROWS, D_MODEL), lambda l, n: (0, 0)),
            pl.BlockSpec((None, D_MODEL, tn), lambda l, n: (l, 0, n)),
            pl.BlockSpec((None, 1, tn), lambda l, n: (l, 0, n)),
        ],
        out_specs=pl.BlockSpec((None, N_MOD_ROWS, tn), lambda l, n: (l, 0, n)),
        out_shape=jax.ShapeDtypeStruct((DEPTH, N_MOD_ROWS, 6 * D_MODEL), F32),
        name="modulation",
    )(cond, w_mod, b_mod.reshape(DEPTH, 1, 6 * D_MODEL))


_IN_SLABS = ((0, 512), (512, 1536), (2048, 768), (2816, 3072), (5888, 128))
W_IN_SPLIT = 2048
W_IN_GAP = 32
W_IN_TAIL = W_IN_COLS - W_IN_SPLIT


def _inproj_kernel(*refs, split_x):
    if split_x:
        xc_ref, xl_ref, mod_ref, g_ref, w_ref, u_ref, b_ref, c_ref, gate_ref, lr_ref, w_tail, u_stage = refs
    else:
        xc_ref, mod_ref, g_ref, w_ref, u_ref, b_ref, c_ref, gate_ref, lr_ref, w_tail, u_stage = refs
    i = pl.program_id(0)

    @pl.when(i == 0)
    def _():
        r = lax.broadcasted_iota(jnp.int32, (256, 128), 0)
        c = lax.broadcasted_iota(jnp.int32, (256, 128), 1)
        shift = (r == c + W_IN_GAP).astype(BF16)
        head = ((r == c) & (c < W_IN_GAP)).astype(BF16)
        ntile = (W_IN_TAIL - 128) // 128
        for t in range(ntile):
            src = W_IN_SPLIT + 128 * t
            w_tail[:, 128 * t:128 * (t + 1)] = _dot(w_ref[:, src:src + 256], shift).astype(BF16)
        w_tail[:, 128 * ntile:] = _dot(w_ref[:, W_IN_SPLIT:W_IN_SPLIT + 256], head).astype(BF16)

    if split_x:
        x = jnp.where(i < NTOK_C // TM, xc_ref[...], xl_ref[...])
    else:
        x = xc_ref[...]
    mod = mod_ref[...]
    h = _rms(x, g_ref[...]) * (1.0 + mod[:, D_MODEL:2 * D_MODEL]) + mod[:, 0:D_MODEL]
    h = h.astype(BF16)
    for j in range(S5_SLABS):
        u_stage[...] = _dot(h, w_ref[:, j * 128:(j + 1) * 128])
        for s in range(S5_CHUNK):
            u_ref[j, :, s * 128:(s + 1) * 128] = u_stage[pl.ds(s, TM // S5_CHUNK, stride=S5_CHUNK), :]
    b_ref[...] = _dot(h, w_ref[:, 512:W_IN_SPLIT])
    for (off, width), o_ref in zip(_IN_SLABS[2:], (c_ref, gate_ref, lr_ref)):
        o_ref[...] = _dot(h, w_tail[:, off - W_IN_SPLIT:off - W_IN_SPLIT + width])


def _inproj(xs, mod, g, w_all, layer):
    return pl.pallas_call(
        functools.partial(_inproj_kernel, split_x=len(xs) == 2),
        grid=(NTOK // TM,),
        in_specs=_split_token_specs(len(xs)) + [
            pl.BlockSpec((None, 1, 6 * D_MODEL), lambda i: (_mod_row(i), 0, 0)),
            pl.BlockSpec((1, D_MODEL), lambda i: (0, 0)),
            pl.BlockSpec((None, D_MODEL, W_IN_COLS), lambda i: (layer, 0, 0), pipeline_mode=pl.Buffered(1)),
        ],
        out_specs=[pl.BlockSpec((S5_SLABS, TM // S5_CHUNK, S5_SLAB_W), lambda i: (0, i, 0))]
        + [pl.BlockSpec((TM, width), lambda i: (i, 0)) for _, width in _IN_SLABS[1:]],
        out_shape=[jax.ShapeDtypeStruct((S5_SLABS, S5_ROWS, S5_SLAB_W), F32)]
        + [jax.ShapeDtypeStruct((NTOK, width), F32) for _, width in _IN_SLABS[1:]],
        scratch_shapes=[pltpu.VMEM((D_MODEL, W_IN_TAIL), BF16), pltpu.VMEM((TM, 128), F32)],
        compiler_params=pltpu.CompilerParams(vmem_limit_bytes=VMEM_LIMIT),
        name="inproj",
    )(*xs, mod, g, w_all)


@functools.lru_cache(maxsize=None)
def _s5_expanders():
    seg = 8
    spread = np.zeros((seg, 256, S5_SLAB_W), np.float32)
    place = np.zeros((seg, 256, S5_SLAB_W), np.float32)
    col = np.arange(256)
    for gl in range(seg):
        spread[gl, col, (col // S5_GROUP) * 128 + gl * S5_GROUP + col % S5_GROUP] = 1.0
        place[gl, col, gl * 256 + col] = 1.0
    return spread, place


def _s5_prep_kernel(par_ref, bre_ref, bim_ref, cre_ref, cim_ref, spread_ref, place_ref,
                    wt_ref, web_ref, wca_ref, a16_ref):
    n = S5_CHUNK
    lam_re = par_ref[0:1, :]
    lam_im = par_ref[1:2, :]
    dt = jnp.exp(par_ref[2:3, :])
    lr = lam_re * dt
    li = lam_im * dt
    krow = lax.broadcasted_iota(jnp.int32, (24, 128), 0).astype(F32)
    tab_mag = jnp.exp(krow * lr)
    tab_re = tab_mag * jnp.cos(krow * li)
    tab_im = tab_mag * jnp.sin(krow * li)
    ar = tab_re[1:2, :]
    ai = tab_im[1:2, :]
    nr = ar - 1.0
    den = lam_re * lam_re + lam_im * lam_im
    fr = (nr * lam_re + ai * lam_im) / den
    fi = (ai * lam_re - nr * lam_im) / den
    b_re = bre_ref[...]
    b_im = bim_ref[...]
    br = fr * b_re - fi * b_im
    bi = fr * b_im + fi * b_re
    c_re = cre_ref[...]
    c_im = cim_ref[...]

    def lo_half(shape):
        return lax.broadcasted_iota(jnp.int32, shape, 1) < S5_STATE

    def tile_rows(a):
        return jnp.concatenate([a] * n, axis=0)

    fwd16 = lo_half((S5_GROUP, 128))

    def powers(t_re, t_im, k_fwd, k_bwd):
        def pick(t, b):
            kf, kb = k_fwd(b), k_bwd(b)
            return jnp.where(fwd16, jnp.broadcast_to(t[kf:kf + 1, :], (S5_GROUP, 128)),
                             jnp.broadcast_to(t[kb:kb + 1, :], (S5_GROUP, 128)))
        return (jnp.concatenate([pick(t_re, b) for b in range(n)], axis=0),
                jnp.concatenate([pick(t_im, b) for b in range(n)], axis=0))

    fwd = lo_half((n * S5_GROUP, 128))
    brt, bit, crt, cit = tile_rows(br), tile_rows(bi), tile_rows(c_re), tile_rows(c_im)

    per, pei = powers(tab_re, tab_im, lambda s: n - 1 - s, lambda s: s)
    eb = jnp.concatenate([brt * per - bit * pei, brt * pei + bit * per], axis=1)
    pcr, pci = powers(tab_re, tab_im, lambda t: t + 1, lambda t: n - t)
    ca = jnp.concatenate([(crt * pcr - cit * pci).T, (-(crt * pci + cit * pcr)).T], axis=0)

    def one_dir(x, d):
        sw = pltpu.roll(x, S5_STATE, 1)
        lo = lo_half(x.shape)
        return jnp.where(lo, x, sw) if d == 0 else jnp.where(lo, sw, x)

    klag = []
    for d in range(2):
        lhs = jnp.where(lo_half(br.shape), one_dir(br, d), -one_dir(bi, d))
        crd, cid = tile_rows(one_dir(c_re, d)), tile_rows(one_dir(c_im, d))
        lag = (lambda b: b) if d == 0 else (lambda b: n - 1 - b)
        pr, pi = powers(one_dir(tab_re, d), one_dir(tab_im, d), lag, lag)
        rhs_t = jnp.where(fwd, crd * pr - cid * pi, crd * pi + cid * pr)
        klag.append(lax.dot_general(lhs, rhs_t, (((1,), (1,)), ((), ())),
                                    precision=lax.Precision.HIGHEST, preferred_element_type=F32))
    lane = lax.broadcasted_iota(jnp.int32, (S5_GROUP, n * S5_GROUP), 1)
    rows = []
    for s in range(n):
        f = klag[0] if s == 0 else jnp.where(lane >= S5_GROUP * s, pltpu.roll(klag[0], S5_GROUP * s, 1), 0.0)
        sh = (n * S5_GROUP - S5_GROUP * (n - 1 - s)) % (n * S5_GROUP)
        b = klag[1] if sh == 0 else pltpu.roll(klag[1], sh, 1)
        rows.append(f + jnp.where(lane < S5_GROUP * (s + 1), b, 0.0))
    toep = jnp.concatenate(rows, axis=0)

    spread = spread_ref[...]
    wt_ref[...] = _dot(toep.astype(BF16), spread).astype(BF16).reshape(n, S5_GROUP, S5_SLAB_W)
    web_ref[...] = _dot(eb.astype(BF16), place_ref[...]).astype(BF16).reshape(n, S5_GROUP, S5_SLAB_W)
    wca_ref[...] = _dot(ca.astype(BF16), spread).astype(BF16)
    a16_ref[0:1, :] = tab_re[n:n + 1, :]
    a16_ref[1:2, :] = tab_im[n:n + 1, :]


def _s5_prep(lam_re, lam_im, log_step, b_re, b_im, c_re, c_im, d_skip):
    seg = 8
    par = jnp.stack([lam_re, lam_im, log_step]).astype(F32).transpose(2, 0, 1, 3).reshape(S5_GROUPS, 3, 128)
    par = jnp.concatenate([par, jnp.zeros((S5_GROUPS, 5, 128), F32)], axis=1)
    b_t = lambda b: b.astype(F32).transpose(1, 3, 0, 2).reshape(S5_GROUPS, S5_GROUP, 128)
    c_t = lambda c: c.astype(F32).transpose(1, 2, 0, 3).reshape(S5_GROUPS, S5_GROUP, 128)
    spread, place = _s5_expanders()
    vec = pl.BlockSpec((None, S5_GROUP, 128), lambda g: (g, 0, 0))
    exp_spec = pl.BlockSpec((None, 256, S5_SLAB_W), lambda g: (g % seg, 0, 0))
    rows_spec = pl.BlockSpec((None, S5_CHUNK, None, S5_GROUP, S5_SLAB_W), lambda g: (g // seg, 0, g % seg, 0, 0))
    wt, web, wca, a16 = pl.pallas_call(
        _s5_prep_kernel,
        grid=(S5_GROUPS,),
        in_specs=[pl.BlockSpec((None, 8, 128), lambda g: (g, 0, 0)), vec, vec, vec, vec, exp_spec, exp_spec],
        out_specs=[
            rows_spec, rows_spec,
            pl.BlockSpec((None, None, 256, S5_SLAB_W), lambda g: (g // seg, g % seg, 0, 0)),
            pl.BlockSpec((None, 2, 128), lambda g: (g, 0, 0)),
        ],
        out_shape=[
            jax.ShapeDtypeStruct((S5_SLABS, S5_CHUNK, seg, S5_GROUP, S5_SLAB_W), BF16),
            jax.ShapeDtypeStruct((S5_SLABS, S5_CHUNK, seg, S5_GROUP, S5_SLAB_W), BF16),
            jax.ShapeDtypeStruct((S5_SLABS, seg, 256, S5_SLAB_W), BF16),
            jax.ShapeDtypeStruct((S5_GROUPS, 2, 128), F32),
        ],
        name="s5_prep",
    )(par, b_t(b_re), b_t(b_im), c_t(c_re), c_t(c_im), jnp.asarray(spread, BF16), jnp.asarray(place, BF16))
    mat = (S5_SLABS, S5_SLAB_W, S5_SLAB_W)
    dj = jnp.tile(d_skip.astype(F32).reshape(S5_SLABS, 1, 128), (1, 1, S5_CHUNK))
    return wt.reshape(mat), web.reshape(mat), wca.reshape(mat), a16.reshape(1, S5_SLABS * S5_SLAB_W), dj


S5_STATE_COLS = S5_SLABS * S5_SLAB_W // 128
S5_SLAB_COLS = S5_SLAB_W // 128


def _s5_state_kernel(u_ref, w_ref, o_ref):
    s = _dot(u_ref[...].astype(BF16), w_ref[...])
    for k in range(S5_SLAB_COLS):
        o_ref[k] = s[:, k * 128:(k + 1) * 128]


def _s5_state(uj, web):
    return pl.pallas_call(
        _s5_state_kernel,
        grid=(S5_SLABS, S5_ROWS // S5_ROW_TILE),
        in_specs=[
            pl.BlockSpec((None, S5_ROW_TILE, S5_SLAB_W), lambda j, p: (j, p, 0)),
            pl.BlockSpec((None, S5_SLAB_W, S5_SLAB_W), lambda j, p: (j, 0, 0)),
        ],
        out_specs=pl.BlockSpec((S5_SLAB_COLS, S5_ROW_TILE, 128), lambda j, p: (j, p, 0)),
        out_shape=jax.ShapeDtypeStruct((S5_STATE_COLS, S5_ROWS, 128), F32),
        compiler_params=pltpu.CompilerParams(vmem_limit_bytes=VMEM_LIMIT),
        name="s5_state",
    )(uj, web)


S5_SCAN_COLS = 8


def _s5_scan_kernel(s_ref, a_ref, h0_ref, hin_ref, fin_ref, sg, hf, hb):
    ncol = S5_SCAN_COLS

    def scan(row0, nc, nb, h0):
        is_f = lax.broadcasted_iota(jnp.int32, (nb, 128), 1) < S5_STATE
        chunk_rows = lambda c: pl.ds(pl.multiple_of(row0 + c * nb, 8), nb)

        def gather(c, carry):
            for k in range(ncol):
                sg[k, chunk_rows(c), :] = s_ref[k, pl.ds(row0 + c, nb, stride=nc), :]
            return carry

        lax.fori_loop(0, nc, gather, 0)

        def body(c, hs):
            rf = chunk_rows(c)
            rb = chunk_rows(nc - 1 - c)
            new = []
            for m in range(ncol // 2):
                h_re, h_im = hs[2 * m], hs[2 * m + 1]
                a_re = a_ref[:, (2 * m) * 128:(2 * m + 1) * 128]
                a_im = a_ref[:, (2 * m + 1) * 128:(2 * m + 2) * 128]
                loc = []
                for k, h in ((2 * m, h_re), (2 * m + 1, h_im)):
                    hf[k, rf, :] = h
                    hb[k, rb, :] = h
                    loc.append(jnp.where(is_f, sg[k, rf, :], sg[k, rb, :]))
                new.append(a_re * h_re - a_im * h_im + loc[0])
                new.append(a_re * h_im + a_im * h_re + loc[1])
            return tuple(new)

        fin = lax.fori_loop(0, nc, body, h0)

        def scatter(c, carry):
            for k in range(ncol):
                hin_ref[k, pl.ds(row0 + c, nb, stride=nc), :] = jnp.where(
                    is_f, hf[k, chunk_rows(c), :], hb[k, chunk_rows(c), :])
            return carry

        lax.fori_loop(0, nc, scatter, 0)
        return fin

    fin = scan(0, SEQ // S5_CHUNK, BATCH, tuple(jnp.zeros((BATCH, 128), F32) for _ in range(ncol)))
    for k in range(ncol):
        fin_ref[:, k * 128:(k + 1) * 128] = fin[k]
    scan(S5_ROWS_C, DEC_SEQ // S5_CHUNK, DEC_BATCH,
         tuple(h0_ref[:, k * 128:(k + 1) * 128] for k in range(ncol)))


def _s5_scan(sloc, a16, h0l):
    ncol = S5_SCAN_COLS
    w = ncol * 128
    return pl.pallas_call(
        _s5_scan_kernel,
        grid=(S5_STATE_COLS // ncol,),
        in_specs=[
            pl.BlockSpec((ncol, S5_ROWS, 128), lambda k: (k, 0, 0)),
            pl.BlockSpec((1, w), lambda k: (0, k)),
            pl.BlockSpec((DEC_BATCH, w), lambda k: (0, k)),
        ],
        out_specs=[
            pl.BlockSpec((ncol, S5_ROWS, 128), lambda k: (k, 0, 0)),
            pl.BlockSpec((BATCH, w), lambda k: (0, k)),
        ],
        out_shape=[
            jax.ShapeDtypeStruct((S5_STATE_COLS, S5_ROWS, 128), F32),
            jax.ShapeDtypeStruct((BATCH, S5_STATE_COLS * 128), F32),
        ],
        scratch_shapes=[pltpu.VMEM((ncol, S5_ROWS, 128), F32)] * 3,
        name="s5_scan",
    )(sloc, a16, h0l)


def _s5_out_kernel(u_ref, hin_ref, wt_ref, wca_ref, d_ref, y_ref):
    u = u_ref[...]
    hin = jnp.concatenate([hin_ref[k] for k in range(S5_SLAB_COLS)], axis=1).astype(BF16)
    y = _dot(u.astype(BF16), wt_ref[...]) + _dot(hin, wca_ref[...]) + u * d_ref[...]
    for t in range(S5_CHUNK):
        y_ref[pl.ds(t, S5_ROW_TILE, stride=S5_CHUNK), :] = y[:, t * 128:(t + 1) * 128]


def _s5_out(uj, hin, wt, wca, dj):
    return pl.pallas_call(
        _s5_out_kernel,
        grid=(S5_SLABS, S5_ROWS // S5_ROW_TILE),
        in_specs=[
            pl.BlockSpec((None, S5_ROW_TILE, S5_SLAB_W), lambda j, p: (j, p, 0)),
            pl.BlockSpec((S5_SLAB_COLS, S5_ROW_TILE, 128), lambda j, p: (j, p, 0)),
            pl.BlockSpec((None, S5_SLAB_W, S5_SLAB_W), lambda j, p: (j, 0, 0)),
            pl.BlockSpec((None, S5_SLAB_W, S5_SLAB_W), lambda j, p: (j, 0, 0)),
            pl.BlockSpec((None, 1, S5_SLAB_W), lambda j, p: (j, 0, 0)),
        ],
        out_specs=pl.BlockSpec((None, S5_ROW_TILE * S5_CHUNK, 128), lambda j, p: (j, p, 0)),
        out_shape=jax.ShapeDtypeStruct((S5_SLABS, NTOK, 128), F32),
        compiler_params=pltpu.CompilerParams(vmem_limit_bytes=VMEM_LIMIT),
        name="s5_out",
    )(uj, hin, wt, wca, dj)


@functools.lru_cache(maxsize=None)
def _gla_consts():
    n = GLA_BLK
    nl = GLA_LEVELS
    r = np.arange(n)
    seg = np.zeros((nl + 2, n, n), np.float32)
    up = np.zeros((n, 128), np.int32)
    for l in range(nl):
        for row in range(n):
            if (row >> l) & 1:
                seg[l, row, (row >> l) << l:row + 1] = 1.0
            else:
                seg[l, row, row + 1:(row | ((1 << l) - 1)) + 1] = 1.0
        up[:, l] = (r >> l) & 1
    for row in range(n):
        seg[nl, row, :row + 1] = 1.0
        seg[nl + 1, row, row + 1:] = 1.0
    i = r[:, None]
    j = r[None, :]
    x = np.maximum(i ^ j, 1)
    lev = np.where(j < i, np.floor(np.log2(x)).astype(np.int32), np.where(i == j, nl, -1)).astype(np.int32)
    seg2 = np.stack([seg, seg[:, ::-1, ::-1]]).reshape(2, (nl + 2) * n, n)
    up2 = np.stack([up, up[::-1]])
    lev2 = np.stack([lev, lev[::-1, ::-1]])
    return seg2, up2, lev2


@functools.lru_cache(maxsize=None)
def _gla_tables():
    rowblk, seq, first, last = [], [], [], []
    for d in range(2):
        rb, sq, fi, la = [], [], [], []
        for s in range(BATCH + DEC_BATCH):
            nblk = 1 if s < BATCH else DEC_SEQ // GLA_BLK
            base = s if s < BATCH else NTOK_C // GLA_BLK + (s - BATCH) * nblk
            order = range(nblk) if d == 0 else range(nblk - 1, -1, -1)
            for pos, b in enumerate(order):
                rb.append(base + b)
                sq.append(s)
                fi.append(int(pos == 0))
                la.append(int(pos == nblk - 1))
        rowblk.append(rb); seq.append(sq); first.append(fi); last.append(la)
    as_np = lambda a: np.asarray(a, np.int32)
    return as_np(rowblk), as_np(seq), as_np(first), as_np(last)


def _gla_kernel(rowblk_ref, seq_ref, first_ref, last_ref,
                q_ref, k_ref, v_ref, lr_ref, wgk_ref, bgk_ref, seg_ref, up_ref, lev_ref, s0_ref,
                o_ref, fin_ref, x_scr, z_scr, st_scr):
    d = pl.program_id(0)
    n = pl.program_id(1)
    nl = GLA_LEVELS
    blk = GLA_BLK

    @pl.when(first_ref[d, n] == 1)
    def _():
        st_scr[...] = jnp.zeros_like(st_scr)
        for h in range(GLA_HEADS):
            st_scr[h * GLA_DK:(h + 1) * GLA_DK, h * GLA_DV:(h + 1) * GLA_DV] = s0_ref[h]

    q = q_ref[...] * (GLA_DK ** -0.5)
    k = k_ref[...]
    vb = v_ref[...].astype(BF16)
    x = _dot(lr_ref[...].astype(BF16), wgk_ref[...]) + bgk_ref[...]
    gk = (jnp.minimum(x, 0.0) - jnp.log1p(jnp.exp(-jnp.abs(x)))) * (1.0 / GLA_NORMALIZER)
    g_hi = gk.astype(BF16)
    g_lo = (gk - g_hi.astype(F32)).astype(BF16)
    x_scr[...] = _dot(seg_ref[...], g_hi) + _dot(seg_ref[...], g_lo)
    ones = jnp.ones((blk, 128), BF16)
    tot = _dot_tn(g_hi, ones) + _dot_tn(g_lo, ones)

    for l in range(nl):
        e = jnp.exp(x_scr[l * blk:(l + 1) * blk, :])
        z_scr[l] = (jnp.where(up_ref[:, l:l + 1] != 0, q, k) * e).astype(BF16)
    z_scr[nl] = q.astype(BF16)
    kb = k.astype(BF16)

    lev = lev_ref[...]
    lane = lax.broadcasted_iota(jnp.int32, (blk, GLA_QK), 1)
    for h in range(GLA_HEADS):
        in_head = (lane >= h * GLA_DK) & (lane < (h + 1) * GLA_DK)
        att = jnp.zeros((blk, blk), F32)
        for l in range(nl + 1):
            lhs = z_scr[l]
            rhs = jnp.where(in_head, kb if l == nl else lhs, jnp.zeros_like(lhs))
            att = jnp.where(lev == l, _dot_nt(lhs, rhs), att)
        o_ref[:, h * GLA_DV:(h + 1) * GLA_DV] = _dot(att.astype(BF16), vb[:, h * GLA_DV:(h + 1) * GLA_DV])

    st = st_scr[...]
    q_in = (q * jnp.exp(x_scr[nl * blk:(nl + 1) * blk, :])).astype(BF16)
    o_ref[...] += _dot(q_in, st.astype(BF16))
    k_out = (k * jnp.exp(x_scr[(nl + 1) * blk:(nl + 2) * blk, :])).astype(BF16)
    kv = _dot_tn(k_out, vb)
    row = lax.broadcasted_iota(jnp.int32, (GLA_QK, GLA_V), 0)
    col = lax.broadcasted_iota(jnp.int32, (GLA_QK, GLA_V), 1)
    same_head = (row >> 6) == (col >> 7)
    decay = jnp.exp(tot)
    decay = jnp.concatenate([decay] * GLA_HEADS, axis=1)
    st_new = decay * st + jnp.where(same_head, kv, 0.0)
    st_scr[...] = st_new

    @pl.when(last_ref[d, n] == 1)
    def _():
        for h in range(GLA_HEADS):
            fin_ref[h] = st_scr[h * GLA_DK:(h + 1) * GLA_DK, h * GLA_DV:(h + 1) * GLA_DV]


def _gla_mix(bslab, lr, wgk, bgk, s0):
    seg, up, lev = _gla_consts()
    rowblk, seq, first, last = _gla_tables()
    nsteps = rowblk.shape[1]
    nseq = BATCH + DEC_BATCH
    nl = GLA_LEVELS
    grid_spec = pltpu.PrefetchScalarGridSpec(
        num_scalar_prefetch=4,
        grid=(2, nsteps),
        in_specs=[
            pl.BlockSpec((GLA_BLK, GLA_QK), lambda d, n, rb, sq, fi, la: (rb[d, n], 0)),
            pl.BlockSpec((GLA_BLK, GLA_QK), lambda d, n, rb, sq, fi, la: (rb[d, n], 1)),
            pl.BlockSpec((GLA_BLK, GLA_V), lambda d, n, rb, sq, fi, la: (rb[d, n], 1)),
            pl.BlockSpec((GLA_BLK, 128), lambda d, n, rb, sq, fi, la: (rb[d, n], 0)),
            pl.BlockSpec((None, 128, GLA_QK), lambda d, n, rb, sq, fi, la: (d, 0, 0)),
            pl.BlockSpec((None, 1, GLA_QK), lambda d, n, rb, sq, fi, la: (d, 0, 0)),
            pl.BlockSpec((None, (nl + 2) * GLA_BLK, GLA_BLK), lambda d, n, rb, sq, fi, la: (d, 0, 0)),
            pl.BlockSpec((None, GLA_BLK, 128), lambda d, n, rb, sq, fi, la: (d, 0, 0)),
            pl.BlockSpec((None, GLA_BLK, GLA_BLK), lambda d, n, rb, sq, fi, la: (d, 0, 0)),
            pl.BlockSpec((None, None, GLA_HEADS, GLA_DK, GLA_DV),
                         lambda d, n, rb, sq, fi, la: (sq[d, n], d, 0, 0, 0)),
        ],
        out_specs=[
            pl.BlockSpec((None, GLA_BLK, GLA_V), lambda d, n, rb, sq, fi, la: (d, rb[d, n], 0)),
            pl.BlockSpec((None, None, GLA_HEADS, GLA_DK, GLA_DV),
                         lambda d, n, rb, sq, fi, la: (sq[d, n], d, 0, 0, 0)),
        ],
        scratch_shapes=[
            pltpu.VMEM(((nl + 2) * GLA_BLK, GLA_QK), F32),
            pltpu.VMEM((nl + 1, GLA_BLK, GLA_QK), BF16),
            pltpu.VMEM((GLA_QK, GLA_V), F32),
        ],
    )
    return pl.pallas_call(
        _gla_kernel,
        grid_spec=grid_spec,
        out_shape=[
            jax.ShapeDtypeStruct((2, NTOK, GLA_V), F32),
            jax.ShapeDtypeStruct((nseq, 2, GLA_HEADS, GLA_DK, GLA_DV), F32),
        ],
        compiler_params=pltpu.CompilerParams(vmem_limit_bytes=VMEM_LIMIT),
        name="gla_mix",
    )(jnp.asarray(rowblk), jnp.asarray(seq), jnp.asarray(first), jnp.asarray(last),
      bslab, bslab, bslab, lr, wgk, bgk,
      jnp.asarray(seg, BF16), jnp.asarray(up), jnp.asarray(lev), s0)


def _softmax_head(s_list, v_list, sink):
    m = sink
    for s in s_list:
        m = jnp.maximum(m, jnp.max(s, axis=-1, keepdims=True))
    den = jnp.exp(sink - m)
    acc = None
    for s, v in zip(s_list, v_list):
        p = jnp.exp(s - m)
        den = den + jnp.sum(p, axis=-1, keepdims=True)
        pv = _dot(p.astype(BF16), v)
        acc = pv if acc is None else acc + pv
    return acc / den


def _attn_ctx_kernel(sink_ref, q_ref, k_ref, v_ref, o_ref):
    k = k_ref[...]
    v = v_ref[...]
    ks = (k.astype(BF16), pltpu.roll(k, 64, 1).astype(BF16))
    vs = (v.astype(BF16), pltpu.roll(v, 64, 1).astype(BF16))
    lo = lax.broadcasted_iota(jnp.int32, (SEQ, 128), 1) < HEAD_DIM
    for t in range(ATT_HEADS // 2):
        qt = q_ref[:, t * 128:(t + 1) * 128] * (HEAD_DIM ** -0.5)
        kvh = t // 2
        outs = []
        for p in range(2):
            sel = lo if p == 0 else jnp.logical_not(lo)
            qm = jnp.where(sel, qt, 0.0).astype(BF16)
            which = 0 if p == kvh else 1
            s = _dot_nt(qm, ks[which])
            outs.append(_softmax_head([s], [vs[which]], sink_ref[2 * t + p]))
        o_ref[:, t * 128:(t + 1) * 128] = jnp.where(lo, outs[0], outs[1])


def _attn_ctx(sink, cslab):
    return pl.pallas_call(
        _attn_ctx_kernel,
        grid=(BATCH,),
        in_specs=[
            pl.BlockSpec(memory_space=pltpu.SMEM),
            pl.BlockSpec((SEQ, ATT_Q), lambda b: (b, 0)),
            pl.BlockSpec((SEQ, ATT_KV), lambda b: (b, 4)),
            pl.BlockSpec((SEQ, ATT_KV), lambda b: (b, 5)),
        ],
        out_specs=pl.BlockSpec((SEQ, ATT_Q), lambda b: (b, 0)),
        out_shape=jax.ShapeDtypeStruct((NTOK, ATT_Q), F32),
        name="attn_ctx",
    )(sink, cslab, cslab, cslab)


def _attn_lat_kernel(sink_ref, q_ref, kp_ref, kc_ref, kn_ref, vp_ref, vc_ref, vn_ref,
                     ck_ref, cv_ref, cos_ref, sin_ref, prev_ref, o_ref):
    del prev_ref
    j = pl.program_id(1)
    nb = DEC_SEQ // ATT_BLOCK
    lane = lax.broadcasted_iota(jnp.int32, (ATT_BLOCK, 128), 1)
    lo = lane < HEAD_DIM
    first16 = (lane & 31) < 16

    def rope(x, blk_idx):
        r0 = pl.multiple_of(blk_idx * ATT_BLOCK, ATT_BLOCK)
        c = cos_ref[pl.ds(r0, ATT_BLOCK), :]
        s = sin_ref[pl.ds(r0, ATT_BLOCK), :]
        xs = jnp.where(first16, pltpu.roll(x, 112, 1), pltpu.roll(x, 16, 1))
        return x * c + xs * s

    nwin = 3 * ATT_BLOCK
    keys = jnp.concatenate([rope(kp_ref[...], jnp.maximum(j - 1, 0)), rope(kc_ref[...], j),
                            rope(kn_ref[...], jnp.minimum(j + 1, nb - 1)), ck_ref[...]], axis=0)
    vals = jnp.concatenate([vp_ref[...], vc_ref[...], vn_ref[...], cv_ref[...]], axis=0)
    keys2 = (keys.astype(BF16), pltpu.roll(keys, 64, 1).astype(BF16))
    vals2 = (vals.astype(BF16), pltpu.roll(vals, 64, 1).astype(BF16))
    nkeys = nwin + PAST_LEN
    qi = lax.broadcasted_iota(jnp.int32, (2 * ATT_BLOCK, nkeys), 0) & (ATT_BLOCK - 1)
    kc = lax.broadcasted_iota(jnp.int32, (2 * ATT_BLOCK, nkeys), 1)
    valid = (jnp.abs(kc - ATT_BLOCK - qi) <= WINDOW) | (kc >= nwin)
    valid = valid & ((j > 0) | (kc >= ATT_BLOCK)) & ((j < nb - 1) | (kc < 2 * ATT_BLOCK) | (kc >= nwin))
    top = lax.broadcasted_iota(jnp.int32, (2 * ATT_BLOCK, 1), 0) < ATT_BLOCK
    q_tiles = [rope(q_ref[:, t * 128:(t + 1) * 128], j) * (HEAD_DIM ** -0.5) for t in range(ATT_HEADS // 2)]
    for kvh in range(ATT_KV_HEADS):
        q2 = jnp.concatenate(q_tiles[2 * kvh:2 * kvh + 2], axis=0)
        lo2 = jnp.concatenate([lo, lo], axis=0)
        outs = []
        for p in range(2):
            qm = jnp.where(lo2 if p == 0 else jnp.logical_not(lo2), q2, 0.0).astype(BF16)
            which = 0 if p == kvh else 1
            sink = jnp.where(top, sink_ref[4 * kvh + p], sink_ref[4 * kvh + 2 + p])
            s = jnp.where(valid, _dot_nt(qm, keys2[which]), -1e30)
            outs.append(_softmax_head([s], [vals2[which]], sink))
        o2 = jnp.where(lo2, outs[0], outs[1])
        for i in range(2):
            t = 2 * kvh + i
            o_ref[:, t * 128:(t + 1) * 128] = o2[i * ATT_BLOCK:(i + 1) * ATT_BLOCK]


def _attn_lat(sink, cslab, ck, cv, cos_t, sin_t, y_prev):
    nb = DEC_SEQ // ATT_BLOCK
    base = NTOK_C // ATT_BLOCK
    cur = lambda b, j: base + b * nb + j
    prv = lambda b, j: base + b * nb + jnp.maximum(j - 1, 0)
    nxt = lambda b, j: base + b * nb + jnp.minimum(j + 1, nb - 1)
    kv_spec = lambda row, col: pl.BlockSpec((ATT_BLOCK, ATT_KV), lambda b, j: (row(b, j), col))
    return pl.pallas_call(
        _attn_lat_kernel,
        grid=(DEC_BATCH, nb),
        in_specs=[
            pl.BlockSpec(memory_space=pltpu.SMEM),
            pl.BlockSpec((ATT_BLOCK, ATT_Q), lambda b, j: (cur(b, j), 0)),
            kv_spec(prv, 4), kv_spec(cur, 4), kv_spec(nxt, 4),
            kv_spec(prv, 5), kv_spec(cur, 5), kv_spec(nxt, 5),
            pl.BlockSpec((None, PAST_LEN, ATT_KV), lambda b, j: (b, 0, 0)),
            pl.BlockSpec((None, PAST_LEN, ATT_KV), lambda b, j: (b, 0, 0)),
            pl.BlockSpec((DEC_SEQ, 128), lambda b, j: (0, 0)),
            pl.BlockSpec((DEC_SEQ, 128), lambda b, j: (0, 0)),
            pl.BlockSpec(memory_space=pl.ANY),
        ],
        out_specs=pl.BlockSpec((ATT_BLOCK, ATT_Q), lambda b, j: (cur(b, j), 0)),
        out_shape=jax.ShapeDtypeStruct((NTOK, ATT_Q), F32),
        input_output_aliases={12: 0},
        name="attn_lat",
    )(sink, cslab, cslab, cslab, cslab, cslab, cslab, cslab, ck, cv, cos_t, sin_t, y_prev)


def _rope_tables():
    rows = DEC_SEQ // GRID_W
    row = np.repeat(np.arange(rows, dtype=np.float32), GRID_W)
    col = np.tile(np.arange(GRID_W, dtype=np.float32), rows)
    quarter = HEAD_DIM // 4
    inv = jnp.asarray(ROPE_BASE, F32) ** (-jnp.arange(quarter, dtype=F32) / quarter)
    lane = np.arange(128)
    use_row = (lane % HEAD_DIM) < HEAD_DIM // 2
    pos = jnp.where(use_row[None, :], jnp.asarray(row)[:, None], jnp.asarray(col)[:, None])
    ang = pos * inv[lane % quarter][None, :]
    sign = np.where((lane % 32) < 16, -1.0, 1.0).astype(np.float32)
    return jnp.cos(ang), jnp.sin(ang) * sign[None, :]


def _merge_kernel(*refs, split_x):
    if split_x:
        xc_ref, xl_ref, *refs = refs
    else:
        xc_ref, *refs = refs
    mod_ref, g_ref, ys5_ref, ogf_ref, ogb_ref, gb_ref, yc_ref, gate_ref, gng_ref, wglu_ref, wbr_ref, wout_ref, o_ref = refs
    if split_x:
        x = jnp.where(pl.program_id(0) < NTOK_C // TM, xc_ref[...], xl_ref[...])
    else:
        x = xc_ref[...]
    y = jnp.concatenate([ys5_ref[j] for j in range(S5_SLABS)], axis=1)
    y = 0.5 * y * (1.0 + jnp.tanh(math.sqrt(2.0 / math.pi) * (y + 0.044715 * (y * y * y))))
    ag = _dot(y.astype(BF16), wglu_ref[...])
    y_a = ag[:, :S5_WIDTH] * _sigmoid(ag[:, S5_WIDTH:])
    gng = gng_ref[...]
    gb = gb_ref[...]
    parts = []
    for h in range(GLA_HEADS):
        sl = slice(h * GLA_DV, (h + 1) * GLA_DV)
        o = ogf_ref[:, sl] + ogb_ref[:, sl]
        g = gb[:, sl]
        parts.append(_rms(o, gng) * (g * _sigmoid(g)))
    y_b = jnp.concatenate(parts, axis=1)
    merged = None
    for n, yn in enumerate((y_a, y_b, yc_ref[...])):
        proj = _dot(yn.astype(BF16), wbr_ref[n])
        term = _sigmoid(gate_ref[:, n * D_MODEL:(n + 1) * D_MODEL]) * proj
        merged = term if merged is None else merged + term
    mixed = _dot(merged.astype(BF16), wout_ref[...])
    g1 = mod_ref[:, 2 * D_MODEL:3 * D_MODEL]
    o_ref[...] = x + g1 * _rms(mixed, g_ref[...])


def _layer_spec(shape, layer):
    return pl.BlockSpec((None,) + shape, lambda i: (layer,) + (0,) * len(shape), pipeline_mode=pl.Buffered(1))


def _split_token_specs(n_arrays):
    nct = NTOK_C // TM
    if n_arrays == 2:
        return [pl.BlockSpec((TM, D_MODEL), lambda i: (jnp.minimum(i, nct - 1), 0)),
                pl.BlockSpec((TM, D_MODEL), lambda i: (jnp.maximum(i - nct, 0), 0))]
    return [pl.BlockSpec((TM, D_MODEL), lambda i: (i, 0))]


def _merge(xs, mod, g, ys5, og, bslab, yc, gates, gng, wglu, wbr, wout, layer):
    tok = lambda width, col=0: pl.BlockSpec((TM, width), lambda i: (i, col))
    full = lambda shape: _layer_spec(shape, layer)
    return pl.pallas_call(
        functools.partial(_merge_kernel, split_x=len(xs) == 2),
        grid=(NTOK // TM,),
        in_specs=_split_token_specs(len(xs)) + [
            pl.BlockSpec((None, 1, 6 * D_MODEL), lambda i: (_mod_row(i), 0, 0)),
            pl.BlockSpec((1, D_MODEL), lambda i: (0, 0)),
            pl.BlockSpec((S5_SLABS, TM, 128), lambda i: (0, i, 0)),
            pl.BlockSpec((None, TM, GLA_V), lambda i: (0, i, 0)),
            pl.BlockSpec((None, TM, GLA_V), lambda i: (1, i, 0)),
            tok(GLA_V, 2),
            tok(ATT_Q),
            tok(N_BRANCH * D_MODEL),
            pl.BlockSpec((1, GLA_DV), lambda i: (0, 0)),
            full((S5_WIDTH, 2 * S5_WIDTH)),
            full((N_BRANCH, BRANCH_W, D_MODEL)),
            full((D_MODEL, D_MODEL)),
        ],
        out_specs=tok(D_MODEL),
        out_shape=jax.ShapeDtypeStruct((NTOK, D_MODEL), F32),
        compiler_params=pltpu.CompilerParams(vmem_limit_bytes=VMEM_LIMIT),
        name="merge",
    )(*xs, mod, g, ys5, og, og, bslab, yc, gates, gng, wglu, wbr, wout)


FFN_SPLIT = 2


def _ffn_kernel(x_ref, mod_ref, gin_ref, gout_ref, w1_ref, w2_ref, *o_refs):
    x = x_ref[...]
    sh = mod_ref[:, 3 * D_MODEL:4 * D_MODEL]
    sc = mod_ref[:, 4 * D_MODEL:5 * D_MODEL]
    g2 = mod_ref[:, 5 * D_MODEL:6 * D_MODEL]
    h = (_rms(x, gin_ref[...]) * (1.0 + sc) + sh).astype(BF16)
    ck = FFN_HIDDEN // FFN_SPLIT
    acc = None
    for c in range(FFN_SPLIT):
        a = _dot(h, w1_ref[:, c * ck:(c + 1) * ck])
        b = _dot(h, w1_ref[:, FFN_HIDDEN + c * ck:FFN_HIDDEN + (c + 1) * ck])
        act = (a * _sigmoid(a) * b).astype(BF16)
        part = _dot(act, w2_ref[c * ck:(c + 1) * ck, :])
        acc = part if acc is None else acc + part
    y = x + g2 * _rms(acc, gout_ref[...])
    if len(o_refs) == 1:
        o_refs[0][...] = y
    else:
        is_ctx = pl.program_id(0) < NTOK_C // TM

        @pl.when(is_ctx)
        def _():
            o_refs[0][...] = y

        @pl.when(jnp.logical_not(is_ctx))
        def _():
            o_refs[1][...] = y


def _ffn(x, mod, gin, gout, w1, w2, layer, split_out):
    small = lambda shape: pl.BlockSpec(shape, lambda i: (0,) * len(shape))
    nct = NTOK_C // TM
    if split_out:
        out_specs = [pl.BlockSpec((TM, D_MODEL), lambda i: (jnp.minimum(i, nct - 1), 0)),
                     pl.BlockSpec((TM, D_MODEL), lambda i: (jnp.maximum(i - nct, 0), 0))]
        out_shape = [jax.ShapeDtypeStruct((NTOK_C, D_MODEL), F32), jax.ShapeDtypeStruct((NTOK_L, D_MODEL), F32)]
    else:
        out_specs = pl.BlockSpec((TM, D_MODEL), lambda i: (i, 0))
        out_shape = jax.ShapeDtypeStruct((NTOK, D_MODEL), F32)
    return pl.pallas_call(
        _ffn_kernel,
        grid=(NTOK // TM,),
        in_specs=[
            pl.BlockSpec((TM, D_MODEL), lambda i: (i, 0)),
            pl.BlockSpec((None, 1, 6 * D_MODEL), lambda i: (_mod_row(i), 0, 0)),
            small((1, D_MODEL)),
            small((1, D_MODEL)),
            _layer_spec((D_MODEL, 2 * FFN_HIDDEN), layer),
            _layer_spec((FFN_HIDDEN, D_MODEL), layer),
        ],
        out_specs=out_specs,
        out_shape=out_shape,
        compiler_params=pltpu.CompilerParams(vmem_limit_bytes=VMEM_LIMIT),
        name="ffn",
    )(x, mod, gin, gout, w1, w2)


def kernel(x_prompt, x_sample, cache_k, cache_v, state_s5, state_gla, c, c_ctx, w_mod, b_mod, norm_g, w_in,
           s5_lam_re, s5_lam_im, s5_log_step, s5_b_re, s5_b_im, s5_c_re, s5_c_im, s5_d, w_glu, gla_w_gk,
           gla_b_gk, gla_norm_g, att_sink, w_branch, w_out, w_ffn_in, w_ffn_out):
    cond = jnp.concatenate([c_ctx[None, :], c, jnp.zeros((N_MOD_ROWS - 1 - DEC_BATCH, D_MODEL), F32)], axis=0)
    mod_all = _modulation(cond, w_mod, b_mod).reshape(DEPTH, N_MOD_ROWS, 1, 6 * D_MODEL)
    cos_t, sin_t = _rope_tables()
    xs = (x_prompt.reshape(NTOK_C, D_MODEL), x_sample.reshape(NTOK_L, D_MODEL))
    w_in_b = jnp.pad(w_in.astype(BF16), ((0, 0), (0, 0), (0, W_IN_COLS - w_in.shape[-1])))
    w_glu_b, w_branch_b, w_out_b = w_glu.astype(BF16), w_branch.astype(BF16), w_out.astype(BF16)
    w_ffn_in_b, w_ffn_out_b = w_ffn_in.astype(BF16), w_ffn_out.astype(BF16)
    new_k, new_v, new_s5, new_gla = [], [], [], []
    for i in range(DEPTH):
        mod = mod_all[i]
        uj, bslab, cslab, gates, lr = _inproj(xs, mod, norm_g[i, 0][None, :], w_in_b, i)

        wt, web, wca, a16, dj = _s5_prep(s5_lam_re[i], s5_lam_im[i], s5_log_step[i], s5_b_re[i], s5_b_im[i],
                                         s5_c_re[i], s5_c_im[i], s5_d[i])
        h0l = state_s5[:, i].astype(F32).transpose(0, 2, 4, 1, 3).reshape(DEC_BATCH, S5_GROUPS * 256)
        hin, finc = _s5_scan(_s5_state(uj, web), a16, h0l)
        ys5 = _s5_out(uj, hin, wt, wca, dj)
        new_s5.append(finc.reshape(BATCH, S5_GROUPS, 2, 2, S5_STATE).transpose(0, 3, 1, 4, 2))

        wgk = jnp.zeros((2, 128, GLA_QK), F32)
        wgk = wgk.at[0, 0:GLA_RANK].set(gla_w_gk[i, 0]).at[1, GLA_RANK:2 * GLA_RANK].set(gla_w_gk[i, 1])
        s0 = jnp.concatenate([jnp.zeros((BATCH, 2, GLA_HEADS, GLA_DK, GLA_DV), F32),
                              state_gla[:, i].astype(F32)], axis=0)
        og, gla_fin = _gla_mix(bslab, lr, wgk.astype(BF16), gla_b_gk[i][:, None, :].astype(F32), s0)
        new_gla.append(gla_fin[:BATCH])

        sink = att_sink[i].astype(F32)
        yc = _attn_ctx(sink, cslab)
        yc = _attn_lat(sink, cslab, cache_k[:, i].reshape(DEC_BATCH, PAST_LEN, ATT_KV).astype(F32),
                       cache_v[:, i].reshape(DEC_BATCH, PAST_LEN, ATT_KV).astype(F32), cos_t, sin_t, yc)
        new_k.append(cslab[:NTOK_C, ATT_Q:ATT_Q + ATT_KV].reshape(BATCH, SEQ, ATT_KV_HEADS, HEAD_DIM))
        new_v.append(cslab[:NTOK_C, ATT_Q + ATT_KV:].reshape(BATCH, SEQ, ATT_KV_HEADS, HEAD_DIM))

        x = _merge(xs, mod, norm_g[i, 1][None, :], ys5, og, bslab, yc, gates, gla_norm_g[i][None, :],
                   w_glu_b, w_branch_b, w_out_b, i)
        last = i == DEPTH - 1
        x = _ffn(x, mod, norm_g[i, 2][None, :], norm_g[i, 3][None, :], w_ffn_in_b, w_ffn_out_b, i, last)
        xs = tuple(x) if last else (x,)

    return (xs[0].reshape(BATCH, SEQ, D_MODEL), xs[1].reshape(DEC_BATCH, DEC_SEQ, D_MODEL),
            jnp.stack(new_k, axis=1), jnp.stack(new_v, axis=1),
            jnp.stack(new_s5, axis=1), jnp.stack(new_gla, axis=1))
```

```python
import functools
import math

import numpy as np
import jax
import jax.numpy as jnp
from jax import lax
from jax.experimental import pallas as pl
from jax.experimental.pallas import tpu as pltpu

F32 = jnp.float32
BF16 = jnp.bfloat16

D_MODEL = 1024
BATCH = 16
SEQ = 256
DEPTH = 2
DEC_BATCH = 8
DEC_SEQ = 1024
PAST_LEN = 256
GRID_W = 64
ROPE_BASE = 10000.0
S5_WIDTH = 512
S5_GROUP = 16
S5_GROUPS = 32
S5_STATE = 64
GLA_HEADS = 4
GLA_DK = 64
GLA_DV = 128
GLA_QK = 256
GLA_V = 512
GLA_RANK = 16
GLA_NORMALIZER = 16.0
ATT_HEADS = 8
ATT_KV_HEADS = 2
HEAD_DIM = 64
ATT_Q = 512
ATT_KV = 128
WINDOW = 128
ATT_BLOCK = 128
N_BRANCH = 3
BRANCH_W = 512
FFN_HIDDEN = 2816
RMS_EPS = 1e-6

NTOK_C = BATCH * SEQ
NTOK_L = DEC_BATCH * DEC_SEQ
NTOK = NTOK_C + NTOK_L
TM = 512
N_MOD_ROWS = 16

W_IN_COLS = 6016
S5_CHUNK = 16
S5_SLABS = S5_WIDTH // 128
S5_SLAB_W = S5_CHUNK * 128
S5_ROWS_C = NTOK_C // S5_CHUNK
S5_ROWS = NTOK // S5_CHUNK
S5_ROW_TILE = 256
GLA_BLK = 256
GLA_LEVELS = 8
VMEM_LIMIT = 56 * 1024 * 1024


def _dot(a, b):
    return jnp.dot(a, b, preferred_element_type=F32)


def _dot_nt(a, b):
    return lax.dot_general(a, b, (((1,), (1,)), ((), ())), preferred_element_type=F32)


def _dot_tn(a, b):
    return lax.dot_general(a, b, (((0,), (0,)), ((), ())), preferred_element_type=F32)


def _rms(x, g):
    return x * lax.rsqrt(jnp.mean(x * x, axis=-1, keepdims=True) + RMS_EPS) * g


def _sigmoid(x):
    return 1.0 / (1.0 + jnp.exp(-x))


def _mod_row(i):
    nct = NTOK_C // TM
    return jnp.where(i < nct, 0, 1 + (i - nct) // (DEC_SEQ // TM))


def _mod_kernel(c_ref, w_ref, b_ref, o_ref):
    c = c_ref[...]
    s = (c * _sigmoid(c)).astype(BF16)
    o_ref[...] = _dot(s, w_ref[...].astype(BF16)) + b_ref[...]


def _modulation(cond, w_mod, b_mod):
    tn = 1024
    return pl.pallas_call(
        _mod_kernel,
        grid=(DEPTH, 6 * D_MODEL // tn),
        in_specs=[
            pl.BlockSpec((N_MOD_ROWS, D_MODEL), lambda l, n: (0, 0)),
            pl.BlockSpec((None, D_MODEL, tn), lambda l, n: (l, 0, n)),
            pl.BlockSpec((None, 1, tn), lambda l, n: (l, 0, n)),
        ],
        out_specs=pl.BlockSpec((None, N_MOD_ROWS, tn), lambda l, n: (l, 0, n)),
        out_shape=jax.ShapeDtypeStruct((DEPTH, N_MOD_ROWS, 6 * D_MODEL), F32),
        name="modulation",
    )(cond, w_mod, b_mod.reshape(DEPTH, 1, 6 * D_MODEL))


_IN_SLABS = ((0, 512), (512, 1536), (2048, 768), (2816, 3072), (5888, 128))
W_IN_SPLIT = 2048
W_IN_GAP = 32
W_IN_TAIL = W_IN_COLS - W_IN_SPLIT


def _inproj_kernel(*refs, split_x):
    if split_x:
        xc_ref, xl_ref, mod_ref, g_ref, w_ref, u_ref, b_ref, c_ref, gate_ref, lr_ref, w_tail, u_stage = refs
    else:
        xc_ref, mod_ref, g_ref, w_ref, u_ref, b_ref, c_ref, gate_ref, lr_ref, w_tail, u_stage = refs
    i = pl.program_id(0)

    @pl.when(i == 0)
    def _():
        r = lax.broadcasted_iota(jnp.int32, (256, 128), 0)
        c = lax.broadcasted_iota(jnp.int32, (256, 128), 1)
        shift = (r == c + W_IN_GAP).astype(BF16)
        head = ((r == c) & (c < W_IN_GAP)).astype(BF16)
        ntile = (W_IN_TAIL - 128) // 128
        for t in range(ntile):
            src = W_IN_SPLIT + 128 * t
            w_tail[:, 128 * t:128 * (t + 1)] = _dot(w_ref[:, src:src + 256], shift).astype(BF16)
        w_tail[:, 128 * ntile:] = _dot(w_ref[:, W_IN_SPLIT:W_IN_SPLIT + 256], head).astype(BF16)

    if split_x:
        x = jnp.where(i < NTOK_C // TM, xc_ref[...], xl_ref[...])
    else:
        x = xc_ref[...]
    mod = mod_ref[...]
    h = _rms(x, g_ref[...]) * (1.0 + mod[:, D_MODEL:2 * D_MODEL]) + mod[:, 0:D_MODEL]
    h = h.astype(BF16)
    for j in range(S5_SLABS):
        u_stage[...] = _dot(h, w_ref[:, j * 128:(j + 1) * 128])
        for s in range(S5_CHUNK):
            u_ref[j, :, s * 128:(s + 1) * 128] = u_stage[pl.ds(s, TM // S5_CHUNK, stride=S5_CHUNK), :]
    b_ref[...] = _dot(h, w_ref[:, 512:W_IN_SPLIT])
    for (off, width), o_ref in zip(_IN_SLABS[2:], (c_ref, gate_ref, lr_ref)):
        o_ref[...] = _dot(h, w_tail[:, off - W_IN_SPLIT:off - W_IN_SPLIT + width])


def _inproj(xs, mod, g, w_all, layer):
    return pl.pallas_call(
        functools.partial(_inproj_kernel, split_x=len(xs) == 2),
        grid=(NTOK // TM,),
        in_specs=_split_token_specs(len(xs)) + [
            pl.BlockSpec((None, 1, 6 * D_MODEL), lambda i: (_mod_row(i), 0, 0)),
            pl.BlockSpec((1, D_MODEL), lambda i: (0, 0)),
            pl.BlockSpec((None, D_MODEL, W_IN_COLS), lambda i: (layer, 0, 0), pipeline_mode=pl.Buffered(1)),
        ],
        out_specs=[pl.BlockSpec((S5_SLABS, TM // S5_CHUNK, S5_SLAB_W), lambda i: (0, i, 0))]
        + [pl.BlockSpec((TM, width), lambda i: (i, 0)) for _, width in _IN_SLABS[1:]],
        out_shape=[jax.ShapeDtypeStruct((S5_SLABS, S5_ROWS, S5_SLAB_W), F32)]
        + [jax.ShapeDtypeStruct((NTOK, width), F32) for _, width in _IN_SLABS[1:]],
        scratch_shapes=[pltpu.VMEM((D_MODEL, W_IN_TAIL), BF16), pltpu.VMEM((TM, 128), F32)],
        compiler_params=pltpu.CompilerParams(vmem_limit_bytes=VMEM_LIMIT),
        name="inproj",
    )(*xs, mod, g, w_all)


@functools.lru_cache(maxsize=None)
def _s5_expanders():
    seg = 8
    spread = np.zeros((seg, 256, S5_SLAB_W), np.float32)
    place = np.zeros((seg, 256, S5_SLAB_W), np.float32)
    col = np.arange(256)
    for gl in range(seg):
        spread[gl, col, (col // S5_GROUP) * 128 + gl * S5_GROUP + col % S5_GROUP] = 1.0
        place[gl, col, gl * 256 + col] = 1.0
    return spread, place


def _s5_prep_kernel(par_ref, bre_ref, bim_ref, cre_ref, cim_ref, spread_ref, place_ref,
                    wt_ref, web_ref, wca_ref, a16_ref):
    n = S5_CHUNK
    lam_re = par_ref[0:1, :]
    lam_im = par_ref[1:2, :]
    dt = jnp.exp(par_ref[2:3, :])
    lr = lam_re * dt
    li = lam_im * dt
    krow = lax.broadcasted_iota(jnp.int32, (24, 128), 0).astype(F32)
    tab_mag = jnp.exp(krow * lr)
    tab_re = tab_mag * jnp.cos(krow * li)
    tab_im = tab_mag * jnp.sin(krow * li)
    ar = tab_re[1:2, :]
    ai = tab_im[1:2, :]
    nr = ar - 1.0
    den = lam_re * lam_re + lam_im * lam_im
    fr = (nr * lam_re + ai * lam_im) / den
    fi = (ai * lam_re - nr * lam_im) / den
    b_re = bre_ref[...]
    b_im = bim_ref[...]
    br = fr * b_re - fi * b_im
    bi = fr * b_im + fi * b_re
    c_re = cre_ref[...]
    c_im = cim_ref[...]

    def lo_half(shape):
        return lax.broadcasted_iota(jnp.int32, shape, 1) < S5_STATE

    def tile_rows(a):
        return jnp.concatenate([a] * n, axis=0)

    fwd16 = lo_half((S5_GROUP, 128))

    def powers(t_re, t_im, k_fwd, k_bwd):
        def pick(t, b):
            kf, kb = k_fwd(b), k_bwd(b)
            return jnp.where(fwd16, jnp.broadcast_to(t[kf:kf + 1, :], (S5_GROUP, 128)),
                             jnp.broadcast_to(t[kb:kb + 1, :], (S5_GROUP, 128)))
        return (jnp.concatenate([pick(t_re, b) for b in range(n)], axis=0),
                jnp.concatenate([pick(t_im, b) for b in range(n)], axis=0))

    fwd = lo_half((n * S5_GROUP, 128))
    brt, bit, crt, cit = tile_rows(br), tile_rows(bi), tile_rows(c_re), tile_rows(c_im)

    per, pei = powers(tab_re, tab_im, lambda s: n - 1 - s, lambda s: s)
    eb = jnp.concatenate([brt * per - bit * pei, brt * pei + bit * per], axis=1)
    pcr, pci = powers(tab_re, tab_im, lambda t: t + 1, lambda t: n - t)
    ca = jnp.concatenate([(crt * pcr - cit * pci).T, (-(crt * pci + cit * pcr)).T], axis=0)

    def one_dir(x, d):
        sw = pltpu.roll(x, S5_STATE, 1)
        lo = lo_half(x.shape)
        return jnp.where(lo, x, sw) if d == 0 else jnp.where(lo, sw, x)

    klag = []
    for d in range(2):
        lhs = jnp.where(lo_half(br.shape), one_dir(br, d), -one_dir(bi, d))
        crd, cid = tile_rows(one_dir(c_re, d)), tile_rows(one_dir(c_im, d))
        lag = (lambda b: b) if d == 0 else (lambda b: n - 1 - b)
        pr, pi = powers(one_dir(tab_re, d), one_dir(tab_im, d), lag, lag)
        rhs_t = jnp.where(fwd, crd * pr - cid * pi, crd * pi + cid * pr)
        klag.append(lax.dot_general(lhs, rhs_t, (((1,), (1,)), ((), ())),
                                    precision=lax.Precision.HIGHEST, preferred_element_type=F32))
    lane = lax.broadcasted_iota(jnp.int32, (S5_GROUP, n * S5_GROUP), 1)
    rows = []
    for s in range(n):
        f = klag[0] if s == 0 else jnp.where(lane >= S5_GROUP * s, pltpu.roll(klag[0], S5_GROUP * s, 1), 0.0)
        sh = (n * S5_GROUP - S5_GROUP * (n - 1 - s)) % (n * S5_GROUP)
        b = klag[1] if sh == 0 else pltpu.roll(klag[1], sh, 1)
        rows.append(f + jnp.where(lane < S5_GROUP * (s + 1), b, 0.0))
    toep = jnp.concatenate(rows, axis=0)

    spread = spread_ref[...]
    wt_ref[...] = _dot(toep.astype(BF16), spread).astype(BF16).reshape(n, S5_GROUP, S5_SLAB_W)
    web_ref[...] = _dot(eb.astype(BF16), place_ref[...]).astype(BF16).reshape(n, S5_GROUP, S5_SLAB_W)
    wca_ref[...] = _dot(ca.astype(BF16), spread).astype(BF16)
    a16_ref[0:1, :] = tab_re[n:n + 1, :]
    a16_ref[1:2, :] = tab_im[n:n + 1, :]


def _s5_prep(lam_re, lam_im, log_step, b_re, b_im, c_re, c_im, d_skip):
    seg = 8
    par = jnp.stack([lam_re, lam_im, log_step]).astype(F32).transpose(2, 0, 1, 3).reshape(S5_GROUPS, 3, 128)
    par = jnp.concatenate([par, jnp.zeros((S5_GROUPS, 5, 128), F32)], axis=1)
    b_t = lambda b: b.astype(F32).transpose(1, 3, 0, 2).reshape(S5_GROUPS, S5_GROUP, 128)
    c_t = lambda c: c.astype(F32).transpose(1, 2, 0, 3).reshape(S5_GROUPS, S5_GROUP, 128)
    spread, place = _s5_expanders()
    vec = pl.BlockSpec((None, S5_GROUP, 128), lambda g: (g, 0, 0))
    exp_spec = pl.BlockSpec((None, 256, S5_SLAB_W), lambda g: (g % seg, 0, 0))
    rows_spec = pl.BlockSpec((None, S5_CHUNK, None, S5_GROUP, S5_SLAB_W), lambda g: (g // seg, 0, g % seg, 0, 0))
    wt, web, wca, a16 = pl.pallas_call(
        _s5_prep_kernel,
        grid=(S5_GROUPS,),
        in_specs=[pl.BlockSpec((None, 8, 128), lambda g: (g, 0, 0)), vec, vec, vec, vec, exp_spec, exp_spec],
        out_specs=[
            rows_spec, rows_spec,
            pl.BlockSpec((None, None, 256, S5_SLAB_W), lambda g: (g // seg, g % seg, 0, 0)),
            pl.BlockSpec((None, 2, 128), lambda g: (g, 0, 0)),
        ],
        out_shape=[
            jax.ShapeDtypeStruct((S5_SLABS, S5_CHUNK, seg, S5_GROUP, S5_SLAB_W), BF16),
            jax.ShapeDtypeStruct((S5_SLABS, S5_CHUNK, seg, S5_GROUP, S5_SLAB_W), BF16),
            jax.ShapeDtypeStruct((S5_SLABS, seg, 256, S5_SLAB_W), BF16),
            jax.ShapeDtypeStruct((S5_GROUPS, 2, 128), F32),
        ],
        name="s5_prep",
    )(par, b_t(b_re), b_t(b_im), c_t(c_re), c_t(c_im), jnp.asarray(spread, BF16), jnp.asarray(place, BF16))
    mat = (S5_SLABS, S5_SLAB_W, S5_SLAB_W)
    dj = jnp.tile(d_skip.astype(F32).reshape(S5_SLABS, 1, 128), (1, 1, S5_CHUNK))
    return wt.reshape(mat), web.reshape(mat), wca.reshape(mat), a16.reshape(1, S5_SLABS * S5_SLAB_W), dj


S5_STATE_COLS = S5_SLABS * S5_SLAB_W // 128
S5_SLAB_COLS = S5_SLAB_W // 128


def _s5_state_kernel(u_ref, w_ref, o_ref):
    s = _dot(u_ref[...].astype(BF16), w_ref[...])
    for k in range(S5_SLAB_COLS):
        o_ref[k] = s[:, k * 128:(k + 1) * 128]


def _s5_state(uj, web):
    return pl.pallas_call(
        _s5_state_kernel,
        grid=(S5_SLABS, S5_ROWS // S5_ROW_TILE),
        in_specs=[
            pl.BlockSpec((None, S5_ROW_TILE, S5_SLAB_W), lambda j, p: (j, p, 0)),
            pl.BlockSpec((None, S5_SLAB_W, S5_SLAB_W), lambda j, p: (j, 0, 0)),
        ],
        out_specs=pl.BlockSpec((S5_SLAB_COLS, S5_ROW_TILE, 128), lambda j, p: (j, p, 0)),
        out_shape=jax.ShapeDtypeStruct((S5_STATE_COLS, S5_ROWS, 128), F32),
        compiler_params=pltpu.CompilerParams(vmem_limit_bytes=VMEM_LIMIT),
        name="s5_state",
    )(uj, web)


S5_SCAN_COLS = 8


def _s5_scan_kernel(s_ref, a_ref, h0_ref, hin_ref, fin_ref, sg, hf, hb):
    ncol = S5_SCAN_COLS

    def scan(row0, nc, nb, h0):
        is_f = lax.broadcasted_iota(jnp.int32, (nb, 128), 1) < S5_STATE
        chunk_rows = lambda c: pl.ds(pl.multiple_of(row0 + c * nb, 8), nb)

        def gather(c, carry):
            for k in range(ncol):
                sg[k, chunk_rows(c), :] = s_ref[k, pl.ds(row0 + c, nb, stride=nc), :]
            return carry

        lax.fori_loop(0, nc, gather, 0)

        def body(c, hs):
            rf = chunk_rows(c)
            rb = chunk_rows(nc - 1 - c)
            new = []
            for m in range(ncol // 2):
                h_re, h_im = hs[2 * m], hs[2 * m + 1]
                a_re = a_ref[:, (2 * m) * 128:(2 * m + 1) * 128]
                a_im = a_ref[:, (2 * m + 1) * 128:(2 * m + 2) * 128]
                loc = []
                for k, h in ((2 * m, h_re), (2 * m + 1, h_im)):
                    hf[k, rf, :] = h
                    hb[k, rb, :] = h
                    loc.append(jnp.where(is_f, sg[k, rf, :], sg[k, rb, :]))
                new.append(a_re * h_re - a_im * h_im + loc[0])
                new.append(a_re * h_im + a_im * h_re + loc[1])
            return tuple(new)

        fin = lax.fori_loop(0, nc, body, h0)

        def scatter(c, carry):
            for k in range(ncol):
                hin_ref[k, pl.ds(row0 + c, nb, stride=nc), :] = jnp.where(
                    is_f, hf[k, chunk_rows(c), :], hb[k, chunk_rows(c), :])
            return carry

        lax.fori_loop(0, nc, scatter, 0)
        return fin

    fin = scan(0, SEQ // S5_CHUNK, BATCH, tuple(jnp.zeros((BATCH, 128), F32) for _ in range(ncol)))
    for k in range(ncol):
        fin_ref[:, k * 128:(k + 1) * 128] = fin[k]
    scan(S5_ROWS_C, DEC_SEQ // S5_CHUNK, DEC_BATCH,
         tuple(h0_ref[:, k * 128:(k + 1) * 128] for k in range(ncol)))


def _s5_scan(sloc, a16, h0l):
    ncol = S5_SCAN_COLS
    w = ncol * 128
    return pl.pallas_call(
        _s5_scan_kernel,
        grid=(S5_STATE_COLS // ncol,),
        in_specs=[
            pl.BlockSpec((ncol, S5_ROWS, 128), lambda k: (k, 0, 0)),
            pl.BlockSpec((1, w), lambda k: (0, k)),
            pl.BlockSpec((DEC_BATCH, w), lambda k: (0, k)),
        ],
        out_specs=[
            pl.BlockSpec((ncol, S5_ROWS, 128), lambda k: (k, 0, 0)),
            pl.BlockSpec((BATCH, w), lambda k: (0, k)),
        ],
        out_shape=[
            jax.ShapeDtypeStruct((S5_STATE_COLS, S5_ROWS, 128), F32),
            jax.ShapeDtypeStruct((BATCH, S5_STATE_COLS * 128), F32),
        ],
        scratch_shapes=[pltpu.VMEM((ncol, S5_ROWS, 128), F32)] * 3,
        name="s5_scan",
    )(sloc, a16, h0l)


def _s5_out_kernel(u_ref, hin_ref, wt_ref, wca_ref, d_ref, y_ref):
    u = u_ref[...]
    hin = jnp.concatenate([hin_ref[k] for k in range(S5_SLAB_COLS)], axis=1).astype(BF16)
    y = _dot(u.astype(BF16), wt_ref[...]) + _dot(hin, wca_ref[...]) + u * d_ref[...]
    for t in range(S5_CHUNK):
        y_ref[pl.ds(t, S5_ROW_TILE, stride=S5_CHUNK), :] = y[:, t * 128:(t + 1) * 128]


def _s5_out(uj, hin, wt, wca, dj):
    return pl.pallas_call(
        _s5_out_kernel,
        grid=(S5_SLABS, S5_ROWS // S5_ROW_TILE),
        in_specs=[
            pl.BlockSpec((None, S5_ROW_TILE, S5_SLAB_W), lambda j, p: (j, p, 0)),
            pl.BlockSpec((S5_SLAB_COLS, S5_ROW_TILE, 128), lambda j, p: (j, p, 0)),
            pl.BlockSpec((None, S5_SLAB_W, S5_SLAB_W), lambda j, p: (j, 0, 0)),
            pl.BlockSpec((None, S5_SLAB_W, S5_SLAB_W), lambda j, p: (j, 0, 0)),
            pl.BlockSpec((None, 1, S5_SLAB_W), lambda j, p: (j, 0, 0)),
        ],
        out_specs=pl.BlockSpec((None, S5_ROW_TILE * S5_CHUNK, 128), lambda j, p: (j, p, 0)),
        out_shape=jax.ShapeDtypeStruct((S5_SLABS, NTOK, 128), F32),
        compiler_params=pltpu.CompilerParams(vmem_limit_bytes=VMEM_LIMIT),
        name="s5_out",
    )(uj, hin, wt, wca, dj)


@functools.lru_cache(maxsize=None)
def _gla_consts():
    n = GLA_BLK
    nl = GLA_LEVELS
    r = np.arange(n)
    up = np.zeros((n, 128), np.int32)
    for l in range(nl):
        up[:, l] = (r >> l) & 1
    i = r[:, None]
    j = r[None, :]
    x = np.maximum(i ^ j, 1)
    lev = np.where(j < i, np.floor(np.log2(x)).astype(np.int32), np.where(i == j, nl, -1)).astype(np.int32)
    up2 = np.stack([up, up[::-1]])
    lev2 = np.stack([lev, lev[::-1, ::-1]])
    return up2, lev2


@functools.lru_cache(maxsize=None)
def _gla_tables():
    rowblk, seq, first, last = [], [], [], []
    for d in range(2):
        rb, sq, fi, la = [], [], [], []
        for s in range(BATCH + DEC_BATCH):
            nblk = 1 if s < BATCH else DEC_SEQ // GLA_BLK
            base = s if s < BATCH else NTOK_C // GLA_BLK + (s - BATCH) * nblk
            order = range(nblk) if d == 0 else range(nblk - 1, -1, -1)
            for pos, b in enumerate(order):
                rb.append(base + b)
                sq.append(s)
                fi.append(int(pos == 0))
                la.append(int(pos == nblk - 1))
        rowblk.append(rb); seq.append(sq); first.append(fi); last.append(la)
    as_np = lambda a: np.asarray(a, np.int32)
    return as_np(rowblk), as_np(seq), as_np(first), as_np(last)


def _gla_kernel(rowblk_ref, seq_ref, first_ref, last_ref,
                q_ref, k_ref, v_ref, lr_ref, wgk_ref, bgk_ref, up_ref, lev_ref, s0_ref,
                o_ref, fin_ref, z_scr, st_scr):
    d = pl.program_id(0)
    n = pl.program_id(1)
    nl = GLA_LEVELS
    blk = GLA_BLK

    @pl.when(first_ref[d, n] == 1)
    def _():
        st_scr[...] = jnp.zeros_like(st_scr)
        for h in range(GLA_HEADS):
            st_scr[h * GLA_DK:(h + 1) * GLA_DK, h * GLA_DV:(h + 1) * GLA_DV] = s0_ref[h]

    q = q_ref[...] * (GLA_DK ** -0.5)
    k = k_ref[...]
    vb = v_ref[...].astype(BF16)
    x = _dot(lr_ref[...].astype(BF16), wgk_ref[...]) + bgk_ref[...]
    gk = (jnp.minimum(x, 0.0) - jnp.log(1.0 + jnp.exp(-jnp.abs(x)))) * (1.0 / GLA_NORMALIZER)
    g_hi = gk.astype(BF16)
    g_lo = (gk - g_hi.astype(F32)).astype(BF16)
    ones = jnp.ones((blk, 128), BF16)
    tot = _dot_tn(g_hi, ones) + _dot_tn(g_lo, ones)

    row = lax.broadcasted_iota(jnp.int32, (blk, 1), 0)

    def sibling(a, l):
        g = 1 << l
        if g < 8:
            a3 = a.reshape(blk // 8, 8, a.shape[-1])
            dn = pltpu.roll(a3, g, 1).reshape(a.shape)
            up_ = pltpu.roll(a3, 8 - g, 1).reshape(a.shape)
            return jnp.where(((row >> l) & 1) == 1, dn, up_)
        a4 = a.reshape(blk // (2 * g), 2, g, a.shape[-1])
        return jnp.concatenate([a4[:, 1:2], a4[:, 0:1]], axis=1).reshape(a.shape)

    part = gk
    total = gk
    z0 = None
    for l in range(nl):
        up = up_ref[:, l:l + 1] != 0
        z = jnp.where(up, q, k) * jnp.exp(jnp.where(up, part, total - part))
        if l == 0:
            z0 = z
        else:
            z_scr[l] = z.astype(BF16)
        other = sibling(total, l)
        part = part + jnp.where(up, other, 0.0)
        total = total + other
    lane128 = lax.broadcasted_iota(jnp.int32, (GLA_QK, 128), 1)
    dim = lax.broadcasted_iota(jnp.int32, (GLA_QK, 128), 0)
    head_sum = ((dim >> 6) == lane128).astype(BF16)
    pair0 = _dot((z0 * sibling(z0, 0)).astype(BF16), head_sum)
    diag = _dot((q * k).astype(BF16), head_sum)

    lev = lev_ref[...]
    lane = lax.broadcasted_iota(jnp.int32, (blk, GLA_QK), 1)
    for h in range(GLA_HEADS):
        in_head = (lane >= h * GLA_DK) & (lane < (h + 1) * GLA_DK)
        att = jnp.where(lev == nl, diag[:, h:h + 1], 0.0)
        att = jnp.where(lev == 0, pair0[:, h:h + 1], att)
        for l in range(1, nl):
            lhs = z_scr[l]
            rhs = jnp.where(in_head, lhs, jnp.zeros_like(lhs))
            att = jnp.where(lev == l, _dot_nt(lhs, rhs), att)
        o_ref[:, h * GLA_DV:(h + 1) * GLA_DV] = _dot(att.astype(BF16), vb[:, h * GLA_DV:(h + 1) * GLA_DV])

    st = st_scr[...]
    q_in = (q * jnp.exp(part)).astype(BF16)
    o_ref[...] += _dot(q_in, st.astype(BF16))
    k_out = (k * jnp.exp(total - part)).astype(BF16)
    kv = _dot_tn(k_out, vb)
    row = lax.broadcasted_iota(jnp.int32, (GLA_QK, GLA_V), 0)
    col = lax.broadcasted_iota(jnp.int32, (GLA_QK, GLA_V), 1)
    same_head = (row >> 6) == (col >> 7)
    decay = jnp.exp(tot)
    decay = jnp.concatenate([decay] * GLA_HEADS, axis=1)
    st_new = decay * st + jnp.where(same_head, kv, 0.0)
    st_scr[...] = st_new

    @pl.when(last_ref[d, n] == 1)
    def _():
        for h in range(GLA_HEADS):
            fin_ref[h] = st_scr[h * GLA_DK:(h + 1) * GLA_DK, h * GLA_DV:(h + 1) * GLA_DV]


def _gla_mix(bslab, lr, wgk, bgk, s0):
    up, lev = _gla_consts()
    rowblk, seq, first, last = _gla_tables()
    nsteps = rowblk.shape[1]
    nseq = BATCH + DEC_BATCH
    nl = GLA_LEVELS
    grid_spec = pltpu.PrefetchScalarGridSpec(
        num_scalar_prefetch=4,
        grid=(2, nsteps),
        in_specs=[
            pl.BlockSpec((GLA_BLK, GLA_QK), lambda d, n, rb, sq, fi, la: (rb[d, n], 0)),
            pl.BlockSpec((GLA_BLK, GLA_QK), lambda d, n, rb, sq, fi, la: (rb[d, n], 1)),
            pl.BlockSpec((GLA_BLK, GLA_V), lambda d, n, rb, sq, fi, la: (rb[d, n], 1)),
            pl.BlockSpec((GLA_BLK, 128), lambda d, n, rb, sq, fi, la: (rb[d, n], 0)),
            pl.BlockSpec((None, 128, GLA_QK), lambda d, n, rb, sq, fi, la: (d, 0, 0)),
            pl.BlockSpec((None, 1, GLA_QK), lambda d, n, rb, sq, fi, la: (d, 0, 0)),
            pl.BlockSpec((None, GLA_BLK, 128), lambda d, n, rb, sq, fi, la: (d, 0, 0)),
            pl.BlockSpec((None, GLA_BLK, GLA_BLK), lambda d, n, rb, sq, fi, la: (d, 0, 0)),
            pl.BlockSpec((None, None, GLA_HEADS, GLA_DK, GLA_DV),
                         lambda d, n, rb, sq, fi, la: (sq[d, n], d, 0, 0, 0)),
        ],
        out_specs=[
            pl.BlockSpec((None, GLA_BLK, GLA_V), lambda d, n, rb, sq, fi, la: (d, rb[d, n], 0)),
            pl.BlockSpec((None, None, GLA_HEADS, GLA_DK, GLA_DV),
                         lambda d, n, rb, sq, fi, la: (sq[d, n], d, 0, 0, 0)),
        ],
        scratch_shapes=[
            pltpu.VMEM((nl, GLA_BLK, GLA_QK), BF16),
            pltpu.VMEM((GLA_QK, GLA_V), F32),
        ],
    )
    return pl.pallas_call(
        _gla_kernel,
        grid_spec=grid_spec,
        out_shape=[
            jax.ShapeDtypeStruct((2, NTOK, GLA_V), F32),
            jax.ShapeDtypeStruct((nseq, 2, GLA_HEADS, GLA_DK, GLA_DV), F32),
        ],
        compiler_params=pltpu.CompilerParams(vmem_limit_bytes=VMEM_LIMIT),
        name="gla_mix",
    )(jnp.asarray(rowblk), jnp.asarray(seq), jnp.asarray(first), jnp.asarray(last),
      bslab, bslab, bslab, lr, wgk, bgk, jnp.asarray(up), jnp.asarray(lev), s0)


def _softmax_head(s_list, v_list, sink):
    m = sink
    for s in s_list:
        m = jnp.maximum(m, jnp.max(s, axis=-1, keepdims=True))
    den = jnp.exp(sink - m)
    acc = None
    for s, v in zip(s_list, v_list):
        p = jnp.exp(s - m)
        den = den + jnp.sum(p, axis=-1, keepdims=True)
        pv = _dot(p.astype(BF16), v)
        acc = pv if acc is None else acc + pv
    return acc / den


def _attn_ctx_kernel(sink_ref, q_ref, k_ref, v_ref, o_ref):
    k = k_ref[...]
    v = v_ref[...]
    ks = (k.astype(BF16), pltpu.roll(k, 64, 1).astype(BF16))
    vs = (v.astype(BF16), pltpu.roll(v, 64, 1).astype(BF16))
    lo = lax.broadcasted_iota(jnp.int32, (SEQ, 128), 1) < HEAD_DIM
    for t in range(ATT_HEADS // 2):
        qt = q_ref[:, t * 128:(t + 1) * 128] * (HEAD_DIM ** -0.5)
        kvh = t // 2
        outs = []
        for p in range(2):
            sel = lo if p == 0 else jnp.logical_not(lo)
            qm = jnp.where(sel, qt, 0.0).astype(BF16)
            which = 0 if p == kvh else 1
            s = _dot_nt(qm, ks[which])
            outs.append(_softmax_head([s], [vs[which]], sink_ref[2 * t + p]))
        o_ref[:, t * 128:(t + 1) * 128] = jnp.where(lo, outs[0], outs[1])


def _attn_ctx(sink, cslab):
    return pl.pallas_call(
        _attn_ctx_kernel,
        grid=(BATCH,),
        in_specs=[
            pl.BlockSpec(memory_space=pltpu.SMEM),
            pl.BlockSpec((SEQ, ATT_Q), lambda b: (b, 0)),
            pl.BlockSpec((SEQ, ATT_KV), lambda b: (b, 4)),
            pl.BlockSpec((SEQ, ATT_KV), lambda b: (b, 5)),
        ],
        out_specs=pl.BlockSpec((SEQ, ATT_Q), lambda b: (b, 0)),
        out_shape=jax.ShapeDtypeStruct((NTOK_C, ATT_Q), F32),
        name="attn_ctx",
    )(sink, cslab, cslab, cslab)


def _attn_lat_kernel(sink_ref, q_ref, kp_ref, kc_ref, kn_ref, vp_ref, vc_ref, vn_ref,
                     ck_ref, cv_ref, cos_ref, sin_ref, o_ref):
    j = pl.program_id(1)
    nb = DEC_SEQ // ATT_BLOCK
    lane = lax.broadcasted_iota(jnp.int32, (ATT_BLOCK, 128), 1)
    lo = lane < HEAD_DIM
    first16 = (lane & 31) < 16

    def rope(x, blk_idx):
        r0 = pl.multiple_of(blk_idx * ATT_BLOCK, ATT_BLOCK)
        c = cos_ref[pl.ds(r0, ATT_BLOCK), :]
        s = sin_ref[pl.ds(r0, ATT_BLOCK), :]
        xs = jnp.where(first16, pltpu.roll(x, 112, 1), pltpu.roll(x, 16, 1))
        return x * c + xs * s

    nwin = 3 * ATT_BLOCK
    keys = jnp.concatenate([rope(kp_ref[...], jnp.maximum(j - 1, 0)), rope(kc_ref[...], j),
                            rope(kn_ref[...], jnp.minimum(j + 1, nb - 1)), ck_ref[...]], axis=0)
    vals = jnp.concatenate([vp_ref[...], vc_ref[...], vn_ref[...], cv_ref[...]], axis=0)
    keys2 = (keys.astype(BF16), pltpu.roll(keys, 64, 1).astype(BF16))
    vals2 = (vals.astype(BF16), pltpu.roll(vals, 64, 1).astype(BF16))
    nkeys = nwin + PAST_LEN
    qi = lax.broadcasted_iota(jnp.int32, (2 * ATT_BLOCK, nkeys), 0) & (ATT_BLOCK - 1)
    kc = lax.broadcasted_iota(jnp.int32, (2 * ATT_BLOCK, nkeys), 1)
    valid = (jnp.abs(kc - ATT_BLOCK - qi) <= WINDOW) | (kc >= nwin)
    valid = valid & ((j > 0) | (kc >= ATT_BLOCK)) & ((j < nb - 1) | (kc < 2 * ATT_BLOCK) | (kc >= nwin))
    top = lax.broadcasted_iota(jnp.int32, (2 * ATT_BLOCK, 1), 0) < ATT_BLOCK
    q_tiles = [rope(q_ref[:, t * 128:(t + 1) * 128], j) * (HEAD_DIM ** -0.5) for t in range(ATT_HEADS // 2)]
    for kvh in range(ATT_KV_HEADS):
        q2 = jnp.concatenate(q_tiles[2 * kvh:2 * kvh + 2], axis=0)
        lo2 = jnp.concatenate([lo, lo], axis=0)
        outs = []
        for p in range(2):
            qm = jnp.where(lo2 if p == 0 else jnp.logical_not(lo2), q2, 0.0).astype(BF16)
            which = 0 if p == kvh else 1
            sink = jnp.where(top, sink_ref[4 * kvh + p], sink_ref[4 * kvh + 2 + p])
            s = jnp.where(valid, _dot_nt(qm, keys2[which]), -1e30)
            outs.append(_softmax_head([s], [vals2[which]], sink))
        o2 = jnp.where(lo2, outs[0], outs[1])
        for i in range(2):
            t = 2 * kvh + i
            o_ref[:, t * 128:(t + 1) * 128] = o2[i * ATT_BLOCK:(i + 1) * ATT_BLOCK]


def _attn_lat(sink, cslab, ck, cv, cos_t, sin_t):
    nb = DEC_SEQ // ATT_BLOCK
    base = NTOK_C // ATT_BLOCK
    cur = lambda b, j: base + b * nb + j
    prv = lambda b, j: base + b * nb + jnp.maximum(j - 1, 0)
    nxt = lambda b, j: base + b * nb + jnp.minimum(j + 1, nb - 1)
    kv_spec = lambda row, col: pl.BlockSpec((ATT_BLOCK, ATT_KV), lambda b, j: (row(b, j), col))
    return pl.pallas_call(
        _attn_lat_kernel,
        grid=(DEC_BATCH, nb),
        in_specs=[
            pl.BlockSpec(memory_space=pltpu.SMEM),
            pl.BlockSpec((ATT_BLOCK, ATT_Q), lambda b, j: (cur(b, j), 0)),
            kv_spec(prv, 4), kv_spec(cur, 4), kv_spec(nxt, 4),
            kv_spec(prv, 5), kv_spec(cur, 5), kv_spec(nxt, 5),
            pl.BlockSpec((None, PAST_LEN, ATT_KV), lambda b, j: (b, 0, 0)),
            pl.BlockSpec((None, PAST_LEN, ATT_KV), lambda b, j: (b, 0, 0)),
            pl.BlockSpec((DEC_SEQ, 128), lambda b, j: (0, 0)),
            pl.BlockSpec((DEC_SEQ, 128), lambda b, j: (0, 0)),
        ],
        out_specs=pl.BlockSpec((ATT_BLOCK, ATT_Q), lambda b, j: (b * nb + j, 0)),
        out_shape=jax.ShapeDtypeStruct((NTOK_L, ATT_Q), F32),
        name="attn_lat",
    )(sink, cslab, cslab, cslab, cslab, cslab, cslab, cslab, ck, cv, cos_t, sin_t)


def _rope_tables():
    rows = DEC_SEQ // GRID_W
    row = np.repeat(np.arange(rows, dtype=np.float32), GRID_W)
    col = np.tile(np.arange(GRID_W, dtype=np.float32), rows)
    quarter = HEAD_DIM // 4
    inv = jnp.asarray(ROPE_BASE, F32) ** (-jnp.arange(quarter, dtype=F32) / quarter)
    lane = np.arange(128)
    use_row = (lane % HEAD_DIM) < HEAD_DIM // 2
    pos = jnp.where(use_row[None, :], jnp.asarray(row)[:, None], jnp.asarray(col)[:, None])
    ang = pos * inv[lane % quarter][None, :]
    sign = np.where((lane % 32) < 16, -1.0, 1.0).astype(np.float32)
    return jnp.cos(ang), jnp.sin(ang) * sign[None, :]


def _merge_kernel(*refs, split_x):
    if split_x:
        xc_ref, xl_ref, *refs = refs
    else:
        xc_ref, *refs = refs
    (mod_ref, g_ref, ys5_ref, ogf_ref, ogb_ref, gb_ref, ycc_ref, ycl_ref, gate_ref, gng_ref,
     wglu_ref, wbr_ref, wout_ref, o_ref) = refs
    is_ctx = pl.program_id(0) < NTOK_C // TM
    if split_x:
        x = jnp.where(is_ctx, xc_ref[...], xl_ref[...])
    else:
        x = xc_ref[...]
    y_c = jnp.where(is_ctx, ycc_ref[...], ycl_ref[...])
    y = jnp.concatenate([ys5_ref[j] for j in range(S5_SLABS)], axis=1)
    y = 0.5 * y * (1.0 + jnp.tanh(math.sqrt(2.0 / math.pi) * (y + 0.044715 * (y * y * y))))
    ag = _dot(y.astype(BF16), wglu_ref[...])
    y_a = ag[:, :S5_WIDTH] * _sigmoid(ag[:, S5_WIDTH:])
    gng = gng_ref[...]
    gb = gb_ref[...]
    parts = []
    for h in range(GLA_HEADS):
        sl = slice(h * GLA_DV, (h + 1) * GLA_DV)
        o = ogf_ref[:, sl] + ogb_ref[:, sl]
        g = gb[:, sl]
        parts.append(_rms(o, gng) * (g * _sigmoid(g)))
    y_b = jnp.concatenate(parts, axis=1)
    merged = None
    for n, yn in enumerate((y_a, y_b, y_c)):
        proj = _dot(yn.astype(BF16), wbr_ref[n])
        term = _sigmoid(gate_ref[:, n * D_MODEL:(n + 1) * D_MODEL]) * proj
        merged = term if merged is None else merged + term
    mixed = _dot(merged.astype(BF16), wout_ref[...])
    g1 = mod_ref[:, 2 * D_MODEL:3 * D_MODEL]
    o_ref[...] = x + g1 * _rms(mixed, g_ref[...])


def _layer_spec(shape, layer):
    return pl.BlockSpec((None,) + shape, lambda i: (layer,) + (0,) * len(shape), pipeline_mode=pl.Buffered(1))


def _split_token_specs(n_arrays, width=D_MODEL):
    nct = NTOK_C // TM
    if n_arrays == 2:
        return [pl.BlockSpec((TM, width), lambda i: (jnp.minimum(i, nct - 1), 0)),
                pl.BlockSpec((TM, width), lambda i: (jnp.maximum(i - nct, 0), 0))]
    return [pl.BlockSpec((TM, width), lambda i: (i, 0))]


def _merge(xs, mod, g, ys5, og, bslab, yc, gates, gng, wglu, wbr, wout, layer):
    tok = lambda width, col=0: pl.BlockSpec((TM, width), lambda i: (i, col))
    full = lambda shape: _layer_spec(shape, layer)
    return pl.pallas_call(
        functools.partial(_merge_kernel, split_x=len(xs) == 2),
        grid=(NTOK // TM,),
        in_specs=_split_token_specs(len(xs)) + [
            pl.BlockSpec((None, 1, 6 * D_MODEL), lambda i: (_mod_row(i), 0, 0)),
            pl.BlockSpec((1, D_MODEL), lambda i: (0, 0)),
            pl.BlockSpec((S5_SLABS, TM, 128), lambda i: (0, i, 0)),
            pl.BlockSpec((None, TM, GLA_V), lambda i: (0, i, 0)),
            pl.BlockSpec((None, TM, GLA_V), lambda i: (1, i, 0)),
            tok(GLA_V, 2),
        ] + _split_token_specs(2, ATT_Q) + [
            tok(N_BRANCH * D_MODEL),
            pl.BlockSpec((1, GLA_DV), lambda i: (0, 0)),
            full((S5_WIDTH, 2 * S5_WIDTH)),
            full((N_BRANCH, BRANCH_W, D_MODEL)),
            full((D_MODEL, D_MODEL)),
        ],
        out_specs=tok(D_MODEL),
        out_shape=jax.ShapeDtypeStruct((NTOK, D_MODEL), F32),
        compiler_params=pltpu.CompilerParams(vmem_limit_bytes=VMEM_LIMIT),
        name="merge",
    )(*xs, mod, g, ys5, og, og, bslab, *yc, gates, gng, wglu, wbr, wout)


FFN_SPLIT = 2


def _ffn_kernel(x_ref, mod_ref, gin_ref, gout_ref, w1_ref, w2_ref, *o_refs):
    x = x_ref[...]
    sh = mod_ref[:, 3 * D_MODEL:4 * D_MODEL]
    sc = mod_ref[:, 4 * D_MODEL:5 * D_MODEL]
    g2 = mod_ref[:, 5 * D_MODEL:6 * D_MODEL]
    h = (_rms(x, gin_ref[...]) * (1.0 + sc) + sh).astype(BF16)
    ck = FFN_HIDDEN // FFN_SPLIT
    acc = None
    for c in range(FFN_SPLIT):
        a = _dot(h, w1_ref[:, c * ck:(c + 1) * ck])
        b = _dot(h, w1_ref[:, FFN_HIDDEN + c * ck:FFN_HIDDEN + (c + 1) * ck])
        act = (a * _sigmoid(a) * b).astype(BF16)
        part = _dot(act, w2_ref[c * ck:(c + 1) * ck, :])
        acc = part if acc is None else acc + part
    y = x + g2 * _rms(acc, gout_ref[...])
    if len(o_refs) == 1:
        o_refs[0][...] = y
    else:
        is_ctx = pl.program_id(0) < NTOK_C // TM

        @pl.when(is_ctx)
        def _():
            o_refs[0][...] = y

        @pl.when(jnp.logical_not(is_ctx))
        def _():
            o_refs[1][...] = y


def _ffn(x, mod, gin, gout, w1, w2, layer, split_out):
    small = lambda shape: pl.BlockSpec(shape, lambda i: (0,) * len(shape))
    nct = NTOK_C // TM
    if split_out:
        out_specs = [pl.BlockSpec((TM, D_MODEL), lambda i: (jnp.minimum(i, nct - 1), 0)),
                     pl.BlockSpec((TM, D_MODEL), lambda i: (jnp.maximum(i - nct, 0), 0))]
        out_shape = [jax.ShapeDtypeStruct((NTOK_C, D_MODEL), F32), jax.ShapeDtypeStruct((NTOK_L, D_MODEL), F32)]
    else:
        out_specs = pl.BlockSpec((TM, D_MODEL), lambda i: (i, 0))
        out_shape = jax.ShapeDtypeStruct((NTOK, D_MODEL), F32)
    return pl.pallas_call(
        _ffn_kernel,
        grid=(NTOK // TM,),
        in_specs=[
            pl.BlockSpec((TM, D_MODEL), lambda i: (i, 0)),
            pl.BlockSpec((None, 1, 6 * D_MODEL), lambda i: (_mod_row(i), 0, 0)),
            small((1, D_MODEL)),
            small((1, D_MODEL)),
            _layer_spec((D_MODEL, 2 * FFN_HIDDEN), layer),
            _layer_spec((FFN_HIDDEN, D_MODEL), layer),
        ],
        out_specs=out_specs,
        out_shape=out_shape,
        compiler_params=pltpu.CompilerParams(vmem_limit_bytes=VMEM_LIMIT),
        name="ffn",
    )(x, mod, gin, gout, w1, w2)


def kernel(x_prompt, x_sample, cache_k, cache_v, state_s5, state_gla, c, c_ctx, w_mod, b_mod, norm_g, w_in,
           s5_lam_re, s5_lam_im, s5_log_step, s5_b_re, s5_b_im, s5_c_re, s5_c_im, s5_d, w_glu, gla_w_gk,
           gla_b_gk, gla_norm_g, att_sink, w_branch, w_out, w_ffn_in, w_ffn_out):
    cond = jnp.concatenate([c_ctx[None, :], c, jnp.zeros((N_MOD_ROWS - 1 - DEC_BATCH, D_MODEL), F32)], axis=0)
    mod_all = _modulation(cond, w_mod, b_mod).reshape(DEPTH, N_MOD_ROWS, 1, 6 * D_MODEL)
    cos_t, sin_t = _rope_tables()
    xs = (x_prompt.reshape(NTOK_C, D_MODEL), x_sample.reshape(NTOK_L, D_MODEL))
    w_in_b = jnp.pad(w_in.astype(BF16), ((0, 0), (0, 0), (0, W_IN_COLS - w_in.shape[-1])))
    w_glu_b, w_branch_b, w_out_b = w_glu.astype(BF16), w_branch.astype(BF16), w_out.astype(BF16)
    w_ffn_in_b, w_ffn_out_b = w_ffn_in.astype(BF16), w_ffn_out.astype(BF16)
    new_k, new_v, new_s5, new_gla = [], [], [], []
    for i in range(DEPTH):
        mod = mod_all[i]
        uj, bslab, cslab, gates, lr = _inproj(xs, mod, norm_g[i, 0][None, :], w_in_b, i)

        wt, web, wca, a16, dj = _s5_prep(s5_lam_re[i], s5_lam_im[i], s5_log_step[i], s5_b_re[i], s5_b_im[i],
                                         s5_c_re[i], s5_c_im[i], s5_d[i])
        h0l = state_s5[:, i].astype(F32).transpose(0, 2, 4, 1, 3).reshape(DEC_BATCH, S5_GROUPS * 256)
        hin, finc = _s5_scan(_s5_state(uj, web), a16, h0l)
        ys5 = _s5_out(uj, hin, wt, wca, dj)
        new_s5.append(finc.reshape(BATCH, S5_GROUPS, 2, 2, S5_STATE).transpose(0, 3, 1, 4, 2))

        wgk = jnp.zeros((2, 128, GLA_QK), F32)
        wgk = wgk.at[0, 0:GLA_RANK].set(gla_w_gk[i, 0]).at[1, GLA_RANK:2 * GLA_RANK].set(gla_w_gk[i, 1])
        s0 = jnp.concatenate([jnp.zeros((BATCH, 2, GLA_HEADS, GLA_DK, GLA_DV), F32),
                              state_gla[:, i].astype(F32)], axis=0)
        og, gla_fin = _gla_mix(bslab, lr, wgk.astype(BF16), gla_b_gk[i][:, None, :].astype(F32), s0)
        new_gla.append(gla_fin[:BATCH])

        sink = att_sink[i].astype(F32)
        yc = (_attn_ctx(sink, cslab),
              _attn_lat(sink, cslab, cache_k[:, i].reshape(DEC_BATCH, PAST_LEN, ATT_KV).astype(F32),
                        cache_v[:, i].reshape(DEC_BATCH, PAST_LEN, ATT_KV).astype(F32), cos_t, sin_t))
        new_k.append(cslab[:NTOK_C, ATT_Q:ATT_Q + ATT_KV].reshape(BATCH, SEQ, ATT_KV_HEADS, HEAD_DIM))
        new_v.append(cslab[:NTOK_C, ATT_Q + ATT_KV:].reshape(BATCH, SEQ, ATT_KV_HEADS, HEAD_DIM))

        x = _merge(xs, mod, norm_g[i, 1][None, :], ys5, og, bslab, yc, gates, gla_norm_g[i][None, :],
                   w_glu_b, w_branch_b, w_out_b, i)
        last = i == DEPTH - 1
        x = _ffn(x, mod, norm_g[i, 2][None, :], norm_g[i, 3][None, :], w_ffn_in_b, w_ffn_out_b, i, last)
        xs = tuple(x) if last else (x,)

    return (xs[0].reshape(BATCH, SEQ, D_MODEL), xs[1].reshape(DEC_BATCH, DEC_SEQ, D_MODEL),
            jnp.stack(new_k, axis=1), jnp.stack(new_v, axis=1),
            jnp.stack(new_s5, axis=1), jnp.stack(new_gla, axis=1))
```

```python
import functools
import math

import numpy as np
import jax
import jax.numpy as jnp
from jax import lax
from jax.experimental import pallas as pl
from jax.experimental.pallas import tpu as pltpu

F32 = jnp.float32
BF16 = jnp.bfloat16

D_MODEL = 1024
BATCH = 16
SEQ = 256
DEPTH = 2
DEC_BATCH = 8
DEC_SEQ = 1024
PAST_LEN = 256
GRID_W = 64
ROPE_BASE = 10000.0
S5_WIDTH = 512
S5_GROUP = 16
S5_GROUPS = 32
S5_STATE = 64
GLA_HEADS = 4
GLA_DK = 64
GLA_DV = 128
GLA_QK = 256
GLA_V = 512
GLA_RANK = 16
GLA_NORMALIZER = 16.0
ATT_HEADS = 8
ATT_KV_HEADS = 2
HEAD_DIM = 64
ATT_Q = 512
ATT_KV = 128
WINDOW = 128
ATT_BLOCK = 128
N_BRANCH = 3
BRANCH_W = 512
FFN_HIDDEN = 2816
RMS_EPS = 1e-6

NTOK_C = BATCH * SEQ
NTOK_L = DEC_BATCH * DEC_SEQ
NTOK = NTOK_C + NTOK_L
TM = 512
N_MOD_ROWS = 16

D_IN = 5920
W_IN_COLS = 6016
S5_CHUNK = 16
S5_SLABS = S5_WIDTH // 128
S5_SLAB_W = S5_CHUNK * 128
S5_ROWS_C = NTOK_C // S5_CHUNK
S5_ROWS = NTOK // S5_CHUNK
S5_ROW_TILE = 256
GLA_BLK = 256
GLA_LEVELS = 8
VMEM_LIMIT = 56 * 1024 * 1024


def _dot(a, b):
    return jnp.dot(a, b, preferred_element_type=F32)


def _dot_nt(a, b):
    return lax.dot_general(a, b, (((1,), (1,)), ((), ())), preferred_element_type=F32)


def _dot_tn(a, b):
    return lax.dot_general(a, b, (((0,), (0,)), ((), ())), preferred_element_type=F32)


def _rms(x, g):
    return x * lax.rsqrt(jnp.mean(x * x, axis=-1, keepdims=True) + RMS_EPS) * g


def _sigmoid(x):
    return 0.5 * jnp.tanh(0.5 * x) + 0.5


def _mod_row(i):
    nct = NTOK_C // TM
    return jnp.where(i < nct, 0, 1 + (i - nct) // (DEC_SEQ // TM))


def _mod_kernel(c_ref, w_ref, b_ref, o_ref):
    c = c_ref[...]
    s = (c * _sigmoid(c)).astype(BF16)
    o_ref[...] = _dot(s, w_ref[...].astype(BF16)) + b_ref[...]


def _modulation(cond, w_mod, b_mod):
    tn = 2048
    return pl.pallas_call(
        _mod_kernel,
        grid=(DEPTH, 6 * D_MODEL // tn),
        in_specs=[
            pl.BlockSpec((N_MOD_ROWS, D_MODEL), lambda l, n: (0, 0)),
            pl.BlockSpec((None, D_MODEL, tn), lambda l, n: (l, 0, n)),
            pl.BlockSpec((None, 1, tn), lambda l, n: (l, 0, n)),
        ],
        out_specs=pl.BlockSpec((None, N_MOD_ROWS, tn), lambda l, n: (l, 0, n)),
        out_shape=jax.ShapeDtypeStruct((DEPTH, N_MOD_ROWS, 6 * D_MODEL), F32),
        name="modulation",
    )(cond, w_mod, b_mod.reshape(DEPTH, 1, 6 * D_MODEL))


_IN_SLABS = ((0, 512), (512, 1536), (2048, 768), (2816, 3072), (5888, 128))
W_IN_SPLIT = 2048
W_IN_GAP = 32
W_IN_TAIL = W_IN_COLS - W_IN_SPLIT


def _inproj_kernel(*refs, split_x):
    if split_x:
        xc_ref, xl_ref, *refs = refs
    else:
        xc_ref, *refs = refs
    mod_ref, g_ref, w_ref, w_end_ref, u_ref, b_ref, c_ref, gate_ref, lr_ref, w_tail, u_stage = refs
    i = pl.program_id(0)

    @pl.when(i == 0)
    def _():
        r = lax.broadcasted_iota(jnp.int32, (256, 128), 0)
        c = lax.broadcasted_iota(jnp.int32, (256, 128), 1)
        shift = (r == c + W_IN_GAP).astype(BF16)
        head = ((r == c) & (c < W_IN_GAP)).astype(BF16)
        ntile = (W_IN_TAIL - 128) // 128
        for t in range(ntile - 1):
            src = W_IN_SPLIT + 128 * t
            w_tail[:, 128 * t:128 * (t + 1)] = _dot(w_ref[:, src:src + 256], shift).astype(BF16)
        src = W_IN_SPLIT + 128 * (ntile - 1)
        last = jnp.concatenate([w_ref[:, src:src + 128], w_end_ref[...]], axis=1)
        w_tail[:, 128 * (ntile - 1):128 * ntile] = _dot(last, shift).astype(BF16)
        w_tail[:, 128 * ntile:] = _dot(w_ref[:, W_IN_SPLIT:W_IN_SPLIT + 256], head).astype(BF16)

    if split_x:
        x = jnp.where(i < NTOK_C // TM, xc_ref[...], xl_ref[...])
    else:
        x = xc_ref[...]
    mod = mod_ref[...]
    h = _rms(x, g_ref[...]) * (1.0 + mod[:, D_MODEL:2 * D_MODEL]) + mod[:, 0:D_MODEL]
    h = h.astype(BF16)
    for j in range(S5_SLABS):
        u_stage[...] = _dot(h, w_ref[:, j * 128:(j + 1) * 128])
        for s in range(S5_CHUNK):
            u_ref[j, :, s * 128:(s + 1) * 128] = u_stage[pl.ds(s, TM // S5_CHUNK, stride=S5_CHUNK), :]
    b_ref[...] = _dot(h, w_ref[:, 512:W_IN_SPLIT])
    for (off, width), o_ref in zip(_IN_SLABS[2:], (c_ref, gate_ref, lr_ref)):
        o_ref[...] = _dot(h, w_tail[:, off - W_IN_SPLIT:off - W_IN_SPLIT + width])


def _inproj(xs, mod, g, w_all, w_end, layer):
    return pl.pallas_call(
        functools.partial(_inproj_kernel, split_x=len(xs) == 2),
        grid=(NTOK // TM,),
        in_specs=_split_token_specs(len(xs)) + [
            pl.BlockSpec((None, 1, 6 * D_MODEL), lambda i: (_mod_row(i), 0, 0)),
            pl.BlockSpec((1, D_MODEL), lambda i: (0, 0)),
            pl.BlockSpec((None, D_MODEL, D_IN), lambda i: (layer, 0, 0), pipeline_mode=pl.Buffered(1)),
            pl.BlockSpec((None, D_MODEL, 128), lambda i: (layer, 0, 0), pipeline_mode=pl.Buffered(1)),
        ],
        out_specs=[pl.BlockSpec((S5_SLABS, TM // S5_CHUNK, S5_SLAB_W), lambda i: (0, i, 0))]
        + [pl.BlockSpec((TM, width), lambda i: (i, 0)) for _, width in _IN_SLABS[1:]],
        out_shape=[jax.ShapeDtypeStruct((S5_SLABS, S5_ROWS, S5_SLAB_W), F32)]
        + [jax.ShapeDtypeStruct((NTOK, width), F32) for _, width in _IN_SLABS[1:]],
        scratch_shapes=[pltpu.VMEM((D_MODEL, W_IN_TAIL), BF16), pltpu.VMEM((TM, 128), F32)],
        compiler_params=pltpu.CompilerParams(vmem_limit_bytes=VMEM_LIMIT),
        name="inproj",
    )(*xs, mod, g, w_all, w_end)


@functools.lru_cache(maxsize=None)
def _s5_expanders():
    seg = 8
    spread = np.zeros((seg, 256, S5_SLAB_W), np.float32)
    place = np.zeros((seg, 256, S5_SLAB_W), np.float32)
    col = np.arange(256)
    for gl in range(seg):
        spread[gl, col, (col // S5_GROUP) * 128 + gl * S5_GROUP + col % S5_GROUP] = 1.0
        place[gl, col, gl * 256 + col] = 1.0
    return spread, place


def _s5_prep_kernel(par_ref, bre_ref, bim_ref, cre_ref, cim_ref, spread_ref, place_ref,
                    wt_ref, web_ref, wca_ref, a16_ref):
    n = S5_CHUNK
    lam_re = par_ref[0:1, :]
    lam_im = par_ref[1:2, :]
    dt = jnp.exp(par_ref[2:3, :])
    lr = lam_re * dt
    li = lam_im * dt
    krow = lax.broadcasted_iota(jnp.int32, (24, 128), 0).astype(F32)
    tab_mag = jnp.exp(krow * lr)
    tab_re = tab_mag * jnp.cos(krow * li)
    tab_im = tab_mag * jnp.sin(krow * li)
    ar = tab_re[1:2, :]
    ai = tab_im[1:2, :]
    nr = ar - 1.0
    den = lam_re * lam_re + lam_im * lam_im
    fr = (nr * lam_re + ai * lam_im) / den
    fi = (ai * lam_re - nr * lam_im) / den
    b_re = bre_ref[...]
    b_im = bim_ref[...]
    br = fr * b_re - fi * b_im
    bi = fr * b_im + fi * b_re
    c_re = cre_ref[...]
    c_im = cim_ref[...]

    def lo_half(shape):
        return lax.broadcasted_iota(jnp.int32, shape, 1) < S5_STATE

    def tile_rows(a):
        return jnp.concatenate([a] * n, axis=0)

    fwd16 = lo_half((S5_GROUP, 128))

    def powers(t_re, t_im, k_fwd, k_bwd):
        def pick(t, b):
            kf, kb = k_fwd(b), k_bwd(b)
            return jnp.where(fwd16, jnp.broadcast_to(t[kf:kf + 1, :], (S5_GROUP, 128)),
                             jnp.broadcast_to(t[kb:kb + 1, :], (S5_GROUP, 128)))
        return (jnp.concatenate([pick(t_re, b) for b in range(n)], axis=0),
                jnp.concatenate([pick(t_im, b) for b in range(n)], axis=0))

    fwd = lo_half((n * S5_GROUP, 128))
    brt, bit, crt, cit = tile_rows(br), tile_rows(bi), tile_rows(c_re), tile_rows(c_im)

    per, pei = powers(tab_re, tab_im, lambda s: n - 1 - s, lambda s: s)
    eb = jnp.concatenate([brt * per - bit * pei, brt * pei + bit * per], axis=1)
    pcr, pci = powers(tab_re, tab_im, lambda t: t + 1, lambda t: n - t)
    ca = jnp.concatenate([(crt * pcr - cit * pci).T, (-(crt * pci + cit * pcr)).T], axis=0)

    def one_dir(x, d):
        sw = pltpu.roll(x, S5_STATE, 1)
        lo = lo_half(x.shape)
        return jnp.where(lo, x, sw) if d == 0 else jnp.where(lo, sw, x)

    klag = []
    for d in range(2):
        lhs = jnp.where(lo_half(br.shape), one_dir(br, d), -one_dir(bi, d))
        crd, cid = tile_rows(one_dir(c_re, d)), tile_rows(one_dir(c_im, d))
        lag = (lambda b: b) if d == 0 else (lambda b: n - 1 - b)
        pr, pi = powers(one_dir(tab_re, d), one_dir(tab_im, d), lag, lag)
        rhs_t = jnp.where(fwd, crd * pr - cid * pi, crd * pi + cid * pr)
        klag.append(lax.dot_general(lhs, rhs_t, (((1,), (1,)), ((), ())),
                                    precision=lax.Precision.HIGHEST, preferred_element_type=F32))
    lane = lax.broadcasted_iota(jnp.int32, (S5_GROUP, n * S5_GROUP), 1)
    rows = []
    for s in range(n):
        f = klag[0] if s == 0 else jnp.where(lane >= S5_GROUP * s, pltpu.roll(klag[0], S5_GROUP * s, 1), 0.0)
        sh = (n * S5_GROUP - S5_GROUP * (n - 1 - s)) % (n * S5_GROUP)
        b = klag[1] if sh == 0 else pltpu.roll(klag[1], sh, 1)
        rows.append(f + jnp.where(lane < S5_GROUP * (s + 1), b, 0.0))
    toep = jnp.concatenate(rows, axis=0)

    spread = spread_ref[...]
    wt_ref[...] = _dot(toep.astype(BF16), spread).astype(BF16).reshape(n, S5_GROUP, S5_SLAB_W)
    web_ref[...] = _dot(eb.astype(BF16), place_ref[...]).astype(BF16).reshape(n, S5_GROUP, S5_SLAB_W)
    wca_ref[...] = _dot(ca.astype(BF16), spread).astype(BF16)
    a16_ref[0:1, :] = tab_re[n:n + 1, :]
    a16_ref[1:2, :] = tab_im[n:n + 1, :]


def _s5_prep(lam_re, lam_im, log_step, b_re, b_im, c_re, c_im, d_skip):
    seg = 8
    par = jnp.stack([lam_re, lam_im, log_step]).astype(F32).transpose(2, 0, 1, 3).reshape(S5_GROUPS, 3, 128)
    par = jnp.concatenate([par, jnp.zeros((S5_GROUPS, 5, 128), F32)], axis=1)
    b_t = lambda b: b.astype(F32).transpose(1, 3, 0, 2).reshape(S5_GROUPS, S5_GROUP, 128)
    c_t = lambda c: c.astype(F32).transpose(1, 2, 0, 3).reshape(S5_GROUPS, S5_GROUP, 128)
    spread, place = _s5_expanders()
    vec = pl.BlockSpec((None, S5_GROUP, 128), lambda gl, j: (j * seg + gl, 0, 0))
    exp_spec = pl.BlockSpec((None, 256, S5_SLAB_W), lambda gl, j: (gl, 0, 0))
    rows_spec = pl.BlockSpec((None, S5_CHUNK, None, S5_GROUP, S5_SLAB_W), lambda gl, j: (j, 0, gl, 0, 0))
    wt, web, wca, a16 = pl.pallas_call(
        _s5_prep_kernel,
        grid=(seg, S5_SLABS),
        in_specs=[pl.BlockSpec((None, 8, 128), lambda gl, j: (j * seg + gl, 0, 0)), vec, vec, vec, vec,
                  exp_spec, exp_spec],
        out_specs=[
            rows_spec, rows_spec,
            pl.BlockSpec((None, None, 256, S5_SLAB_W), lambda gl, j: (j, gl, 0, 0)),
            pl.BlockSpec((None, 2, 128), lambda gl, j: (j * seg + gl, 0, 0)),
        ],
        out_shape=[
            jax.ShapeDtypeStruct((S5_SLABS, S5_CHUNK, seg, S5_GROUP, S5_SLAB_W), BF16),
            jax.ShapeDtypeStruct((S5_SLABS, S5_CHUNK, seg, S5_GROUP, S5_SLAB_W), BF16),
            jax.ShapeDtypeStruct((S5_SLABS, seg, 256, S5_SLAB_W), BF16),
            jax.ShapeDtypeStruct((S5_GROUPS, 2, 128), F32),
        ],
        name="s5_prep",
    )(par, b_t(b_re), b_t(b_im), c_t(c_re), c_t(c_im), jnp.asarray(spread, BF16), jnp.asarray(place, BF16))
    mat = (S5_SLABS, S5_SLAB_W, S5_SLAB_W)
    dj = jnp.tile(d_skip.astype(F32).reshape(S5_SLABS, 1, 128), (1, 1, S5_CHUNK))
    return wt.reshape(mat), web.reshape(mat), wca.reshape(mat), a16.reshape(1, S5_SLABS * S5_SLAB_W), dj


S5_STATE_COLS = S5_SLABS * S5_SLAB_W // 128
S5_SLAB_COLS = S5_SLAB_W // 128


def _s5_state_kernel(u_ref, w_ref, o_ref):
    s = _dot(u_ref[...].astype(BF16), w_ref[...])
    for k in range(S5_SLAB_COLS):
        o_ref[k] = s[:, k * 128:(k + 1) * 128]


def _s5_state(uj, web):
    return pl.pallas_call(
        _s5_state_kernel,
        grid=(S5_SLABS, S5_ROWS // S5_ROW_TILE),
        in_specs=[
            pl.BlockSpec((None, S5_ROW_TILE, S5_SLAB_W), lambda j, p: (j, p, 0)),
            pl.BlockSpec((None, S5_SLAB_W, S5_SLAB_W), lambda j, p: (j, 0, 0)),
        ],
        out_specs=pl.BlockSpec((S5_SLAB_COLS, S5_ROW_TILE, 128), lambda j, p: (j, p, 0)),
        out_shape=jax.ShapeDtypeStruct((S5_STATE_COLS, S5_ROWS, 128), F32),
        compiler_params=pltpu.CompilerParams(vmem_limit_bytes=VMEM_LIMIT),
        name="s5_state",
    )(uj, web)


S5_SCAN_COLS = 8


def _s5_scan_kernel(s_ref, a_ref, h0_ref, hin_ref, fin_ref, sg, hf, hb):
    ncol = S5_SCAN_COLS

    def scan(row0, nc, nb, h0):
        is_f = lax.broadcasted_iota(jnp.int32, (nb, 128), 1) < S5_STATE
        chunk_rows = lambda c: pl.ds(pl.multiple_of(row0 + c * nb, 8), nb)

        def gather(c, carry):
            for k in range(ncol):
                sg[k, chunk_rows(c), :] = s_ref[k, pl.ds(row0 + c, nb, stride=nc), :]
            return carry

        lax.fori_loop(0, nc, gather, 0)

        def body(c, hs):
            rf = chunk_rows(c)
            rb = chunk_rows(nc - 1 - c)
            new = []
            for m in range(ncol // 2):
                h_re, h_im = hs[2 * m], hs[2 * m + 1]
                a_re = a_ref[:, (2 * m) * 128:(2 * m + 1) * 128]
                a_im = a_ref[:, (2 * m + 1) * 128:(2 * m + 2) * 128]
                loc = []
                for k, h in ((2 * m, h_re), (2 * m + 1, h_im)):
                    hf[k, rf, :] = h
                    hb[k, rb, :] = h
                    loc.append(jnp.where(is_f, sg[k, rf, :], sg[k, rb, :]))
                new.append(a_re * h_re - a_im * h_im + loc[0])
                new.append(a_re * h_im + a_im * h_re + loc[1])
            return tuple(new)

        fin = lax.fori_loop(0, nc, body, h0)

        def scatter(c, carry):
            for k in range(ncol):
                hin_ref[k, pl.ds(row0 + c, nb, stride=nc), :] = jnp.where(
                    is_f, hf[k, chunk_rows(c), :], hb[k, chunk_rows(c), :])
            return carry

        lax.fori_loop(0, nc, scatter, 0)
        return fin

    fin = scan(0, SEQ // S5_CHUNK, BATCH, tuple(jnp.zeros((BATCH, 128), F32) for _ in range(ncol)))
    for k in range(ncol):
        fin_ref[:, k * 128:(k + 1) * 128] = fin[k]
    scan(S5_ROWS_C, DEC_SEQ // S5_CHUNK, DEC_BATCH,
         tuple(h0_ref[:, k * 128:(k + 1) * 128] for k in range(ncol)))


def _s5_scan(sloc, a16, h0l):
    ncol = S5_SCAN_COLS
    w = ncol * 128
    return pl.pallas_call(
        _s5_scan_kernel,
        grid=(S5_STATE_COLS // ncol,),
        in_specs=[
            pl.BlockSpec((ncol, S5_ROWS, 128), lambda k: (k, 0, 0)),
            pl.BlockSpec((1, w), lambda k: (0, k)),
            pl.BlockSpec((DEC_BATCH, w), lambda k: (0, k)),
        ],
        out_specs=[
            pl.BlockSpec((ncol, S5_ROWS, 128), lambda k: (k, 0, 0)),
            pl.BlockSpec((BATCH, w), lambda k: (0, k)),
        ],
        out_shape=[
            jax.ShapeDtypeStruct((S5_STATE_COLS, S5_ROWS, 128), F32),
            jax.ShapeDtypeStruct((BATCH, S5_STATE_COLS * 128), F32),
        ],
        scratch_shapes=[pltpu.VMEM((ncol, S5_ROWS, 128), F32)] * 3,
        name="s5_scan",
    )(sloc, a16, h0l)


def _s5_out_kernel(u_ref, hin_ref, wt_ref, wca_ref, d_ref, y_ref):
    u = u_ref[...]
    hin = jnp.concatenate([hin_ref[k] for k in range(S5_SLAB_COLS)], axis=1).astype(BF16)
    y = _dot(u.astype(BF16), wt_ref[...]) + _dot(hin, wca_ref[...]) + u * d_ref[...]
    for t in range(S5_CHUNK):
        y_ref[pl.ds(t, S5_ROW_TILE, stride=S5_CHUNK), :] = y[:, t * 128:(t + 1) * 128]


def _s5_out(uj, hin, wt, wca, dj):
    return pl.pallas_call(
        _s5_out_kernel,
        grid=(S5_SLABS, S5_ROWS // S5_ROW_TILE),
        in_specs=[
            pl.BlockSpec((None, S5_ROW_TILE, S5_SLAB_W), lambda j, p: (j, p, 0)),
            pl.BlockSpec((S5_SLAB_COLS, S5_ROW_TILE, 128), lambda j, p: (j, p, 0)),
            pl.BlockSpec((None, S5_SLAB_W, S5_SLAB_W), lambda j, p: (j, 0, 0)),
            pl.BlockSpec((None, S5_SLAB_W, S5_SLAB_W), lambda j, p: (j, 0, 0)),
            pl.BlockSpec((None, 1, S5_SLAB_W), lambda j, p: (j, 0, 0)),
        ],
        out_specs=pl.BlockSpec((None, S5_ROW_TILE * S5_CHUNK, 128), lambda j, p: (j, p, 0)),
        out_shape=jax.ShapeDtypeStruct((S5_SLABS, NTOK, 128), F32),
        compiler_params=pltpu.CompilerParams(vmem_limit_bytes=VMEM_LIMIT),
        name="s5_out",
    )(uj, hin, wt, wca, dj)


@functools.lru_cache(maxsize=None)
def _gla_consts():
    n = GLA_BLK
    nl = GLA_LEVELS
    r = np.arange(n)
    up = np.zeros((n, 128), np.int32)
    for l in range(nl):
        up[:, l] = (r >> l) & 1
    i = r[:, None]
    j = r[None, :]
    x = np.maximum(i ^ j, 1)
    lev = np.where(j < i, np.floor(np.log2(x)).astype(np.int32), np.where(i == j, nl, -1)).astype(np.int32)
    up2 = np.stack([up, up[::-1]])
    h = n // 2

    def tiled(a):
        return np.stack([np.concatenate([a[:h, :h], a[h:, h:]]), np.concatenate([a[:h, h:], a[h:, :h]])])

    lev2 = np.stack([tiled(lev), tiled(lev[::-1, ::-1])])
    return up2, lev2


@functools.lru_cache(maxsize=None)
def _gla_tables():
    rowblk, seq, first, last = [], [], [], []
    for d in range(2):
        rb, sq, fi, la = [], [], [], []
        for s in range(BATCH + DEC_BATCH):
            nblk = 1 if s < BATCH else DEC_SEQ // GLA_BLK
            base = s if s < BATCH else NTOK_C // GLA_BLK + (s - BATCH) * nblk
            order = range(nblk) if d == 0 else range(nblk - 1, -1, -1)
            for pos, b in enumerate(order):
                rb.append(base + b)
                sq.append(s)
                fi.append(int(pos == 0))
                la.append(int(pos == nblk - 1))
        rowblk.append(rb); seq.append(sq); first.append(fi); last.append(la)
    as_np = lambda a: np.asarray(a, np.int32)
    return as_np(rowblk), as_np(seq), as_np(first), as_np(last)


def _gla_kernel(rowblk_ref, seq_ref, first_ref, last_ref,
                qf_ref, kf_ref, vf_ref, lrf_ref, qb_ref, kb_ref, vb_ref, lrb_ref,
                wgk_ref, bgk_ref, up_ref, lev_ref, s0_ref,
                of_ref, ob_ref, fin_ref, z_scr, st_scr):
    del rowblk_ref, seq_ref
    n = pl.program_id(0)

    @pl.when(first_ref[n] == 1)
    def _():
        st_scr[...] = jnp.zeros_like(st_scr)
        for d in range(2):
            for h in range(GLA_HEADS):
                st_scr[d, h * GLA_DK:(h + 1) * GLA_DK, h * GLA_DV:(h + 1) * GLA_DV] = s0_ref[d, h]

    _gla_block(False, qf_ref, kf_ref, vf_ref, lrf_ref, wgk_ref.at[0], bgk_ref.at[0], up_ref.at[0], lev_ref.at[0],
               of_ref, z_scr.at[0], st_scr.at[0])
    _gla_block(True, qb_ref, kb_ref, vb_ref, lrb_ref, wgk_ref.at[1], bgk_ref.at[1], up_ref.at[1], lev_ref.at[1],
               ob_ref, z_scr.at[1], st_scr.at[1])

    @pl.when(last_ref[n] == 1)
    def _():
        for d in range(2):
            for h in range(GLA_HEADS):
                fin_ref[d, h] = st_scr[d, h * GLA_DK:(h + 1) * GLA_DK, h * GLA_DV:(h + 1) * GLA_DV]


def _gla_block(backward, q_ref, k_ref, v_ref, lr_ref, wgk_ref, bgk_ref, up_ref, lev_ref, o_ref, z_scr, st_scr):
    nl = GLA_LEVELS
    blk = GLA_BLK
    q = q_ref[...] * (GLA_DK ** -0.5)
    k = k_ref[...]
    vb = v_ref[...].astype(BF16)
    x = _dot(lr_ref[...].astype(BF16), wgk_ref[...]) + bgk_ref[...]
    gk = (jnp.minimum(x, 0.0) - jnp.log(1.0 + jnp.exp(-jnp.abs(x)))) * (1.0 / GLA_NORMALIZER)
    g_hi = gk.astype(BF16)
    g_lo = (gk - g_hi.astype(F32)).astype(BF16)
    ones = jnp.ones((blk, 128), BF16)
    tot = _dot_tn(g_hi, ones) + _dot_tn(g_lo, ones)

    row = lax.broadcasted_iota(jnp.int32, (blk, 1), 0)

    def sibling(a, l):
        g = 1 << l
        if g < 8:
            a3 = a.reshape(blk // 8, 8, a.shape[-1])
            dn = pltpu.roll(a3, g, 1).reshape(a.shape)
            up_ = pltpu.roll(a3, 8 - g, 1).reshape(a.shape)
            return jnp.where(((row >> l) & 1) == 1, dn, up_)
        a4 = a.reshape(blk // (2 * g), 2, g, a.shape[-1])
        return jnp.concatenate([a4[:, 1:2], a4[:, 0:1]], axis=1).reshape(a.shape)

    part = gk
    total = gk
    z0 = None
    for l in range(nl):
        g = 1 << l
        if g < 8:
            up = up_ref[:, l:l + 1] != 0
            z = jnp.where(up, q, k) * jnp.exp(jnp.where(up, part, total - part))
            other = sibling(total, l)
            part = part + jnp.where(up, other, 0.0)
            total = total + other
        else:
            halves = lambda a: (a.reshape(blk // (2 * g), 2, g, a.shape[-1])[:, 1 - int(backward)],
                                a.reshape(blk // (2 * g), 2, g, a.shape[-1])[:, int(backward)])
            join = lambda u, d: jnp.stack([d, u] if not backward else [u, d], axis=1).reshape(blk, u.shape[-1])
            part_u, part_d = halves(part)
            tot_u, tot_d = halves(total)
            q_u, _ = halves(q)
            _, k_d = halves(k)
            z = join(q_u * jnp.exp(part_u), k_d * jnp.exp(tot_d - part_d))
            part = join(part_u + tot_d, part_d)
            both = tot_u + tot_d
            total = join(both, both)
        if l == 0:
            z0 = z
        else:
            z_scr[l] = z.astype(BF16)
    lane128 = lax.broadcasted_iota(jnp.int32, (GLA_QK, 128), 1)
    dim = lax.broadcasted_iota(jnp.int32, (GLA_QK, 128), 0)
    head_sum = ((dim >> 6) == lane128).astype(BF16)
    pair0 = _dot((z0 * sibling(z0, 0)).astype(BF16), head_sum)
    diag = _dot((q * k).astype(BF16), head_sum)

    half = blk // 2
    lev_d = lev_ref[0]
    lev_o = lev_ref[1]
    lane = lax.broadcasted_iota(jnp.int32, (half, GLA_QK), 1)

    def tiles(l, in_head, crossed):
        out = []
        for r in range(2):
            c = 1 - r if crossed else r
            lhs = z_scr[l, r * half:(r + 1) * half, :]
            keys = z_scr[l, c * half:(c + 1) * half, :]
            out.append(_dot_nt(lhs, jnp.where(in_head, keys, jnp.zeros_like(keys))))
        return jnp.concatenate(out, axis=0)

    upi = 0 if backward else 1
    key_lanes = {}
    for l in range(3, nl - 1):
        g = 1 << l
        c = lax.broadcasted_iota(jnp.int32, (blk // (2 * g), g, 128), 0)
        ln = lax.broadcasted_iota(jnp.int32, (blk // (2 * g), g, 128), 2)
        base = (2 * g * c + (g if backward else 0)) & 127
        key_lanes[l] = (ln >= base) & (ln < base + g)

    for h in range(GLA_HEADS):
        in_head = (lane >= h * GLA_DK) & (lane < (h + 1) * GLA_DK)
        acc = jnp.where(lev_d == nl, diag[:, h:h + 1], 0.0)
        acc = jnp.where(lev_d == 0, pair0[:, h:h + 1], acc)
        for l in range(1, 3):
            acc = jnp.where(lev_d == l, tiles(l, in_head, False), acc)
        for l in range(3, nl - 1):
            g = 1 << l
            acc4 = acc.reshape(blk // (2 * g), 2, g, 128)
            s4 = tiles(l, in_head, False).reshape(blk // (2 * g), 2, g, 128)
            new_up = jnp.where(key_lanes[l], s4[:, upi], acc4[:, upi])
            pieces = [acc4[:, 0], new_up] if upi == 1 else [new_up, acc4[:, 1]]
            acc = jnp.stack(pieces, axis=1).reshape(blk, 128)
        off = jnp.where(lev_o == nl - 1, tiles(nl - 1, in_head, True), 0.0)
        att = jnp.concatenate([jnp.concatenate([acc[:half], off[:half]], axis=1),
                               jnp.concatenate([off[half:], acc[half:]], axis=1)], axis=0)
        o_ref[:, h * GLA_DV:(h + 1) * GLA_DV] = _dot(att.astype(BF16), vb[:, h * GLA_DV:(h + 1) * GLA_DV])

    st = st_scr[...]
    q_in = (q * jnp.exp(part)).astype(BF16)
    o_ref[...] += _dot(q_in, st.astype(BF16))
    k_out = (k * jnp.exp(total - part)).astype(BF16)
    kv = _dot_tn(k_out, vb)
    row = lax.broadcasted_iota(jnp.int32, (GLA_QK, GLA_V), 0)
    col = lax.broadcasted_iota(jnp.int32, (GLA_QK, GLA_V), 1)
    same_head = (row >> 6) == (col >> 7)
    decay = jnp.exp(tot)
    decay = jnp.concatenate([decay] * GLA_HEADS, axis=1)
    st_new = decay * st + jnp.where(same_head, kv, 0.0)
    st_scr[...] = st_new


def _gla_mix(bslab, lr, wgk, bgk, s0):
    up, lev = _gla_consts()
    rowblk, seq, first, last = _gla_tables()
    nsteps = rowblk.shape[1]
    nseq = BATCH + DEC_BATCH
    nl = GLA_LEVELS
    whole = lambda shape: pl.BlockSpec(shape, lambda n, rb, sq, fi, la: (0,) * len(shape))

    def token_specs(d):
        return [
            pl.BlockSpec((GLA_BLK, GLA_QK), lambda n, rb, sq, fi, la: (rb[d, n], 0)),
            pl.BlockSpec((GLA_BLK, GLA_QK), lambda n, rb, sq, fi, la: (rb[d, n], 1)),
            pl.BlockSpec((GLA_BLK, GLA_V), lambda n, rb, sq, fi, la: (rb[d, n], 1)),
            pl.BlockSpec((GLA_BLK, 128), lambda n, rb, sq, fi, la: (rb[d, n], 0)),
        ]

    state_spec = pl.BlockSpec((None, 2, GLA_HEADS, GLA_DK, GLA_DV), lambda n, rb, sq, fi, la: (sq[n], 0, 0, 0, 0))
    grid_spec = pltpu.PrefetchScalarGridSpec(
        num_scalar_prefetch=4,
        grid=(nsteps,),
        in_specs=token_specs(0) + token_specs(1) + [
            whole((2, 128, GLA_QK)),
            whole((2, 1, GLA_QK)),
            whole((2, GLA_BLK, 128)),
            whole((2, 2, GLA_BLK, GLA_BLK // 2)),
            state_spec,
        ],
        out_specs=[
            pl.BlockSpec((GLA_BLK, GLA_V), lambda n, rb, sq, fi, la: (rb[0, n], 0)),
            pl.BlockSpec((GLA_BLK, GLA_V), lambda n, rb, sq, fi, la: (rb[1, n], 0)),
            state_spec,
        ],
        scratch_shapes=[
            pltpu.VMEM((2, nl, GLA_BLK, GLA_QK), BF16),
            pltpu.VMEM((2, GLA_QK, GLA_V), F32),
        ],
    )
    return pl.pallas_call(
        _gla_kernel,
        grid_spec=grid_spec,
        out_shape=[
            jax.ShapeDtypeStruct((NTOK, GLA_V), F32),
            jax.ShapeDtypeStruct((NTOK, GLA_V), F32),
            jax.ShapeDtypeStruct((nseq, 2, GLA_HEADS, GLA_DK, GLA_DV), F32),
        ],
        compiler_params=pltpu.CompilerParams(vmem_limit_bytes=VMEM_LIMIT),
        name="gla_mix",
    )(jnp.asarray(rowblk), jnp.asarray(seq[0]), jnp.asarray(first[0]), jnp.asarray(last[0]),
      bslab, bslab, bslab, lr, bslab, bslab, bslab, lr, wgk, bgk, jnp.asarray(up), jnp.asarray(lev), s0)


def _softmax_head(s_list, v_list, sink):
    m = sink
    for s in s_list:
        m = jnp.maximum(m, jnp.max(s, axis=-1, keepdims=True))
    den = jnp.exp(sink - m)
    acc = None
    for s, v in zip(s_list, v_list):
        p = jnp.exp(s - m)
        den = den + jnp.sum(p, axis=-1, keepdims=True)
        pv = _dot(p.astype(BF16), v)
        acc = pv if acc is None else acc + pv
    return acc / den


def _attn_ctx_kernel(sink_ref, q_ref, k_ref, v_ref, o_ref):
    k = k_ref[...]
    v = v_ref[...]
    ks = (k.astype(BF16), pltpu.roll(k, 64, 1).astype(BF16))
    vs = (v.astype(BF16), pltpu.roll(v, 64, 1).astype(BF16))
    lo = lax.broadcasted_iota(jnp.int32, (SEQ, 128), 1) < HEAD_DIM
    for t in range(ATT_HEADS // 2):
        qt = q_ref[:, t * 128:(t + 1) * 128] * (HEAD_DIM ** -0.5)
        kvh = t // 2
        outs = []
        for p in range(2):
            sel = lo if p == 0 else jnp.logical_not(lo)
            qm = jnp.where(sel, qt, 0.0).astype(BF16)
            which = 0 if p == kvh else 1
            s = _dot_nt(qm, ks[which])
            outs.append(_softmax_head([s], [vs[which]], sink_ref[2 * t + p]))
        o_ref[:, t * 128:(t + 1) * 128] = jnp.where(lo, outs[0], outs[1])


def _attn_ctx(sink, cslab):
    return pl.pallas_call(
        _attn_ctx_kernel,
        grid=(BATCH,),
        in_specs=[
            pl.BlockSpec(memory_space=pltpu.SMEM),
            pl.BlockSpec((SEQ, ATT_Q), lambda b: (b, 0)),
            pl.BlockSpec((SEQ, ATT_KV), lambda b: (b, 4)),
            pl.BlockSpec((SEQ, ATT_KV), lambda b: (b, 5)),
        ],
        out_specs=pl.BlockSpec((SEQ, ATT_Q), lambda b: (b, 0)),
        out_shape=jax.ShapeDtypeStruct((NTOK_C, ATT_Q), F32),
        name="attn_ctx",
    )(sink, cslab, cslab, cslab)


def _attn_lat_kernel(sink_ref, q_ref, kp_ref, kc_ref, kn_ref, vp_ref, vc_ref, vn_ref,
                     ck_ref, cv_ref, cos_ref, sin_ref, bias_ref, o_ref):
    j = pl.program_id(1)
    nb = DEC_SEQ // ATT_BLOCK
    lane = lax.broadcasted_iota(jnp.int32, (ATT_BLOCK, 128), 1)
    lo = lane < HEAD_DIM
    first16 = (lane & 31) < 16

    def rope(x, blk_idx):
        r0 = pl.multiple_of(blk_idx * ATT_BLOCK, ATT_BLOCK)
        c = cos_ref[pl.ds(r0, ATT_BLOCK), :]
        s = sin_ref[pl.ds(r0, ATT_BLOCK), :]
        xs = jnp.where(first16, pltpu.roll(x, 112, 1), pltpu.roll(x, 16, 1))
        return x * c + xs * s

    nwin = 3 * ATT_BLOCK
    keys = jnp.concatenate([rope(kp_ref[...], jnp.maximum(j - 1, 0)), rope(kc_ref[...], j),
                            rope(kn_ref[...], jnp.minimum(j + 1, nb - 1)), ck_ref[...]], axis=0)
    vals = jnp.concatenate([vp_ref[...], vc_ref[...], vn_ref[...], cv_ref[...]], axis=0)
    keys2 = (keys.astype(BF16), pltpu.roll(keys, 64, 1).astype(BF16))
    vals2 = (vals.astype(BF16), pltpu.roll(vals, 64, 1).astype(BF16))
    kcol = lax.broadcasted_iota(jnp.int32, (1, nwin + PAST_LEN), 1)
    edge = jnp.where(((j == 0) & (kcol < ATT_BLOCK)) | ((j == nb - 1) & (kcol >= 2 * ATT_BLOCK) & (kcol < nwin)),
                     -1e30, 0.0)
    bias = bias_ref[...] + edge
    top = lax.broadcasted_iota(jnp.int32, (2 * ATT_BLOCK, 1), 0) < ATT_BLOCK
    q_tiles = [rope(q_ref[:, t * 128:(t + 1) * 128], j) * (HEAD_DIM ** -0.5) for t in range(ATT_HEADS // 2)]
    for kvh in range(ATT_KV_HEADS):
        q2 = jnp.concatenate(q_tiles[2 * kvh:2 * kvh + 2], axis=0)
        lo2 = jnp.concatenate([lo, lo], axis=0)
        outs = []
        for p in range(2):
            qm = jnp.where(lo2 if p == 0 else jnp.logical_not(lo2), q2, 0.0).astype(BF16)
            which = 0 if p == kvh else 1
            sink = jnp.where(top, sink_ref[4 * kvh + p], sink_ref[4 * kvh + 2 + p])
            s = _dot_nt(qm, keys2[which]) + bias
            outs.append(_softmax_head([s], [vals2[which]], sink))
        o2 = jnp.where(lo2, outs[0], outs[1])
        for i in range(2):
            t = 2 * kvh + i
            o_ref[:, t * 128:(t + 1) * 128] = o2[i * ATT_BLOCK:(i + 1) * ATT_BLOCK]


def _attn_lat(sink, cslab, ck, cv, cos_t, sin_t):
    nb = DEC_SEQ // ATT_BLOCK
    base = NTOK_C // ATT_BLOCK
    cur = lambda b, j: base + b * nb + j
    prv = lambda b, j: base + b * nb + jnp.maximum(j - 1, 0)
    nxt = lambda b, j: base + b * nb + jnp.minimum(j + 1, nb - 1)
    kv_spec = lambda row, col: pl.BlockSpec((ATT_BLOCK, ATT_KV), lambda b, j: (row(b, j), col))
    qi = np.arange(2 * ATT_BLOCK)[:, None] % ATT_BLOCK
    kc = np.arange(3 * ATT_BLOCK + PAST_LEN)[None, :]
    inside = (np.abs(kc - ATT_BLOCK - qi) <= WINDOW) | (kc >= 3 * ATT_BLOCK)
    band = np.where(inside, 0.0, -1e30).astype(np.float32)
    return pl.pallas_call(
        _attn_lat_kernel,
        grid=(DEC_BATCH, nb),
        in_specs=[
            pl.BlockSpec(memory_space=pltpu.SMEM),
            pl.BlockSpec((ATT_BLOCK, ATT_Q), lambda b, j: (cur(b, j), 0)),
            kv_spec(prv, 4), kv_spec(cur, 4), kv_spec(nxt, 4),
            kv_spec(prv, 5), kv_spec(cur, 5), kv_spec(nxt, 5),
            pl.BlockSpec((None, PAST_LEN, ATT_KV), lambda b, j: (b, 0, 0)),
            pl.BlockSpec((None, PAST_LEN, ATT_KV), lambda b, j: (b, 0, 0)),
            pl.BlockSpec((DEC_SEQ, 128), lambda b, j: (0, 0)),
            pl.BlockSpec((DEC_SEQ, 128), lambda b, j: (0, 0)),
            pl.BlockSpec(band.shape, lambda b, j: (0, 0)),
        ],
        out_specs=pl.BlockSpec((ATT_BLOCK, ATT_Q), lambda b, j: (b * nb + j, 0)),
        out_shape=jax.ShapeDtypeStruct((NTOK_L, ATT_Q), F32),
        name="attn_lat",
    )(sink, cslab, cslab, cslab, cslab, cslab, cslab, cslab, ck, cv, cos_t, sin_t, jnp.asarray(band))


def _rope_tables():
    rows = DEC_SEQ // GRID_W
    row = np.repeat(np.arange(rows, dtype=np.float32), GRID_W)
    col = np.tile(np.arange(GRID_W, dtype=np.float32), rows)
    quarter = HEAD_DIM // 4
    inv = jnp.asarray(ROPE_BASE, F32) ** (-jnp.arange(quarter, dtype=F32) / quarter)
    lane = np.arange(128)
    use_row = (lane % HEAD_DIM) < HEAD_DIM // 2
    pos = jnp.where(use_row[None, :], jnp.asarray(row)[:, None], jnp.asarray(col)[:, None])
    ang = pos * inv[lane % quarter][None, :]
    sign = np.where((lane % 32) < 16, -1.0, 1.0).astype(np.float32)
    return jnp.cos(ang), jnp.sin(ang) * sign[None, :]


def _merge_kernel(*refs, split_x):
    if split_x:
        xc_ref, xl_ref, *refs = refs
    else:
        xc_ref, *refs = refs
    (mod_ref, g_ref, ys5_ref, ogf_ref, ogb_ref, gb_ref, ycc_ref, ycl_ref, gate_ref, gng_ref,
     wglu_ref, wbr_ref, wout_ref, o_ref) = refs
    is_ctx = pl.program_id(0) < NTOK_C // TM
    if split_x:
        x = jnp.where(is_ctx, xc_ref[...], xl_ref[...])
    else:
        x = xc_ref[...]
    y_c = jnp.where(is_ctx, ycc_ref[...], ycl_ref[...])
    y = jnp.concatenate([ys5_ref[j] for j in range(S5_SLABS)], axis=1)
    y = 0.5 * y * (1.0 + jnp.tanh(math.sqrt(2.0 / math.pi) * (y + 0.044715 * (y * y * y))))
    ag = _dot(y.astype(BF16), wglu_ref[...])
    y_a = ag[:, :S5_WIDTH] * _sigmoid(ag[:, S5_WIDTH:])
    gng = gng_ref[...]
    gb = gb_ref[...]
    parts = []
    for h in range(GLA_HEADS):
        sl = slice(h * GLA_DV, (h + 1) * GLA_DV)
        o = ogf_ref[:, sl] + ogb_ref[:, sl]
        g = gb[:, sl]
        parts.append(_rms(o, gng) * (g * _sigmoid(g)))
    y_b = jnp.concatenate(parts, axis=1)
    merged = None
    for n, yn in enumerate((y_a, y_b, y_c)):
        proj = _dot(yn.astype(BF16), wbr_ref[n])
        term = _sigmoid(gate_ref[:, n * D_MODEL:(n + 1) * D_MODEL]) * proj
        merged = term if merged is None else merged + term
    mixed = _dot(merged.astype(BF16), wout_ref[...])
    g1 = mod_ref[:, 2 * D_MODEL:3 * D_MODEL]
    o_ref[...] = x + g1 * _rms(mixed, g_ref[...])


def _layer_spec(shape, layer):
    return pl.BlockSpec((None,) + shape, lambda i: (layer,) + (0,) * len(shape), pipeline_mode=pl.Buffered(1))


def _split_token_specs(n_arrays, width=D_MODEL):
    nct = NTOK_C // TM
    if n_arrays == 2:
        return [pl.BlockSpec((TM, width), lambda i: (jnp.minimum(i, nct - 1), 0)),
                pl.BlockSpec((TM, width), lambda i: (jnp.maximum(i - nct, 0), 0))]
    return [pl.BlockSpec((TM, width), lambda i: (i, 0))]


def _merge(xs, mod, g, ys5, og, bslab, yc, gates, gng, wglu, wbr, wout, layer):
    tok = lambda width, col=0: pl.BlockSpec((TM, width), lambda i: (i, col))
    full = lambda shape: _layer_spec(shape, layer)
    return pl.pallas_call(
        functools.partial(_merge_kernel, split_x=len(xs) == 2),
        grid=(NTOK // TM,),
        in_specs=_split_token_specs(len(xs)) + [
            pl.BlockSpec((None, 1, 6 * D_MODEL), lambda i: (_mod_row(i), 0, 0)),
            pl.BlockSpec((1, D_MODEL), lambda i: (0, 0)),
            pl.BlockSpec((S5_SLABS, TM, 128), lambda i: (0, i, 0)),
            tok(GLA_V),
            tok(GLA_V),
            tok(GLA_V, 2),
        ] + _split_token_specs(2, ATT_Q) + [
            tok(N_BRANCH * D_MODEL),
            pl.BlockSpec((1, GLA_DV), lambda i: (0, 0)),
            full((S5_WIDTH, 2 * S5_WIDTH)),
            full((N_BRANCH, BRANCH_W, D_MODEL)),
            full((D_MODEL, D_MODEL)),
        ],
        out_specs=tok(D_MODEL),
        out_shape=jax.ShapeDtypeStruct((NTOK, D_MODEL), F32),
        compiler_params=pltpu.CompilerParams(vmem_limit_bytes=VMEM_LIMIT),
        name="merge",
    )(*xs, mod, g, ys5, *og, bslab, *yc, gates, gng, wglu, wbr, wout)


FFN_SPLIT = 2


def _ffn_kernel(x_ref, mod_ref, gin_ref, gout_ref, w1_ref, w2_ref, *o_refs):
    x = x_ref[...]
    sh = mod_ref[:, 3 * D_MODEL:4 * D_MODEL]
    sc = mod_ref[:, 4 * D_MODEL:5 * D_MODEL]
    g2 = mod_ref[:, 5 * D_MODEL:6 * D_MODEL]
    h = (_rms(x, gin_ref[...]) * (1.0 + sc) + sh).astype(BF16)
    ck = FFN_HIDDEN // FFN_SPLIT
    acc = None
    for c in range(FFN_SPLIT):
        a = _dot(h, w1_ref[:, c * ck:(c + 1) * ck])
        b = _dot(h, w1_ref[:, FFN_HIDDEN + c * ck:FFN_HIDDEN + (c + 1) * ck])
        act = (a * _sigmoid(a) * b).astype(BF16)
        part = _dot(act, w2_ref[c * ck:(c + 1) * ck, :])
        acc = part if acc is None else acc + part
    y = x + g2 * _rms(acc, gout_ref[...])
    if len(o_refs) == 1:
        o_refs[0][...] = y
    else:
        is_ctx = pl.program_id(0) < NTOK_C // TM

        @pl.when(is_ctx)
        def _():
            o_refs[0][...] = y

        @pl.when(jnp.logical_not(is_ctx))
        def _():
            o_refs[1][...] = y


def _ffn(x, mod, gin, gout, w1, w2, layer, split_out):
    small = lambda shape: pl.BlockSpec(shape, lambda i: (0,) * len(shape))
    nct = NTOK_C // TM
    if split_out:
        out_specs = [pl.BlockSpec((TM, D_MODEL), lambda i: (jnp.minimum(i, nct - 1), 0)),
                     pl.BlockSpec((TM, D_MODEL), lambda i: (jnp.maximum(i - nct, 0), 0))]
        out_shape = [jax.ShapeDtypeStruct((NTOK_C, D_MODEL), F32), jax.ShapeDtypeStruct((NTOK_L, D_MODEL), F32)]
    else:
        out_specs = pl.BlockSpec((TM, D_MODEL), lambda i: (i, 0))
        out_shape = jax.ShapeDtypeStruct((NTOK, D_MODEL), F32)
    return pl.pallas_call(
        _ffn_kernel,
        grid=(NTOK // TM,),
        in_specs=[
            pl.BlockSpec((TM, D_MODEL), lambda i: (i, 0)),
            pl.BlockSpec((None, 1, 6 * D_MODEL), lambda i: (_mod_row(i), 0, 0)),
            small((1, D_MODEL)),
            small((1, D_MODEL)),
            _layer_spec((D_MODEL, 2 * FFN_HIDDEN), layer),
            _layer_spec((FFN_HIDDEN, D_MODEL), layer),
        ],
        out_specs=out_specs,
        out_shape=out_shape,
        compiler_params=pltpu.CompilerParams(vmem_limit_bytes=VMEM_LIMIT),
        name="ffn",
    )(x, mod, gin, gout, w1, w2)


def kernel(x_prompt, x_sample, cache_k, cache_v, state_s5, state_gla, c, c_ctx, w_mod, b_mod, norm_g, w_in,
           s5_lam_re, s5_lam_im, s5_log_step, s5_b_re, s5_b_im, s5_c_re, s5_c_im, s5_d, w_glu, gla_w_gk,
           gla_b_gk, gla_norm_g, att_sink, w_branch, w_out, w_ffn_in, w_ffn_out):
    cond = jnp.concatenate([c_ctx[None, :], c, jnp.zeros((N_MOD_ROWS - 1 - DEC_BATCH, D_MODEL), F32)], axis=0)
    mod_all = _modulation(cond, w_mod, b_mod).reshape(DEPTH, N_MOD_ROWS, 1, 6 * D_MODEL)
    cos_t, sin_t = _rope_tables()
    xs = (x_prompt.reshape(NTOK_C, D_MODEL), x_sample.reshape(NTOK_L, D_MODEL))
    w_in_b = w_in.astype(BF16)
    full_cols = D_IN // 128 * 128
    w_in_end = jnp.pad(w_in[:, :, full_cols:].astype(BF16), ((0, 0), (0, 0), (0, W_IN_COLS - D_IN)))
    w_glu_b, w_branch_b, w_out_b = w_glu.astype(BF16), w_branch.astype(BF16), w_out.astype(BF16)
    w_ffn_in_b, w_ffn_out_b = w_ffn_in.astype(BF16), w_ffn_out.astype(BF16)
    new_k, new_v, new_s5, new_gla = [], [], [], []
    for i in range(DEPTH):
        mod = mod_all[i]
        uj, bslab, cslab, gates, lr = _inproj(xs, mod, norm_g[i, 0][None, :], w_in_b, w_in_end, i)

        wt, web, wca, a16, dj = _s5_prep(s5_lam_re[i], s5_lam_im[i], s5_log_step[i], s5_b_re[i], s5_b_im[i],
                                         s5_c_re[i], s5_c_im[i], s5_d[i])
        h0l = state_s5[:, i].astype(F32).transpose(0, 2, 4, 1, 3).reshape(DEC_BATCH, S5_GROUPS * 256)
        hin, finc = _s5_scan(_s5_state(uj, web), a16, h0l)
        ys5 = _s5_out(uj, hin, wt, wca, dj)
        new_s5.append(finc.reshape(BATCH, S5_GROUPS, 2, 2, S5_STATE).transpose(0, 3, 1, 4, 2))

        wgk = jnp.zeros((2, 128, GLA_QK), F32)
        wgk = wgk.at[0, 0:GLA_RANK].set(gla_w_gk[i, 0]).at[1, GLA_RANK:2 * GLA_RANK].set(gla_w_gk[i, 1])
        s0 = jnp.concatenate([jnp.zeros((BATCH, 2, GLA_HEADS, GLA_DK, GLA_DV), F32),
                              state_gla[:, i].astype(F32)], axis=0)
        *og, gla_fin = _gla_mix(bslab, lr, wgk.astype(BF16), gla_b_gk[i][:, None, :].astype(F32), s0)
        new_gla.append(gla_fin[:BATCH])

        sink = att_sink[i].astype(F32)
        yc = (_attn_ctx(sink, cslab),
              _attn_lat(sink, cslab, cache_k[:, i].reshape(DEC_BATCH, PAST_LEN, ATT_KV).astype(F32),
                        cache_v[:, i].reshape(DEC_BATCH, PAST_LEN, ATT_KV).astype(F32), cos_t, sin_t))
        new_k.append(cslab[:NTOK_C, ATT_Q:ATT_Q + ATT_KV].reshape(BATCH, SEQ, ATT_KV_HEADS, HEAD_DIM))
        new_v.append(cslab[:NTOK_C, ATT_Q + ATT_KV:].reshape(BATCH, SEQ, ATT_KV_HEADS, HEAD_DIM))

        x = _merge(xs, mod, norm_g[i, 1][None, :], ys5, og, bslab, yc, gates, gla_norm_g[i][None, :],
                   w_glu_b, w_branch_b, w_out_b, i)
        last = i == DEPTH - 1
        x = _ffn(x, mod, norm_g[i, 2][None, :], norm_g[i, 3][None, :], w_ffn_in_b, w_ffn_out_b, i, last)
        xs = tuple(x) if last else (x,)

    return (xs[0].reshape(BATCH, SEQ, D_MODEL), xs[1].reshape(DEC_BATCH, DEC_SEQ, D_MODEL),
            jnp.stack(new_k, axis=1), jnp.stack(new_v, axis=1),
            jnp.stack(new_s5, axis=1), jnp.stack(new_gla, axis=1))
```

```python
import functools
import math

import numpy as np
import jax
import jax.numpy as jnp
from jax import lax
from jax.experimental import pallas as pl
from jax.experimental.pallas import tpu as pltpu

F32 = jnp.float32
BF16 = jnp.bfloat16

D_MODEL = 1024
BATCH = 16
SEQ = 256
DEPTH = 2
DEC_BATCH = 8
DEC_SEQ = 1024
PAST_LEN = 256
GRID_W = 64
ROPE_BASE = 10000.0
S5_WIDTH = 512
S5_GROUP = 16
S5_GROUPS = 32
S5_STATE = 64
GLA_HEADS = 4
GLA_DK = 64
GLA_DV = 128
GLA_QK = 256
GLA_V = 512
GLA_RANK = 16
GLA_NORMALIZER = 16.0
ATT_HEADS = 8
ATT_KV_HEADS = 2
HEAD_DIM = 64
ATT_Q = 512
ATT_KV = 128
WINDOW = 128
ATT_BLOCK = 128
N_BRANCH = 3
BRANCH_W = 512
FFN_HIDDEN = 2816
RMS_EPS = 1e-6

NTOK_C = BATCH * SEQ
NTOK_L = DEC_BATCH * DEC_SEQ
NTOK = NTOK_C + NTOK_L
TM = 512
N_MOD_ROWS = 16

D_IN = 5920
D_IN_TILED = D_IN // 128 * 128
W_IN_COLS = 6016
S5_CHUNK = 16
S5_SLABS = S5_WIDTH // 128
S5_SLAB_W = S5_CHUNK * 128
S5_ROWS_C = NTOK_C // S5_CHUNK
S5_ROWS = NTOK // S5_CHUNK
S5_ROW_TILE = 256
GLA_BLK = 256
GLA_LEVELS = 8
VMEM_LIMIT = 56 * 1024 * 1024


def _dot(a, b):
    return jnp.dot(a, b, preferred_element_type=F32)


def _dot_nt(a, b):
    return lax.dot_general(a, b, (((1,), (1,)), ((), ())), preferred_element_type=F32)


def _dot_tn(a, b):
    return lax.dot_general(a, b, (((0,), (0,)), ((), ())), preferred_element_type=F32)


def _rms(x, g):
    return x * lax.rsqrt(jnp.mean(x * x, axis=-1, keepdims=True) + RMS_EPS) * g


def _sigmoid(x):
    return 0.5 * jnp.tanh(0.5 * x) + 0.5


def _mod_row(i):
    nct = NTOK_C // TM
    return jnp.where(i < nct, 0, 1 + (i - nct) // (DEC_SEQ // TM))


def _mod_kernel(c_ref, w_ref, b_ref, o_ref):
    c = c_ref[...]
    s = (c * _sigmoid(c)).astype(BF16)
    o_ref[...] = _dot(s, w_ref[...].astype(BF16)) + b_ref[...]


def _modulation(cond, w_mod, b_mod):
    tn = 2048
    return pl.pallas_call(
        _mod_kernel,
        grid=(DEPTH, 6 * D_MODEL // tn),
        in_specs=[
            pl.BlockSpec((N_MOD_ROWS, D_MODEL), lambda l, n: (0, 0)),
            pl.BlockSpec((None, D_MODEL, tn), lambda l, n: (l, 0, n)),
            pl.BlockSpec((None, 1, tn), lambda l, n: (l, 0, n)),
        ],
        out_specs=pl.BlockSpec((None, N_MOD_ROWS, tn), lambda l, n: (l, 0, n)),
        out_shape=jax.ShapeDtypeStruct((DEPTH, N_MOD_ROWS, 6 * D_MODEL), F32),
        name="modulation",
    )(cond, w_mod, b_mod.reshape(DEPTH, 1, 6 * D_MODEL))


_IN_SLABS = ((0, 512), (512, 1536), (2048, 768), (2816, 3072), (5888, 128))
W_IN_SPLIT = 2048
W_IN_GAP = 32
W_IN_TAIL = W_IN_COLS - W_IN_SPLIT


def _inproj_kernel(*refs, split_x):
    if split_x:
        xc_ref, xl_ref, *refs = refs
    else:
        xc_ref, *refs = refs
    mod_ref, g_ref, w_ref, w_end_ref, u_ref, b_ref, c_ref, gate_ref, lr_ref, w_tail, u_stage = refs
    i = pl.program_id(0)

    @pl.when(i == 0)
    def _():
        r = lax.broadcasted_iota(jnp.int32, (256, 128), 0)
        c = lax.broadcasted_iota(jnp.int32, (256, 128), 1)
        shift = (r == c + W_IN_GAP).astype(BF16)
        head = ((r == c) & (c < W_IN_GAP)).astype(BF16)
        ntile = (W_IN_TAIL - 128) // 128
        for t in range(ntile - 1):
            src = W_IN_SPLIT + 128 * t
            w_tail[:, 128 * t:128 * (t + 1)] = _dot(w_ref[:, src:src + 256], shift).astype(BF16)
        src = W_IN_SPLIT + 128 * (ntile - 1)
        last = jnp.concatenate([w_ref[:, src:src + 128], w_end_ref[...]], axis=1)
        w_tail[:, 128 * (ntile - 1):128 * ntile] = _dot(last, shift).astype(BF16)
        w_tail[:, 128 * ntile:] = _dot(w_ref[:, W_IN_SPLIT:W_IN_SPLIT + 256], head).astype(BF16)

    if split_x:
        x = jnp.where(i < NTOK_C // TM, xc_ref[...], xl_ref[...])
    else:
        x = xc_ref[...]
    mod = mod_ref[...]
    h = _rms(x, g_ref[...]) * (1.0 + mod[:, D_MODEL:2 * D_MODEL]) + mod[:, 0:D_MODEL]
    h = h.astype(BF16)
    for j in range(S5_SLABS):
        u_stage[...] = _dot(h, w_ref[:, j * 128:(j + 1) * 128])
        for s in range(S5_CHUNK):
            u_ref[j, :, s * 128:(s + 1) * 128] = u_stage[pl.ds(s, TM // S5_CHUNK, stride=S5_CHUNK), :]
    b_ref[...] = _dot(h, w_ref[:, 512:W_IN_SPLIT])
    for (off, width), o_ref in zip(_IN_SLABS[2:], (c_ref, gate_ref, lr_ref)):
        o_ref[...] = _dot(h, w_tail[:, off - W_IN_SPLIT:off - W_IN_SPLIT + width])


def _inproj(xs, mod, g, w_all, w_end, layer):
    return pl.pallas_call(
        functools.partial(_inproj_kernel, split_x=len(xs) == 2),
        grid=(NTOK // TM,),
        in_specs=_split_token_specs(len(xs)) + [
            pl.BlockSpec((None, 1, 6 * D_MODEL), lambda i: (_mod_row(i), 0, 0)),
            pl.BlockSpec((1, D_MODEL), lambda i: (0, 0)),
            pl.BlockSpec((None, D_MODEL, D_IN_TILED), lambda i: (layer, 0, 0), pipeline_mode=pl.Buffered(1)),
            pl.BlockSpec((None, D_MODEL, 128), lambda i: (layer, 0, 0), pipeline_mode=pl.Buffered(1)),
        ],
        out_specs=[pl.BlockSpec((S5_SLABS, TM // S5_CHUNK, S5_SLAB_W), lambda i: (0, i, 0))]
        + [pl.BlockSpec((TM, width), lambda i: (i, 0)) for _, width in _IN_SLABS[1:]],
        out_shape=[jax.ShapeDtypeStruct((S5_SLABS, S5_ROWS, S5_SLAB_W), F32)]
        + [jax.ShapeDtypeStruct((NTOK, width), F32) for _, width in _IN_SLABS[1:]],
        scratch_shapes=[pltpu.VMEM((D_MODEL, W_IN_TAIL), BF16), pltpu.VMEM((TM, 128), F32)],
        compiler_params=pltpu.CompilerParams(vmem_limit_bytes=VMEM_LIMIT),
        name="inproj",
    )(*xs, mod, g, w_all, w_end)


@functools.lru_cache(maxsize=None)
def _s5_expanders():
    seg = 8
    spread = np.zeros((seg, 256, S5_SLAB_W), np.float32)
    place = np.zeros((seg, 256, S5_SLAB_W), np.float32)
    col = np.arange(256)
    for gl in range(seg):
        spread[gl, col, (col // S5_GROUP) * 128 + gl * S5_GROUP + col % S5_GROUP] = 1.0
        place[gl, col, gl * 256 + col] = 1.0
    return spread, place


def _s5_prep_kernel(par_ref, bre_ref, bim_ref, cre_ref, cim_ref, spread_ref, place_ref,
                    wt_ref, web_ref, wca_ref, a16_ref):
    n = S5_CHUNK
    lam_re = par_ref[0:1, :]
    lam_im = par_ref[1:2, :]
    dt = jnp.exp(par_ref[2:3, :])
    lr = lam_re * dt
    li = lam_im * dt
    krow = lax.broadcasted_iota(jnp.int32, (24, 128), 0).astype(F32)
    tab_mag = jnp.exp(krow * lr)
    tab_re = tab_mag * jnp.cos(krow * li)
    tab_im = tab_mag * jnp.sin(krow * li)
    ar = tab_re[1:2, :]
    ai = tab_im[1:2, :]
    nr = ar - 1.0
    den = lam_re * lam_re + lam_im * lam_im
    fr = (nr * lam_re + ai * lam_im) / den
    fi = (ai * lam_re - nr * lam_im) / den
    b_re = bre_ref[...]
    b_im = bim_ref[...]
    br = fr * b_re - fi * b_im
    bi = fr * b_im + fi * b_re
    c_re = cre_ref[...]
    c_im = cim_ref[...]

    def lo_half(shape):
        return lax.broadcasted_iota(jnp.int32, shape, 1) < S5_STATE

    def tile_rows(a):
        return jnp.concatenate([a] * n, axis=0)

    fwd16 = lo_half((S5_GROUP, 128))

    def powers(t_re, t_im, k_fwd, k_bwd):
        def pick(t, b):
            kf, kb = k_fwd(b), k_bwd(b)
            return jnp.where(fwd16, jnp.broadcast_to(t[kf:kf + 1, :], (S5_GROUP, 128)),
                             jnp.broadcast_to(t[kb:kb + 1, :], (S5_GROUP, 128)))
        return (jnp.concatenate([pick(t_re, b) for b in range(n)], axis=0),
                jnp.concatenate([pick(t_im, b) for b in range(n)], axis=0))

    fwd = lo_half((n * S5_GROUP, 128))
    brt, bit, crt, cit = tile_rows(br), tile_rows(bi), tile_rows(c_re), tile_rows(c_im)

    per, pei = powers(tab_re, tab_im, lambda s: n - 1 - s, lambda s: s)
    eb = jnp.concatenate([brt * per - bit * pei, brt * pei + bit * per], axis=1)
    pcr, pci = powers(tab_re, tab_im, lambda t: t + 1, lambda t: n - t)
    ca = jnp.concatenate([(crt * pcr - cit * pci).T, (-(crt * pci + cit * pcr)).T], axis=0)

    def one_dir(x, d):
        sw = pltpu.roll(x, S5_STATE, 1)
        lo = lo_half(x.shape)
        return jnp.where(lo, x, sw) if d == 0 else jnp.where(lo, sw, x)

    klag = []
    for d in range(2):
        lhs = jnp.where(lo_half(br.shape), one_dir(br, d), -one_dir(bi, d))
        crd, cid = tile_rows(one_dir(c_re, d)), tile_rows(one_dir(c_im, d))
        lag = (lambda b: b) if d == 0 else (lambda b: n - 1 - b)
        pr, pi = powers(one_dir(tab_re, d), one_dir(tab_im, d), lag, lag)
        rhs_t = jnp.where(fwd, crd * pr - cid * pi, crd * pi + cid * pr)
        klag.append(lax.dot_general(lhs, rhs_t, (((1,), (1,)), ((), ())),
                                    precision=lax.Precision.HIGHEST, preferred_element_type=F32))
    lane = lax.broadcasted_iota(jnp.int32, (S5_GROUP, n * S5_GROUP), 1)
    rows = []
    for s in range(n):
        f = klag[0] if s == 0 else jnp.where(lane >= S5_GROUP * s, pltpu.roll(klag[0], S5_GROUP * s, 1), 0.0)
        sh = (n * S5_GROUP - S5_GROUP * (n - 1 - s)) % (n * S5_GROUP)
        b = klag[1] if sh == 0 else pltpu.roll(klag[1], sh, 1)
        rows.append(f + jnp.where(lane < S5_GROUP * (s + 1), b, 0.0))
    toep = jnp.concatenate(rows, axis=0)

    spread = spread_ref[...]
    wt_ref[...] = _dot(toep.astype(BF16), spread).astype(BF16).reshape(n, S5_GROUP, S5_SLAB_W)
    web_ref[...] = _dot(eb.astype(BF16), place_ref[...]).astype(BF16).reshape(n, S5_GROUP, S5_SLAB_W)
    wca_ref[...] = _dot(ca.astype(BF16), spread).astype(BF16)
    a16_ref[0:1, :] = tab_re[n:n + 1, :]
    a16_ref[1:2, :] = tab_im[n:n + 1, :]


def _s5_prep(lam_re, lam_im, log_step, b_re, b_im, c_re, c_im, d_skip):
    seg = 8
    par = jnp.stack([lam_re, lam_im, log_step]).astype(F32).transpose(2, 0, 1, 3).reshape(S5_GROUPS, 3, 128)
    par = jnp.concatenate([par, jnp.zeros((S5_GROUPS, 5, 128), F32)], axis=1)
    b_t = lambda b: b.astype(F32).transpose(1, 3, 0, 2).reshape(S5_GROUPS, S5_GROUP, 128)
    c_t = lambda c: c.astype(F32).transpose(1, 2, 0, 3).reshape(S5_GROUPS, S5_GROUP, 128)
    spread, place = _s5_expanders()
    vec = pl.BlockSpec((None, S5_GROUP, 128), lambda gl, j: (j * seg + gl, 0, 0))
    exp_spec = pl.BlockSpec((None, 256, S5_SLAB_W), lambda gl, j: (gl, 0, 0))
    rows_spec = pl.BlockSpec((None, S5_CHUNK, None, S5_GROUP, S5_SLAB_W), lambda gl, j: (j, 0, gl, 0, 0))
    wt, web, wca, a16 = pl.pallas_call(
        _s5_prep_kernel,
        grid=(seg, S5_SLABS),
        in_specs=[pl.BlockSpec((None, 8, 128), lambda gl, j: (j * seg + gl, 0, 0)), vec, vec, vec, vec,
                  exp_spec, exp_spec],
        out_specs=[
            rows_spec, rows_spec,
            pl.BlockSpec((None, None, 256, S5_SLAB_W), lambda gl, j: (j, gl, 0, 0)),
            pl.BlockSpec((None, 2, 128), lambda gl, j: (j * seg + gl, 0, 0)),
        ],
        out_shape=[
            jax.ShapeDtypeStruct((S5_SLABS, S5_CHUNK, seg, S5_GROUP, S5_SLAB_W), BF16),
            jax.ShapeDtypeStruct((S5_SLABS, S5_CHUNK, seg, S5_GROUP, S5_SLAB_W), BF16),
            jax.ShapeDtypeStruct((S5_SLABS, seg, 256, S5_SLAB_W), BF16),
            jax.ShapeDtypeStruct((S5_GROUPS, 2, 128), F32),
        ],
        name="s5_prep",
    )(par, b_t(b_re), b_t(b_im), c_t(c_re), c_t(c_im), jnp.asarray(spread, BF16), jnp.asarray(place, BF16))
    mat = (S5_SLABS, S5_SLAB_W, S5_SLAB_W)
    dj = jnp.tile(d_skip.astype(F32).reshape(S5_SLABS, 1, 128), (1, 1, S5_CHUNK))
    return wt.reshape(mat), web.reshape(mat), wca.reshape(mat), a16.reshape(1, S5_SLABS * S5_SLAB_W), dj


S5_STATE_COLS = S5_SLABS * S5_SLAB_W // 128
S5_SLAB_COLS = S5_SLAB_W // 128


def _s5_state_kernel(u_ref, w_ref, o_ref):
    s = _dot(u_ref[...].astype(BF16), w_ref[...])
    for k in range(S5_SLAB_COLS):
        o_ref[k] = s[:, k * 128:(k + 1) * 128]


def _s5_state(uj, web):
    return pl.pallas_call(
        _s5_state_kernel,
        grid=(S5_SLABS, S5_ROWS // S5_ROW_TILE),
        in_specs=[
            pl.BlockSpec((None, S5_ROW_TILE, S5_SLAB_W), lambda j, p: (j, p, 0)),
            pl.BlockSpec((None, S5_SLAB_W, S5_SLAB_W), lambda j, p: (j, 0, 0)),
        ],
        out_specs=pl.BlockSpec((S5_SLAB_COLS, S5_ROW_TILE, 128), lambda j, p: (j, p, 0)),
        out_shape=jax.ShapeDtypeStruct((S5_STATE_COLS, S5_ROWS, 128), F32),
        compiler_params=pltpu.CompilerParams(vmem_limit_bytes=VMEM_LIMIT),
        name="s5_state",
    )(uj, web)


S5_SCAN_COLS = 8


def _s5_scan_kernel(s_ref, a_ref, h0_ref, hin_ref, fin_ref, sg, hf, hb):
    ncol = S5_SCAN_COLS

    def scan(row0, nc, nb, h0):
        is_f = lax.broadcasted_iota(jnp.int32, (nb, 128), 1) < S5_STATE
        chunk_rows = lambda c: pl.ds(pl.multiple_of(row0 + c * nb, 8), nb)

        def gather(c, carry):
            for k in range(ncol):
                sg[k, chunk_rows(c), :] = s_ref[k, pl.ds(row0 + c, nb, stride=nc), :]
            return carry

        lax.fori_loop(0, nc, gather, 0)

        def body(c, hs):
            rf = chunk_rows(c)
            rb = chunk_rows(nc - 1 - c)
            new = []
            for m in range(ncol // 2):
                h_re, h_im = hs[2 * m], hs[2 * m + 1]
                a_re = a_ref[:, (2 * m) * 128:(2 * m + 1) * 128]
                a_im = a_ref[:, (2 * m + 1) * 128:(2 * m + 2) * 128]
                loc = []
                for k, h in ((2 * m, h_re), (2 * m + 1, h_im)):
                    hf[k, rf, :] = h
                    hb[k, rb, :] = h
                    loc.append(jnp.where(is_f, sg[k, rf, :], sg[k, rb, :]))
                new.append(a_re * h_re - a_im * h_im + loc[0])
                new.append(a_re * h_im + a_im * h_re + loc[1])
            return tuple(new)

        fin = lax.fori_loop(0, nc, body, h0)

        def scatter(c, carry):
            for k in range(ncol):
                hin_ref[k, pl.ds(row0 + c, nb, stride=nc), :] = jnp.where(
                    is_f, hf[k, chunk_rows(c), :], hb[k, chunk_rows(c), :])
            return carry

        lax.fori_loop(0, nc, scatter, 0)
        return fin

    fin = scan(0, SEQ // S5_CHUNK, BATCH, tuple(jnp.zeros((BATCH, 128), F32) for _ in range(ncol)))
    for k in range(ncol):
        fin_ref[:, k * 128:(k + 1) * 128] = fin[k]
    scan(S5_ROWS_C, DEC_SEQ // S5_CHUNK, DEC_BATCH,
         tuple(h0_ref[:, k * 128:(k + 1) * 128] for k in range(ncol)))


def _s5_scan(sloc, a16, h0l):
    ncol = S5_SCAN_COLS
    w = ncol * 128
    return pl.pallas_call(
        _s5_scan_kernel,
        grid=(S5_STATE_COLS // ncol,),
        in_specs=[
            pl.BlockSpec((ncol, S5_ROWS, 128), lambda k: (k, 0, 0)),
            pl.BlockSpec((1, w), lambda k: (0, k)),
            pl.BlockSpec((DEC_BATCH, w), lambda k: (0, k)),
        ],
        out_specs=[
            pl.BlockSpec((ncol, S5_ROWS, 128), lambda k: (k, 0, 0)),
            pl.BlockSpec((BATCH, w), lambda k: (0, k)),
        ],
        out_shape=[
            jax.ShapeDtypeStruct((S5_STATE_COLS, S5_ROWS, 128), F32),
            jax.ShapeDtypeStruct((BATCH, S5_STATE_COLS * 128), F32),
        ],
        scratch_shapes=[pltpu.VMEM((ncol, S5_ROWS, 128), F32)] * 3,
        name="s5_scan",
    )(sloc, a16, h0l)


def _s5_out_kernel(u_ref, hin_ref, wt_ref, wca_ref, d_ref, y_ref):
    u = u_ref[...]
    hin = jnp.concatenate([hin_ref[k] for k in range(S5_SLAB_COLS)], axis=1).astype(BF16)
    y = _dot(u.astype(BF16), wt_ref[...]) + _dot(hin, wca_ref[...]) + u * d_ref[...]
    for t in range(S5_CHUNK):
        y_ref[pl.ds(t, S5_ROW_TILE, stride=S5_CHUNK), :] = y[:, t * 128:(t + 1) * 128]


def _s5_out(uj, hin, wt, wca, dj):
    return pl.pallas_call(
        _s5_out_kernel,
        grid=(S5_SLABS, S5_ROWS // S5_ROW_TILE),
        in_specs=[
            pl.BlockSpec((None, S5_ROW_TILE, S5_SLAB_W), lambda j, p: (j, p, 0)),
            pl.BlockSpec((S5_SLAB_COLS, S5_ROW_TILE, 128), lambda j, p: (j, p, 0)),
            pl.BlockSpec((None, S5_SLAB_W, S5_SLAB_W), lambda j, p: (j, 0, 0)),
            pl.BlockSpec((None, S5_SLAB_W, S5_SLAB_W), lambda j, p: (j, 0, 0)),
            pl.BlockSpec((None, 1, S5_SLAB_W), lambda j, p: (j, 0, 0)),
        ],
        out_specs=pl.BlockSpec((None, S5_ROW_TILE * S5_CHUNK, 128), lambda j, p: (j, p, 0)),
        out_shape=jax.ShapeDtypeStruct((S5_SLABS, NTOK, 128), F32),
        compiler_params=pltpu.CompilerParams(vmem_limit_bytes=VMEM_LIMIT),
        name="s5_out",
    )(uj, hin, wt, wca, dj)


@functools.lru_cache(maxsize=None)
def _gla_consts():
    n = GLA_BLK
    nl = GLA_LEVELS
    r = np.arange(n)
    up = np.zeros((n, 128), np.int32)
    for l in range(nl):
        up[:, l] = (r >> l) & 1
    i = r[:, None]
    j = r[None, :]
    x = np.maximum(i ^ j, 1)
    lev = np.where(j < i, np.floor(np.log2(x)).astype(np.int32), np.where(i == j, nl, -1)).astype(np.int32)
    up2 = np.stack([up, up[::-1]])
    h = n // 2

    def tiled(a):
        return np.stack([np.concatenate([a[:h, :h], a[h:, h:]]), np.concatenate([a[:h, h:], a[h:, :h]])])

    lev2 = np.stack([tiled(lev), tiled(lev[::-1, ::-1])])
    return up2, lev2


@functools.lru_cache(maxsize=None)
def _gla_tables():
    rowblk, seq, first, last = [], [], [], []
    for d in range(2):
        rb, sq, fi, la = [], [], [], []
        for s in range(BATCH + DEC_BATCH):
            nblk = 1 if s < BATCH else DEC_SEQ // GLA_BLK
            base = s if s < BATCH else NTOK_C // GLA_BLK + (s - BATCH) * nblk
            order = range(nblk) if d == 0 else range(nblk - 1, -1, -1)
            for pos, b in enumerate(order):
                rb.append(base + b)
                sq.append(s)
                fi.append(int(pos == 0))
                la.append(int(pos == nblk - 1))
        rowblk.append(rb); seq.append(sq); first.append(fi); last.append(la)
    as_np = lambda a: np.asarray(a, np.int32)
    return as_np(rowblk), as_np(seq), as_np(first), as_np(last)


def _gla_kernel(rowblk_ref, seq_ref, first_ref, last_ref,
                qf_ref, kf_ref, vf_ref, lrf_ref, qb_ref, kb_ref, vb_ref, lrb_ref,
                wgk_ref, bgk_ref, up_ref, lev_ref, s0_ref,
                of_ref, ob_ref, fin_ref, z_scr, st_scr):
    del rowblk_ref, seq_ref
    n = pl.program_id(0)

    @pl.when(first_ref[n] == 1)
    def _():
        st_scr[...] = jnp.zeros_like(st_scr)
        for d in range(2):
            for h in range(GLA_HEADS):
                st_scr[d, h * GLA_DK:(h + 1) * GLA_DK, h * GLA_DV:(h + 1) * GLA_DV] = s0_ref[d, h]

    _gla_block(False, qf_ref, kf_ref, vf_ref, lrf_ref, wgk_ref.at[0], bgk_ref.at[0], up_ref.at[0], lev_ref.at[0],
               of_ref, z_scr.at[0], st_scr.at[0])
    _gla_block(True, qb_ref, kb_ref, vb_ref, lrb_ref, wgk_ref.at[1], bgk_ref.at[1], up_ref.at[1], lev_ref.at[1],
               ob_ref, z_scr.at[1], st_scr.at[1])

    @pl.when(last_ref[n] == 1)
    def _():
        for d in range(2):
            for h in range(GLA_HEADS):
                fin_ref[d, h] = st_scr[d, h * GLA_DK:(h + 1) * GLA_DK, h * GLA_DV:(h + 1) * GLA_DV]


def _gla_block(backward, q_ref, k_ref, v_ref, lr_ref, wgk_ref, bgk_ref, up_ref, lev_ref, o_ref, z_scr, st_scr):
    nl = GLA_LEVELS
    blk = GLA_BLK
    q = q_ref[...] * (GLA_DK ** -0.5)
    k = k_ref[...]
    vb = v_ref[...].astype(BF16)
    x = _dot(lr_ref[...].astype(BF16), wgk_ref[...]) + bgk_ref[...]
    gk = (jnp.minimum(x, 0.0) - jnp.log(1.0 + jnp.exp(-jnp.abs(x)))) * (1.0 / GLA_NORMALIZER)
    g_hi = gk.astype(BF16)
    g_lo = (gk - g_hi.astype(F32)).astype(BF16)
    ones = jnp.ones((blk, 128), BF16)
    tot = _dot_tn(g_hi, ones) + _dot_tn(g_lo, ones)

    row = lax.broadcasted_iota(jnp.int32, (blk, 1), 0)

    def sibling(a, l):
        g = 1 << l
        if g < 8:
            a3 = a.reshape(blk // 8, 8, a.shape[-1])
            dn = pltpu.roll(a3, g, 1).reshape(a.shape)
            up_ = pltpu.roll(a3, 8 - g, 1).reshape(a.shape)
            return jnp.where(((row >> l) & 1) == 1, dn, up_)
        a4 = a.reshape(blk // (2 * g), 2, g, a.shape[-1])
        return jnp.concatenate([a4[:, 1:2], a4[:, 0:1]], axis=1).reshape(a.shape)

    part = gk
    total = gk
    z0 = None
    for l in range(nl):
        g = 1 << l
        if g < 8:
            up = up_ref[:, l:l + 1] != 0
            z = jnp.where(up, q, k) * jnp.exp(jnp.where(up, part, total - part))
            other = sibling(total, l)
            part = part + jnp.where(up, other, 0.0)
            total = total + other
        else:
            halves = lambda a: (a.reshape(blk // (2 * g), 2, g, a.shape[-1])[:, 1 - int(backward)],
                                a.reshape(blk // (2 * g), 2, g, a.shape[-1])[:, int(backward)])
            join = lambda u, d: jnp.stack([d, u] if not backward else [u, d], axis=1).reshape(blk, u.shape[-1])
            part_u, part_d = halves(part)
            tot_u, tot_d = halves(total)
            q_u, _ = halves(q)
            _, k_d = halves(k)
            z = join(q_u * jnp.exp(part_u), k_d * jnp.exp(tot_d - part_d))
            part = join(part_u + tot_d, part_d)
            both = tot_u + tot_d
            total = join(both, both)
        if l == 0:
            z0 = z
        else:
            z_scr[l] = z.astype(BF16)
    lane128 = lax.broadcasted_iota(jnp.int32, (GLA_QK, 128), 1)
    dim = lax.broadcasted_iota(jnp.int32, (GLA_QK, 128), 0)
    head_sum = ((dim >> 6) == lane128).astype(BF16)
    pair0 = _dot((z0 * sibling(z0, 0)).astype(BF16), head_sum)
    diag = _dot((q * k).astype(BF16), head_sum)

    half = blk // 2
    lev_d = lev_ref[0]
    lev_o = lev_ref[1]
    lane = lax.broadcasted_iota(jnp.int32, (half, GLA_QK), 1)

    def tiles(l, in_head, crossed):
        out = []
        for r in range(2):
            c = 1 - r if crossed else r
            lhs = z_scr[l, r * half:(r + 1) * half, :]
            keys = z_scr[l, c * half:(c + 1) * half, :]
            out.append(_dot_nt(lhs, jnp.where(in_head, keys, jnp.zeros_like(keys))))
        return jnp.concatenate(out, axis=0)

    upi = 0 if backward else 1
    key_lanes = {}
    for l in range(3, nl - 1):
        g = 1 << l
        c = lax.broadcasted_iota(jnp.int32, (blk // (2 * g), g, 128), 0)
        ln = lax.broadcasted_iota(jnp.int32, (blk // (2 * g), g, 128), 2)
        base = (2 * g * c + (g if backward else 0)) & 127
        key_lanes[l] = (ln >= base) & (ln < base + g)

    for h in range(GLA_HEADS):
        in_head = (lane >= h * GLA_DK) & (lane < (h + 1) * GLA_DK)
        acc = jnp.where(lev_d == nl, diag[:, h:h + 1], 0.0)
        acc = jnp.where(lev_d == 0, pair0[:, h:h + 1], acc)
        for l in range(1, 3):
            acc = jnp.where(lev_d == l, tiles(l, in_head, False), acc)
        for l in range(3, nl - 1):
            g = 1 << l
            acc4 = acc.reshape(blk // (2 * g), 2, g, 128)
            s4 = tiles(l, in_head, False).reshape(blk // (2 * g), 2, g, 128)
            new_up = jnp.where(key_lanes[l], s4[:, upi], acc4[:, upi])
            pieces = [acc4[:, 0], new_up] if upi == 1 else [new_up, acc4[:, 1]]
            acc = jnp.stack(pieces, axis=1).reshape(blk, 128)
        off = jnp.where(lev_o == nl - 1, tiles(nl - 1, in_head, True), 0.0)
        att = jnp.concatenate([jnp.concatenate([acc[:half], off[:half]], axis=1),
                               jnp.concatenate([off[half:], acc[half:]], axis=1)], axis=0)
        o_ref[:, h * GLA_DV:(h + 1) * GLA_DV] = _dot(att.astype(BF16), vb[:, h * GLA_DV:(h + 1) * GLA_DV])

    st = st_scr[...]
    q_in = (q * jnp.exp(part)).astype(BF16)
    o_ref[...] += _dot(q_in, st.astype(BF16))
    k_out = (k * jnp.exp(total - part)).astype(BF16)
    kv = _dot_tn(k_out, vb)
    row = lax.broadcasted_iota(jnp.int32, (GLA_QK, GLA_V), 0)
    col = lax.broadcasted_iota(jnp.int32, (GLA_QK, GLA_V), 1)
    same_head = (row >> 6) == (col >> 7)
    decay = jnp.exp(tot)
    decay = jnp.concatenate([decay] * GLA_HEADS, axis=1)
    st_new = decay * st + jnp.where(same_head, kv, 0.0)
    st_scr[...] = st_new


def _gla_mix(bslab, lr, wgk, bgk, s0):
    up, lev = _gla_consts()
    rowblk, seq, first, last = _gla_tables()
    nsteps = rowblk.shape[1]
    nseq = BATCH + DEC_BATCH
    nl = GLA_LEVELS
    whole = lambda shape: pl.BlockSpec(shape, lambda n, rb, sq, fi, la: (0,) * len(shape))

    def token_specs(d):
        return [
            pl.BlockSpec((GLA_BLK, GLA_QK), lambda n, rb, sq, fi, la: (rb[d, n], 0)),
            pl.BlockSpec((GLA_BLK, GLA_QK), lambda n, rb, sq, fi, la: (rb[d, n], 1)),
            pl.BlockSpec((GLA_BLK, GLA_V), lambda n, rb, sq, fi, la: (rb[d, n], 1)),
            pl.BlockSpec((GLA_BLK, 128), lambda n, rb, sq, fi, la: (rb[d, n], 0)),
        ]

    state_spec = pl.BlockSpec((None, 2, GLA_HEADS, GLA_DK, GLA_DV), lambda n, rb, sq, fi, la: (sq[n], 0, 0, 0, 0))
    grid_spec = pltpu.PrefetchScalarGridSpec(
        num_scalar_prefetch=4,
        grid=(nsteps,),
        in_specs=token_specs(0) + token_specs(1) + [
            whole((2, 128, GLA_QK)),
            whole((2, 1, GLA_QK)),
            whole((2, GLA_BLK, 128)),
            whole((2, 2, GLA_BLK, GLA_BLK // 2)),
            state_spec,
        ],
        out_specs=[
            pl.BlockSpec((GLA_BLK, GLA_V), lambda n, rb, sq, fi, la: (rb[0, n], 0)),
            pl.BlockSpec((GLA_BLK, GLA_V), lambda n, rb, sq, fi, la: (rb[1, n], 0)),
            state_spec,
        ],
        scratch_shapes=[
            pltpu.VMEM((2, nl, GLA_BLK, GLA_QK), BF16),
            pltpu.VMEM((2, GLA_QK, GLA_V), F32),
        ],
    )
    return pl.pallas_call(
        _gla_kernel,
        grid_spec=grid_spec,
        out_shape=[
            jax.ShapeDtypeStruct((NTOK, GLA_V), F32),
            jax.ShapeDtypeStruct((NTOK, GLA_V), F32),
            jax.ShapeDtypeStruct((nseq, 2, GLA_HEADS, GLA_DK, GLA_DV), F32),
        ],
        compiler_params=pltpu.CompilerParams(vmem_limit_bytes=VMEM_LIMIT),
        name="gla_mix",
    )(jnp.asarray(rowblk), jnp.asarray(seq[0]), jnp.asarray(first[0]), jnp.asarray(last[0]),
      bslab, bslab, bslab, lr, bslab, bslab, bslab, lr, wgk, bgk, jnp.asarray(up), jnp.asarray(lev), s0)


def _attn_ctx_kernel(sink_ref, q_ref, k_ref, v_ref, o_ref):
    k = k_ref[...]
    v = v_ref[...]
    ks = (k.astype(BF16), pltpu.roll(k, 64, 1).astype(BF16))
    vs = (v.astype(BF16), pltpu.roll(v, 64, 1).astype(BF16))
    lo = lax.broadcasted_iota(jnp.int32, (SEQ, 128), 1) < HEAD_DIM
    units = []
    for t in range(ATT_HEADS // 2):
        qt = q_ref[:, t * 128:(t + 1) * 128] * (HEAD_DIM ** -0.5)
        for p in range(2):
            qm = jnp.where(lo if p == 0 else jnp.logical_not(lo), qt, 0.0).astype(BF16)
            units.append((qm, 0 if p == t // 2 else 1, sink_ref[2 * t + p]))
    scores = [_dot_nt(qm, ks[which]) for qm, which, _ in units]
    maxes = [jnp.maximum(sink, jnp.max(s, axis=-1, keepdims=True)) for s, (_, _, sink) in zip(scores, units)]
    probs = [jnp.exp(s - m) for s, m in zip(scores, maxes)]
    dens = [jnp.exp(sink - m) + jnp.sum(p, axis=-1, keepdims=True)
            for p, m, (_, _, sink) in zip(probs, maxes, units)]
    outs = [_dot(p.astype(BF16), vs[which]) / den for p, den, (_, which, _) in zip(probs, dens, units)]
    for t in range(ATT_HEADS // 2):
        o_ref[:, t * 128:(t + 1) * 128] = jnp.where(lo, outs[2 * t], outs[2 * t + 1])


def _attn_ctx(sink, cslab):
    return pl.pallas_call(
        _attn_ctx_kernel,
        grid=(BATCH,),
        in_specs=[
            pl.BlockSpec(memory_space=pltpu.SMEM),
            pl.BlockSpec((SEQ, ATT_Q), lambda b: (b, 0)),
            pl.BlockSpec((SEQ, ATT_KV), lambda b: (b, 4)),
            pl.BlockSpec((SEQ, ATT_KV), lambda b: (b, 5)),
        ],
        out_specs=pl.BlockSpec((SEQ, ATT_Q), lambda b: (b, 0)),
        out_shape=jax.ShapeDtypeStruct((NTOK_C, ATT_Q), F32),
        name="attn_ctx",
    )(sink, cslab, cslab, cslab)


def _attn_lat_kernel(sink_ref, q_ref, kp_ref, kc_ref, kn_ref, vp_ref, vc_ref, vn_ref,
                     ck_ref, cv_ref, cos_ref, sin_ref, bias_ref, o_ref):
    j = pl.program_id(1)
    nb = DEC_SEQ // ATT_BLOCK
    lane = lax.broadcasted_iota(jnp.int32, (ATT_BLOCK, 128), 1)
    lo = lane < HEAD_DIM
    first16 = (lane & 31) < 16

    def rope(x, blk_idx):
        r0 = pl.multiple_of(blk_idx * ATT_BLOCK, ATT_BLOCK)
        c = cos_ref[pl.ds(r0, ATT_BLOCK), :]
        s = sin_ref[pl.ds(r0, ATT_BLOCK), :]
        xs = jnp.where(first16, pltpu.roll(x, 112, 1), pltpu.roll(x, 16, 1))
        return x * c + xs * s

    nwin = 3 * ATT_BLOCK
    keys = jnp.concatenate([rope(kp_ref[...], jnp.maximum(j - 1, 0)), rope(kc_ref[...], j),
                            rope(kn_ref[...], jnp.minimum(j + 1, nb - 1)), ck_ref[...]], axis=0)
    vals = jnp.concatenate([vp_ref[...], vc_ref[...], vn_ref[...], cv_ref[...]], axis=0)
    keys2 = (keys.astype(BF16), pltpu.roll(keys, 64, 1).astype(BF16))
    vals2 = (vals.astype(BF16), pltpu.roll(vals, 64, 1).astype(BF16))
    kcol = lax.broadcasted_iota(jnp.int32, (1, nwin + PAST_LEN), 1)
    edge = jnp.where(((j == 0) & (kcol < ATT_BLOCK)) | ((j == nb - 1) & (kcol >= 2 * ATT_BLOCK) & (kcol < nwin)),
                     -1e30, 0.0)
    bias = bias_ref[...] + edge
    top = lax.broadcasted_iota(jnp.int32, (2 * ATT_BLOCK, 1), 0) < ATT_BLOCK
    q_tiles = [rope(q_ref[:, t * 128:(t + 1) * 128], j) * (HEAD_DIM ** -0.5) for t in range(ATT_HEADS // 2)]
    lo2 = jnp.concatenate([lo, lo], axis=0)
    units = []
    for kvh in range(ATT_KV_HEADS):
        q2 = jnp.concatenate(q_tiles[2 * kvh:2 * kvh + 2], axis=0)
        for p in range(2):
            qm = jnp.where(lo2 if p == 0 else jnp.logical_not(lo2), q2, 0.0).astype(BF16)
            sink = jnp.where(top, sink_ref[4 * kvh + p], sink_ref[4 * kvh + 2 + p])
            units.append((qm, 0 if p == kvh else 1, sink))
    scores = [_dot_nt(qm, keys2[which]) + bias for qm, which, _ in units]
    maxes = [jnp.maximum(sink, jnp.max(s, axis=-1, keepdims=True)) for s, (_, _, sink) in zip(scores, units)]
    probs = [jnp.exp(s - m) for s, m in zip(scores, maxes)]
    dens = [jnp.exp(sink - m) + jnp.sum(p, axis=-1, keepdims=True)
            for p, m, (_, _, sink) in zip(probs, maxes, units)]
    outs = [_dot(p.astype(BF16), vals2[which]) / den for p, den, (_, which, _) in zip(probs, dens, units)]
    for kvh in range(ATT_KV_HEADS):
        o2 = jnp.where(lo2, outs[2 * kvh], outs[2 * kvh + 1])
        for i in range(2):
            t = 2 * kvh + i
            o_ref[:, t * 128:(t + 1) * 128] = o2[i * ATT_BLOCK:(i + 1) * ATT_BLOCK]


def _attn_lat(sink, cslab, ck, cv, cos_t, sin_t):
    nb = DEC_SEQ // ATT_BLOCK
    base = NTOK_C // ATT_BLOCK
    cur = lambda b, j: base + b * nb + j
    prv = lambda b, j: base + b * nb + jnp.maximum(j - 1, 0)
    nxt = lambda b, j: base + b * nb + jnp.minimum(j + 1, nb - 1)
    kv_spec = lambda row, col: pl.BlockSpec((ATT_BLOCK, ATT_KV), lambda b, j: (row(b, j), col))
    qi = np.arange(2 * ATT_BLOCK)[:, None] % ATT_BLOCK
    kc = np.arange(3 * ATT_BLOCK + PAST_LEN)[None, :]
    inside = (np.abs(kc - ATT_BLOCK - qi) <= WINDOW) | (kc >= 3 * ATT_BLOCK)
    band = np.where(inside, 0.0, -1e30).astype(np.float32)
    return pl.pallas_call(
        _attn_lat_kernel,
        grid=(DEC_BATCH, nb),
        in_specs=[
            pl.BlockSpec(memory_space=pltpu.SMEM),
            pl.BlockSpec((ATT_BLOCK, ATT_Q), lambda b, j: (cur(b, j), 0)),
            kv_spec(prv, 4), kv_spec(cur, 4), kv_spec(nxt, 4),
            kv_spec(prv, 5), kv_spec(cur, 5), kv_spec(nxt, 5),
            pl.BlockSpec((None, PAST_LEN, ATT_KV), lambda b, j: (b, 0, 0)),
            pl.BlockSpec((None, PAST_LEN, ATT_KV), lambda b, j: (b, 0, 0)),
            pl.BlockSpec((DEC_SEQ, 128), lambda b, j: (0, 0)),
            pl.BlockSpec((DEC_SEQ, 128), lambda b, j: (0, 0)),
            pl.BlockSpec(band.shape, lambda b, j: (0, 0)),
        ],
        out_specs=pl.BlockSpec((ATT_BLOCK, ATT_Q), lambda b, j: (b * nb + j, 0)),
        out_shape=jax.ShapeDtypeStruct((NTOK_L, ATT_Q), F32),
        name="attn_lat",
    )(sink, cslab, cslab, cslab, cslab, cslab, cslab, cslab, ck, cv, cos_t, sin_t, jnp.asarray(band))


def _rope_tables():
    rows = DEC_SEQ // GRID_W
    row = np.repeat(np.arange(rows, dtype=np.float32), GRID_W)
    col = np.tile(np.arange(GRID_W, dtype=np.float32), rows)
    quarter = HEAD_DIM // 4
    inv = jnp.asarray(ROPE_BASE, F32) ** (-jnp.arange(quarter, dtype=F32) / quarter)
    lane = np.arange(128)
    use_row = (lane % HEAD_DIM) < HEAD_DIM // 2
    pos = jnp.where(use_row[None, :], jnp.asarray(row)[:, None], jnp.asarray(col)[:, None])
    ang = pos * inv[lane % quarter][None, :]
    sign = np.where((lane % 32) < 16, -1.0, 1.0).astype(np.float32)
    return jnp.cos(ang), jnp.sin(ang) * sign[None, :]


_STAGE_END = object()
MERGE_PARTS = 2


def _merge_kernel(*refs, split_x):
    if split_x:
        xc_ref, xl_ref, *refs = refs
    else:
        xc_ref, *refs = refs
    (mod_ref, g_ref, ys5_ref, ogf_ref, ogb_ref, gb_ref, ycc_ref, ycl_ref, gate_ref, gng_ref,
     wglu_ref, wbr_ref, wout_ref, o_ref) = refs
    is_ctx = pl.program_id(0) < NTOK_C // TM
    gng = gng_ref[...]
    g1 = mod_ref[:, 2 * D_MODEL:3 * D_MODEL]

    def rows_stage(rows):
        y = jnp.concatenate([ys5_ref[j, rows, :] for j in range(S5_SLABS)], axis=1)
        y = (0.5 * y * (1.0 + jnp.tanh(math.sqrt(2.0 / math.pi) * (y + 0.044715 * (y * y * y))))).astype(BF16)
        yield
        ag = _dot(y, wglu_ref[...])
        yield
        y_a = ag[:, :S5_WIDTH] * _sigmoid(ag[:, S5_WIDTH:])
        parts = []
        for h in range(GLA_HEADS):
            sl = slice(h * GLA_DV, (h + 1) * GLA_DV)
            o = ogf_ref[rows, sl] + ogb_ref[rows, sl]
            g = gb_ref[rows, sl]
            parts.append(_rms(o, gng) * (g * _sigmoid(g)))
        y_b = jnp.concatenate(parts, axis=1)
        y_c = jnp.where(is_ctx, ycc_ref[rows, :], ycl_ref[rows, :])
        branches = [yn.astype(BF16) for yn in (y_a, y_b, y_c)]
        yield
        projs = [_dot(yn, wbr_ref[n]) for n, yn in enumerate(branches)]
        yield
        merged = None
        for n, proj in enumerate(projs):
            term = _sigmoid(gate_ref[rows, n * D_MODEL:(n + 1) * D_MODEL]) * proj
            merged = term if merged is None else merged + term
        merged = merged.astype(BF16)
        yield
        mixed = _dot(merged, wout_ref[...])
        yield
        if split_x:
            x = jnp.where(is_ctx, xc_ref[rows, :], xl_ref[rows, :])
        else:
            x = xc_ref[rows, :]
        o_ref[rows, :] = x + g1 * _rms(mixed, g_ref[...])

    part = TM // MERGE_PARTS
    waiting = [rows_stage(slice(k * part, (k + 1) * part)) for k in range(MERGE_PARTS)]
    live = []
    while waiting or live:
        if waiting:
            live.append(waiting.pop(0))
        live = [g for g in live if next(g, _STAGE_END) is not _STAGE_END]


def _layer_spec(shape, layer):
    return pl.BlockSpec((None,) + shape, lambda i: (layer,) + (0,) * len(shape), pipeline_mode=pl.Buffered(1))


def _split_token_specs(n_arrays, width=D_MODEL):
    nct = NTOK_C // TM
    if n_arrays == 2:
        return [pl.BlockSpec((TM, width), lambda i: (jnp.minimum(i, nct - 1), 0)),
                pl.BlockSpec((TM, width), lambda i: (jnp.maximum(i - nct, 0), 0))]
    return [pl.BlockSpec((TM, width), lambda i: (i, 0))]


def _merge(xs, mod, g, ys5, og, bslab, yc, gates, gng, wglu, wbr, wout, layer):
    tok = lambda width, col=0: pl.BlockSpec((TM, width), lambda i: (i, col))
    full = lambda shape: _layer_spec(shape, layer)
    return pl.pallas_call(
        functools.partial(_merge_kernel, split_x=len(xs) == 2),
        grid=(NTOK // TM,),
        in_specs=_split_token_specs(len(xs)) + [
            pl.BlockSpec((None, 1, 6 * D_MODEL), lambda i: (_mod_row(i), 0, 0)),
            pl.BlockSpec((1, D_MODEL), lambda i: (0, 0)),
            pl.BlockSpec((S5_SLABS, TM, 128), lambda i: (0, i, 0)),
            tok(GLA_V),
            tok(GLA_V),
            tok(GLA_V, 2),
        ] + _split_token_specs(2, ATT_Q) + [
            tok(N_BRANCH * D_MODEL),
            pl.BlockSpec((1, GLA_DV), lambda i: (0, 0)),
            full((S5_WIDTH, 2 * S5_WIDTH)),
            full((N_BRANCH, BRANCH_W, D_MODEL)),
            full((D_MODEL, D_MODEL)),
        ],
        out_specs=tok(D_MODEL),
        out_shape=jax.ShapeDtypeStruct((NTOK, D_MODEL), F32),
        compiler_params=pltpu.CompilerParams(vmem_limit_bytes=VMEM_LIMIT),
        name="merge",
    )(*xs, mod, g, ys5, *og, bslab, *yc, gates, gng, wglu, wbr, wout)


FFN_SPLIT = 2


def _ffn_kernel(x_ref, mod_ref, gin_ref, gout_ref, w1_ref, w2_ref, *o_refs):
    x = x_ref[...]
    sh = mod_ref[:, 3 * D_MODEL:4 * D_MODEL]
    sc = mod_ref[:, 4 * D_MODEL:5 * D_MODEL]
    g2 = mod_ref[:, 5 * D_MODEL:6 * D_MODEL]
    h = (_rms(x, gin_ref[...]) * (1.0 + sc) + sh).astype(BF16)
    ck = FFN_HIDDEN // FFN_SPLIT
    acc = None
    for c in range(FFN_SPLIT):
        a = _dot(h, w1_ref[:, c * ck:(c + 1) * ck])
        b = _dot(h, w1_ref[:, FFN_HIDDEN + c * ck:FFN_HIDDEN + (c + 1) * ck])
        act = (a * _sigmoid(a) * b).astype(BF16)
        part = _dot(act, w2_ref[c * ck:(c + 1) * ck, :])
        acc = part if acc is None else acc + part
    y = x + g2 * _rms(acc, gout_ref[...])
    if len(o_refs) == 1:
        o_refs[0][...] = y
    else:
        is_ctx = pl.program_id(0) < NTOK_C // TM

        @pl.when(is_ctx)
        def _():
            o_refs[0][...] = y

        @pl.when(jnp.logical_not(is_ctx))
        def _():
            o_refs[1][...] = y


def _ffn(x, mod, gin, gout, w1, w2, layer, split_out):
    small = lambda shape: pl.BlockSpec(shape, lambda i: (0,) * len(shape))
    nct = NTOK_C // TM
    if split_out:
        out_specs = [pl.BlockSpec((TM, D_MODEL), lambda i: (jnp.minimum(i, nct - 1), 0)),
                     pl.BlockSpec((TM, D_MODEL), lambda i: (jnp.maximum(i - nct, 0), 0))]
        out_shape = [jax.ShapeDtypeStruct((NTOK_C, D_MODEL), F32), jax.ShapeDtypeStruct((NTOK_L, D_MODEL), F32)]
    else:
        out_specs = pl.BlockSpec((TM, D_MODEL), lambda i: (i, 0))
        out_shape = jax.ShapeDtypeStruct((NTOK, D_MODEL), F32)
    return pl.pallas_call(
        _ffn_kernel,
        grid=(NTOK // TM,),
        in_specs=[
            pl.BlockSpec((TM, D_MODEL), lambda i: (i, 0)),
            pl.BlockSpec((None, 1, 6 * D_MODEL), lambda i: (_mod_row(i), 0, 0)),
            small((1, D_MODEL)),
            small((1, D_MODEL)),
            _layer_spec((D_MODEL, 2 * FFN_HIDDEN), layer),
            _layer_spec((FFN_HIDDEN, D_MODEL), layer),
        ],
        out_specs=out_specs,
        out_shape=out_shape,
        compiler_params=pltpu.CompilerParams(vmem_limit_bytes=VMEM_LIMIT),
        name="ffn",
    )(x, mod, gin, gout, w1, w2)


def kernel(x_prompt, x_sample, cache_k, cache_v, state_s5, state_gla, c, c_ctx, w_mod, b_mod, norm_g, w_in,
           s5_lam_re, s5_lam_im, s5_log_step, s5_b_re, s5_b_im, s5_c_re, s5_c_im, s5_d, w_glu, gla_w_gk,
           gla_b_gk, gla_norm_g, att_sink, w_branch, w_out, w_ffn_in, w_ffn_out):
    cond = jnp.concatenate([c_ctx[None, :], c, jnp.zeros((N_MOD_ROWS - 1 - DEC_BATCH, D_MODEL), F32)], axis=0)
    mod_all = _modulation(cond, w_mod, b_mod).reshape(DEPTH, N_MOD_ROWS, 1, 6 * D_MODEL)
    cos_t, sin_t = _rope_tables()
    xs = (x_prompt.reshape(NTOK_C, D_MODEL), x_sample.reshape(NTOK_L, D_MODEL))
    w_in_b = w_in[:, :, :D_IN_TILED].astype(BF16)
    w_in_end = jnp.pad(w_in[:, :, D_IN_TILED:].astype(BF16), ((0, 0), (0, 0), (0, W_IN_COLS - D_IN)))
    w_glu_b, w_branch_b, w_out_b = w_glu.astype(BF16), w_branch.astype(BF16), w_out.astype(BF16)
    w_ffn_in_b, w_ffn_out_b = w_ffn_in.astype(BF16), w_ffn_out.astype(BF16)
    new_k, new_v, new_s5, new_gla = [], [], [], []
    for i in range(DEPTH):
        mod = mod_all[i]
        uj, bslab, cslab, gates, lr = _inproj(xs, mod, norm_g[i, 0][None, :], w_in_b, w_in_end, i)

        wt, web, wca, a16, dj = _s5_prep(s5_lam_re[i], s5_lam_im[i], s5_log_step[i], s5_b_re[i], s5_b_im[i],
                                         s5_c_re[i], s5_c_im[i], s5_d[i])
        h0l = state_s5[:, i].astype(F32).transpose(0, 2, 4, 1, 3).reshape(DEC_BATCH, S5_GROUPS * 256)
        hin, finc = _s5_scan(_s5_state(uj, web), a16, h0l)
        ys5 = _s5_out(uj, hin, wt, wca, dj)
        new_s5.append(finc.reshape(BATCH, S5_GROUPS, 2, 2, S5_STATE).transpose(0, 3, 1, 4, 2))

        wgk = jnp.zeros((2, 128, GLA_QK), F32)
        wgk = wgk.at[0, 0:GLA_RANK].set(gla_w_gk[i, 0]).at[1, GLA_RANK:2 * GLA_RANK].set(gla_w_gk[i, 1])
        s0 = jnp.concatenate([jnp.zeros((BATCH, 2, GLA_HEADS, GLA_DK, GLA_DV), F32),
                              state_gla[:, i].astype(F32)], axis=0)
        *og, gla_fin = _gla_mix(bslab, lr, wgk.astype(BF16), gla_b_gk[i][:, None, :].astype(F32), s0)
        new_gla.append(gla_fin[:BATCH])

        sink = att_sink[i].astype(F32)
        yc = (_attn_ctx(sink, cslab),
              _attn_lat(sink, cslab, cache_k[:, i].reshape(DEC_BATCH, PAST_LEN, ATT_KV).astype(F32),
                        cache_v[:, i].reshape(DEC_BATCH, PAST_LEN, ATT_KV).astype(F32), cos_t, sin_t))
        new_k.append(cslab[:NTOK_C, ATT_Q:ATT_Q + ATT_KV].reshape(BATCH, SEQ, ATT_KV_HEADS, HEAD_DIM))
        new_v.append(cslab[:NTOK_C, ATT_Q + ATT_KV:].reshape(BATCH, SEQ, ATT_KV_HEADS, HEAD_DIM))

        x = _merge(xs, mod, norm_g[i, 1][None, :], ys5, og, bslab, yc, gates, gla_norm_g[i][None, :],
                   w_glu_b, w_branch_b, w_out_b, i)
        last = i == DEPTH - 1
        x = _ffn(x, mod, norm_g[i, 2][None, :], norm_g[i, 3][None, :], w_ffn_in_b, w_ffn_out_b, i, last)
        xs = tuple(x) if last else (x,)

    return (xs[0].reshape(BATCH, SEQ, D_MODEL), xs[1].reshape(DEC_BATCH, DEC_SEQ, D_MODEL),
            jnp.stack(new_k, axis=1), jnp.stack(new_v, axis=1),
            jnp.stack(new_s5, axis=1), jnp.stack(new_gla, axis=1))
```

```python
import functools
import math

import numpy as np
import jax
import jax.numpy as jnp
from jax import lax
from jax.experimental import pallas as pl
from jax.experimental.pallas import tpu as pltpu

F32 = jnp.float32
BF16 = jnp.bfloat16

D_MODEL = 1024
BATCH = 16
SEQ = 256
DEPTH = 2
DEC_BATCH = 8
DEC_SEQ = 1024
PAST_LEN = 256
GRID_W = 64
ROPE_BASE = 10000.0
S5_WIDTH = 512
S5_GROUP = 16
S5_GROUPS = 32
S5_STATE = 64
GLA_HEADS = 4
GLA_DK = 64
GLA_DV = 128
GLA_QK = 256
GLA_V = 512
GLA_RANK = 16
GLA_NORMALIZER = 16.0
ATT_HEADS = 8
ATT_KV_HEADS = 2
HEAD_DIM = 64
ATT_Q = 512
ATT_KV = 128
WINDOW = 128
ATT_BLOCK = 128
N_BRANCH = 3
BRANCH_W = 512
FFN_HIDDEN = 2816
RMS_EPS = 1e-6

NTOK_C = BATCH * SEQ
NTOK_L = DEC_BATCH * DEC_SEQ
NTOK = NTOK_C + NTOK_L
TM = 512
N_MOD_ROWS = 16

D_IN = 5920
D_IN_TILED = D_IN // 128 * 128
W_IN_COLS = 6016
S5_CHUNK = 16
S5_SLABS = S5_WIDTH // 128
S5_SLAB_W = S5_CHUNK * 128
S5_ROWS_C = NTOK_C // S5_CHUNK
S5_ROWS = NTOK // S5_CHUNK
S5_ROW_TILE = 256
GLA_BLK = 256
GLA_LEVELS = 8
VMEM_LIMIT = 56 * 1024 * 1024


def _dot(a, b):
    return jnp.dot(a, b, preferred_element_type=F32)


def _dot_nt(a, b):
    return lax.dot_general(a, b, (((1,), (1,)), ((), ())), preferred_element_type=F32)


def _dot_tn(a, b):
    return lax.dot_general(a, b, (((0,), (0,)), ((), ())), preferred_element_type=F32)


def _rms(x, g):
    return x * lax.rsqrt(jnp.mean(x * x, axis=-1, keepdims=True) + RMS_EPS) * g


def _sigmoid(x):
    return 0.5 * jnp.tanh(0.5 * x) + 0.5


def _mod_row(i):
    nct = NTOK_C // TM
    return jnp.where(i < nct, 0, 1 + (i - nct) // (DEC_SEQ // TM))


def _s5_tile(i):
    nct = NTOK_C // TM
    per_seq = DEC_SEQ // TM
    k = i - nct
    return jnp.where(i < nct, i, nct + (k % per_seq) * DEC_BATCH + k // per_seq)


def _mod_kernel(c_ref, w_ref, b_ref, o_ref):
    c = c_ref[...]
    s = (c * _sigmoid(c)).astype(BF16)
    o_ref[...] = _dot(s, w_ref[...].astype(BF16)) + b_ref[...]


def _modulation(cond, w_mod, b_mod):
    tn = 2048
    return pl.pallas_call(
        _mod_kernel,
        grid=(DEPTH, 6 * D_MODEL // tn),
        in_specs=[
            pl.BlockSpec((N_MOD_ROWS, D_MODEL), lambda l, n: (0, 0)),
            pl.BlockSpec((None, D_MODEL, tn), lambda l, n: (l, 0, n)),
            pl.BlockSpec((None, 1, tn), lambda l, n: (l, 0, n)),
        ],
        out_specs=pl.BlockSpec((None, N_MOD_ROWS, tn), lambda l, n: (l, 0, n)),
        out_shape=jax.ShapeDtypeStruct((DEPTH, N_MOD_ROWS, 6 * D_MODEL), F32),
        name="modulation",
    )(cond, w_mod, b_mod.reshape(DEPTH, 1, 6 * D_MODEL))


_IN_SLABS = ((0, 512), (512, 1536), (2048, 768), (2816, 3072), (5888, 128))
W_IN_SPLIT = 2048
W_IN_GAP = 32
W_IN_TAIL = W_IN_COLS - W_IN_SPLIT


def _inproj_kernel(*refs, split_x):
    if split_x:
        xc_ref, xl_ref, *refs = refs
    else:
        xc_ref, *refs = refs
    mod_ref, g_ref, w_ref, w_end_ref, u_ref, b_ref, c_ref, gate_ref, lr_ref, w_tail, u_stage = refs
    i = pl.program_id(0)

    @pl.when(i == 0)
    def _():
        r = lax.broadcasted_iota(jnp.int32, (256, 128), 0)
        c = lax.broadcasted_iota(jnp.int32, (256, 128), 1)
        shift = (r == c + W_IN_GAP).astype(BF16)
        head = ((r == c) & (c < W_IN_GAP)).astype(BF16)
        ntile = (W_IN_TAIL - 128) // 128
        for t in range(ntile - 1):
            src = W_IN_SPLIT + 128 * t
            w_tail[:, 128 * t:128 * (t + 1)] = _dot(w_ref[:, src:src + 256], shift).astype(BF16)
        src = W_IN_SPLIT + 128 * (ntile - 1)
        last = jnp.concatenate([w_ref[:, src:src + 128], w_end_ref[...]], axis=1)
        w_tail[:, 128 * (ntile - 1):128 * ntile] = _dot(last, shift).astype(BF16)
        w_tail[:, 128 * ntile:] = _dot(w_ref[:, W_IN_SPLIT:W_IN_SPLIT + 256], head).astype(BF16)

    if split_x:
        x = jnp.where(i < NTOK_C // TM, xc_ref[...], xl_ref[...])
    else:
        x = xc_ref[...]
    mod = mod_ref[...]
    h = _rms(x, g_ref[...]) * (1.0 + mod[:, D_MODEL:2 * D_MODEL]) + mod[:, 0:D_MODEL]
    h = h.astype(BF16)
    for j in range(S5_SLABS):
        u_stage[...] = _dot(h, w_ref[:, j * 128:(j + 1) * 128])
        for s in range(S5_CHUNK):
            u_ref[j, :, s * 128:(s + 1) * 128] = u_stage[pl.ds(s, TM // S5_CHUNK, stride=S5_CHUNK), :]
    b_ref[...] = _dot(h, w_ref[:, 512:W_IN_SPLIT])
    for (off, width), o_ref in zip(_IN_SLABS[2:], (c_ref, gate_ref, lr_ref)):
        o_ref[...] = _dot(h, w_tail[:, off - W_IN_SPLIT:off - W_IN_SPLIT + width])


def _inproj(xs, mod, g, w_all, w_end, layer):
    return pl.pallas_call(
        functools.partial(_inproj_kernel, split_x=len(xs) == 2),
        grid=(NTOK // TM,),
        in_specs=_split_token_specs(len(xs)) + [
            pl.BlockSpec((None, 1, 6 * D_MODEL), lambda i: (_mod_row(i), 0, 0)),
            pl.BlockSpec((1, D_MODEL), lambda i: (0, 0)),
            pl.BlockSpec((None, D_MODEL, D_IN), lambda i: (layer, 0, 0), pipeline_mode=pl.Buffered(1)),
            pl.BlockSpec((None, D_MODEL, 128), lambda i: (layer, 0, 0), pipeline_mode=pl.Buffered(1)),
        ],
        out_specs=[pl.BlockSpec((S5_SLABS, TM // S5_CHUNK, S5_SLAB_W), lambda i: (0, _s5_tile(i), 0))]
        + [pl.BlockSpec((TM, width), lambda i: (i, 0)) for _, width in _IN_SLABS[1:]],
        out_shape=[jax.ShapeDtypeStruct((S5_SLABS, S5_ROWS, S5_SLAB_W), F32)]
        + [jax.ShapeDtypeStruct((NTOK, width), F32) for _, width in _IN_SLABS[1:]],
        scratch_shapes=[pltpu.VMEM((D_MODEL, W_IN_TAIL), BF16), pltpu.VMEM((TM, 128), F32)],
        compiler_params=pltpu.CompilerParams(vmem_limit_bytes=VMEM_LIMIT),
        name="inproj",
    )(*xs, mod, g, w_all, w_end)


@functools.lru_cache(maxsize=None)
def _s5_expanders():
    seg = 8
    spread = np.zeros((seg, 256, S5_SLAB_W), np.float32)
    place = np.zeros((seg, 256, S5_SLAB_W), np.float32)
    col = np.arange(256)
    for gl in range(seg):
        spread[gl, col, (col // S5_GROUP) * 128 + gl * S5_GROUP + col % S5_GROUP] = 1.0
        place[gl, col, gl * 256 + col] = 1.0
    return spread, place


def _s5_prep_kernel(par_ref, bre_ref, bim_ref, cre_ref, cim_ref, spread_ref, place_ref,
                    wt_ref, web_ref, wca_ref, a16_ref):
    n = S5_CHUNK
    lam_re = par_ref[0:1, :]
    lam_im = par_ref[1:2, :]
    dt = jnp.exp(par_ref[2:3, :])
    lr = lam_re * dt
    li = lam_im * dt
    krow = lax.broadcasted_iota(jnp.int32, (24, 128), 0).astype(F32)
    tab_mag = jnp.exp(krow * lr)
    tab_re = tab_mag * jnp.cos(krow * li)
    tab_im = tab_mag * jnp.sin(krow * li)
    ar = tab_re[1:2, :]
    ai = tab_im[1:2, :]
    nr = ar - 1.0
    den = lam_re * lam_re + lam_im * lam_im
    fr = (nr * lam_re + ai * lam_im) / den
    fi = (ai * lam_re - nr * lam_im) / den
    b_re = bre_ref[...]
    b_im = bim_ref[...]
    br = fr * b_re - fi * b_im
    bi = fr * b_im + fi * b_re
    c_re = cre_ref[...]
    c_im = cim_ref[...]

    def lo_half(shape):
        return lax.broadcasted_iota(jnp.int32, shape, 1) < S5_STATE

    def tile_rows(a):
        return jnp.concatenate([a] * n, axis=0)

    fwd16 = lo_half((S5_GROUP, 128))

    def powers(t_re, t_im, k_fwd, k_bwd):
        def pick(t, b):
            kf, kb = k_fwd(b), k_bwd(b)
            return jnp.where(fwd16, jnp.broadcast_to(t[kf:kf + 1, :], (S5_GROUP, 128)),
                             jnp.broadcast_to(t[kb:kb + 1, :], (S5_GROUP, 128)))
        return (jnp.concatenate([pick(t_re, b) for b in range(n)], axis=0),
                jnp.concatenate([pick(t_im, b) for b in range(n)], axis=0))

    fwd = lo_half((n * S5_GROUP, 128))
    brt, bit, crt, cit = tile_rows(br), tile_rows(bi), tile_rows(c_re), tile_rows(c_im)

    per, pei = powers(tab_re, tab_im, lambda s: n - 1 - s, lambda s: s)
    eb = jnp.concatenate([brt * per - bit * pei, brt * pei + bit * per], axis=1)
    pcr, pci = powers(tab_re, tab_im, lambda t: t + 1, lambda t: n - t)
    ca = jnp.concatenate([(crt * pcr - cit * pci).T, (-(crt * pci + cit * pcr)).T], axis=0)

    def one_dir(x, d):
        sw = pltpu.roll(x, S5_STATE, 1)
        lo = lo_half(x.shape)
        return jnp.where(lo, x, sw) if d == 0 else jnp.where(lo, sw, x)

    klag = []
    for d in range(2):
        lhs = jnp.where(lo_half(br.shape), one_dir(br, d), -one_dir(bi, d))
        crd, cid = tile_rows(one_dir(c_re, d)), tile_rows(one_dir(c_im, d))
        lag = (lambda b: b) if d == 0 else (lambda b: n - 1 - b)
        pr, pi = powers(one_dir(tab_re, d), one_dir(tab_im, d), lag, lag)
        rhs_t = jnp.where(fwd, crd * pr - cid * pi, crd * pi + cid * pr)
        klag.append(lax.dot_general(lhs, rhs_t, (((1,), (1,)), ((), ())),
                                    precision=lax.Precision.HIGHEST, preferred_element_type=F32))
    lane = lax.broadcasted_iota(jnp.int32, (S5_GROUP, n * S5_GROUP), 1)
    rows = []
    for s in range(n):
        f = klag[0] if s == 0 else jnp.where(lane >= S5_GROUP * s, pltpu.roll(klag[0], S5_GROUP * s, 1), 0.0)
        sh = (n * S5_GROUP - S5_GROUP * (n - 1 - s)) % (n * S5_GROUP)
        b = klag[1] if sh == 0 else pltpu.roll(klag[1], sh, 1)
        rows.append(f + jnp.where(lane < S5_GROUP * (s + 1), b, 0.0))
    toep = jnp.concatenate(rows, axis=0)

    spread = spread_ref[...]
    wt_ref[...] = _dot(toep.astype(BF16), spread).astype(BF16).reshape(n, S5_GROUP, S5_SLAB_W)
    web_ref[...] = _dot(eb.astype(BF16), place_ref[...]).astype(BF16).reshape(n, S5_GROUP, S5_SLAB_W)
    wca_ref[...] = _dot(ca.astype(BF16), spread).astype(BF16)
    a16_ref[0:1, :] = tab_re[n:n + 1, :]
    a16_ref[1:2, :] = tab_im[n:n + 1, :]


def _s5_prep(lam_re, lam_im, log_step, b_re, b_im, c_re, c_im, d_skip):
    seg = 8
    par = jnp.stack([lam_re, lam_im, log_step]).astype(F32).transpose(2, 0, 1, 3).reshape(S5_GROUPS, 3, 128)
    par = jnp.concatenate([par, jnp.zeros((S5_GROUPS, 5, 128), F32)], axis=1)
    b_t = lambda b: b.astype(F32).transpose(1, 3, 0, 2).reshape(S5_GROUPS, S5_GROUP, 128)
    c_t = lambda c: c.astype(F32).transpose(1, 2, 0, 3).reshape(S5_GROUPS, S5_GROUP, 128)
    spread, place = _s5_expanders()
    vec = pl.BlockSpec((None, S5_GROUP, 128), lambda gl, j: (j * seg + gl, 0, 0))
    exp_spec = pl.BlockSpec((None, 256, S5_SLAB_W), lambda gl, j: (gl, 0, 0))
    rows_spec = pl.BlockSpec((None, S5_CHUNK, None, S5_GROUP, S5_SLAB_W), lambda gl, j: (j, 0, gl, 0, 0))
    wt, web, wca, a16 = pl.pallas_call(
        _s5_prep_kernel,
        grid=(seg, S5_SLABS),
        in_specs=[pl.BlockSpec((None, 8, 128), lambda gl, j: (j * seg + gl, 0, 0)), vec, vec, vec, vec,
                  exp_spec, exp_spec],
        out_specs=[
            rows_spec, rows_spec,
            pl.BlockSpec((None, None, 256, S5_SLAB_W), lambda gl, j: (j, gl, 0, 0)),
            pl.BlockSpec((None, 2, 128), lambda gl, j: (j * seg + gl, 0, 0)),
        ],
        out_shape=[
            jax.ShapeDtypeStruct((S5_SLABS, S5_CHUNK, seg, S5_GROUP, S5_SLAB_W), BF16),
            jax.ShapeDtypeStruct((S5_SLABS, S5_CHUNK, seg, S5_GROUP, S5_SLAB_W), BF16),
            jax.ShapeDtypeStruct((S5_SLABS, seg, 256, S5_SLAB_W), BF16),
            jax.ShapeDtypeStruct((S5_GROUPS, 2, 128), F32),
        ],
        name="s5_prep",
    )(par, b_t(b_re), b_t(b_im), c_t(c_re), c_t(c_im), jnp.asarray(spread, BF16), jnp.asarray(place, BF16))
    mat = (S5_SLABS, S5_SLAB_W, S5_SLAB_W)
    dj = jnp.tile(d_skip.astype(F32).reshape(S5_SLABS, 1, 128), (1, 1, S5_CHUNK))
    return wt.reshape(mat), web.reshape(mat), wca.reshape(mat), a16.reshape(1, S5_SLABS * S5_SLAB_W), dj


S5_STATE_COLS = S5_SLABS * S5_SLAB_W // 128
S5_SLAB_COLS = S5_SLAB_W // 128


@functools.lru_cache(maxsize=None)
def _s5_row_perms():
    assert S5_ROW_TILE == S5_ROWS_C == DEC_BATCH * TM // S5_CHUNK
    perm = np.zeros((3, S5_ROW_TILE, S5_ROW_TILE), np.float32)
    for p, (nseq, nchunk) in enumerate(((BATCH, SEQ // S5_CHUNK), (DEC_BATCH, TM // S5_CHUNK),
                                        (DEC_BATCH, TM // S5_CHUNK))):
        b, c = np.meshgrid(np.arange(nseq), np.arange(nchunk), indexing="ij")
        perm[p, (c * nseq + b).ravel(), (b * nchunk + c).ravel()] = 1.0
    return perm, perm.transpose(0, 2, 1).copy()


def _s5_state_kernel(u_ref, perm_ref, w_ref, o_ref):
    u = _dot(perm_ref[...], u_ref[...].astype(BF16)).astype(BF16)
    s = _dot(u, w_ref[...])
    for k in range(S5_SLAB_COLS):
        o_ref[k] = s[:, k * 128:(k + 1) * 128]


def _s5_state(uj, web):
    perm, _ = _s5_row_perms()
    return pl.pallas_call(
        _s5_state_kernel,
        grid=(S5_SLABS, S5_ROWS // S5_ROW_TILE),
        in_specs=[
            pl.BlockSpec((None, S5_ROW_TILE, S5_SLAB_W), lambda j, p: (j, p, 0)),
            pl.BlockSpec((None, S5_ROW_TILE, S5_ROW_TILE), lambda j, p: (p, 0, 0)),
            pl.BlockSpec((None, S5_SLAB_W, S5_SLAB_W), lambda j, p: (j, 0, 0)),
        ],
        out_specs=pl.BlockSpec((S5_SLAB_COLS, S5_ROW_TILE, 128), lambda j, p: (j, p, 0)),
        out_shape=jax.ShapeDtypeStruct((S5_STATE_COLS, S5_ROWS, 128), F32),
        compiler_params=pltpu.CompilerParams(vmem_limit_bytes=VMEM_LIMIT),
        name="s5_state",
    )(uj, jnp.asarray(perm, BF16), web)


S5_SCAN_COLS = 8


def _s5_scan_kernel(s_ref, a_ref, h0_ref, hin_ref, fin_ref, hf, hb):
    ncol = S5_SCAN_COLS

    def scan(row0, nc, nb, h0):
        is_f = lax.broadcasted_iota(jnp.int32, (nb, 128), 1) < S5_STATE
        chunk_rows = lambda c: pl.ds(pl.multiple_of(row0 + c * nb, 8), nb)

        def body(c, hs):
            rf = chunk_rows(c)
            rb = chunk_rows(nc - 1 - c)
            new = []
            for m in range(ncol // 2):
                h_re, h_im = hs[2 * m], hs[2 * m + 1]
                a_re = a_ref[:, (2 * m) * 128:(2 * m + 1) * 128]
                a_im = a_ref[:, (2 * m + 1) * 128:(2 * m + 2) * 128]
                loc = []
                for k, h in ((2 * m, h_re), (2 * m + 1, h_im)):
                    hf[k, rf, :] = h
                    hb[k, rb, :] = h
                    loc.append(jnp.where(is_f, s_ref[k, rf, :], s_ref[k, rb, :]))
                new.append(a_re * h_re - a_im * h_im + loc[0])
                new.append(a_re * h_im + a_im * h_re + loc[1])
            return tuple(new)

        return lax.fori_loop(0, nc, body, h0)

    fin = scan(0, SEQ // S5_CHUNK, BATCH, tuple(jnp.zeros((BATCH, 128), F32) for _ in range(ncol)))
    for k in range(ncol):
        fin_ref[:, k * 128:(k + 1) * 128] = fin[k]
    scan(S5_ROWS_C, DEC_SEQ // S5_CHUNK, DEC_BATCH,
         tuple(h0_ref[:, k * 128:(k + 1) * 128] for k in range(ncol)))
    fwd = lax.broadcasted_iota(jnp.int32, (ncol, S5_ROWS, 128), 2) < S5_STATE
    hin_ref[...] = jnp.where(fwd, hf[...], hb[...])


def _s5_scan(sloc, a16, h0l):
    ncol = S5_SCAN_COLS
    w = ncol * 128
    return pl.pallas_call(
        _s5_scan_kernel,
        grid=(S5_STATE_COLS // ncol,),
        in_specs=[
            pl.BlockSpec((ncol, S5_ROWS, 128), lambda k: (k, 0, 0)),
            pl.BlockSpec((1, w), lambda k: (0, k)),
            pl.BlockSpec((DEC_BATCH, w), lambda k: (0, k)),
        ],
        out_specs=[
            pl.BlockSpec((ncol, S5_ROWS, 128), lambda k: (k, 0, 0)),
            pl.BlockSpec((BATCH, w), lambda k: (0, k)),
        ],
        out_shape=[
            jax.ShapeDtypeStruct((S5_STATE_COLS, S5_ROWS, 128), F32),
            jax.ShapeDtypeStruct((BATCH, S5_STATE_COLS * 128), F32),
        ],
        scratch_shapes=[pltpu.VMEM((ncol, S5_ROWS, 128), F32)] * 2,
        name="s5_scan",
    )(sloc, a16, h0l)


def _s5_out_kernel(u_ref, hin_ref, perm_t_ref, wt_ref, wca_ref, d_ref, y_ref):
    u = u_ref[...]
    hin = jnp.concatenate([hin_ref[k] for k in range(S5_SLAB_COLS)], axis=1).astype(BF16)
    hin = _dot(perm_t_ref[...], hin).astype(BF16)
    y = _dot(u.astype(BF16), wt_ref[...]) + _dot(hin, wca_ref[...]) + u * d_ref[...]
    for t in range(S5_CHUNK):
        y_ref[pl.ds(t, S5_ROW_TILE, stride=S5_CHUNK), :] = y[:, t * 128:(t + 1) * 128]


def _s5_out(uj, hin, wt, wca, dj):
    _, perm_t = _s5_row_perms()
    return pl.pallas_call(
        _s5_out_kernel,
        grid=(S5_SLABS, S5_ROWS // S5_ROW_TILE),
        in_specs=[
            pl.BlockSpec((None, S5_ROW_TILE, S5_SLAB_W), lambda j, p: (j, p, 0)),
            pl.BlockSpec((S5_SLAB_COLS, S5_ROW_TILE, 128), lambda j, p: (j, p, 0)),
            pl.BlockSpec((None, S5_ROW_TILE, S5_ROW_TILE), lambda j, p: (p, 0, 0)),
            pl.BlockSpec((None, S5_SLAB_W, S5_SLAB_W), lambda j, p: (j, 0, 0)),
            pl.BlockSpec((None, S5_SLAB_W, S5_SLAB_W), lambda j, p: (j, 0, 0)),
            pl.BlockSpec((None, 1, S5_SLAB_W), lambda j, p: (j, 0, 0)),
        ],
        out_specs=pl.BlockSpec((None, S5_ROW_TILE * S5_CHUNK, 128), lambda j, p: (j, p, 0)),
        out_shape=jax.ShapeDtypeStruct((S5_SLABS, NTOK, 128), F32),
        compiler_params=pltpu.CompilerParams(vmem_limit_bytes=VMEM_LIMIT),
        name="s5_out",
    )(uj, hin, jnp.asarray(perm_t, BF16), wt, wca, dj)


@functools.lru_cache(maxsize=None)
def _gla_consts():
    n = GLA_BLK
    nl = GLA_LEVELS
    r = np.arange(n)
    up = np.zeros((n, 128), np.int32)
    for l in range(nl):
        up[:, l] = (r >> l) & 1
    i = r[:, None]
    j = r[None, :]
    x = np.maximum(i ^ j, 1)
    lev = np.where(j < i, np.floor(np.log2(x)).astype(np.int32), np.where(i == j, nl, -1)).astype(np.int32)
    up2 = np.stack([up, up[::-1]])
    h = n // 2

    def tiled(a):
        return np.stack([np.concatenate([a[:h, :h], a[h:, h:]]), np.concatenate([a[:h, h:], a[h:, :h]])])

    lev2 = np.stack([tiled(lev), tiled(lev[::-1, ::-1])])
    return up2, lev2


@functools.lru_cache(maxsize=None)
def _gla_tables():
    rowblk, seq, first, last = [], [], [], []
    for d in range(2):
        rb, sq, fi, la = [], [], [], []
        for s in range(BATCH + DEC_BATCH):
            nblk = 1 if s < BATCH else DEC_SEQ // GLA_BLK
            base = s if s < BATCH else NTOK_C // GLA_BLK + (s - BATCH) * nblk
            order = range(nblk) if d == 0 else range(nblk - 1, -1, -1)
            for pos, b in enumerate(order):
                rb.append(base + b)
                sq.append(s)
                fi.append(int(pos == 0))
                la.append(int(pos == nblk - 1))
        rowblk.append(rb); seq.append(sq); first.append(fi); last.append(la)
    as_np = lambda a: np.asarray(a, np.int32)
    return as_np(rowblk), as_np(seq), as_np(first), as_np(last)


def _gla_kernel(rowblk_ref, seq_ref, first_ref, last_ref,
                qf_ref, kf_ref, vf_ref, lrf_ref, qb_ref, kb_ref, vb_ref, lrb_ref,
                wgk_ref, bgk_ref, up_ref, lev_ref, s0_ref,
                of_ref, ob_ref, fin_ref, z_scr, st_scr):
    del rowblk_ref, seq_ref
    n = pl.program_id(0)

    @pl.when(first_ref[n] == 1)
    def _():
        st_scr[...] = jnp.zeros_like(st_scr)
        for d in range(2):
            for h in range(GLA_HEADS):
                st_scr[d, h * GLA_DK:(h + 1) * GLA_DK, h * GLA_DV:(h + 1) * GLA_DV] = s0_ref[d, h]

    _gla_block(False, qf_ref, kf_ref, vf_ref, lrf_ref, wgk_ref.at[0], bgk_ref.at[0], up_ref.at[0], lev_ref.at[0],
               of_ref, z_scr.at[0], st_scr.at[0])
    _gla_block(True, qb_ref, kb_ref, vb_ref, lrb_ref, wgk_ref.at[1], bgk_ref.at[1], up_ref.at[1], lev_ref.at[1],
               ob_ref, z_scr.at[1], st_scr.at[1])

    @pl.when(last_ref[n] == 1)
    def _():
        for d in range(2):
            for h in range(GLA_HEADS):
                fin_ref[d, h] = st_scr[d, h * GLA_DK:(h + 1) * GLA_DK, h * GLA_DV:(h + 1) * GLA_DV]


def _gla_block(backward, q_ref, k_ref, v_ref, lr_ref, wgk_ref, bgk_ref, up_ref, lev_ref, o_ref, z_scr, st_scr):
    nl = GLA_LEVELS
    blk = GLA_BLK
    q = q_ref[...] * (GLA_DK ** -0.5)
    k = k_ref[...]
    vb = v_ref[...].astype(BF16)
    x = _dot(lr_ref[...].astype(BF16), wgk_ref[...]) + bgk_ref[...]
    gk = (jnp.minimum(x, 0.0) - jnp.log(1.0 + jnp.exp(-jnp.abs(x)))) * (1.0 / GLA_NORMALIZER)
    g_hi = gk.astype(BF16)
    g_lo = (gk - g_hi.astype(F32)).astype(BF16)
    ones = jnp.ones((blk, 128), BF16)
    tot = _dot_tn(g_hi, ones) + _dot_tn(g_lo, ones)

    row = lax.broadcasted_iota(jnp.int32, (blk, 1), 0)

    def sibling(a, l):
        g = 1 << l
        if g < 8:
            a3 = a.reshape(blk // 8, 8, a.shape[-1])
            dn = pltpu.roll(a3, g, 1).reshape(a.shape)
            up_ = pltpu.roll(a3, 8 - g, 1).reshape(a.shape)
            return jnp.where(((row >> l) & 1) == 1, dn, up_)
        a4 = a.reshape(blk // (2 * g), 2, g, a.shape[-1])
        return jnp.concatenate([a4[:, 1:2], a4[:, 0:1]], axis=1).reshape(a.shape)

    part = gk
    total = gk
    z0 = None
    for l in range(nl):
        g = 1 << l
        if g < 8:
            up = up_ref[:, l:l + 1] != 0
            z = jnp.where(up, q, k) * jnp.exp(jnp.where(up, part, total - part))
            other = sibling(total, l)
            part = part + jnp.where(up, other, 0.0)
            total = total + other
        else:
            halves = lambda a: (a.reshape(blk // (2 * g), 2, g, a.shape[-1])[:, 1 - int(backward)],
                                a.reshape(blk // (2 * g), 2, g, a.shape[-1])[:, int(backward)])
            join = lambda u, d: jnp.stack([d, u] if not backward else [u, d], axis=1).reshape(blk, u.shape[-1])
            part_u, part_d = halves(part)
            tot_u, tot_d = halves(total)
            q_u, _ = halves(q)
            _, k_d = halves(k)
            z = join(q_u * jnp.exp(part_u), k_d * jnp.exp(tot_d - part_d))
            part = join(part_u + tot_d, part_d)
            both = tot_u + tot_d
            total = join(both, both)
        if l == 0:
            z0 = z
        else:
            z_scr[l] = z.astype(BF16)
    lane128 = lax.broadcasted_iota(jnp.int32, (GLA_QK, 128), 1)
    dim = lax.broadcasted_iota(jnp.int32, (GLA_QK, 128), 0)
    head_sum = ((dim >> 6) == lane128).astype(BF16)
    pair0 = _dot((z0 * sibling(z0, 0)).astype(BF16), head_sum)
    diag = _dot((q * k).astype(BF16), head_sum)

    half = blk // 2
    lev_d = lev_ref[0]
    lev_o = lev_ref[1]
    lane = lax.broadcasted_iota(jnp.int32, (half, GLA_QK), 1)

    def tiles(l, in_head, crossed):
        out = []
        for r in range(2):
            c = 1 - r if crossed else r
            lhs = z_scr[l, r * half:(r + 1) * half, :]
            keys = z_scr[l, c * half:(c + 1) * half, :]
            out.append(_dot_nt(lhs, jnp.where(in_head, keys, jnp.zeros_like(keys))))
        return jnp.concatenate(out, axis=0)

    upi = 0 if backward else 1
    key_lanes = {}
    for l in range(3, nl - 1):
        g = 1 << l
        c = lax.broadcasted_iota(jnp.int32, (blk // (2 * g), g, 128), 0)
        ln = lax.broadcasted_iota(jnp.int32, (blk // (2 * g), g, 128), 2)
        base = (2 * g * c + (g if backward else 0)) & 127
        key_lanes[l] = (ln >= base) & (ln < base + g)

    for h in range(GLA_HEADS):
        in_head = (lane >= h * GLA_DK) & (lane < (h + 1) * GLA_DK)
        acc = jnp.where(lev_d == nl, diag[:, h:h + 1], 0.0)
        acc = jnp.where(lev_d == 0, pair0[:, h:h + 1], acc)
        for l in range(1, 3):
            acc = jnp.where(lev_d == l, tiles(l, in_head, False), acc)
        for l in range(3, nl - 1):
            g = 1 << l
            acc4 = acc.reshape(blk // (2 * g), 2, g, 128)
            s4 = tiles(l, in_head, False).reshape(blk // (2 * g), 2, g, 128)
            new_up = jnp.where(key_lanes[l], s4[:, upi], acc4[:, upi])
            pieces = [acc4[:, 0], new_up] if upi == 1 else [new_up, acc4[:, 1]]
            acc = jnp.stack(pieces, axis=1).reshape(blk, 128)
        off = jnp.where(lev_o == nl - 1, tiles(nl - 1, in_head, True), 0.0)
        att = jnp.concatenate([jnp.concatenate([acc[:half], off[:half]], axis=1),
                               jnp.concatenate([off[half:], acc[half:]], axis=1)], axis=0)
        o_ref[:, h * GLA_DV:(h + 1) * GLA_DV] = _dot(att.astype(BF16), vb[:, h * GLA_DV:(h + 1) * GLA_DV])

    st = st_scr[...]
    q_in = (q * jnp.exp(part)).astype(BF16)
    o_ref[...] += _dot(q_in, st.astype(BF16))
    k_out = (k * jnp.exp(total - part)).astype(BF16)
    kv = _dot_tn(k_out, vb)
    row = lax.broadcasted_iota(jnp.int32, (GLA_QK, GLA_V), 0)
    col = lax.broadcasted_iota(jnp.int32, (GLA_QK, GLA_V), 1)
    same_head = (row >> 6) == (col >> 7)
    decay = jnp.exp(tot)
    decay = jnp.concatenate([decay] * GLA_HEADS, axis=1)
    st_new = decay * st + jnp.where(same_head, kv, 0.0)
    st_scr[...] = st_new


def _gla_mix(bslab, lr, wgk, bgk, s0):
    up, lev = _gla_consts()
    rowblk, seq, first, last = _gla_tables()
    nsteps = rowblk.shape[1]
    nseq = BATCH + DEC_BATCH
    nl = GLA_LEVELS
    whole = lambda shape: pl.BlockSpec(shape, lambda n, rb, sq, fi, la: (0,) * len(shape))

    def token_specs(d):
        return [
            pl.BlockSpec((GLA_BLK, GLA_QK), lambda n, rb, sq, fi, la: (rb[d, n], 0)),
            pl.BlockSpec((GLA_BLK, GLA_QK), lambda n, rb, sq, fi, la: (rb[d, n], 1)),
            pl.BlockSpec((GLA_BLK, GLA_V), lambda n, rb, sq, fi, la: (rb[d, n], 1)),
            pl.BlockSpec((GLA_BLK, 128), lambda n, rb, sq, fi, la: (rb[d, n], 0)),
        ]

    state_spec = pl.BlockSpec((None, 2, GLA_HEADS, GLA_DK, GLA_DV), lambda n, rb, sq, fi, la: (sq[n], 0, 0, 0, 0))
    grid_spec = pltpu.PrefetchScalarGridSpec(
        num_scalar_prefetch=4,
        grid=(nsteps,),
        in_specs=token_specs(0) + token_specs(1) + [
            whole((2, 128, GLA_QK)),
            whole((2, 1, GLA_QK)),
            whole((2, GLA_BLK, 128)),
            whole((2, 2, GLA_BLK, GLA_BLK // 2)),
            state_spec,
        ],
        out_specs=[
            pl.BlockSpec((GLA_BLK, GLA_V), lambda n, rb, sq, fi, la: (rb[0, n], 0)),
            pl.BlockSpec((GLA_BLK, GLA_V), lambda n, rb, sq, fi, la: (rb[1, n], 0)),
            state_spec,
        ],
        scratch_shapes=[
            pltpu.VMEM((2, nl, GLA_BLK, GLA_QK), BF16),
            pltpu.VMEM((2, GLA_QK, GLA_V), F32),
        ],
    )
    return pl.pallas_call(
        _gla_kernel,
        grid_spec=grid_spec,
        out_shape=[
            jax.ShapeDtypeStruct((NTOK, GLA_V), F32),
            jax.ShapeDtypeStruct((NTOK, GLA_V), F32),
            jax.ShapeDtypeStruct((nseq, 2, GLA_HEADS, GLA_DK, GLA_DV), F32),
        ],
        compiler_params=pltpu.CompilerParams(vmem_limit_bytes=VMEM_LIMIT),
        name="gla_mix",
    )(jnp.asarray(rowblk), jnp.asarray(seq[0]), jnp.asarray(first[0]), jnp.asarray(last[0]),
      bslab, bslab, bslab, lr, bslab, bslab, bslab, lr, wgk, bgk, jnp.asarray(up), jnp.asarray(lev), s0)


def _attn_ctx_kernel(sink_ref, q_ref, k_ref, v_ref, o_ref):
    k = k_ref[...]
    v = v_ref[...]
    ks = (k.astype(BF16), pltpu.roll(k, 64, 1).astype(BF16))
    vs = (v.astype(BF16), pltpu.roll(v, 64, 1).astype(BF16))
    lo = lax.broadcasted_iota(jnp.int32, (SEQ, 128), 1) < HEAD_DIM
    units = []
    for t in range(ATT_HEADS // 2):
        qt = q_ref[:, t * 128:(t + 1) * 128] * (HEAD_DIM ** -0.5)
        for p in range(2):
            qm = jnp.where(lo if p == 0 else jnp.logical_not(lo), qt, 0.0).astype(BF16)
            units.append((qm, 0 if p == t // 2 else 1, sink_ref[2 * t + p]))
    scores = [_dot_nt(qm, ks[which]) for qm, which, _ in units]
    maxes = [jnp.maximum(sink, jnp.max(s, axis=-1, keepdims=True)) for s, (_, _, sink) in zip(scores, units)]
    probs = [jnp.exp(s - m) for s, m in zip(scores, maxes)]
    dens = [jnp.exp(sink - m) + jnp.sum(p, axis=-1, keepdims=True)
            for p, m, (_, _, sink) in zip(probs, maxes, units)]
    outs = [_dot(p.astype(BF16), vs[which]) / den for p, den, (_, which, _) in zip(probs, dens, units)]
    for t in range(ATT_HEADS // 2):
        o_ref[:, t * 128:(t + 1) * 128] = jnp.where(lo, outs[2 * t], outs[2 * t + 1]).astype(BF16)


def _attn_ctx(sink, cslab):
    return pl.pallas_call(
        _attn_ctx_kernel,
        grid=(BATCH,),
        in_specs=[
            pl.BlockSpec(memory_space=pltpu.SMEM),
            pl.BlockSpec((SEQ, ATT_Q), lambda b: (b, 0)),
            pl.BlockSpec((SEQ, ATT_KV), lambda b: (b, 4)),
            pl.BlockSpec((SEQ, ATT_KV), lambda b: (b, 5)),
        ],
        out_specs=pl.BlockSpec((SEQ, ATT_Q), lambda b: (b, 0)),
        out_shape=jax.ShapeDtypeStruct((NTOK_C, ATT_Q), BF16),
        name="attn_ctx",
    )(sink, cslab, cslab, cslab)


def _attn_lat_kernel(sink_ref, q_ref, kp_ref, kc_ref, kn_ref, vp_ref, vc_ref, vn_ref,
                     ck_ref, cv_ref, cos_ref, sin_ref, bias_ref, o_ref):
    j = pl.program_id(1)
    nb = DEC_SEQ // ATT_BLOCK
    lane = lax.broadcasted_iota(jnp.int32, (ATT_BLOCK, 128), 1)
    lo = lane < HEAD_DIM
    first16 = (lane & 31) < 16

    def rope(x, blk_idx):
        r0 = pl.multiple_of(blk_idx * ATT_BLOCK, ATT_BLOCK)
        c = cos_ref[pl.ds(r0, ATT_BLOCK), :]
        s = sin_ref[pl.ds(r0, ATT_BLOCK), :]
        xs = jnp.where(first16, pltpu.roll(x, 112, 1), pltpu.roll(x, 16, 1))
        return x * c + xs * s

    nwin = 3 * ATT_BLOCK
    keys = jnp.concatenate([rope(kp_ref[...], jnp.maximum(j - 1, 0)), rope(kc_ref[...], j),
                            rope(kn_ref[...], jnp.minimum(j + 1, nb - 1)), ck_ref[...]], axis=0)
    vals = jnp.concatenate([vp_ref[...], vc_ref[...], vn_ref[...], cv_ref[...]], axis=0)
    keys2 = (keys.astype(BF16), pltpu.roll(keys, 64, 1).astype(BF16))
    vals2 = (vals.astype(BF16), pltpu.roll(vals, 64, 1).astype(BF16))
    kcol = lax.broadcasted_iota(jnp.int32, (1, nwin + PAST_LEN), 1)
    edge = jnp.where(((j == 0) & (kcol < ATT_BLOCK)) | ((j == nb - 1) & (kcol >= 2 * ATT_BLOCK) & (kcol < nwin)),
                     -1e30, 0.0)
    bias = bias_ref[...] + edge
    top = lax.broadcasted_iota(jnp.int32, (2 * ATT_BLOCK, 1), 0) < ATT_BLOCK
    q_tiles = [rope(q_ref[:, t * 128:(t + 1) * 128], j) * (HEAD_DIM ** -0.5) for t in range(ATT_HEADS // 2)]
    lo2 = jnp.concatenate([lo, lo], axis=0)
    units = []
    for kvh in range(ATT_KV_HEADS):
        q2 = jnp.concatenate(q_tiles[2 * kvh:2 * kvh + 2], axis=0)
        for p in range(2):
            qm = jnp.where(lo2 if p == 0 else jnp.logical_not(lo2), q2, 0.0).astype(BF16)
            sink = jnp.where(top, sink_ref[4 * kvh + p], sink_ref[4 * kvh + 2 + p])
            units.append((qm, 0 if p == kvh else 1, sink))
    scores = [_dot_nt(qm, keys2[which]) + bias for qm, which, _ in units]
    maxes = [jnp.maximum(sink, jnp.max(s, axis=-1, keepdims=True)) for s, (_, _, sink) in zip(scores, units)]
    probs = [jnp.exp(s - m) for s, m in zip(scores, maxes)]
    dens = [jnp.exp(sink - m) + jnp.sum(p, axis=-1, keepdims=True)
            for p, m, (_, _, sink) in zip(probs, maxes, units)]
    outs = [_dot(p.astype(BF16), vals2[which]) / den for p, den, (_, which, _) in zip(probs, dens, units)]
    for kvh in range(ATT_KV_HEADS):
        o2 = jnp.where(lo2, outs[2 * kvh], outs[2 * kvh + 1])
        for i in range(2):
            t = 2 * kvh + i
            o_ref[:, t * 128:(t + 1) * 128] = o2[i * ATT_BLOCK:(i + 1) * ATT_BLOCK].astype(BF16)


def _attn_lat(sink, cslab, ck, cv, cos_t, sin_t):
    nb = DEC_SEQ // ATT_BLOCK
    base = NTOK_C // ATT_BLOCK
    cur = lambda b, j: base + b * nb + j
    prv = lambda b, j: base + b * nb + jnp.maximum(j - 1, 0)
    nxt = lambda b, j: base + b * nb + jnp.minimum(j + 1, nb - 1)
    kv_spec = lambda row, col: pl.BlockSpec((ATT_BLOCK, ATT_KV), lambda b, j: (row(b, j), col))
    qi = np.arange(2 * ATT_BLOCK)[:, None] % ATT_BLOCK
    kc = np.arange(3 * ATT_BLOCK + PAST_LEN)[None, :]
    inside = (np.abs(kc - ATT_BLOCK - qi) <= WINDOW) | (kc >= 3 * ATT_BLOCK)
    band = np.where(inside, 0.0, -1e30).astype(np.float32)
    return pl.pallas_call(
        _attn_lat_kernel,
        grid=(DEC_BATCH, nb),
        in_specs=[
            pl.BlockSpec(memory_space=pltpu.SMEM),
            pl.BlockSpec((ATT_BLOCK, ATT_Q), lambda b, j: (cur(b, j), 0)),
            kv_spec(prv, 4), kv_spec(cur, 4), kv_spec(nxt, 4),
            kv_spec(prv, 5), kv_spec(cur, 5), kv_spec(nxt, 5),
            pl.BlockSpec((None, PAST_LEN, ATT_KV), lambda b, j: (b, 0, 0)),
            pl.BlockSpec((None, PAST_LEN, ATT_KV), lambda b, j: (b, 0, 0)),
            pl.BlockSpec((DEC_SEQ, 128), lambda b, j: (0, 0)),
            pl.BlockSpec((DEC_SEQ, 128), lambda b, j: (0, 0)),
            pl.BlockSpec(band.shape, lambda b, j: (0, 0)),
        ],
        out_specs=pl.BlockSpec((ATT_BLOCK, ATT_Q), lambda b, j: (b * nb + j, 0)),
        out_shape=jax.ShapeDtypeStruct((NTOK_L, ATT_Q), BF16),
        name="attn_lat",
    )(sink, cslab, cslab, cslab, cslab, cslab, cslab, cslab, ck, cv, cos_t, sin_t, jnp.asarray(band))


def _rope_tables():
    rows = DEC_SEQ // GRID_W
    row = np.repeat(np.arange(rows, dtype=np.float32), GRID_W)
    col = np.tile(np.arange(GRID_W, dtype=np.float32), rows)
    quarter = HEAD_DIM // 4
    inv = jnp.asarray(ROPE_BASE, F32) ** (-jnp.arange(quarter, dtype=F32) / quarter)
    lane = np.arange(128)
    use_row = (lane % HEAD_DIM) < HEAD_DIM // 2
    pos = jnp.where(use_row[None, :], jnp.asarray(row)[:, None], jnp.asarray(col)[:, None])
    ang = pos * inv[lane % quarter][None, :]
    sign = np.where((lane % 32) < 16, -1.0, 1.0).astype(np.float32)
    return jnp.cos(ang), jnp.sin(ang) * sign[None, :]


_STAGE_END = object()
MERGE_PARTS = 2


def _merge_kernel(*refs, split_x):
    if split_x:
        xc_ref, xl_ref, *refs = refs
    else:
        xc_ref, *refs = refs
    (mod_ref, g_ref, ys5_ref, ogf_ref, ogb_ref, gb_ref, ycc_ref, ycl_ref, gate_ref, gng_ref,
     wglu_ref, wbr_ref, wout_ref, o_ref) = refs
    is_ctx = pl.program_id(0) < NTOK_C // TM
    gng = gng_ref[...]
    g1 = mod_ref[:, 2 * D_MODEL:3 * D_MODEL]

    def rows_stage(rows):
        y = jnp.concatenate([ys5_ref[j, rows, :] for j in range(S5_SLABS)], axis=1)
        y = (0.5 * y * (1.0 + jnp.tanh(math.sqrt(2.0 / math.pi) * (y + 0.044715 * (y * y * y))))).astype(BF16)
        yield
        ag = _dot(y, wglu_ref[...])
        yield
        y_a = ag[:, :S5_WIDTH] * _sigmoid(ag[:, S5_WIDTH:])
        parts = []
        for h in range(GLA_HEADS):
            sl = slice(h * GLA_DV, (h + 1) * GLA_DV)
            o = ogf_ref[rows, sl] + ogb_ref[rows, sl]
            g = gb_ref[rows, sl]
            parts.append(_rms(o, gng) * (g * _sigmoid(g)))
        y_b = jnp.concatenate(parts, axis=1)
        y_c = jnp.where(is_ctx, ycc_ref[rows, :], ycl_ref[rows, :])
        branches = [yn.astype(BF16) for yn in (y_a, y_b, y_c)]
        yield
        projs = [_dot(yn, wbr_ref[n]) for n, yn in enumerate(branches)]
        yield
        merged = None
        for n, proj in enumerate(projs):
            term = _sigmoid(gate_ref[rows, n * D_MODEL:(n + 1) * D_MODEL]) * proj
            merged = term if merged is None else merged + term
        merged = merged.astype(BF16)
        yield
        mixed = _dot(merged, wout_ref[...])
        yield
        if split_x:
            x = jnp.where(is_ctx, xc_ref[rows, :], xl_ref[rows, :])
        else:
            x = xc_ref[rows, :]
        o_ref[rows, :] = x + g1 * _rms(mixed, g_ref[...])

    part = TM // MERGE_PARTS
    waiting = [rows_stage(slice(k * part, (k + 1) * part)) for k in range(MERGE_PARTS)]
    live = []
    while waiting or live:
        if waiting:
            live.append(waiting.pop(0))
        live = [g for g in live if next(g, _STAGE_END) is not _STAGE_END]


def _layer_spec(shape, layer):
    return pl.BlockSpec((None,) + shape, lambda i: (layer,) + (0,) * len(shape), pipeline_mode=pl.Buffered(1))


def _split_token_specs(n_arrays, width=D_MODEL):
    nct = NTOK_C // TM
    if n_arrays == 2:
        return [pl.BlockSpec((TM, width), lambda i: (jnp.minimum(i, nct - 1), 0)),
                pl.BlockSpec((TM, width), lambda i: (jnp.maximum(i - nct, 0), 0))]
    return [pl.BlockSpec((TM, width), lambda i: (i, 0))]


def _merge(xs, mod, g, ys5, og, bslab, yc, gates, gng, wglu, wbr, wout, layer):
    tok = lambda width, col=0: pl.BlockSpec((TM, width), lambda i: (i, col))
    full = lambda shape: _layer_spec(shape, layer)
    return pl.pallas_call(
        functools.partial(_merge_kernel, split_x=len(xs) == 2),
        grid=(NTOK // TM,),
        in_specs=_split_token_specs(len(xs)) + [
            pl.BlockSpec((None, 1, 6 * D_MODEL), lambda i: (_mod_row(i), 0, 0)),
            pl.BlockSpec((1, D_MODEL), lambda i: (0, 0)),
            pl.BlockSpec((S5_SLABS, TM, 128), lambda i: (0, _s5_tile(i), 0)),
            tok(GLA_V),
            tok(GLA_V),
            tok(GLA_V, 2),
        ] + _split_token_specs(2, ATT_Q) + [
            tok(N_BRANCH * D_MODEL),
            pl.BlockSpec((1, GLA_DV), lambda i: (0, 0)),
            full((S5_WIDTH, 2 * S5_WIDTH)),
            full((N_BRANCH, BRANCH_W, D_MODEL)),
            full((D_MODEL, D_MODEL)),
        ],
        out_specs=tok(D_MODEL),
        out_shape=jax.ShapeDtypeStruct((NTOK, D_MODEL), F32),
        compiler_params=pltpu.CompilerParams(vmem_limit_bytes=VMEM_LIMIT),
        name="merge",
    )(*xs, mod, g, ys5, *og, bslab, *yc, gates, gng, wglu, wbr, wout)


FFN_SPLIT = 2


def _ffn_kernel(x_ref, mod_ref, gin_ref, gout_ref, w1_ref, w2_ref, *o_refs):
    x = x_ref[...]
    sh = mod_ref[:, 3 * D_MODEL:4 * D_MODEL]
    sc = mod_ref[:, 4 * D_MODEL:5 * D_MODEL]
    g2 = mod_ref[:, 5 * D_MODEL:6 * D_MODEL]
    h = (_rms(x, gin_ref[...]) * (1.0 + sc) + sh).astype(BF16)
    ck = FFN_HIDDEN // FFN_SPLIT
    acc = None
    for c in range(FFN_SPLIT):
        a = _dot(h, w1_ref[:, c * ck:(c + 1) * ck])
        b = _dot(h, w1_ref[:, FFN_HIDDEN + c * ck:FFN_HIDDEN + (c + 1) * ck])
        act = (a * _sigmoid(a) * b).astype(BF16)
        part = _dot(act, w2_ref[c * ck:(c + 1) * ck, :])
        acc = part if acc is None else acc + part
    y = x + g2 * _rms(acc, gout_ref[...])
    if len(o_refs) == 1:
        o_refs[0][...] = y
    else:
        is_ctx = pl.program_id(0) < NTOK_C // TM

        @pl.when(is_ctx)
        def _():
            o_refs[0][...] = y

        @pl.when(jnp.logical_not(is_ctx))
        def _():
            o_refs[1][...] = y


def _ffn(x, mod, gin, gout, w1, w2, layer, split_out):
    small = lambda shape: pl.BlockSpec(shape, lambda i: (0,) * len(shape))
    nct = NTOK_C // TM
    if split_out:
        out_specs = [pl.BlockSpec((TM, D_MODEL), lambda i: (jnp.minimum(i, nct - 1), 0)),
                     pl.BlockSpec((TM, D_MODEL), lambda i: (jnp.maximum(i - nct, 0), 0))]
        out_shape = [jax.ShapeDtypeStruct((NTOK_C, D_MODEL), F32), jax.ShapeDtypeStruct((NTOK_L, D_MODEL), F32)]
    else:
        out_specs = pl.BlockSpec((TM, D_MODEL), lambda i: (i, 0))
        out_shape = jax.ShapeDtypeStruct((NTOK, D_MODEL), F32)
    return pl.pallas_call(
        _ffn_kernel,
        grid=(NTOK // TM,),
        in_specs=[
            pl.BlockSpec((TM, D_MODEL), lambda i: (i, 0)),
            pl.BlockSpec((None, 1, 6 * D_MODEL), lambda i: (_mod_row(i), 0, 0)),
            small((1, D_MODEL)),
            small((1, D_MODEL)),
            _layer_spec((D_MODEL, 2 * FFN_HIDDEN), layer),
            _layer_spec((FFN_HIDDEN, D_MODEL), layer),
        ],
        out_specs=out_specs,
        out_shape=out_shape,
        compiler_params=pltpu.CompilerParams(vmem_limit_bytes=VMEM_LIMIT),
        name="ffn",
    )(x, mod, gin, gout, w1, w2)


def kernel(x_prompt, x_sample, cache_k, cache_v, state_s5, state_gla, c, c_ctx, w_mod, b_mod, norm_g, w_in,
           s5_lam_re, s5_lam_im, s5_log_step, s5_b_re, s5_b_im, s5_c_re, s5_c_im, s5_d, w_glu, gla_w_gk,
           gla_b_gk, gla_norm_g, att_sink, w_branch, w_out, w_ffn_in, w_ffn_out):
    cond = jnp.concatenate([c_ctx[None, :], c, jnp.zeros((N_MOD_ROWS - 1 - DEC_BATCH, D_MODEL), F32)], axis=0)
    mod_all = _modulation(cond, w_mod, b_mod).reshape(DEPTH, N_MOD_ROWS, 1, 6 * D_MODEL)
    cos_t, sin_t = _rope_tables()
    xs = (x_prompt.reshape(NTOK_C, D_MODEL), x_sample.reshape(NTOK_L, D_MODEL))
    w_in_b = w_in.astype(BF16)
    w_in_end = jnp.pad(w_in[:, :, D_IN_TILED:].astype(BF16), ((0, 0), (0, 0), (0, W_IN_COLS - D_IN)))
    w_glu_b, w_branch_b, w_out_b = w_glu.astype(BF16), w_branch.astype(BF16), w_out.astype(BF16)
    w_ffn_in_b, w_ffn_out_b = w_ffn_in.astype(BF16), w_ffn_out.astype(BF16)
    new_k, new_v, new_s5, new_gla = [], [], [], []
    for i in range(DEPTH):
        mod = mod_all[i]
        uj, bslab, cslab, gates, lr = _inproj(xs, mod, norm_g[i, 0][None, :], w_in_b, w_in_end, i)

        wt, web, wca, a16, dj = _s5_prep(s5_lam_re[i], s5_lam_im[i], s5_log_step[i], s5_b_re[i], s5_b_im[i],
                                         s5_c_re[i], s5_c_im[i], s5_d[i])
        h0l = state_s5[:, i].astype(F32).transpose(0, 2, 4, 1, 3).reshape(DEC_BATCH, S5_GROUPS * 256)
        hin, finc = _s5_scan(_s5_state(uj, web), a16, h0l)
        ys5 = _s5_out(uj, hin, wt, wca, dj)
        new_s5.append(finc.reshape(BATCH, S5_GROUPS, 2, 2, S5_STATE).transpose(0, 3, 1, 4, 2))

        wgk = jnp.zeros((2, 128, GLA_QK), F32)
        wgk = wgk.at[0, 0:GLA_RANK].set(gla_w_gk[i, 0]).at[1, GLA_RANK:2 * GLA_RANK].set(gla_w_gk[i, 1])
        s0 = jnp.concatenate([jnp.zeros((BATCH, 2, GLA_HEADS, GLA_DK, GLA_DV), F32),
                              state_gla[:, i].astype(F32)], axis=0)
        *og, gla_fin = _gla_mix(bslab, lr, wgk.astype(BF16), gla_b_gk[i][:, None, :].astype(F32), s0)
        new_gla.append(gla_fin[:BATCH])

        sink = att_sink[i].astype(F32)
        yc = (_attn_ctx(sink, cslab),
              _attn_lat(sink, cslab, cache_k[:, i].reshape(DEC_BATCH, PAST_LEN, ATT_KV).astype(F32),
                        cache_v[:, i].reshape(DEC_BATCH, PAST_LEN, ATT_KV).astype(F32), cos_t, sin_t))
        new_k.append(cslab[:NTOK_C, ATT_Q:ATT_Q + ATT_KV].reshape(BATCH, SEQ, ATT_KV_HEADS, HEAD_DIM))
        new_v.append(cslab[:NTOK_C, ATT_Q + ATT_KV:].reshape(BATCH, SEQ, ATT_KV_HEADS, HEAD_DIM))

        x = _merge(xs, mod, norm_g[i, 1][None, :], ys5, og, bslab, yc, gates, gla_norm_g[i][None, :],
                   w_glu_b, w_branch_b, w_out_b, i)
        last = i == DEPTH - 1
        x = _ffn(x, mod, norm_g[i, 2][None, :], norm_g[i, 3][None, :], w_ffn_in_b, w_ffn_out_b, i, last)
        xs = tuple(x) if last else (x,)

    return (xs[0].reshape(BATCH, SEQ, D_MODEL), xs[1].reshape(DEC_BATCH, DEC_SEQ, D_MODEL),
            jnp.stack(new_k, axis=1), jnp.stack(new_v, axis=1),
            jnp.stack(new_s5, axis=1), jnp.stack(new_gla, axis=1))
```

```python
import functools
import math

import numpy as np
import jax
import jax.numpy as jnp
from jax import lax
from jax.experimental import pallas as pl
from jax.experimental.pallas import tpu as pltpu

F32 = jnp.float32
BF16 = jnp.bfloat16

D_MODEL = 1024
BATCH = 16
SEQ = 256
DEPTH = 2
DEC_BATCH = 8
DEC_SEQ = 1024
PAST_LEN = 256
GRID_W = 64
ROPE_BASE = 10000.0
S5_WIDTH = 512
S5_GROUP = 16
S5_GROUPS = 32
S5_STATE = 64
GLA_HEADS = 4
GLA_DK = 64
GLA_DV = 128
GLA_QK = 256
GLA_V = 512
GLA_RANK = 16
GLA_NORMALIZER = 16.0
ATT_HEADS = 8
ATT_KV_HEADS = 2
HEAD_DIM = 64
ATT_Q = 512
ATT_KV = 128
WINDOW = 128
ATT_BLOCK = 128
N_BRANCH = 3
BRANCH_W = 512
FFN_HIDDEN = 2816
RMS_EPS = 1e-6

NTOK_C = BATCH * SEQ
NTOK_L = DEC_BATCH * DEC_SEQ
NTOK = NTOK_C + NTOK_L
TM = 512
N_MOD_ROWS = 16

D_IN = 5920
D_IN_TILED = D_IN // 128 * 128
W_IN_COLS = 6016
S5_CHUNK = 16
S5_SLABS = S5_WIDTH // 128
S5_SLAB_W = S5_CHUNK * 128
S5_ROWS_C = NTOK_C // S5_CHUNK
S5_ROWS = NTOK // S5_CHUNK
S5_ROW_TILE = 256
GLA_BLK = 256
GLA_LEVELS = 8
VMEM_LIMIT = 56 * 1024 * 1024


def _dot(a, b):
    return jnp.dot(a, b, preferred_element_type=F32)


def _dot_nt(a, b):
    return lax.dot_general(a, b, (((1,), (1,)), ((), ())), preferred_element_type=F32)


def _dot_tn(a, b):
    return lax.dot_general(a, b, (((0,), (0,)), ((), ())), preferred_element_type=F32)


def _rms(x, g):
    return x * lax.rsqrt(jnp.mean(x * x, axis=-1, keepdims=True) + RMS_EPS) * g


def _sigmoid(x):
    return 0.5 * jnp.tanh(0.5 * x) + 0.5


def _mod_row(i):
    nct = NTOK_C // TM
    return jnp.where(i < nct, 0, 1 + (i - nct) // (DEC_SEQ // TM))


def _s5_tile(i):
    nct = NTOK_C // TM
    per_seq = DEC_SEQ // TM
    k = i - nct
    return jnp.where(i < nct, i, nct + (k % per_seq) * DEC_BATCH + k // per_seq)


def _mod_kernel(c_ref, w_ref, b_ref, o_ref):
    c = c_ref[...]
    s = (c * _sigmoid(c)).astype(BF16)
    o_ref[...] = _dot(s, w_ref[...].astype(BF16)) + b_ref[...]


def _modulation(cond, w_mod, b_mod):
    tn = 2048
    return pl.pallas_call(
        _mod_kernel,
        grid=(DEPTH, 6 * D_MODEL // tn),
        in_specs=[
            pl.BlockSpec((N_MOD_ROWS, D_MODEL), lambda l, n: (0, 0)),
            pl.BlockSpec((None, D_MODEL, tn), lambda l, n: (l, 0, n)),
            pl.BlockSpec((None, 1, tn), lambda l, n: (l, 0, n)),
        ],
        out_specs=pl.BlockSpec((None, N_MOD_ROWS, tn), lambda l, n: (l, 0, n)),
        out_shape=jax.ShapeDtypeStruct((DEPTH, N_MOD_ROWS, 6 * D_MODEL), F32),
        name="modulation",
    )(cond, w_mod, b_mod.reshape(DEPTH, 1, 6 * D_MODEL))


_IN_SLABS = ((0, 512), (512, 1536), (2048, 768), (2816, 3072), (5888, 128))
W_IN_SPLIT = 2048
W_IN_GAP = 32
W_IN_TAIL = W_IN_COLS - W_IN_SPLIT


def _inproj_kernel(*refs, split_x):
    if split_x:
        xc_ref, xl_ref, *refs = refs
    else:
        xc_ref, *refs = refs
    mod_ref, g_ref, w_ref, w_end_ref, u_ref, b_ref, c_ref, gate_ref, lr_ref, w_tail, u_stage = refs
    i = pl.program_id(0)

    @pl.when(i == 0)
    def _():
        r = lax.broadcasted_iota(jnp.int32, (256, 128), 0)
        c = lax.broadcasted_iota(jnp.int32, (256, 128), 1)
        shift = (r == c + W_IN_GAP).astype(BF16)
        head = ((r == c) & (c < W_IN_GAP)).astype(BF16)
        ntile = (W_IN_TAIL - 128) // 128
        for t in range(ntile - 1):
            src = W_IN_SPLIT + 128 * t
            w_tail[:, 128 * t:128 * (t + 1)] = _dot(w_ref[:, src:src + 256], shift).astype(BF16)
        src = W_IN_SPLIT + 128 * (ntile - 1)
        last = jnp.concatenate([w_ref[:, src:src + 128], w_end_ref[...]], axis=1)
        w_tail[:, 128 * (ntile - 1):128 * ntile] = _dot(last, shift).astype(BF16)
        w_tail[:, 128 * ntile:] = _dot(w_ref[:, W_IN_SPLIT:W_IN_SPLIT + 256], head).astype(BF16)

    if split_x:
        x = jnp.where(i < NTOK_C // TM, xc_ref[...], xl_ref[...])
    else:
        x = xc_ref[...]
    mod = mod_ref[...]
    h = _rms(x, g_ref[...]) * (1.0 + mod[:, D_MODEL:2 * D_MODEL]) + mod[:, 0:D_MODEL]
    h = h.astype(BF16)
    for j in range(S5_SLABS):
        u_stage[...] = _dot(h, w_ref[:, j * 128:(j + 1) * 128])
        for s in range(S5_CHUNK):
            u_ref[j, :, s * 128:(s + 1) * 128] = u_stage[pl.ds(s, TM // S5_CHUNK, stride=S5_CHUNK), :]
    b_ref[...] = _dot(h, w_ref[:, 512:W_IN_SPLIT])
    for (off, width), o_ref in zip(_IN_SLABS[2:], (c_ref, gate_ref, lr_ref)):
        z = _dot(h, w_tail[:, off - W_IN_SPLIT:off - W_IN_SPLIT + width])
        o_ref[...] = _sigmoid(z).astype(BF16) if o_ref is gate_ref else z


def _inproj(xs, mod, g, w_all, w_end, layer):
    return pl.pallas_call(
        functools.partial(_inproj_kernel, split_x=len(xs) == 2),
        grid=(NTOK // TM,),
        in_specs=_split_token_specs(len(xs)) + [
            pl.BlockSpec((None, 1, 6 * D_MODEL), lambda i: (_mod_row(i), 0, 0)),
            pl.BlockSpec((1, D_MODEL), lambda i: (0, 0)),
            pl.BlockSpec((None, D_MODEL, D_IN), lambda i: (layer, 0, 0), pipeline_mode=pl.Buffered(1)),
            pl.BlockSpec((None, D_MODEL, 128), lambda i: (layer, 0, 0), pipeline_mode=pl.Buffered(1)),
        ],
        out_specs=[pl.BlockSpec((S5_SLABS, TM // S5_CHUNK, S5_SLAB_W), lambda i: (0, _s5_tile(i), 0))]
        + [pl.BlockSpec((TM, width), lambda i: (i, 0)) for _, width in _IN_SLABS[1:]],
        out_shape=[jax.ShapeDtypeStruct((S5_SLABS, S5_ROWS, S5_SLAB_W), F32)]
        + [jax.ShapeDtypeStruct((NTOK, width), BF16 if width == N_BRANCH * D_MODEL else F32)
           for _, width in _IN_SLABS[1:]],
        scratch_shapes=[pltpu.VMEM((D_MODEL, W_IN_TAIL), BF16), pltpu.VMEM((TM, 128), F32)],
        compiler_params=pltpu.CompilerParams(vmem_limit_bytes=VMEM_LIMIT),
        name="inproj",
    )(*xs, mod, g, w_all, w_end)


@functools.lru_cache(maxsize=None)
def _s5_expanders():
    seg = 8
    spread = np.zeros((seg, 256, S5_SLAB_W), np.float32)
    place = np.zeros((seg, 256, S5_SLAB_W), np.float32)
    col = np.arange(256)
    for gl in range(seg):
        spread[gl, col, (col // S5_GROUP) * 128 + gl * S5_GROUP + col % S5_GROUP] = 1.0
        place[gl, col, gl * 256 + col] = 1.0
    return spread, place


def _s5_prep_kernel(par_ref, bre_ref, bim_ref, cre_ref, cim_ref, spread_ref, place_ref,
                    wt_ref, web_ref, wca_ref, a16_ref):
    n = S5_CHUNK
    lam_re = par_ref[0:1, :]
    lam_im = par_ref[1:2, :]
    dt = jnp.exp(par_ref[2:3, :])
    lr = lam_re * dt
    li = lam_im * dt
    krow = lax.broadcasted_iota(jnp.int32, (24, 128), 0).astype(F32)
    tab_mag = jnp.exp(krow * lr)
    tab_re = tab_mag * jnp.cos(krow * li)
    tab_im = tab_mag * jnp.sin(krow * li)
    ar = tab_re[1:2, :]
    ai = tab_im[1:2, :]
    nr = ar - 1.0
    den = lam_re * lam_re + lam_im * lam_im
    fr = (nr * lam_re + ai * lam_im) / den
    fi = (ai * lam_re - nr * lam_im) / den
    b_re = bre_ref[...]
    b_im = bim_ref[...]
    br = fr * b_re - fi * b_im
    bi = fr * b_im + fi * b_re
    c_re = cre_ref[...]
    c_im = cim_ref[...]

    def lo_half(shape):
        return lax.broadcasted_iota(jnp.int32, shape, 1) < S5_STATE

    def tile_rows(a):
        return jnp.concatenate([a] * n, axis=0)

    fwd16 = lo_half((S5_GROUP, 128))

    def powers(t_re, t_im, k_fwd, k_bwd):
        def pick(t, b):
            kf, kb = k_fwd(b), k_bwd(b)
            return jnp.where(fwd16, jnp.broadcast_to(t[kf:kf + 1, :], (S5_GROUP, 128)),
                             jnp.broadcast_to(t[kb:kb + 1, :], (S5_GROUP, 128)))
        return (jnp.concatenate([pick(t_re, b) for b in range(n)], axis=0),
                jnp.concatenate([pick(t_im, b) for b in range(n)], axis=0))

    fwd = lo_half((n * S5_GROUP, 128))
    brt, bit, crt, cit = tile_rows(br), tile_rows(bi), tile_rows(c_re), tile_rows(c_im)

    per, pei = powers(tab_re, tab_im, lambda s: n - 1 - s, lambda s: s)
    eb = jnp.concatenate([brt * per - bit * pei, brt * pei + bit * per], axis=1)
    pcr, pci = powers(tab_re, tab_im, lambda t: t + 1, lambda t: n - t)
    ca = jnp.concatenate([(crt * pcr - cit * pci).T, (-(crt * pci + cit * pcr)).T], axis=0)

    def one_dir(x, d):
        sw = pltpu.roll(x, S5_STATE, 1)
        lo = lo_half(x.shape)
        return jnp.where(lo, x, sw) if d == 0 else jnp.where(lo, sw, x)

    klag = []
    for d in range(2):
        lhs = jnp.where(lo_half(br.shape), one_dir(br, d), -one_dir(bi, d))
        crd, cid = tile_rows(one_dir(c_re, d)), tile_rows(one_dir(c_im, d))
        lag = (lambda b: b) if d == 0 else (lambda b: n - 1 - b)
        pr, pi = powers(one_dir(tab_re, d), one_dir(tab_im, d), lag, lag)
        rhs_t = jnp.where(fwd, crd * pr - cid * pi, crd * pi + cid * pr)
        klag.append(lax.dot_general(lhs, rhs_t, (((1,), (1,)), ((), ())),
                                    precision=lax.Precision.HIGHEST, preferred_element_type=F32))
    lane = lax.broadcasted_iota(jnp.int32, (S5_GROUP, n * S5_GROUP), 1)
    rows = []
    for s in range(n):
        f = klag[0] if s == 0 else jnp.where(lane >= S5_GROUP * s, pltpu.roll(klag[0], S5_GROUP * s, 1), 0.0)
        sh = (n * S5_GROUP - S5_GROUP * (n - 1 - s)) % (n * S5_GROUP)
        b = klag[1] if sh == 0 else pltpu.roll(klag[1], sh, 1)
        rows.append(f + jnp.where(lane < S5_GROUP * (s + 1), b, 0.0))
    toep = jnp.concatenate(rows, axis=0)

    spread = spread_ref[...]
    wt_ref[...] = _dot(toep.astype(BF16), spread).astype(BF16).reshape(n, S5_GROUP, S5_SLAB_W)
    web_ref[...] = _dot(eb.astype(BF16), place_ref[...]).astype(BF16).reshape(n, S5_GROUP, S5_SLAB_W)
    wca_ref[...] = _dot(ca.astype(BF16), spread).astype(BF16)
    a16_ref[0:1, :] = tab_re[n:n + 1, :]
    a16_ref[1:2, :] = tab_im[n:n + 1, :]


def _s5_prep(lam_re, lam_im, log_step, b_re, b_im, c_re, c_im, d_skip):
    seg = 8
    par = jnp.stack([lam_re, lam_im, log_step]).astype(F32).transpose(2, 0, 1, 3).reshape(S5_GROUPS, 3, 128)
    par = jnp.concatenate([par, jnp.zeros((S5_GROUPS, 5, 128), F32)], axis=1)
    b_t = lambda b: b.astype(F32).transpose(1, 3, 0, 2).reshape(S5_GROUPS, S5_GROUP, 128)
    c_t = lambda c: c.astype(F32).transpose(1, 2, 0, 3).reshape(S5_GROUPS, S5_GROUP, 128)
    spread, place = _s5_expanders()
    vec = pl.BlockSpec((None, S5_GROUP, 128), lambda gl, j: (j * seg + gl, 0, 0))
    exp_spec = pl.BlockSpec((None, 256, S5_SLAB_W), lambda gl, j: (gl, 0, 0))
    rows_spec = pl.BlockSpec((None, S5_CHUNK, None, S5_GROUP, S5_SLAB_W), lambda gl, j: (j, 0, gl, 0, 0))
    wt, web, wca, a16 = pl.pallas_call(
        _s5_prep_kernel,
        grid=(seg, S5_SLABS),
        in_specs=[pl.BlockSpec((None, 8, 128), lambda gl, j: (j * seg + gl, 0, 0)), vec, vec, vec, vec,
                  exp_spec, exp_spec],
        out_specs=[
            rows_spec, rows_spec,
            pl.BlockSpec((None, None, 256, S5_SLAB_W), lambda gl, j: (j, gl, 0, 0)),
            pl.BlockSpec((None, 2, 128), lambda gl, j: (j * seg + gl, 0, 0)),
        ],
        out_shape=[
            jax.ShapeDtypeStruct((S5_SLABS, S5_CHUNK, seg, S5_GROUP, S5_SLAB_W), BF16),
            jax.ShapeDtypeStruct((S5_SLABS, S5_CHUNK, seg, S5_GROUP, S5_SLAB_W), BF16),
            jax.ShapeDtypeStruct((S5_SLABS, seg, 256, S5_SLAB_W), BF16),
            jax.ShapeDtypeStruct((S5_GROUPS, 2, 128), F32),
        ],
        name="s5_prep",
    )(par, b_t(b_re), b_t(b_im), c_t(c_re), c_t(c_im), jnp.asarray(spread, BF16), jnp.asarray(place, BF16))
    mat = (S5_SLABS, S5_SLAB_W, S5_SLAB_W)
    dj = jnp.tile(d_skip.astype(F32).reshape(S5_SLABS, 1, 128), (1, 1, S5_CHUNK))
    return wt.reshape(mat), web.reshape(mat), wca.reshape(mat), a16.reshape(1, S5_SLABS * S5_SLAB_W), dj


S5_STATE_COLS = S5_SLABS * S5_SLAB_W // 128
S5_SLAB_COLS = S5_SLAB_W // 128


@functools.lru_cache(maxsize=None)
def _s5_row_perms():
    assert S5_ROW_TILE == S5_ROWS_C == DEC_BATCH * TM // S5_CHUNK
    perm = np.zeros((3, S5_ROW_TILE, S5_ROW_TILE), np.float32)
    for p, (nseq, nchunk) in enumerate(((BATCH, SEQ // S5_CHUNK), (DEC_BATCH, TM // S5_CHUNK),
                                        (DEC_BATCH, TM // S5_CHUNK))):
        b, c = np.meshgrid(np.arange(nseq), np.arange(nchunk), indexing="ij")
        perm[p, (c * nseq + b).ravel(), (b * nchunk + c).ravel()] = 1.0
    return perm, perm.transpose(0, 2, 1).copy()


def _s5_state_kernel(u_ref, perm_ref, w_ref, o_ref):
    u = _dot(perm_ref[...], u_ref[...].astype(BF16)).astype(BF16)
    s = _dot(u, w_ref[...])
    for k in range(S5_SLAB_COLS):
        o_ref[k] = s[:, k * 128:(k + 1) * 128]


def _s5_state(uj, web):
    perm, _ = _s5_row_perms()
    return pl.pallas_call(
        _s5_state_kernel,
        grid=(S5_SLABS, S5_ROWS // S5_ROW_TILE),
        in_specs=[
            pl.BlockSpec((None, S5_ROW_TILE, S5_SLAB_W), lambda j, p: (j, p, 0)),
            pl.BlockSpec((None, S5_ROW_TILE, S5_ROW_TILE), lambda j, p: (p, 0, 0)),
            pl.BlockSpec((None, S5_SLAB_W, S5_SLAB_W), lambda j, p: (j, 0, 0)),
        ],
        out_specs=pl.BlockSpec((S5_SLAB_COLS, S5_ROW_TILE, 128), lambda j, p: (j, p, 0)),
        out_shape=jax.ShapeDtypeStruct((S5_STATE_COLS, S5_ROWS, 128), F32),
        compiler_params=pltpu.CompilerParams(vmem_limit_bytes=VMEM_LIMIT),
        name="s5_state",
    )(uj, jnp.asarray(perm, BF16), web)


S5_SCAN_COLS = 8


def _s5_scan_kernel(s_ref, a_ref, h0_ref, hin_ref, fin_ref, hf, hb):
    ncol = S5_SCAN_COLS

    def scan(row0, nc, nb, h0):
        is_f = lax.broadcasted_iota(jnp.int32, (nb, 128), 1) < S5_STATE
        chunk_rows = lambda c: pl.ds(pl.multiple_of(row0 + c * nb, 8), nb)

        def body(c, hs):
            rf = chunk_rows(c)
            rb = chunk_rows(nc - 1 - c)
            new = []
            for m in range(ncol // 2):
                h_re, h_im = hs[2 * m], hs[2 * m + 1]
                a_re = a_ref[:, (2 * m) * 128:(2 * m + 1) * 128]
                a_im = a_ref[:, (2 * m + 1) * 128:(2 * m + 2) * 128]
                loc = []
                for k, h in ((2 * m, h_re), (2 * m + 1, h_im)):
                    hf[k, rf, :] = h
                    hb[k, rb, :] = h
                    loc.append(jnp.where(is_f, s_ref[k, rf, :], s_ref[k, rb, :]))
                new.append(a_re * h_re - a_im * h_im + loc[0])
                new.append(a_re * h_im + a_im * h_re + loc[1])
            return tuple(new)

        return lax.fori_loop(0, nc, body, h0)

    fin = scan(0, SEQ // S5_CHUNK, BATCH, tuple(jnp.zeros((BATCH, 128), F32) for _ in range(ncol)))
    for k in range(ncol):
        fin_ref[:, k * 128:(k + 1) * 128] = fin[k]
    scan(S5_ROWS_C, DEC_SEQ // S5_CHUNK, DEC_BATCH,
         tuple(h0_ref[:, k * 128:(k + 1) * 128] for k in range(ncol)))
    fwd = lax.broadcasted_iota(jnp.int32, (ncol, S5_ROWS, 128), 2) < S5_STATE
    hin_ref[...] = jnp.where(fwd, hf[...], hb[...])


def _s5_scan(sloc, a16, h0l):
    ncol = S5_SCAN_COLS
    w = ncol * 128
    return pl.pallas_call(
        _s5_scan_kernel,
        grid=(S5_STATE_COLS // ncol,),
        in_specs=[
            pl.BlockSpec((ncol, S5_ROWS, 128), lambda k: (k, 0, 0)),
            pl.BlockSpec((1, w), lambda k: (0, k)),
            pl.BlockSpec((DEC_BATCH, w), lambda k: (0, k)),
        ],
        out_specs=[
            pl.BlockSpec((ncol, S5_ROWS, 128), lambda k: (k, 0, 0)),
            pl.BlockSpec((BATCH, w), lambda k: (0, k)),
        ],
        out_shape=[
            jax.ShapeDtypeStruct((S5_STATE_COLS, S5_ROWS, 128), F32),
            jax.ShapeDtypeStruct((BATCH, S5_STATE_COLS * 128), F32),
        ],
        scratch_shapes=[pltpu.VMEM((ncol, S5_ROWS, 128), F32)] * 2,
        name="s5_scan",
    )(sloc, a16, h0l)


def _s5_out_kernel(u_ref, hin_ref, perm_t_ref, wt_ref, wca_ref, d_ref, y_ref):
    u = u_ref[...]
    hin = jnp.concatenate([hin_ref[k] for k in range(S5_SLAB_COLS)], axis=1).astype(BF16)
    hin = _dot(perm_t_ref[...], hin).astype(BF16)
    y = _dot(u.astype(BF16), wt_ref[...]) + _dot(hin, wca_ref[...]) + u * d_ref[...]
    for t in range(S5_CHUNK):
        y_ref[pl.ds(t, S5_ROW_TILE, stride=S5_CHUNK), :] = y[:, t * 128:(t + 1) * 128]


def _s5_out(uj, hin, wt, wca, dj):
    _, perm_t = _s5_row_perms()
    return pl.pallas_call(
        _s5_out_kernel,
        grid=(S5_SLABS, S5_ROWS // S5_ROW_TILE),
        in_specs=[
            pl.BlockSpec((None, S5_ROW_TILE, S5_SLAB_W), lambda j, p: (j, p, 0)),
            pl.BlockSpec((S5_SLAB_COLS, S5_ROW_TILE, 128), lambda j, p: (j, p, 0)),
            pl.BlockSpec((None, S5_ROW_TILE, S5_ROW_TILE), lambda j, p: (p, 0, 0)),
            pl.BlockSpec((None, S5_SLAB_W, S5_SLAB_W), lambda j, p: (j, 0, 0)),
            pl.BlockSpec((None, S5_SLAB_W, S5_SLAB_W), lambda j, p: (j, 0, 0)),
            pl.BlockSpec((None, 1, S5_SLAB_W), lambda j, p: (j, 0, 0)),
        ],
        out_specs=pl.BlockSpec((None, S5_ROW_TILE * S5_CHUNK, 128), lambda j, p: (j, p, 0)),
        out_shape=jax.ShapeDtypeStruct((S5_SLABS, NTOK, 128), F32),
        compiler_params=pltpu.CompilerParams(vmem_limit_bytes=VMEM_LIMIT),
        name="s5_out",
    )(uj, hin, jnp.asarray(perm_t, BF16), wt, wca, dj)


@functools.lru_cache(maxsize=None)
def _gla_consts():
    n = GLA_BLK
    nl = GLA_LEVELS
    r = np.arange(n)
    up = np.zeros((n, 128), np.int32)
    for l in range(nl):
        up[:, l] = (r >> l) & 1
    i = r[:, None]
    j = r[None, :]
    x = np.maximum(i ^ j, 1)
    lev = np.where(j < i, np.floor(np.log2(x)).astype(np.int32), np.where(i == j, nl, -1)).astype(np.int32)
    up2 = np.stack([up, up[::-1]])
    h = n // 2

    def tiled(a):
        return np.stack([np.concatenate([a[:h, :h], a[h:, h:]]), np.concatenate([a[:h, h:], a[h:, :h]])])

    lev2 = np.stack([tiled(lev), tiled(lev[::-1, ::-1])])
    return up2, lev2


@functools.lru_cache(maxsize=None)
def _gla_tables():
    rowblk, seq, first, last = [], [], [], []
    for d in range(2):
        rb, sq, fi, la = [], [], [], []
        for s in range(BATCH + DEC_BATCH):
            nblk = 1 if s < BATCH else DEC_SEQ // GLA_BLK
            base = s if s < BATCH else NTOK_C // GLA_BLK + (s - BATCH) * nblk
            order = range(nblk) if d == 0 else range(nblk - 1, -1, -1)
            for pos, b in enumerate(order):
                rb.append(base + b)
                sq.append(s)
                fi.append(int(pos == 0))
                la.append(int(pos == nblk - 1))
        rowblk.append(rb); seq.append(sq); first.append(fi); last.append(la)
    as_np = lambda a: np.asarray(a, np.int32)
    return as_np(rowblk), as_np(seq), as_np(first), as_np(last)


def _gla_kernel(rowblk_ref, seq_ref, first_ref, last_ref,
                qf_ref, kf_ref, vf_ref, lrf_ref, qb_ref, kb_ref, vb_ref, lrb_ref,
                wgk_ref, bgk_ref, up_ref, lev_ref, s0_ref,
                of_ref, ob_ref, fin_ref, z_scr, st_scr):
    del rowblk_ref, seq_ref
    n = pl.program_id(0)

    @pl.when(first_ref[n] == 1)
    def _():
        st_scr[...] = jnp.zeros_like(st_scr)
        for d in range(2):
            for h in range(GLA_HEADS):
                st_scr[d, h * GLA_DK:(h + 1) * GLA_DK, h * GLA_DV:(h + 1) * GLA_DV] = s0_ref[d, h]

    blocks = [
        _gla_block(False, qf_ref, kf_ref, vf_ref, lrf_ref, wgk_ref.at[0], bgk_ref.at[0], up_ref.at[0],
                   lev_ref.at[0], of_ref, z_scr.at[0], st_scr.at[0]),
        _gla_block(True, qb_ref, kb_ref, vb_ref, lrb_ref, wgk_ref.at[1], bgk_ref.at[1], up_ref.at[1],
                   lev_ref.at[1], ob_ref, z_scr.at[1], st_scr.at[1]),
    ]
    for stage in range(2):
        for block in blocks:
            next(block, None)

    @pl.when(last_ref[n] == 1)
    def _():
        for d in range(2):
            for h in range(GLA_HEADS):
                fin_ref[d, h] = st_scr[d, h * GLA_DK:(h + 1) * GLA_DK, h * GLA_DV:(h + 1) * GLA_DV]


def _gla_block(backward, q_ref, k_ref, v_ref, lr_ref, wgk_ref, bgk_ref, up_ref, lev_ref, o_ref, z_scr, st_scr):
    nl = GLA_LEVELS
    blk = GLA_BLK
    q = q_ref[...] * (GLA_DK ** -0.5)
    k = k_ref[...]
    vb = v_ref[...].astype(BF16)
    x = _dot(lr_ref[...].astype(BF16), wgk_ref[...]) + bgk_ref[...]
    gk = (jnp.minimum(x, 0.0) - jnp.log(1.0 + jnp.exp(-jnp.abs(x)))) * (1.0 / GLA_NORMALIZER)
    g_hi = gk.astype(BF16)
    g_lo = (gk - g_hi.astype(F32)).astype(BF16)
    ones = jnp.ones((blk, 128), BF16)
    tot = _dot_tn(g_hi, ones) + _dot_tn(g_lo, ones)

    row = lax.broadcasted_iota(jnp.int32, (blk, 1), 0)

    def sibling(a, l):
        g = 1 << l
        if g < 8:
            a3 = a.reshape(blk // 8, 8, a.shape[-1])
            dn = pltpu.roll(a3, g, 1).reshape(a.shape)
            up_ = pltpu.roll(a3, 8 - g, 1).reshape(a.shape)
            return jnp.where(((row >> l) & 1) == 1, dn, up_)
        a4 = a.reshape(blk // (2 * g), 2, g, a.shape[-1])
        return jnp.concatenate([a4[:, 1:2], a4[:, 0:1]], axis=1).reshape(a.shape)

    part = gk
    total = gk
    z0 = None
    for l in range(nl):
        g = 1 << l
        if g < 8:
            up = up_ref[:, l:l + 1] != 0
            z = jnp.where(up, q, k) * jnp.exp(jnp.where(up, part, total - part))
            other = sibling(total, l)
            part = part + jnp.where(up, other, 0.0)
            total = total + other
        else:
            halves = lambda a: (a.reshape(blk // (2 * g), 2, g, a.shape[-1])[:, 1 - int(backward)],
                                a.reshape(blk // (2 * g), 2, g, a.shape[-1])[:, int(backward)])
            join = lambda u, d: jnp.stack([d, u] if not backward else [u, d], axis=1).reshape(blk, u.shape[-1])
            part_u, part_d = halves(part)
            tot_u, tot_d = halves(total)
            q_u, _ = halves(q)
            _, k_d = halves(k)
            z = join(q_u * jnp.exp(part_u), k_d * jnp.exp(tot_d - part_d))
            part = join(part_u + tot_d, part_d)
            both = tot_u + tot_d
            total = join(both, both)
        if l == 0:
            z0 = z
        else:
            z_scr[l] = z.astype(BF16)
    yield
    lane128 = lax.broadcasted_iota(jnp.int32, (GLA_QK, 128), 1)
    dim = lax.broadcasted_iota(jnp.int32, (GLA_QK, 128), 0)
    head_sum = ((dim >> 6) == lane128).astype(BF16)
    pair0 = _dot((z0 * sibling(z0, 0)).astype(BF16), head_sum)
    diag = _dot((q * k).astype(BF16), head_sum)

    half = blk // 2
    lev_d = lev_ref[0]
    lev_o = lev_ref[1]
    lane = lax.broadcasted_iota(jnp.int32, (half, GLA_QK), 1)

    def tiles(l, in_head, crossed):
        out = []
        for r in range(2):
            c = 1 - r if crossed else r
            lhs = z_scr[l, r * half:(r + 1) * half, :]
            keys = z_scr[l, c * half:(c + 1) * half, :]
            out.append(_dot_nt(lhs, jnp.where(in_head, keys, jnp.zeros_like(keys))))
        return jnp.concatenate(out, axis=0)

    upi = 0 if backward else 1
    key_lanes = {}
    for l in range(3, nl - 1):
        g = 1 << l
        c = lax.broadcasted_iota(jnp.int32, (blk // (2 * g), g, 128), 0)
        ln = lax.broadcasted_iota(jnp.int32, (blk // (2 * g), g, 128), 2)
        base = (2 * g * c + (g if backward else 0)) & 127
        key_lanes[l] = (ln >= base) & (ln < base + g)

    heads = range(GLA_HEADS)
    in_head = [(lane >= h * GLA_DK) & (lane < (h + 1) * GLA_DK) for h in heads]
    acc = [jnp.where(lev_d == 0, pair0[:, h:h + 1], jnp.where(lev_d == nl, diag[:, h:h + 1], 0.0)) for h in heads]
    for l in range(1, 3):
        acc = [jnp.where(lev_d == l, tiles(l, in_head[h], False), acc[h]) for h in heads]
    for l in range(3, nl - 1):
        g = 1 << l
        for h in heads:
            acc4 = acc[h].reshape(blk // (2 * g), 2, g, 128)
            s4 = tiles(l, in_head[h], False).reshape(blk // (2 * g), 2, g, 128)
            new_up = jnp.where(key_lanes[l], s4[:, upi], acc4[:, upi])
            pieces = [acc4[:, 0], new_up] if upi == 1 else [new_up, acc4[:, 1]]
            acc[h] = jnp.stack(pieces, axis=1).reshape(blk, 128)
    off = [jnp.where(lev_o == nl - 1, tiles(nl - 1, in_head[h], True), 0.0) for h in heads]
    for h in heads:
        att = jnp.concatenate([jnp.concatenate([acc[h][:half], off[h][:half]], axis=1),
                               jnp.concatenate([off[h][half:], acc[h][half:]], axis=1)], axis=0)
        o_ref[:, h * GLA_DV:(h + 1) * GLA_DV] = _dot(att.astype(BF16), vb[:, h * GLA_DV:(h + 1) * GLA_DV])

    st = st_scr[...]
    q_in = (q * jnp.exp(part)).astype(BF16)
    o_ref[...] += _dot(q_in, st.astype(BF16))
    k_out = (k * jnp.exp(total - part)).astype(BF16)
    kv = _dot_tn(k_out, vb)
    row = lax.broadcasted_iota(jnp.int32, (GLA_QK, GLA_V), 0)
    col = lax.broadcasted_iota(jnp.int32, (GLA_QK, GLA_V), 1)
    same_head = (row >> 6) == (col >> 7)
    decay = jnp.exp(tot)
    decay = jnp.concatenate([decay] * GLA_HEADS, axis=1)
    st_new = decay * st + jnp.where(same_head, kv, 0.0)
    st_scr[...] = st_new


def _gla_mix(bslab, lr, wgk, bgk, s0):
    up, lev = _gla_consts()
    rowblk, seq, first, last = _gla_tables()
    nsteps = rowblk.shape[1]
    nseq = BATCH + DEC_BATCH
    nl = GLA_LEVELS
    whole = lambda shape: pl.BlockSpec(shape, lambda n, rb, sq, fi, la: (0,) * len(shape))

    def token_specs(d):
        return [
            pl.BlockSpec((GLA_BLK, GLA_QK), lambda n, rb, sq, fi, la: (rb[d, n], 0)),
            pl.BlockSpec((GLA_BLK, GLA_QK), lambda n, rb, sq, fi, la: (rb[d, n], 1)),
            pl.BlockSpec((GLA_BLK, GLA_V), lambda n, rb, sq, fi, la: (rb[d, n], 1)),
            pl.BlockSpec((GLA_BLK, 128), lambda n, rb, sq, fi, la: (rb[d, n], 0)),
        ]

    state_spec = pl.BlockSpec((None, 2, GLA_HEADS, GLA_DK, GLA_DV), lambda n, rb, sq, fi, la: (sq[n], 0, 0, 0, 0))
    grid_spec = pltpu.PrefetchScalarGridSpec(
        num_scalar_prefetch=4,
        grid=(nsteps,),
        in_specs=token_specs(0) + token_specs(1) + [
            whole((2, 128, GLA_QK)),
            whole((2, 1, GLA_QK)),
            whole((2, GLA_BLK, 128)),
            whole((2, 2, GLA_BLK, GLA_BLK // 2)),
            state_spec,
        ],
        out_specs=[
            pl.BlockSpec((GLA_BLK, GLA_V), lambda n, rb, sq, fi, la: (rb[0, n], 0)),
            pl.BlockSpec((GLA_BLK, GLA_V), lambda n, rb, sq, fi, la: (rb[1, n], 0)),
            state_spec,
        ],
        scratch_shapes=[
            pltpu.VMEM((2, nl, GLA_BLK, GLA_QK), BF16),
            pltpu.VMEM((2, GLA_QK, GLA_V), F32),
        ],
    )
    return pl.pallas_call(
        _gla_kernel,
        grid_spec=grid_spec,
        out_shape=[
            jax.ShapeDtypeStruct((NTOK, GLA_V), F32),
            jax.ShapeDtypeStruct((NTOK, GLA_V), F32),
            jax.ShapeDtypeStruct((nseq, 2, GLA_HEADS, GLA_DK, GLA_DV), F32),
        ],
        compiler_params=pltpu.CompilerParams(vmem_limit_bytes=VMEM_LIMIT),
        name="gla_mix",
    )(jnp.asarray(rowblk), jnp.asarray(seq[0]), jnp.asarray(first[0]), jnp.asarray(last[0]),
      bslab, bslab, bslab, lr, bslab, bslab, bslab, lr, wgk, bgk, jnp.asarray(up), jnp.asarray(lev), s0)


def _attn_ctx_kernel(sink_ref, q_ref, k_ref, v_ref, o_ref):
    k = k_ref[...]
    v = v_ref[...]
    ks = (k.astype(BF16), pltpu.roll(k, 64, 1).astype(BF16))
    vs = (v.astype(BF16), pltpu.roll(v, 64, 1).astype(BF16))
    lo = lax.broadcasted_iota(jnp.int32, (SEQ, 128), 1) < HEAD_DIM
    units = []
    for t in range(ATT_HEADS // 2):
        qt = q_ref[:, t * 128:(t + 1) * 128] * (HEAD_DIM ** -0.5)
        for p in range(2):
            qm = jnp.where(lo if p == 0 else jnp.logical_not(lo), qt, 0.0).astype(BF16)
            units.append((qm, 0 if p == t // 2 else 1, sink_ref[2 * t + p]))
    scores = [_dot_nt(qm, ks[which]) for qm, which, _ in units]
    maxes = [jnp.maximum(sink, jnp.max(s, axis=-1, keepdims=True)) for s, (_, _, sink) in zip(scores, units)]
    probs = [jnp.exp(s - m) for s, m in zip(scores, maxes)]
    dens = [jnp.exp(sink - m) + jnp.sum(p, axis=-1, keepdims=True)
            for p, m, (_, _, sink) in zip(probs, maxes, units)]
    outs = [_dot(p.astype(BF16), vs[which]) / den for p, den, (_, which, _) in zip(probs, dens, units)]
    for t in range(ATT_HEADS // 2):
        o_ref[:, t * 128:(t + 1) * 128] = jnp.where(lo, outs[2 * t], outs[2 * t + 1]).astype(BF16)


def _attn_ctx(sink, cslab):
    return pl.pallas_call(
        _attn_ctx_kernel,
        grid=(BATCH,),
        in_specs=[
            pl.BlockSpec(memory_space=pltpu.SMEM),
            pl.BlockSpec((SEQ, ATT_Q), lambda b: (b, 0)),
            pl.BlockSpec((SEQ, ATT_KV), lambda b: (b, 4)),
            pl.BlockSpec((SEQ, ATT_KV), lambda b: (b, 5)),
        ],
        out_specs=pl.BlockSpec((SEQ, ATT_Q), lambda b: (b, 0)),
        out_shape=jax.ShapeDtypeStruct((NTOK_C, ATT_Q), BF16),
        name="attn_ctx",
    )(sink, cslab, cslab, cslab)


def _attn_lat_kernel(sink_ref, q_ref, kp_ref, kc_ref, kn_ref, vp_ref, vc_ref, vn_ref,
                     ck_ref, cv_ref, cos_ref, sin_ref, bias_ref, o_ref):
    j = pl.program_id(1)
    nb = DEC_SEQ // ATT_BLOCK
    lane = lax.broadcasted_iota(jnp.int32, (ATT_BLOCK, 128), 1)
    lo = lane < HEAD_DIM
    first16 = (lane & 31) < 16

    def rope(x, blk_idx):
        r0 = pl.multiple_of(blk_idx * ATT_BLOCK, ATT_BLOCK)
        c = cos_ref[pl.ds(r0, ATT_BLOCK), :]
        s = sin_ref[pl.ds(r0, ATT_BLOCK), :]
        xs = jnp.where(first16, pltpu.roll(x, 112, 1), pltpu.roll(x, 16, 1))
        return x * c + xs * s

    nwin = 3 * ATT_BLOCK
    keys = jnp.concatenate([rope(kp_ref[...], jnp.maximum(j - 1, 0)), rope(kc_ref[...], j),
                            rope(kn_ref[...], jnp.minimum(j + 1, nb - 1)), ck_ref[...]], axis=0)
    vals = jnp.concatenate([vp_ref[...], vc_ref[...], vn_ref[...], cv_ref[...]], axis=0)
    keys2 = (keys.astype(BF16), pltpu.roll(keys, 64, 1).astype(BF16))
    vals2 = (vals.astype(BF16), pltpu.roll(vals, 64, 1).astype(BF16))
    kcol = lax.broadcasted_iota(jnp.int32, (1, nwin + PAST_LEN), 1)
    edge = jnp.where(((j == 0) & (kcol < ATT_BLOCK)) | ((j == nb - 1) & (kcol >= 2 * ATT_BLOCK) & (kcol < nwin)),
                     -1e30, 0.0)
    bias = bias_ref[...] + edge
    top = lax.broadcasted_iota(jnp.int32, (2 * ATT_BLOCK, 1), 0) < ATT_BLOCK
    q_tiles = [rope(q_ref[:, t * 128:(t + 1) * 128], j) * (HEAD_DIM ** -0.5) for t in range(ATT_HEADS // 2)]
    lo2 = jnp.concatenate([lo, lo], axis=0)
    units = []
    for kvh in range(ATT_KV_HEADS):
        q2 = jnp.concatenate(q_tiles[2 * kvh:2 * kvh + 2], axis=0)
        for p in range(2):
            qm = jnp.where(lo2 if p == 0 else jnp.logical_not(lo2), q2, 0.0).astype(BF16)
            sink = jnp.where(top, sink_ref[4 * kvh + p], sink_ref[4 * kvh + 2 + p])
            units.append((qm, 0 if p == kvh else 1, sink))
    scores = [_dot_nt(qm, keys2[which]) + bias for qm, which, _ in units]
    maxes = [jnp.maximum(sink, jnp.max(s, axis=-1, keepdims=True)) for s, (_, _, sink) in zip(scores, units)]
    probs = [jnp.exp(s - m) for s, m in zip(scores, maxes)]
    dens = [jnp.exp(sink - m) + jnp.sum(p, axis=-1, keepdims=True)
            for p, m, (_, _, sink) in zip(probs, maxes, units)]
    outs = [_dot(p.astype(BF16), vals2[which]) / den for p, den, (_, which, _) in zip(probs, dens, units)]
    for kvh in range(ATT_KV_HEADS):
        o2 = jnp.where(lo2, outs[2 * kvh], outs[2 * kvh + 1])
        for i in range(2):
            t = 2 * kvh + i
            o_ref[:, t * 128:(t + 1) * 128] = o2[i * ATT_BLOCK:(i + 1) * ATT_BLOCK].astype(BF16)


def _attn_lat(sink, cslab, ck, cv, cos_t, sin_t):
    nb = DEC_SEQ // ATT_BLOCK
    base = NTOK_C // ATT_BLOCK
    cur = lambda b, j: base + b * nb + j
    prv = lambda b, j: base + b * nb + jnp.maximum(j - 1, 0)
    nxt = lambda b, j: base + b * nb + jnp.minimum(j + 1, nb - 1)
    kv_spec = lambda row, col: pl.BlockSpec((ATT_BLOCK, ATT_KV), lambda b, j: (row(b, j), col))
    qi = np.arange(2 * ATT_BLOCK)[:, None] % ATT_BLOCK
    kc = np.arange(3 * ATT_BLOCK + PAST_LEN)[None, :]
    inside = (np.abs(kc - ATT_BLOCK - qi) <= WINDOW) | (kc >= 3 * ATT_BLOCK)
    band = np.where(inside, 0.0, -1e30).astype(np.float32)
    return pl.pallas_call(
        _attn_lat_kernel,
        grid=(DEC_BATCH, nb),
        in_specs=[
            pl.BlockSpec(memory_space=pltpu.SMEM),
            pl.BlockSpec((ATT_BLOCK, ATT_Q), lambda b, j: (cur(b, j), 0)),
            kv_spec(prv, 4), kv_spec(cur, 4), kv_spec(nxt, 4),
            kv_spec(prv, 5), kv_spec(cur, 5), kv_spec(nxt, 5),
            pl.BlockSpec((None, PAST_LEN, ATT_KV), lambda b, j: (b, 0, 0)),
            pl.BlockSpec((None, PAST_LEN, ATT_KV), lambda b, j: (b, 0, 0)),
            pl.BlockSpec((DEC_SEQ, 128), lambda b, j: (0, 0)),
            pl.BlockSpec((DEC_SEQ, 128), lambda b, j: (0, 0)),
            pl.BlockSpec(band.shape, lambda b, j: (0, 0)),
        ],
        out_specs=pl.BlockSpec((ATT_BLOCK, ATT_Q), lambda b, j: (b * nb + j, 0)),
        out_shape=jax.ShapeDtypeStruct((NTOK_L, ATT_Q), BF16),
        name="attn_lat",
    )(sink, cslab, cslab, cslab, cslab, cslab, cslab, cslab, ck, cv, cos_t, sin_t, jnp.asarray(band))


def _rope_tables():
    rows = DEC_SEQ // GRID_W
    row = np.repeat(np.arange(rows, dtype=np.float32), GRID_W)
    col = np.tile(np.arange(GRID_W, dtype=np.float32), rows)
    quarter = HEAD_DIM // 4
    inv = jnp.asarray(ROPE_BASE, F32) ** (-jnp.arange(quarter, dtype=F32) / quarter)
    lane = np.arange(128)
    use_row = (lane % HEAD_DIM) < HEAD_DIM // 2
    pos = jnp.where(use_row[None, :], jnp.asarray(row)[:, None], jnp.asarray(col)[:, None])
    ang = pos * inv[lane % quarter][None, :]
    sign = np.where((lane % 32) < 16, -1.0, 1.0).astype(np.float32)
    return jnp.cos(ang), jnp.sin(ang) * sign[None, :]


def _merge_kernel(*refs, split_x):
    if split_x:
        xc_ref, xl_ref, *refs = refs
    else:
        xc_ref, *refs = refs
    (mod_ref, g_ref, ys5_ref, ogf_ref, ogb_ref, gb_ref, ycc_ref, ycl_ref, gate_ref, gng_ref,
     wglu_ref, wbr_ref, wout_ref, o_ref) = refs
    is_ctx = pl.program_id(0) < NTOK_C // TM
    if split_x:
        x = jnp.where(is_ctx, xc_ref[...], xl_ref[...])
    else:
        x = xc_ref[...]
    y = jnp.concatenate([ys5_ref[j] for j in range(S5_SLABS)], axis=1)
    y = 0.5 * y * (1.0 + jnp.tanh(math.sqrt(2.0 / math.pi) * (y + 0.044715 * (y * y * y))))
    ag = _dot(y.astype(BF16), wglu_ref[...])
    y_a = ag[:, :S5_WIDTH] * _sigmoid(ag[:, S5_WIDTH:])
    gng = gng_ref[...]
    parts = []
    for h in range(GLA_HEADS):
        sl = slice(h * GLA_DV, (h + 1) * GLA_DV)
        o = ogf_ref[:, sl] + ogb_ref[:, sl]
        g = gb_ref[:, sl]
        parts.append(_rms(o, gng) * (g * _sigmoid(g)))
    y_b = jnp.concatenate(parts, axis=1)
    y_c = jnp.where(is_ctx, ycc_ref[...], ycl_ref[...])
    merged = None
    for n, yn in enumerate((y_a, y_b, y_c)):
        proj = _dot(yn.astype(BF16), wbr_ref[n])
        term = gate_ref[:, n * D_MODEL:(n + 1) * D_MODEL].astype(F32) * proj
        merged = term if merged is None else merged + term
    mixed = _dot(merged.astype(BF16), wout_ref[...])
    g1 = mod_ref[:, 2 * D_MODEL:3 * D_MODEL]
    o_ref[...] = x + g1 * _rms(mixed, g_ref[...])


def _layer_spec(shape, layer):
    return pl.BlockSpec((None,) + shape, lambda i: (layer,) + (0,) * len(shape), pipeline_mode=pl.Buffered(1))


def _split_token_specs(n_arrays, width=D_MODEL):
    nct = NTOK_C // TM
    if n_arrays == 2:
        return [pl.BlockSpec((TM, width), lambda i: (jnp.minimum(i, nct - 1), 0)),
                pl.BlockSpec((TM, width), lambda i: (jnp.maximum(i - nct, 0), 0))]
    return [pl.BlockSpec((TM, width), lambda i: (i, 0))]


def _merge(xs, mod, g, ys5, og, bslab, yc, gates, gng, wglu, wbr, wout, layer):
    tok = lambda width, col=0: pl.BlockSpec((TM, width), lambda i: (i, col))
    full = lambda shape: _layer_spec(shape, layer)
    return pl.pallas_call(
        functools.partial(_merge_kernel, split_x=len(xs) == 2),
        grid=(NTOK // TM,),
        in_specs=_split_token_specs(len(xs)) + [
            pl.BlockSpec((None, 1, 6 * D_MODEL), lambda i: (_mod_row(i), 0, 0)),
            pl.BlockSpec((1, D_MODEL), lambda i: (0, 0)),
            pl.BlockSpec((S5_SLABS, TM, 128), lambda i: (0, _s5_tile(i), 0)),
            tok(GLA_V),
            tok(GLA_V),
            tok(GLA_V, 2),
        ] + _split_token_specs(2, ATT_Q) + [
            tok(N_BRANCH * D_MODEL),
            pl.BlockSpec((1, GLA_DV), lambda i: (0, 0)),
            full((S5_WIDTH, 2 * S5_WIDTH)),
            full((N_BRANCH, BRANCH_W, D_MODEL)),
            full((D_MODEL, D_MODEL)),
        ],
        out_specs=tok(D_MODEL),
        out_shape=jax.ShapeDtypeStruct((NTOK, D_MODEL), F32),
        compiler_params=pltpu.CompilerParams(vmem_limit_bytes=VMEM_LIMIT),
        name="merge",
    )(*xs, mod, g, ys5, *og, bslab, *yc, gates, gng, wglu, wbr, wout)


FFN_SPLIT = 2


def _ffn_kernel(x_ref, mod_ref, gin_ref, gout_ref, w1_ref, w2_ref, *o_refs):
    x = x_ref[...]
    sh = mod_ref[:, 3 * D_MODEL:4 * D_MODEL]
    sc = mod_ref[:, 4 * D_MODEL:5 * D_MODEL]
    g2 = mod_ref[:, 5 * D_MODEL:6 * D_MODEL]
    h = (_rms(x, gin_ref[...]) * (1.0 + sc) + sh).astype(BF16)
    ck = FFN_HIDDEN // FFN_SPLIT
    acc = None
    for c in range(FFN_SPLIT):
        a = _dot(h, w1_ref[:, c * ck:(c + 1) * ck])
        b = _dot(h, w1_ref[:, FFN_HIDDEN + c * ck:FFN_HIDDEN + (c + 1) * ck])
        act = (a * _sigmoid(a) * b).astype(BF16)
        part = _dot(act, w2_ref[c * ck:(c + 1) * ck, :])
        acc = part if acc is None else acc + part
    y = x + g2 * _rms(acc, gout_ref[...])
    if len(o_refs) == 1:
        o_refs[0][...] = y
    else:
        is_ctx = pl.program_id(0) < NTOK_C // TM

        @pl.when(is_ctx)
        def _():
            o_refs[0][...] = y

        @pl.when(jnp.logical_not(is_ctx))
        def _():
            o_refs[1][...] = y


def _ffn(x, mod, gin, gout, w1, w2, layer, split_out):
    small = lambda shape: pl.BlockSpec(shape, lambda i: (0,) * len(shape))
    nct = NTOK_C // TM
    if split_out:
        out_specs = [pl.BlockSpec((TM, D_MODEL), lambda i: (jnp.minimum(i, nct - 1), 0)),
                     pl.BlockSpec((TM, D_MODEL), lambda i: (jnp.maximum(i - nct, 0), 0))]
        out_shape = [jax.ShapeDtypeStruct((NTOK_C, D_MODEL), F32), jax.ShapeDtypeStruct((NTOK_L, D_MODEL), F32)]
    else:
        out_specs = pl.BlockSpec((TM, D_MODEL), lambda i: (i, 0))
        out_shape = jax.ShapeDtypeStruct((NTOK, D_MODEL), F32)
    return pl.pallas_call(
        _ffn_kernel,
        grid=(NTOK // TM,),
        in_specs=[
            pl.BlockSpec((TM, D_MODEL), lambda i: (i, 0)),
            pl.BlockSpec((None, 1, 6 * D_MODEL), lambda i: (_mod_row(i), 0, 0)),
            small((1, D_MODEL)),
            small((1, D_MODEL)),
            _layer_spec((D_MODEL, 2 * FFN_HIDDEN), layer),
            _layer_spec((FFN_HIDDEN, D_MODEL), layer),
        ],
        out_specs=out_specs,
        out_shape=out_shape,
        compiler_params=pltpu.CompilerParams(vmem_limit_bytes=VMEM_LIMIT),
        name="ffn",
    )(x, mod, gin, gout, w1, w2)


def kernel(x_prompt, x_sample, cache_k, cache_v, state_s5, state_gla, c, c_ctx, w_mod, b_mod, norm_g, w_in,
           s5_lam_re, s5_lam_im, s5_log_step, s5_b_re, s5_b_im, s5_c_re, s5_c_im, s5_d, w_glu, gla_w_gk,
           gla_b_gk, gla_norm_g, att_sink, w_branch, w_out, w_ffn_in, w_ffn_out):
    cond = jnp.concatenate([c_ctx[None, :], c, jnp.zeros((N_MOD_ROWS - 1 - DEC_BATCH, D_MODEL), F32)], axis=0)
    mod_all = _modulation(cond, w_mod, b_mod).reshape(DEPTH, N_MOD_ROWS, 1, 6 * D_MODEL)
    cos_t, sin_t = _rope_tables()
    xs = (x_prompt.reshape(NTOK_C, D_MODEL), x_sample.reshape(NTOK_L, D_MODEL))
    w_in_b = w_in.astype(BF16)
    w_in_end = jnp.pad(w_in[:, :, D_IN_TILED:].astype(BF16), ((0, 0), (0, 0), (0, W_IN_COLS - D_IN)))
    w_glu_b, w_branch_b, w_out_b = w_glu.astype(BF16), w_branch.astype(BF16), w_out.astype(BF16)
    w_ffn_in_b, w_ffn_out_b = w_ffn_in.astype(BF16), w_ffn_out.astype(BF16)
    new_k, new_v, new_s5, new_gla = [], [], [], []
    for i in range(DEPTH):
        mod = mod_all[i]
        uj, bslab, cslab, gates, lr = _inproj(xs, mod, norm_g[i, 0][None, :], w_in_b, w_in_end, i)

        wt, web, wca, a16, dj = _s5_prep(s5_lam_re[i], s5_lam_im[i], s5_log_step[i], s5_b_re[i], s5_b_im[i],
                                         s5_c_re[i], s5_c_im[i], s5_d[i])
        h0l = state_s5[:, i].astype(F32).transpose(0, 2, 4, 1, 3).reshape(DEC_BATCH, S5_GROUPS * 256)
        hin, finc = _s5_scan(_s5_state(uj, web), a16, h0l)
        ys5 = _s5_out(uj, hin, wt, wca, dj)
        new_s5.append(finc.reshape(BATCH, S5_GROUPS, 2, 2, S5_STATE).transpose(0, 3, 1, 4, 2))

        wgk = jnp.zeros((2, 128, GLA_QK), F32)
        wgk = wgk.at[0, 0:GLA_RANK].set(gla_w_gk[i, 0]).at[1, GLA_RANK:2 * GLA_RANK].set(gla_w_gk[i, 1])
        s0 = jnp.concatenate([jnp.zeros((BATCH, 2, GLA_HEADS, GLA_DK, GLA_DV), F32),
                              state_gla[:, i].astype(F32)], axis=0)
        *og, gla_fin = _gla_mix(bslab, lr, wgk.astype(BF16), gla_b_gk[i][:, None, :].astype(F32), s0)
        new_gla.append(gla_fin[:BATCH])

        sink = att_sink[i].astype(F32)
        yc = (_attn_ctx(sink, cslab),
              _attn_lat(sink, cslab, cache_k[:, i].reshape(DEC_BATCH, PAST_LEN, ATT_KV).astype(F32),
                        cache_v[:, i].reshape(DEC_BATCH, PAST_LEN, ATT_KV).astype(F32), cos_t, sin_t))
        new_k.append(cslab[:NTOK_C, ATT_Q:ATT_Q + ATT_KV].reshape(BATCH, SEQ, ATT_KV_HEADS, HEAD_DIM))
        new_v.append(cslab[:NTOK_C, ATT_Q + ATT_KV:].reshape(BATCH, SEQ, ATT_KV_HEADS, HEAD_DIM))

        x = _merge(xs, mod, norm_g[i, 1][None, :], ys5, og, bslab, yc, gates, gla_norm_g[i][None, :],
                   w_glu_b, w_branch_b, w_out_b, i)
        last = i == DEPTH - 1
        x = _ffn(x, mod, norm_g[i, 2][None, :], norm_g[i, 3][None, :], w_ffn_in_b, w_ffn_out_b, i, last)
        xs = tuple(x) if last else (x,)

    return (xs[0].reshape(BATCH, SEQ, D_MODEL), xs[1].reshape(DEC_BATCH, DEC_SEQ, D_MODEL),
            jnp.stack(new_k, axis=1), jnp.stack(new_v, axis=1),
            jnp.stack(new_s5, axis=1), jnp.stack(new_gla, axis=1))
```

```python
import functools
import math

import numpy as np
import jax
import jax.numpy as jnp
from jax import lax
from jax.experimental import pallas as pl
from jax.experimental.pallas import tpu as pltpu

F32 = jnp.float32
BF16 = jnp.bfloat16

D_MODEL = 1024
BATCH = 16
SEQ = 256
DEPTH = 2
DEC_BATCH = 8
DEC_SEQ = 1024
PAST_LEN = 256
GRID_W = 64
ROPE_BASE = 10000.0
S5_WIDTH = 512
S5_GROUP = 16
S5_GROUPS = 32
S5_STATE = 64
GLA_HEADS = 4
GLA_DK = 64
GLA_DV = 128
GLA_QK = 256
GLA_V = 512
GLA_RANK = 16
GLA_NORMALIZER = 16.0
ATT_HEADS = 8
ATT_KV_HEADS = 2
HEAD_DIM = 64
ATT_Q = 512
ATT_KV = 128
WINDOW = 128
ATT_BLOCK = 128
N_BRANCH = 3
BRANCH_W = 512
FFN_HIDDEN = 2816
RMS_EPS = 1e-6

NTOK_C = BATCH * SEQ
NTOK_L = DEC_BATCH * DEC_SEQ
NTOK = NTOK_C + NTOK_L
TM = 512
N_MOD_ROWS = 16

D_IN = 5920
D_IN_TILED = D_IN // 128 * 128
W_IN_COLS = 6016
S5_CHUNK = 16
S5_SLABS = S5_WIDTH // 128
S5_SLAB_W = S5_CHUNK * 128
S5_ROWS_C = NTOK_C // S5_CHUNK
S5_ROWS = NTOK // S5_CHUNK
S5_ROW_TILE = 256
GLA_BLK = 256
GLA_LEVELS = 8
VMEM_LIMIT = 56 * 1024 * 1024


def _dot(a, b):
    return jnp.dot(a, b, preferred_element_type=F32)


def _dot_nt(a, b):
    return lax.dot_general(a, b, (((1,), (1,)), ((), ())), preferred_element_type=F32)


def _dot_tn(a, b):
    return lax.dot_general(a, b, (((0,), (0,)), ((), ())), preferred_element_type=F32)


def _rms(x, g):
    return x * lax.rsqrt(jnp.mean(x * x, axis=-1, keepdims=True) + RMS_EPS) * g


def _sigmoid(x):
    return 0.5 * jnp.tanh(0.5 * x) + 0.5


def _mod_row(i):
    nct = NTOK_C // TM
    return jnp.where(i < nct, 0, 1 + (i - nct) // (DEC_SEQ // TM))


def _s5_tile(i):
    nct = NTOK_C // TM
    per_seq = DEC_SEQ // TM
    k = i - nct
    return jnp.where(i < nct, i, nct + (k % per_seq) * DEC_BATCH + k // per_seq)


def _mod_kernel(c_ref, w_ref, b_ref, o_ref):
    c = c_ref[...]
    s = (c * _sigmoid(c)).astype(BF16)
    o_ref[...] = _dot(s, w_ref[...].astype(BF16)) + b_ref[...]


def _modulation(cond, w_mod, b_mod):
    tn = 2048
    return pl.pallas_call(
        _mod_kernel,
        grid=(DEPTH, 6 * D_MODEL // tn),
        in_specs=[
            pl.BlockSpec((N_MOD_ROWS, D_MODEL), lambda l, n: (0, 0)),
            pl.BlockSpec((None, D_MODEL, tn), lambda l, n: (l, 0, n)),
            pl.BlockSpec((None, 1, tn), lambda l, n: (l, 0, n)),
        ],
        out_specs=pl.BlockSpec((None, N_MOD_ROWS, tn), lambda l, n: (l, 0, n)),
        out_shape=jax.ShapeDtypeStruct((DEPTH, N_MOD_ROWS, 6 * D_MODEL), F32),
        name="modulation",
    )(cond, w_mod, b_mod.reshape(DEPTH, 1, 6 * D_MODEL))


_IN_SLABS = ((0, 512), (512, 1536), (2048, 768), (2816, 3072), (5888, 128))
W_IN_SPLIT = 2048
W_IN_GAP = 32
W_IN_TAIL = W_IN_COLS - W_IN_SPLIT


def _inproj_kernel(*refs, split_x):
    if split_x:
        xc_ref, xl_ref, *refs = refs
    else:
        xc_ref, *refs = refs
    mod_ref, g_ref, w_ref, w_end_ref, u_ref, b_ref, c_ref, gate_ref, lr_ref, w_tail, u_stage = refs
    i = pl.program_id(0)

    @pl.when(i == 0)
    def _():
        r = lax.broadcasted_iota(jnp.int32, (256, 128), 0)
        c = lax.broadcasted_iota(jnp.int32, (256, 128), 1)
        shift = (r == c + W_IN_GAP).astype(BF16)
        head = ((r == c) & (c < W_IN_GAP)).astype(BF16)
        ntile = (W_IN_TAIL - 128) // 128
        for t in range(ntile - 1):
            src = W_IN_SPLIT + 128 * t
            w_tail[:, 128 * t:128 * (t + 1)] = _dot(w_ref[:, src:src + 256], shift).astype(BF16)
        src = W_IN_SPLIT + 128 * (ntile - 1)
        last = jnp.concatenate([w_ref[:, src:src + 128], w_end_ref[...]], axis=1)
        w_tail[:, 128 * (ntile - 1):128 * ntile] = _dot(last, shift).astype(BF16)
        w_tail[:, 128 * ntile:] = _dot(w_ref[:, W_IN_SPLIT:W_IN_SPLIT + 256], head).astype(BF16)

    if split_x:
        x = jnp.where(i < NTOK_C // TM, xc_ref[...], xl_ref[...])
    else:
        x = xc_ref[...]
    mod = mod_ref[...]
    h = _rms(x, g_ref[...]) * (1.0 + mod[:, D_MODEL:2 * D_MODEL]) + mod[:, 0:D_MODEL]
    h = h.astype(BF16)
    for j in range(S5_SLABS):
        u_stage[...] = _dot(h, w_ref[:, j * 128:(j + 1) * 128])
        for s in range(S5_CHUNK):
            u_ref[j, :, s * 128:(s + 1) * 128] = u_stage[pl.ds(s, TM // S5_CHUNK, stride=S5_CHUNK), :]
    b_ref[...] = _dot(h, w_ref[:, 512:W_IN_SPLIT])
    for (off, width), o_ref in zip(_IN_SLABS[2:], (c_ref, gate_ref, lr_ref)):
        z = _dot(h, w_tail[:, off - W_IN_SPLIT:off - W_IN_SPLIT + width])
        o_ref[...] = _sigmoid(z).astype(BF16) if o_ref is gate_ref else z


def _inproj(xs, mod, g, w_all, w_end, layer):
    return pl.pallas_call(
        functools.partial(_inproj_kernel, split_x=len(xs) == 2),
        grid=(NTOK // TM,),
        in_specs=_split_token_specs(len(xs)) + [
            pl.BlockSpec((None, 1, 6 * D_MODEL), lambda i: (_mod_row(i), 0, 0)),
            pl.BlockSpec((1, D_MODEL), lambda i: (0, 0)),
            pl.BlockSpec((None, D_MODEL, D_IN), lambda i: (layer, 0, 0), pipeline_mode=pl.Buffered(1)),
            pl.BlockSpec((None, D_MODEL, 128), lambda i: (layer, 0, 0), pipeline_mode=pl.Buffered(1)),
        ],
        out_specs=[pl.BlockSpec((S5_SLABS, TM // S5_CHUNK, S5_SLAB_W), lambda i: (0, _s5_tile(i), 0))]
        + [pl.BlockSpec((TM, width), lambda i: (i, 0)) for _, width in _IN_SLABS[1:]],
        out_shape=[jax.ShapeDtypeStruct((S5_SLABS, S5_ROWS, S5_SLAB_W), F32)]
        + [jax.ShapeDtypeStruct((NTOK, width), BF16 if width == N_BRANCH * D_MODEL else F32)
           for _, width in _IN_SLABS[1:]],
        scratch_shapes=[pltpu.VMEM((D_MODEL, W_IN_TAIL), BF16), pltpu.VMEM((TM, 128), F32)],
        compiler_params=pltpu.CompilerParams(vmem_limit_bytes=VMEM_LIMIT),
        name="inproj",
    )(*xs, mod, g, w_all, w_end)


@functools.lru_cache(maxsize=None)
def _s5_expanders():
    seg = 8
    spread = np.zeros((seg, 256, S5_SLAB_W), np.float32)
    place = np.zeros((seg, 256, S5_SLAB_W), np.float32)
    col = np.arange(256)
    for gl in range(seg):
        spread[gl, col, (col // S5_GROUP) * 128 + gl * S5_GROUP + col % S5_GROUP] = 1.0
        place[gl, col, gl * 256 + col] = 1.0
    return spread, place


def _s5_prep_kernel(par_ref, bre_ref, bim_ref, cre_ref, cim_ref, spread_ref, place_ref,
                    wt_ref, web_ref, wca_ref, a16_ref):
    n = S5_CHUNK
    lam_re = par_ref[0:1, :]
    lam_im = par_ref[1:2, :]
    dt = jnp.exp(par_ref[2:3, :])
    lr = lam_re * dt
    li = lam_im * dt
    krow = lax.broadcasted_iota(jnp.int32, (24, 128), 0).astype(F32)
    tab_mag = jnp.exp(krow * lr)
    tab_re = tab_mag * jnp.cos(krow * li)
    tab_im = tab_mag * jnp.sin(krow * li)
    ar = tab_re[1:2, :]
    ai = tab_im[1:2, :]
    nr = ar - 1.0
    den = lam_re * lam_re + lam_im * lam_im
    fr = (nr * lam_re + ai * lam_im) / den
    fi = (ai * lam_re - nr * lam_im) / den
    b_re = bre_ref[...]
    b_im = bim_ref[...]
    br = fr * b_re - fi * b_im
    bi = fr * b_im + fi * b_re
    c_re = cre_ref[...]
    c_im = cim_ref[...]

    def lo_half(shape):
        return lax.broadcasted_iota(jnp.int32, shape, 1) < S5_STATE

    def tile_rows(a):
        return jnp.concatenate([a] * n, axis=0)

    fwd16 = lo_half((S5_GROUP, 128))

    def powers(t_re, t_im, k_fwd, k_bwd):
        def pick(t, b):
            kf, kb = k_fwd(b), k_bwd(b)
            return jnp.where(fwd16, jnp.broadcast_to(t[kf:kf + 1, :], (S5_GROUP, 128)),
                             jnp.broadcast_to(t[kb:kb + 1, :], (S5_GROUP, 128)))
        return (jnp.concatenate([pick(t_re, b) for b in range(n)], axis=0),
                jnp.concatenate([pick(t_im, b) for b in range(n)], axis=0))

    fwd = lo_half((n * S5_GROUP, 128))
    brt, bit, crt, cit = tile_rows(br), tile_rows(bi), tile_rows(c_re), tile_rows(c_im)

    per, pei = powers(tab_re, tab_im, lambda s: n - 1 - s, lambda s: s)
    eb = jnp.concatenate([brt * per - bit * pei, brt * pei + bit * per], axis=1)
    pcr, pci = powers(tab_re, tab_im, lambda t: t + 1, lambda t: n - t)
    ca = jnp.concatenate([(crt * pcr - cit * pci).T, (-(crt * pci + cit * pcr)).T], axis=0)

    def one_dir(x, d):
        sw = pltpu.roll(x, S5_STATE, 1)
        lo = lo_half(x.shape)
        return jnp.where(lo, x, sw) if d == 0 else jnp.where(lo, sw, x)

    klag = []
    for d in range(2):
        lhs = jnp.where(lo_half(br.shape), one_dir(br, d), -one_dir(bi, d))
        crd, cid = tile_rows(one_dir(c_re, d)), tile_rows(one_dir(c_im, d))
        lag = (lambda b: b) if d == 0 else (lambda b: n - 1 - b)
        pr, pi = powers(one_dir(tab_re, d), one_dir(tab_im, d), lag, lag)
        rhs_t = jnp.where(fwd, crd * pr - cid * pi, crd * pi + cid * pr)
        klag.append(lax.dot_general(lhs, rhs_t, (((1,), (1,)), ((), ())),
                                    precision=lax.Precision.HIGHEST, preferred_element_type=F32))
    lane = lax.broadcasted_iota(jnp.int32, (S5_GROUP, n * S5_GROUP), 1)
    rows = []
    for s in range(n):
        f = klag[0] if s == 0 else jnp.where(lane >= S5_GROUP * s, pltpu.roll(klag[0], S5_GROUP * s, 1), 0.0)
        sh = (n * S5_GROUP - S5_GROUP * (n - 1 - s)) % (n * S5_GROUP)
        b = klag[1] if sh == 0 else pltpu.roll(klag[1], sh, 1)
        rows.append(f + jnp.where(lane < S5_GROUP * (s + 1), b, 0.0))
    toep = jnp.concatenate(rows, axis=0)

    spread = spread_ref[...]
    wt_ref[...] = _dot(toep.astype(BF16), spread).astype(BF16).reshape(n, S5_GROUP, S5_SLAB_W)
    web_ref[...] = _dot(eb.astype(BF16), place_ref[...]).astype(BF16).reshape(n, S5_GROUP, S5_SLAB_W)
    wca_ref[...] = _dot(ca.astype(BF16), spread).astype(BF16)
    a16_ref[0:1, :] = tab_re[n:n + 1, :]
    a16_ref[1:2, :] = tab_im[n:n + 1, :]


def _s5_params(lam_re, lam_im, log_step, b_re, b_im, c_re, c_im):
    par = jnp.stack([lam_re, lam_im, log_step], axis=1).astype(F32)
    par = par.transpose(0, 3, 1, 2, 4).reshape(DEPTH, S5_GROUPS, 3, 128)
    par = jnp.pad(par, ((0, 0), (0, 0), (0, 5), (0, 0)))
    b_t = lambda b: b.astype(F32).transpose(0, 2, 4, 1, 3).reshape(DEPTH, S5_GROUPS, S5_GROUP, 128)
    c_t = lambda c: c.astype(F32).transpose(0, 2, 3, 1, 4).reshape(DEPTH, S5_GROUPS, S5_GROUP, 128)
    return par, b_t(b_re), b_t(b_im), c_t(c_re), c_t(c_im)


def _s5_prep(params, d_skip, layer):
    seg = 8
    spread, place = _s5_expanders()
    vec = pl.BlockSpec((None, None, S5_GROUP, 128), lambda gl, j: (layer, j * seg + gl, 0, 0))
    exp_spec = pl.BlockSpec((None, 256, S5_SLAB_W), lambda gl, j: (gl, 0, 0))
    rows_spec = pl.BlockSpec((None, S5_CHUNK, None, S5_GROUP, S5_SLAB_W), lambda gl, j: (j, 0, gl, 0, 0))
    wt, web, wca, a16 = pl.pallas_call(
        _s5_prep_kernel,
        grid=(seg, S5_SLABS),
        in_specs=[pl.BlockSpec((None, None, 8, 128), lambda gl, j: (layer, j * seg + gl, 0, 0)),
                  vec, vec, vec, vec, exp_spec, exp_spec],
        out_specs=[
            rows_spec, rows_spec,
            pl.BlockSpec((None, None, 256, S5_SLAB_W), lambda gl, j: (j, gl, 0, 0)),
            pl.BlockSpec((None, 2, 128), lambda gl, j: (j * seg + gl, 0, 0)),
        ],
        out_shape=[
            jax.ShapeDtypeStruct((S5_SLABS, S5_CHUNK, seg, S5_GROUP, S5_SLAB_W), BF16),
            jax.ShapeDtypeStruct((S5_SLABS, S5_CHUNK, seg, S5_GROUP, S5_SLAB_W), BF16),
            jax.ShapeDtypeStruct((S5_SLABS, seg, 256, S5_SLAB_W), BF16),
            jax.ShapeDtypeStruct((S5_GROUPS, 2, 128), F32),
        ],
        name="s5_prep",
    )(*params, jnp.asarray(spread, BF16), jnp.asarray(place, BF16))
    mat = (S5_SLABS, S5_SLAB_W, S5_SLAB_W)
    dj = jnp.tile(d_skip.astype(F32).reshape(S5_SLABS, 1, 128), (1, 1, S5_CHUNK))
    return wt.reshape(mat), web.reshape(mat), wca.reshape(mat), a16.reshape(1, S5_SLABS * S5_SLAB_W), dj


S5_STATE_COLS = S5_SLABS * S5_SLAB_W // 128
S5_SLAB_COLS = S5_SLAB_W // 128


@functools.lru_cache(maxsize=None)
def _s5_row_perms():
    assert S5_ROW_TILE == S5_ROWS_C == DEC_BATCH * TM // S5_CHUNK
    perm = np.zeros((3, S5_ROW_TILE, S5_ROW_TILE), np.float32)
    for p, (nseq, nchunk) in enumerate(((BATCH, SEQ // S5_CHUNK), (DEC_BATCH, TM // S5_CHUNK),
                                        (DEC_BATCH, TM // S5_CHUNK))):
        b, c = np.meshgrid(np.arange(nseq), np.arange(nchunk), indexing="ij")
        perm[p, (c * nseq + b).ravel(), (b * nchunk + c).ravel()] = 1.0
    return perm, perm.transpose(0, 2, 1).copy()


def _s5_state_kernel(u_ref, perm_ref, w_ref, o_ref):
    u = _dot(perm_ref[...], u_ref[...].astype(BF16)).astype(BF16)
    s = _dot(u, w_ref[...])
    for k in range(S5_SLAB_COLS):
        o_ref[k] = s[:, k * 128:(k + 1) * 128]


def _s5_state(uj, web):
    perm, _ = _s5_row_perms()
    return pl.pallas_call(
        _s5_state_kernel,
        grid=(S5_SLABS, S5_ROWS // S5_ROW_TILE),
        in_specs=[
            pl.BlockSpec((None, S5_ROW_TILE, S5_SLAB_W), lambda j, p: (j, p, 0)),
            pl.BlockSpec((None, S5_ROW_TILE, S5_ROW_TILE), lambda j, p: (p, 0, 0)),
            pl.BlockSpec((None, S5_SLAB_W, S5_SLAB_W), lambda j, p: (j, 0, 0)),
        ],
        out_specs=pl.BlockSpec((S5_SLAB_COLS, S5_ROW_TILE, 128), lambda j, p: (j, p, 0)),
        out_shape=jax.ShapeDtypeStruct((S5_STATE_COLS, S5_ROWS, 128), F32),
        compiler_params=pltpu.CompilerParams(vmem_limit_bytes=VMEM_LIMIT),
        name="s5_state",
    )(uj, jnp.asarray(perm, BF16), web)


S5_SCAN_COLS = 8


def _s5_scan_kernel(s_ref, a_ref, h0_ref, hin_ref, fin_ref, hf, hb):
    ncol = S5_SCAN_COLS

    def scan(row0, nc, nb, h0):
        is_f = lax.broadcasted_iota(jnp.int32, (nb, 128), 1) < S5_STATE
        chunk_rows = lambda c: pl.ds(pl.multiple_of(row0 + c * nb, 8), nb)

        def body(c, hs):
            rf = chunk_rows(c)
            rb = chunk_rows(nc - 1 - c)
            new = []
            for m in range(ncol // 2):
                h_re, h_im = hs[2 * m], hs[2 * m + 1]
                a_re = a_ref[:, (2 * m) * 128:(2 * m + 1) * 128]
                a_im = a_ref[:, (2 * m + 1) * 128:(2 * m + 2) * 128]
                loc = []
                for k, h in ((2 * m, h_re), (2 * m + 1, h_im)):
                    hf[k, rf, :] = h
                    hb[k, rb, :] = h
                    loc.append(jnp.where(is_f, s_ref[k, rf, :], s_ref[k, rb, :]))
                new.append(a_re * h_re - a_im * h_im + loc[0])
                new.append(a_re * h_im + a_im * h_re + loc[1])
            return tuple(new)

        return lax.fori_loop(0, nc, body, h0)

    fin = scan(0, SEQ // S5_CHUNK, BATCH, tuple(jnp.zeros((BATCH, 128), F32) for _ in range(ncol)))
    for k in range(ncol):
        fin_ref[:, k * 128:(k + 1) * 128] = fin[k]
    scan(S5_ROWS_C, DEC_SEQ // S5_CHUNK, DEC_BATCH,
         tuple(h0_ref[:, k * 128:(k + 1) * 128] for k in range(ncol)))
    fwd = lax.broadcasted_iota(jnp.int32, (ncol, S5_ROWS, 128), 2) < S5_STATE
    hin_ref[...] = jnp.where(fwd, hf[...], hb[...]).astype(BF16)


def _s5_scan(sloc, a16, h0l):
    ncol = S5_SCAN_COLS
    w = ncol * 128
    return pl.pallas_call(
        _s5_scan_kernel,
        grid=(S5_STATE_COLS // ncol,),
        in_specs=[
            pl.BlockSpec((ncol, S5_ROWS, 128), lambda k: (k, 0, 0)),
            pl.BlockSpec((1, w), lambda k: (0, k)),
            pl.BlockSpec((DEC_BATCH, w), lambda k: (0, k)),
        ],
        out_specs=[
            pl.BlockSpec((ncol, S5_ROWS, 128), lambda k: (k, 0, 0)),
            pl.BlockSpec((BATCH, w), lambda k: (0, k)),
        ],
        out_shape=[
            jax.ShapeDtypeStruct((S5_STATE_COLS, S5_ROWS, 128), BF16),
            jax.ShapeDtypeStruct((BATCH, S5_STATE_COLS * 128), F32),
        ],
        scratch_shapes=[pltpu.VMEM((ncol, S5_ROWS, 128), F32)] * 2,
        name="s5_scan",
    )(sloc, a16, h0l)


def _s5_out_kernel(u_ref, hin_ref, perm_t_ref, wt_ref, wca_ref, d_ref, y_ref):
    u = u_ref[...]
    hin = jnp.concatenate([hin_ref[k] for k in range(S5_SLAB_COLS)], axis=1).astype(BF16)
    hin = _dot(perm_t_ref[...], hin).astype(BF16)
    y = _dot(u.astype(BF16), wt_ref[...]) + _dot(hin, wca_ref[...]) + u * d_ref[...]
    for t in range(S5_CHUNK):
        y_ref[pl.ds(t, S5_ROW_TILE, stride=S5_CHUNK), :] = y[:, t * 128:(t + 1) * 128]


def _s5_out(uj, hin, wt, wca, dj):
    _, perm_t = _s5_row_perms()
    return pl.pallas_call(
        _s5_out_kernel,
        grid=(S5_SLABS, S5_ROWS // S5_ROW_TILE),
        in_specs=[
            pl.BlockSpec((None, S5_ROW_TILE, S5_SLAB_W), lambda j, p: (j, p, 0)),
            pl.BlockSpec((S5_SLAB_COLS, S5_ROW_TILE, 128), lambda j, p: (j, p, 0)),
            pl.BlockSpec((None, S5_ROW_TILE, S5_ROW_TILE), lambda j, p: (p, 0, 0)),
            pl.BlockSpec((None, S5_SLAB_W, S5_SLAB_W), lambda j, p: (j, 0, 0)),
            pl.BlockSpec((None, S5_SLAB_W, S5_SLAB_W), lambda j, p: (j, 0, 0)),
            pl.BlockSpec((None, 1, S5_SLAB_W), lambda j, p: (j, 0, 0)),
        ],
        out_specs=pl.BlockSpec((None, S5_ROW_TILE * S5_CHUNK, 128), lambda j, p: (j, p, 0)),
        out_shape=jax.ShapeDtypeStruct((S5_SLABS, NTOK, 128), F32),
        compiler_params=pltpu.CompilerParams(vmem_limit_bytes=VMEM_LIMIT),
        name="s5_out",
    )(uj, hin, jnp.asarray(perm_t, BF16), wt, wca, dj)


@functools.lru_cache(maxsize=None)
def _gla_consts():
    n = GLA_BLK
    nl = GLA_LEVELS
    r = np.arange(n)
    up = np.zeros((n, 128), np.int32)
    for l in range(nl):
        up[:, l] = (r >> l) & 1
    i = r[:, None]
    j = r[None, :]
    x = np.maximum(i ^ j, 1)
    lev = np.where(j < i, np.floor(np.log2(x)).astype(np.int32), np.where(i == j, nl, -1)).astype(np.int32)
    up2 = np.stack([up, up[::-1]])
    h = n // 2

    def tiled(a):
        return np.stack([np.concatenate([a[:h, :h], a[h:, h:]]), np.concatenate([a[:h, h:], a[h:, :h]])])

    lev2 = np.stack([tiled(lev), tiled(lev[::-1, ::-1])])
    return up2, lev2


@functools.lru_cache(maxsize=None)
def _gla_tables():
    rowblk, seq, first, last = [], [], [], []
    for d in range(2):
        rb, sq, fi, la = [], [], [], []
        for s in range(BATCH + DEC_BATCH):
            nblk = 1 if s < BATCH else DEC_SEQ // GLA_BLK
            base = s if s < BATCH else NTOK_C // GLA_BLK + (s - BATCH) * nblk
            order = range(nblk) if d == 0 else range(nblk - 1, -1, -1)
            for pos, b in enumerate(order):
                rb.append(base + b)
                sq.append(s)
                fi.append(int(pos == 0))
                la.append(int(pos == nblk - 1))
        rowblk.append(rb); seq.append(sq); first.append(fi); last.append(la)
    as_np = lambda a: np.asarray(a, np.int32)
    return as_np(rowblk), as_np(seq), as_np(first), as_np(last)


def _gla_kernel(rowblk_ref, seq_ref, first_ref, last_ref,
                qf_ref, kf_ref, vf_ref, lrf_ref, qb_ref, kb_ref, vb_ref, lrb_ref,
                wgk_ref, bgk_ref, up_ref, lev_ref, s0_ref,
                of_ref, ob_ref, fin_ref, z_scr, st_scr):
    del rowblk_ref, seq_ref
    n = pl.program_id(0)

    @pl.when(first_ref[n] == 1)
    def _():
        st_scr[...] = jnp.zeros_like(st_scr)
        for d in range(2):
            for h in range(GLA_HEADS):
                st_scr[d, h * GLA_DK:(h + 1) * GLA_DK, h * GLA_DV:(h + 1) * GLA_DV] = s0_ref[d, h]

    blocks = [
        _gla_block(False, qf_ref, kf_ref, vf_ref, lrf_ref, wgk_ref.at[0], bgk_ref.at[0], up_ref.at[0],
                   lev_ref.at[0], of_ref, z_scr.at[0], st_scr.at[0]),
        _gla_block(True, qb_ref, kb_ref, vb_ref, lrb_ref, wgk_ref.at[1], bgk_ref.at[1], up_ref.at[1],
                   lev_ref.at[1], ob_ref, z_scr.at[1], st_scr.at[1]),
    ]
    for stage in range(2):
        for block in blocks:
            next(block, None)

    @pl.when(last_ref[n] == 1)
    def _():
        for d in range(2):
            for h in range(GLA_HEADS):
                fin_ref[d, h] = st_scr[d, h * GLA_DK:(h + 1) * GLA_DK, h * GLA_DV:(h + 1) * GLA_DV]


def _gla_block(backward, q_ref, k_ref, v_ref, lr_ref, wgk_ref, bgk_ref, up_ref, lev_ref, o_ref, z_scr, st_scr):
    nl = GLA_LEVELS
    blk = GLA_BLK
    q = q_ref[...] * (GLA_DK ** -0.5)
    k = k_ref[...]
    vb = v_ref[...].astype(BF16)
    x = _dot(lr_ref[...].astype(BF16), wgk_ref[...]) + bgk_ref[...]
    gk = (jnp.minimum(x, 0.0) - jnp.log(1.0 + jnp.exp(-jnp.abs(x)))) * (1.0 / GLA_NORMALIZER)
    g_hi = gk.astype(BF16)
    g_lo = (gk - g_hi.astype(F32)).astype(BF16)
    ones = jnp.ones((blk, 128), BF16)
    tot = _dot_tn(g_hi, ones) + _dot_tn(g_lo, ones)

    row = lax.broadcasted_iota(jnp.int32, (blk, 1), 0)

    def sibling(a, l):
        g = 1 << l
        if g < 8:
            a3 = a.reshape(blk // 8, 8, a.shape[-1])
            dn = pltpu.roll(a3, g, 1).reshape(a.shape)
            up_ = pltpu.roll(a3, 8 - g, 1).reshape(a.shape)
            return jnp.where(((row >> l) & 1) == 1, dn, up_)
        a4 = a.reshape(blk // (2 * g), 2, g, a.shape[-1])
        return jnp.concatenate([a4[:, 1:2], a4[:, 0:1]], axis=1).reshape(a.shape)

    part = gk
    total = gk
    z0 = None
    for l in range(nl):
        g = 1 << l
        if g < 8:
            up = up_ref[:, l:l + 1] != 0
            z = jnp.where(up, q, k) * jnp.exp(jnp.where(up, part, total - part))
            other = sibling(total, l)
            part = part + jnp.where(up, other, 0.0)
            total = total + other
        else:
            halves = lambda a: (a.reshape(blk // (2 * g), 2, g, a.shape[-1])[:, 1 - int(backward)],
                                a.reshape(blk // (2 * g), 2, g, a.shape[-1])[:, int(backward)])
            join = lambda u, d: jnp.stack([d, u] if not backward else [u, d], axis=1).reshape(blk, u.shape[-1])
            part_u, part_d = halves(part)
            tot_u, tot_d = halves(total)
            q_u, _ = halves(q)
            _, k_d = halves(k)
            z = join(q_u * jnp.exp(part_u), k_d * jnp.exp(tot_d - part_d))
            part = join(part_u + tot_d, part_d)
            both = tot_u + tot_d
            total = join(both, both)
        if l == 0:
            z0 = z
        else:
            z_scr[l] = z.astype(BF16)
    yield
    lane128 = lax.broadcasted_iota(jnp.int32, (GLA_QK, 128), 1)
    dim = lax.broadcasted_iota(jnp.int32, (GLA_QK, 128), 0)
    head_sum = ((dim >> 6) == lane128).astype(BF16)
    pair0 = _dot((z0 * sibling(z0, 0)).astype(BF16), head_sum)
    diag = _dot((q * k).astype(BF16), head_sum)

    half = blk // 2
    lev_d = lev_ref[0]
    lev_o = lev_ref[1]
    lane = lax.broadcasted_iota(jnp.int32, (half, GLA_QK), 1)

    def tiles(l, in_head, crossed):
        out = []
        for r in range(2):
            c = 1 - r if crossed else r
            lhs = z_scr[l, r * half:(r + 1) * half, :]
            keys = z_scr[l, c * half:(c + 1) * half, :]
            out.append(_dot_nt(lhs, jnp.where(in_head, keys, jnp.zeros_like(keys))))
        return jnp.concatenate(out, axis=0)

    upi = 0 if backward else 1
    key_lanes = {}
    for l in range(3, nl - 1):
        g = 1 << l
        c = lax.broadcasted_iota(jnp.int32, (blk // (2 * g), g, 128), 0)
        ln = lax.broadcasted_iota(jnp.int32, (blk // (2 * g), g, 128), 2)
        base = (2 * g * c + (g if backward else 0)) & 127
        key_lanes[l] = (ln >= base) & (ln < base + g)

    heads = range(GLA_HEADS)
    in_head = [(lane >= h * GLA_DK) & (lane < (h + 1) * GLA_DK) for h in heads]
    acc = [jnp.where(lev_d == 0, pair0[:, h:h + 1], jnp.where(lev_d == nl, diag[:, h:h + 1], 0.0)) for h in heads]
    for l in range(1, 3):
        acc = [jnp.where(lev_d == l, tiles(l, in_head[h], False), acc[h]) for h in heads]
    for l in range(3, nl - 1):
        g = 1 << l
        for h in heads:
            acc4 = acc[h].reshape(blk // (2 * g), 2, g, 128)
            s4 = tiles(l, in_head[h], False).reshape(blk // (2 * g), 2, g, 128)
            new_up = jnp.where(key_lanes[l], s4[:, upi], acc4[:, upi])
            pieces = [acc4[:, 0], new_up] if upi == 1 else [new_up, acc4[:, 1]]
            acc[h] = jnp.stack(pieces, axis=1).reshape(blk, 128)
    off = [jnp.where(lev_o == nl - 1, tiles(nl - 1, in_head[h], True), 0.0) for h in heads]
    for h in heads:
        att = jnp.concatenate([jnp.concatenate([acc[h][:half], off[h][:half]], axis=1),
                               jnp.concatenate([off[h][half:], acc[h][half:]], axis=1)], axis=0)
        o_ref[:, h * GLA_DV:(h + 1) * GLA_DV] = _dot(att.astype(BF16), vb[:, h * GLA_DV:(h + 1) * GLA_DV])

    st = st_scr[...]
    q_in = (q * jnp.exp(part)).astype(BF16)
    o_ref[...] += _dot(q_in, st.astype(BF16))
    k_out = (k * jnp.exp(total - part)).astype(BF16)
    kv = _dot_tn(k_out, vb)
    row = lax.broadcasted_iota(jnp.int32, (GLA_QK, GLA_V), 0)
    col = lax.broadcasted_iota(jnp.int32, (GLA_QK, GLA_V), 1)
    same_head = (row >> 6) == (col >> 7)
    decay = jnp.exp(tot)
    decay = jnp.concatenate([decay] * GLA_HEADS, axis=1)
    st_new = decay * st + jnp.where(same_head, kv, 0.0)
    st_scr[...] = st_new


def _gla_mix(bslab, lr, wgk, bgk, s0):
    up, lev = _gla_consts()
    rowblk, seq, first, last = _gla_tables()
    nsteps = rowblk.shape[1]
    nseq = BATCH + DEC_BATCH
    nl = GLA_LEVELS
    whole = lambda shape: pl.BlockSpec(shape, lambda n, rb, sq, fi, la: (0,) * len(shape))

    def token_specs(d):
        return [
            pl.BlockSpec((GLA_BLK, GLA_QK), lambda n, rb, sq, fi, la: (rb[d, n], 0)),
            pl.BlockSpec((GLA_BLK, GLA_QK), lambda n, rb, sq, fi, la: (rb[d, n], 1)),
            pl.BlockSpec((GLA_BLK, GLA_V), lambda n, rb, sq, fi, la: (rb[d, n], 1)),
            pl.BlockSpec((GLA_BLK, 128), lambda n, rb, sq, fi, la: (rb[d, n], 0)),
        ]

    state_spec = pl.BlockSpec((None, 2, GLA_HEADS, GLA_DK, GLA_DV), lambda n, rb, sq, fi, la: (sq[n], 0, 0, 0, 0))
    grid_spec = pltpu.PrefetchScalarGridSpec(
        num_scalar_prefetch=4,
        grid=(nsteps,),
        in_specs=token_specs(0) + token_specs(1) + [
            whole((2, 128, GLA_QK)),
            whole((2, 1, GLA_QK)),
            whole((2, GLA_BLK, 128)),
            whole((2, 2, GLA_BLK, GLA_BLK // 2)),
            state_spec,
        ],
        out_specs=[
            pl.BlockSpec((GLA_BLK, GLA_V), lambda n, rb, sq, fi, la: (rb[0, n], 0)),
            pl.BlockSpec((GLA_BLK, GLA_V), lambda n, rb, sq, fi, la: (rb[1, n], 0)),
            state_spec,
        ],
        scratch_shapes=[
            pltpu.VMEM((2, nl, GLA_BLK, GLA_QK), BF16),
            pltpu.VMEM((2, GLA_QK, GLA_V), F32),
        ],
    )
    return pl.pallas_call(
        _gla_kernel,
        grid_spec=grid_spec,
        out_shape=[
            jax.ShapeDtypeStruct((NTOK, GLA_V), F32),
            jax.ShapeDtypeStruct((NTOK, GLA_V), F32),
            jax.ShapeDtypeStruct((nseq, 2, GLA_HEADS, GLA_DK, GLA_DV), F32),
        ],
        compiler_params=pltpu.CompilerParams(vmem_limit_bytes=VMEM_LIMIT),
        name="gla_mix",
    )(jnp.asarray(rowblk), jnp.asarray(seq[0]), jnp.asarray(first[0]), jnp.asarray(last[0]),
      bslab, bslab, bslab, lr, bslab, bslab, bslab, lr, wgk, bgk, jnp.asarray(up), jnp.asarray(lev), s0)


def _attn_ctx_kernel(sink_ref, q_ref, k_ref, v_ref, o_ref):
    k = k_ref[...]
    v = v_ref[...]
    ks = (k.astype(BF16), pltpu.roll(k, 64, 1).astype(BF16))
    vs = (v.astype(BF16), pltpu.roll(v, 64, 1).astype(BF16))
    lo = lax.broadcasted_iota(jnp.int32, (SEQ, 128), 1) < HEAD_DIM
    units = []
    for t in range(ATT_HEADS // 2):
        qt = q_ref[:, t * 128:(t + 1) * 128] * (HEAD_DIM ** -0.5)
        for p in range(2):
            qm = jnp.where(lo if p == 0 else jnp.logical_not(lo), qt, 0.0).astype(BF16)
            units.append((qm, 0 if p == t // 2 else 1, sink_ref[2 * t + p]))
    scores = [_dot_nt(qm, ks[which]) for qm, which, _ in units]
    maxes = [jnp.maximum(sink, jnp.max(s, axis=-1, keepdims=True)) for s, (_, _, sink) in zip(scores, units)]
    probs = [jnp.exp(s - m) for s, m in zip(scores, maxes)]
    dens = [jnp.exp(sink - m) + jnp.sum(p, axis=-1, keepdims=True)
            for p, m, (_, _, sink) in zip(probs, maxes, units)]
    outs = [_dot(p.astype(BF16), vs[which]) / den for p, den, (_, which, _) in zip(probs, dens, units)]
    for t in range(ATT_HEADS // 2):
        o_ref[:, t * 128:(t + 1) * 128] = jnp.where(lo, outs[2 * t], outs[2 * t + 1]).astype(BF16)


def _attn_ctx(sink, cslab):
    return pl.pallas_call(
        _attn_ctx_kernel,
        grid=(BATCH,),
        in_specs=[
            pl.BlockSpec(memory_space=pltpu.SMEM),
            pl.BlockSpec((SEQ, ATT_Q), lambda b: (b, 0)),
            pl.BlockSpec((SEQ, ATT_KV), lambda b: (b, 4)),
            pl.BlockSpec((SEQ, ATT_KV), lambda b: (b, 5)),
        ],
        out_specs=pl.BlockSpec((SEQ, ATT_Q), lambda b: (b, 0)),
        out_shape=jax.ShapeDtypeStruct((NTOK_C, ATT_Q), BF16),
        name="attn_ctx",
    )(sink, cslab, cslab, cslab)


def _attn_lat_kernel(sink_ref, q_ref, kp_ref, kc_ref, kn_ref, vp_ref, vc_ref, vn_ref,
                     ck_ref, cv_ref, cos_ref, sin_ref, bias_ref, o_ref):
    j = pl.program_id(1)
    nb = DEC_SEQ // ATT_BLOCK
    lane = lax.broadcasted_iota(jnp.int32, (ATT_BLOCK, 128), 1)
    lo = lane < HEAD_DIM
    first16 = (lane & 31) < 16

    def rope(x, blk_idx):
        r0 = pl.multiple_of(blk_idx * ATT_BLOCK, ATT_BLOCK)
        c = cos_ref[pl.ds(r0, ATT_BLOCK), :]
        s = sin_ref[pl.ds(r0, ATT_BLOCK), :]
        xs = jnp.where(first16, pltpu.roll(x, 112, 1), pltpu.roll(x, 16, 1))
        return x * c + xs * s

    nwin = 3 * ATT_BLOCK
    keys = jnp.concatenate([rope(kp_ref[...], jnp.maximum(j - 1, 0)), rope(kc_ref[...], j),
                            rope(kn_ref[...], jnp.minimum(j + 1, nb - 1)), ck_ref[...]], axis=0)
    vals = jnp.concatenate([vp_ref[...], vc_ref[...], vn_ref[...], cv_ref[...]], axis=0)
    keys2 = (keys.astype(BF16), pltpu.roll(keys, 64, 1).astype(BF16))
    vals2 = (vals.astype(BF16), pltpu.roll(vals, 64, 1).astype(BF16))
    kcol = lax.broadcasted_iota(jnp.int32, (1, nwin + PAST_LEN), 1)
    edge = jnp.where(((j == 0) & (kcol < ATT_BLOCK)) | ((j == nb - 1) & (kcol >= 2 * ATT_BLOCK) & (kcol < nwin)),
                     -1e30, 0.0)
    bias = bias_ref[...] + edge
    top = lax.broadcasted_iota(jnp.int32, (2 * ATT_BLOCK, 1), 0) < ATT_BLOCK
    q_tiles = [rope(q_ref[:, t * 128:(t + 1) * 128], j) * (HEAD_DIM ** -0.5) for t in range(ATT_HEADS // 2)]
    lo2 = jnp.concatenate([lo, lo], axis=0)
    units = []
    for kvh in range(ATT_KV_HEADS):
        q2 = jnp.concatenate(q_tiles[2 * kvh:2 * kvh + 2], axis=0)
        for p in range(2):
            qm = jnp.where(lo2 if p == 0 else jnp.logical_not(lo2), q2, 0.0).astype(BF16)
            sink = jnp.where(top, sink_ref[4 * kvh + p], sink_ref[4 * kvh + 2 + p])
            units.append((qm, 0 if p == kvh else 1, sink))
    scores = [_dot_nt(qm, keys2[which]) + bias for qm, which, _ in units]
    maxes = [jnp.maximum(sink, jnp.max(s, axis=-1, keepdims=True)) for s, (_, _, sink) in zip(scores, units)]
    probs = [jnp.exp(s - m) for s, m in zip(scores, maxes)]
    dens = [jnp.exp(sink - m) + jnp.sum(p, axis=-1, keepdims=True)
            for p, m, (_, _, sink) in zip(probs, maxes, units)]
    outs = [_dot(p.astype(BF16), vals2[which]) / den for p, den, (_, which, _) in zip(probs, dens, units)]
    for kvh in range(ATT_KV_HEADS):
        o2 = jnp.where(lo2, outs[2 * kvh], outs[2 * kvh + 1])
        for i in range(2):
            t = 2 * kvh + i
            o_ref[:, t * 128:(t + 1) * 128] = o2[i * ATT_BLOCK:(i + 1) * ATT_BLOCK].astype(BF16)


def _attn_lat(sink, cslab, ck, cv, cos_t, sin_t):
    nb = DEC_SEQ // ATT_BLOCK
    base = NTOK_C // ATT_BLOCK
    cur = lambda b, j: base + b * nb + j
    prv = lambda b, j: base + b * nb + jnp.maximum(j - 1, 0)
    nxt = lambda b, j: base + b * nb + jnp.minimum(j + 1, nb - 1)
    kv_spec = lambda row, col: pl.BlockSpec((ATT_BLOCK, ATT_KV), lambda b, j: (row(b, j), col))
    qi = np.arange(2 * ATT_BLOCK)[:, None] % ATT_BLOCK
    kc = np.arange(3 * ATT_BLOCK + PAST_LEN)[None, :]
    inside = (np.abs(kc - ATT_BLOCK - qi) <= WINDOW) | (kc >= 3 * ATT_BLOCK)
    band = np.where(inside, 0.0, -1e30).astype(np.float32)
    return pl.pallas_call(
        _attn_lat_kernel,
        grid=(DEC_BATCH, nb),
        in_specs=[
            pl.BlockSpec(memory_space=pltpu.SMEM),
            pl.BlockSpec((ATT_BLOCK, ATT_Q), lambda b, j: (cur(b, j), 0)),
            kv_spec(prv, 4), kv_spec(cur, 4), kv_spec(nxt, 4),
            kv_spec(prv, 5), kv_spec(cur, 5), kv_spec(nxt, 5),
            pl.BlockSpec((None, PAST_LEN, ATT_KV), lambda b, j: (b, 0, 0)),
            pl.BlockSpec((None, PAST_LEN, ATT_KV), lambda b, j: (b, 0, 0)),
            pl.BlockSpec((DEC_SEQ, 128), lambda b, j: (0, 0)),
            pl.BlockSpec((DEC_SEQ, 128), lambda b, j: (0, 0)),
            pl.BlockSpec(band.shape, lambda b, j: (0, 0)),
        ],
        out_specs=pl.BlockSpec((ATT_BLOCK, ATT_Q), lambda b, j: (b * nb + j, 0)),
        out_shape=jax.ShapeDtypeStruct((NTOK_L, ATT_Q), BF16),
        name="attn_lat",
    )(sink, cslab, cslab, cslab, cslab, cslab, cslab, cslab, ck, cv, cos_t, sin_t, jnp.asarray(band))


@functools.lru_cache(maxsize=None)
def _rope_tables():
    rows = DEC_SEQ // GRID_W
    row = np.repeat(np.arange(rows, dtype=np.float64), GRID_W)
    col = np.tile(np.arange(GRID_W, dtype=np.float64), rows)
    quarter = HEAD_DIM // 4
    inv = ROPE_BASE ** (-np.arange(quarter, dtype=np.float64) / quarter)
    lane = np.arange(128)
    use_row = (lane % HEAD_DIM) < HEAD_DIM // 2
    pos = np.where(use_row[None, :], row[:, None], col[:, None])
    ang = pos * inv[lane % quarter][None, :]
    sign = np.where((lane % 32) < 16, -1.0, 1.0)
    return np.cos(ang).astype(np.float32), (np.sin(ang) * sign[None, :]).astype(np.float32)


def _merge_kernel(*refs, split_x):
    if split_x:
        xc_ref, xl_ref, *refs = refs
    else:
        xc_ref, *refs = refs
    (mod_ref, g_ref, ys5_ref, ogf_ref, ogb_ref, gb_ref, ycc_ref, ycl_ref, gate_ref, gng_ref,
     wglu_ref, wbr_ref, wout_ref, o_ref) = refs
    is_ctx = pl.program_id(0) < NTOK_C // TM
    if split_x:
        x = jnp.where(is_ctx, xc_ref[...], xl_ref[...])
    else:
        x = xc_ref[...]
    y = jnp.concatenate([ys5_ref[j] for j in range(S5_SLABS)], axis=1)
    y = 0.5 * y * (1.0 + jnp.tanh(math.sqrt(2.0 / math.pi) * (y + 0.044715 * (y * y * y))))
    ag = _dot(y.astype(BF16), wglu_ref[...])
    y_a = ag[:, :S5_WIDTH] * _sigmoid(ag[:, S5_WIDTH:])
    gng = gng_ref[...]
    parts = []
    for h in range(GLA_HEADS):
        sl = slice(h * GLA_DV, (h + 1) * GLA_DV)
        o = ogf_ref[:, sl] + ogb_ref[:, sl]
        g = gb_ref[:, sl]
        parts.append(_rms(o, gng) * (g * _sigmoid(g)))
    y_b = jnp.concatenate(parts, axis=1)
    y_c = jnp.where(is_ctx, ycc_ref[...], ycl_ref[...])
    merged = None
    for n, yn in enumerate((y_a, y_b, y_c)):
        proj = _dot(yn.astype(BF16), wbr_ref[n])
        term = gate_ref[:, n * D_MODEL:(n + 1) * D_MODEL].astype(F32) * proj
        merged = term if merged is None else merged + term
    mixed = _dot(merged.astype(BF16), wout_ref[...])
    g1 = mod_ref[:, 2 * D_MODEL:3 * D_MODEL]
    o_ref[...] = x + g1 * _rms(mixed, g_ref[...])


def _layer_spec(shape, layer):
    return pl.BlockSpec((None,) + shape, lambda i: (layer,) + (0,) * len(shape), pipeline_mode=pl.Buffered(1))


def _split_token_specs(n_arrays, width=D_MODEL):
    nct = NTOK_C // TM
    if n_arrays == 2:
        return [pl.BlockSpec((TM, width), lambda i: (jnp.minimum(i, nct - 1), 0)),
                pl.BlockSpec((TM, width), lambda i: (jnp.maximum(i - nct, 0), 0))]
    return [pl.BlockSpec((TM, width), lambda i: (i, 0))]


def _merge(xs, mod, g, ys5, og, bslab, yc, gates, gng, wglu, wbr, wout, layer):
    tok = lambda width, col=0: pl.BlockSpec((TM, width), lambda i: (i, col))
    full = lambda shape: _layer_spec(shape, layer)
    return pl.pallas_call(
        functools.partial(_merge_kernel, split_x=len(xs) == 2),
        grid=(NTOK // TM,),
        in_specs=_split_token_specs(len(xs)) + [
            pl.BlockSpec((None, 1, 6 * D_MODEL), lambda i: (_mod_row(i), 0, 0)),
            pl.BlockSpec((1, D_MODEL), lambda i: (0, 0)),
            pl.BlockSpec((S5_SLABS, TM, 128), lambda i: (0, _s5_tile(i), 0)),
            tok(GLA_V),
            tok(GLA_V),
            tok(GLA_V, 2),
        ] + _split_token_specs(2, ATT_Q) + [
            tok(N_BRANCH * D_MODEL),
            pl.BlockSpec((1, GLA_DV), lambda i: (0, 0)),
            full((S5_WIDTH, 2 * S5_WIDTH)),
            full((N_BRANCH, BRANCH_W, D_MODEL)),
            full((D_MODEL, D_MODEL)),
        ],
        out_specs=tok(D_MODEL),
        out_shape=jax.ShapeDtypeStruct((NTOK, D_MODEL), F32),
        compiler_params=pltpu.CompilerParams(vmem_limit_bytes=VMEM_LIMIT),
        name="merge",
    )(*xs, mod, g, ys5, *og, bslab, *yc, gates, gng, wglu, wbr, wout)


FFN_SPLIT = 2


def _ffn_kernel(x_ref, mod_ref, gin_ref, gout_ref, w1_ref, w2_ref, *o_refs):
    x = x_ref[...]
    sh = mod_ref[:, 3 * D_MODEL:4 * D_MODEL]
    sc = mod_ref[:, 4 * D_MODEL:5 * D_MODEL]
    g2 = mod_ref[:, 5 * D_MODEL:6 * D_MODEL]
    h = (_rms(x, gin_ref[...]) * (1.0 + sc) + sh).astype(BF16)
    ck = FFN_HIDDEN // FFN_SPLIT
    acc = None
    for c in range(FFN_SPLIT):
        a = _dot(h, w1_ref[:, c * ck:(c + 1) * ck])
        b = _dot(h, w1_ref[:, FFN_HIDDEN + c * ck:FFN_HIDDEN + (c + 1) * ck])
        act = (a * _sigmoid(a) * b).astype(BF16)
        part = _dot(act, w2_ref[c * ck:(c + 1) * ck, :])
        acc = part if acc is None else acc + part
    y = x + g2 * _rms(acc, gout_ref[...])
    if len(o_refs) == 1:
        o_refs[0][...] = y
    else:
        is_ctx = pl.program_id(0) < NTOK_C // TM

        @pl.when(is_ctx)
        def _():
            o_refs[0][...] = y

        @pl.when(jnp.logical_not(is_ctx))
        def _():
            o_refs[1][...] = y


def _ffn(x, mod, gin, gout, w1, w2, layer, split_out):
    small = lambda shape: pl.BlockSpec(shape, lambda i: (0,) * len(shape))
    nct = NTOK_C // TM
    if split_out:
        out_specs = [pl.BlockSpec((TM, D_MODEL), lambda i: (jnp.minimum(i, nct - 1), 0)),
                     pl.BlockSpec((TM, D_MODEL), lambda i: (jnp.maximum(i - nct, 0), 0))]
        out_shape = [jax.ShapeDtypeStruct((NTOK_C, D_MODEL), F32), jax.ShapeDtypeStruct((NTOK_L, D_MODEL), F32)]
    else:
        out_specs = pl.BlockSpec((TM, D_MODEL), lambda i: (i, 0))
        out_shape = jax.ShapeDtypeStruct((NTOK, D_MODEL), F32)
    return pl.pallas_call(
        _ffn_kernel,
        grid=(NTOK // TM,),
        in_specs=[
            pl.BlockSpec((TM, D_MODEL), lambda i: (i, 0)),
            pl.BlockSpec((None, 1, 6 * D_MODEL), lambda i: (_mod_row(i), 0, 0)),
            small((1, D_MODEL)),
            small((1, D_MODEL)),
            _layer_spec((D_MODEL, 2 * FFN_HIDDEN), layer),
            _layer_spec((FFN_HIDDEN, D_MODEL), layer),
        ],
        out_specs=out_specs,
        out_shape=out_shape,
        compiler_params=pltpu.CompilerParams(vmem_limit_bytes=VMEM_LIMIT),
        name="ffn",
    )(x, mod, gin, gout, w1, w2)


def kernel(x_prompt, x_sample, cache_k, cache_v, state_s5, state_gla, c, c_ctx, w_mod, b_mod, norm_g, w_in,
           s5_lam_re, s5_lam_im, s5_log_step, s5_b_re, s5_b_im, s5_c_re, s5_c_im, s5_d, w_glu, gla_w_gk,
           gla_b_gk, gla_norm_g, att_sink, w_branch, w_out, w_ffn_in, w_ffn_out):
    cond = jnp.concatenate([c_ctx[None, :], c, jnp.zeros((N_MOD_ROWS - 1 - DEC_BATCH, D_MODEL), F32)], axis=0)
    mod_all = _modulation(cond, w_mod, b_mod).reshape(DEPTH, N_MOD_ROWS, 1, 6 * D_MODEL)
    cos_t, sin_t = _rope_tables()
    xs = (x_prompt.reshape(NTOK_C, D_MODEL), x_sample.reshape(NTOK_L, D_MODEL))
    w_in_b = w_in.astype(BF16)
    w_in_end = jnp.pad(w_in[:, :, D_IN_TILED:].astype(BF16), ((0, 0), (0, 0), (0, W_IN_COLS - D_IN)))
    w_glu_b, w_branch_b, w_out_b = w_glu.astype(BF16), w_branch.astype(BF16), w_out.astype(BF16)
    w_ffn_in_b, w_ffn_out_b = w_ffn_in.astype(BF16), w_ffn_out.astype(BF16)
    s5_params = _s5_params(s5_lam_re, s5_lam_im, s5_log_step, s5_b_re, s5_b_im, s5_c_re, s5_c_im)
    h0_all = state_s5.astype(F32).transpose(1, 0, 3, 5, 2, 4).reshape(DEPTH, DEC_BATCH, S5_GROUPS * 256)
    wgk_all = jnp.stack([jnp.pad(gla_w_gk[:, d], ((0, 0), (d * GLA_RANK, 128 - (d + 1) * GLA_RANK), (0, 0)))
                         for d in range(2)], axis=1).astype(BF16)
    bgk_all = gla_b_gk[:, :, None, :].astype(F32)
    new_k, new_v, new_s5, new_gla = [], [], [], []
    for i in range(DEPTH):
        mod = mod_all[i]
        uj, bslab, cslab, gates, lr = _inproj(xs, mod, norm_g[i, 0][None, :], w_in_b, w_in_end, i)

        wt, web, wca, a16, dj = _s5_prep(s5_params, s5_d[i], i)
        hin, finc = _s5_scan(_s5_state(uj, web), a16, h0_all[i])
        ys5 = _s5_out(uj, hin, wt, wca, dj)
        new_s5.append(finc)

        s0 = jnp.concatenate([jnp.zeros((BATCH, 2, GLA_HEADS, GLA_DK, GLA_DV), F32),
                              state_gla[:, i].astype(F32)], axis=0)
        *og, gla_fin = _gla_mix(bslab, lr, wgk_all[i], bgk_all[i], s0)
        new_gla.append(gla_fin[:BATCH])

        sink = att_sink[i].astype(F32)
        yc = (_attn_ctx(sink, cslab),
              _attn_lat(sink, cslab, cache_k[:, i].reshape(DEC_BATCH, PAST_LEN, ATT_KV).astype(F32),
                        cache_v[:, i].reshape(DEC_BATCH, PAST_LEN, ATT_KV).astype(F32), cos_t, sin_t))
        new_k.append(cslab[:NTOK_C, ATT_Q:ATT_Q + ATT_KV].reshape(BATCH, SEQ, ATT_KV_HEADS, HEAD_DIM))
        new_v.append(cslab[:NTOK_C, ATT_Q + ATT_KV:].reshape(BATCH, SEQ, ATT_KV_HEADS, HEAD_DIM))

        x = _merge(xs, mod, norm_g[i, 1][None, :], ys5, og, bslab, yc, gates, gla_norm_g[i][None, :],
                   w_glu_b, w_branch_b, w_out_b, i)
        last = i == DEPTH - 1
        x = _ffn(x, mod, norm_g[i, 2][None, :], norm_g[i, 3][None, :], w_ffn_in_b, w_ffn_out_b, i, last)
        xs = tuple(x) if last else (x,)

    return (xs[0].reshape(BATCH, SEQ, D_MODEL), xs[1].reshape(DEC_BATCH, DEC_SEQ, D_MODEL),
            jnp.stack(new_k, axis=1), jnp.stack(new_v, axis=1),
            jnp.stack(new_s5).reshape(DEPTH, BATCH, S5_GROUPS, 2, 2, S5_STATE).transpose(1, 0, 4, 2, 5, 3),
            jnp.stack(new_gla, axis=1))
```

```python
import functools
import math

import numpy as np
import jax
import jax.numpy as jnp
from jax import lax
from jax.experimental import pallas as pl
from jax.experimental.pallas import tpu as pltpu

F32 = jnp.float32
BF16 = jnp.bfloat16

D_MODEL = 1024
BATCH = 16
SEQ = 256
DEPTH = 2
DEC_BATCH = 8
DEC_SEQ = 1024
PAST_LEN = 256
GRID_W = 64
ROPE_BASE = 10000.0
S5_WIDTH = 512
S5_GROUP = 16
S5_GROUPS = 32
S5_STATE = 64
GLA_HEADS = 4
GLA_DK = 64
GLA_DV = 128
GLA_QK = 256
GLA_V = 512
GLA_RANK = 16
GLA_NORMALIZER = 16.0
ATT_HEADS = 8
ATT_KV_HEADS = 2
HEAD_DIM = 64
ATT_Q = 512
ATT_KV = 128
WINDOW = 128
ATT_BLOCK = 128
N_BRANCH = 3
BRANCH_W = 512
FFN_HIDDEN = 2816
RMS_EPS = 1e-6

NTOK_C = BATCH * SEQ
NTOK_L = DEC_BATCH * DEC_SEQ
NTOK = NTOK_C + NTOK_L
TM = 512
N_MOD_ROWS = 16

D_IN = 5920
D_IN_TILED = D_IN // 128 * 128
W_IN_COLS = 6016
S5_CHUNK = 16
S5_SLABS = S5_WIDTH // 128
S5_SLAB_W = S5_CHUNK * 128
S5_ROWS_C = NTOK_C // S5_CHUNK
S5_ROWS = NTOK // S5_CHUNK
S5_ROW_TILE = 256
GLA_BLK = 256
GLA_LEVELS = 8
VMEM_LIMIT = 56 * 1024 * 1024


def _dot(a, b):
    return jnp.dot(a, b, preferred_element_type=F32)


def _dot_nt(a, b):
    return lax.dot_general(a, b, (((1,), (1,)), ((), ())), preferred_element_type=F32)


def _dot_tn(a, b):
    return lax.dot_general(a, b, (((0,), (0,)), ((), ())), preferred_element_type=F32)


def _rms(x, g):
    return x * lax.rsqrt(jnp.mean(x * x, axis=-1, keepdims=True) + RMS_EPS) * g


def _sigmoid(x):
    return 0.5 * jnp.tanh(0.5 * x) + 0.5


def _mod_row(i):
    nct = NTOK_C // TM
    return jnp.where(i < nct, 0, 1 + (i - nct) // (DEC_SEQ // TM))


def _mod_spec(layer):
    return pl.BlockSpec((None, None, 1, 6 * D_MODEL), lambda i: (layer, _mod_row(i), 0, 0))


def _gain_spec(layer, k):
    return pl.BlockSpec((None, 1, D_MODEL), lambda i: (layer * 4 + k, 0, 0))


def _s5_tile(i):
    nct = NTOK_C // TM
    per_seq = DEC_SEQ // TM
    k = i - nct
    return jnp.where(i < nct, i, nct + (k % per_seq) * DEC_BATCH + k // per_seq)


def _mod_kernel(c_ref, w_ref, b_ref, o_ref):
    c = c_ref[...]
    s = (c * _sigmoid(c)).astype(BF16)
    o_ref[...] = _dot(s, w_ref[...].astype(BF16)) + b_ref[...]


def _modulation(cond, w_mod, b_mod):
    tn = 2048
    return pl.pallas_call(
        _mod_kernel,
        grid=(DEPTH, 6 * D_MODEL // tn),
        in_specs=[
            pl.BlockSpec((N_MOD_ROWS, D_MODEL), lambda l, n: (0, 0)),
            pl.BlockSpec((None, D_MODEL, tn), lambda l, n: (l, 0, n)),
            pl.BlockSpec((None, 1, tn), lambda l, n: (l, 0, n)),
        ],
        out_specs=pl.BlockSpec((None, N_MOD_ROWS, tn), lambda l, n: (l, 0, n)),
        out_shape=jax.ShapeDtypeStruct((DEPTH, N_MOD_ROWS, 6 * D_MODEL), F32),
        name="modulation",
    )(cond, w_mod, b_mod.reshape(DEPTH, 1, 6 * D_MODEL))


_IN_SLABS = ((0, 512), (512, 1536), (2048, 768), (2816, 3072), (5888, 128))
W_IN_SPLIT = 2048
W_IN_GAP = 32
W_IN_TAIL = W_IN_COLS - W_IN_SPLIT


def _inproj_kernel(*refs, split_x):
    if split_x:
        xc_ref, xl_ref, *refs = refs
    else:
        xc_ref, *refs = refs
    mod_ref, g_ref, w_ref, w_end_ref, u_ref, b_ref, c_ref, gate_ref, lr_ref, w_tail, u_stage = refs
    i = pl.program_id(0)

    @pl.when(i == 0)
    def _():
        r = lax.broadcasted_iota(jnp.int32, (256, 128), 0)
        c = lax.broadcasted_iota(jnp.int32, (256, 128), 1)
        shift = (r == c + W_IN_GAP).astype(BF16)
        head = ((r == c) & (c < W_IN_GAP)).astype(BF16)
        ntile = (W_IN_TAIL - 128) // 128
        for t in range(ntile - 1):
            src = W_IN_SPLIT + 128 * t
            w_tail[:, 128 * t:128 * (t + 1)] = _dot(w_ref[:, src:src + 256], shift).astype(BF16)
        src = W_IN_SPLIT + 128 * (ntile - 1)
        last = jnp.concatenate([w_ref[:, src:src + 128], w_end_ref[...]], axis=1)
        w_tail[:, 128 * (ntile - 1):128 * ntile] = _dot(last, shift).astype(BF16)
        w_tail[:, 128 * ntile:] = _dot(w_ref[:, W_IN_SPLIT:W_IN_SPLIT + 256], head).astype(BF16)

    if split_x:
        x = jnp.where(i < NTOK_C // TM, xc_ref[...], xl_ref[...])
    else:
        x = xc_ref[...]
    mod = mod_ref[...]
    h = _rms(x, g_ref[...]) * (1.0 + mod[:, D_MODEL:2 * D_MODEL]) + mod[:, 0:D_MODEL]
    h = h.astype(BF16)
    for j in range(S5_SLABS):
        u_stage[...] = _dot(h, w_ref[:, j * 128:(j + 1) * 128])
        for s in range(S5_CHUNK):
            u_ref[j, :, s * 128:(s + 1) * 128] = u_stage[pl.ds(s, TM // S5_CHUNK, stride=S5_CHUNK), :]
    b_ref[...] = _dot(h, w_ref[:, 512:W_IN_SPLIT])
    for (off, width), o_ref in zip(_IN_SLABS[2:], (c_ref, gate_ref, lr_ref)):
        z = _dot(h, w_tail[:, off - W_IN_SPLIT:off - W_IN_SPLIT + width])
        o_ref[...] = _sigmoid(z).astype(BF16) if o_ref is gate_ref else z


def _inproj(xs, mod, g, w_all, w_end, layer):
    return pl.pallas_call(
        functools.partial(_inproj_kernel, split_x=len(xs) == 2),
        grid=(NTOK // TM,),
        in_specs=_split_token_specs(len(xs)) + [
            _mod_spec(layer),
            _gain_spec(layer, 0),
            pl.BlockSpec((None, D_MODEL, D_IN), lambda i: (layer, 0, 0), pipeline_mode=pl.Buffered(1)),
            pl.BlockSpec((None, D_MODEL, 128), lambda i: (layer, 0, 0), pipeline_mode=pl.Buffered(1)),
        ],
        out_specs=[pl.BlockSpec((S5_SLABS, TM // S5_CHUNK, S5_SLAB_W), lambda i: (0, _s5_tile(i), 0))]
        + [pl.BlockSpec((TM, width), lambda i: (i, 0)) for _, width in _IN_SLABS[1:]],
        out_shape=[jax.ShapeDtypeStruct((S5_SLABS, S5_ROWS, S5_SLAB_W), F32)]
        + [jax.ShapeDtypeStruct((NTOK, width), BF16 if width == N_BRANCH * D_MODEL else F32)
           for _, width in _IN_SLABS[1:]],
        scratch_shapes=[pltpu.VMEM((D_MODEL, W_IN_TAIL), BF16), pltpu.VMEM((TM, 128), F32)],
        compiler_params=pltpu.CompilerParams(vmem_limit_bytes=VMEM_LIMIT),
        name="inproj",
    )(*xs, mod, g, w_all, w_end)


@functools.lru_cache(maxsize=None)
def _s5_expanders():
    seg = 8
    spread = np.zeros((seg, 256, S5_SLAB_W), np.float32)
    place = np.zeros((seg, 256, S5_SLAB_W), np.float32)
    col = np.arange(256)
    for gl in range(seg):
        spread[gl, col, (col // S5_GROUP) * 128 + gl * S5_GROUP + col % S5_GROUP] = 1.0
        place[gl, col, gl * 256 + col] = 1.0
    return spread, place


def _s5_prep_kernel(par_ref, bre_ref, bim_ref, cre_ref, cim_ref, spread_ref, place_ref,
                    wt_ref, web_ref, wca_ref, a16_ref):
    n = S5_CHUNK
    lam_re = par_ref[0:1, :]
    lam_im = par_ref[1:2, :]
    dt = jnp.exp(par_ref[2:3, :])
    lr = lam_re * dt
    li = lam_im * dt
    krow = lax.broadcasted_iota(jnp.int32, (24, 128), 0).astype(F32)
    tab_mag = jnp.exp(krow * lr)
    tab_re = tab_mag * jnp.cos(krow * li)
    tab_im = tab_mag * jnp.sin(krow * li)
    ar = tab_re[1:2, :]
    ai = tab_im[1:2, :]
    nr = ar - 1.0
    den = lam_re * lam_re + lam_im * lam_im
    fr = (nr * lam_re + ai * lam_im) / den
    fi = (ai * lam_re - nr * lam_im) / den
    b_re = bre_ref[...]
    b_im = bim_ref[...]
    br = fr * b_re - fi * b_im
    bi = fr * b_im + fi * b_re
    c_re = cre_ref[...]
    c_im = cim_ref[...]

    def lo_half(shape):
        return lax.broadcasted_iota(jnp.int32, shape, 1) < S5_STATE

    def tile_rows(a):
        return jnp.concatenate([a] * n, axis=0)

    fwd16 = lo_half((S5_GROUP, 128))

    def powers(t_re, t_im, k_fwd, k_bwd):
        def pick(t, b):
            kf, kb = k_fwd(b), k_bwd(b)
            return jnp.where(fwd16, jnp.broadcast_to(t[kf:kf + 1, :], (S5_GROUP, 128)),
                             jnp.broadcast_to(t[kb:kb + 1, :], (S5_GROUP, 128)))
        return (jnp.concatenate([pick(t_re, b) for b in range(n)], axis=0),
                jnp.concatenate([pick(t_im, b) for b in range(n)], axis=0))

    fwd = lo_half((n * S5_GROUP, 128))
    brt, bit, crt, cit = tile_rows(br), tile_rows(bi), tile_rows(c_re), tile_rows(c_im)

    per, pei = powers(tab_re, tab_im, lambda s: n - 1 - s, lambda s: s)
    eb = jnp.concatenate([brt * per - bit * pei, brt * pei + bit * per], axis=1)
    pcr, pci = powers(tab_re, tab_im, lambda t: t + 1, lambda t: n - t)
    ca = jnp.concatenate([(crt * pcr - cit * pci).T, (-(crt * pci + cit * pcr)).T], axis=0)

    def one_dir(x, d):
        sw = pltpu.roll(x, S5_STATE, 1)
        lo = lo_half(x.shape)
        return jnp.where(lo, x, sw) if d == 0 else jnp.where(lo, sw, x)

    klag = []
    for d in range(2):
        lhs = jnp.where(lo_half(br.shape), one_dir(br, d), -one_dir(bi, d))
        crd, cid = tile_rows(one_dir(c_re, d)), tile_rows(one_dir(c_im, d))
        lag = (lambda b: b) if d == 0 else (lambda b: n - 1 - b)
        pr, pi = powers(one_dir(tab_re, d), one_dir(tab_im, d), lag, lag)
        rhs_t = jnp.where(fwd, crd * pr - cid * pi, crd * pi + cid * pr)
        klag.append(lax.dot_general(lhs, rhs_t, (((1,), (1,)), ((), ())),
                                    precision=lax.Precision.HIGHEST, preferred_element_type=F32))
    lane = lax.broadcasted_iota(jnp.int32, (S5_GROUP, n * S5_GROUP), 1)
    rows = []
    for s in range(n):
        f = klag[0] if s == 0 else jnp.where(lane >= S5_GROUP * s, pltpu.roll(klag[0], S5_GROUP * s, 1), 0.0)
        sh = (n * S5_GROUP - S5_GROUP * (n - 1 - s)) % (n * S5_GROUP)
        b = klag[1] if sh == 0 else pltpu.roll(klag[1], sh, 1)
        rows.append(f + jnp.where(lane < S5_GROUP * (s + 1), b, 0.0))
    toep = jnp.concatenate(rows, axis=0)

    spread = spread_ref[...]
    wt_ref[...] = _dot(toep.astype(BF16), spread).astype(BF16).reshape(n, S5_GROUP, S5_SLAB_W)
    web_ref[...] = _dot(eb.astype(BF16), place_ref[...]).astype(BF16).reshape(n, S5_GROUP, S5_SLAB_W)
    wca_ref[...] = _dot(ca.astype(BF16), spread).astype(BF16)
    a16_ref[0:1, :] = tab_re[n:n + 1, :]
    a16_ref[1:2, :] = tab_im[n:n + 1, :]


def _s5_params(lam_re, lam_im, log_step, b_re, b_im, c_re, c_im):
    par = jnp.stack([lam_re, lam_im, log_step], axis=1).astype(F32)
    par = par.transpose(0, 3, 1, 2, 4).reshape(DEPTH, S5_GROUPS, 3, 128)
    par = jnp.pad(par, ((0, 0), (0, 0), (0, 5), (0, 0)))
    b_t = lambda b: b.astype(F32).transpose(0, 2, 4, 1, 3).reshape(DEPTH, S5_GROUPS, S5_GROUP, 128)
    c_t = lambda c: c.astype(F32).transpose(0, 2, 3, 1, 4).reshape(DEPTH, S5_GROUPS, S5_GROUP, 128)
    return par, b_t(b_re), b_t(b_im), c_t(c_re), c_t(c_im)


def _s5_prep(params, d_skip, layer):
    seg = 8
    spread, place = _s5_expanders()
    vec = pl.BlockSpec((None, None, S5_GROUP, 128), lambda gl, j: (layer, j * seg + gl, 0, 0))
    exp_spec = pl.BlockSpec((None, 256, S5_SLAB_W), lambda gl, j: (gl, 0, 0))
    rows_spec = pl.BlockSpec((None, S5_CHUNK, None, S5_GROUP, S5_SLAB_W), lambda gl, j: (j, 0, gl, 0, 0))
    wt, web, wca, a16 = pl.pallas_call(
        _s5_prep_kernel,
        grid=(seg, S5_SLABS),
        in_specs=[pl.BlockSpec((None, None, 8, 128), lambda gl, j: (layer, j * seg + gl, 0, 0)),
                  vec, vec, vec, vec, exp_spec, exp_spec],
        out_specs=[
            rows_spec, rows_spec,
            pl.BlockSpec((None, None, 256, S5_SLAB_W), lambda gl, j: (j, gl, 0, 0)),
            pl.BlockSpec((None, 2, 128), lambda gl, j: (j * seg + gl, 0, 0)),
        ],
        out_shape=[
            jax.ShapeDtypeStruct((S5_SLABS, S5_CHUNK, seg, S5_GROUP, S5_SLAB_W), BF16),
            jax.ShapeDtypeStruct((S5_SLABS, S5_CHUNK, seg, S5_GROUP, S5_SLAB_W), BF16),
            jax.ShapeDtypeStruct((S5_SLABS, seg, 256, S5_SLAB_W), BF16),
            jax.ShapeDtypeStruct((S5_GROUPS, 2, 128), F32),
        ],
        name="s5_prep",
    )(*params, jnp.asarray(spread, BF16), jnp.asarray(place, BF16))
    mat = (S5_SLABS, S5_SLAB_W, S5_SLAB_W)
    dj = jnp.tile(d_skip.astype(F32).reshape(S5_SLABS, 1, 128), (1, 1, S5_CHUNK))
    return wt.reshape(mat), web.reshape(mat), wca.reshape(mat), a16.reshape(1, S5_SLABS * S5_SLAB_W), dj


S5_STATE_COLS = S5_SLABS * S5_SLAB_W // 128
S5_SLAB_COLS = S5_SLAB_W // 128


@functools.lru_cache(maxsize=None)
def _s5_row_perms():
    assert S5_ROW_TILE == S5_ROWS_C == DEC_BATCH * TM // S5_CHUNK
    perm = np.zeros((3, S5_ROW_TILE, S5_ROW_TILE), np.float32)
    for p, (nseq, nchunk) in enumerate(((BATCH, SEQ // S5_CHUNK), (DEC_BATCH, TM // S5_CHUNK),
                                        (DEC_BATCH, TM // S5_CHUNK))):
        b, c = np.meshgrid(np.arange(nseq), np.arange(nchunk), indexing="ij")
        perm[p, (c * nseq + b).ravel(), (b * nchunk + c).ravel()] = 1.0
    return perm, perm.transpose(0, 2, 1).copy()


def _s5_state_kernel(u_ref, perm_ref, w_ref, o_ref):
    u = _dot(perm_ref[...], u_ref[...].astype(BF16)).astype(BF16)
    s = _dot(u, w_ref[...])
    for k in range(S5_SLAB_COLS):
        o_ref[k] = s[:, k * 128:(k + 1) * 128]


def _s5_state(uj, web):
    perm, _ = _s5_row_perms()
    return pl.pallas_call(
        _s5_state_kernel,
        grid=(S5_SLABS, S5_ROWS // S5_ROW_TILE),
        in_specs=[
            pl.BlockSpec((None, S5_ROW_TILE, S5_SLAB_W), lambda j, p: (j, p, 0)),
            pl.BlockSpec((None, S5_ROW_TILE, S5_ROW_TILE), lambda j, p: (p, 0, 0)),
            pl.BlockSpec((None, S5_SLAB_W, S5_SLAB_W), lambda j, p: (j, 0, 0)),
        ],
        out_specs=pl.BlockSpec((S5_SLAB_COLS, S5_ROW_TILE, 128), lambda j, p: (j, p, 0)),
        out_shape=jax.ShapeDtypeStruct((S5_STATE_COLS, S5_ROWS, 128), F32),
        compiler_params=pltpu.CompilerParams(vmem_limit_bytes=VMEM_LIMIT),
        name="s5_state",
    )(uj, jnp.asarray(perm, BF16), web)


S5_SCAN_COLS = 8


def _s5_scan_kernel(s_ref, a_ref, h0_ref, hin_ref, fin_ref, hf, hb):
    ncol = S5_SCAN_COLS

    def scan(row0, nc, nb, h0):
        is_f = lax.broadcasted_iota(jnp.int32, (nb, 128), 1) < S5_STATE
        chunk_rows = lambda c: pl.ds(pl.multiple_of(row0 + c * nb, 8), nb)

        def body(c, hs):
            rf = chunk_rows(c)
            rb = chunk_rows(nc - 1 - c)
            new = []
            for m in range(ncol // 2):
                h_re, h_im = hs[2 * m], hs[2 * m + 1]
                a_re = a_ref[:, (2 * m) * 128:(2 * m + 1) * 128]
                a_im = a_ref[:, (2 * m + 1) * 128:(2 * m + 2) * 128]
                loc = []
                for k, h in ((2 * m, h_re), (2 * m + 1, h_im)):
                    hf[k, rf, :] = h
                    hb[k, rb, :] = h
                    loc.append(jnp.where(is_f, s_ref[k, rf, :], s_ref[k, rb, :]))
                new.append(a_re * h_re - a_im * h_im + loc[0])
                new.append(a_re * h_im + a_im * h_re + loc[1])
            return tuple(new)

        return lax.fori_loop(0, nc, body, h0)

    fin = scan(0, SEQ // S5_CHUNK, BATCH, tuple(jnp.zeros((BATCH, 128), F32) for _ in range(ncol)))
    for k in range(ncol):
        fin_ref[:, k * 128:(k + 1) * 128] = fin[k]
    scan(S5_ROWS_C, DEC_SEQ // S5_CHUNK, DEC_BATCH,
         tuple(h0_ref[:, k * 128:(k + 1) * 128] for k in range(ncol)))
    fwd = lax.broadcasted_iota(jnp.int32, (ncol, S5_ROWS, 128), 2) < S5_STATE
    hin_ref[...] = jnp.where(fwd, hf[...], hb[...]).astype(BF16)


def _s5_scan(sloc, a16, h0l):
    ncol = S5_SCAN_COLS
    w = ncol * 128
    return pl.pallas_call(
        _s5_scan_kernel,
        grid=(S5_STATE_COLS // ncol,),
        in_specs=[
            pl.BlockSpec((ncol, S5_ROWS, 128), lambda k: (k, 0, 0)),
            pl.BlockSpec((1, w), lambda k: (0, k)),
            pl.BlockSpec((DEC_BATCH, w), lambda k: (0, k)),
        ],
        out_specs=[
            pl.BlockSpec((ncol, S5_ROWS, 128), lambda k: (k, 0, 0)),
            pl.BlockSpec((BATCH, w), lambda k: (0, k)),
        ],
        out_shape=[
            jax.ShapeDtypeStruct((S5_STATE_COLS, S5_ROWS, 128), BF16),
            jax.ShapeDtypeStruct((BATCH, S5_STATE_COLS * 128), F32),
        ],
        scratch_shapes=[pltpu.VMEM((ncol, S5_ROWS, 128), F32)] * 2,
        name="s5_scan",
    )(sloc, a16, h0l)


def _s5_out_kernel(u_ref, hin_ref, perm_t_ref, wt_ref, wca_ref, d_ref, y_ref):
    u = u_ref[...]
    hin = jnp.concatenate([hin_ref[k] for k in range(S5_SLAB_COLS)], axis=1).astype(BF16)
    hin = _dot(perm_t_ref[...], hin).astype(BF16)
    y = _dot(u.astype(BF16), wt_ref[...]) + _dot(hin, wca_ref[...]) + u * d_ref[...]
    for t in range(S5_CHUNK):
        y_ref[pl.ds(t, S5_ROW_TILE, stride=S5_CHUNK), :] = y[:, t * 128:(t + 1) * 128]


def _s5_out(uj, hin, wt, wca, dj):
    _, perm_t = _s5_row_perms()
    return pl.pallas_call(
        _s5_out_kernel,
        grid=(S5_SLABS, S5_ROWS // S5_ROW_TILE),
        in_specs=[
            pl.BlockSpec((None, S5_ROW_TILE, S5_SLAB_W), lambda j, p: (j, p, 0)),
            pl.BlockSpec((S5_SLAB_COLS, S5_ROW_TILE, 128), lambda j, p: (j, p, 0)),
            pl.BlockSpec((None, S5_ROW_TILE, S5_ROW_TILE), lambda j, p: (p, 0, 0)),
            pl.BlockSpec((None, S5_SLAB_W, S5_SLAB_W), lambda j, p: (j, 0, 0)),
            pl.BlockSpec((None, S5_SLAB_W, S5_SLAB_W), lambda j, p: (j, 0, 0)),
            pl.BlockSpec((None, 1, S5_SLAB_W), lambda j, p: (j, 0, 0)),
        ],
        out_specs=pl.BlockSpec((None, S5_ROW_TILE * S5_CHUNK, 128), lambda j, p: (j, p, 0)),
        out_shape=jax.ShapeDtypeStruct((S5_SLABS, NTOK, 128), F32),
        compiler_params=pltpu.CompilerParams(vmem_limit_bytes=VMEM_LIMIT),
        name="s5_out",
    )(uj, hin, jnp.asarray(perm_t, BF16), wt, wca, dj)


@functools.lru_cache(maxsize=None)
def _gla_consts():
    n = GLA_BLK
    nl = GLA_LEVELS
    r = np.arange(n)
    up = np.zeros((n, 128), np.int32)
    for l in range(nl):
        up[:, l] = (r >> l) & 1
    i = r[:, None]
    j = r[None, :]
    x = np.maximum(i ^ j, 1)
    lev = np.where(j < i, np.floor(np.log2(x)).astype(np.int32), np.where(i == j, nl, -1)).astype(np.int32)
    up2 = np.stack([up, up[::-1]])
    h = n // 2

    def tiled(a):
        return np.stack([np.concatenate([a[:h, :h], a[h:, h:]]), np.concatenate([a[:h, h:], a[h:, :h]])])

    lev2 = np.stack([tiled(lev), tiled(lev[::-1, ::-1])])
    return up2, lev2


@functools.lru_cache(maxsize=None)
def _gla_tables():
    rowblk, seq, first, last = [], [], [], []
    for d in range(2):
        rb, sq, fi, la = [], [], [], []
        for s in range(BATCH + DEC_BATCH):
            nblk = 1 if s < BATCH else DEC_SEQ // GLA_BLK
            base = s if s < BATCH else NTOK_C // GLA_BLK + (s - BATCH) * nblk
            order = range(nblk) if d == 0 else range(nblk - 1, -1, -1)
            for pos, b in enumerate(order):
                rb.append(base + b)
                sq.append(s)
                fi.append(int(pos == 0))
                la.append(int(pos == nblk - 1))
        rowblk.append(rb); seq.append(sq); first.append(fi); last.append(la)
    as_np = lambda a: np.asarray(a, np.int32)
    return as_np(rowblk), as_np(seq), as_np(first), as_np(last)


def _gla_kernel(rowblk_ref, seq_ref, first_ref, last_ref,
                qf_ref, kf_ref, vf_ref, lrf_ref, qb_ref, kb_ref, vb_ref, lrb_ref,
                wgk_ref, bgk_ref, up_ref, lev_ref, s0_ref,
                of_ref, ob_ref, fin_ref, z_scr, st_scr):
    del rowblk_ref
    n = pl.program_id(0)

    @pl.when(first_ref[n] == 1)
    def _():
        latent = seq_ref[n] >= BATCH
        st_scr[...] = jnp.zeros_like(st_scr)
        for d in range(2):
            for h in range(GLA_HEADS):
                st_scr[d, h * GLA_DK:(h + 1) * GLA_DK, h * GLA_DV:(h + 1) * GLA_DV] = jnp.where(
                    latent, s0_ref[d, h], 0.0)

    blocks = [
        _gla_block(False, qf_ref, kf_ref, vf_ref, lrf_ref, wgk_ref.at[0], bgk_ref.at[0], up_ref.at[0],
                   lev_ref.at[0], of_ref, z_scr.at[0], st_scr.at[0]),
        _gla_block(True, qb_ref, kb_ref, vb_ref, lrb_ref, wgk_ref.at[1], bgk_ref.at[1], up_ref.at[1],
                   lev_ref.at[1], ob_ref, z_scr.at[1], st_scr.at[1]),
    ]
    for stage in range(2):
        for block in blocks:
            next(block, None)

    @pl.when(last_ref[n] == 1)
    def _():
        for d in range(2):
            for h in range(GLA_HEADS):
                fin_ref[d, h] = st_scr[d, h * GLA_DK:(h + 1) * GLA_DK, h * GLA_DV:(h + 1) * GLA_DV]


def _gla_block(backward, q_ref, k_ref, v_ref, lr_ref, wgk_ref, bgk_ref, up_ref, lev_ref, o_ref, z_scr, st_scr):
    nl = GLA_LEVELS
    blk = GLA_BLK
    q = q_ref[...] * (GLA_DK ** -0.5)
    k = k_ref[...]
    vb = v_ref[...].astype(BF16)
    x = _dot(lr_ref[...].astype(BF16), wgk_ref[...]) + bgk_ref[...]
    gk = (jnp.minimum(x, 0.0) - jnp.log(1.0 + jnp.exp(-jnp.abs(x)))) * (1.0 / GLA_NORMALIZER)
    g_hi = gk.astype(BF16)
    g_lo = (gk - g_hi.astype(F32)).astype(BF16)
    ones = jnp.ones((blk, 128), BF16)
    tot = _dot_tn(g_hi, ones) + _dot_tn(g_lo, ones)

    row = lax.broadcasted_iota(jnp.int32, (blk, 1), 0)

    def sibling(a, l):
        g = 1 << l
        if g < 8:
            a3 = a.reshape(blk // 8, 8, a.shape[-1])
            dn = pltpu.roll(a3, g, 1).reshape(a.shape)
            up_ = pltpu.roll(a3, 8 - g, 1).reshape(a.shape)
            return jnp.where(((row >> l) & 1) == 1, dn, up_)
        a4 = a.reshape(blk // (2 * g), 2, g, a.shape[-1])
        return jnp.concatenate([a4[:, 1:2], a4[:, 0:1]], axis=1).reshape(a.shape)

    part = gk
    total = gk
    z0 = None
    for l in range(nl):
        g = 1 << l
        if g < 8:
            up = up_ref[:, l:l + 1] != 0
            z = jnp.where(up, q, k) * jnp.exp(jnp.where(up, part, total - part))
            other = sibling(total, l)
            part = part + jnp.where(up, other, 0.0)
            total = total + other
        else:
            halves = lambda a: (a.reshape(blk // (2 * g), 2, g, a.shape[-1])[:, 1 - int(backward)],
                                a.reshape(blk // (2 * g), 2, g, a.shape[-1])[:, int(backward)])
            join = lambda u, d: jnp.stack([d, u] if not backward else [u, d], axis=1).reshape(blk, u.shape[-1])
            part_u, part_d = halves(part)
            tot_u, tot_d = halves(total)
            q_u, _ = halves(q)
            _, k_d = halves(k)
            z = join(q_u * jnp.exp(part_u), k_d * jnp.exp(tot_d - part_d))
            part = join(part_u + tot_d, part_d)
            both = tot_u + tot_d
            total = join(both, both)
        if l == 0:
            z0 = z
        else:
            z_scr[l] = z.astype(BF16)
    yield
    lane128 = lax.broadcasted_iota(jnp.int32, (GLA_QK, 128), 1)
    dim = lax.broadcasted_iota(jnp.int32, (GLA_QK, 128), 0)
    head_sum = ((dim >> 6) == lane128).astype(BF16)
    pair0 = _dot((z0 * sibling(z0, 0)).astype(BF16), head_sum)
    diag = _dot((q * k).astype(BF16), head_sum)

    half = blk // 2
    lev_d = lev_ref[0]
    lev_o = lev_ref[1]
    lane = lax.broadcasted_iota(jnp.int32, (half, GLA_QK), 1)

    def tiles(l, in_head, crossed):
        out = []
        for r in range(2):
            c = 1 - r if crossed else r
            lhs = z_scr[l, r * half:(r + 1) * half, :]
            keys = z_scr[l, c * half:(c + 1) * half, :]
            out.append(_dot_nt(lhs, jnp.where(in_head, keys, jnp.zeros_like(keys))))
        return jnp.concatenate(out, axis=0)

    upi = 0 if backward else 1
    key_lanes = {}
    for l in range(3, nl - 1):
        g = 1 << l
        c = lax.broadcasted_iota(jnp.int32, (blk // (2 * g), g, 128), 0)
        ln = lax.broadcasted_iota(jnp.int32, (blk // (2 * g), g, 128), 2)
        base = (2 * g * c + (g if backward else 0)) & 127
        key_lanes[l] = (ln >= base) & (ln < base + g)

    heads = range(GLA_HEADS)
    in_head = [(lane >= h * GLA_DK) & (lane < (h + 1) * GLA_DK) for h in heads]
    acc = [jnp.where(lev_d == 0, pair0[:, h:h + 1], jnp.where(lev_d == nl, diag[:, h:h + 1], 0.0)) for h in heads]
    for l in range(1, 3):
        acc = [jnp.where(lev_d == l, tiles(l, in_head[h], False), acc[h]) for h in heads]
    for l in range(3, nl - 1):
        g = 1 << l
        for h in heads:
            acc4 = acc[h].reshape(blk // (2 * g), 2, g, 128)
            s4 = tiles(l, in_head[h], False).reshape(blk // (2 * g), 2, g, 128)
            new_up = jnp.where(key_lanes[l], s4[:, upi], acc4[:, upi])
            pieces = [acc4[:, 0], new_up] if upi == 1 else [new_up, acc4[:, 1]]
            acc[h] = jnp.stack(pieces, axis=1).reshape(blk, 128)
    off = [jnp.where(lev_o == nl - 1, tiles(nl - 1, in_head[h], True), 0.0) for h in heads]
    for h in heads:
        att = jnp.concatenate([jnp.concatenate([acc[h][:half], off[h][:half]], axis=1),
                               jnp.concatenate([off[h][half:], acc[h][half:]], axis=1)], axis=0)
        o_ref[:, h * GLA_DV:(h + 1) * GLA_DV] = _dot(att.astype(BF16), vb[:, h * GLA_DV:(h + 1) * GLA_DV])

    st = st_scr[...]
    q_in = (q * jnp.exp(part)).astype(BF16)
    o_ref[...] += _dot(q_in, st.astype(BF16))
    k_out = (k * jnp.exp(total - part)).astype(BF16)
    kv = _dot_tn(k_out, vb)
    row = lax.broadcasted_iota(jnp.int32, (GLA_QK, GLA_V), 0)
    col = lax.broadcasted_iota(jnp.int32, (GLA_QK, GLA_V), 1)
    same_head = (row >> 6) == (col >> 7)
    decay = jnp.exp(tot)
    decay = jnp.concatenate([decay] * GLA_HEADS, axis=1)
    st_new = decay * st + jnp.where(same_head, kv, 0.0)
    st_scr[...] = st_new


def _gla_mix(bslab, lr, wgk, bgk, state_gla, layer):
    up, lev = _gla_consts()
    rowblk, seq, first, last = _gla_tables()
    nsteps = rowblk.shape[1]
    nseq = BATCH + DEC_BATCH
    nl = GLA_LEVELS
    whole = lambda shape: pl.BlockSpec(shape, lambda n, rb, sq, fi, la: (0,) * len(shape))

    def token_specs(d):
        return [
            pl.BlockSpec((GLA_BLK, GLA_QK), lambda n, rb, sq, fi, la: (rb[d, n], 0)),
            pl.BlockSpec((GLA_BLK, GLA_QK), lambda n, rb, sq, fi, la: (rb[d, n], 1)),
            pl.BlockSpec((GLA_BLK, GLA_V), lambda n, rb, sq, fi, la: (rb[d, n], 1)),
            pl.BlockSpec((GLA_BLK, 128), lambda n, rb, sq, fi, la: (rb[d, n], 0)),
        ]

    state_spec = pl.BlockSpec((None, 2, GLA_HEADS, GLA_DK, GLA_DV), lambda n, rb, sq, fi, la: (sq[n], 0, 0, 0, 0))
    of_layer = lambda shape: pl.BlockSpec((None,) + shape, lambda n, rb, sq, fi, la: (layer,) + (0,) * len(shape))
    grid_spec = pltpu.PrefetchScalarGridSpec(
        num_scalar_prefetch=4,
        grid=(nsteps,),
        in_specs=token_specs(0) + token_specs(1) + [
            of_layer((2, 128, GLA_QK)),
            of_layer((2, 1, GLA_QK)),
            whole((2, GLA_BLK, 128)),
            whole((2, 2, GLA_BLK, GLA_BLK // 2)),
            pl.BlockSpec((None, None, 2, GLA_HEADS, GLA_DK, GLA_DV),
                         lambda n, rb, sq, fi, la: (jnp.maximum(sq[n] - BATCH, 0), layer, 0, 0, 0, 0)),
        ],
        out_specs=[
            pl.BlockSpec((GLA_BLK, GLA_V), lambda n, rb, sq, fi, la: (rb[0, n], 0)),
            pl.BlockSpec((GLA_BLK, GLA_V), lambda n, rb, sq, fi, la: (rb[1, n], 0)),
            state_spec,
        ],
        scratch_shapes=[
            pltpu.VMEM((2, nl, GLA_BLK, GLA_QK), BF16),
            pltpu.VMEM((2, GLA_QK, GLA_V), F32),
        ],
    )
    return pl.pallas_call(
        _gla_kernel,
        grid_spec=grid_spec,
        out_shape=[
            jax.ShapeDtypeStruct((NTOK, GLA_V), F32),
            jax.ShapeDtypeStruct((NTOK, GLA_V), F32),
            jax.ShapeDtypeStruct((nseq, 2, GLA_HEADS, GLA_DK, GLA_DV), F32),
        ],
        compiler_params=pltpu.CompilerParams(vmem_limit_bytes=VMEM_LIMIT),
        name="gla_mix",
    )(jnp.asarray(rowblk), jnp.asarray(seq[0]), jnp.asarray(first[0]), jnp.asarray(last[0]),
      bslab, bslab, bslab, lr, bslab, bslab, bslab, lr, wgk, bgk, jnp.asarray(up), jnp.asarray(lev), state_gla)


def _attn_ctx_kernel(sink_ref, q_ref, k_ref, v_ref, o_ref, ko_ref, vo_ref, *, layer):
    k = k_ref[...]
    v = v_ref[...]
    ko_ref[...] = k
    vo_ref[...] = v
    ks = (k.astype(BF16), pltpu.roll(k, 64, 1).astype(BF16))
    vs = (v.astype(BF16), pltpu.roll(v, 64, 1).astype(BF16))
    lo = lax.broadcasted_iota(jnp.int32, (SEQ, 128), 1) < HEAD_DIM
    units = []
    for t in range(ATT_HEADS // 2):
        qt = q_ref[:, t * 128:(t + 1) * 128] * (HEAD_DIM ** -0.5)
        for p in range(2):
            qm = jnp.where(lo if p == 0 else jnp.logical_not(lo), qt, 0.0).astype(BF16)
            units.append((qm, 0 if p == t // 2 else 1, sink_ref[layer, 2 * t + p]))
    scores = [_dot_nt(qm, ks[which]) for qm, which, _ in units]
    maxes = [jnp.maximum(sink, jnp.max(s, axis=-1, keepdims=True)) for s, (_, _, sink) in zip(scores, units)]
    probs = [jnp.exp(s - m) for s, m in zip(scores, maxes)]
    dens = [jnp.exp(sink - m) + jnp.sum(p, axis=-1, keepdims=True)
            for p, m, (_, _, sink) in zip(probs, maxes, units)]
    outs = [_dot(p.astype(BF16), vs[which]) / den for p, den, (_, which, _) in zip(probs, dens, units)]
    for t in range(ATT_HEADS // 2):
        o_ref[:, t * 128:(t + 1) * 128] = jnp.where(lo, outs[2 * t], outs[2 * t + 1]).astype(BF16)


def _attn_ctx(sink, cslab, layer):
    kv_out = pl.BlockSpec((None, SEQ, ATT_KV), lambda b: (b, 0, 0))
    return pl.pallas_call(
        functools.partial(_attn_ctx_kernel, layer=layer),
        grid=(BATCH,),
        in_specs=[
            pl.BlockSpec(memory_space=pltpu.SMEM),
            pl.BlockSpec((SEQ, ATT_Q), lambda b: (b, 0)),
            pl.BlockSpec((SEQ, ATT_KV), lambda b: (b, 4)),
            pl.BlockSpec((SEQ, ATT_KV), lambda b: (b, 5)),
        ],
        out_specs=[pl.BlockSpec((SEQ, ATT_Q), lambda b: (b, 0)), kv_out, kv_out],
        out_shape=[jax.ShapeDtypeStruct((NTOK_C, ATT_Q), BF16),
                   jax.ShapeDtypeStruct((BATCH, SEQ, ATT_KV), F32),
                   jax.ShapeDtypeStruct((BATCH, SEQ, ATT_KV), F32)],
        name="attn_ctx",
    )(sink, cslab, cslab, cslab)


def _attn_lat_kernel(sink_ref, q_ref, kp_ref, kc_ref, kn_ref, vp_ref, vc_ref, vn_ref,
                     ck_ref, cv_ref, cos_ref, sin_ref, bias_ref, o_ref, *, layer):
    j = pl.program_id(1)
    nb = DEC_SEQ // ATT_BLOCK
    lane = lax.broadcasted_iota(jnp.int32, (ATT_BLOCK, 128), 1)
    lo = lane < HEAD_DIM
    first16 = (lane & 31) < 16

    def rope(x, blk_idx):
        r0 = pl.multiple_of(blk_idx * ATT_BLOCK, ATT_BLOCK)
        c = cos_ref[pl.ds(r0, ATT_BLOCK), :]
        s = sin_ref[pl.ds(r0, ATT_BLOCK), :]
        xs = jnp.where(first16, pltpu.roll(x, 112, 1), pltpu.roll(x, 16, 1))
        return x * c + xs * s

    nwin = 3 * ATT_BLOCK
    keys = jnp.concatenate([rope(kp_ref[...], jnp.maximum(j - 1, 0)), rope(kc_ref[...], j),
                            rope(kn_ref[...], jnp.minimum(j + 1, nb - 1)), ck_ref[...]], axis=0)
    vals = jnp.concatenate([vp_ref[...], vc_ref[...], vn_ref[...], cv_ref[...]], axis=0)
    keys2 = (keys.astype(BF16), pltpu.roll(keys, 64, 1).astype(BF16))
    vals2 = (vals.astype(BF16), pltpu.roll(vals, 64, 1).astype(BF16))
    kcol = lax.broadcasted_iota(jnp.int32, (1, nwin + PAST_LEN), 1)
    edge = jnp.where(((j == 0) & (kcol < ATT_BLOCK)) | ((j == nb - 1) & (kcol >= 2 * ATT_BLOCK) & (kcol < nwin)),
                     -1e30, 0.0)
    bias = bias_ref[...] + edge
    top = lax.broadcasted_iota(jnp.int32, (2 * ATT_BLOCK, 1), 0) < ATT_BLOCK
    q_tiles = [rope(q_ref[:, t * 128:(t + 1) * 128], j) * (HEAD_DIM ** -0.5) for t in range(ATT_HEADS // 2)]
    lo2 = jnp.concatenate([lo, lo], axis=0)
    units = []
    for kvh in range(ATT_KV_HEADS):
        q2 = jnp.concatenate(q_tiles[2 * kvh:2 * kvh + 2], axis=0)
        for p in range(2):
            qm = jnp.where(lo2 if p == 0 else jnp.logical_not(lo2), q2, 0.0).astype(BF16)
            sink = jnp.where(top, sink_ref[layer, 4 * kvh + p], sink_ref[layer, 4 * kvh + 2 + p])
            units.append((qm, 0 if p == kvh else 1, sink))
    scores = [_dot_nt(qm, keys2[which]) + bias for qm, which, _ in units]
    maxes = [jnp.maximum(sink, jnp.max(s, axis=-1, keepdims=True)) for s, (_, _, sink) in zip(scores, units)]
    probs = [jnp.exp(s - m) for s, m in zip(scores, maxes)]
    dens = [jnp.exp(sink - m) + jnp.sum(p, axis=-1, keepdims=True)
            for p, m, (_, _, sink) in zip(probs, maxes, units)]
    outs = [_dot(p.astype(BF16), vals2[which]) / den for p, den, (_, which, _) in zip(probs, dens, units)]
    for kvh in range(ATT_KV_HEADS):
        o2 = jnp.where(lo2, outs[2 * kvh], outs[2 * kvh + 1])
        for i in range(2):
            t = 2 * kvh + i
            o_ref[:, t * 128:(t + 1) * 128] = o2[i * ATT_BLOCK:(i + 1) * ATT_BLOCK].astype(BF16)


def _attn_lat(sink, cslab, ck, cv, cos_t, sin_t, layer):
    nb = DEC_SEQ // ATT_BLOCK
    base = NTOK_C // ATT_BLOCK
    cur = lambda b, j: base + b * nb + j
    prv = lambda b, j: base + b * nb + jnp.maximum(j - 1, 0)
    nxt = lambda b, j: base + b * nb + jnp.minimum(j + 1, nb - 1)
    kv_spec = lambda row, col: pl.BlockSpec((ATT_BLOCK, ATT_KV), lambda b, j: (row(b, j), col))
    qi = np.arange(2 * ATT_BLOCK)[:, None] % ATT_BLOCK
    kc = np.arange(3 * ATT_BLOCK + PAST_LEN)[None, :]
    inside = (np.abs(kc - ATT_BLOCK - qi) <= WINDOW) | (kc >= 3 * ATT_BLOCK)
    band = np.where(inside, 0.0, -1e30).astype(np.float32)
    cache_spec = pl.BlockSpec((None, None, PAST_LEN, ATT_KV), lambda b, j: (b, layer, 0, 0))
    return pl.pallas_call(
        functools.partial(_attn_lat_kernel, layer=layer),
        grid=(DEC_BATCH, nb),
        in_specs=[
            pl.BlockSpec(memory_space=pltpu.SMEM),
            pl.BlockSpec((ATT_BLOCK, ATT_Q), lambda b, j: (cur(b, j), 0)),
            kv_spec(prv, 4), kv_spec(cur, 4), kv_spec(nxt, 4),
            kv_spec(prv, 5), kv_spec(cur, 5), kv_spec(nxt, 5),
            cache_spec, cache_spec,
            pl.BlockSpec((DEC_SEQ, 128), lambda b, j: (0, 0)),
            pl.BlockSpec((DEC_SEQ, 128), lambda b, j: (0, 0)),
            pl.BlockSpec(band.shape, lambda b, j: (0, 0)),
        ],
        out_specs=pl.BlockSpec((ATT_BLOCK, ATT_Q), lambda b, j: (b * nb + j, 0)),
        out_shape=jax.ShapeDtypeStruct((NTOK_L, ATT_Q), BF16),
        name="attn_lat",
    )(sink, cslab, cslab, cslab, cslab, cslab, cslab, cslab, ck, cv, cos_t, sin_t, jnp.asarray(band))


@functools.lru_cache(maxsize=None)
def _rope_tables():
    rows = DEC_SEQ // GRID_W
    row = np.repeat(np.arange(rows, dtype=np.float64), GRID_W)
    col = np.tile(np.arange(GRID_W, dtype=np.float64), rows)
    quarter = HEAD_DIM // 4
    inv = ROPE_BASE ** (-np.arange(quarter, dtype=np.float64) / quarter)
    lane = np.arange(128)
    use_row = (lane % HEAD_DIM) < HEAD_DIM // 2
    pos = np.where(use_row[None, :], row[:, None], col[:, None])
    ang = pos * inv[lane % quarter][None, :]
    sign = np.where((lane % 32) < 16, -1.0, 1.0)
    return np.cos(ang).astype(np.float32), (np.sin(ang) * sign[None, :]).astype(np.float32)


def _merge_kernel(*refs, split_x):
    if split_x:
        xc_ref, xl_ref, *refs = refs
    else:
        xc_ref, *refs = refs
    (mod_ref, g_ref, ys5_ref, ogf_ref, ogb_ref, gb_ref, ycc_ref, ycl_ref, gate_ref, gng_ref,
     wglu_ref, wbr_ref, wout_ref, o_ref) = refs
    is_ctx = pl.program_id(0) < NTOK_C // TM
    if split_x:
        x = jnp.where(is_ctx, xc_ref[...], xl_ref[...])
    else:
        x = xc_ref[...]
    y = jnp.concatenate([ys5_ref[j] for j in range(S5_SLABS)], axis=1)
    y = 0.5 * y * (1.0 + jnp.tanh(math.sqrt(2.0 / math.pi) * (y + 0.044715 * (y * y * y))))
    ag = _dot(y.astype(BF16), wglu_ref[...])
    y_a = ag[:, :S5_WIDTH] * _sigmoid(ag[:, S5_WIDTH:])
    gng = gng_ref[...]
    parts = []
    for h in range(GLA_HEADS):
        sl = slice(h * GLA_DV, (h + 1) * GLA_DV)
        o = ogf_ref[:, sl] + ogb_ref[:, sl]
        g = gb_ref[:, sl]
        parts.append(_rms(o, gng) * (g * _sigmoid(g)))
    y_b = jnp.concatenate(parts, axis=1)
    y_c = jnp.where(is_ctx, ycc_ref[...], ycl_ref[...])
    merged = None
    for n, yn in enumerate((y_a, y_b, y_c)):
        proj = _dot(yn.astype(BF16), wbr_ref[n])
        term = gate_ref[:, n * D_MODEL:(n + 1) * D_MODEL].astype(F32) * proj
        merged = term if merged is None else merged + term
    mixed = _dot(merged.astype(BF16), wout_ref[...])
    g1 = mod_ref[:, 2 * D_MODEL:3 * D_MODEL]
    o_ref[...] = x + g1 * _rms(mixed, g_ref[...])


def _layer_spec(shape, layer):
    return pl.BlockSpec((None,) + shape, lambda i: (layer,) + (0,) * len(shape), pipeline_mode=pl.Buffered(1))


def _split_token_specs(n_arrays, width=D_MODEL):
    nct = NTOK_C // TM
    if n_arrays == 2:
        return [pl.BlockSpec((TM, width), lambda i: (jnp.minimum(i, nct - 1), 0)),
                pl.BlockSpec((TM, width), lambda i: (jnp.maximum(i - nct, 0), 0))]
    return [pl.BlockSpec((TM, width), lambda i: (i, 0))]


def _merge(xs, mod, g, ys5, og, bslab, yc, gates, gng, wglu, wbr, wout, layer):
    tok = lambda width, col=0: pl.BlockSpec((TM, width), lambda i: (i, col))
    full = lambda shape: _layer_spec(shape, layer)
    return pl.pallas_call(
        functools.partial(_merge_kernel, split_x=len(xs) == 2),
        grid=(NTOK // TM,),
        in_specs=_split_token_specs(len(xs)) + [
            _mod_spec(layer),
            _gain_spec(layer, 1),
            pl.BlockSpec((S5_SLABS, TM, 128), lambda i: (0, _s5_tile(i), 0)),
            tok(GLA_V),
            tok(GLA_V),
            tok(GLA_V, 2),
        ] + _split_token_specs(2, ATT_Q) + [
            tok(N_BRANCH * D_MODEL),
            pl.BlockSpec((None, 1, GLA_DV), lambda i: (layer, 0, 0)),
            full((S5_WIDTH, 2 * S5_WIDTH)),
            full((N_BRANCH, BRANCH_W, D_MODEL)),
            full((D_MODEL, D_MODEL)),
        ],
        out_specs=tok(D_MODEL),
        out_shape=jax.ShapeDtypeStruct((NTOK, D_MODEL), F32),
        compiler_params=pltpu.CompilerParams(vmem_limit_bytes=VMEM_LIMIT),
        name="merge",
    )(*xs, mod, g, ys5, *og, bslab, *yc, gates, gng, wglu, wbr, wout)


FFN_SPLIT = 2


def _ffn_kernel(x_ref, mod_ref, gin_ref, gout_ref, w1_ref, w2_ref, *o_refs):
    x = x_ref[...]
    sh = mod_ref[:, 3 * D_MODEL:4 * D_MODEL]
    sc = mod_ref[:, 4 * D_MODEL:5 * D_MODEL]
    g2 = mod_ref[:, 5 * D_MODEL:6 * D_MODEL]
    h = (_rms(x, gin_ref[...]) * (1.0 + sc) + sh).astype(BF16)
    ck = FFN_HIDDEN // FFN_SPLIT
    acc = None
    for c in range(FFN_SPLIT):
        a = _dot(h, w1_ref[:, c * ck:(c + 1) * ck])
        b = _dot(h, w1_ref[:, FFN_HIDDEN + c * ck:FFN_HIDDEN + (c + 1) * ck])
        act = (a * _sigmoid(a) * b).astype(BF16)
        part = _dot(act, w2_ref[c * ck:(c + 1) * ck, :])
        acc = part if acc is None else acc + part
    y = x + g2 * _rms(acc, gout_ref[...])
    if len(o_refs) == 1:
        o_refs[0][...] = y
    else:
        is_ctx = pl.program_id(0) < NTOK_C // TM

        @pl.when(is_ctx)
        def _():
            o_refs[0][...] = y

        @pl.when(jnp.logical_not(is_ctx))
        def _():
            o_refs[1][...] = y


def _ffn(x, mod, gains, w1, w2, layer, split_out):
    nct = NTOK_C // TM
    if split_out:
        out_specs = [pl.BlockSpec((TM, D_MODEL), lambda i: (jnp.minimum(i, nct - 1), 0)),
                     pl.BlockSpec((TM, D_MODEL), lambda i: (jnp.maximum(i - nct, 0), 0))]
        out_shape = [jax.ShapeDtypeStruct((NTOK_C, D_MODEL), F32), jax.ShapeDtypeStruct((NTOK_L, D_MODEL), F32)]
    else:
        out_specs = pl.BlockSpec((TM, D_MODEL), lambda i: (i, 0))
        out_shape = jax.ShapeDtypeStruct((NTOK, D_MODEL), F32)
    return pl.pallas_call(
        _ffn_kernel,
        grid=(NTOK // TM,),
        in_specs=[
            pl.BlockSpec((TM, D_MODEL), lambda i: (i, 0)),
            _mod_spec(layer),
            _gain_spec(layer, 2),
            _gain_spec(layer, 3),
            _layer_spec((D_MODEL, 2 * FFN_HIDDEN), layer),
            _layer_spec((FFN_HIDDEN, D_MODEL), layer),
        ],
        out_specs=out_specs,
        out_shape=out_shape,
        compiler_params=pltpu.CompilerParams(vmem_limit_bytes=VMEM_LIMIT),
        name="ffn",
    )(x, mod, gains, gains, w1, w2)


def kernel(x_prompt, x_sample, cache_k, cache_v, state_s5, state_gla, c, c_ctx, w_mod, b_mod, norm_g, w_in,
           s5_lam_re, s5_lam_im, s5_log_step, s5_b_re, s5_b_im, s5_c_re, s5_c_im, s5_d, w_glu, gla_w_gk,
           gla_b_gk, gla_norm_g, att_sink, w_branch, w_out, w_ffn_in, w_ffn_out):
    cond = jnp.concatenate([c_ctx[None, :], c, jnp.zeros((N_MOD_ROWS - 1 - DEC_BATCH, D_MODEL), F32)], axis=0)
    mod_all = _modulation(cond, w_mod, b_mod).reshape(DEPTH, N_MOD_ROWS, 1, 6 * D_MODEL)
    cos_t, sin_t = _rope_tables()
    xs = (x_prompt.reshape(NTOK_C, D_MODEL), x_sample.reshape(NTOK_L, D_MODEL))
    w_in_b = w_in.astype(BF16)
    w_in_end = jnp.pad(w_in[:, :, D_IN_TILED:].astype(BF16), ((0, 0), (0, 0), (0, W_IN_COLS - D_IN)))
    w_glu_b, w_branch_b, w_out_b = w_glu.astype(BF16), w_branch.astype(BF16), w_out.astype(BF16)
    w_ffn_in_b, w_ffn_out_b = w_ffn_in.astype(BF16), w_ffn_out.astype(BF16)
    s5_params = _s5_params(s5_lam_re, s5_lam_im, s5_log_step, s5_b_re, s5_b_im, s5_c_re, s5_c_im)
    h0_all = state_s5.astype(F32).transpose(1, 0, 3, 5, 2, 4).reshape(DEPTH, DEC_BATCH, S5_GROUPS * 256)
    wgk_all = jnp.stack([jnp.pad(gla_w_gk[:, d], ((0, 0), (d * GLA_RANK, 128 - (d + 1) * GLA_RANK), (0, 0)))
                         for d in range(2)], axis=1).astype(BF16)
    bgk_all = gla_b_gk[:, :, None, :].astype(F32)
    gains = norm_g.astype(F32).reshape(DEPTH * 4, 1, D_MODEL)
    gla_gain = gla_norm_g.astype(F32).reshape(DEPTH, 1, GLA_DV)
    sink = att_sink.astype(F32)
    cache_k2 = cache_k.astype(F32).reshape(DEC_BATCH, DEPTH, PAST_LEN, ATT_KV)
    cache_v2 = cache_v.astype(F32).reshape(DEC_BATCH, DEPTH, PAST_LEN, ATT_KV)
    state_gla = state_gla.astype(F32)
    mod = mod_all
    new_k, new_v, new_s5, new_gla = [], [], [], []
    for i in range(DEPTH):
        uj, bslab, cslab, gates, lr = _inproj(xs, mod, gains, w_in_b, w_in_end, i)

        wt, web, wca, a16, dj = _s5_prep(s5_params, s5_d[i], i)
        hin, finc = _s5_scan(_s5_state(uj, web), a16, h0_all[i])
        ys5 = _s5_out(uj, hin, wt, wca, dj)
        new_s5.append(finc)

        *og, gla_fin = _gla_mix(bslab, lr, wgk_all, bgk_all, state_gla, i)
        new_gla.append(gla_fin[:BATCH])

        yc_ctx, k_new, v_new = _attn_ctx(sink, cslab, i)
        yc = (yc_ctx, _attn_lat(sink, cslab, cache_k2, cache_v2, cos_t, sin_t, i))
        new_k.append(k_new.reshape(BATCH, SEQ, ATT_KV_HEADS, HEAD_DIM))
        new_v.append(v_new.reshape(BATCH, SEQ, ATT_KV_HEADS, HEAD_DIM))

        x = _merge(xs, mod, gains, ys5, og, bslab, yc, gates, gla_gain, w_glu_b, w_branch_b, w_out_b, i)
        last = i == DEPTH - 1
        x = _ffn(x, mod, gains, w_ffn_in_b, w_ffn_out_b, i, last)
        xs = tuple(x) if last else (x,)

    return (xs[0].reshape(BATCH, SEQ, D_MODEL), xs[1].reshape(DEC_BATCH, DEC_SEQ, D_MODEL),
            jnp.stack(new_k, axis=1), jnp.stack(new_v, axis=1),
            jnp.stack(new_s5).reshape(DEPTH, BATCH, S5_GROUPS, 2, 2, S5_STATE).transpose(1, 0, 4, 2, 5, 3),
            jnp.stack(new_gla, axis=1))
```

```python
import functools
import math

import numpy as np
import jax
import jax.numpy as jnp
from jax import lax
from jax.experimental import pallas as pl
from jax.experimental.pallas import tpu as pltpu

F32 = jnp.float32
BF16 = jnp.bfloat16

D_MODEL = 1024
BATCH = 16
SEQ = 256
DEPTH = 2
DEC_BATCH = 8
DEC_SEQ = 1024
PAST_LEN = 256
GRID_W = 64
ROPE_BASE = 10000.0
S5_WIDTH = 512
S5_GROUP = 16
S5_GROUPS = 32
S5_STATE = 64
GLA_HEADS = 4
GLA_DK = 64
GLA_DV = 128
GLA_QK = 256
GLA_V = 512
GLA_RANK = 16
GLA_NORMALIZER = 16.0
ATT_HEADS = 8
ATT_KV_HEADS = 2
HEAD_DIM = 64
ATT_Q = 512
ATT_KV = 128
WINDOW = 128
ATT_BLOCK = 128
N_BRANCH = 3
BRANCH_W = 512
FFN_HIDDEN = 2816
RMS_EPS = 1e-6

NTOK_C = BATCH * SEQ
NTOK_L = DEC_BATCH * DEC_SEQ
NTOK = NTOK_C + NTOK_L
TM = 512
N_MOD_ROWS = 16

D_IN = 5920
D_IN_TILED = D_IN // 128 * 128
W_IN_COLS = 6016
S5_CHUNK = 16
S5_SLABS = S5_WIDTH // 128
S5_SLAB_W = S5_CHUNK * 128
S5_ROWS_C = NTOK_C // S5_CHUNK
S5_ROWS = NTOK // S5_CHUNK
S5_ROW_TILE = 256
GLA_BLK = 256
GLA_LEVELS = 8
VMEM_LIMIT = 56 * 1024 * 1024


def _dot(a, b):
    return jnp.dot(a, b, preferred_element_type=F32)


def _dot_nt(a, b):
    return lax.dot_general(a, b, (((1,), (1,)), ((), ())), preferred_element_type=F32)


def _dot_tn(a, b):
    return lax.dot_general(a, b, (((0,), (0,)), ((), ())), preferred_element_type=F32)


def _rms(x, g):
    return x * lax.rsqrt(jnp.mean(x * x, axis=-1, keepdims=True) + RMS_EPS) * g


def _sigmoid(x):
    return 0.5 * jnp.tanh(0.5 * x) + 0.5


def _mod_row(i):
    nct = NTOK_C // TM
    return jnp.where(i < nct, 0, 1 + (i - nct) // (DEC_SEQ // TM))


def _mod_spec(layer):
    return pl.BlockSpec((None, None, 1, 6 * D_MODEL), lambda i: (layer, _mod_row(i), 0, 0))


def _gain_spec(layer, k):
    return pl.BlockSpec((None, 1, D_MODEL), lambda i: (layer * 4 + k, 0, 0))


def _s5_tile(i):
    nct = NTOK_C // TM
    per_seq = DEC_SEQ // TM
    k = i - nct
    return jnp.where(i < nct, i, nct + (k % per_seq) * DEC_BATCH + k // per_seq)


def _mod_kernel(c_ref, w_ref, b_ref, o_ref):
    c = c_ref[...]
    s = (c * _sigmoid(c)).astype(BF16)
    o_ref[...] = _dot(s, w_ref[...].astype(BF16)) + b_ref[...]


def _modulation(cond, w_mod, b_mod):
    tn = 2048
    return pl.pallas_call(
        _mod_kernel,
        grid=(DEPTH, 6 * D_MODEL // tn),
        in_specs=[
            pl.BlockSpec((N_MOD_ROWS, D_MODEL), lambda l, n: (0, 0)),
            pl.BlockSpec((None, D_MODEL, tn), lambda l, n: (l, 0, n)),
            pl.BlockSpec((None, 1, tn), lambda l, n: (l, 0, n)),
        ],
        out_specs=pl.BlockSpec((None, N_MOD_ROWS, tn), lambda l, n: (l, 0, n)),
        out_shape=jax.ShapeDtypeStruct((DEPTH, N_MOD_ROWS, 6 * D_MODEL), F32),
        name="modulation",
    )(cond, w_mod, b_mod.reshape(DEPTH, 1, 6 * D_MODEL))


_IN_SLABS = ((0, 512), (512, 1536), (2048, 768), (2816, 3072), (5888, 128))
W_IN_SPLIT = 2048
W_IN_GAP = 32
W_IN_TAIL = W_IN_COLS - W_IN_SPLIT


def _inproj_kernel(*refs, split_x):
    if split_x:
        xc_ref, xl_ref, *refs = refs
    else:
        xc_ref, *refs = refs
    mod_ref, g_ref, w_ref, w_end_ref, u_ref, b_ref, c_ref, gate_ref, lr_ref, w_tail, u_stage = refs
    i = pl.program_id(0)

    @pl.when(i == 0)
    def _():
        r = lax.broadcasted_iota(jnp.int32, (256, 128), 0)
        c = lax.broadcasted_iota(jnp.int32, (256, 128), 1)
        shift = (r == c + W_IN_GAP).astype(BF16)
        head = ((r == c) & (c < W_IN_GAP)).astype(BF16)
        ntile = (W_IN_TAIL - 128) // 128
        for t in range(ntile - 1):
            src = W_IN_SPLIT + 128 * t
            w_tail[:, 128 * t:128 * (t + 1)] = _dot(w_ref[:, src:src + 256], shift).astype(BF16)
        src = W_IN_SPLIT + 128 * (ntile - 1)
        last = jnp.concatenate([w_ref[:, src:src + 128], w_end_ref[...]], axis=1)
        w_tail[:, 128 * (ntile - 1):128 * ntile] = _dot(last, shift).astype(BF16)
        w_tail[:, 128 * ntile:] = _dot(w_ref[:, W_IN_SPLIT:W_IN_SPLIT + 256], head).astype(BF16)

    if split_x:
        x = jnp.where(i < NTOK_C // TM, xc_ref[...], xl_ref[...])
    else:
        x = xc_ref[...]
    mod = mod_ref[...]
    h = _rms(x, g_ref[...]) * (1.0 + mod[:, D_MODEL:2 * D_MODEL]) + mod[:, 0:D_MODEL]
    h = h.astype(BF16)
    for j in range(S5_SLABS):
        u_stage[...] = _dot(h, w_ref[:, j * 128:(j + 1) * 128])
        for s in range(S5_CHUNK):
            u_ref[j, :, s * 128:(s + 1) * 128] = u_stage[pl.ds(s, TM // S5_CHUNK, stride=S5_CHUNK), :]
    b_ref[...] = _dot(h, w_ref[:, 512:W_IN_SPLIT])
    for (off, width), o_ref in zip(_IN_SLABS[2:], (c_ref, gate_ref, lr_ref)):
        z = _dot(h, w_tail[:, off - W_IN_SPLIT:off - W_IN_SPLIT + width])
        o_ref[...] = _sigmoid(z).astype(BF16) if o_ref is gate_ref else z


def _inproj(xs, mod, g, w_all, w_end, layer):
    return pl.pallas_call(
        functools.partial(_inproj_kernel, split_x=len(xs) == 2),
        grid=(NTOK // TM,),
        in_specs=_split_token_specs(len(xs)) + [
            _mod_spec(layer),
            _gain_spec(layer, 0),
            pl.BlockSpec((None, D_MODEL, D_IN), lambda i: (layer, 0, 0), pipeline_mode=pl.Buffered(1)),
            pl.BlockSpec((None, D_MODEL, 128), lambda i: (layer, 0, 0), pipeline_mode=pl.Buffered(1)),
        ],
        out_specs=[pl.BlockSpec((S5_SLABS, TM // S5_CHUNK, S5_SLAB_W), lambda i: (0, _s5_tile(i), 0))]
        + [pl.BlockSpec((TM, width), lambda i: (i, 0)) for _, width in _IN_SLABS[1:]],
        out_shape=[jax.ShapeDtypeStruct((S5_SLABS, S5_ROWS, S5_SLAB_W), F32)]
        + [jax.ShapeDtypeStruct((NTOK, width), BF16 if width == N_BRANCH * D_MODEL else F32)
           for _, width in _IN_SLABS[1:]],
        scratch_shapes=[pltpu.VMEM((D_MODEL, W_IN_TAIL), BF16), pltpu.VMEM((TM, 128), F32)],
        compiler_params=pltpu.CompilerParams(vmem_limit_bytes=VMEM_LIMIT),
        name="inproj",
    )(*xs, mod, g, w_all, w_end)


@functools.lru_cache(maxsize=None)
def _s5_expanders():
    seg = 8
    spread = np.zeros((seg, 256, S5_SLAB_W), np.float32)
    col = np.arange(256)
    for gl in range(seg):
        spread[gl, col, (col // S5_GROUP) * 128 + gl * S5_GROUP + col % S5_GROUP] = 1.0
    return spread


def _s5_prep_kernel(par_ref, bre_ref, bim_ref, cre_ref, cim_ref, spread_ref,
                    wt_ref, web_ref, wca_ref, a16_ref):
    n = S5_CHUNK
    lam_re = par_ref[0:1, :]
    lam_im = par_ref[1:2, :]
    dt = jnp.exp(par_ref[2:3, :])
    lr = lam_re * dt
    li = lam_im * dt
    krow = lax.broadcasted_iota(jnp.int32, (24, 128), 0).astype(F32)
    tab_mag = jnp.exp(krow * lr)
    tab_re = tab_mag * jnp.cos(krow * li)
    tab_im = tab_mag * jnp.sin(krow * li)
    ar = tab_re[1:2, :]
    ai = tab_im[1:2, :]
    nr = ar - 1.0
    den = lam_re * lam_re + lam_im * lam_im
    fr = (nr * lam_re + ai * lam_im) / den
    fi = (ai * lam_re - nr * lam_im) / den
    b_re = bre_ref[...]
    b_im = bim_ref[...]
    br = fr * b_re - fi * b_im
    bi = fr * b_im + fi * b_re
    c_re = cre_ref[...]
    c_im = cim_ref[...]

    def lo_half(shape):
        return lax.broadcasted_iota(jnp.int32, shape, 1) < S5_STATE

    def tile_rows(a):
        return jnp.concatenate([a] * n, axis=0)

    fwd16 = lo_half((S5_GROUP, 128))

    def powers(t_re, t_im, k_fwd, k_bwd):
        def pick(t, b):
            kf, kb = k_fwd(b), k_bwd(b)
            return jnp.where(fwd16, jnp.broadcast_to(t[kf:kf + 1, :], (S5_GROUP, 128)),
                             jnp.broadcast_to(t[kb:kb + 1, :], (S5_GROUP, 128)))
        return (jnp.concatenate([pick(t_re, b) for b in range(n)], axis=0),
                jnp.concatenate([pick(t_im, b) for b in range(n)], axis=0))

    fwd = lo_half((n * S5_GROUP, 128))
    brt, bit, crt, cit = tile_rows(br), tile_rows(bi), tile_rows(c_re), tile_rows(c_im)

    per, pei = powers(tab_re, tab_im, lambda s: n - 1 - s, lambda s: s)
    eb = jnp.concatenate([brt * per - bit * pei, brt * pei + bit * per], axis=1)
    pcr, pci = powers(tab_re, tab_im, lambda t: t + 1, lambda t: n - t)
    ca = jnp.concatenate([(crt * pcr - cit * pci).T, (-(crt * pci + cit * pcr)).T], axis=0)

    def one_dir(x, d):
        sw = pltpu.roll(x, S5_STATE, 1)
        lo = lo_half(x.shape)
        return jnp.where(lo, x, sw) if d == 0 else jnp.where(lo, sw, x)

    klag = []
    for d in range(2):
        lhs = jnp.where(lo_half(br.shape), one_dir(br, d), -one_dir(bi, d))
        crd, cid = tile_rows(one_dir(c_re, d)), tile_rows(one_dir(c_im, d))
        lag = (lambda b: b) if d == 0 else (lambda b: n - 1 - b)
        pr, pi = powers(one_dir(tab_re, d), one_dir(tab_im, d), lag, lag)
        rhs_t = jnp.where(fwd, crd * pr - cid * pi, crd * pi + cid * pr)
        klag.append(lax.dot_general(lhs, rhs_t, (((1,), (1,)), ((), ())),
                                    precision=lax.Precision.HIGHEST, preferred_element_type=F32))
    lane = lax.broadcasted_iota(jnp.int32, (S5_GROUP, n * S5_GROUP), 1)
    rows = []
    for s in range(n):
        f = klag[0] if s == 0 else jnp.where(lane >= S5_GROUP * s, pltpu.roll(klag[0], S5_GROUP * s, 1), 0.0)
        sh = (n * S5_GROUP - S5_GROUP * (n - 1 - s)) % (n * S5_GROUP)
        b = klag[1] if sh == 0 else pltpu.roll(klag[1], sh, 1)
        rows.append(f + jnp.where(lane < S5_GROUP * (s + 1), b, 0.0))
    toep = jnp.concatenate(rows, axis=0)

    spread = spread_ref[...]
    wt_ref[...] = _dot(toep.astype(BF16), spread).astype(BF16).reshape(n, S5_GROUP, S5_SLAB_W)
    wca_ref[...] = _dot(ca.astype(BF16), spread).astype(BF16)
    a16_ref[0:1, :] = tab_re[n:n + 1, :]
    a16_ref[1:2, :] = tab_im[n:n + 1, :]
    web_ref[...] = jnp.zeros_like(web_ref)
    eb3 = eb.astype(BF16).reshape(n, S5_GROUP, 256)
    for pos in range(S5_SLAB_W // 256):
        @pl.when(pl.program_id(0) == pos)
        def _():
            web_ref[:, :, pos * 256:(pos + 1) * 256] = eb3


def _s5_params(lam_re, lam_im, log_step, b_re, b_im, c_re, c_im):
    par = jnp.stack([lam_re, lam_im, log_step], axis=1).astype(F32)
    par = par.transpose(0, 3, 1, 2, 4).reshape(DEPTH, S5_GROUPS, 3, 128)
    par = jnp.pad(par, ((0, 0), (0, 0), (0, 5), (0, 0)))
    b_t = lambda b: b.astype(F32).transpose(0, 2, 4, 1, 3).reshape(DEPTH, S5_GROUPS, S5_GROUP, 128)
    c_t = lambda c: c.astype(F32).transpose(0, 2, 3, 1, 4).reshape(DEPTH, S5_GROUPS, S5_GROUP, 128)
    return par, b_t(b_re), b_t(b_im), c_t(c_re), c_t(c_im)


def _s5_prep(params, d_skip, layer):
    seg = 8
    spread = _s5_expanders()
    vec = pl.BlockSpec((None, None, S5_GROUP, 128), lambda gl, j: (layer, j * seg + gl, 0, 0))
    exp_spec = pl.BlockSpec((None, 256, S5_SLAB_W), lambda gl, j: (gl, 0, 0))
    rows_spec = pl.BlockSpec((None, S5_CHUNK, None, S5_GROUP, S5_SLAB_W), lambda gl, j: (j, 0, gl, 0, 0))
    wt, web, wca, a16 = pl.pallas_call(
        _s5_prep_kernel,
        grid=(seg, S5_SLABS),
        in_specs=[pl.BlockSpec((None, None, 8, 128), lambda gl, j: (layer, j * seg + gl, 0, 0)),
                  vec, vec, vec, vec, exp_spec],
        out_specs=[
            rows_spec, rows_spec,
            pl.BlockSpec((None, None, 256, S5_SLAB_W), lambda gl, j: (j, gl, 0, 0)),
            pl.BlockSpec((None, 2, 128), lambda gl, j: (j * seg + gl, 0, 0)),
        ],
        out_shape=[
            jax.ShapeDtypeStruct((S5_SLABS, S5_CHUNK, seg, S5_GROUP, S5_SLAB_W), BF16),
            jax.ShapeDtypeStruct((S5_SLABS, S5_CHUNK, seg, S5_GROUP, S5_SLAB_W), BF16),
            jax.ShapeDtypeStruct((S5_SLABS, seg, 256, S5_SLAB_W), BF16),
            jax.ShapeDtypeStruct((S5_GROUPS, 2, 128), F32),
        ],
        name="s5_prep",
    )(*params, jnp.asarray(spread, BF16))
    mat = (S5_SLABS, S5_SLAB_W, S5_SLAB_W)
    dj = jnp.tile(d_skip.astype(F32).reshape(S5_SLABS, 1, 128), (1, 1, S5_CHUNK))
    return wt.reshape(mat), web.reshape(mat), wca.reshape(mat), a16.reshape(1, S5_SLABS * S5_SLAB_W), dj


S5_STATE_COLS = S5_SLABS * S5_SLAB_W // 128
S5_SLAB_COLS = S5_SLAB_W // 128


@functools.lru_cache(maxsize=None)
def _s5_row_perms():
    assert S5_ROW_TILE == S5_ROWS_C == DEC_BATCH * TM // S5_CHUNK
    perm = np.zeros((3, S5_ROW_TILE, S5_ROW_TILE), np.float32)
    for p, (nseq, nchunk) in enumerate(((BATCH, SEQ // S5_CHUNK), (DEC_BATCH, TM // S5_CHUNK),
                                        (DEC_BATCH, TM // S5_CHUNK))):
        b, c = np.meshgrid(np.arange(nseq), np.arange(nchunk), indexing="ij")
        perm[p, (c * nseq + b).ravel(), (b * nchunk + c).ravel()] = 1.0
    return perm, perm.transpose(0, 2, 1).copy()


def _s5_state_kernel(u_ref, perm_ref, w_ref, o_ref):
    u = _dot(perm_ref[...], u_ref[...].astype(BF16)).astype(BF16)
    s = _dot(u, w_ref[...])
    for k in range(S5_SLAB_COLS):
        o_ref[k] = s[:, k * 128:(k + 1) * 128]


def _s5_state(uj, web):
    perm, _ = _s5_row_perms()
    return pl.pallas_call(
        _s5_state_kernel,
        grid=(S5_SLABS, S5_ROWS // S5_ROW_TILE),
        in_specs=[
            pl.BlockSpec((None, S5_ROW_TILE, S5_SLAB_W), lambda j, p: (j, p, 0)),
            pl.BlockSpec((None, S5_ROW_TILE, S5_ROW_TILE), lambda j, p: (p, 0, 0)),
            pl.BlockSpec((None, S5_SLAB_W, S5_SLAB_W), lambda j, p: (j, 0, 0)),
        ],
        out_specs=pl.BlockSpec((S5_SLAB_COLS, S5_ROW_TILE, 128), lambda j, p: (j, p, 0)),
        out_shape=jax.ShapeDtypeStruct((S5_STATE_COLS, S5_ROWS, 128), F32),
        compiler_params=pltpu.CompilerParams(vmem_limit_bytes=VMEM_LIMIT),
        name="s5_state",
    )(uj, jnp.asarray(perm, BF16), web)


S5_SCAN_COLS = 8


def _s5_scan_kernel(s_ref, a_ref, h0_ref, hin_ref, fin_ref, hf, hb):
    ncol = S5_SCAN_COLS

    def scan(row0, nc, nb, h0):
        is_f = lax.broadcasted_iota(jnp.int32, (nb, 128), 1) < S5_STATE
        chunk_rows = lambda c: pl.ds(pl.multiple_of(row0 + c * nb, 8), nb)

        def body(c, hs):
            rf = chunk_rows(c)
            rb = chunk_rows(nc - 1 - c)
            new = []
            for m in range(ncol // 2):
                h_re, h_im = hs[2 * m], hs[2 * m + 1]
                a_re = a_ref[:, (2 * m) * 128:(2 * m + 1) * 128]
                a_im = a_ref[:, (2 * m + 1) * 128:(2 * m + 2) * 128]
                loc = []
                for k, h in ((2 * m, h_re), (2 * m + 1, h_im)):
                    hf[k, rf, :] = h
                    hb[k, rb, :] = h
                    loc.append(jnp.where(is_f, s_ref[k, rf, :], s_ref[k, rb, :]))
                new.append(a_re * h_re - a_im * h_im + loc[0])
                new.append(a_re * h_im + a_im * h_re + loc[1])
            return tuple(new)

        return lax.fori_loop(0, nc, body, h0)

    fin = scan(0, SEQ // S5_CHUNK, BATCH, tuple(jnp.zeros((BATCH, 128), F32) for _ in range(ncol)))
    for k in range(ncol):
        fin_ref[:, k * 128:(k + 1) * 128] = fin[k]
    scan(S5_ROWS_C, DEC_SEQ // S5_CHUNK, DEC_BATCH,
         tuple(h0_ref[:, k * 128:(k + 1) * 128] for k in range(ncol)))
    fwd = lax.broadcasted_iota(jnp.int32, (ncol, S5_ROWS, 128), 2) < S5_STATE
    hin_ref[...] = jnp.where(fwd, hf[...], hb[...]).astype(BF16)


def _s5_scan(sloc, a16, h0l):
    ncol = S5_SCAN_COLS
    w = ncol * 128
    return pl.pallas_call(
        _s5_scan_kernel,
        grid=(S5_STATE_COLS // ncol,),
        in_specs=[
            pl.BlockSpec((ncol, S5_ROWS, 128), lambda k: (k, 0, 0)),
            pl.BlockSpec((1, w), lambda k: (0, k)),
            pl.BlockSpec((DEC_BATCH, w), lambda k: (0, k)),
        ],
        out_specs=[
            pl.BlockSpec((ncol, S5_ROWS, 128), lambda k: (k, 0, 0)),
            pl.BlockSpec((BATCH, w), lambda k: (0, k)),
        ],
        out_shape=[
            jax.ShapeDtypeStruct((S5_STATE_COLS, S5_ROWS, 128), BF16),
            jax.ShapeDtypeStruct((BATCH, S5_STATE_COLS * 128), F32),
        ],
        scratch_shapes=[pltpu.VMEM((ncol, S5_ROWS, 128), F32)] * 2,
        name="s5_scan",
    )(sloc, a16, h0l)


def _s5_out_kernel(u_ref, hin_ref, perm_t_ref, wt_ref, wca_ref, d_ref, y_ref):
    u = u_ref[...]
    hin = jnp.concatenate([hin_ref[k] for k in range(S5_SLAB_COLS)], axis=1).astype(BF16)
    hin = _dot(perm_t_ref[...], hin).astype(BF16)
    y = _dot(u.astype(BF16), wt_ref[...]) + _dot(hin, wca_ref[...]) + u * d_ref[...]
    for t in range(S5_CHUNK):
        y_ref[pl.ds(t, S5_ROW_TILE, stride=S5_CHUNK), :] = y[:, t * 128:(t + 1) * 128]


def _s5_out(uj, hin, wt, wca, dj):
    _, perm_t = _s5_row_perms()
    return pl.pallas_call(
        _s5_out_kernel,
        grid=(S5_SLABS, S5_ROWS // S5_ROW_TILE),
        in_specs=[
            pl.BlockSpec((None, S5_ROW_TILE, S5_SLAB_W), lambda j, p: (j, p, 0)),
            pl.BlockSpec((S5_SLAB_COLS, S5_ROW_TILE, 128), lambda j, p: (j, p, 0)),
            pl.BlockSpec((None, S5_ROW_TILE, S5_ROW_TILE), lambda j, p: (p, 0, 0)),
            pl.BlockSpec((None, S5_SLAB_W, S5_SLAB_W), lambda j, p: (j, 0, 0)),
            pl.BlockSpec((None, S5_SLAB_W, S5_SLAB_W), lambda j, p: (j, 0, 0)),
            pl.BlockSpec((None, 1, S5_SLAB_W), lambda j, p: (j, 0, 0)),
        ],
        out_specs=pl.BlockSpec((None, S5_ROW_TILE * S5_CHUNK, 128), lambda j, p: (j, p, 0)),
        out_shape=jax.ShapeDtypeStruct((S5_SLABS, NTOK, 128), F32),
        compiler_params=pltpu.CompilerParams(vmem_limit_bytes=VMEM_LIMIT),
        name="s5_out",
    )(uj, hin, jnp.asarray(perm_t, BF16), wt, wca, dj)


@functools.lru_cache(maxsize=None)
def _gla_consts():
    n = GLA_BLK
    nl = GLA_LEVELS
    r = np.arange(n)
    up = np.zeros((n, 128), np.int32)
    for l in range(nl):
        up[:, l] = (r >> l) & 1
    i = r[:, None]
    j = r[None, :]
    x = np.maximum(i ^ j, 1)
    lev = np.where(j < i, np.floor(np.log2(x)).astype(np.int32), np.where(i == j, nl, -1)).astype(np.int32)
    up2 = np.stack([up, up[::-1]])
    h = n // 2

    def tiled(a):
        return np.stack([np.concatenate([a[:h, :h], a[h:, h:]]), np.concatenate([a[:h, h:], a[h:, :h]])])

    lev2 = np.stack([tiled(lev), tiled(lev[::-1, ::-1])])
    return up2, lev2


@functools.lru_cache(maxsize=None)
def _gla_tables():
    rowblk, seq, first, last = [], [], [], []
    for d in range(2):
        rb, sq, fi, la = [], [], [], []
        for s in range(BATCH + DEC_BATCH):
            nblk = 1 if s < BATCH else DEC_SEQ // GLA_BLK
            base = s if s < BATCH else NTOK_C // GLA_BLK + (s - BATCH) * nblk
            order = range(nblk) if d == 0 else range(nblk - 1, -1, -1)
            for pos, b in enumerate(order):
                rb.append(base + b)
                sq.append(s)
                fi.append(int(pos == 0))
                la.append(int(pos == nblk - 1))
        rowblk.append(rb); seq.append(sq); first.append(fi); last.append(la)
    as_np = lambda a: np.asarray(a, np.int32)
    return as_np(rowblk), as_np(seq), as_np(first), as_np(last)


def _gla_kernel(rowblk_ref, seq_ref, first_ref, last_ref,
                qf_ref, kf_ref, vf_ref, lrf_ref, qb_ref, kb_ref, vb_ref, lrb_ref,
                wgk_ref, bgk_ref, up_ref, lev_ref, s0_ref,
                of_ref, ob_ref, fin_ref, z_scr, st_scr):
    del rowblk_ref
    n = pl.program_id(0)

    @pl.when(first_ref[n] == 1)
    def _():
        latent = seq_ref[n] >= BATCH
        st_scr[...] = jnp.zeros_like(st_scr)
        for d in range(2):
            for h in range(GLA_HEADS):
                st_scr[d, h * GLA_DK:(h + 1) * GLA_DK, h * GLA_DV:(h + 1) * GLA_DV] = jnp.where(
                    latent, s0_ref[d, h], 0.0)

    blocks = [
        _gla_block(False, qf_ref, kf_ref, vf_ref, lrf_ref, wgk_ref.at[0], bgk_ref.at[0], up_ref.at[0],
                   lev_ref.at[0], of_ref, z_scr.at[0], st_scr.at[0]),
        _gla_block(True, qb_ref, kb_ref, vb_ref, lrb_ref, wgk_ref.at[1], bgk_ref.at[1], up_ref.at[1],
                   lev_ref.at[1], ob_ref, z_scr.at[1], st_scr.at[1]),
    ]
    for stage in range(2):
        for block in blocks:
            next(block, None)

    @pl.when(last_ref[n] == 1)
    def _():
        for d in range(2):
            for h in range(GLA_HEADS):
                fin_ref[d, h] = st_scr[d, h * GLA_DK:(h + 1) * GLA_DK, h * GLA_DV:(h + 1) * GLA_DV]


def _gla_block(backward, q_ref, k_ref, v_ref, lr_ref, wgk_ref, bgk_ref, up_ref, lev_ref, o_ref, z_scr, st_scr):
    nl = GLA_LEVELS
    blk = GLA_BLK
    q = q_ref[...] * (GLA_DK ** -0.5)
    k = k_ref[...]
    vb = v_ref[...].astype(BF16)
    x = _dot(lr_ref[...].astype(BF16), wgk_ref[...]) + bgk_ref[...]
    gk = (jnp.minimum(x, 0.0) - jnp.log(1.0 + jnp.exp(-jnp.abs(x)))) * (1.0 / GLA_NORMALIZER)
    g_hi = gk.astype(BF16)
    g_lo = (gk - g_hi.astype(F32)).astype(BF16)
    ones = jnp.ones((blk, 128), BF16)
    tot = _dot_tn(g_hi, ones) + _dot_tn(g_lo, ones)

    row = lax.broadcasted_iota(jnp.int32, (blk, 1), 0)

    def sibling(a, l):
        g = 1 << l
        if g < 8:
            a3 = a.reshape(blk // 8, 8, a.shape[-1])
            dn = pltpu.roll(a3, g, 1).reshape(a.shape)
            up_ = pltpu.roll(a3, 8 - g, 1).reshape(a.shape)
            return jnp.where(((row >> l) & 1) == 1, dn, up_)
        a4 = a.reshape(blk // (2 * g), 2, g, a.shape[-1])
        return jnp.concatenate([a4[:, 1:2], a4[:, 0:1]], axis=1).reshape(a.shape)

    part = gk
    total = gk
    z0 = None
    for l in range(nl):
        g = 1 << l
        if g < 8:
            up = up_ref[:, l:l + 1] != 0
            z = jnp.where(up, q, k) * jnp.exp(jnp.where(up, part, total - part))
            other = sibling(total, l)
            part = part + jnp.where(up, other, 0.0)
            total = total + other
        else:
            halves = lambda a: (a.reshape(blk // (2 * g), 2, g, a.shape[-1])[:, 1 - int(backward)],
                                a.reshape(blk // (2 * g), 2, g, a.shape[-1])[:, int(backward)])
            join = lambda u, d: jnp.stack([d, u] if not backward else [u, d], axis=1).reshape(blk, u.shape[-1])
            part_u, part_d = halves(part)
            tot_u, tot_d = halves(total)
            q_u, _ = halves(q)
            _, k_d = halves(k)
            z = join(q_u * jnp.exp(part_u), k_d * jnp.exp(tot_d - part_d))
            part = join(part_u + tot_d, part_d)
            both = tot_u + tot_d
            total = join(both, both)
        if l == 0:
            z0 = z
        else:
            z_scr[l] = z.astype(BF16)
    yield
    lane128 = lax.broadcasted_iota(jnp.int32, (GLA_QK, 128), 1)
    dim = lax.broadcasted_iota(jnp.int32, (GLA_QK, 128), 0)
    head_sum = ((dim >> 6) == lane128).astype(BF16)
    pair0 = _dot((z0 * sibling(z0, 0)).astype(BF16), head_sum)
    diag = _dot((q * k).astype(BF16), head_sum)

    half = blk // 2
    lev_d = lev_ref[0]
    lev_o = lev_ref[1]
    lane = lax.broadcasted_iota(jnp.int32, (half, GLA_QK), 1)

    def tiles(l, in_head, crossed):
        out = []
        for r in range(2):
            c = 1 - r if crossed else r
            lhs = z_scr[l, r * half:(r + 1) * half, :]
            keys = z_scr[l, c * half:(c + 1) * half, :]
            out.append(_dot_nt(lhs, jnp.where(in_head, keys, jnp.zeros_like(keys))))
        return jnp.concatenate(out, axis=0)

    upi = 0 if backward else 1
    key_lanes = {}
    for l in range(3, nl - 1):
        g = 1 << l
        c = lax.broadcasted_iota(jnp.int32, (blk // (2 * g), g, 128), 0)
        ln = lax.broadcasted_iota(jnp.int32, (blk // (2 * g), g, 128), 2)
        base = (2 * g * c + (g if backward else 0)) & 127
        key_lanes[l] = (ln >= base) & (ln < base + g)

    heads = range(GLA_HEADS)
    in_head = [(lane >= h * GLA_DK) & (lane < (h + 1) * GLA_DK) for h in heads]
    acc = [jnp.where(lev_d == 0, pair0[:, h:h + 1], jnp.where(lev_d == nl, diag[:, h:h + 1], 0.0)) for h in heads]
    for l in range(1, 3):
        acc = [jnp.where(lev_d == l, tiles(l, in_head[h], False), acc[h]) for h in heads]
    for l in range(3, nl - 1):
        g = 1 << l
        for h in heads:
            acc4 = acc[h].reshape(blk // (2 * g), 2, g, 128)
            s4 = tiles(l, in_head[h], False).reshape(blk // (2 * g), 2, g, 128)
            new_up = jnp.where(key_lanes[l], s4[:, upi], acc4[:, upi])
            pieces = [acc4[:, 0], new_up] if upi == 1 else [new_up, acc4[:, 1]]
            acc[h] = jnp.stack(pieces, axis=1).reshape(blk, 128)
    off = [jnp.where(lev_o == nl - 1, tiles(nl - 1, in_head[h], True), 0.0) for h in heads]
    for h in heads:
        att = jnp.concatenate([jnp.concatenate([acc[h][:half], off[h][:half]], axis=1),
                               jnp.concatenate([off[h][half:], acc[h][half:]], axis=1)], axis=0)
        o_ref[:, h * GLA_DV:(h + 1) * GLA_DV] = _dot(att.astype(BF16), vb[:, h * GLA_DV:(h + 1) * GLA_DV])

    st = st_scr[...]
    q_in = (q * jnp.exp(part)).astype(BF16)
    o_ref[...] += _dot(q_in, st.astype(BF16))
    k_out = (k * jnp.exp(total - part)).astype(BF16)
    kv = _dot_tn(k_out, vb)
    row = lax.broadcasted_iota(jnp.int32, (GLA_QK, GLA_V), 0)
    col = lax.broadcasted_iota(jnp.int32, (GLA_QK, GLA_V), 1)
    same_head = (row >> 6) == (col >> 7)
    decay = jnp.exp(tot)
    decay = jnp.concatenate([decay] * GLA_HEADS, axis=1)
    st_new = decay * st + jnp.where(same_head, kv, 0.0)
    st_scr[...] = st_new


def _gla_mix(bslab, lr, wgk, bgk, state_gla, layer):
    up, lev = _gla_consts()
    rowblk, seq, first, last = _gla_tables()
    nsteps = rowblk.shape[1]
    nseq = BATCH + DEC_BATCH
    nl = GLA_LEVELS
    whole = lambda shape: pl.BlockSpec(shape, lambda n, rb, sq, fi, la: (0,) * len(shape))

    def token_specs(d):
        return [
            pl.BlockSpec((GLA_BLK, GLA_QK), lambda n, rb, sq, fi, la: (rb[d, n], 0)),
            pl.BlockSpec((GLA_BLK, GLA_QK), lambda n, rb, sq, fi, la: (rb[d, n], 1)),
            pl.BlockSpec((GLA_BLK, GLA_V), lambda n, rb, sq, fi, la: (rb[d, n], 1)),
            pl.BlockSpec((GLA_BLK, 128), lambda n, rb, sq, fi, la: (rb[d, n], 0)),
        ]

    state_spec = pl.BlockSpec((None, 2, GLA_HEADS, GLA_DK, GLA_DV), lambda n, rb, sq, fi, la: (sq[n], 0, 0, 0, 0))
    of_layer = lambda shape: pl.BlockSpec((None,) + shape, lambda n, rb, sq, fi, la: (layer,) + (0,) * len(shape))
    grid_spec = pltpu.PrefetchScalarGridSpec(
        num_scalar_prefetch=4,
        grid=(nsteps,),
        in_specs=token_specs(0) + token_specs(1) + [
            of_layer((2, 128, GLA_QK)),
            of_layer((2, 1, GLA_QK)),
            whole((2, GLA_BLK, 128)),
            whole((2, 2, GLA_BLK, GLA_BLK // 2)),
            pl.BlockSpec((None, None, 2, GLA_HEADS, GLA_DK, GLA_DV),
                         lambda n, rb, sq, fi, la: (jnp.maximum(sq[n] - BATCH, 0), layer, 0, 0, 0, 0)),
        ],
        out_specs=[
            pl.BlockSpec((GLA_BLK, GLA_V), lambda n, rb, sq, fi, la: (rb[0, n], 0)),
            pl.BlockSpec((GLA_BLK, GLA_V), lambda n, rb, sq, fi, la: (rb[1, n], 0)),
            state_spec,
        ],
        scratch_shapes=[
            pltpu.VMEM((2, nl, GLA_BLK, GLA_QK), BF16),
            pltpu.VMEM((2, GLA_QK, GLA_V), F32),
        ],
    )
    return pl.pallas_call(
        _gla_kernel,
        grid_spec=grid_spec,
        out_shape=[
            jax.ShapeDtypeStruct((NTOK, GLA_V), F32),
            jax.ShapeDtypeStruct((NTOK, GLA_V), F32),
            jax.ShapeDtypeStruct((nseq, 2, GLA_HEADS, GLA_DK, GLA_DV), F32),
        ],
        compiler_params=pltpu.CompilerParams(vmem_limit_bytes=VMEM_LIMIT),
        name="gla_mix",
    )(jnp.asarray(rowblk), jnp.asarray(seq[0]), jnp.asarray(first[0]), jnp.asarray(last[0]),
      bslab, bslab, bslab, lr, bslab, bslab, bslab, lr, wgk, bgk, jnp.asarray(up), jnp.asarray(lev), state_gla)


def _attn_ctx_kernel(sink_ref, q_ref, k_ref, v_ref, o_ref, ko_ref, vo_ref, *, layer):
    k = k_ref[...]
    v = v_ref[...]
    ko_ref[...] = k
    vo_ref[...] = v
    ks = (k.astype(BF16), pltpu.roll(k, 64, 1).astype(BF16))
    vs = (v.astype(BF16), pltpu.roll(v, 64, 1).astype(BF16))
    lo = lax.broadcasted_iota(jnp.int32, (SEQ, 128), 1) < HEAD_DIM
    units = []
    for t in range(ATT_HEADS // 2):
        qt = q_ref[:, t * 128:(t + 1) * 128] * (HEAD_DIM ** -0.5)
        for p in range(2):
            qm = jnp.where(lo if p == 0 else jnp.logical_not(lo), qt, 0.0).astype(BF16)
            units.append((qm, 0 if p == t // 2 else 1, sink_ref[layer, 2 * t + p]))
    scores = [_dot_nt(qm, ks[which]) for qm, which, _ in units]
    maxes = [jnp.maximum(sink, jnp.max(s, axis=-1, keepdims=True)) for s, (_, _, sink) in zip(scores, units)]
    probs = [jnp.exp(s - m) for s, m in zip(scores, maxes)]
    dens = [jnp.exp(sink - m) + jnp.sum(p, axis=-1, keepdims=True)
            for p, m, (_, _, sink) in zip(probs, maxes, units)]
    outs = [_dot(p.astype(BF16), vs[which]) / den for p, den, (_, which, _) in zip(probs, dens, units)]
    for t in range(ATT_HEADS // 2):
        o_ref[:, t * 128:(t + 1) * 128] = jnp.where(lo, outs[2 * t], outs[2 * t + 1]).astype(BF16)


def _attn_ctx(sink, cslab, layer):
    kv_out = pl.BlockSpec((None, SEQ, ATT_KV), lambda b: (b, 0, 0))
    return pl.pallas_call(
        functools.partial(_attn_ctx_kernel, layer=layer),
        grid=(BATCH,),
        in_specs=[
            pl.BlockSpec(memory_space=pltpu.SMEM),
            pl.BlockSpec((SEQ, ATT_Q), lambda b: (b, 0)),
            pl.BlockSpec((SEQ, ATT_KV), lambda b: (b, 4)),
            pl.BlockSpec((SEQ, ATT_KV), lambda b: (b, 5)),
        ],
        out_specs=[pl.BlockSpec((SEQ, ATT_Q), lambda b: (b, 0)), kv_out, kv_out],
        out_shape=[jax.ShapeDtypeStruct((NTOK_C, ATT_Q), BF16),
                   jax.ShapeDtypeStruct((BATCH, SEQ, ATT_KV), F32),
                   jax.ShapeDtypeStruct((BATCH, SEQ, ATT_KV), F32)],
        name="attn_ctx",
    )(sink, cslab, cslab, cslab)


def _attn_lat_kernel(sink_ref, q_ref, kp_ref, kc_ref, kn_ref, vp_ref, vc_ref, vn_ref,
                     ck_ref, cv_ref, cos_ref, sin_ref, bias_ref, o_ref, *, layer):
    j = pl.program_id(1)
    nb = DEC_SEQ // ATT_BLOCK
    lane = lax.broadcasted_iota(jnp.int32, (ATT_BLOCK, 128), 1)
    lo = lane < HEAD_DIM
    first16 = (lane & 31) < 16

    def rope(x, blk_idx):
        r0 = pl.multiple_of(blk_idx * ATT_BLOCK, ATT_BLOCK)
        c = cos_ref[pl.ds(r0, ATT_BLOCK), :]
        s = sin_ref[pl.ds(r0, ATT_BLOCK), :]
        xs = jnp.where(first16, pltpu.roll(x, 112, 1), pltpu.roll(x, 16, 1))
        return x * c + xs * s

    nwin = 3 * ATT_BLOCK
    keys = jnp.concatenate([rope(kp_ref[...], jnp.maximum(j - 1, 0)), rope(kc_ref[...], j),
                            rope(kn_ref[...], jnp.minimum(j + 1, nb - 1)), ck_ref[...]], axis=0)
    vals = jnp.concatenate([vp_ref[...], vc_ref[...], vn_ref[...], cv_ref[...]], axis=0)
    keys2 = (keys.astype(BF16), pltpu.roll(keys, 64, 1).astype(BF16))
    vals2 = (vals.astype(BF16), pltpu.roll(vals, 64, 1).astype(BF16))
    kcol = lax.broadcasted_iota(jnp.int32, (1, nwin + PAST_LEN), 1)
    edge = jnp.where(((j == 0) & (kcol < ATT_BLOCK)) | ((j == nb - 1) & (kcol >= 2 * ATT_BLOCK) & (kcol < nwin)),
                     -1e30, 0.0)
    bias = bias_ref[...] + edge
    top = lax.broadcasted_iota(jnp.int32, (2 * ATT_BLOCK, 1), 0) < ATT_BLOCK
    q_tiles = [rope(q_ref[:, t * 128:(t + 1) * 128], j) * (HEAD_DIM ** -0.5) for t in range(ATT_HEADS // 2)]
    lo2 = jnp.concatenate([lo, lo], axis=0)
    units = []
    for kvh in range(ATT_KV_HEADS):
        q2 = jnp.concatenate(q_tiles[2 * kvh:2 * kvh + 2], axis=0)
        for p in range(2):
            qm = jnp.where(lo2 if p == 0 else jnp.logical_not(lo2), q2, 0.0).astype(BF16)
            sink = jnp.where(top, sink_ref[layer, 4 * kvh + p], sink_ref[layer, 4 * kvh + 2 + p])
            units.append((qm, 0 if p == kvh else 1, sink))
    scores = [_dot_nt(qm, keys2[which]) + bias for qm, which, _ in units]
    maxes = [jnp.maximum(sink, jnp.max(s, axis=-1, keepdims=True)) for s, (_, _, sink) in zip(scores, units)]
    probs = [jnp.exp(s - m) for s, m in zip(scores, maxes)]
    dens = [jnp.exp(sink - m) + jnp.sum(p, axis=-1, keepdims=True)
            for p, m, (_, _, sink) in zip(probs, maxes, units)]
    outs = [_dot(p.astype(BF16), vals2[which]) / den for p, den, (_, which, _) in zip(probs, dens, units)]
    for kvh in range(ATT_KV_HEADS):
        o2 = jnp.where(lo2, outs[2 * kvh], outs[2 * kvh + 1])
        for i in range(2):
            t = 2 * kvh + i
            o_ref[:, t * 128:(t + 1) * 128] = o2[i * ATT_BLOCK:(i + 1) * ATT_BLOCK].astype(BF16)


def _attn_lat(sink, cslab, ck, cv, cos_t, sin_t, layer):
    nb = DEC_SEQ // ATT_BLOCK
    base = NTOK_C // ATT_BLOCK
    cur = lambda b, j: base + b * nb + j
    prv = lambda b, j: base + b * nb + jnp.maximum(j - 1, 0)
    nxt = lambda b, j: base + b * nb + jnp.minimum(j + 1, nb - 1)
    kv_spec = lambda row, col: pl.BlockSpec((ATT_BLOCK, ATT_KV), lambda b, j: (row(b, j), col))
    qi = np.arange(2 * ATT_BLOCK)[:, None] % ATT_BLOCK
    kc = np.arange(3 * ATT_BLOCK + PAST_LEN)[None, :]
    inside = (np.abs(kc - ATT_BLOCK - qi) <= WINDOW) | (kc >= 3 * ATT_BLOCK)
    band = np.where(inside, 0.0, -1e30).astype(np.float32)
    cache_spec = pl.BlockSpec((None, None, PAST_LEN, ATT_KV), lambda b, j: (b, layer, 0, 0))
    return pl.pallas_call(
        functools.partial(_attn_lat_kernel, layer=layer),
        grid=(DEC_BATCH, nb),
        in_specs=[
            pl.BlockSpec(memory_space=pltpu.SMEM),
            pl.BlockSpec((ATT_BLOCK, ATT_Q), lambda b, j: (cur(b, j), 0)),
            kv_spec(prv, 4), kv_spec(cur, 4), kv_spec(nxt, 4),
            kv_spec(prv, 5), kv_spec(cur, 5), kv_spec(nxt, 5),
            cache_spec, cache_spec,
            pl.BlockSpec((DEC_SEQ, 128), lambda b, j: (0, 0)),
            pl.BlockSpec((DEC_SEQ, 128), lambda b, j: (0, 0)),
            pl.BlockSpec(band.shape, lambda b, j: (0, 0)),
        ],
        out_specs=pl.BlockSpec((ATT_BLOCK, ATT_Q), lambda b, j: (b * nb + j, 0)),
        out_shape=jax.ShapeDtypeStruct((NTOK_L, ATT_Q), BF16),
        name="attn_lat",
    )(sink, cslab, cslab, cslab, cslab, cslab, cslab, cslab, ck, cv, cos_t, sin_t, jnp.asarray(band))


@functools.lru_cache(maxsize=None)
def _rope_tables():
    rows = DEC_SEQ // GRID_W
    row = np.repeat(np.arange(rows, dtype=np.float64), GRID_W)
    col = np.tile(np.arange(GRID_W, dtype=np.float64), rows)
    quarter = HEAD_DIM // 4
    inv = ROPE_BASE ** (-np.arange(quarter, dtype=np.float64) / quarter)
    lane = np.arange(128)
    use_row = (lane % HEAD_DIM) < HEAD_DIM // 2
    pos = np.where(use_row[None, :], row[:, None], col[:, None])
    ang = pos * inv[lane % quarter][None, :]
    sign = np.where((lane % 32) < 16, -1.0, 1.0)
    return np.cos(ang).astype(np.float32), (np.sin(ang) * sign[None, :]).astype(np.float32)


def _merge_kernel(*refs, split_x):
    if split_x:
        xc_ref, xl_ref, *refs = refs
    else:
        xc_ref, *refs = refs
    (mod_ref, g_ref, ys5_ref, ogf_ref, ogb_ref, gb_ref, ycc_ref, ycl_ref, gate_ref, gng_ref,
     wglu_ref, wbr_ref, wout_ref, o_ref) = refs
    is_ctx = pl.program_id(0) < NTOK_C // TM
    if split_x:
        x = jnp.where(is_ctx, xc_ref[...], xl_ref[...])
    else:
        x = xc_ref[...]
    y = jnp.concatenate([ys5_ref[j] for j in range(S5_SLABS)], axis=1)
    y = 0.5 * y * (1.0 + jnp.tanh(math.sqrt(2.0 / math.pi) * (y + 0.044715 * (y * y * y))))
    ag = _dot(y.astype(BF16), wglu_ref[...])
    y_a = ag[:, :S5_WIDTH] * _sigmoid(ag[:, S5_WIDTH:])
    gng = gng_ref[...]
    parts = []
    for h in range(GLA_HEADS):
        sl = slice(h * GLA_DV, (h + 1) * GLA_DV)
        o = ogf_ref[:, sl] + ogb_ref[:, sl]
        g = gb_ref[:, sl]
        parts.append(_rms(o, gng) * (g * _sigmoid(g)))
    y_b = jnp.concatenate(parts, axis=1)
    y_c = jnp.where(is_ctx, ycc_ref[...], ycl_ref[...])
    merged = None
    for n, yn in enumerate((y_a, y_b, y_c)):
        proj = _dot(yn.astype(BF16), wbr_ref[n])
        term = gate_ref[:, n * D_MODEL:(n + 1) * D_MODEL].astype(F32) * proj
        merged = term if merged is None else merged + term
    mixed = _dot(merged.astype(BF16), wout_ref[...])
    g1 = mod_ref[:, 2 * D_MODEL:3 * D_MODEL]
    o_ref[...] = x + g1 * _rms(mixed, g_ref[...])


def _layer_spec(shape, layer):
    return pl.BlockSpec((None,) + shape, lambda i: (layer,) + (0,) * len(shape), pipeline_mode=pl.Buffered(1))


def _split_token_specs(n_arrays, width=D_MODEL):
    nct = NTOK_C // TM
    if n_arrays == 2:
        return [pl.BlockSpec((TM, width), lambda i: (jnp.minimum(i, nct - 1), 0)),
                pl.BlockSpec((TM, width), lambda i: (jnp.maximum(i - nct, 0), 0))]
    return [pl.BlockSpec((TM, width), lambda i: (i, 0))]


def _merge(xs, mod, g, ys5, og, bslab, yc, gates, gng, wglu, wbr, wout, layer):
    tok = lambda width, col=0: pl.BlockSpec((TM, width), lambda i: (i, col))
    full = lambda shape: _layer_spec(shape, layer)
    return pl.pallas_call(
        functools.partial(_merge_kernel, split_x=len(xs) == 2),
        grid=(NTOK // TM,),
        in_specs=_split_token_specs(len(xs)) + [
            _mod_spec(layer),
            _gain_spec(layer, 1),
            pl.BlockSpec((S5_SLABS, TM, 128), lambda i: (0, _s5_tile(i), 0)),
            tok(GLA_V),
            tok(GLA_V),
            tok(GLA_V, 2),
        ] + _split_token_specs(2, ATT_Q) + [
            tok(N_BRANCH * D_MODEL),
            pl.BlockSpec((None, 1, GLA_DV), lambda i: (layer, 0, 0)),
            full((S5_WIDTH, 2 * S5_WIDTH)),
            full((N_BRANCH, BRANCH_W, D_MODEL)),
            full((D_MODEL, D_MODEL)),
        ],
        out_specs=tok(D_MODEL),
        out_shape=jax.ShapeDtypeStruct((NTOK, D_MODEL), F32),
        compiler_params=pltpu.CompilerParams(vmem_limit_bytes=VMEM_LIMIT),
        name="merge",
    )(*xs, mod, g, ys5, *og, bslab, *yc, gates, gng, wglu, wbr, wout)


FFN_SPLIT = 1


def _ffn_kernel(x_ref, mod_ref, gin_ref, gout_ref, w1_ref, w2_ref, *o_refs):
    x = x_ref[...]
    sh = mod_ref[:, 3 * D_MODEL:4 * D_MODEL]
    sc = mod_ref[:, 4 * D_MODEL:5 * D_MODEL]
    g2 = mod_ref[:, 5 * D_MODEL:6 * D_MODEL]
    h = (_rms(x, gin_ref[...]) * (1.0 + sc) + sh).astype(BF16)
    ck = FFN_HIDDEN // FFN_SPLIT
    acc = None
    for c in range(FFN_SPLIT):
        a = _dot(h, w1_ref[:, c * ck:(c + 1) * ck])
        b = _dot(h, w1_ref[:, FFN_HIDDEN + c * ck:FFN_HIDDEN + (c + 1) * ck])
        act = (a * _sigmoid(a) * b).astype(BF16)
        part = _dot(act, w2_ref[c * ck:(c + 1) * ck, :])
        acc = part if acc is None else acc + part
    y = x + g2 * _rms(acc, gout_ref[...])
    if len(o_refs) == 1:
        o_refs[0][...] = y
    else:
        is_ctx = pl.program_id(0) < NTOK_C // TM

        @pl.when(is_ctx)
        def _():
            o_refs[0][...] = y

        @pl.when(jnp.logical_not(is_ctx))
        def _():
            o_refs[1][...] = y


def _ffn(x, mod, gains, w1, w2, layer, split_out):
    nct = NTOK_C // TM
    if split_out:
        out_specs = [pl.BlockSpec((TM, D_MODEL), lambda i: (jnp.minimum(i, nct - 1), 0)),
                     pl.BlockSpec((TM, D_MODEL), lambda i: (jnp.maximum(i - nct, 0), 0))]
        out_shape = [jax.ShapeDtypeStruct((NTOK_C, D_MODEL), F32), jax.ShapeDtypeStruct((NTOK_L, D_MODEL), F32)]
    else:
        out_specs = pl.BlockSpec((TM, D_MODEL), lambda i: (i, 0))
        out_shape = jax.ShapeDtypeStruct((NTOK, D_MODEL), F32)
    return pl.pallas_call(
        _ffn_kernel,
        grid=(NTOK // TM,),
        in_specs=[
            pl.BlockSpec((TM, D_MODEL), lambda i: (i, 0)),
            _mod_spec(layer),
            _gain_spec(layer, 2),
            _gain_spec(layer, 3),
            _layer_spec((D_MODEL, 2 * FFN_HIDDEN), layer),
            _layer_spec((FFN_HIDDEN, D_MODEL), layer),
        ],
        out_specs=out_specs,
        out_shape=out_shape,
        compiler_params=pltpu.CompilerParams(vmem_limit_bytes=VMEM_LIMIT),
        name="ffn",
    )(x, mod, gains, gains, w1, w2)


def kernel(x_prompt, x_sample, cache_k, cache_v, state_s5, state_gla, c, c_ctx, w_mod, b_mod, norm_g, w_in,
           s5_lam_re, s5_lam_im, s5_log_step, s5_b_re, s5_b_im, s5_c_re, s5_c_im, s5_d, w_glu, gla_w_gk,
           gla_b_gk, gla_norm_g, att_sink, w_branch, w_out, w_ffn_in, w_ffn_out):
    cond = jnp.concatenate([c_ctx[None, :], c, jnp.zeros((N_MOD_ROWS - 1 - DEC_BATCH, D_MODEL), F32)], axis=0)
    mod_all = _modulation(cond, w_mod, b_mod).reshape(DEPTH, N_MOD_ROWS, 1, 6 * D_MODEL)
    cos_t, sin_t = _rope_tables()
    xs = (x_prompt.reshape(NTOK_C, D_MODEL), x_sample.reshape(NTOK_L, D_MODEL))
    w_in_b = w_in.astype(BF16)
    w_in_end = jnp.pad(w_in[:, :, D_IN_TILED:].astype(BF16), ((0, 0), (0, 0), (0, W_IN_COLS - D_IN)))
    w_glu_b, w_branch_b, w_out_b = w_glu.astype(BF16), w_branch.astype(BF16), w_out.astype(BF16)
    w_ffn_in_b, w_ffn_out_b = w_ffn_in.astype(BF16), w_ffn_out.astype(BF16)
    s5_params = _s5_params(s5_lam_re, s5_lam_im, s5_log_step, s5_b_re, s5_b_im, s5_c_re, s5_c_im)
    h0_all = state_s5.astype(F32).transpose(1, 0, 3, 5, 2, 4).reshape(DEPTH, DEC_BATCH, S5_GROUPS * 256)
    wgk_all = jnp.stack([jnp.pad(gla_w_gk[:, d], ((0, 0), (d * GLA_RANK, 128 - (d + 1) * GLA_RANK), (0, 0)))
                         for d in range(2)], axis=1).astype(BF16)
    bgk_all = gla_b_gk[:, :, None, :].astype(F32)
    gains = norm_g.astype(F32).reshape(DEPTH * 4, 1, D_MODEL)
    gla_gain = gla_norm_g.astype(F32).reshape(DEPTH, 1, GLA_DV)
    sink = att_sink.astype(F32)
    cache_k2 = cache_k.astype(F32).reshape(DEC_BATCH, DEPTH, PAST_LEN, ATT_KV)
    cache_v2 = cache_v.astype(F32).reshape(DEC_BATCH, DEPTH, PAST_LEN, ATT_KV)
    state_gla = state_gla.astype(F32)
    mod = mod_all
    new_k, new_v, new_s5, new_gla = [], [], [], []
    for i in range(DEPTH):
        uj, bslab, cslab, gates, lr = _inproj(xs, mod, gains, w_in_b, w_in_end, i)

        wt, web, wca, a16, dj = _s5_prep(s5_params, s5_d[i], i)
        hin, finc = _s5_scan(_s5_state(uj, web), a16, h0_all[i])
        ys5 = _s5_out(uj, hin, wt, wca, dj)
        new_s5.append(finc)

        *og, gla_fin = _gla_mix(bslab, lr, wgk_all, bgk_all, state_gla, i)
        new_gla.append(gla_fin[:BATCH])

        yc_ctx, k_new, v_new = _attn_ctx(sink, cslab, i)
        yc = (yc_ctx, _attn_lat(sink, cslab, cache_k2, cache_v2, cos_t, sin_t, i))
        new_k.append(k_new.reshape(BATCH, SEQ, ATT_KV_HEADS, HEAD_DIM))
        new_v.append(v_new.reshape(BATCH, SEQ, ATT_KV_HEADS, HEAD_DIM))

        x = _merge(xs, mod, gains, ys5, og, bslab, yc, gates, gla_gain, w_glu_b, w_branch_b, w_out_b, i)
        last = i == DEPTH - 1
        x = _ffn(x, mod, gains, w_ffn_in_b, w_ffn_out_b, i, last)
        xs = tuple(x) if last else (x,)

    return (xs[0].reshape(BATCH, SEQ, D_MODEL), xs[1].reshape(DEC_BATCH, DEC_SEQ, D_MODEL),
            jnp.stack(new_k, axis=1), jnp.stack(new_v, axis=1),
            jnp.stack(new_s5).reshape(DEPTH, BATCH, S5_GROUPS, 2, 2, S5_STATE).transpose(1, 0, 4, 2, 5, 3),
            jnp.stack(new_gla, axis=1))
```

```python
import functools
import math

import numpy as np
import jax
import jax.numpy as jnp
from jax import lax
from jax.experimental import pallas as pl
from jax.experimental.pallas import tpu as pltpu

F32 = jnp.float32
BF16 = jnp.bfloat16

D_MODEL = 1024
BATCH = 16
SEQ = 256
DEPTH = 2
DEC_BATCH = 8
DEC_SEQ = 1024
PAST_LEN = 256
GRID_W = 64
ROPE_BASE = 10000.0
S5_WIDTH = 512
S5_GROUP = 16
S5_GROUPS = 32
S5_STATE = 64
GLA_HEADS = 4
GLA_DK = 64
GLA_DV = 128
GLA_QK = 256
GLA_V = 512
GLA_RANK = 16
GLA_NORMALIZER = 16.0
ATT_HEADS = 8
ATT_KV_HEADS = 2
HEAD_DIM = 64
ATT_Q = 512
ATT_KV = 128
WINDOW = 128
ATT_BLOCK = 128
N_BRANCH = 3
BRANCH_W = 512
FFN_HIDDEN = 2816
RMS_EPS = 1e-6

NTOK_C = BATCH * SEQ
NTOK_L = DEC_BATCH * DEC_SEQ
NTOK = NTOK_C + NTOK_L
TM = 512
N_MOD_ROWS = 16

D_IN = 5920
D_IN_TILED = D_IN // 128 * 128
W_IN_COLS = 6016
S5_CHUNK = 16
S5_SLABS = S5_WIDTH // 128
S5_SLAB_W = S5_CHUNK * 128
S5_ROWS_C = NTOK_C // S5_CHUNK
S5_ROWS = NTOK // S5_CHUNK
S5_ROW_TILE = 256
GLA_BLK = 256
GLA_LEVELS = 8
VMEM_LIMIT = 56 * 1024 * 1024


def _dot(a, b):
    return jnp.dot(a, b, preferred_element_type=F32)


def _dot_nt(a, b):
    return lax.dot_general(a, b, (((1,), (1,)), ((), ())), preferred_element_type=F32)


def _dot_tn(a, b):
    return lax.dot_general(a, b, (((0,), (0,)), ((), ())), preferred_element_type=F32)


def _rms(x, g):
    return x * lax.rsqrt(jnp.mean(x * x, axis=-1, keepdims=True) + RMS_EPS) * g


def _sigmoid(x):
    return 0.5 * jnp.tanh(0.5 * x) + 0.5


def _mod_row(i):
    nct = NTOK_C // TM
    return jnp.where(i < nct, 0, 1 + (i - nct) // (DEC_SEQ // TM))


def _mod_spec(layer):
    return pl.BlockSpec((None, None, 1, 6 * D_MODEL), lambda i: (layer, _mod_row(i), 0, 0))


def _gain_spec(layer, k):
    return pl.BlockSpec((None, 1, D_MODEL), lambda i: (layer * 4 + k, 0, 0))


def _s5_tile(i):
    nct = NTOK_C // TM
    per_seq = DEC_SEQ // TM
    k = i - nct
    return jnp.where(i < nct, i, nct + (k % per_seq) * DEC_BATCH + k // per_seq)


def _mod_kernel(c_ref, w_ref, b_ref, o_ref):
    c = c_ref[...]
    s = (c * _sigmoid(c)).astype(BF16)
    o_ref[...] = _dot(s, w_ref[...].astype(BF16)) + b_ref[...]


def _modulation(cond, w_mod, b_mod):
    tn = 2048
    return pl.pallas_call(
        _mod_kernel,
        grid=(DEPTH, 6 * D_MODEL // tn),
        in_specs=[
            pl.BlockSpec((N_MOD_ROWS, D_MODEL), lambda l, n: (0, 0)),
            pl.BlockSpec((None, D_MODEL, tn), lambda l, n: (l, 0, n)),
            pl.BlockSpec((None, 1, tn), lambda l, n: (l, 0, n)),
        ],
        out_specs=pl.BlockSpec((None, N_MOD_ROWS, tn), lambda l, n: (l, 0, n)),
        out_shape=jax.ShapeDtypeStruct((DEPTH, N_MOD_ROWS, 6 * D_MODEL), F32),
        name="modulation",
    )(cond, w_mod, b_mod.reshape(DEPTH, 1, 6 * D_MODEL))


_IN_SLABS = ((0, 512), (512, 1536), (2048, 768), (2816, 3072), (5888, 128))
W_IN_SPLIT = 2048
W_IN_GAP = 32
W_IN_TAIL = W_IN_COLS - W_IN_SPLIT


def _inproj_kernel(*refs, split_x):
    if split_x:
        xc_ref, xl_ref, *refs = refs
    else:
        xc_ref, *refs = refs
    mod_ref, g_ref, w_ref, w_end_ref, u_ref, b_ref, c_ref, gate_ref, lr_ref, w_tail, u_stage = refs
    i = pl.program_id(0)

    @pl.when(i == 0)
    def _():
        r = lax.broadcasted_iota(jnp.int32, (256, 128), 0)
        c = lax.broadcasted_iota(jnp.int32, (256, 128), 1)
        shift = (r == c + W_IN_GAP).astype(BF16)
        head = ((r == c) & (c < W_IN_GAP)).astype(BF16)
        ntile = (W_IN_TAIL - 128) // 128
        for t in range(ntile - 1):
            src = W_IN_SPLIT + 128 * t
            w_tail[:, 128 * t:128 * (t + 1)] = _dot(w_ref[:, src:src + 256], shift).astype(BF16)
        src = W_IN_SPLIT + 128 * (ntile - 1)
        last = jnp.concatenate([w_ref[:, src:src + 128], w_end_ref[...]], axis=1)
        w_tail[:, 128 * (ntile - 1):128 * ntile] = _dot(last, shift).astype(BF16)
        w_tail[:, 128 * ntile:] = _dot(w_ref[:, W_IN_SPLIT:W_IN_SPLIT + 256], head).astype(BF16)

    if split_x:
        x = jnp.where(i < NTOK_C // TM, xc_ref[...], xl_ref[...])
    else:
        x = xc_ref[...]
    mod = mod_ref[...]
    h = _rms(x, g_ref[...]) * (1.0 + mod[:, D_MODEL:2 * D_MODEL]) + mod[:, 0:D_MODEL]
    h = h.astype(BF16)
    z_head = _dot(h, w_ref[:, 0:W_IN_SPLIT])
    for j in range(S5_SLABS):
        u_stage[...] = z_head[:, j * 128:(j + 1) * 128]
        for s in range(S5_CHUNK):
            u_ref[j, :, s * 128:(s + 1) * 128] = u_stage[pl.ds(s, TM // S5_CHUNK, stride=S5_CHUNK), :]
    b_ref[...] = z_head[:, S5_WIDTH:W_IN_SPLIT]
    z_tail = _dot(h, w_tail[...])
    for (off, width), o_ref in zip(_IN_SLABS[2:], (c_ref, gate_ref, lr_ref)):
        z = z_tail[:, off - W_IN_SPLIT:off - W_IN_SPLIT + width]
        o_ref[...] = _sigmoid(z).astype(BF16) if o_ref is gate_ref else z


def _inproj(xs, mod, g, w_all, w_end, layer):
    return pl.pallas_call(
        functools.partial(_inproj_kernel, split_x=len(xs) == 2),
        grid=(NTOK // TM,),
        in_specs=_split_token_specs(len(xs)) + [
            _mod_spec(layer),
            _gain_spec(layer, 0),
            pl.BlockSpec((None, D_MODEL, D_IN), lambda i: (layer, 0, 0), pipeline_mode=pl.Buffered(1)),
            pl.BlockSpec((None, D_MODEL, 128), lambda i: (layer, 0, 0), pipeline_mode=pl.Buffered(1)),
        ],
        out_specs=[pl.BlockSpec((S5_SLABS, TM // S5_CHUNK, S5_SLAB_W), lambda i: (0, _s5_tile(i), 0))]
        + [pl.BlockSpec((TM, width), lambda i: (i, 0)) for _, width in _IN_SLABS[1:]],
        out_shape=[jax.ShapeDtypeStruct((S5_SLABS, S5_ROWS, S5_SLAB_W), F32)]
        + [jax.ShapeDtypeStruct((NTOK, width), BF16 if width == N_BRANCH * D_MODEL else F32)
           for _, width in _IN_SLABS[1:]],
        scratch_shapes=[pltpu.VMEM((D_MODEL, W_IN_TAIL), BF16), pltpu.VMEM((TM, 128), F32)],
        compiler_params=pltpu.CompilerParams(vmem_limit_bytes=VMEM_LIMIT),
        name="inproj",
    )(*xs, mod, g, w_all, w_end)


@functools.lru_cache(maxsize=None)
def _s5_expanders():
    seg = 8
    spread = np.zeros((seg, 256, S5_SLAB_W), np.float32)
    col = np.arange(256)
    for gl in range(seg):
        spread[gl, col, (col // S5_GROUP) * 128 + gl * S5_GROUP + col % S5_GROUP] = 1.0
    return spread


def _s5_prep_kernel(par_ref, bre_ref, bim_ref, cre_ref, cim_ref, spread_ref,
                    wt_ref, web_ref, wca_ref, a16_ref):
    n = S5_CHUNK
    lam_re = par_ref[0:1, :]
    lam_im = par_ref[1:2, :]
    dt = jnp.exp(par_ref[2:3, :])
    lr = lam_re * dt
    li = lam_im * dt
    krow = lax.broadcasted_iota(jnp.int32, (24, 128), 0).astype(F32)
    tab_mag = jnp.exp(krow * lr)
    tab_re = tab_mag * jnp.cos(krow * li)
    tab_im = tab_mag * jnp.sin(krow * li)
    ar = tab_re[1:2, :]
    ai = tab_im[1:2, :]
    nr = ar - 1.0
    den = lam_re * lam_re + lam_im * lam_im
    fr = (nr * lam_re + ai * lam_im) / den
    fi = (ai * lam_re - nr * lam_im) / den
    b_re = bre_ref[...]
    b_im = bim_ref[...]
    br = fr * b_re - fi * b_im
    bi = fr * b_im + fi * b_re
    c_re = cre_ref[...]
    c_im = cim_ref[...]

    def lo_half(shape):
        return lax.broadcasted_iota(jnp.int32, shape, 1) < S5_STATE

    def tile_rows(a):
        return jnp.concatenate([a] * n, axis=0)

    fwd16 = lo_half((S5_GROUP, 128))

    def powers(t_re, t_im, k_fwd, k_bwd):
        def pick(t, b):
            kf, kb = k_fwd(b), k_bwd(b)
            return jnp.where(fwd16, jnp.broadcast_to(t[kf:kf + 1, :], (S5_GROUP, 128)),
                             jnp.broadcast_to(t[kb:kb + 1, :], (S5_GROUP, 128)))
        return (jnp.concatenate([pick(t_re, b) for b in range(n)], axis=0),
                jnp.concatenate([pick(t_im, b) for b in range(n)], axis=0))

    fwd = lo_half((n * S5_GROUP, 128))
    brt, bit, crt, cit = tile_rows(br), tile_rows(bi), tile_rows(c_re), tile_rows(c_im)

    per, pei = powers(tab_re, tab_im, lambda s: n - 1 - s, lambda s: s)
    eb = jnp.concatenate([brt * per - bit * pei, brt * pei + bit * per], axis=1)
    pcr, pci = powers(tab_re, tab_im, lambda t: t + 1, lambda t: n - t)
    ca = jnp.concatenate([(crt * pcr - cit * pci).T, (-(crt * pci + cit * pcr)).T], axis=0)

    def one_dir(x, d):
        sw = pltpu.roll(x, S5_STATE, 1)
        lo = lo_half(x.shape)
        return jnp.where(lo, x, sw) if d == 0 else jnp.where(lo, sw, x)

    klag = []
    for d in range(2):
        lhs = jnp.where(lo_half(br.shape), one_dir(br, d), -one_dir(bi, d))
        crd, cid = tile_rows(one_dir(c_re, d)), tile_rows(one_dir(c_im, d))
        lag = (lambda b: b) if d == 0 else (lambda b: n - 1 - b)
        pr, pi = powers(one_dir(tab_re, d), one_dir(tab_im, d), lag, lag)
        rhs_t = jnp.where(fwd, crd * pr - cid * pi, crd * pi + cid * pr)
        klag.append(lax.dot_general(lhs, rhs_t, (((1,), (1,)), ((), ())),
                                    precision=lax.Precision.HIGHEST, preferred_element_type=F32))
    lane = lax.broadcasted_iota(jnp.int32, (S5_GROUP, n * S5_GROUP), 1)
    rows = []
    for s in range(n):
        f = klag[0] if s == 0 else jnp.where(lane >= S5_GROUP * s, pltpu.roll(klag[0], S5_GROUP * s, 1), 0.0)
        sh = (n * S5_GROUP - S5_GROUP * (n - 1 - s)) % (n * S5_GROUP)
        b = klag[1] if sh == 0 else pltpu.roll(klag[1], sh, 1)
        rows.append(f + jnp.where(lane < S5_GROUP * (s + 1), b, 0.0))
    toep = jnp.concatenate(rows, axis=0)

    spread = spread_ref[...]
    wt_ref[...] = _dot(toep.astype(BF16), spread).astype(BF16).reshape(n, S5_GROUP, S5_SLAB_W)
    wca_ref[...] = _dot(ca.astype(BF16), spread).astype(BF16)
    a16_ref[0:1, :] = tab_re[n:n + 1, :]
    a16_ref[1:2, :] = tab_im[n:n + 1, :]
    web_ref[...] = jnp.zeros_like(web_ref)
    eb3 = eb.astype(BF16).reshape(n, S5_GROUP, 256)
    for pos in range(S5_SLAB_W // 256):
        @pl.when(pl.program_id(0) == pos)
        def _():
            web_ref[:, :, pos * 256:(pos + 1) * 256] = eb3


def _s5_params(lam_re, lam_im, log_step, b_re, b_im, c_re, c_im):
    par = jnp.stack([lam_re, lam_im, log_step], axis=1).astype(F32)
    par = par.transpose(0, 3, 1, 2, 4).reshape(DEPTH, S5_GROUPS, 3, 128)
    par = jnp.pad(par, ((0, 0), (0, 0), (0, 5), (0, 0)))
    b_t = lambda b: b.astype(F32).transpose(0, 2, 4, 1, 3).reshape(DEPTH, S5_GROUPS, S5_GROUP, 128)
    c_t = lambda c: c.astype(F32).transpose(0, 2, 3, 1, 4).reshape(DEPTH, S5_GROUPS, S5_GROUP, 128)
    return par, b_t(b_re), b_t(b_im), c_t(c_re), c_t(c_im)


def _s5_prep(params, d_skip, layer):
    seg = 8
    spread = _s5_expanders()
    vec = pl.BlockSpec((None, None, S5_GROUP, 128), lambda gl, j: (layer, j * seg + gl, 0, 0))
    exp_spec = pl.BlockSpec((None, 256, S5_SLAB_W), lambda gl, j: (gl, 0, 0))
    rows_spec = pl.BlockSpec((None, S5_CHUNK, None, S5_GROUP, S5_SLAB_W), lambda gl, j: (j, 0, gl, 0, 0))
    wt, web, wca, a16 = pl.pallas_call(
        _s5_prep_kernel,
        grid=(seg, S5_SLABS),
        in_specs=[pl.BlockSpec((None, None, 8, 128), lambda gl, j: (layer, j * seg + gl, 0, 0)),
                  vec, vec, vec, vec, exp_spec],
        out_specs=[
            rows_spec, rows_spec,
            pl.BlockSpec((None, None, 256, S5_SLAB_W), lambda gl, j: (j, gl, 0, 0)),
            pl.BlockSpec((None, 2, 128), lambda gl, j: (j * seg + gl, 0, 0)),
        ],
        out_shape=[
            jax.ShapeDtypeStruct((S5_SLABS, S5_CHUNK, seg, S5_GROUP, S5_SLAB_W), BF16),
            jax.ShapeDtypeStruct((S5_SLABS, S5_CHUNK, seg, S5_GROUP, S5_SLAB_W), BF16),
            jax.ShapeDtypeStruct((S5_SLABS, seg, 256, S5_SLAB_W), BF16),
            jax.ShapeDtypeStruct((S5_GROUPS, 2, 128), F32),
        ],
        name="s5_prep",
    )(*params, jnp.asarray(spread, BF16))
    mat = (S5_SLABS, S5_SLAB_W, S5_SLAB_W)
    dj = jnp.tile(d_skip.astype(F32).reshape(S5_SLABS, 1, 128), (1, 1, S5_CHUNK))
    return wt.reshape(mat), web.reshape(mat), wca.reshape(mat), a16.reshape(1, S5_SLABS * S5_SLAB_W), dj


S5_STATE_COLS = S5_SLABS * S5_SLAB_W // 128
S5_SLAB_COLS = S5_SLAB_W // 128


@functools.lru_cache(maxsize=None)
def _s5_row_perms():
    assert S5_ROW_TILE == S5_ROWS_C == DEC_BATCH * TM // S5_CHUNK
    perm = np.zeros((3, S5_ROW_TILE, S5_ROW_TILE), np.float32)
    for p, (nseq, nchunk) in enumerate(((BATCH, SEQ // S5_CHUNK), (DEC_BATCH, TM // S5_CHUNK),
                                        (DEC_BATCH, TM // S5_CHUNK))):
        b, c = np.meshgrid(np.arange(nseq), np.arange(nchunk), indexing="ij")
        perm[p, (c * nseq + b).ravel(), (b * nchunk + c).ravel()] = 1.0
    return perm, perm.transpose(0, 2, 1).copy()


def _s5_state_kernel(u_ref, perm_ref, w_ref, o_ref):
    u = _dot(perm_ref[...], u_ref[...].astype(BF16)).astype(BF16)
    s = _dot(u, w_ref[...])
    for k in range(S5_SLAB_COLS):
        o_ref[k] = s[:, k * 128:(k + 1) * 128]


def _s5_state(uj, web):
    perm, _ = _s5_row_perms()
    return pl.pallas_call(
        _s5_state_kernel,
        grid=(S5_SLABS, S5_ROWS // S5_ROW_TILE),
        in_specs=[
            pl.BlockSpec((None, S5_ROW_TILE, S5_SLAB_W), lambda j, p: (j, p, 0)),
            pl.BlockSpec((None, S5_ROW_TILE, S5_ROW_TILE), lambda j, p: (p, 0, 0)),
            pl.BlockSpec((None, S5_SLAB_W, S5_SLAB_W), lambda j, p: (j, 0, 0)),
        ],
        out_specs=pl.BlockSpec((S5_SLAB_COLS, S5_ROW_TILE, 128), lambda j, p: (j, p, 0)),
        out_shape=jax.ShapeDtypeStruct((S5_STATE_COLS, S5_ROWS, 128), F32),
        compiler_params=pltpu.CompilerParams(vmem_limit_bytes=VMEM_LIMIT),
        name="s5_state",
    )(uj, jnp.asarray(perm, BF16), web)


S5_SCAN_COLS = 8


def _s5_scan_kernel(s_ref, a_ref, h0_ref, hin_ref, fin_ref, hf, hb):
    ncol = S5_SCAN_COLS

    def scan(row0, nc, nb, h0):
        is_f = lax.broadcasted_iota(jnp.int32, (nb, 128), 1) < S5_STATE
        chunk_rows = lambda c: pl.ds(pl.multiple_of(row0 + c * nb, 8), nb)

        def body(c, hs):
            rf = chunk_rows(c)
            rb = chunk_rows(nc - 1 - c)
            new = []
            for m in range(ncol // 2):
                h_re, h_im = hs[2 * m], hs[2 * m + 1]
                a_re = a_ref[:, (2 * m) * 128:(2 * m + 1) * 128]
                a_im = a_ref[:, (2 * m + 1) * 128:(2 * m + 2) * 128]
                loc = []
                for k, h in ((2 * m, h_re), (2 * m + 1, h_im)):
                    hf[k, rf, :] = h
                    hb[k, rb, :] = h
                    loc.append(jnp.where(is_f, s_ref[k, rf, :], s_ref[k, rb, :]))
                new.append(a_re * h_re - a_im * h_im + loc[0])
                new.append(a_re * h_im + a_im * h_re + loc[1])
            return tuple(new)

        return lax.fori_loop(0, nc, body, h0)

    fin = scan(0, SEQ // S5_CHUNK, BATCH, tuple(jnp.zeros((BATCH, 128), F32) for _ in range(ncol)))
    for k in range(ncol):
        fin_ref[:, k * 128:(k + 1) * 128] = fin[k]
    scan(S5_ROWS_C, DEC_SEQ // S5_CHUNK, DEC_BATCH,
         tuple(h0_ref[:, k * 128:(k + 1) * 128] for k in range(ncol)))
    fwd = lax.broadcasted_iota(jnp.int32, (ncol, S5_ROWS, 128), 2) < S5_STATE
    hin_ref[...] = jnp.where(fwd, hf[...], hb[...]).astype(BF16)


def _s5_scan(sloc, a16, h0l):
    ncol = S5_SCAN_COLS
    w = ncol * 128
    return pl.pallas_call(
        _s5_scan_kernel,
        grid=(S5_STATE_COLS // ncol,),
        in_specs=[
            pl.BlockSpec((ncol, S5_ROWS, 128), lambda k: (k, 0, 0)),
            pl.BlockSpec((1, w), lambda k: (0, k)),
            pl.BlockSpec((DEC_BATCH, w), lambda k: (0, k)),
        ],
        out_specs=[
            pl.BlockSpec((ncol, S5_ROWS, 128), lambda k: (k, 0, 0)),
            pl.BlockSpec((BATCH, w), lambda k: (0, k)),
        ],
        out_shape=[
            jax.ShapeDtypeStruct((S5_STATE_COLS, S5_ROWS, 128), BF16),
            jax.ShapeDtypeStruct((BATCH, S5_STATE_COLS * 128), F32),
        ],
        scratch_shapes=[pltpu.VMEM((ncol, S5_ROWS, 128), F32)] * 2,
        name="s5_scan",
    )(sloc, a16, h0l)


def _s5_out_kernel(u_ref, hin_ref, perm_t_ref, wt_ref, wca_ref, d_ref, y_ref):
    u = u_ref[...]
    hin = jnp.concatenate([hin_ref[k] for k in range(S5_SLAB_COLS)], axis=1).astype(BF16)
    hin = _dot(perm_t_ref[...], hin).astype(BF16)
    y = _dot(u.astype(BF16), wt_ref[...]) + _dot(hin, wca_ref[...]) + u * d_ref[...]
    for t in range(S5_CHUNK):
        y_ref[pl.ds(t, S5_ROW_TILE, stride=S5_CHUNK), :] = y[:, t * 128:(t + 1) * 128]


def _s5_out(uj, hin, wt, wca, dj):
    _, perm_t = _s5_row_perms()
    return pl.pallas_call(
        _s5_out_kernel,
        grid=(S5_SLABS, S5_ROWS // S5_ROW_TILE),
        in_specs=[
            pl.BlockSpec((None, S5_ROW_TILE, S5_SLAB_W), lambda j, p: (j, p, 0)),
            pl.BlockSpec((S5_SLAB_COLS, S5_ROW_TILE, 128), lambda j, p: (j, p, 0)),
            pl.BlockSpec((None, S5_ROW_TILE, S5_ROW_TILE), lambda j, p: (p, 0, 0)),
            pl.BlockSpec((None, S5_SLAB_W, S5_SLAB_W), lambda j, p: (j, 0, 0)),
            pl.BlockSpec((None, S5_SLAB_W, S5_SLAB_W), lambda j, p: (j, 0, 0)),
            pl.BlockSpec((None, 1, S5_SLAB_W), lambda j, p: (j, 0, 0)),
        ],
        out_specs=pl.BlockSpec((None, S5_ROW_TILE * S5_CHUNK, 128), lambda j, p: (j, p, 0)),
        out_shape=jax.ShapeDtypeStruct((S5_SLABS, NTOK, 128), F32),
        compiler_params=pltpu.CompilerParams(vmem_limit_bytes=VMEM_LIMIT),
        name="s5_out",
    )(uj, hin, jnp.asarray(perm_t, BF16), wt, wca, dj)


@functools.lru_cache(maxsize=None)
def _gla_consts():
    n = GLA_BLK
    nl = GLA_LEVELS
    r = np.arange(n)
    up = np.zeros((n, 128), np.int32)
    for l in range(nl):
        up[:, l] = (r >> l) & 1
    i = r[:, None]
    j = r[None, :]
    x = np.maximum(i ^ j, 1)
    lev = np.where(j < i, np.floor(np.log2(x)).astype(np.int32), np.where(i == j, nl, -1)).astype(np.int32)
    up2 = np.stack([up, up[::-1]])
    h = n // 2

    def tiled(a):
        return np.stack([np.concatenate([a[:h, :h], a[h:, h:]]), np.concatenate([a[:h, h:], a[h:, :h]])])

    lev2 = np.stack([tiled(lev), tiled(lev[::-1, ::-1])])
    return up2, lev2


@functools.lru_cache(maxsize=None)
def _gla_tables():
    rowblk, seq, first, last = [], [], [], []
    for d in range(2):
        rb, sq, fi, la = [], [], [], []
        for s in range(BATCH + DEC_BATCH):
            nblk = 1 if s < BATCH else DEC_SEQ // GLA_BLK
            base = s if s < BATCH else NTOK_C // GLA_BLK + (s - BATCH) * nblk
            order = range(nblk) if d == 0 else range(nblk - 1, -1, -1)
            for pos, b in enumerate(order):
                rb.append(base + b)
                sq.append(s)
                fi.append(int(pos == 0))
                la.append(int(pos == nblk - 1))
        rowblk.append(rb); seq.append(sq); first.append(fi); last.append(la)
    as_np = lambda a: np.asarray(a, np.int32)
    return as_np(rowblk), as_np(seq), as_np(first), as_np(last)


def _gla_kernel(rowblk_ref, seq_ref, first_ref, last_ref,
                qf_ref, kf_ref, vf_ref, lrf_ref, qb_ref, kb_ref, vb_ref, lrb_ref,
                wgk_ref, bgk_ref, up_ref, lev_ref, s0_ref,
                of_ref, ob_ref, fin_ref, z_scr, st_scr):
    del rowblk_ref
    n = pl.program_id(0)

    @pl.when(first_ref[n] == 1)
    def _():
        latent = seq_ref[n] >= BATCH
        st_scr[...] = jnp.zeros_like(st_scr)
        for d in range(2):
            for h in range(GLA_HEADS):
                st_scr[d, h * GLA_DK:(h + 1) * GLA_DK, h * GLA_DV:(h + 1) * GLA_DV] = jnp.where(
                    latent, s0_ref[d, h], 0.0)

    blocks = [
        _gla_block(False, qf_ref, kf_ref, vf_ref, lrf_ref, wgk_ref.at[0], bgk_ref.at[0], up_ref.at[0],
                   lev_ref.at[0], of_ref, z_scr.at[0], st_scr.at[0]),
        _gla_block(True, qb_ref, kb_ref, vb_ref, lrb_ref, wgk_ref.at[1], bgk_ref.at[1], up_ref.at[1],
                   lev_ref.at[1], ob_ref, z_scr.at[1], st_scr.at[1]),
    ]
    for stage in range(2):
        for block in blocks:
            next(block, None)

    @pl.when(last_ref[n] == 1)
    def _():
        for d in range(2):
            for h in range(GLA_HEADS):
                fin_ref[d, h] = st_scr[d, h * GLA_DK:(h + 1) * GLA_DK, h * GLA_DV:(h + 1) * GLA_DV]


def _gla_block(backward, q_ref, k_ref, v_ref, lr_ref, wgk_ref, bgk_ref, up_ref, lev_ref, o_ref, z_scr, st_scr):
    nl = GLA_LEVELS
    blk = GLA_BLK
    q = q_ref[...] * (GLA_DK ** -0.5)
    k = k_ref[...]
    vb = v_ref[...].astype(BF16)
    x = _dot(lr_ref[...].astype(BF16), wgk_ref[...]) + bgk_ref[...]
    gk = (jnp.minimum(x, 0.0) - jnp.log(1.0 + jnp.exp(-jnp.abs(x)))) * (1.0 / GLA_NORMALIZER)
    g_hi = gk.astype(BF16)
    g_lo = (gk - g_hi.astype(F32)).astype(BF16)
    ones = jnp.ones((blk, 128), BF16)
    tot = _dot_tn(g_hi, ones) + _dot_tn(g_lo, ones)

    row = lax.broadcasted_iota(jnp.int32, (blk, 1), 0)

    def sibling(a, l):
        g = 1 << l
        if g < 8:
            a3 = a.reshape(blk // 8, 8, a.shape[-1])
            dn = pltpu.roll(a3, g, 1).reshape(a.shape)
            up_ = pltpu.roll(a3, 8 - g, 1).reshape(a.shape)
            return jnp.where(((row >> l) & 1) == 1, dn, up_)
        a4 = a.reshape(blk // (2 * g), 2, g, a.shape[-1])
        return jnp.concatenate([a4[:, 1:2], a4[:, 0:1]], axis=1).reshape(a.shape)

    part = gk
    total = gk
    z0 = None
    for l in range(nl):
        g = 1 << l
        if g < 8:
            up = up_ref[:, l:l + 1] != 0
            z = jnp.where(up, q, k) * jnp.exp(jnp.where(up, part, total - part))
            other = sibling(total, l)
            part = part + jnp.where(up, other, 0.0)
            total = total + other
        else:
            halves = lambda a: (a.reshape(blk // (2 * g), 2, g, a.shape[-1])[:, 1 - int(backward)],
                                a.reshape(blk // (2 * g), 2, g, a.shape[-1])[:, int(backward)])
            join = lambda u, d: jnp.stack([d, u] if not backward else [u, d], axis=1).reshape(blk, u.shape[-1])
            part_u, part_d = halves(part)
            tot_u, tot_d = halves(total)
            q_u, _ = halves(q)
            _, k_d = halves(k)
            z = join(q_u * jnp.exp(part_u), k_d * jnp.exp(tot_d - part_d))
            part = join(part_u + tot_d, part_d)
            both = tot_u + tot_d
            total = join(both, both)
        if l == 0:
            z0 = z
        else:
            z_scr[l] = z.astype(BF16)
    yield
    lane128 = lax.broadcasted_iota(jnp.int32, (GLA_QK, 128), 1)
    dim = lax.broadcasted_iota(jnp.int32, (GLA_QK, 128), 0)
    head_sum = ((dim >> 6) == lane128).astype(BF16)
    pair0 = _dot((z0 * sibling(z0, 0)).astype(BF16), head_sum)
    diag = _dot((q * k).astype(BF16), head_sum)

    half = blk // 2
    lev_d = lev_ref[0]
    lev_o = lev_ref[1]
    lane = lax.broadcasted_iota(jnp.int32, (half, GLA_QK), 1)

    def tiles(l, in_head, crossed):
        out = []
        for r in range(2):
            c = 1 - r if crossed else r
            lhs = z_scr[l, r * half:(r + 1) * half, :]
            keys = z_scr[l, c * half:(c + 1) * half, :]
            out.append(_dot_nt(lhs, jnp.where(in_head, keys, jnp.zeros_like(keys))))
        return jnp.concatenate(out, axis=0)

    upi = 0 if backward else 1
    key_lanes = {}
    for l in range(3, nl - 1):
        g = 1 << l
        c = lax.broadcasted_iota(jnp.int32, (blk // (2 * g), g, 128), 0)
        ln = lax.broadcasted_iota(jnp.int32, (blk // (2 * g), g, 128), 2)
        base = (2 * g * c + (g if backward else 0)) & 127
        key_lanes[l] = (ln >= base) & (ln < base + g)

    heads = range(GLA_HEADS)
    in_head = [(lane >= h * GLA_DK) & (lane < (h + 1) * GLA_DK) for h in heads]
    acc = [jnp.where(lev_d == 0, pair0[:, h:h + 1], jnp.where(lev_d == nl, diag[:, h:h + 1], 0.0)) for h in heads]
    for l in range(1, 3):
        acc = [jnp.where(lev_d == l, tiles(l, in_head[h], False), acc[h]) for h in heads]
    for l in range(3, nl - 1):
        g = 1 << l
        for h in heads:
            acc4 = acc[h].reshape(blk // (2 * g), 2, g, 128)
            s4 = tiles(l, in_head[h], False).reshape(blk // (2 * g), 2, g, 128)
            new_up = jnp.where(key_lanes[l], s4[:, upi], acc4[:, upi])
            pieces = [acc4[:, 0], new_up] if upi == 1 else [new_up, acc4[:, 1]]
            acc[h] = jnp.stack(pieces, axis=1).reshape(blk, 128)
    off = [jnp.where(lev_o == nl - 1, tiles(nl - 1, in_head[h], True), 0.0) for h in heads]
    for h in heads:
        att = jnp.concatenate([jnp.concatenate([acc[h][:half], off[h][:half]], axis=1),
                               jnp.concatenate([off[h][half:], acc[h][half:]], axis=1)], axis=0)
        o_ref[:, h * GLA_DV:(h + 1) * GLA_DV] = _dot(att.astype(BF16), vb[:, h * GLA_DV:(h + 1) * GLA_DV])

    st = st_scr[...]
    q_in = (q * jnp.exp(part)).astype(BF16)
    o_ref[...] += _dot(q_in, st.astype(BF16))
    k_out = (k * jnp.exp(total - part)).astype(BF16)
    kv = _dot_tn(k_out, vb)
    row = lax.broadcasted_iota(jnp.int32, (GLA_QK, GLA_V), 0)
    col = lax.broadcasted_iota(jnp.int32, (GLA_QK, GLA_V), 1)
    same_head = (row >> 6) == (col >> 7)
    decay = jnp.exp(tot)
    decay = jnp.concatenate([decay] * GLA_HEADS, axis=1)
    st_new = decay * st + jnp.where(same_head, kv, 0.0)
    st_scr[...] = st_new


def _gla_mix(bslab, lr, wgk, bgk, state_gla, layer):
    up, lev = _gla_consts()
    rowblk, seq, first, last = _gla_tables()
    nsteps = rowblk.shape[1]
    nseq = BATCH + DEC_BATCH
    nl = GLA_LEVELS
    whole = lambda shape: pl.BlockSpec(shape, lambda n, rb, sq, fi, la: (0,) * len(shape))

    def token_specs(d):
        return [
            pl.BlockSpec((GLA_BLK, GLA_QK), lambda n, rb, sq, fi, la: (rb[d, n], 0)),
            pl.BlockSpec((GLA_BLK, GLA_QK), lambda n, rb, sq, fi, la: (rb[d, n], 1)),
            pl.BlockSpec((GLA_BLK, GLA_V), lambda n, rb, sq, fi, la: (rb[d, n], 1)),
            pl.BlockSpec((GLA_BLK, 128), lambda n, rb, sq, fi, la: (rb[d, n], 0)),
        ]

    state_spec = pl.BlockSpec((None, 2, GLA_HEADS, GLA_DK, GLA_DV), lambda n, rb, sq, fi, la: (sq[n], 0, 0, 0, 0))
    of_layer = lambda shape: pl.BlockSpec((None,) + shape, lambda n, rb, sq, fi, la: (layer,) + (0,) * len(shape))
    grid_spec = pltpu.PrefetchScalarGridSpec(
        num_scalar_prefetch=4,
        grid=(nsteps,),
        in_specs=token_specs(0) + token_specs(1) + [
            of_layer((2, 128, GLA_QK)),
            of_layer((2, 1, GLA_QK)),
            whole((2, GLA_BLK, 128)),
            whole((2, 2, GLA_BLK, GLA_BLK // 2)),
            pl.BlockSpec((None, None, 2, GLA_HEADS, GLA_DK, GLA_DV),
                         lambda n, rb, sq, fi, la: (jnp.maximum(sq[n] - BATCH, 0), layer, 0, 0, 0, 0)),
        ],
        out_specs=[
            pl.BlockSpec((GLA_BLK, GLA_V), lambda n, rb, sq, fi, la: (rb[0, n], 0)),
            pl.BlockSpec((GLA_BLK, GLA_V), lambda n, rb, sq, fi, la: (rb[1, n], 0)),
            state_spec,
        ],
        scratch_shapes=[
            pltpu.VMEM((2, nl, GLA_BLK, GLA_QK), BF16),
            pltpu.VMEM((2, GLA_QK, GLA_V), F32),
        ],
    )
    return pl.pallas_call(
        _gla_kernel,
        grid_spec=grid_spec,
        out_shape=[
            jax.ShapeDtypeStruct((NTOK, GLA_V), F32),
            jax.ShapeDtypeStruct((NTOK, GLA_V), F32),
            jax.ShapeDtypeStruct((nseq, 2, GLA_HEADS, GLA_DK, GLA_DV), F32),
        ],
        compiler_params=pltpu.CompilerParams(vmem_limit_bytes=VMEM_LIMIT),
        name="gla_mix",
    )(jnp.asarray(rowblk), jnp.asarray(seq[0]), jnp.asarray(first[0]), jnp.asarray(last[0]),
      bslab, bslab, bslab, lr, bslab, bslab, bslab, lr, wgk, bgk, jnp.asarray(up), jnp.asarray(lev), state_gla)


def _attn_ctx_kernel(sink_ref, q_ref, k_ref, v_ref, o_ref, ko_ref, vo_ref, *, layer):
    k = k_ref[...]
    v = v_ref[...]
    ko_ref[...] = k
    vo_ref[...] = v
    ks = (k.astype(BF16), pltpu.roll(k, 64, 1).astype(BF16))
    vs = (v.astype(BF16), pltpu.roll(v, 64, 1).astype(BF16))
    lo = lax.broadcasted_iota(jnp.int32, (SEQ, 128), 1) < HEAD_DIM
    units = []
    for t in range(ATT_HEADS // 2):
        qt = q_ref[:, t * 128:(t + 1) * 128] * (HEAD_DIM ** -0.5)
        for p in range(2):
            qm = jnp.where(lo if p == 0 else jnp.logical_not(lo), qt, 0.0).astype(BF16)
            units.append((qm, 0 if p == t // 2 else 1, sink_ref[layer, 2 * t + p]))
    scores = [_dot_nt(qm, ks[which]) for qm, which, _ in units]
    maxes = [jnp.maximum(sink, jnp.max(s, axis=-1, keepdims=True)) for s, (_, _, sink) in zip(scores, units)]
    probs = [jnp.exp(s - m) for s, m in zip(scores, maxes)]
    dens = [jnp.exp(sink - m) + jnp.sum(p, axis=-1, keepdims=True)
            for p, m, (_, _, sink) in zip(probs, maxes, units)]
    outs = [_dot(p.astype(BF16), vs[which]) / den for p, den, (_, which, _) in zip(probs, dens, units)]
    for t in range(ATT_HEADS // 2):
        o_ref[:, t * 128:(t + 1) * 128] = jnp.where(lo, outs[2 * t], outs[2 * t + 1]).astype(BF16)


def _attn_ctx(sink, cslab, layer):
    kv_out = pl.BlockSpec((None, SEQ, ATT_KV), lambda b: (b, 0, 0))
    return pl.pallas_call(
        functools.partial(_attn_ctx_kernel, layer=layer),
        grid=(BATCH,),
        in_specs=[
            pl.BlockSpec(memory_space=pltpu.SMEM),
            pl.BlockSpec((SEQ, ATT_Q), lambda b: (b, 0)),
            pl.BlockSpec((SEQ, ATT_KV), lambda b: (b, 4)),
            pl.BlockSpec((SEQ, ATT_KV), lambda b: (b, 5)),
        ],
        out_specs=[pl.BlockSpec((SEQ, ATT_Q), lambda b: (b, 0)), kv_out, kv_out],
        out_shape=[jax.ShapeDtypeStruct((NTOK_C, ATT_Q), BF16),
                   jax.ShapeDtypeStruct((BATCH, SEQ, ATT_KV), F32),
                   jax.ShapeDtypeStruct((BATCH, SEQ, ATT_KV), F32)],
        name="attn_ctx",
    )(sink, cslab, cslab, cslab)


def _attn_lat_kernel(sink_ref, q_ref, kp_ref, kc_ref, kn_ref, vp_ref, vc_ref, vn_ref,
                     ck_ref, cv_ref, cos_ref, sin_ref, bias_ref, o_ref, *, layer):
    j = pl.program_id(1)
    nb = DEC_SEQ // ATT_BLOCK
    lane = lax.broadcasted_iota(jnp.int32, (ATT_BLOCK, 128), 1)
    lo = lane < HEAD_DIM
    first16 = (lane & 31) < 16

    def rope(x, blk_idx):
        r0 = pl.multiple_of(blk_idx * ATT_BLOCK, ATT_BLOCK)
        c = cos_ref[pl.ds(r0, ATT_BLOCK), :]
        s = sin_ref[pl.ds(r0, ATT_BLOCK), :]
        xs = jnp.where(first16, pltpu.roll(x, 112, 1), pltpu.roll(x, 16, 1))
        return x * c + xs * s

    nwin = 3 * ATT_BLOCK
    keys = jnp.concatenate([rope(kp_ref[...], jnp.maximum(j - 1, 0)), rope(kc_ref[...], j),
                            rope(kn_ref[...], jnp.minimum(j + 1, nb - 1)), ck_ref[...]], axis=0)
    vals = jnp.concatenate([vp_ref[...], vc_ref[...], vn_ref[...], cv_ref[...]], axis=0)
    keys2 = (keys.astype(BF16), pltpu.roll(keys, 64, 1).astype(BF16))
    vals2 = (vals.astype(BF16), pltpu.roll(vals, 64, 1).astype(BF16))
    kcol = lax.broadcasted_iota(jnp.int32, (1, nwin + PAST_LEN), 1)
    edge = jnp.where(((j == 0) & (kcol < ATT_BLOCK)) | ((j == nb - 1) & (kcol >= 2 * ATT_BLOCK) & (kcol < nwin)),
                     -1e30, 0.0)
    bias = bias_ref[...] + edge
    top = lax.broadcasted_iota(jnp.int32, (2 * ATT_BLOCK, 1), 0) < ATT_BLOCK
    q_tiles = [rope(q_ref[:, t * 128:(t + 1) * 128], j) * (HEAD_DIM ** -0.5) for t in range(ATT_HEADS // 2)]
    lo2 = jnp.concatenate([lo, lo], axis=0)
    units = []
    for kvh in range(ATT_KV_HEADS):
        q2 = jnp.concatenate(q_tiles[2 * kvh:2 * kvh + 2], axis=0)
        for p in range(2):
            qm = jnp.where(lo2 if p == 0 else jnp.logical_not(lo2), q2, 0.0).astype(BF16)
            sink = jnp.where(top, sink_ref[layer, 4 * kvh + p], sink_ref[layer, 4 * kvh + 2 + p])
            units.append((qm, 0 if p == kvh else 1, sink))
    scores = [_dot_nt(qm, keys2[which]) + bias for qm, which, _ in units]
    maxes = [jnp.maximum(sink, jnp.max(s, axis=-1, keepdims=True)) for s, (_, _, sink) in zip(scores, units)]
    probs = [jnp.exp(s - m) for s, m in zip(scores, maxes)]
    dens = [jnp.exp(sink - m) + jnp.sum(p, axis=-1, keepdims=True)
            for p, m, (_, _, sink) in zip(probs, maxes, units)]
    outs = [_dot(p.astype(BF16), vals2[which]) / den for p, den, (_, which, _) in zip(probs, dens, units)]
    for kvh in range(ATT_KV_HEADS):
        o2 = jnp.where(lo2, outs[2 * kvh], outs[2 * kvh + 1])
        for i in range(2):
            t = 2 * kvh + i
            o_ref[:, t * 128:(t + 1) * 128] = o2[i * ATT_BLOCK:(i + 1) * ATT_BLOCK].astype(BF16)


def _attn_lat(sink, cslab, ck, cv, cos_t, sin_t, layer):
    nb = DEC_SEQ // ATT_BLOCK
    base = NTOK_C // ATT_BLOCK
    cur = lambda b, j: base + b * nb + j
    prv = lambda b, j: base + b * nb + jnp.maximum(j - 1, 0)
    nxt = lambda b, j: base + b * nb + jnp.minimum(j + 1, nb - 1)
    kv_spec = lambda row, col: pl.BlockSpec((ATT_BLOCK, ATT_KV), lambda b, j: (row(b, j), col))
    qi = np.arange(2 * ATT_BLOCK)[:, None] % ATT_BLOCK
    kc = np.arange(3 * ATT_BLOCK + PAST_LEN)[None, :]
    inside = (np.abs(kc - ATT_BLOCK - qi) <= WINDOW) | (kc >= 3 * ATT_BLOCK)
    band = np.where(inside, 0.0, -1e30).astype(np.float32)
    cache_spec = pl.BlockSpec((None, None, PAST_LEN, ATT_KV), lambda b, j: (b, layer, 0, 0))
    return pl.pallas_call(
        functools.partial(_attn_lat_kernel, layer=layer),
        grid=(DEC_BATCH, nb),
        in_specs=[
            pl.BlockSpec(memory_space=pltpu.SMEM),
            pl.BlockSpec((ATT_BLOCK, ATT_Q), lambda b, j: (cur(b, j), 0)),
            kv_spec(prv, 4), kv_spec(cur, 4), kv_spec(nxt, 4),
            kv_spec(prv, 5), kv_spec(cur, 5), kv_spec(nxt, 5),
            cache_spec, cache_spec,
            pl.BlockSpec((DEC_SEQ, 128), lambda b, j: (0, 0)),
            pl.BlockSpec((DEC_SEQ, 128), lambda b, j: (0, 0)),
            pl.BlockSpec(band.shape, lambda b, j: (0, 0)),
        ],
        out_specs=pl.BlockSpec((ATT_BLOCK, ATT_Q), lambda b, j: (b * nb + j, 0)),
        out_shape=jax.ShapeDtypeStruct((NTOK_L, ATT_Q), BF16),
        name="attn_lat",
    )(sink, cslab, cslab, cslab, cslab, cslab, cslab, cslab, ck, cv, cos_t, sin_t, jnp.asarray(band))


@functools.lru_cache(maxsize=None)
def _rope_tables():
    rows = DEC_SEQ // GRID_W
    row = np.repeat(np.arange(rows, dtype=np.float64), GRID_W)
    col = np.tile(np.arange(GRID_W, dtype=np.float64), rows)
    quarter = HEAD_DIM // 4
    inv = ROPE_BASE ** (-np.arange(quarter, dtype=np.float64) / quarter)
    lane = np.arange(128)
    use_row = (lane % HEAD_DIM) < HEAD_DIM // 2
    pos = np.where(use_row[None, :], row[:, None], col[:, None])
    ang = pos * inv[lane % quarter][None, :]
    sign = np.where((lane % 32) < 16, -1.0, 1.0)
    return np.cos(ang).astype(np.float32), (np.sin(ang) * sign[None, :]).astype(np.float32)


def _merge_kernel(*refs, split_x):
    if split_x:
        xc_ref, xl_ref, *refs = refs
    else:
        xc_ref, *refs = refs
    (mod_ref, g_ref, ys5_ref, ogf_ref, ogb_ref, gb_ref, ycc_ref, ycl_ref, gate_ref, gng_ref,
     wglu_ref, wbr_ref, wout_ref, o_ref) = refs
    is_ctx = pl.program_id(0) < NTOK_C // TM
    if split_x:
        x = jnp.where(is_ctx, xc_ref[...], xl_ref[...])
    else:
        x = xc_ref[...]
    y = jnp.concatenate([ys5_ref[j] for j in range(S5_SLABS)], axis=1)
    y = 0.5 * y * (1.0 + jnp.tanh(math.sqrt(2.0 / math.pi) * (y + 0.044715 * (y * y * y))))
    ag = _dot(y.astype(BF16), wglu_ref[...])
    y_a = ag[:, :S5_WIDTH] * _sigmoid(ag[:, S5_WIDTH:])
    gng = gng_ref[...]
    parts = []
    for h in range(GLA_HEADS):
        sl = slice(h * GLA_DV, (h + 1) * GLA_DV)
        o = ogf_ref[:, sl] + ogb_ref[:, sl]
        g = gb_ref[:, sl]
        parts.append(_rms(o, gng) * (g * _sigmoid(g)))
    y_b = jnp.concatenate(parts, axis=1)
    y_c = jnp.where(is_ctx, ycc_ref[...], ycl_ref[...])
    merged = None
    for n, yn in enumerate((y_a, y_b, y_c)):
        proj = _dot(yn.astype(BF16), wbr_ref[n])
        term = gate_ref[:, n * D_MODEL:(n + 1) * D_MODEL].astype(F32) * proj
        merged = term if merged is None else merged + term
    mixed = _dot(merged.astype(BF16), wout_ref[...])
    g1 = mod_ref[:, 2 * D_MODEL:3 * D_MODEL]
    o_ref[...] = x + g1 * _rms(mixed, g_ref[...])


def _layer_spec(shape, layer):
    return pl.BlockSpec((None,) + shape, lambda i: (layer,) + (0,) * len(shape), pipeline_mode=pl.Buffered(1))


def _split_token_specs(n_arrays, width=D_MODEL):
    nct = NTOK_C // TM
    if n_arrays == 2:
        return [pl.BlockSpec((TM, width), lambda i: (jnp.minimum(i, nct - 1), 0)),
                pl.BlockSpec((TM, width), lambda i: (jnp.maximum(i - nct, 0), 0))]
    return [pl.BlockSpec((TM, width), lambda i: (i, 0))]


def _merge(xs, mod, g, ys5, og, bslab, yc, gates, gng, wglu, wbr, wout, layer):
    tok = lambda width, col=0: pl.BlockSpec((TM, width), lambda i: (i, col))
    full = lambda shape: _layer_spec(shape, layer)
    return pl.pallas_call(
        functools.partial(_merge_kernel, split_x=len(xs) == 2),
        grid=(NTOK // TM,),
        in_specs=_split_token_specs(len(xs)) + [
            _mod_spec(layer),
            _gain_spec(layer, 1),
            pl.BlockSpec((S5_SLABS, TM, 128), lambda i: (0, _s5_tile(i), 0)),
            tok(GLA_V),
            tok(GLA_V),
            tok(GLA_V, 2),
        ] + _split_token_specs(2, ATT_Q) + [
            tok(N_BRANCH * D_MODEL),
            pl.BlockSpec((None, 1, GLA_DV), lambda i: (layer, 0, 0)),
            full((S5_WIDTH, 2 * S5_WIDTH)),
            full((N_BRANCH, BRANCH_W, D_MODEL)),
            full((D_MODEL, D_MODEL)),
        ],
        out_specs=tok(D_MODEL),
        out_shape=jax.ShapeDtypeStruct((NTOK, D_MODEL), F32),
        compiler_params=pltpu.CompilerParams(vmem_limit_bytes=VMEM_LIMIT),
        name="merge",
    )(*xs, mod, g, ys5, *og, bslab, *yc, gates, gng, wglu, wbr, wout)


FFN_SPLIT = 1


def _ffn_kernel(x_ref, mod_ref, gin_ref, gout_ref, w1_ref, w2_ref, *o_refs):
    x = x_ref[...]
    sh = mod_ref[:, 3 * D_MODEL:4 * D_MODEL]
    sc = mod_ref[:, 4 * D_MODEL:5 * D_MODEL]
    g2 = mod_ref[:, 5 * D_MODEL:6 * D_MODEL]
    h = (_rms(x, gin_ref[...]) * (1.0 + sc) + sh).astype(BF16)
    ck = FFN_HIDDEN // FFN_SPLIT
    acc = None
    for c in range(FFN_SPLIT):
        a = _dot(h, w1_ref[:, c * ck:(c + 1) * ck])
        b = _dot(h, w1_ref[:, FFN_HIDDEN + c * ck:FFN_HIDDEN + (c + 1) * ck])
        act = (a * _sigmoid(a) * b).astype(BF16)
        part = _dot(act, w2_ref[c * ck:(c + 1) * ck, :])
        acc = part if acc is None else acc + part
    y = x + g2 * _rms(acc, gout_ref[...])
    if len(o_refs) == 1:
        o_refs[0][...] = y
    else:
        is_ctx = pl.program_id(0) < NTOK_C // TM

        @pl.when(is_ctx)
        def _():
            o_refs[0][...] = y

        @pl.when(jnp.logical_not(is_ctx))
        def _():
            o_refs[1][...] = y


def _ffn(x, mod, gains, w1, w2, layer, split_out):
    nct = NTOK_C // TM
    if split_out:
        out_specs = [pl.BlockSpec((TM, D_MODEL), lambda i: (jnp.minimum(i, nct - 1), 0)),
                     pl.BlockSpec((TM, D_MODEL), lambda i: (jnp.maximum(i - nct, 0), 0))]
        out_shape = [jax.ShapeDtypeStruct((NTOK_C, D_MODEL), F32), jax.ShapeDtypeStruct((NTOK_L, D_MODEL), F32)]
    else:
        out_specs = pl.BlockSpec((TM, D_MODEL), lambda i: (i, 0))
        out_shape = jax.ShapeDtypeStruct((NTOK, D_MODEL), F32)
    return pl.pallas_call(
        _ffn_kernel,
        grid=(NTOK // TM,),
        in_specs=[
            pl.BlockSpec((TM, D_MODEL), lambda i: (i, 0)),
            _mod_spec(layer),
            _gain_spec(layer, 2),
            _gain_spec(layer, 3),
            _layer_spec((D_MODEL, 2 * FFN_HIDDEN), layer),
            _layer_spec((FFN_HIDDEN, D_MODEL), layer),
        ],
        out_specs=out_specs,
        out_shape=out_shape,
        compiler_params=pltpu.CompilerParams(vmem_limit_bytes=VMEM_LIMIT),
        name="ffn",
    )(x, mod, gains, gains, w1, w2)


def kernel(x_prompt, x_sample, cache_k, cache_v, state_s5, state_gla, c, c_ctx, w_mod, b_mod, norm_g, w_in,
           s5_lam_re, s5_lam_im, s5_log_step, s5_b_re, s5_b_im, s5_c_re, s5_c_im, s5_d, w_glu, gla_w_gk,
           gla_b_gk, gla_norm_g, att_sink, w_branch, w_out, w_ffn_in, w_ffn_out):
    cond = jnp.concatenate([c_ctx[None, :], c, jnp.zeros((N_MOD_ROWS - 1 - DEC_BATCH, D_MODEL), F32)], axis=0)
    mod_all = _modulation(cond, w_mod, b_mod).reshape(DEPTH, N_MOD_ROWS, 1, 6 * D_MODEL)
    cos_t, sin_t = _rope_tables()
    xs = (x_prompt.reshape(NTOK_C, D_MODEL), x_sample.reshape(NTOK_L, D_MODEL))
    w_in_b = w_in.astype(BF16)
    w_in_end = jnp.pad(w_in[:, :, D_IN_TILED:].astype(BF16), ((0, 0), (0, 0), (0, W_IN_COLS - D_IN)))
    w_glu_b, w_branch_b, w_out_b = w_glu.astype(BF16), w_branch.astype(BF16), w_out.astype(BF16)
    w_ffn_in_b, w_ffn_out_b = w_ffn_in.astype(BF16), w_ffn_out.astype(BF16)
    s5_params = _s5_params(s5_lam_re, s5_lam_im, s5_log_step, s5_b_re, s5_b_im, s5_c_re, s5_c_im)
    h0_all = state_s5.astype(F32).transpose(1, 0, 3, 5, 2, 4).reshape(DEPTH, DEC_BATCH, S5_GROUPS * 256)
    wgk_all = jnp.stack([jnp.pad(gla_w_gk[:, d], ((0, 0), (d * GLA_RANK, 128 - (d + 1) * GLA_RANK), (0, 0)))
                         for d in range(2)], axis=1).astype(BF16)
    bgk_all = gla_b_gk[:, :, None, :].astype(F32)
    gains = norm_g.astype(F32).reshape(DEPTH * 4, 1, D_MODEL)
    gla_gain = gla_norm_g.astype(F32).reshape(DEPTH, 1, GLA_DV)
    sink = att_sink.astype(F32)
    cache_k2 = cache_k.astype(F32).reshape(DEC_BATCH, DEPTH, PAST_LEN, ATT_KV)
    cache_v2 = cache_v.astype(F32).reshape(DEC_BATCH, DEPTH, PAST_LEN, ATT_KV)
    state_gla = state_gla.astype(F32)
    mod = mod_all
    new_k, new_v, new_s5, new_gla = [], [], [], []
    for i in range(DEPTH):
        uj, bslab, cslab, gates, lr = _inproj(xs, mod, gains, w_in_b, w_in_end, i)

        wt, web, wca, a16, dj = _s5_prep(s5_params, s5_d[i], i)
        hin, finc = _s5_scan(_s5_state(uj, web), a16, h0_all[i])
        ys5 = _s5_out(uj, hin, wt, wca, dj)
        new_s5.append(finc)

        *og, gla_fin = _gla_mix(bslab, lr, wgk_all, bgk_all, state_gla, i)
        new_gla.append(gla_fin[:BATCH])

        yc_ctx, k_new, v_new = _attn_ctx(sink, cslab, i)
        yc = (yc_ctx, _attn_lat(sink, cslab, cache_k2, cache_v2, cos_t, sin_t, i))
        new_k.append(k_new.reshape(BATCH, SEQ, ATT_KV_HEADS, HEAD_DIM))
        new_v.append(v_new.reshape(BATCH, SEQ, ATT_KV_HEADS, HEAD_DIM))

        x = _merge(xs, mod, gains, ys5, og, bslab, yc, gates, gla_gain, w_glu_b, w_branch_b, w_out_b, i)
        last = i == DEPTH - 1
        x = _ffn(x, mod, gains, w_ffn_in_b, w_ffn_out_b, i, last)
        xs = tuple(x) if last else (x,)

    return (xs[0].reshape(BATCH, SEQ, D_MODEL), xs[1].reshape(DEC_BATCH, DEC_SEQ, D_MODEL),
            jnp.stack(new_k, axis=1), jnp.stack(new_v, axis=1),
            jnp.stack(new_s5).reshape(DEPTH, BATCH, S5_GROUPS, 2, 2, S5_STATE).transpose(1, 0, 4, 2, 5, 3),
            jnp.stack(new_gla, axis=1))
```

```python
import functools
import math

import numpy as np
import jax
import jax.numpy as jnp
from jax import lax
from jax.experimental import pallas as pl
from jax.experimental.pallas import tpu as pltpu

F32 = jnp.float32
BF16 = jnp.bfloat16

D_MODEL = 1024
BATCH = 16
SEQ = 256
DEPTH = 2
DEC_BATCH = 8
DEC_SEQ = 1024
PAST_LEN = 256
GRID_W = 64
ROPE_BASE = 10000.0
S5_WIDTH = 512
S5_GROUP = 16
S5_GROUPS = 32
S5_STATE = 64
GLA_HEADS = 4
GLA_DK = 64
GLA_DV = 128
GLA_QK = 256
GLA_V = 512
GLA_RANK = 16
GLA_NORMALIZER = 16.0
ATT_HEADS = 8
ATT_KV_HEADS = 2
HEAD_DIM = 64
ATT_Q = 512
ATT_KV = 128
WINDOW = 128
ATT_BLOCK = 128
N_BRANCH = 3
BRANCH_W = 512
FFN_HIDDEN = 2816
RMS_EPS = 1e-6

NTOK_C = BATCH * SEQ
NTOK_L = DEC_BATCH * DEC_SEQ
NTOK = NTOK_C + NTOK_L
TM = 512
N_MOD_ROWS = 16

D_IN = 5920
D_IN_TILED = D_IN // 128 * 128
W_IN_COLS = 6016
S5_CHUNK = 16
S5_SLABS = S5_WIDTH // 128
S5_SLAB_W = S5_CHUNK * 128
S5_ROWS_C = NTOK_C // S5_CHUNK
S5_ROWS = NTOK // S5_CHUNK
S5_ROW_TILE = 256
GLA_BLK = 256
GLA_LEVELS = 8
VMEM_LIMIT = 56 * 1024 * 1024


def _dot(a, b):
    return jnp.dot(a, b, preferred_element_type=F32)


def _dot_nt(a, b):
    return lax.dot_general(a, b, (((1,), (1,)), ((), ())), preferred_element_type=F32)


def _dot_tn(a, b):
    return lax.dot_general(a, b, (((0,), (0,)), ((), ())), preferred_element_type=F32)


def _rms(x, g):
    return x * lax.rsqrt(jnp.mean(x * x, axis=-1, keepdims=True) + RMS_EPS) * g


def _sigmoid(x):
    return 0.5 * jnp.tanh(0.5 * x) + 0.5


def _mod_row(i):
    nct = NTOK_C // TM
    return jnp.where(i < nct, 0, 1 + (i - nct) // (DEC_SEQ // TM))


def _mod_spec(layer):
    return pl.BlockSpec((None, None, 1, 6 * D_MODEL), lambda i: (layer, _mod_row(i), 0, 0))


def _gain_spec(layer, k):
    return pl.BlockSpec((None, 1, D_MODEL), lambda i: (layer * 4 + k, 0, 0))


def _s5_tile(i):
    nct = NTOK_C // TM
    per_seq = DEC_SEQ // TM
    k = i - nct
    return jnp.where(i < nct, i, nct + (k % per_seq) * DEC_BATCH + k // per_seq)


def _mod_kernel(c_ref, w_ref, b_ref, o_ref):
    c = c_ref[...]
    s = (c * _sigmoid(c)).astype(BF16)
    o_ref[...] = _dot(s, w_ref[...].astype(BF16)) + b_ref[...]


def _modulation(cond, w_mod, b_mod):
    tn = 2048
    return pl.pallas_call(
        _mod_kernel,
        grid=(DEPTH, 6 * D_MODEL // tn),
        in_specs=[
            pl.BlockSpec((N_MOD_ROWS, D_MODEL), lambda l, n: (0, 0)),
            pl.BlockSpec((None, D_MODEL, tn), lambda l, n: (l, 0, n)),
            pl.BlockSpec((None, 1, tn), lambda l, n: (l, 0, n)),
        ],
        out_specs=pl.BlockSpec((None, N_MOD_ROWS, tn), lambda l, n: (l, 0, n)),
        out_shape=jax.ShapeDtypeStruct((DEPTH, N_MOD_ROWS, 6 * D_MODEL), F32),
        name="modulation",
    )(cond, w_mod, b_mod.reshape(DEPTH, 1, 6 * D_MODEL))


_IN_SLABS = ((0, 512), (512, 1536), (2048, 768), (2816, 3072), (5888, 128))
W_IN_SPLIT = 2048
W_IN_GAP = 32
W_IN_TAIL = W_IN_COLS - W_IN_SPLIT


def _inproj_kernel(*refs, split_x):
    if split_x:
        xc_ref, xl_ref, *refs = refs
    else:
        xc_ref, *refs = refs
    mod_ref, g_ref, w_ref, w_end_ref, u_ref, b_ref, c_ref, gate_ref, lr_ref, w_tail, u_stage = refs
    i = pl.program_id(0)

    @pl.when(i == 0)
    def _():
        r = lax.broadcasted_iota(jnp.int32, (256, 128), 0)
        c = lax.broadcasted_iota(jnp.int32, (256, 128), 1)
        shift = (r == c + W_IN_GAP).astype(BF16)
        head = ((r == c) & (c < W_IN_GAP)).astype(BF16)
        ntile = (W_IN_TAIL - 128) // 128
        for t in range(ntile - 1):
            src = W_IN_SPLIT + 128 * t
            w_tail[:, 128 * t:128 * (t + 1)] = _dot(w_ref[:, src:src + 256], shift).astype(BF16)
        src = W_IN_SPLIT + 128 * (ntile - 1)
        last = jnp.concatenate([w_ref[:, src:src + 128], w_end_ref[...]], axis=1)
        w_tail[:, 128 * (ntile - 1):128 * ntile] = _dot(last, shift).astype(BF16)
        w_tail[:, 128 * ntile:] = _dot(w_ref[:, W_IN_SPLIT:W_IN_SPLIT + 256], head).astype(BF16)

    if split_x:
        x = jnp.where(i < NTOK_C // TM, xc_ref[...], xl_ref[...])
    else:
        x = xc_ref[...]
    mod = mod_ref[...]
    h = _rms(x, g_ref[...]) * (1.0 + mod[:, D_MODEL:2 * D_MODEL]) + mod[:, 0:D_MODEL]
    h = h.astype(BF16)
    z_head = _dot(h, w_ref[:, 0:W_IN_SPLIT])
    z_tail = _dot(h, w_tail[...])
    for j in range(S5_SLABS):
        u_stage[...] = z_head[:, j * 128:(j + 1) * 128]
        for s in range(S5_CHUNK):
            u_ref[j, :, s * 128:(s + 1) * 128] = u_stage[pl.ds(s, TM // S5_CHUNK, stride=S5_CHUNK), :]
    b_ref[...] = z_head[:, S5_WIDTH:W_IN_SPLIT]
    for (off, width), o_ref in zip(_IN_SLABS[2:], (c_ref, gate_ref, lr_ref)):
        z = z_tail[:, off - W_IN_SPLIT:off - W_IN_SPLIT + width]
        o_ref[...] = _sigmoid(z).astype(BF16) if o_ref is gate_ref else z


def _inproj(xs, mod, g, w_all, w_end, layer):
    return pl.pallas_call(
        functools.partial(_inproj_kernel, split_x=len(xs) == 2),
        grid=(NTOK // TM,),
        in_specs=_split_token_specs(len(xs)) + [
            _mod_spec(layer),
            _gain_spec(layer, 0),
            pl.BlockSpec((None, D_MODEL, D_IN), lambda i: (layer, 0, 0), pipeline_mode=pl.Buffered(1)),
            pl.BlockSpec((None, D_MODEL, 128), lambda i: (layer, 0, 0), pipeline_mode=pl.Buffered(1)),
        ],
        out_specs=[pl.BlockSpec((S5_SLABS, TM // S5_CHUNK, S5_SLAB_W), lambda i: (0, _s5_tile(i), 0))]
        + [pl.BlockSpec((TM, width), lambda i: (i, 0)) for _, width in _IN_SLABS[1:]],
        out_shape=[jax.ShapeDtypeStruct((S5_SLABS, S5_ROWS, S5_SLAB_W), F32)]
        + [jax.ShapeDtypeStruct((NTOK, width), BF16 if width == N_BRANCH * D_MODEL else F32)
           for _, width in _IN_SLABS[1:]],
        scratch_shapes=[pltpu.VMEM((D_MODEL, W_IN_TAIL), BF16), pltpu.VMEM((TM, 128), F32)],
        compiler_params=pltpu.CompilerParams(vmem_limit_bytes=VMEM_LIMIT),
        name="inproj",
    )(*xs, mod, g, w_all, w_end)


@functools.lru_cache(maxsize=None)
def _s5_expanders():
    seg = 8
    spread = np.zeros((seg, 256, S5_SLAB_W), np.float32)
    col = np.arange(256)
    for gl in range(seg):
        spread[gl, col, (col // S5_GROUP) * 128 + gl * S5_GROUP + col % S5_GROUP] = 1.0
    return spread


def _s5_prep_kernel(par_ref, bre_ref, bim_ref, cre_ref, cim_ref, spread_ref,
                    wt_ref, web_ref, wca_ref, a16_ref):
    n = S5_CHUNK
    lam_re = par_ref[0:1, :]
    lam_im = par_ref[1:2, :]
    dt = jnp.exp(par_ref[2:3, :])
    lr = lam_re * dt
    li = lam_im * dt
    krow = lax.broadcasted_iota(jnp.int32, (24, 128), 0).astype(F32)
    tab_mag = jnp.exp(krow * lr)
    tab_re = tab_mag * jnp.cos(krow * li)
    tab_im = tab_mag * jnp.sin(krow * li)
    ar = tab_re[1:2, :]
    ai = tab_im[1:2, :]
    nr = ar - 1.0
    den = lam_re * lam_re + lam_im * lam_im
    fr = (nr * lam_re + ai * lam_im) / den
    fi = (ai * lam_re - nr * lam_im) / den
    b_re = bre_ref[...]
    b_im = bim_ref[...]
    br = fr * b_re - fi * b_im
    bi = fr * b_im + fi * b_re
    c_re = cre_ref[...]
    c_im = cim_ref[...]

    def lo_half(shape):
        return lax.broadcasted_iota(jnp.int32, shape, 1) < S5_STATE

    def tile_rows(a):
        return jnp.concatenate([a] * n, axis=0)

    fwd16 = lo_half((S5_GROUP, 128))

    def powers(t_re, t_im, k_fwd, k_bwd):
        def pick(t, b):
            kf, kb = k_fwd(b), k_bwd(b)
            return jnp.where(fwd16, jnp.broadcast_to(t[kf:kf + 1, :], (S5_GROUP, 128)),
                             jnp.broadcast_to(t[kb:kb + 1, :], (S5_GROUP, 128)))
        return (jnp.concatenate([pick(t_re, b) for b in range(n)], axis=0),
                jnp.concatenate([pick(t_im, b) for b in range(n)], axis=0))

    fwd = lo_half((n * S5_GROUP, 128))
    brt, bit, crt, cit = tile_rows(br), tile_rows(bi), tile_rows(c_re), tile_rows(c_im)

    per, pei = powers(tab_re, tab_im, lambda s: n - 1 - s, lambda s: s)
    eb = jnp.concatenate([brt * per - bit * pei, brt * pei + bit * per], axis=1)
    pcr, pci = powers(tab_re, tab_im, lambda t: t + 1, lambda t: n - t)
    ca = jnp.concatenate([(crt * pcr - cit * pci).T, (-(crt * pci + cit * pcr)).T], axis=0)

    def one_dir(x, d):
        sw = pltpu.roll(x, S5_STATE, 1)
        lo = lo_half(x.shape)
        return jnp.where(lo, x, sw) if d == 0 else jnp.where(lo, sw, x)

    klag = []
    for d in range(2):
        lhs = jnp.where(lo_half(br.shape), one_dir(br, d), -one_dir(bi, d))
        crd, cid = tile_rows(one_dir(c_re, d)), tile_rows(one_dir(c_im, d))
        lag = (lambda b: b) if d == 0 else (lambda b: n - 1 - b)
        pr, pi = powers(one_dir(tab_re, d), one_dir(tab_im, d), lag, lag)
        rhs_t = jnp.where(fwd, crd * pr - cid * pi, crd * pi + cid * pr)
        klag.append(lax.dot_general(lhs, rhs_t, (((1,), (1,)), ((), ())),
                                    precision=lax.Precision.HIGHEST, preferred_element_type=F32))
    lane = lax.broadcasted_iota(jnp.int32, (S5_GROUP, n * S5_GROUP), 1)
    rows = []
    for s in range(n):
        f = klag[0] if s == 0 else jnp.where(lane >= S5_GROUP * s, pltpu.roll(klag[0], S5_GROUP * s, 1), 0.0)
        sh = (n * S5_GROUP - S5_GROUP * (n - 1 - s)) % (n * S5_GROUP)
        b = klag[1] if sh == 0 else pltpu.roll(klag[1], sh, 1)
        rows.append(f + jnp.where(lane < S5_GROUP * (s + 1), b, 0.0))
    toep = jnp.concatenate(rows, axis=0)

    spread = spread_ref[...]
    wt_ref[...] = _dot(toep.astype(BF16), spread).astype(BF16).reshape(n, S5_GROUP, S5_SLAB_W)
    wca_ref[...] = _dot(ca.astype(BF16), spread).astype(BF16)
    a16_ref[0:1, :] = tab_re[n:n + 1, :]
    a16_ref[1:2, :] = tab_im[n:n + 1, :]
    web_ref[...] = jnp.zeros_like(web_ref)
    eb3 = eb.astype(BF16).reshape(n, S5_GROUP, 256)
    for pos in range(S5_SLAB_W // 256):
        @pl.when(pl.program_id(0) == pos)
        def _():
            web_ref[:, :, pos * 256:(pos + 1) * 256] = eb3


def _s5_params(lam_re, lam_im, log_step, b_re, b_im, c_re, c_im):
    par = jnp.stack([lam_re, lam_im, log_step], axis=1).astype(F32)
    par = par.transpose(0, 3, 1, 2, 4).reshape(DEPTH, S5_GROUPS, 3, 128)
    par = jnp.pad(par, ((0, 0), (0, 0), (0, 5), (0, 0)))
    b_t = lambda b: b.astype(F32).transpose(0, 2, 4, 1, 3).reshape(DEPTH, S5_GROUPS, S5_GROUP, 128)
    c_t = lambda c: c.astype(F32).transpose(0, 2, 3, 1, 4).reshape(DEPTH, S5_GROUPS, S5_GROUP, 128)
    return par, b_t(b_re), b_t(b_im), c_t(c_re), c_t(c_im)


def _s5_prep(params, d_skip, layer):
    seg = 8
    spread = _s5_expanders()
    vec = pl.BlockSpec((None, None, S5_GROUP, 128), lambda gl, j: (layer, j * seg + gl, 0, 0))
    exp_spec = pl.BlockSpec((None, 256, S5_SLAB_W), lambda gl, j: (gl, 0, 0))
    rows_spec = pl.BlockSpec((None, S5_CHUNK, None, S5_GROUP, S5_SLAB_W), lambda gl, j: (j, 0, gl, 0, 0))
    wt, web, wca, a16 = pl.pallas_call(
        _s5_prep_kernel,
        grid=(seg, S5_SLABS),
        in_specs=[pl.BlockSpec((None, None, 8, 128), lambda gl, j: (layer, j * seg + gl, 0, 0)),
                  vec, vec, vec, vec, exp_spec],
        out_specs=[
            rows_spec, rows_spec,
            pl.BlockSpec((None, None, 256, S5_SLAB_W), lambda gl, j: (j, gl, 0, 0)),
            pl.BlockSpec((None, 2, 128), lambda gl, j: (j * seg + gl, 0, 0)),
        ],
        out_shape=[
            jax.ShapeDtypeStruct((S5_SLABS, S5_CHUNK, seg, S5_GROUP, S5_SLAB_W), BF16),
            jax.ShapeDtypeStruct((S5_SLABS, S5_CHUNK, seg, S5_GROUP, S5_SLAB_W), BF16),
            jax.ShapeDtypeStruct((S5_SLABS, seg, 256, S5_SLAB_W), BF16),
            jax.ShapeDtypeStruct((S5_GROUPS, 2, 128), F32),
        ],
        name="s5_prep",
    )(*params, jnp.asarray(spread, BF16))
    mat = (S5_SLABS, S5_SLAB_W, S5_SLAB_W)
    dj = jnp.tile(d_skip.astype(F32).reshape(S5_SLABS, 1, 128), (1, 1, S5_CHUNK))
    return wt.reshape(mat), web.reshape(mat), wca.reshape(mat), a16.reshape(1, S5_SLABS * S5_SLAB_W), dj


S5_STATE_COLS = S5_SLABS * S5_SLAB_W // 128
S5_SLAB_COLS = S5_SLAB_W // 128


@functools.lru_cache(maxsize=None)
def _s5_row_perms():
    assert S5_ROW_TILE == S5_ROWS_C == DEC_BATCH * TM // S5_CHUNK
    perm = np.zeros((3, S5_ROW_TILE, S5_ROW_TILE), np.float32)
    for p, (nseq, nchunk) in enumerate(((BATCH, SEQ // S5_CHUNK), (DEC_BATCH, TM // S5_CHUNK),
                                        (DEC_BATCH, TM // S5_CHUNK))):
        b, c = np.meshgrid(np.arange(nseq), np.arange(nchunk), indexing="ij")
        perm[p, (c * nseq + b).ravel(), (b * nchunk + c).ravel()] = 1.0
    return perm, perm.transpose(0, 2, 1).copy()


def _s5_state_kernel(u_ref, perm_ref, w_ref, o_ref):
    u = _dot(perm_ref[...], u_ref[...].astype(BF16)).astype(BF16)
    s = _dot(u, w_ref[...])
    for k in range(S5_SLAB_COLS):
        o_ref[k] = s[:, k * 128:(k + 1) * 128]


def _s5_state(uj, web):
    perm, _ = _s5_row_perms()
    return pl.pallas_call(
        _s5_state_kernel,
        grid=(S5_SLABS, S5_ROWS // S5_ROW_TILE),
        in_specs=[
            pl.BlockSpec((None, S5_ROW_TILE, S5_SLAB_W), lambda j, p: (j, p, 0)),
            pl.BlockSpec((None, S5_ROW_TILE, S5_ROW_TILE), lambda j, p: (p, 0, 0)),
            pl.BlockSpec((None, S5_SLAB_W, S5_SLAB_W), lambda j, p: (j, 0, 0)),
        ],
        out_specs=pl.BlockSpec((S5_SLAB_COLS, S5_ROW_TILE, 128), lambda j, p: (j, p, 0)),
        out_shape=jax.ShapeDtypeStruct((S5_STATE_COLS, S5_ROWS, 128), F32),
        compiler_params=pltpu.CompilerParams(vmem_limit_bytes=VMEM_LIMIT),
        name="s5_state",
    )(uj, jnp.asarray(perm, BF16), web)


S5_SCAN_COLS = 8


def _s5_scan_kernel(s_ref, a_ref, h0_ref, hin_ref, fin_ref, hf, hb):
    ncol = S5_SCAN_COLS

    def scan(row0, nc, nb, h0):
        is_f = lax.broadcasted_iota(jnp.int32, (nb, 128), 1) < S5_STATE
        chunk_rows = lambda c: pl.ds(pl.multiple_of(row0 + c * nb, 8), nb)

        def body(c, hs):
            rf = chunk_rows(c)
            rb = chunk_rows(nc - 1 - c)
            new = []
            for m in range(ncol // 2):
                h_re, h_im = hs[2 * m], hs[2 * m + 1]
                a_re = a_ref[:, (2 * m) * 128:(2 * m + 1) * 128]
                a_im = a_ref[:, (2 * m + 1) * 128:(2 * m + 2) * 128]
                loc = []
                for k, h in ((2 * m, h_re), (2 * m + 1, h_im)):
                    hf[k, rf, :] = h
                    hb[k, rb, :] = h
                    loc.append(jnp.where(is_f, s_ref[k, rf, :], s_ref[k, rb, :]))
                new.append(a_re * h_re - a_im * h_im + loc[0])
                new.append(a_re * h_im + a_im * h_re + loc[1])
            return tuple(new)

        return lax.fori_loop(0, nc, body, h0)

    fin = scan(0, SEQ // S5_CHUNK, BATCH, tuple(jnp.zeros((BATCH, 128), F32) for _ in range(ncol)))
    for k in range(ncol):
        fin_ref[:, k * 128:(k + 1) * 128] = fin[k]
    scan(S5_ROWS_C, DEC_SEQ // S5_CHUNK, DEC_BATCH,
         tuple(h0_ref[:, k * 128:(k + 1) * 128] for k in range(ncol)))
    fwd = lax.broadcasted_iota(jnp.int32, (ncol, S5_ROWS, 128), 2) < S5_STATE
    hin_ref[...] = jnp.where(fwd, hf[...], hb[...]).astype(BF16)


def _s5_scan(sloc, a16, h0l):
    ncol = S5_SCAN_COLS
    w = ncol * 128
    return pl.pallas_call(
        _s5_scan_kernel,
        grid=(S5_STATE_COLS // ncol,),
        in_specs=[
            pl.BlockSpec((ncol, S5_ROWS, 128), lambda k: (k, 0, 0)),
            pl.BlockSpec((1, w), lambda k: (0, k)),
            pl.BlockSpec((DEC_BATCH, w), lambda k: (0, k)),
        ],
        out_specs=[
            pl.BlockSpec((ncol, S5_ROWS, 128), lambda k: (k, 0, 0)),
            pl.BlockSpec((BATCH, w), lambda k: (0, k)),
        ],
        out_shape=[
            jax.ShapeDtypeStruct((S5_STATE_COLS, S5_ROWS, 128), BF16),
            jax.ShapeDtypeStruct((BATCH, S5_STATE_COLS * 128), F32),
        ],
        scratch_shapes=[pltpu.VMEM((ncol, S5_ROWS, 128), F32)] * 2,
        name="s5_scan",
    )(sloc, a16, h0l)


def _s5_out_kernel(u_ref, hin_ref, perm_t_ref, wt_ref, wca_ref, d_ref, y_ref):
    u = u_ref[...]
    hin = jnp.concatenate([hin_ref[k] for k in range(S5_SLAB_COLS)], axis=1).astype(BF16)
    hin = _dot(perm_t_ref[...], hin).astype(BF16)
    y = _dot(u.astype(BF16), wt_ref[...]) + _dot(hin, wca_ref[...]) + u * d_ref[...]
    for t in range(S5_CHUNK):
        y_ref[pl.ds(t, S5_ROW_TILE, stride=S5_CHUNK), :] = y[:, t * 128:(t + 1) * 128]


def _s5_out(uj, hin, wt, wca, dj):
    _, perm_t = _s5_row_perms()
    return pl.pallas_call(
        _s5_out_kernel,
        grid=(S5_SLABS, S5_ROWS // S5_ROW_TILE),
        in_specs=[
            pl.BlockSpec((None, S5_ROW_TILE, S5_SLAB_W), lambda j, p: (j, p, 0)),
            pl.BlockSpec((S5_SLAB_COLS, S5_ROW_TILE, 128), lambda j, p: (j, p, 0)),
            pl.BlockSpec((None, S5_ROW_TILE, S5_ROW_TILE), lambda j, p: (p, 0, 0)),
            pl.BlockSpec((None, S5_SLAB_W, S5_SLAB_W), lambda j, p: (j, 0, 0)),
            pl.BlockSpec((None, S5_SLAB_W, S5_SLAB_W), lambda j, p: (j, 0, 0)),
            pl.BlockSpec((None, 1, S5_SLAB_W), lambda j, p: (j, 0, 0)),
        ],
        out_specs=pl.BlockSpec((None, S5_ROW_TILE * S5_CHUNK, 128), lambda j, p: (j, p, 0)),
        out_shape=jax.ShapeDtypeStruct((S5_SLABS, NTOK, 128), F32),
        compiler_params=pltpu.CompilerParams(vmem_limit_bytes=VMEM_LIMIT),
        name="s5_out",
    )(uj, hin, jnp.asarray(perm_t, BF16), wt, wca, dj)


@functools.lru_cache(maxsize=None)
def _gla_consts():
    n = GLA_BLK
    nl = GLA_LEVELS
    r = np.arange(n)
    up = np.zeros((n, 128), np.int32)
    for l in range(nl):
        up[:, l] = (r >> l) & 1
    i = r[:, None]
    j = r[None, :]
    x = np.maximum(i ^ j, 1)
    lev = np.where(j < i, np.floor(np.log2(x)).astype(np.int32), np.where(i == j, nl, -1)).astype(np.int32)
    up2 = np.stack([up, up[::-1]])
    h = n // 2

    def tiled(a):
        return np.stack([np.concatenate([a[:h, :h], a[h:, h:]]), np.concatenate([a[:h, h:], a[h:, :h]])])

    lev2 = np.stack([tiled(lev), tiled(lev[::-1, ::-1])])
    return up2, lev2


@functools.lru_cache(maxsize=None)
def _gla_tables():
    rowblk, seq, first, last = [], [], [], []
    for d in range(2):
        rb, sq, fi, la = [], [], [], []
        for s in range(BATCH + DEC_BATCH):
            nblk = 1 if s < BATCH else DEC_SEQ // GLA_BLK
            base = s if s < BATCH else NTOK_C // GLA_BLK + (s - BATCH) * nblk
            order = range(nblk) if d == 0 else range(nblk - 1, -1, -1)
            for pos, b in enumerate(order):
                rb.append(base + b)
                sq.append(s)
                fi.append(int(pos == 0))
                la.append(int(pos == nblk - 1))
        rowblk.append(rb); seq.append(sq); first.append(fi); last.append(la)
    as_np = lambda a: np.asarray(a, np.int32)
    return as_np(rowblk), as_np(seq), as_np(first), as_np(last)


def _gla_kernel(rowblk_ref, seq_ref, first_ref, last_ref,
                qf_ref, kf_ref, vf_ref, lrf_ref, qb_ref, kb_ref, vb_ref, lrb_ref,
                wgk_ref, bgk_ref, up_ref, lev_ref, s0_ref,
                of_ref, ob_ref, fin_ref, z_scr, st_scr):
    del rowblk_ref
    n = pl.program_id(0)

    @pl.when(first_ref[n] == 1)
    def _():
        latent = seq_ref[n] >= BATCH
        st_scr[...] = jnp.zeros_like(st_scr)
        for d in range(2):
            for h in range(GLA_HEADS):
                st_scr[d, h * GLA_DK:(h + 1) * GLA_DK, h * GLA_DV:(h + 1) * GLA_DV] = jnp.where(
                    latent, s0_ref[d, h], 0.0)

    blocks = [
        _gla_block(False, qf_ref, kf_ref, vf_ref, lrf_ref, wgk_ref.at[0], bgk_ref.at[0], up_ref.at[0],
                   lev_ref.at[0], of_ref, z_scr.at[0], st_scr.at[0]),
        _gla_block(True, qb_ref, kb_ref, vb_ref, lrb_ref, wgk_ref.at[1], bgk_ref.at[1], up_ref.at[1],
                   lev_ref.at[1], ob_ref, z_scr.at[1], st_scr.at[1]),
    ]
    for stage in range(2):
        for block in blocks:
            next(block, None)

    @pl.when(last_ref[n] == 1)
    def _():
        for d in range(2):
            for h in range(GLA_HEADS):
                fin_ref[d, h] = st_scr[d, h * GLA_DK:(h + 1) * GLA_DK, h * GLA_DV:(h + 1) * GLA_DV]


def _gla_block(backward, q_ref, k_ref, v_ref, lr_ref, wgk_ref, bgk_ref, up_ref, lev_ref, o_ref, z_scr, st_scr):
    nl = GLA_LEVELS
    blk = GLA_BLK
    q = q_ref[...] * (GLA_DK ** -0.5)
    k = k_ref[...]
    vb = v_ref[...].astype(BF16)
    x = _dot(lr_ref[...].astype(BF16), wgk_ref[...]) + bgk_ref[...]
    gk = (jnp.minimum(x, 0.0) - jnp.log(1.0 + jnp.exp(-jnp.abs(x)))) * (1.0 / GLA_NORMALIZER)

    row = lax.broadcasted_iota(jnp.int32, (blk, 1), 0)

    def sibling(a, l):
        g = 1 << l
        if g < 8:
            a3 = a.reshape(blk // 8, 8, a.shape[-1])
            dn = pltpu.roll(a3, g, 1).reshape(a.shape)
            up_ = pltpu.roll(a3, 8 - g, 1).reshape(a.shape)
            return jnp.where(((row >> l) & 1) == 1, dn, up_)
        a4 = a.reshape(blk // (2 * g), 2, g, a.shape[-1])
        return jnp.concatenate([a4[:, 1:2], a4[:, 0:1]], axis=1).reshape(a.shape)

    part = gk
    total = gk
    z0 = None
    for l in range(nl):
        g = 1 << l
        if g < 8:
            up = up_ref[:, l:l + 1] != 0
            z = jnp.where(up, q, k) * jnp.exp(jnp.where(up, part, total - part))
            other = sibling(total, l)
            part = part + jnp.where(up, other, 0.0)
            total = total + other
        else:
            halves = lambda a: (a.reshape(blk // (2 * g), 2, g, a.shape[-1])[:, 1 - int(backward)],
                                a.reshape(blk // (2 * g), 2, g, a.shape[-1])[:, int(backward)])
            join = lambda u, d: jnp.stack([d, u] if not backward else [u, d], axis=1).reshape(blk, u.shape[-1])
            part_u, part_d = halves(part)
            tot_u, tot_d = halves(total)
            q_u, _ = halves(q)
            _, k_d = halves(k)
            z = join(q_u * jnp.exp(part_u), k_d * jnp.exp(tot_d - part_d))
            part = join(part_u + tot_d, part_d)
            both = tot_u + tot_d
            total = join(both, both)
        if l == 0:
            z0 = z
        else:
            z_scr[l] = z.astype(BF16)
    yield
    lane128 = lax.broadcasted_iota(jnp.int32, (GLA_QK, 128), 1)
    dim = lax.broadcasted_iota(jnp.int32, (GLA_QK, 128), 0)
    head_sum = ((dim >> 6) == lane128).astype(BF16)
    pair0 = _dot((z0 * sibling(z0, 0)).astype(BF16), head_sum)
    diag = _dot((q * k).astype(BF16), head_sum)

    half = blk // 2
    lev_d = lev_ref[0]
    lev_o = lev_ref[1]
    lane = lax.broadcasted_iota(jnp.int32, (half, GLA_QK), 1)

    def tiles(l, in_head, crossed):
        out = []
        for r in range(2):
            c = 1 - r if crossed else r
            lhs = z_scr[l, r * half:(r + 1) * half, :]
            keys = z_scr[l, c * half:(c + 1) * half, :]
            out.append(_dot_nt(lhs, jnp.where(in_head, keys, jnp.zeros_like(keys))))
        return jnp.concatenate(out, axis=0)

    upi = 0 if backward else 1
    key_lanes = {}
    for l in range(3, nl - 1):
        g = 1 << l
        c = lax.broadcasted_iota(jnp.int32, (blk // (2 * g), g, 128), 0)
        ln = lax.broadcasted_iota(jnp.int32, (blk // (2 * g), g, 128), 2)
        base = (2 * g * c + (g if backward else 0)) & 127
        key_lanes[l] = (ln >= base) & (ln < base + g)

    heads = range(GLA_HEADS)
    in_head = [(lane >= h * GLA_DK) & (lane < (h + 1) * GLA_DK) for h in heads]
    acc = [jnp.where(lev_d == 0, pair0[:, h:h + 1], jnp.where(lev_d == nl, diag[:, h:h + 1], 0.0)) for h in heads]
    for l in range(1, 3):
        acc = [jnp.where(lev_d == l, tiles(l, in_head[h], False), acc[h]) for h in heads]
    for l in range(3, nl - 1):
        g = 1 << l
        for h in heads:
            acc4 = acc[h].reshape(blk // (2 * g), 2, g, 128)
            s4 = tiles(l, in_head[h], False).reshape(blk // (2 * g), 2, g, 128)
            new_up = jnp.where(key_lanes[l], s4[:, upi], acc4[:, upi])
            pieces = [acc4[:, 0], new_up] if upi == 1 else [new_up, acc4[:, 1]]
            acc[h] = jnp.stack(pieces, axis=1).reshape(blk, 128)
    off = [jnp.where(lev_o == nl - 1, tiles(nl - 1, in_head[h], True), 0.0) for h in heads]
    for h in heads:
        att = jnp.concatenate([jnp.concatenate([acc[h][:half], off[h][:half]], axis=1),
                               jnp.concatenate([off[h][half:], acc[h][half:]], axis=1)], axis=0)
        o_ref[:, h * GLA_DV:(h + 1) * GLA_DV] = _dot(att.astype(BF16), vb[:, h * GLA_DV:(h + 1) * GLA_DV])

    st = st_scr[...]
    q_in = (q * jnp.exp(part)).astype(BF16)
    o_ref[...] += _dot(q_in, st.astype(BF16))
    k_out = (k * jnp.exp(total - part)).astype(BF16)
    kv = _dot_tn(k_out, vb)
    row = lax.broadcasted_iota(jnp.int32, (GLA_QK, GLA_V), 0)
    col = lax.broadcasted_iota(jnp.int32, (GLA_QK, GLA_V), 1)
    same_head = (row >> 6) == (col >> 7)
    decay = jnp.exp(total.T[:, :128])
    decay = jnp.concatenate([decay] * GLA_HEADS, axis=1)
    st_new = decay * st + jnp.where(same_head, kv, 0.0)
    st_scr[...] = st_new


def _gla_mix(bslab, lr, wgk, bgk, state_gla, layer):
    up, lev = _gla_consts()
    rowblk, seq, first, last = _gla_tables()
    nsteps = rowblk.shape[1]
    nseq = BATCH + DEC_BATCH
    nl = GLA_LEVELS
    whole = lambda shape: pl.BlockSpec(shape, lambda n, rb, sq, fi, la: (0,) * len(shape))

    def token_specs(d):
        return [
            pl.BlockSpec((GLA_BLK, GLA_QK), lambda n, rb, sq, fi, la: (rb[d, n], 0)),
            pl.BlockSpec((GLA_BLK, GLA_QK), lambda n, rb, sq, fi, la: (rb[d, n], 1)),
            pl.BlockSpec((GLA_BLK, GLA_V), lambda n, rb, sq, fi, la: (rb[d, n], 1)),
            pl.BlockSpec((GLA_BLK, 128), lambda n, rb, sq, fi, la: (rb[d, n], 0)),
        ]

    state_spec = pl.BlockSpec((None, 2, GLA_HEADS, GLA_DK, GLA_DV), lambda n, rb, sq, fi, la: (sq[n], 0, 0, 0, 0))
    of_layer = lambda shape: pl.BlockSpec((None,) + shape, lambda n, rb, sq, fi, la: (layer,) + (0,) * len(shape))
    grid_spec = pltpu.PrefetchScalarGridSpec(
        num_scalar_prefetch=4,
        grid=(nsteps,),
        in_specs=token_specs(0) + token_specs(1) + [
            of_layer((2, 128, GLA_QK)),
            of_layer((2, 1, GLA_QK)),
            whole((2, GLA_BLK, 128)),
            whole((2, 2, GLA_BLK, GLA_BLK // 2)),
            pl.BlockSpec((None, None, 2, GLA_HEADS, GLA_DK, GLA_DV),
                         lambda n, rb, sq, fi, la: (jnp.maximum(sq[n] - BATCH, 0), layer, 0, 0, 0, 0)),
        ],
        out_specs=[
            pl.BlockSpec((GLA_BLK, GLA_V), lambda n, rb, sq, fi, la: (rb[0, n], 0)),
            pl.BlockSpec((GLA_BLK, GLA_V), lambda n, rb, sq, fi, la: (rb[1, n], 0)),
            state_spec,
        ],
        scratch_shapes=[
            pltpu.VMEM((2, nl, GLA_BLK, GLA_QK), BF16),
            pltpu.VMEM((2, GLA_QK, GLA_V), F32),
        ],
    )
    return pl.pallas_call(
        _gla_kernel,
        grid_spec=grid_spec,
        out_shape=[
            jax.ShapeDtypeStruct((NTOK, GLA_V), F32),
            jax.ShapeDtypeStruct((NTOK, GLA_V), F32),
            jax.ShapeDtypeStruct((nseq, 2, GLA_HEADS, GLA_DK, GLA_DV), F32),
        ],
        compiler_params=pltpu.CompilerParams(vmem_limit_bytes=VMEM_LIMIT),
        name="gla_mix",
    )(jnp.asarray(rowblk), jnp.asarray(seq[0]), jnp.asarray(first[0]), jnp.asarray(last[0]),
      bslab, bslab, bslab, lr, bslab, bslab, bslab, lr, wgk, bgk, jnp.asarray(up), jnp.asarray(lev), state_gla)


def _attn_ctx_kernel(sink_ref, q_ref, k_ref, v_ref, o_ref, ko_ref, vo_ref, *, layer):
    k = k_ref[...]
    v = v_ref[...]
    ko_ref[...] = k
    vo_ref[...] = v
    ks = (k.astype(BF16), pltpu.roll(k, 64, 1).astype(BF16))
    vs = (v.astype(BF16), pltpu.roll(v, 64, 1).astype(BF16))
    lo = lax.broadcasted_iota(jnp.int32, (SEQ, 128), 1) < HEAD_DIM
    units = []
    for t in range(ATT_HEADS // 2):
        qt = q_ref[:, t * 128:(t + 1) * 128] * (HEAD_DIM ** -0.5)
        for p in range(2):
            qm = jnp.where(lo if p == 0 else jnp.logical_not(lo), qt, 0.0).astype(BF16)
            units.append((qm, 0 if p == t // 2 else 1, sink_ref[layer, 2 * t + p]))
    scores = [_dot_nt(qm, ks[which]) for qm, which, _ in units]
    maxes = [jnp.maximum(sink, jnp.max(s, axis=-1, keepdims=True)) for s, (_, _, sink) in zip(scores, units)]
    probs = [jnp.exp(s - m) for s, m in zip(scores, maxes)]
    dens = [jnp.exp(sink - m) + jnp.sum(p, axis=-1, keepdims=True)
            for p, m, (_, _, sink) in zip(probs, maxes, units)]
    outs = [_dot(p.astype(BF16), vs[which]) / den for p, den, (_, which, _) in zip(probs, dens, units)]
    for t in range(ATT_HEADS // 2):
        o_ref[:, t * 128:(t + 1) * 128] = jnp.where(lo, outs[2 * t], outs[2 * t + 1]).astype(BF16)


def _attn_ctx(sink, cslab, layer):
    kv_out = pl.BlockSpec((None, SEQ, ATT_KV), lambda b: (b, 0, 0))
    return pl.pallas_call(
        functools.partial(_attn_ctx_kernel, layer=layer),
        grid=(BATCH,),
        in_specs=[
            pl.BlockSpec(memory_space=pltpu.SMEM),
            pl.BlockSpec((SEQ, ATT_Q), lambda b: (b, 0)),
            pl.BlockSpec((SEQ, ATT_KV), lambda b: (b, 4)),
            pl.BlockSpec((SEQ, ATT_KV), lambda b: (b, 5)),
        ],
        out_specs=[pl.BlockSpec((SEQ, ATT_Q), lambda b: (b, 0)), kv_out, kv_out],
        out_shape=[jax.ShapeDtypeStruct((NTOK_C, ATT_Q), BF16),
                   jax.ShapeDtypeStruct((BATCH, SEQ, ATT_KV), F32),
                   jax.ShapeDtypeStruct((BATCH, SEQ, ATT_KV), F32)],
        name="attn_ctx",
    )(sink, cslab, cslab, cslab)


def _attn_lat_kernel(sink_ref, q_ref, kp_ref, kc_ref, kn_ref, vp_ref, vc_ref, vn_ref,
                     ck_ref, cv_ref, cos_ref, sin_ref, bias_ref, o_ref, *, layer):
    j = pl.program_id(1)
    nb = DEC_SEQ // ATT_BLOCK
    lane = lax.broadcasted_iota(jnp.int32, (ATT_BLOCK, 128), 1)
    lo = lane < HEAD_DIM
    first16 = (lane & 31) < 16

    def rope(x, blk_idx):
        r0 = pl.multiple_of(blk_idx * ATT_BLOCK, ATT_BLOCK)
        c = cos_ref[pl.ds(r0, ATT_BLOCK), :]
        s = sin_ref[pl.ds(r0, ATT_BLOCK), :]
        xs = jnp.where(first16, pltpu.roll(x, 112, 1), pltpu.roll(x, 16, 1))
        return x * c + xs * s

    nwin = 3 * ATT_BLOCK
    keys = jnp.concatenate([rope(kp_ref[...], jnp.maximum(j - 1, 0)), rope(kc_ref[...], j),
                            rope(kn_ref[...], jnp.minimum(j + 1, nb - 1)), ck_ref[...]], axis=0)
    vals = jnp.concatenate([vp_ref[...], vc_ref[...], vn_ref[...], cv_ref[...]], axis=0)
    keys2 = (keys.astype(BF16), pltpu.roll(keys, 64, 1).astype(BF16))
    vals2 = (vals.astype(BF16), pltpu.roll(vals, 64, 1).astype(BF16))
    kcol = lax.broadcasted_iota(jnp.int32, (1, nwin + PAST_LEN), 1)
    edge = jnp.where(((j == 0) & (kcol < ATT_BLOCK)) | ((j == nb - 1) & (kcol >= 2 * ATT_BLOCK) & (kcol < nwin)),
                     -1e30, 0.0)
    bias = bias_ref[...] + edge
    top = lax.broadcasted_iota(jnp.int32, (2 * ATT_BLOCK, 1), 0) < ATT_BLOCK
    q_tiles = [rope(q_ref[:, t * 128:(t + 1) * 128], j) * (HEAD_DIM ** -0.5) for t in range(ATT_HEADS // 2)]
    lo2 = jnp.concatenate([lo, lo], axis=0)
    units = []
    for kvh in range(ATT_KV_HEADS):
        q2 = jnp.concatenate(q_tiles[2 * kvh:2 * kvh + 2], axis=0)
        for p in range(2):
            qm = jnp.where(lo2 if p == 0 else jnp.logical_not(lo2), q2, 0.0).astype(BF16)
            sink = jnp.where(top, sink_ref[layer, 4 * kvh + p], sink_ref[layer, 4 * kvh + 2 + p])
            units.append((qm, 0 if p == kvh else 1, sink))
    scores = [_dot_nt(qm, keys2[which]) + bias for qm, which, _ in units]
    maxes = [jnp.maximum(sink, jnp.max(s, axis=-1, keepdims=True)) for s, (_, _, sink) in zip(scores, units)]
    probs = [jnp.exp(s - m) for s, m in zip(scores, maxes)]
    dens = [jnp.exp(sink - m) + jnp.sum(p, axis=-1, keepdims=True)
            for p, m, (_, _, sink) in zip(probs, maxes, units)]
    outs = [_dot(p.astype(BF16), vals2[which]) / den for p, den, (_, which, _) in zip(probs, dens, units)]
    for kvh in range(ATT_KV_HEADS):
        o2 = jnp.where(lo2, outs[2 * kvh], outs[2 * kvh + 1])
        for i in range(2):
            t = 2 * kvh + i
            o_ref[:, t * 128:(t + 1) * 128] = o2[i * ATT_BLOCK:(i + 1) * ATT_BLOCK].astype(BF16)


def _attn_lat(sink, cslab, ck, cv, cos_t, sin_t, layer):
    nb = DEC_SEQ // ATT_BLOCK
    base = NTOK_C // ATT_BLOCK
    cur = lambda b, j: base + b * nb + j
    prv = lambda b, j: base + b * nb + jnp.maximum(j - 1, 0)
    nxt = lambda b, j: base + b * nb + jnp.minimum(j + 1, nb - 1)
    kv_spec = lambda row, col: pl.BlockSpec((ATT_BLOCK, ATT_KV), lambda b, j: (row(b, j), col))
    qi = np.arange(2 * ATT_BLOCK)[:, None] % ATT_BLOCK
    kc = np.arange(3 * ATT_BLOCK + PAST_LEN)[None, :]
    inside = (np.abs(kc - ATT_BLOCK - qi) <= WINDOW) | (kc >= 3 * ATT_BLOCK)
    band = np.where(inside, 0.0, -1e30).astype(np.float32)
    cache_spec = pl.BlockSpec((None, None, PAST_LEN, ATT_KV), lambda b, j: (b, layer, 0, 0))
    return pl.pallas_call(
        functools.partial(_attn_lat_kernel, layer=layer),
        grid=(DEC_BATCH, nb),
        in_specs=[
            pl.BlockSpec(memory_space=pltpu.SMEM),
            pl.BlockSpec((ATT_BLOCK, ATT_Q), lambda b, j: (cur(b, j), 0)),
            kv_spec(prv, 4), kv_spec(cur, 4), kv_spec(nxt, 4),
            kv_spec(prv, 5), kv_spec(cur, 5), kv_spec(nxt, 5),
            cache_spec, cache_spec,
            pl.BlockSpec((DEC_SEQ, 128), lambda b, j: (0, 0)),
            pl.BlockSpec((DEC_SEQ, 128), lambda b, j: (0, 0)),
            pl.BlockSpec(band.shape, lambda b, j: (0, 0)),
        ],
        out_specs=pl.BlockSpec((ATT_BLOCK, ATT_Q), lambda b, j: (b * nb + j, 0)),
        out_shape=jax.ShapeDtypeStruct((NTOK_L, ATT_Q), BF16),
        name="attn_lat",
    )(sink, cslab, cslab, cslab, cslab, cslab, cslab, cslab, ck, cv, cos_t, sin_t, jnp.asarray(band))


@functools.lru_cache(maxsize=None)
def _rope_tables():
    rows = DEC_SEQ // GRID_W
    row = np.repeat(np.arange(rows, dtype=np.float64), GRID_W)
    col = np.tile(np.arange(GRID_W, dtype=np.float64), rows)
    quarter = HEAD_DIM // 4
    inv = ROPE_BASE ** (-np.arange(quarter, dtype=np.float64) / quarter)
    lane = np.arange(128)
    use_row = (lane % HEAD_DIM) < HEAD_DIM // 2
    pos = np.where(use_row[None, :], row[:, None], col[:, None])
    ang = pos * inv[lane % quarter][None, :]
    sign = np.where((lane % 32) < 16, -1.0, 1.0)
    return np.cos(ang).astype(np.float32), (np.sin(ang) * sign[None, :]).astype(np.float32)


def _merge_kernel(*refs, split_x):
    if split_x:
        xc_ref, xl_ref, *refs = refs
    else:
        xc_ref, *refs = refs
    (mod_ref, g_ref, ys5_ref, ogf_ref, ogb_ref, gb_ref, ycc_ref, ycl_ref, gate_ref, gng_ref,
     wglu_ref, wbr_ref, wout_ref, o_ref) = refs
    is_ctx = pl.program_id(0) < NTOK_C // TM
    if split_x:
        x = jnp.where(is_ctx, xc_ref[...], xl_ref[...])
    else:
        x = xc_ref[...]
    y = jnp.concatenate([ys5_ref[j] for j in range(S5_SLABS)], axis=1)
    y = 0.5 * y * (1.0 + jnp.tanh(math.sqrt(2.0 / math.pi) * (y + 0.044715 * (y * y * y))))
    ag = _dot(y.astype(BF16), wglu_ref[...])
    y_a = ag[:, :S5_WIDTH] * _sigmoid(ag[:, S5_WIDTH:])
    gng = gng_ref[...]
    parts = []
    for h in range(GLA_HEADS):
        sl = slice(h * GLA_DV, (h + 1) * GLA_DV)
        o = ogf_ref[:, sl] + ogb_ref[:, sl]
        g = gb_ref[:, sl]
        parts.append(_rms(o, gng) * (g * _sigmoid(g)))
    y_b = jnp.concatenate(parts, axis=1)
    y_c = jnp.where(is_ctx, ycc_ref[...], ycl_ref[...])
    merged = None
    for n, yn in enumerate((y_a, y_b, y_c)):
        proj = _dot(yn.astype(BF16), wbr_ref[n])
        term = gate_ref[:, n * D_MODEL:(n + 1) * D_MODEL].astype(F32) * proj
        merged = term if merged is None else merged + term
    mixed = _dot(merged.astype(BF16), wout_ref[...])
    g1 = mod_ref[:, 2 * D_MODEL:3 * D_MODEL]
    o_ref[...] = x + g1 * _rms(mixed, g_ref[...])


def _layer_spec(shape, layer):
    return pl.BlockSpec((None,) + shape, lambda i: (layer,) + (0,) * len(shape), pipeline_mode=pl.Buffered(1))


def _split_token_specs(n_arrays, width=D_MODEL):
    nct = NTOK_C // TM
    if n_arrays == 2:
        return [pl.BlockSpec((TM, width), lambda i: (jnp.minimum(i, nct - 1), 0)),
                pl.BlockSpec((TM, width), lambda i: (jnp.maximum(i - nct, 0), 0))]
    return [pl.BlockSpec((TM, width), lambda i: (i, 0))]


def _merge(xs, mod, g, ys5, og, bslab, yc, gates, gng, wglu, wbr, wout, layer):
    tok = lambda width, col=0: pl.BlockSpec((TM, width), lambda i: (i, col))
    full = lambda shape: _layer_spec(shape, layer)
    return pl.pallas_call(
        functools.partial(_merge_kernel, split_x=len(xs) == 2),
        grid=(NTOK // TM,),
        in_specs=_split_token_specs(len(xs)) + [
            _mod_spec(layer),
            _gain_spec(layer, 1),
            pl.BlockSpec((S5_SLABS, TM, 128), lambda i: (0, _s5_tile(i), 0)),
            tok(GLA_V),
            tok(GLA_V),
            tok(GLA_V, 2),
        ] + _split_token_specs(2, ATT_Q) + [
            tok(N_BRANCH * D_MODEL),
            pl.BlockSpec((None, 1, GLA_DV), lambda i: (layer, 0, 0)),
            full((S5_WIDTH, 2 * S5_WIDTH)),
            full((N_BRANCH, BRANCH_W, D_MODEL)),
            full((D_MODEL, D_MODEL)),
        ],
        out_specs=tok(D_MODEL),
        out_shape=jax.ShapeDtypeStruct((NTOK, D_MODEL), F32),
        compiler_params=pltpu.CompilerParams(vmem_limit_bytes=VMEM_LIMIT),
        name="merge",
    )(*xs, mod, g, ys5, *og, bslab, *yc, gates, gng, wglu, wbr, wout)


FFN_SPLIT = 1


def _ffn_kernel(x_ref, mod_ref, gin_ref, gout_ref, w1_ref, w2_ref, *o_refs):
    x = x_ref[...]
    sh = mod_ref[:, 3 * D_MODEL:4 * D_MODEL]
    sc = mod_ref[:, 4 * D_MODEL:5 * D_MODEL]
    g2 = mod_ref[:, 5 * D_MODEL:6 * D_MODEL]
    h = (_rms(x, gin_ref[...]) * (1.0 + sc) + sh).astype(BF16)
    ck = FFN_HIDDEN // FFN_SPLIT
    acc = None
    for c in range(FFN_SPLIT):
        a = _dot(h, w1_ref[:, c * ck:(c + 1) * ck])
        b = _dot(h, w1_ref[:, FFN_HIDDEN + c * ck:FFN_HIDDEN + (c + 1) * ck])
        act = (a * _sigmoid(a) * b).astype(BF16)
        part = _dot(act, w2_ref[c * ck:(c + 1) * ck, :])
        acc = part if acc is None else acc + part
    y = x + g2 * _rms(acc, gout_ref[...])
    if len(o_refs) == 1:
        o_refs[0][...] = y
    else:
        is_ctx = pl.program_id(0) < NTOK_C // TM

        @pl.when(is_ctx)
        def _():
            o_refs[0][...] = y

        @pl.when(jnp.logical_not(is_ctx))
        def _():
            o_refs[1][...] = y


def _ffn(x, mod, gains, w1, w2, layer, split_out):
    nct = NTOK_C // TM
    if split_out:
        out_specs = [pl.BlockSpec((TM, D_MODEL), lambda i: (jnp.minimum(i, nct - 1), 0)),
                     pl.BlockSpec((TM, D_MODEL), lambda i: (jnp.maximum(i - nct, 0), 0))]
        out_shape = [jax.ShapeDtypeStruct((NTOK_C, D_MODEL), F32), jax.ShapeDtypeStruct((NTOK_L, D_MODEL), F32)]
    else:
        out_specs = pl.BlockSpec((TM, D_MODEL), lambda i: (i, 0))
        out_shape = jax.ShapeDtypeStruct((NTOK, D_MODEL), F32)
    return pl.pallas_call(
        _ffn_kernel,
        grid=(NTOK // TM,),
        in_specs=[
            pl.BlockSpec((TM, D_MODEL), lambda i: (i, 0)),
            _mod_spec(layer),
            _gain_spec(layer, 2),
            _gain_spec(layer, 3),
            _layer_spec((D_MODEL, 2 * FFN_HIDDEN), layer),
            _layer_spec((FFN_HIDDEN, D_MODEL), layer),
        ],
        out_specs=out_specs,
        out_shape=out_shape,
        compiler_params=pltpu.CompilerParams(vmem_limit_bytes=VMEM_LIMIT),
        name="ffn",
    )(x, mod, gains, gains, w1, w2)


def kernel(x_prompt, x_sample, cache_k, cache_v, state_s5, state_gla, c, c_ctx, w_mod, b_mod, norm_g, w_in,
           s5_lam_re, s5_lam_im, s5_log_step, s5_b_re, s5_b_im, s5_c_re, s5_c_im, s5_d, w_glu, gla_w_gk,
           gla_b_gk, gla_norm_g, att_sink, w_branch, w_out, w_ffn_in, w_ffn_out):
    cond = jnp.concatenate([c_ctx[None, :], c, jnp.zeros((N_MOD_ROWS - 1 - DEC_BATCH, D_MODEL), F32)], axis=0)
    mod_all = _modulation(cond, w_mod, b_mod).reshape(DEPTH, N_MOD_ROWS, 1, 6 * D_MODEL)
    cos_t, sin_t = _rope_tables()
    xs = (x_prompt.reshape(NTOK_C, D_MODEL), x_sample.reshape(NTOK_L, D_MODEL))
    w_in_b = w_in.astype(BF16)
    w_in_end = jnp.pad(w_in[:, :, D_IN_TILED:].astype(BF16), ((0, 0), (0, 0), (0, W_IN_COLS - D_IN)))
    w_glu_b, w_branch_b, w_out_b = w_glu.astype(BF16), w_branch.astype(BF16), w_out.astype(BF16)
    w_ffn_in_b, w_ffn_out_b = w_ffn_in.astype(BF16), w_ffn_out.astype(BF16)
    s5_params = _s5_params(s5_lam_re, s5_lam_im, s5_log_step, s5_b_re, s5_b_im, s5_c_re, s5_c_im)
    h0_all = state_s5.astype(F32).transpose(1, 0, 3, 5, 2, 4).reshape(DEPTH, DEC_BATCH, S5_GROUPS * 256)
    wgk_all = jnp.stack([jnp.pad(gla_w_gk[:, d], ((0, 0), (d * GLA_RANK, 128 - (d + 1) * GLA_RANK), (0, 0)))
                         for d in range(2)], axis=1).astype(BF16)
    bgk_all = gla_b_gk[:, :, None, :].astype(F32)
    gains = norm_g.astype(F32).reshape(DEPTH * 4, 1, D_MODEL)
    gla_gain = gla_norm_g.astype(F32).reshape(DEPTH, 1, GLA_DV)
    sink = att_sink.astype(F32)
    cache_k2 = cache_k.astype(F32).reshape(DEC_BATCH, DEPTH, PAST_LEN, ATT_KV)
    cache_v2 = cache_v.astype(F32).reshape(DEC_BATCH, DEPTH, PAST_LEN, ATT_KV)
    state_gla = state_gla.astype(F32)
    mod = mod_all
    new_k, new_v, new_s5, new_gla = [], [], [], []
    for i in range(DEPTH):
        uj, bslab, cslab, gates, lr = _inproj(xs, mod, gains, w_in_b, w_in_end, i)

        wt, web, wca, a16, dj = _s5_prep(s5_params, s5_d[i], i)
        hin, finc = _s5_scan(_s5_state(uj, web), a16, h0_all[i])
        ys5 = _s5_out(uj, hin, wt, wca, dj)
        new_s5.append(finc)

        *og, gla_fin = _gla_mix(bslab, lr, wgk_all, bgk_all, state_gla, i)
        new_gla.append(gla_fin[:BATCH])

        yc_ctx, k_new, v_new = _attn_ctx(sink, cslab, i)
        yc = (yc_ctx, _attn_lat(sink, cslab, cache_k2, cache_v2, cos_t, sin_t, i))
        new_k.append(k_new.reshape(BATCH, SEQ, ATT_KV_HEADS, HEAD_DIM))
        new_v.append(v_new.reshape(BATCH, SEQ, ATT_KV_HEADS, HEAD_DIM))

        x = _merge(xs, mod, gains, ys5, og, bslab, yc, gates, gla_gain, w_glu_b, w_branch_b, w_out_b, i)
        last = i == DEPTH - 1
        x = _ffn(x, mod, gains, w_ffn_in_b, w_ffn_out_b, i, last)
        xs = tuple(x) if last else (x,)

    return (xs[0].reshape(BATCH, SEQ, D_MODEL), xs[1].reshape(DEC_BATCH, DEC_SEQ, D_MODEL),
            jnp.stack(new_k, axis=1), jnp.stack(new_v, axis=1),
            jnp.stack(new_s5).reshape(DEPTH, BATCH, S5_GROUPS, 2, 2, S5_STATE).transpose(1, 0, 4, 2, 5, 3),
            jnp.stack(new_gla, axis=1))
```

```python
import functools
import math

import numpy as np
import jax
import jax.numpy as jnp
from jax import lax
from jax.experimental import pallas as pl
from jax.experimental.pallas import tpu as pltpu

F32 = jnp.float32
BF16 = jnp.bfloat16

D_MODEL = 1024
BATCH = 16
SEQ = 256
DEPTH = 2
DEC_BATCH = 8
DEC_SEQ = 1024
PAST_LEN = 256
GRID_W = 64
ROPE_BASE = 10000.0
S5_WIDTH = 512
S5_GROUP = 16
S5_GROUPS = 32
S5_STATE = 64
GLA_HEADS = 4
GLA_DK = 64
GLA_DV = 128
GLA_QK = 256
GLA_V = 512
GLA_RANK = 16
GLA_NORMALIZER = 16.0
ATT_HEADS = 8
ATT_KV_HEADS = 2
HEAD_DIM = 64
ATT_Q = 512
ATT_KV = 128
WINDOW = 128
ATT_BLOCK = 128
N_BRANCH = 3
BRANCH_W = 512
FFN_HIDDEN = 2816
RMS_EPS = 1e-6

NTOK_C = BATCH * SEQ
NTOK_L = DEC_BATCH * DEC_SEQ
NTOK = NTOK_C + NTOK_L
TM = 512
N_MOD_ROWS = 16

D_IN = 5920
D_IN_TILED = D_IN // 128 * 128
W_IN_COLS = 6016
S5_CHUNK = 16
S5_SLABS = S5_WIDTH // 128
S5_SLAB_W = S5_CHUNK * 128
S5_ROWS_C = NTOK_C // S5_CHUNK
S5_ROWS = NTOK // S5_CHUNK
S5_ROW_TILE = 256
GLA_BLK = 256
GLA_LEVELS = 8
VMEM_LIMIT = 56 * 1024 * 1024


def _dot(a, b):
    return jnp.dot(a, b, preferred_element_type=F32)


def _dot_nt(a, b):
    return lax.dot_general(a, b, (((1,), (1,)), ((), ())), preferred_element_type=F32)


def _dot_tn(a, b):
    return lax.dot_general(a, b, (((0,), (0,)), ((), ())), preferred_element_type=F32)


def _rms(x, g):
    return x * lax.rsqrt(jnp.mean(x * x, axis=-1, keepdims=True) + RMS_EPS) * g


def _sigmoid(x):
    return 0.5 * jnp.tanh(0.5 * x) + 0.5


def _mod_row(i):
    nct = NTOK_C // TM
    return jnp.where(i < nct, 0, 1 + (i - nct) // (DEC_SEQ // TM))


def _mod_spec(layer):
    return pl.BlockSpec((None, None, 1, 6 * D_MODEL), lambda i: (layer, _mod_row(i), 0, 0))


def _gain_spec(layer, k):
    return pl.BlockSpec((None, 1, D_MODEL), lambda i: (layer * 4 + k, 0, 0))


def _s5_tile(i):
    nct = NTOK_C // TM
    per_seq = DEC_SEQ // TM
    k = i - nct
    return jnp.where(i < nct, i, nct + (k % per_seq) * DEC_BATCH + k // per_seq)


def _mod_kernel(c_ref, w_ref, b_ref, o_ref):
    c = c_ref[...]
    s = (c * _sigmoid(c)).astype(BF16)
    o_ref[...] = _dot(s, w_ref[...].astype(BF16)) + b_ref[...]


def _modulation(cond, w_mod, b_mod):
    tn = 2048
    return pl.pallas_call(
        _mod_kernel,
        grid=(DEPTH, 6 * D_MODEL // tn),
        in_specs=[
            pl.BlockSpec((N_MOD_ROWS, D_MODEL), lambda l, n: (0, 0)),
            pl.BlockSpec((None, D_MODEL, tn), lambda l, n: (l, 0, n)),
            pl.BlockSpec((None, 1, tn), lambda l, n: (l, 0, n)),
        ],
        out_specs=pl.BlockSpec((None, N_MOD_ROWS, tn), lambda l, n: (l, 0, n)),
        out_shape=jax.ShapeDtypeStruct((DEPTH, N_MOD_ROWS, 6 * D_MODEL), F32),
        name="modulation",
    )(cond, w_mod, b_mod.reshape(DEPTH, 1, 6 * D_MODEL))


_IN_SLABS = ((0, 512), (512, 1536), (2048, 768), (2816, 3072), (5888, 128))
W_IN_SPLIT = 2048
W_IN_GAP = 32
W_IN_TAIL = W_IN_COLS - W_IN_SPLIT


def _inproj_kernel(*refs, split_x):
    if split_x:
        xc_ref, xl_ref, *refs = refs
    else:
        xc_ref, *refs = refs
    mod_ref, g_ref, w_ref, w_end_ref, u_ref, b_ref, c_ref, gate_ref, lr_ref, w_tail, u_stage = refs
    i = pl.program_id(0)

    @pl.when(i == 0)
    def _():
        r = lax.broadcasted_iota(jnp.int32, (256, 128), 0)
        c = lax.broadcasted_iota(jnp.int32, (256, 128), 1)
        shift = (r == c + W_IN_GAP).astype(BF16)
        head = ((r == c) & (c < W_IN_GAP)).astype(BF16)
        ntile = (W_IN_TAIL - 128) // 128
        for t in range(ntile - 1):
            src = W_IN_SPLIT + 128 * t
            w_tail[:, 128 * t:128 * (t + 1)] = _dot(w_ref[:, src:src + 256], shift).astype(BF16)
        src = W_IN_SPLIT + 128 * (ntile - 1)
        last = jnp.concatenate([w_ref[:, src:src + 128], w_end_ref[...]], axis=1)
        w_tail[:, 128 * (ntile - 1):128 * ntile] = _dot(last, shift).astype(BF16)
        w_tail[:, 128 * ntile:] = _dot(w_ref[:, W_IN_SPLIT:W_IN_SPLIT + 256], head).astype(BF16)

    if split_x:
        x = jnp.where(i < NTOK_C // TM, xc_ref[...], xl_ref[...])
    else:
        x = xc_ref[...]
    mod = mod_ref[...]
    h = _rms(x, g_ref[...]) * (1.0 + mod[:, D_MODEL:2 * D_MODEL]) + mod[:, 0:D_MODEL]
    h = h.astype(BF16)
    z_head = _dot(h, w_ref[:, 0:W_IN_SPLIT])
    z_tail = _dot(h, w_tail[...])
    for j in range(S5_SLABS):
        u_stage[...] = z_head[:, j * 128:(j + 1) * 128]
        for s in range(S5_CHUNK):
            u_ref[j, :, s * 128:(s + 1) * 128] = u_stage[pl.ds(s, TM // S5_CHUNK, stride=S5_CHUNK), :]
    b_ref[...] = z_head[:, S5_WIDTH:W_IN_SPLIT]
    for (off, width), o_ref in zip(_IN_SLABS[2:], (c_ref, gate_ref, lr_ref)):
        z = z_tail[:, off - W_IN_SPLIT:off - W_IN_SPLIT + width]
        o_ref[...] = _sigmoid(z).astype(BF16) if o_ref is gate_ref else z


def _inproj(xs, mod, g, w_all, w_end, layer):
    return pl.pallas_call(
        functools.partial(_inproj_kernel, split_x=len(xs) == 2),
        grid=(NTOK // TM,),
        in_specs=_split_token_specs(len(xs)) + [
            _mod_spec(layer),
            _gain_spec(layer, 0),
            pl.BlockSpec((None, D_MODEL, D_IN), lambda i: (layer, 0, 0), pipeline_mode=pl.Buffered(1)),
            pl.BlockSpec((None, D_MODEL, 128), lambda i: (layer, 0, 0), pipeline_mode=pl.Buffered(1)),
        ],
        out_specs=[pl.BlockSpec((S5_SLABS, TM // S5_CHUNK, S5_SLAB_W), lambda i: (0, _s5_tile(i), 0))]
        + [pl.BlockSpec((TM, width), lambda i: (i, 0)) for _, width in _IN_SLABS[1:]],
        out_shape=[jax.ShapeDtypeStruct((S5_SLABS, S5_ROWS, S5_SLAB_W), F32)]
        + [jax.ShapeDtypeStruct((NTOK, width), BF16 if width == N_BRANCH * D_MODEL else F32)
           for _, width in _IN_SLABS[1:]],
        scratch_shapes=[pltpu.VMEM((D_MODEL, W_IN_TAIL), BF16), pltpu.VMEM((TM, 128), F32)],
        compiler_params=pltpu.CompilerParams(vmem_limit_bytes=VMEM_LIMIT),
        name="inproj",
    )(*xs, mod, g, w_all, w_end)


@functools.lru_cache(maxsize=None)
def _s5_expanders():
    seg = 8
    spread = np.zeros((seg, 256, S5_SLAB_W), np.float32)
    col = np.arange(256)
    for gl in range(seg):
        spread[gl, col, (col // S5_GROUP) * 128 + gl * S5_GROUP + col % S5_GROUP] = 1.0
    return spread


def _s5_prep_kernel(par_ref, bre_ref, bim_ref, cre_ref, cim_ref, spread_ref,
                    wt_ref, web_ref, wca_ref, a16_ref):
    n = S5_CHUNK
    lam_re = par_ref[0:1, :]
    lam_im = par_ref[1:2, :]
    dt = jnp.exp(par_ref[2:3, :])
    lr = lam_re * dt
    li = lam_im * dt
    krow = lax.broadcasted_iota(jnp.int32, (24, 128), 0).astype(F32)
    tab_mag = jnp.exp(krow * lr)
    tab_re = tab_mag * jnp.cos(krow * li)
    tab_im = tab_mag * jnp.sin(krow * li)
    ar = tab_re[1:2, :]
    ai = tab_im[1:2, :]
    nr = ar - 1.0
    den = lam_re * lam_re + lam_im * lam_im
    fr = (nr * lam_re + ai * lam_im) / den
    fi = (ai * lam_re - nr * lam_im) / den
    b_re = bre_ref[...]
    b_im = bim_ref[...]
    br = fr * b_re - fi * b_im
    bi = fr * b_im + fi * b_re
    c_re = cre_ref[...]
    c_im = cim_ref[...]

    def lo_half(shape):
        return lax.broadcasted_iota(jnp.int32, shape, 1) < S5_STATE

    def tile_rows(a):
        return jnp.concatenate([a] * n, axis=0)

    fwd16 = lo_half((S5_GROUP, 128))

    def powers(t_re, t_im, k_fwd, k_bwd):
        def pick(t, b):
            kf, kb = k_fwd(b), k_bwd(b)
            return jnp.where(fwd16, jnp.broadcast_to(t[kf:kf + 1, :], (S5_GROUP, 128)),
                             jnp.broadcast_to(t[kb:kb + 1, :], (S5_GROUP, 128)))
        return (jnp.concatenate([pick(t_re, b) for b in range(n)], axis=0),
                jnp.concatenate([pick(t_im, b) for b in range(n)], axis=0))

    fwd = lo_half((n * S5_GROUP, 128))
    brt, bit, crt, cit = tile_rows(br), tile_rows(bi), tile_rows(c_re), tile_rows(c_im)

    per, pei = powers(tab_re, tab_im, lambda s: n - 1 - s, lambda s: s)
    eb = jnp.concatenate([brt * per - bit * pei, brt * pei + bit * per], axis=1)
    pcr, pci = powers(tab_re, tab_im, lambda t: t + 1, lambda t: n - t)
    ca = jnp.concatenate([(crt * pcr - cit * pci).T, (-(crt * pci + cit * pcr)).T], axis=0)

    def one_dir(x, d):
        sw = pltpu.roll(x, S5_STATE, 1)
        lo = lo_half(x.shape)
        return jnp.where(lo, x, sw) if d == 0 else jnp.where(lo, sw, x)

    klag = []
    for d in range(2):
        lhs = jnp.where(lo_half(br.shape), one_dir(br, d), -one_dir(bi, d))
        crd, cid = tile_rows(one_dir(c_re, d)), tile_rows(one_dir(c_im, d))
        lag = (lambda b: b) if d == 0 else (lambda b: n - 1 - b)
        pr, pi = powers(one_dir(tab_re, d), one_dir(tab_im, d), lag, lag)
        rhs_t = jnp.where(fwd, crd * pr - cid * pi, crd * pi + cid * pr)
        klag.append(lax.dot_general(lhs, rhs_t, (((1,), (1,)), ((), ())),
                                    precision=lax.Precision.HIGHEST, preferred_element_type=F32))
    lane = lax.broadcasted_iota(jnp.int32, (S5_GROUP, n * S5_GROUP), 1)
    rows = []
    for s in range(n):
        f = klag[0] if s == 0 else jnp.where(lane >= S5_GROUP * s, pltpu.roll(klag[0], S5_GROUP * s, 1), 0.0)
        sh = (n * S5_GROUP - S5_GROUP * (n - 1 - s)) % (n * S5_GROUP)
        b = klag[1] if sh == 0 else pltpu.roll(klag[1], sh, 1)
        rows.append(f + jnp.where(lane < S5_GROUP * (s + 1), b, 0.0))
    toep = jnp.concatenate(rows, axis=0)

    spread = spread_ref[...]
    wt_ref[...] = _dot(toep.astype(BF16), spread).astype(BF16).reshape(n, S5_GROUP, S5_SLAB_W)
    wca_ref[...] = _dot(ca.astype(BF16), spread).astype(BF16)
    a16_ref[0:1, :] = tab_re[n:n + 1, :]
    a16_ref[1:2, :] = tab_im[n:n + 1, :]
    web_ref[...] = jnp.zeros_like(web_ref)
    eb3 = eb.astype(BF16).reshape(n, S5_GROUP, 256)
    for pos in range(S5_SLAB_W // 256):
        @pl.when(pl.program_id(0) == pos)
        def _():
            web_ref[:, :, pos * 256:(pos + 1) * 256] = eb3


def _s5_params(lam_re, lam_im, log_step, b_re, b_im, c_re, c_im):
    par = jnp.stack([lam_re, lam_im, log_step], axis=1).astype(F32)
    par = par.transpose(0, 3, 1, 2, 4).reshape(DEPTH, S5_GROUPS, 3, 128)
    par = jnp.pad(par, ((0, 0), (0, 0), (0, 5), (0, 0)))
    b_t = lambda b: b.astype(F32).transpose(0, 2, 4, 1, 3).reshape(DEPTH, S5_GROUPS, S5_GROUP, 128)
    c_t = lambda c: c.astype(F32).transpose(0, 2, 3, 1, 4).reshape(DEPTH, S5_GROUPS, S5_GROUP, 128)
    return par, b_t(b_re), b_t(b_im), c_t(c_re), c_t(c_im)


def _s5_prep(params, d_skip, layer):
    seg = 8
    spread = _s5_expanders()
    vec = pl.BlockSpec((None, None, S5_GROUP, 128), lambda gl, j: (layer, j * seg + gl, 0, 0))
    exp_spec = pl.BlockSpec((None, 256, S5_SLAB_W), lambda gl, j: (gl, 0, 0))
    rows_spec = pl.BlockSpec((None, S5_CHUNK, None, S5_GROUP, S5_SLAB_W), lambda gl, j: (j, 0, gl, 0, 0))
    wt, web, wca, a16 = pl.pallas_call(
        _s5_prep_kernel,
        grid=(seg, S5_SLABS),
        in_specs=[pl.BlockSpec((None, None, 8, 128), lambda gl, j: (layer, j * seg + gl, 0, 0)),
                  vec, vec, vec, vec, exp_spec],
        out_specs=[
            rows_spec, rows_spec,
            pl.BlockSpec((None, None, 256, S5_SLAB_W), lambda gl, j: (j, gl, 0, 0)),
            pl.BlockSpec((None, 2, 128), lambda gl, j: (j * seg + gl, 0, 0)),
        ],
        out_shape=[
            jax.ShapeDtypeStruct((S5_SLABS, S5_CHUNK, seg, S5_GROUP, S5_SLAB_W), BF16),
            jax.ShapeDtypeStruct((S5_SLABS, S5_CHUNK, seg, S5_GROUP, S5_SLAB_W), BF16),
            jax.ShapeDtypeStruct((S5_SLABS, seg, 256, S5_SLAB_W), BF16),
            jax.ShapeDtypeStruct((S5_GROUPS, 2, 128), F32),
        ],
        name="s5_prep",
    )(*params, jnp.asarray(spread, BF16))
    mat = (S5_SLABS, S5_SLAB_W, S5_SLAB_W)
    dj = jnp.tile(d_skip.astype(F32).reshape(S5_SLABS, 1, 128), (1, 1, S5_CHUNK))
    return wt.reshape(mat), web.reshape(mat), wca.reshape(mat), a16.reshape(1, S5_SLABS * S5_SLAB_W), dj


S5_STATE_COLS = S5_SLABS * S5_SLAB_W // 128
S5_SLAB_COLS = S5_SLAB_W // 128


@functools.lru_cache(maxsize=None)
def _s5_row_perms():
    assert S5_ROW_TILE == S5_ROWS_C == DEC_BATCH * TM // S5_CHUNK
    perm = np.zeros((3, S5_ROW_TILE, S5_ROW_TILE), np.float32)
    for p, (nseq, nchunk) in enumerate(((BATCH, SEQ // S5_CHUNK), (DEC_BATCH, TM // S5_CHUNK),
                                        (DEC_BATCH, TM // S5_CHUNK))):
        b, c = np.meshgrid(np.arange(nseq), np.arange(nchunk), indexing="ij")
        perm[p, (c * nseq + b).ravel(), (b * nchunk + c).ravel()] = 1.0
    return perm, perm.transpose(0, 2, 1).copy()


def _s5_state_kernel(u_ref, perm_ref, w_ref, o_ref):
    u = _dot(perm_ref[...], u_ref[...].astype(BF16)).astype(BF16)
    s = _dot(u, w_ref[...])
    for k in range(S5_SLAB_COLS):
        o_ref[k] = s[:, k * 128:(k + 1) * 128]


def _s5_state(uj, web):
    perm, _ = _s5_row_perms()
    return pl.pallas_call(
        _s5_state_kernel,
        grid=(S5_SLABS, S5_ROWS // S5_ROW_TILE),
        in_specs=[
            pl.BlockSpec((None, S5_ROW_TILE, S5_SLAB_W), lambda j, p: (j, p, 0)),
            pl.BlockSpec((None, S5_ROW_TILE, S5_ROW_TILE), lambda j, p: (p, 0, 0)),
            pl.BlockSpec((None, S5_SLAB_W, S5_SLAB_W), lambda j, p: (j, 0, 0)),
        ],
        out_specs=pl.BlockSpec((S5_SLAB_COLS, S5_ROW_TILE, 128), lambda j, p: (j, p, 0)),
        out_shape=jax.ShapeDtypeStruct((S5_STATE_COLS, S5_ROWS, 128), F32),
        compiler_params=pltpu.CompilerParams(vmem_limit_bytes=VMEM_LIMIT),
        name="s5_state",
    )(uj, jnp.asarray(perm, BF16), web)


S5_SCAN_COLS = 8


def _s5_scan_kernel(s_ref, a_ref, h0_ref, hin_ref, fin_ref, hf, hb):
    ncol = S5_SCAN_COLS

    def scan(row0, nc, nb, h0):
        is_f = lax.broadcasted_iota(jnp.int32, (nb, 128), 1) < S5_STATE
        chunk_rows = lambda c: pl.ds(pl.multiple_of(row0 + c * nb, 8), nb)

        def body(c, hs):
            rf = chunk_rows(c)
            rb = chunk_rows(nc - 1 - c)
            new = []
            for m in range(ncol // 2):
                h_re, h_im = hs[2 * m], hs[2 * m + 1]
                a_re = a_ref[:, (2 * m) * 128:(2 * m + 1) * 128]
                a_im = a_ref[:, (2 * m + 1) * 128:(2 * m + 2) * 128]
                loc = []
                for k, h in ((2 * m, h_re), (2 * m + 1, h_im)):
                    hf[k, rf, :] = h
                    hb[k, rb, :] = h
                    loc.append(jnp.where(is_f, s_ref[k, rf, :], s_ref[k, rb, :]))
                new.append(a_re * h_re - a_im * h_im + loc[0])
                new.append(a_re * h_im + a_im * h_re + loc[1])
            return tuple(new)

        return lax.fori_loop(0, nc, body, h0)

    fin = scan(0, SEQ // S5_CHUNK, BATCH, tuple(jnp.zeros((BATCH, 128), F32) for _ in range(ncol)))
    for k in range(ncol):
        fin_ref[:, k * 128:(k + 1) * 128] = fin[k]
    scan(S5_ROWS_C, DEC_SEQ // S5_CHUNK, DEC_BATCH,
         tuple(h0_ref[:, k * 128:(k + 1) * 128] for k in range(ncol)))
    fwd = lax.broadcasted_iota(jnp.int32, (ncol, S5_ROWS, 128), 2) < S5_STATE
    hin_ref[...] = jnp.where(fwd, hf[...], hb[...]).astype(BF16)


def _s5_scan(sloc, a16, h0l):
    ncol = S5_SCAN_COLS
    w = ncol * 128
    return pl.pallas_call(
        _s5_scan_kernel,
        grid=(S5_STATE_COLS // ncol,),
        in_specs=[
            pl.BlockSpec((ncol, S5_ROWS, 128), lambda k: (k, 0, 0)),
            pl.BlockSpec((1, w), lambda k: (0, k)),
            pl.BlockSpec((DEC_BATCH, w), lambda k: (0, k)),
        ],
        out_specs=[
            pl.BlockSpec((ncol, S5_ROWS, 128), lambda k: (k, 0, 0)),
            pl.BlockSpec((BATCH, w), lambda k: (0, k)),
        ],
        out_shape=[
            jax.ShapeDtypeStruct((S5_STATE_COLS, S5_ROWS, 128), BF16),
            jax.ShapeDtypeStruct((BATCH, S5_STATE_COLS * 128), F32),
        ],
        scratch_shapes=[pltpu.VMEM((ncol, S5_ROWS, 128), F32)] * 2,
        name="s5_scan",
    )(sloc, a16, h0l)


def _s5_out_kernel(u_ref, hin_ref, perm_t_ref, wt_ref, wca_ref, d_ref, y_ref):
    u = u_ref[...]
    hin = jnp.concatenate([hin_ref[k] for k in range(S5_SLAB_COLS)], axis=1).astype(BF16)
    hin = _dot(perm_t_ref[...], hin).astype(BF16)
    y = _dot(u.astype(BF16), wt_ref[...]) + _dot(hin, wca_ref[...]) + u * d_ref[...]
    for t in range(S5_CHUNK):
        y_ref[pl.ds(t, S5_ROW_TILE, stride=S5_CHUNK), :] = y[:, t * 128:(t + 1) * 128]


def _s5_out(uj, hin, wt, wca, dj):
    _, perm_t = _s5_row_perms()
    return pl.pallas_call(
        _s5_out_kernel,
        grid=(S5_SLABS, S5_ROWS // S5_ROW_TILE),
        in_specs=[
            pl.BlockSpec((None, S5_ROW_TILE, S5_SLAB_W), lambda j, p: (j, p, 0)),
            pl.BlockSpec((S5_SLAB_COLS, S5_ROW_TILE, 128), lambda j, p: (j, p, 0)),
            pl.BlockSpec((None, S5_ROW_TILE, S5_ROW_TILE), lambda j, p: (p, 0, 0)),
            pl.BlockSpec((None, S5_SLAB_W, S5_SLAB_W), lambda j, p: (j, 0, 0)),
            pl.BlockSpec((None, S5_SLAB_W, S5_SLAB_W), lambda j, p: (j, 0, 0)),
            pl.BlockSpec((None, 1, S5_SLAB_W), lambda j, p: (j, 0, 0)),
        ],
        out_specs=pl.BlockSpec((None, S5_ROW_TILE * S5_CHUNK, 128), lambda j, p: (j, p, 0)),
        out_shape=jax.ShapeDtypeStruct((S5_SLABS, NTOK, 128), F32),
        compiler_params=pltpu.CompilerParams(vmem_limit_bytes=VMEM_LIMIT),
        name="s5_out",
    )(uj, hin, jnp.asarray(perm_t, BF16), wt, wca, dj)


@functools.lru_cache(maxsize=None)
def _gla_consts():
    n = GLA_BLK
    nl = GLA_LEVELS
    r = np.arange(n)
    up = np.zeros((n, 128), np.int32)
    for l in range(nl):
        up[:, l] = (r >> l) & 1
    i = r[:, None]
    j = r[None, :]
    x = np.maximum(i ^ j, 1)
    lev = np.where(j < i, np.floor(np.log2(x)).astype(np.int32), np.where(i == j, nl, -1)).astype(np.int32)
    up2 = np.stack([up, up[::-1]])
    h = n // 2

    def tiled(a):
        return np.stack([np.concatenate([a[:h, :h], a[h:, h:]]), np.concatenate([a[:h, h:], a[h:, :h]])])

    lev2 = np.stack([tiled(lev), tiled(lev[::-1, ::-1])])
    return up2, lev2


@functools.lru_cache(maxsize=None)
def _gla_tables():
    rowblk, seq, first, last = [], [], [], []
    for d in range(2):
        rb, sq, fi, la = [], [], [], []
        for s in range(BATCH + DEC_BATCH):
            nblk = 1 if s < BATCH else DEC_SEQ // GLA_BLK
            base = s if s < BATCH else NTOK_C // GLA_BLK + (s - BATCH) * nblk
            order = range(nblk) if d == 0 else range(nblk - 1, -1, -1)
            for pos, b in enumerate(order):
                rb.append(base + b)
                sq.append(s)
                fi.append(int(pos == 0))
                la.append(int(pos == nblk - 1))
        rowblk.append(rb); seq.append(sq); first.append(fi); last.append(la)
    as_np = lambda a: np.asarray(a, np.int32)
    return as_np(rowblk), as_np(seq), as_np(first), as_np(last)


def _gla_kernel(rowblk_ref, seq_ref, first_ref, last_ref,
                qf_ref, kf_ref, vf_ref, lrf_ref, qb_ref, kb_ref, vb_ref, lrb_ref,
                wgk_ref, bgk_ref, up_ref, lev_ref, s0_ref,
                of_ref, ob_ref, fin_ref, z_scr, st_scr):
    del rowblk_ref
    n = pl.program_id(0)

    @pl.when(first_ref[n] == 1)
    def _():
        latent = seq_ref[n] >= BATCH
        st_scr[...] = jnp.zeros_like(st_scr)
        for d in range(2):
            for h in range(GLA_HEADS):
                st_scr[d, h * GLA_DK:(h + 1) * GLA_DK, h * GLA_DV:(h + 1) * GLA_DV] = jnp.where(
                    latent, s0_ref[d, h], 0.0)

    blocks = [
        _gla_block(False, qf_ref, kf_ref, vf_ref, lrf_ref, wgk_ref.at[0], bgk_ref.at[0], up_ref.at[0],
                   lev_ref.at[0], of_ref, z_scr.at[0], st_scr.at[0]),
        _gla_block(True, qb_ref, kb_ref, vb_ref, lrb_ref, wgk_ref.at[1], bgk_ref.at[1], up_ref.at[1],
                   lev_ref.at[1], ob_ref, z_scr.at[1], st_scr.at[1]),
    ]
    for stage in range(2):
        for block in blocks:
            next(block, None)

    @pl.when(last_ref[n] == 1)
    def _():
        for d in range(2):
            for h in range(GLA_HEADS):
                fin_ref[d, h] = st_scr[d, h * GLA_DK:(h + 1) * GLA_DK, h * GLA_DV:(h + 1) * GLA_DV]


def _gla_block(backward, q_ref, k_ref, v_ref, lr_ref, wgk_ref, bgk_ref, up_ref, lev_ref, o_ref, z_scr, st_scr):
    nl = GLA_LEVELS
    blk = GLA_BLK
    q = q_ref[...] * (GLA_DK ** -0.5)
    k = k_ref[...]
    vb = v_ref[...].astype(BF16)
    x = _dot(lr_ref[...].astype(BF16), wgk_ref[...]) + bgk_ref[...]
    gk = (jnp.minimum(x, 0.0) - jnp.log(1.0 + jnp.exp(-jnp.abs(x)))) * (1.0 / GLA_NORMALIZER)

    row = lax.broadcasted_iota(jnp.int32, (blk, 1), 0)

    def sibling(a, l):
        g = 1 << l
        if g < 8:
            a3 = a.reshape(blk // 8, 8, a.shape[-1])
            dn = pltpu.roll(a3, g, 1).reshape(a.shape)
            up_ = pltpu.roll(a3, 8 - g, 1).reshape(a.shape)
            return jnp.where(((row >> l) & 1) == 1, dn, up_)
        a4 = a.reshape(blk // (2 * g), 2, g, a.shape[-1])
        return jnp.concatenate([a4[:, 1:2], a4[:, 0:1]], axis=1).reshape(a.shape)

    part = gk
    total = gk
    z0 = None
    for l in range(nl):
        g = 1 << l
        if g < 8:
            up = up_ref[:, l:l + 1] != 0
            z = jnp.where(up, q, k) * jnp.exp(jnp.where(up, part, total - part))
            other = sibling(total, l)
            part = part + jnp.where(up, other, 0.0)
            total = total + other
        else:
            halves = lambda a: (a.reshape(blk // (2 * g), 2, g, a.shape[-1])[:, 1 - int(backward)],
                                a.reshape(blk // (2 * g), 2, g, a.shape[-1])[:, int(backward)])
            join = lambda u, d: jnp.stack([d, u] if not backward else [u, d], axis=1).reshape(blk, u.shape[-1])
            part_u, part_d = halves(part)
            tot_u, tot_d = halves(total)
            q_u, _ = halves(q)
            _, k_d = halves(k)
            z = join(q_u * jnp.exp(part_u), k_d * jnp.exp(tot_d - part_d))
            part = join(part_u + tot_d, part_d)
            both = tot_u + tot_d
            total = join(both, both)
        if l == 0:
            z0 = z
        else:
            z_scr[l] = z.astype(BF16)
    yield
    lane128 = lax.broadcasted_iota(jnp.int32, (GLA_QK, 128), 1)
    dim = lax.broadcasted_iota(jnp.int32, (GLA_QK, 128), 0)
    head_sum = ((dim >> 6) == lane128).astype(BF16)
    pair0 = _dot((z0 * sibling(z0, 0)).astype(BF16), head_sum)
    diag = _dot((q * k).astype(BF16), head_sum)

    half = blk // 2
    lev_d = lev_ref[0]
    lev_o = lev_ref[1]
    lane = lax.broadcasted_iota(jnp.int32, (half, GLA_QK), 1)

    def tiles(l, in_head, crossed):
        z = z_scr[l]
        s = _dot_nt(z, jnp.where(jnp.concatenate([in_head, in_head], axis=0), z, jnp.zeros_like(z)))
        out = [s[r * half:(r + 1) * half, (1 - r if crossed else r) * half:((1 - r if crossed else r) + 1) * half]
               for r in range(2)]
        return jnp.concatenate(out, axis=0)

    upi = 0 if backward else 1
    key_lanes = {}
    for l in range(3, nl - 1):
        g = 1 << l
        c = lax.broadcasted_iota(jnp.int32, (blk // (2 * g), g, 128), 0)
        ln = lax.broadcasted_iota(jnp.int32, (blk // (2 * g), g, 128), 2)
        base = (2 * g * c + (g if backward else 0)) & 127
        key_lanes[l] = (ln >= base) & (ln < base + g)

    heads = range(GLA_HEADS)
    in_head = [(lane >= h * GLA_DK) & (lane < (h + 1) * GLA_DK) for h in heads]
    acc = [jnp.where(lev_d == 0, pair0[:, h:h + 1], jnp.where(lev_d == nl, diag[:, h:h + 1], 0.0)) for h in heads]
    for l in range(1, 3):
        acc = [jnp.where(lev_d == l, tiles(l, in_head[h], False), acc[h]) for h in heads]
    for l in range(3, nl - 1):
        g = 1 << l
        for h in heads:
            acc4 = acc[h].reshape(blk // (2 * g), 2, g, 128)
            s4 = tiles(l, in_head[h], False).reshape(blk // (2 * g), 2, g, 128)
            new_up = jnp.where(key_lanes[l], s4[:, upi], acc4[:, upi])
            pieces = [acc4[:, 0], new_up] if upi == 1 else [new_up, acc4[:, 1]]
            acc[h] = jnp.stack(pieces, axis=1).reshape(blk, 128)
    off = [jnp.where(lev_o == nl - 1, tiles(nl - 1, in_head[h], True), 0.0) for h in heads]
    for h in heads:
        att = jnp.concatenate([jnp.concatenate([acc[h][:half], off[h][:half]], axis=1),
                               jnp.concatenate([off[h][half:], acc[h][half:]], axis=1)], axis=0)
        o_ref[:, h * GLA_DV:(h + 1) * GLA_DV] = _dot(att.astype(BF16), vb[:, h * GLA_DV:(h + 1) * GLA_DV])

    st = st_scr[...]
    q_in = (q * jnp.exp(part)).astype(BF16)
    o_ref[...] += _dot(q_in, st.astype(BF16))
    k_out = (k * jnp.exp(total - part)).astype(BF16)
    kv = _dot_tn(k_out, vb)
    row = lax.broadcasted_iota(jnp.int32, (GLA_QK, GLA_V), 0)
    col = lax.broadcasted_iota(jnp.int32, (GLA_QK, GLA_V), 1)
    same_head = (row >> 6) == (col >> 7)
    decay = jnp.exp(total.T[:, :128])
    decay = jnp.concatenate([decay] * GLA_HEADS, axis=1)
    st_new = decay * st + jnp.where(same_head, kv, 0.0)
    st_scr[...] = st_new


def _gla_mix(bslab, lr, wgk, bgk, state_gla, layer):
    up, lev = _gla_consts()
    rowblk, seq, first, last = _gla_tables()
    nsteps = rowblk.shape[1]
    nseq = BATCH + DEC_BATCH
    nl = GLA_LEVELS
    whole = lambda shape: pl.BlockSpec(shape, lambda n, rb, sq, fi, la: (0,) * len(shape))

    def token_specs(d):
        return [
            pl.BlockSpec((GLA_BLK, GLA_QK), lambda n, rb, sq, fi, la: (rb[d, n], 0)),
            pl.BlockSpec((GLA_BLK, GLA_QK), lambda n, rb, sq, fi, la: (rb[d, n], 1)),
            pl.BlockSpec((GLA_BLK, GLA_V), lambda n, rb, sq, fi, la: (rb[d, n], 1)),
            pl.BlockSpec((GLA_BLK, 128), lambda n, rb, sq, fi, la: (rb[d, n], 0)),
        ]

    state_spec = pl.BlockSpec((None, 2, GLA_HEADS, GLA_DK, GLA_DV), lambda n, rb, sq, fi, la: (sq[n], 0, 0, 0, 0))
    of_layer = lambda shape: pl.BlockSpec((None,) + shape, lambda n, rb, sq, fi, la: (layer,) + (0,) * len(shape))
    grid_spec = pltpu.PrefetchScalarGridSpec(
        num_scalar_prefetch=4,
        grid=(nsteps,),
        in_specs=token_specs(0) + token_specs(1) + [
            of_layer((2, 128, GLA_QK)),
            of_layer((2, 1, GLA_QK)),
            whole((2, GLA_BLK, 128)),
            whole((2, 2, GLA_BLK, GLA_BLK // 2)),
            pl.BlockSpec((None, None, 2, GLA_HEADS, GLA_DK, GLA_DV),
                         lambda n, rb, sq, fi, la: (jnp.maximum(sq[n] - BATCH, 0), layer, 0, 0, 0, 0)),
        ],
        out_specs=[
            pl.BlockSpec((GLA_BLK, GLA_V), lambda n, rb, sq, fi, la: (rb[0, n], 0)),
            pl.BlockSpec((GLA_BLK, GLA_V), lambda n, rb, sq, fi, la: (rb[1, n], 0)),
            state_spec,
        ],
        scratch_shapes=[
            pltpu.VMEM((2, nl, GLA_BLK, GLA_QK), BF16),
            pltpu.VMEM((2, GLA_QK, GLA_V), F32),
        ],
    )
    return pl.pallas_call(
        _gla_kernel,
        grid_spec=grid_spec,
        out_shape=[
            jax.ShapeDtypeStruct((NTOK, GLA_V), F32),
            jax.ShapeDtypeStruct((NTOK, GLA_V), F32),
            jax.ShapeDtypeStruct((nseq, 2, GLA_HEADS, GLA_DK, GLA_DV), F32),
        ],
        compiler_params=pltpu.CompilerParams(vmem_limit_bytes=VMEM_LIMIT),
        name="gla_mix",
    )(jnp.asarray(rowblk), jnp.asarray(seq[0]), jnp.asarray(first[0]), jnp.asarray(last[0]),
      bslab, bslab, bslab, lr, bslab, bslab, bslab, lr, wgk, bgk, jnp.asarray(up), jnp.asarray(lev), state_gla)


def _attn_ctx_kernel(sink_ref, q_ref, k_ref, v_ref, o_ref, ko_ref, vo_ref, *, layer):
    k = k_ref[...]
    v = v_ref[...]
    ko_ref[...] = k
    vo_ref[...] = v
    ks = (k.astype(BF16), pltpu.roll(k, 64, 1).astype(BF16))
    vs = (v.astype(BF16), pltpu.roll(v, 64, 1).astype(BF16))
    lo = lax.broadcasted_iota(jnp.int32, (SEQ, 128), 1) < HEAD_DIM
    units = []
    for t in range(ATT_HEADS // 2):
        qt = q_ref[:, t * 128:(t + 1) * 128] * (HEAD_DIM ** -0.5)
        for p in range(2):
            qm = jnp.where(lo if p == 0 else jnp.logical_not(lo), qt, 0.0).astype(BF16)
            units.append((qm, 0 if p == t // 2 else 1, sink_ref[layer, 2 * t + p]))
    scores = [_dot_nt(qm, ks[which]) for qm, which, _ in units]
    maxes = [jnp.maximum(sink, jnp.max(s, axis=-1, keepdims=True)) for s, (_, _, sink) in zip(scores, units)]
    probs = [jnp.exp(s - m) for s, m in zip(scores, maxes)]
    dens = [jnp.exp(sink - m) + jnp.sum(p, axis=-1, keepdims=True)
            for p, m, (_, _, sink) in zip(probs, maxes, units)]
    outs = [_dot(p.astype(BF16), vs[which]) / den for p, den, (_, which, _) in zip(probs, dens, units)]
    for t in range(ATT_HEADS // 2):
        o_ref[:, t * 128:(t + 1) * 128] = jnp.where(lo, outs[2 * t], outs[2 * t + 1]).astype(BF16)


def _attn_ctx(sink, cslab, layer):
    kv_out = pl.BlockSpec((None, SEQ, ATT_KV), lambda b: (b, 0, 0))
    return pl.pallas_call(
        functools.partial(_attn_ctx_kernel, layer=layer),
        grid=(BATCH,),
        in_specs=[
            pl.BlockSpec(memory_space=pltpu.SMEM),
            pl.BlockSpec((SEQ, ATT_Q), lambda b: (b, 0)),
            pl.BlockSpec((SEQ, ATT_KV), lambda b: (b, 4)),
            pl.BlockSpec((SEQ, ATT_KV), lambda b: (b, 5)),
        ],
        out_specs=[pl.BlockSpec((SEQ, ATT_Q), lambda b: (b, 0)), kv_out, kv_out],
        out_shape=[jax.ShapeDtypeStruct((NTOK_C, ATT_Q), BF16),
                   jax.ShapeDtypeStruct((BATCH, SEQ, ATT_KV), F32),
                   jax.ShapeDtypeStruct((BATCH, SEQ, ATT_KV), F32)],
        name="attn_ctx",
    )(sink, cslab, cslab, cslab)


def _attn_lat_kernel(sink_ref, q_ref, kp_ref, kc_ref, kn_ref, vp_ref, vc_ref, vn_ref,
                     ck_ref, cv_ref, cos_ref, sin_ref, bias_ref, o_ref, *, layer):
    j = pl.program_id(1)
    nb = DEC_SEQ // ATT_BLOCK
    lane = lax.broadcasted_iota(jnp.int32, (ATT_BLOCK, 128), 1)
    lo = lane < HEAD_DIM
    first16 = (lane & 31) < 16

    def rope(x, blk_idx):
        r0 = pl.multiple_of(blk_idx * ATT_BLOCK, ATT_BLOCK)
        c = cos_ref[pl.ds(r0, ATT_BLOCK), :]
        s = sin_ref[pl.ds(r0, ATT_BLOCK), :]
        xs = jnp.where(first16, pltpu.roll(x, 112, 1), pltpu.roll(x, 16, 1))
        return x * c + xs * s

    nwin = 3 * ATT_BLOCK
    keys = jnp.concatenate([rope(kp_ref[...], jnp.maximum(j - 1, 0)), rope(kc_ref[...], j),
                            rope(kn_ref[...], jnp.minimum(j + 1, nb - 1)), ck_ref[...]], axis=0)
    vals = jnp.concatenate([vp_ref[...], vc_ref[...], vn_ref[...], cv_ref[...]], axis=0)
    keys2 = (keys.astype(BF16), pltpu.roll(keys, 64, 1).astype(BF16))
    vals2 = (vals.astype(BF16), pltpu.roll(vals, 64, 1).astype(BF16))
    kcol = lax.broadcasted_iota(jnp.int32, (1, nwin + PAST_LEN), 1)
    edge = jnp.where(((j == 0) & (kcol < ATT_BLOCK)) | ((j == nb - 1) & (kcol >= 2 * ATT_BLOCK) & (kcol < nwin)),
                     -1e30, 0.0)
    bias = bias_ref[...] + edge
    top = lax.broadcasted_iota(jnp.int32, (2 * ATT_BLOCK, 1), 0) < ATT_BLOCK
    q_tiles = [rope(q_ref[:, t * 128:(t + 1) * 128], j) * (HEAD_DIM ** -0.5) for t in range(ATT_HEADS // 2)]
    lo2 = jnp.concatenate([lo, lo], axis=0)
    units = []
    for kvh in range(ATT_KV_HEADS):
        q2 = jnp.concatenate(q_tiles[2 * kvh:2 * kvh + 2], axis=0)
        for p in range(2):
            qm = jnp.where(lo2 if p == 0 else jnp.logical_not(lo2), q2, 0.0).astype(BF16)
            sink = jnp.where(top, sink_ref[layer, 4 * kvh + p], sink_ref[layer, 4 * kvh + 2 + p])
            units.append((qm, 0 if p == kvh else 1, sink))
    scores = [_dot_nt(qm, keys2[which]) + bias for qm, which, _ in units]
    maxes = [jnp.maximum(sink, jnp.max(s, axis=-1, keepdims=True)) for s, (_, _, sink) in zip(scores, units)]
    probs = [jnp.exp(s - m) for s, m in zip(scores, maxes)]
    dens = [jnp.exp(sink - m) + jnp.sum(p, axis=-1, keepdims=True)
            for p, m, (_, _, sink) in zip(probs, maxes, units)]
    outs = [_dot(p.astype(BF16), vals2[which]) / den for p, den, (_, which, _) in zip(probs, dens, units)]
    for kvh in range(ATT_KV_HEADS):
        o2 = jnp.where(lo2, outs[2 * kvh], outs[2 * kvh + 1])
        for i in range(2):
            t = 2 * kvh + i
            o_ref[:, t * 128:(t + 1) * 128] = o2[i * ATT_BLOCK:(i + 1) * ATT_BLOCK].astype(BF16)


def _attn_lat(sink, cslab, ck, cv, cos_t, sin_t, layer):
    nb = DEC_SEQ // ATT_BLOCK
    base = NTOK_C // ATT_BLOCK
    cur = lambda b, j: base + b * nb + j
    prv = lambda b, j: base + b * nb + jnp.maximum(j - 1, 0)
    nxt = lambda b, j: base + b * nb + jnp.minimum(j + 1, nb - 1)
    kv_spec = lambda row, col: pl.BlockSpec((ATT_BLOCK, ATT_KV), lambda b, j: (row(b, j), col))
    qi = np.arange(2 * ATT_BLOCK)[:, None] % ATT_BLOCK
    kc = np.arange(3 * ATT_BLOCK + PAST_LEN)[None, :]
    inside = (np.abs(kc - ATT_BLOCK - qi) <= WINDOW) | (kc >= 3 * ATT_BLOCK)
    band = np.where(inside, 0.0, -1e30).astype(np.float32)
    cache_spec = pl.BlockSpec((None, None, PAST_LEN, ATT_KV), lambda b, j: (b, layer, 0, 0))
    return pl.pallas_call(
        functools.partial(_attn_lat_kernel, layer=layer),
        grid=(DEC_BATCH, nb),
        in_specs=[
            pl.BlockSpec(memory_space=pltpu.SMEM),
            pl.BlockSpec((ATT_BLOCK, ATT_Q), lambda b, j: (cur(b, j), 0)),
            kv_spec(prv, 4), kv_spec(cur, 4), kv_spec(nxt, 4),
            kv_spec(prv, 5), kv_spec(cur, 5), kv_spec(nxt, 5),
            cache_spec, cache_spec,
            pl.BlockSpec((DEC_SEQ, 128), lambda b, j: (0, 0)),
            pl.BlockSpec((DEC_SEQ, 128), lambda b, j: (0, 0)),
            pl.BlockSpec(band.shape, lambda b, j: (0, 0)),
        ],
        out_specs=pl.BlockSpec((ATT_BLOCK, ATT_Q), lambda b, j: (b * nb + j, 0)),
        out_shape=jax.ShapeDtypeStruct((NTOK_L, ATT_Q), BF16),
        name="attn_lat",
    )(sink, cslab, cslab, cslab, cslab, cslab, cslab, cslab, ck, cv, cos_t, sin_t, jnp.asarray(band))


@functools.lru_cache(maxsize=None)
def _rope_tables():
    rows = DEC_SEQ // GRID_W
    row = np.repeat(np.arange(rows, dtype=np.float64), GRID_W)
    col = np.tile(np.arange(GRID_W, dtype=np.float64), rows)
    quarter = HEAD_DIM // 4
    inv = ROPE_BASE ** (-np.arange(quarter, dtype=np.float64) / quarter)
    lane = np.arange(128)
    use_row = (lane % HEAD_DIM) < HEAD_DIM // 2
    pos = np.where(use_row[None, :], row[:, None], col[:, None])
    ang = pos * inv[lane % quarter][None, :]
    sign = np.where((lane % 32) < 16, -1.0, 1.0)
    return np.cos(ang).astype(np.float32), (np.sin(ang) * sign[None, :]).astype(np.float32)


def _merge_kernel(*refs, split_x):
    if split_x:
        xc_ref, xl_ref, *refs = refs
    else:
        xc_ref, *refs = refs
    (mod_ref, g_ref, ys5_ref, ogf_ref, ogb_ref, gb_ref, ycc_ref, ycl_ref, gate_ref, gng_ref,
     wglu_ref, wbr_ref, wout_ref, o_ref) = refs
    is_ctx = pl.program_id(0) < NTOK_C // TM
    if split_x:
        x = jnp.where(is_ctx, xc_ref[...], xl_ref[...])
    else:
        x = xc_ref[...]
    y = jnp.concatenate([ys5_ref[j] for j in range(S5_SLABS)], axis=1)
    y = 0.5 * y * (1.0 + jnp.tanh(math.sqrt(2.0 / math.pi) * (y + 0.044715 * (y * y * y))))
    ag = _dot(y.astype(BF16), wglu_ref[...])
    y_a = ag[:, :S5_WIDTH] * _sigmoid(ag[:, S5_WIDTH:])
    gng = gng_ref[...]
    parts = []
    for h in range(GLA_HEADS):
        sl = slice(h * GLA_DV, (h + 1) * GLA_DV)
        o = ogf_ref[:, sl] + ogb_ref[:, sl]
        g = gb_ref[:, sl]
        parts.append(_rms(o, gng) * (g * _sigmoid(g)))
    y_b = jnp.concatenate(parts, axis=1)
    y_c = jnp.where(is_ctx, ycc_ref[...], ycl_ref[...])
    merged = None
    for n, yn in enumerate((y_a, y_b, y_c)):
        proj = _dot(yn.astype(BF16), wbr_ref[n])
        term = gate_ref[:, n * D_MODEL:(n + 1) * D_MODEL].astype(F32) * proj
        merged = term if merged is None else merged + term
    mixed = _dot(merged.astype(BF16), wout_ref[...])
    g1 = mod_ref[:, 2 * D_MODEL:3 * D_MODEL]
    o_ref[...] = x + g1 * _rms(mixed, g_ref[...])


def _layer_spec(shape, layer):
    return pl.BlockSpec((None,) + shape, lambda i: (layer,) + (0,) * len(shape), pipeline_mode=pl.Buffered(1))


def _split_token_specs(n_arrays, width=D_MODEL):
    nct = NTOK_C // TM
    if n_arrays == 2:
        return [pl.BlockSpec((TM, width), lambda i: (jnp.minimum(i, nct - 1), 0)),
                pl.BlockSpec((TM, width), lambda i: (jnp.maximum(i - nct, 0), 0))]
    return [pl.BlockSpec((TM, width), lambda i: (i, 0))]


def _merge(xs, mod, g, ys5, og, bslab, yc, gates, gng, wglu, wbr, wout, layer):
    tok = lambda width, col=0: pl.BlockSpec((TM, width), lambda i: (i, col))
    full = lambda shape: _layer_spec(shape, layer)
    return pl.pallas_call(
        functools.partial(_merge_kernel, split_x=len(xs) == 2),
        grid=(NTOK // TM,),
        in_specs=_split_token_specs(len(xs)) + [
            _mod_spec(layer),
            _gain_spec(layer, 1),
            pl.BlockSpec((S5_SLABS, TM, 128), lambda i: (0, _s5_tile(i), 0)),
            tok(GLA_V),
            tok(GLA_V),
            tok(GLA_V, 2),
        ] + _split_token_specs(2, ATT_Q) + [
            tok(N_BRANCH * D_MODEL),
            pl.BlockSpec((None, 1, GLA_DV), lambda i: (layer, 0, 0)),
            full((S5_WIDTH, 2 * S5_WIDTH)),
            full((N_BRANCH, BRANCH_W, D_MODEL)),
            full((D_MODEL, D_MODEL)),
        ],
        out_specs=tok(D_MODEL),
        out_shape=jax.ShapeDtypeStruct((NTOK, D_MODEL), F32),
        compiler_params=pltpu.CompilerParams(vmem_limit_bytes=VMEM_LIMIT),
        name="merge",
    )(*xs, mod, g, ys5, *og, bslab, *yc, gates, gng, wglu, wbr, wout)


FFN_SPLIT = 1


def _ffn_kernel(x_ref, mod_ref, gin_ref, gout_ref, w1_ref, w2_ref, *o_refs):
    x = x_ref[...]
    sh = mod_ref[:, 3 * D_MODEL:4 * D_MODEL]
    sc = mod_ref[:, 4 * D_MODEL:5 * D_MODEL]
    g2 = mod_ref[:, 5 * D_MODEL:6 * D_MODEL]
    h = (_rms(x, gin_ref[...]) * (1.0 + sc) + sh).astype(BF16)
    ck = FFN_HIDDEN // FFN_SPLIT
    acc = None
    for c in range(FFN_SPLIT):
        a = _dot(h, w1_ref[:, c * ck:(c + 1) * ck])
        b = _dot(h, w1_ref[:, FFN_HIDDEN + c * ck:FFN_HIDDEN + (c + 1) * ck])
        act = (a * _sigmoid(a) * b).astype(BF16)
        part = _dot(act, w2_ref[c * ck:(c + 1) * ck, :])
        acc = part if acc is None else acc + part
    y = x + g2 * _rms(acc, gout_ref[...])
    if len(o_refs) == 1:
        o_refs[0][...] = y
    else:
        is_ctx = pl.program_id(0) < NTOK_C // TM

        @pl.when(is_ctx)
        def _():
            o_refs[0][...] = y

        @pl.when(jnp.logical_not(is_ctx))
        def _():
            o_refs[1][...] = y


def _ffn(x, mod, gains, w1, w2, layer, split_out):
    nct = NTOK_C // TM
    if split_out:
        out_specs = [pl.BlockSpec((TM, D_MODEL), lambda i: (jnp.minimum(i, nct - 1), 0)),
                     pl.BlockSpec((TM, D_MODEL), lambda i: (jnp.maximum(i - nct, 0), 0))]
        out_shape = [jax.ShapeDtypeStruct((NTOK_C, D_MODEL), F32), jax.ShapeDtypeStruct((NTOK_L, D_MODEL), F32)]
    else:
        out_specs = pl.BlockSpec((TM, D_MODEL), lambda i: (i, 0))
        out_shape = jax.ShapeDtypeStruct((NTOK, D_MODEL), F32)
    return pl.pallas_call(
        _ffn_kernel,
        grid=(NTOK // TM,),
        in_specs=[
            pl.BlockSpec((TM, D_MODEL), lambda i: (i, 0)),
            _mod_spec(layer),
            _gain_spec(layer, 2),
            _gain_spec(layer, 3),
            _layer_spec((D_MODEL, 2 * FFN_HIDDEN), layer),
            _layer_spec((FFN_HIDDEN, D_MODEL), layer),
        ],
        out_specs=out_specs,
        out_shape=out_shape,
        compiler_params=pltpu.CompilerParams(vmem_limit_bytes=VMEM_LIMIT),
        name="ffn",
    )(x, mod, gains, gains, w1, w2)


def kernel(x_prompt, x_sample, cache_k, cache_v, state_s5, state_gla, c, c_ctx, w_mod, b_mod, norm_g, w_in,
           s5_lam_re, s5_lam_im, s5_log_step, s5_b_re, s5_b_im, s5_c_re, s5_c_im, s5_d, w_glu, gla_w_gk,
           gla_b_gk, gla_norm_g, att_sink, w_branch, w_out, w_ffn_in, w_ffn_out):
    cond = jnp.concatenate([c_ctx[None, :], c, jnp.zeros((N_MOD_ROWS - 1 - DEC_BATCH, D_MODEL), F32)], axis=0)
    mod_all = _modulation(cond, w_mod, b_mod).reshape(DEPTH, N_MOD_ROWS, 1, 6 * D_MODEL)
    cos_t, sin_t = _rope_tables()
    xs = (x_prompt.reshape(NTOK_C, D_MODEL), x_sample.reshape(NTOK_L, D_MODEL))
    w_in_b = w_in.astype(BF16)
    w_in_end = jnp.pad(w_in[:, :, D_IN_TILED:].astype(BF16), ((0, 0), (0, 0), (0, W_IN_COLS - D_IN)))
    w_glu_b, w_branch_b, w_out_b = w_glu.astype(BF16), w_branch.astype(BF16), w_out.astype(BF16)
    w_ffn_in_b, w_ffn_out_b = w_ffn_in.astype(BF16), w_ffn_out.astype(BF16)
    s5_params = _s5_params(s5_lam_re, s5_lam_im, s5_log_step, s5_b_re, s5_b_im, s5_c_re, s5_c_im)
    h0_all = state_s5.astype(F32).transpose(1, 0, 3, 5, 2, 4).reshape(DEPTH, DEC_BATCH, S5_GROUPS * 256)
    wgk_all = jnp.stack([jnp.pad(gla_w_gk[:, d], ((0, 0), (d * GLA_RANK, 128 - (d + 1) * GLA_RANK), (0, 0)))
                         for d in range(2)], axis=1).astype(BF16)
    bgk_all = gla_b_gk[:, :, None, :].astype(F32)
    gains = norm_g.astype(F32).reshape(DEPTH * 4, 1, D_MODEL)
    gla_gain = gla_norm_g.astype(F32).reshape(DEPTH, 1, GLA_DV)
    sink = att_sink.astype(F32)
    cache_k2 = cache_k.astype(F32).reshape(DEC_BATCH, DEPTH, PAST_LEN, ATT_KV)
    cache_v2 = cache_v.astype(F32).reshape(DEC_BATCH, DEPTH, PAST_LEN, ATT_KV)
    state_gla = state_gla.astype(F32)
    mod = mod_all
    new_k, new_v, new_s5, new_gla = [], [], [], []
    for i in range(DEPTH):
        uj, bslab, cslab, gates, lr = _inproj(xs, mod, gains, w_in_b, w_in_end, i)

        wt, web, wca, a16, dj = _s5_prep(s5_params, s5_d[i], i)
        hin, finc = _s5_scan(_s5_state(uj, web), a16, h0_all[i])
        ys5 = _s5_out(uj, hin, wt, wca, dj)
        new_s5.append(finc)

        *og, gla_fin = _gla_mix(bslab, lr, wgk_all, bgk_all, state_gla, i)
        new_gla.append(gla_fin[:BATCH])

        yc_ctx, k_new, v_new = _attn_ctx(sink, cslab, i)
        yc = (yc_ctx, _attn_lat(sink, cslab, cache_k2, cache_v2, cos_t, sin_t, i))
        new_k.append(k_new.reshape(BATCH, SEQ, ATT_KV_HEADS, HEAD_DIM))
        new_v.append(v_new.reshape(BATCH, SEQ, ATT_KV_HEADS, HEAD_DIM))

        x = _merge(xs, mod, gains, ys5, og, bslab, yc, gates, gla_gain, w_glu_b, w_branch_b, w_out_b, i)
        last = i == DEPTH - 1
        x = _ffn(x, mod, gains, w_ffn_in_b, w_ffn_out_b, i, last)
        xs = tuple(x) if last else (x,)

    return (xs[0].reshape(BATCH, SEQ, D_MODEL), xs[1].reshape(DEC_BATCH, DEC_SEQ, D_MODEL),
            jnp.stack(new_k, axis=1), jnp.stack(new_v, axis=1),
            jnp.stack(new_s5).reshape(DEPTH, BATCH, S5_GROUPS, 2, 2, S5_STATE).transpose(1, 0, 4, 2, 5, 3),
            jnp.stack(new_gla, axis=1))
```

```python
import functools
import math

import numpy as np
import jax
import jax.numpy as jnp
from jax import lax
from jax.experimental import pallas as pl
from jax.experimental.pallas import tpu as pltpu

F32 = jnp.float32
BF16 = jnp.bfloat16

D_MODEL = 1024
BATCH = 16
SEQ = 256
DEPTH = 2
DEC_BATCH = 8
DEC_SEQ = 1024
PAST_LEN = 256
GRID_W = 64
ROPE_BASE = 10000.0
S5_WIDTH = 512
S5_GROUP = 16
S5_GROUPS = 32
S5_STATE = 64
GLA_HEADS = 4
GLA_DK = 64
GLA_DV = 128
GLA_QK = 256
GLA_V = 512
GLA_RANK = 16
GLA_NORMALIZER = 16.0
ATT_HEADS = 8
ATT_KV_HEADS = 2
HEAD_DIM = 64
ATT_Q = 512
ATT_KV = 128
WINDOW = 128
ATT_BLOCK = 128
N_BRANCH = 3
BRANCH_W = 512
FFN_HIDDEN = 2816
RMS_EPS = 1e-6

NTOK_C = BATCH * SEQ
NTOK_L = DEC_BATCH * DEC_SEQ
NTOK = NTOK_C + NTOK_L
TM = 512
N_MOD_ROWS = 16

D_IN = 5920
D_IN_TILED = D_IN // 128 * 128
W_IN_COLS = 6016
S5_CHUNK = 16
S5_SLABS = S5_WIDTH // 128
S5_SLAB_W = S5_CHUNK * 128
S5_ROWS_C = NTOK_C // S5_CHUNK
S5_ROWS = NTOK // S5_CHUNK
S5_ROW_TILE = 256
GLA_BLK = 256
GLA_LEVELS = 8
VMEM_LIMIT = 56 * 1024 * 1024


def _dot(a, b):
    return jnp.dot(a, b, preferred_element_type=F32)


def _dot_nt(a, b):
    return lax.dot_general(a, b, (((1,), (1,)), ((), ())), preferred_element_type=F32)


def _dot_tn(a, b):
    return lax.dot_general(a, b, (((0,), (0,)), ((), ())), preferred_element_type=F32)


def _rms(x, g):
    return x * lax.rsqrt(jnp.mean(x * x, axis=-1, keepdims=True) + RMS_EPS) * g


def _sigmoid(x):
    return 0.5 * jnp.tanh(0.5 * x) + 0.5


def _mod_row(i):
    nct = NTOK_C // TM
    return jnp.where(i < nct, 0, 1 + (i - nct) // (DEC_SEQ // TM))


def _mod_spec(layer):
    return pl.BlockSpec((None, None, 1, 6 * D_MODEL), lambda i: (layer, _mod_row(i), 0, 0))


def _gain_spec(layer, k):
    return pl.BlockSpec((None, 1, D_MODEL), lambda i: (layer * 4 + k, 0, 0))


def _s5_tile(i):
    nct = NTOK_C // TM
    per_seq = DEC_SEQ // TM
    k = i - nct
    return jnp.where(i < nct, i, nct + (k % per_seq) * DEC_BATCH + k // per_seq)


def _mod_kernel(c_ref, w_ref, b_ref, o_ref):
    c = c_ref[...]
    s = (c * _sigmoid(c)).astype(BF16)
    o_ref[...] = _dot(s, w_ref[...].astype(BF16)) + b_ref[...]


def _modulation(cond, w_mod, b_mod):
    tn = 2048
    return pl.pallas_call(
        _mod_kernel,
        grid=(DEPTH, 6 * D_MODEL // tn),
        in_specs=[
            pl.BlockSpec((N_MOD_ROWS, D_MODEL), lambda l, n: (0, 0)),
            pl.BlockSpec((None, D_MODEL, tn), lambda l, n: (l, 0, n)),
            pl.BlockSpec((None, 1, tn), lambda l, n: (l, 0, n)),
        ],
        out_specs=pl.BlockSpec((None, N_MOD_ROWS, tn), lambda l, n: (l, 0, n)),
        out_shape=jax.ShapeDtypeStruct((DEPTH, N_MOD_ROWS, 6 * D_MODEL), F32),
        name="modulation",
    )(cond, w_mod, b_mod.reshape(DEPTH, 1, 6 * D_MODEL))


_IN_SLABS = ((0, 512), (512, 1536), (2048, 768), (2816, 3072), (5888, 128))
W_IN_SPLIT = 2048
W_IN_GAP = 32
W_IN_TAIL = W_IN_COLS - W_IN_SPLIT


def _inproj_kernel(*refs, split_x):
    if split_x:
        xc_ref, xl_ref, *refs = refs
    else:
        xc_ref, *refs = refs
    mod_ref, g_ref, w_ref, w_end_ref, u_ref, b_ref, c_ref, gate_ref, lr_ref, w_tail, u_stage = refs
    i = pl.program_id(0)

    @pl.when(i == 0)
    def _():
        r = lax.broadcasted_iota(jnp.int32, (256, 128), 0)
        c = lax.broadcasted_iota(jnp.int32, (256, 128), 1)
        shift = (r == c + W_IN_GAP).astype(BF16)
        head = ((r == c) & (c < W_IN_GAP)).astype(BF16)
        ntile = (W_IN_TAIL - 128) // 128
        for t in range(ntile - 1):
            src = W_IN_SPLIT + 128 * t
            w_tail[:, 128 * t:128 * (t + 1)] = _dot(w_ref[:, src:src + 256], shift).astype(BF16)
        src = W_IN_SPLIT + 128 * (ntile - 1)
        last = jnp.concatenate([w_ref[:, src:src + 128], w_end_ref[...]], axis=1)
        w_tail[:, 128 * (ntile - 1):128 * ntile] = _dot(last, shift).astype(BF16)
        w_tail[:, 128 * ntile:] = _dot(w_ref[:, W_IN_SPLIT:W_IN_SPLIT + 256], head).astype(BF16)

    if split_x:
        x = jnp.where(i < NTOK_C // TM, xc_ref[...], xl_ref[...])
    else:
        x = xc_ref[...]
    mod = mod_ref[...]
    h = _rms(x, g_ref[...]) * (1.0 + mod[:, D_MODEL:2 * D_MODEL]) + mod[:, 0:D_MODEL]
    h = h.astype(BF16)
    z_head = _dot(h, w_ref[:, 0:W_IN_SPLIT])
    z_tail = _dot(h, w_tail[...])
    for j in range(S5_SLABS):
        u_stage[...] = z_head[:, j * 128:(j + 1) * 128]
        for s in range(S5_CHUNK):
            u_ref[j, :, s * 128:(s + 1) * 128] = u_stage[pl.ds(s, TM // S5_CHUNK, stride=S5_CHUNK), :]
    b_ref[...] = z_head[:, S5_WIDTH:W_IN_SPLIT]
    for (off, width), o_ref in zip(_IN_SLABS[2:], (c_ref, gate_ref, lr_ref)):
        z = z_tail[:, off - W_IN_SPLIT:off - W_IN_SPLIT + width]
        o_ref[...] = _sigmoid(z).astype(BF16) if o_ref is gate_ref else z


def _inproj(xs, mod, g, w_all, w_end, layer):
    return pl.pallas_call(
        functools.partial(_inproj_kernel, split_x=len(xs) == 2),
        grid=(NTOK // TM,),
        in_specs=_split_token_specs(len(xs)) + [
            _mod_spec(layer),
            _gain_spec(layer, 0),
            pl.BlockSpec((None, D_MODEL, D_IN), lambda i: (layer, 0, 0), pipeline_mode=pl.Buffered(1)),
            pl.BlockSpec((None, D_MODEL, 128), lambda i: (layer, 0, 0), pipeline_mode=pl.Buffered(1)),
        ],
        out_specs=[pl.BlockSpec((S5_SLABS, TM // S5_CHUNK, S5_SLAB_W), lambda i: (0, _s5_tile(i), 0))]
        + [pl.BlockSpec((TM, width), lambda i: (i, 0)) for _, width in _IN_SLABS[1:]],
        out_shape=[jax.ShapeDtypeStruct((S5_SLABS, S5_ROWS, S5_SLAB_W), F32)]
        + [jax.ShapeDtypeStruct((NTOK, width), BF16 if width == N_BRANCH * D_MODEL else F32)
           for _, width in _IN_SLABS[1:]],
        scratch_shapes=[pltpu.VMEM((D_MODEL, W_IN_TAIL), BF16), pltpu.VMEM((TM, 128), F32)],
        compiler_params=pltpu.CompilerParams(vmem_limit_bytes=VMEM_LIMIT),
        name="inproj",
    )(*xs, mod, g, w_all, w_end)


@functools.lru_cache(maxsize=None)
def _s5_expanders():
    seg = 8
    spread = np.zeros((seg, 256, S5_SLAB_W), np.float32)
    col = np.arange(256)
    for gl in range(seg):
        spread[gl, col, (col // S5_GROUP) * 128 + gl * S5_GROUP + col % S5_GROUP] = 1.0
    return spread


def _s5_prep_kernel(par_ref, bre_ref, bim_ref, cre_ref, cim_ref, spread_ref,
                    wt_ref, web_ref, wca_ref, a16_ref):
    n = S5_CHUNK
    lam_re = par_ref[0:1, :]
    lam_im = par_ref[1:2, :]
    dt = jnp.exp(par_ref[2:3, :])
    lr = lam_re * dt
    li = lam_im * dt
    krow = lax.broadcasted_iota(jnp.int32, (24, 128), 0).astype(F32)
    tab_mag = jnp.exp(krow * lr)
    tab_re = tab_mag * jnp.cos(krow * li)
    tab_im = tab_mag * jnp.sin(krow * li)
    ar = tab_re[1:2, :]
    ai = tab_im[1:2, :]
    nr = ar - 1.0
    den = lam_re * lam_re + lam_im * lam_im
    fr = (nr * lam_re + ai * lam_im) / den
    fi = (ai * lam_re - nr * lam_im) / den
    b_re = bre_ref[...]
    b_im = bim_ref[...]
    br = fr * b_re - fi * b_im
    bi = fr * b_im + fi * b_re
    c_re = cre_ref[...]
    c_im = cim_ref[...]

    def lo_half(shape):
        return lax.broadcasted_iota(jnp.int32, shape, 1) < S5_STATE

    def tile_rows(a):
        return jnp.concatenate([a] * n, axis=0)

    fwd16 = lo_half((S5_GROUP, 128))

    def powers(t_re, t_im, k_fwd, k_bwd):
        def pick(t, b):
            kf, kb = k_fwd(b), k_bwd(b)
            return jnp.where(fwd16, jnp.broadcast_to(t[kf:kf + 1, :], (S5_GROUP, 128)),
                             jnp.broadcast_to(t[kb:kb + 1, :], (S5_GROUP, 128)))
        return (jnp.concatenate([pick(t_re, b) for b in range(n)], axis=0),
                jnp.concatenate([pick(t_im, b) for b in range(n)], axis=0))

    fwd = lo_half((n * S5_GROUP, 128))
    brt, bit, crt, cit = tile_rows(br), tile_rows(bi), tile_rows(c_re), tile_rows(c_im)

    per, pei = powers(tab_re, tab_im, lambda s: n - 1 - s, lambda s: s)
    eb = jnp.concatenate([brt * per - bit * pei, brt * pei + bit * per], axis=1)
    pcr, pci = powers(tab_re, tab_im, lambda t: t + 1, lambda t: n - t)
    ca = jnp.concatenate([(crt * pcr - cit * pci).T, (-(crt * pci + cit * pcr)).T], axis=0)

    def one_dir(x, d):
        sw = pltpu.roll(x, S5_STATE, 1)
        lo = lo_half(x.shape)
        return jnp.where(lo, x, sw) if d == 0 else jnp.where(lo, sw, x)

    klag = []
    for d in range(2):
        lhs = jnp.where(lo_half(br.shape), one_dir(br, d), -one_dir(bi, d))
        crd, cid = tile_rows(one_dir(c_re, d)), tile_rows(one_dir(c_im, d))
        lag = (lambda b: b) if d == 0 else (lambda b: n - 1 - b)
        pr, pi = powers(one_dir(tab_re, d), one_dir(tab_im, d), lag, lag)
        rhs_t = jnp.where(fwd, crd * pr - cid * pi, crd * pi + cid * pr)
        klag.append(lax.dot_general(lhs, rhs_t, (((1,), (1,)), ((), ())),
                                    precision=lax.Precision.HIGHEST, preferred_element_type=F32))
    lane = lax.broadcasted_iota(jnp.int32, (S5_GROUP, n * S5_GROUP), 1)
    rows = []
    for s in range(n):
        f = klag[0] if s == 0 else jnp.where(lane >= S5_GROUP * s, pltpu.roll(klag[0], S5_GROUP * s, 1), 0.0)
        sh = (n * S5_GROUP - S5_GROUP * (n - 1 - s)) % (n * S5_GROUP)
        b = klag[1] if sh == 0 else pltpu.roll(klag[1], sh, 1)
        rows.append(f + jnp.where(lane < S5_GROUP * (s + 1), b, 0.0))
    toep = jnp.concatenate(rows, axis=0)

    spread = spread_ref[...]
    wt_ref[...] = _dot(toep.astype(BF16), spread).astype(BF16).reshape(n, S5_GROUP, S5_SLAB_W)
    wca_ref[...] = _dot(ca.astype(BF16), spread).astype(BF16)
    a16_ref[0:1, :] = tab_re[n:n + 1, :]
    a16_ref[1:2, :] = tab_im[n:n + 1, :]
    web_ref[...] = jnp.zeros_like(web_ref)
    eb3 = eb.astype(BF16).reshape(n, S5_GROUP, 256)
    for pos in range(S5_SLAB_W // 256):
        @pl.when(pl.program_id(0) == pos)
        def _():
            web_ref[:, :, pos * 256:(pos + 1) * 256] = eb3


def _s5_params(lam_re, lam_im, log_step, b_re, b_im, c_re, c_im):
    par = jnp.stack([lam_re, lam_im, log_step], axis=1).astype(F32)
    par = par.transpose(0, 3, 1, 2, 4).reshape(DEPTH, S5_GROUPS, 3, 128)
    par = jnp.pad(par, ((0, 0), (0, 0), (0, 5), (0, 0)))
    b_t = lambda b: b.astype(F32).transpose(0, 2, 4, 1, 3).reshape(DEPTH, S5_GROUPS, S5_GROUP, 128)
    c_t = lambda c: c.astype(F32).transpose(0, 2, 3, 1, 4).reshape(DEPTH, S5_GROUPS, S5_GROUP, 128)
    return par, b_t(b_re), b_t(b_im), c_t(c_re), c_t(c_im)


def _s5_prep(params, d_skip, layer):
    seg = 8
    spread = _s5_expanders()
    vec = pl.BlockSpec((None, None, S5_GROUP, 128), lambda gl, j: (layer, j * seg + gl, 0, 0))
    exp_spec = pl.BlockSpec((None, 256, S5_SLAB_W), lambda gl, j: (gl, 0, 0))
    rows_spec = pl.BlockSpec((None, S5_CHUNK, None, S5_GROUP, S5_SLAB_W), lambda gl, j: (j, 0, gl, 0, 0))
    wt, web, wca, a16 = pl.pallas_call(
        _s5_prep_kernel,
        grid=(seg, S5_SLABS),
        in_specs=[pl.BlockSpec((None, None, 8, 128), lambda gl, j: (layer, j * seg + gl, 0, 0)),
                  vec, vec, vec, vec, exp_spec],
        out_specs=[
            rows_spec, rows_spec,
            pl.BlockSpec((None, None, 256, S5_SLAB_W), lambda gl, j: (j, gl, 0, 0)),
            pl.BlockSpec((None, 2, 128), lambda gl, j: (j * seg + gl, 0, 0)),
        ],
        out_shape=[
            jax.ShapeDtypeStruct((S5_SLABS, S5_CHUNK, seg, S5_GROUP, S5_SLAB_W), BF16),
            jax.ShapeDtypeStruct((S5_SLABS, S5_CHUNK, seg, S5_GROUP, S5_SLAB_W), BF16),
            jax.ShapeDtypeStruct((S5_SLABS, seg, 256, S5_SLAB_W), BF16),
            jax.ShapeDtypeStruct((S5_GROUPS, 2, 128), F32),
        ],
        name="s5_prep",
    )(*params, jnp.asarray(spread, BF16))
    mat = (S5_SLABS, S5_SLAB_W, S5_SLAB_W)
    dj = jnp.tile(d_skip.astype(F32).reshape(S5_SLABS, 1, 128), (1, 1, S5_CHUNK))
    return wt.reshape(mat), web.reshape(mat), wca.reshape(mat), a16.reshape(1, S5_SLABS * S5_SLAB_W), dj


S5_STATE_COLS = S5_SLABS * S5_SLAB_W // 128
S5_SLAB_COLS = S5_SLAB_W // 128


@functools.lru_cache(maxsize=None)
def _s5_row_perms():
    assert S5_ROW_TILE == S5_ROWS_C == DEC_BATCH * TM // S5_CHUNK
    perm = np.zeros((3, S5_ROW_TILE, S5_ROW_TILE), np.float32)
    for p, (nseq, nchunk) in enumerate(((BATCH, SEQ // S5_CHUNK), (DEC_BATCH, TM // S5_CHUNK),
                                        (DEC_BATCH, TM // S5_CHUNK))):
        b, c = np.meshgrid(np.arange(nseq), np.arange(nchunk), indexing="ij")
        perm[p, (c * nseq + b).ravel(), (b * nchunk + c).ravel()] = 1.0
    return perm, perm.transpose(0, 2, 1).copy()


def _s5_state_kernel(u_ref, perm_ref, w_ref, o_ref):
    u = _dot(perm_ref[...], u_ref[...].astype(BF16)).astype(BF16)
    s = _dot(u, w_ref[...])
    for k in range(S5_SLAB_COLS):
        o_ref[k] = s[:, k * 128:(k + 1) * 128]


def _s5_state(uj, web):
    perm, _ = _s5_row_perms()
    return pl.pallas_call(
        _s5_state_kernel,
        grid=(S5_SLABS, S5_ROWS // S5_ROW_TILE),
        in_specs=[
            pl.BlockSpec((None, S5_ROW_TILE, S5_SLAB_W), lambda j, p: (j, p, 0)),
            pl.BlockSpec((None, S5_ROW_TILE, S5_ROW_TILE), lambda j, p: (p, 0, 0)),
            pl.BlockSpec((None, S5_SLAB_W, S5_SLAB_W), lambda j, p: (j, 0, 0)),
        ],
        out_specs=pl.BlockSpec((S5_SLAB_COLS, S5_ROW_TILE, 128), lambda j, p: (j, p, 0)),
        out_shape=jax.ShapeDtypeStruct((S5_STATE_COLS, S5_ROWS, 128), F32),
        compiler_params=pltpu.CompilerParams(vmem_limit_bytes=VMEM_LIMIT),
        name="s5_state",
    )(uj, jnp.asarray(perm, BF16), web)


S5_SCAN_COLS = 8


def _s5_scan_kernel(s_ref, a_ref, h0_ref, hin_ref, fin_ref, hf, hb):
    ncol = S5_SCAN_COLS

    def scan(row0, nc, nb, h0):
        is_f = lax.broadcasted_iota(jnp.int32, (nb, 128), 1) < S5_STATE
        chunk_rows = lambda c: pl.ds(pl.multiple_of(row0 + c * nb, 8), nb)

        def body(c, hs):
            rf = chunk_rows(c)
            rb = chunk_rows(nc - 1 - c)
            new = []
            for m in range(ncol // 2):
                h_re, h_im = hs[2 * m], hs[2 * m + 1]
                a_re = a_ref[:, (2 * m) * 128:(2 * m + 1) * 128]
                a_im = a_ref[:, (2 * m + 1) * 128:(2 * m + 2) * 128]
                loc = []
                for k, h in ((2 * m, h_re), (2 * m + 1, h_im)):
                    hf[k, rf, :] = h
                    hb[k, rb, :] = h
                    loc.append(jnp.where(is_f, s_ref[k, rf, :], s_ref[k, rb, :]))
                new.append(a_re * h_re - a_im * h_im + loc[0])
                new.append(a_re * h_im + a_im * h_re + loc[1])
            return tuple(new)

        return lax.fori_loop(0, nc, body, h0)

    fin = scan(0, SEQ // S5_CHUNK, BATCH, tuple(jnp.zeros((BATCH, 128), F32) for _ in range(ncol)))
    for k in range(ncol):
        fin_ref[:, k * 128:(k + 1) * 128] = fin[k]
    scan(S5_ROWS_C, DEC_SEQ // S5_CHUNK, DEC_BATCH,
         tuple(h0_ref[:, k * 128:(k + 1) * 128] for k in range(ncol)))
    fwd = lax.broadcasted_iota(jnp.int32, (ncol, S5_ROWS, 128), 2) < S5_STATE
    hin_ref[...] = jnp.where(fwd, hf[...], hb[...]).astype(BF16)


def _s5_scan(sloc, a16, h0l):
    ncol = S5_SCAN_COLS
    w = ncol * 128
    return pl.pallas_call(
        _s5_scan_kernel,
        grid=(S5_STATE_COLS // ncol,),
        in_specs=[
            pl.BlockSpec((ncol, S5_ROWS, 128), lambda k: (k, 0, 0)),
            pl.BlockSpec((1, w), lambda k: (0, k)),
            pl.BlockSpec((DEC_BATCH, w), lambda k: (0, k)),
        ],
        out_specs=[
            pl.BlockSpec((ncol, S5_ROWS, 128), lambda k: (k, 0, 0)),
            pl.BlockSpec((BATCH, w), lambda k: (0, k)),
        ],
        out_shape=[
            jax.ShapeDtypeStruct((S5_STATE_COLS, S5_ROWS, 128), BF16),
            jax.ShapeDtypeStruct((BATCH, S5_STATE_COLS * 128), F32),
        ],
        scratch_shapes=[pltpu.VMEM((ncol, S5_ROWS, 128), F32)] * 2,
        name="s5_scan",
    )(sloc, a16, h0l)


def _s5_out_kernel(u_ref, hin_ref, perm_t_ref, wt_ref, wca_ref, d_ref, y_ref):
    u = u_ref[...]
    hin = jnp.concatenate([hin_ref[k] for k in range(S5_SLAB_COLS)], axis=1).astype(BF16)
    hin = _dot(perm_t_ref[...], hin).astype(BF16)
    y = _dot(u.astype(BF16), wt_ref[...]) + _dot(hin, wca_ref[...]) + u * d_ref[...]
    for t in range(S5_CHUNK):
        y_ref[pl.ds(t, S5_ROW_TILE, stride=S5_CHUNK), :] = y[:, t * 128:(t + 1) * 128]


def _s5_out(uj, hin, wt, wca, dj):
    _, perm_t = _s5_row_perms()
    return pl.pallas_call(
        _s5_out_kernel,
        grid=(S5_SLABS, S5_ROWS // S5_ROW_TILE),
        in_specs=[
            pl.BlockSpec((None, S5_ROW_TILE, S5_SLAB_W), lambda j, p: (j, p, 0)),
            pl.BlockSpec((S5_SLAB_COLS, S5_ROW_TILE, 128), lambda j, p: (j, p, 0)),
            pl.BlockSpec((None, S5_ROW_TILE, S5_ROW_TILE), lambda j, p: (p, 0, 0)),
            pl.BlockSpec((None, S5_SLAB_W, S5_SLAB_W), lambda j, p: (j, 0, 0)),
            pl.BlockSpec((None, S5_SLAB_W, S5_SLAB_W), lambda j, p: (j, 0, 0)),
            pl.BlockSpec((None, 1, S5_SLAB_W), lambda j, p: (j, 0, 0)),
        ],
        out_specs=pl.BlockSpec((None, S5_ROW_TILE * S5_CHUNK, 128), lambda j, p: (j, p, 0)),
        out_shape=jax.ShapeDtypeStruct((S5_SLABS, NTOK, 128), F32),
        compiler_params=pltpu.CompilerParams(vmem_limit_bytes=VMEM_LIMIT),
        name="s5_out",
    )(uj, hin, jnp.asarray(perm_t, BF16), wt, wca, dj)


@functools.lru_cache(maxsize=None)
def _gla_consts():
    n = GLA_BLK
    nl = GLA_LEVELS
    r = np.arange(n)
    up = np.zeros((n, 128), np.int32)
    for l in range(nl):
        up[:, l] = (r >> l) & 1
    i = r[:, None]
    j = r[None, :]
    x = np.maximum(i ^ j, 1)
    lev = np.where(j < i, np.floor(np.log2(x)).astype(np.int32), np.where(i == j, nl, -1)).astype(np.int32)
    up2 = np.stack([up, up[::-1]])
    h = n // 2

    def tiled(a):
        return np.stack([np.concatenate([a[:h, :h], a[h:, h:]]), np.concatenate([a[:h, h:], a[h:, :h]])])

    lev2 = np.stack([tiled(lev), tiled(lev[::-1, ::-1])])
    return up2, lev2


@functools.lru_cache(maxsize=None)
def _gla_tables():
    rowblk, seq, first, last = [], [], [], []
    for d in range(2):
        rb, sq, fi, la = [], [], [], []
        for s in range(BATCH + DEC_BATCH):
            nblk = 1 if s < BATCH else DEC_SEQ // GLA_BLK
            base = s if s < BATCH else NTOK_C // GLA_BLK + (s - BATCH) * nblk
            order = range(nblk) if d == 0 else range(nblk - 1, -1, -1)
            for pos, b in enumerate(order):
                rb.append(base + b)
                sq.append(s)
                fi.append(int(pos == 0))
                la.append(int(pos == nblk - 1))
        rowblk.append(rb); seq.append(sq); first.append(fi); last.append(la)
    as_np = lambda a: np.asarray(a, np.int32)
    return as_np(rowblk), as_np(seq), as_np(first), as_np(last)


def _gla_kernel(rowblk_ref, seq_ref, first_ref, last_ref,
                qf_ref, kf_ref, vf_ref, lrf_ref, qb_ref, kb_ref, vb_ref, lrb_ref,
                wgk_ref, bgk_ref, up_ref, lev_ref, s0_ref,
                of_ref, ob_ref, fin_ref, z_scr, st_scr):
    del rowblk_ref
    n = pl.program_id(0)

    @pl.when(first_ref[n] == 1)
    def _():
        latent = seq_ref[n] >= BATCH
        st_scr[...] = jnp.zeros_like(st_scr)
        for d in range(2):
            for h in range(GLA_HEADS):
                st_scr[d, h * GLA_DK:(h + 1) * GLA_DK, h * GLA_DV:(h + 1) * GLA_DV] = jnp.where(
                    latent, s0_ref[d, h], 0.0)

    blocks = [
        _gla_block(False, qf_ref, kf_ref, vf_ref, lrf_ref, wgk_ref.at[0], bgk_ref.at[0], up_ref.at[0],
                   lev_ref.at[0], of_ref, z_scr.at[0], st_scr.at[0]),
        _gla_block(True, qb_ref, kb_ref, vb_ref, lrb_ref, wgk_ref.at[1], bgk_ref.at[1], up_ref.at[1],
                   lev_ref.at[1], ob_ref, z_scr.at[1], st_scr.at[1]),
    ]
    for stage in range(2):
        for block in blocks:
            next(block, None)

    @pl.when(last_ref[n] == 1)
    def _():
        for d in range(2):
            for h in range(GLA_HEADS):
                fin_ref[d, h] = st_scr[d, h * GLA_DK:(h + 1) * GLA_DK, h * GLA_DV:(h + 1) * GLA_DV]


def _gla_block(backward, q_ref, k_ref, v_ref, lr_ref, wgk_ref, bgk_ref, up_ref, lev_ref, o_ref, z_scr, st_scr):
    nl = GLA_LEVELS
    blk = GLA_BLK
    q = q_ref[...] * (GLA_DK ** -0.5)
    k = k_ref[...]
    vb = v_ref[...].astype(BF16)
    x = _dot(lr_ref[...].astype(BF16), wgk_ref[...]) + bgk_ref[...]
    gk = (jnp.minimum(x, 0.0) - jnp.log(1.0 + jnp.exp(-jnp.abs(x)))) * (1.0 / GLA_NORMALIZER)

    row = lax.broadcasted_iota(jnp.int32, (blk, 1), 0)

    def sibling(a, l):
        g = 1 << l
        if g < 8:
            a3 = a.reshape(blk // 8, 8, a.shape[-1])
            dn = pltpu.roll(a3, g, 1).reshape(a.shape)
            up_ = pltpu.roll(a3, 8 - g, 1).reshape(a.shape)
            return jnp.where(((row >> l) & 1) == 1, dn, up_)
        a4 = a.reshape(blk // (2 * g), 2, g, a.shape[-1])
        return jnp.concatenate([a4[:, 1:2], a4[:, 0:1]], axis=1).reshape(a.shape)

    part = gk
    total = gk
    z0 = None
    for l in range(nl):
        g = 1 << l
        if g < 8:
            up = up_ref[:, l:l + 1] != 0
            z = jnp.where(up, q, k) * jnp.exp(jnp.where(up, part, total - part))
            other = sibling(total, l)
            part = part + jnp.where(up, other, 0.0)
            total = total + other
        else:
            halves = lambda a: (a.reshape(blk // (2 * g), 2, g, a.shape[-1])[:, 1 - int(backward)],
                                a.reshape(blk // (2 * g), 2, g, a.shape[-1])[:, int(backward)])
            join = lambda u, d: jnp.stack([d, u] if not backward else [u, d], axis=1).reshape(blk, u.shape[-1])
            part_u, part_d = halves(part)
            tot_u, tot_d = halves(total)
            q_u, _ = halves(q)
            _, k_d = halves(k)
            z = join(q_u * jnp.exp(part_u), k_d * jnp.exp(tot_d - part_d))
            part = join(part_u + tot_d, part_d)
            both = tot_u + tot_d
            total = join(both, both)
        if l == 0:
            z0 = z
        else:
            z_scr[l] = z.astype(BF16)
    yield
    lane128 = lax.broadcasted_iota(jnp.int32, (GLA_QK, 128), 1)
    dim = lax.broadcasted_iota(jnp.int32, (GLA_QK, 128), 0)
    head_sum = ((dim >> 6) == lane128).astype(BF16)
    pair0 = _dot((z0 * sibling(z0, 0)).astype(BF16), head_sum)
    diag = _dot((q * k).astype(BF16), head_sum)

    half = blk // 2
    lev_d = lev_ref[0]
    lev_o = lev_ref[1]
    lane = lax.broadcasted_iota(jnp.int32, (half, GLA_QK), 1)

    def tiles(l, in_head, crossed):
        z = z_scr[l]
        s = _dot_nt(z, jnp.where(jnp.concatenate([in_head, in_head], axis=0), z, jnp.zeros_like(z)))
        out = [s[r * half:(r + 1) * half, (1 - r if crossed else r) * half:((1 - r if crossed else r) + 1) * half]
               for r in range(2)]
        return jnp.concatenate(out, axis=0)

    upi = 0 if backward else 1
    key_lanes = {}
    for l in range(3, nl - 1):
        g = 1 << l
        c = lax.broadcasted_iota(jnp.int32, (blk // (2 * g), g, 128), 0)
        ln = lax.broadcasted_iota(jnp.int32, (blk // (2 * g), g, 128), 2)
        base = (2 * g * c + (g if backward else 0)) & 127
        key_lanes[l] = (ln >= base) & (ln < base + g)

    heads = range(GLA_HEADS)
    in_head = [(lane >= h * GLA_DK) & (lane < (h + 1) * GLA_DK) for h in heads]
    acc = [jnp.where(lev_d == 0, pair0[:, h:h + 1], jnp.where(lev_d == nl, diag[:, h:h + 1], 0.0)) for h in heads]
    for l in range(1, 3):
        acc = [jnp.where(lev_d == l, tiles(l, in_head[h], False), acc[h]) for h in heads]
    for l in range(3, nl - 1):
        g = 1 << l
        for h in heads:
            acc4 = acc[h].reshape(blk // (2 * g), 2, g, 128)
            s4 = tiles(l, in_head[h], False).reshape(blk // (2 * g), 2, g, 128)
            new_up = jnp.where(key_lanes[l], s4[:, upi], acc4[:, upi])
            pieces = [acc4[:, 0], new_up] if upi == 1 else [new_up, acc4[:, 1]]
            acc[h] = jnp.stack(pieces, axis=1).reshape(blk, 128)
    off = [jnp.where(lev_o == nl - 1, tiles(nl - 1, in_head[h], True), 0.0) for h in heads]
    for h in heads:
        att = jnp.concatenate([jnp.concatenate([acc[h][:half], off[h][:half]], axis=1),
                               jnp.concatenate([off[h][half:], acc[h][half:]], axis=1)], axis=0)
        o_ref[:, h * GLA_DV:(h + 1) * GLA_DV] = _dot(att.astype(BF16), vb[:, h * GLA_DV:(h + 1) * GLA_DV])

    st = st_scr[...]
    q_in = (q * jnp.exp(part)).astype(BF16)
    o_ref[...] += _dot(q_in, st.astype(BF16))
    k_out = (k * jnp.exp(total - part)).astype(BF16)
    kv = _dot_tn(k_out, vb)
    row = lax.broadcasted_iota(jnp.int32, (GLA_QK, GLA_V), 0)
    col = lax.broadcasted_iota(jnp.int32, (GLA_QK, GLA_V), 1)
    same_head = (row >> 6) == (col >> 7)
    decay = jnp.exp(total.T[:, :128])
    decay = jnp.concatenate([decay] * GLA_HEADS, axis=1)
    st_new = decay * st + jnp.where(same_head, kv, 0.0)
    st_scr[...] = st_new


def _gla_mix(bslab, lr, wgk, bgk, state_gla, layer):
    up, lev = _gla_consts()
    rowblk, seq, first, last = _gla_tables()
    nsteps = rowblk.shape[1]
    nseq = BATCH + DEC_BATCH
    nl = GLA_LEVELS
    whole = lambda shape: pl.BlockSpec(shape, lambda n, rb, sq, fi, la: (0,) * len(shape))

    def token_specs(d):
        return [
            pl.BlockSpec((GLA_BLK, GLA_QK), lambda n, rb, sq, fi, la: (rb[d, n], 0)),
            pl.BlockSpec((GLA_BLK, GLA_QK), lambda n, rb, sq, fi, la: (rb[d, n], 1)),
            pl.BlockSpec((GLA_BLK, GLA_V), lambda n, rb, sq, fi, la: (rb[d, n], 1)),
            pl.BlockSpec((GLA_BLK, 128), lambda n, rb, sq, fi, la: (rb[d, n], 0)),
        ]

    state_spec = pl.BlockSpec((None, 2, GLA_HEADS, GLA_DK, GLA_DV), lambda n, rb, sq, fi, la: (sq[n], 0, 0, 0, 0))
    of_layer = lambda shape: pl.BlockSpec((None,) + shape, lambda n, rb, sq, fi, la: (layer,) + (0,) * len(shape))
    grid_spec = pltpu.PrefetchScalarGridSpec(
        num_scalar_prefetch=4,
        grid=(nsteps,),
        in_specs=token_specs(0) + token_specs(1) + [
            of_layer((2, 128, GLA_QK)),
            of_layer((2, 1, GLA_QK)),
            whole((2, GLA_BLK, 128)),
            whole((2, 2, GLA_BLK, GLA_BLK // 2)),
            pl.BlockSpec((None, None, 2, GLA_HEADS, GLA_DK, GLA_DV),
                         lambda n, rb, sq, fi, la: (jnp.maximum(sq[n] - BATCH, 0), layer, 0, 0, 0, 0)),
        ],
        out_specs=[
            pl.BlockSpec((GLA_BLK, GLA_V), lambda n, rb, sq, fi, la: (rb[0, n], 0)),
            pl.BlockSpec((GLA_BLK, GLA_V), lambda n, rb, sq, fi, la: (rb[1, n], 0)),
            state_spec,
        ],
        scratch_shapes=[
            pltpu.VMEM((2, nl, GLA_BLK, GLA_QK), BF16),
            pltpu.VMEM((2, GLA_QK, GLA_V), F32),
        ],
    )
    return pl.pallas_call(
        _gla_kernel,
        grid_spec=grid_spec,
        out_shape=[
            jax.ShapeDtypeStruct((NTOK, GLA_V), F32),
            jax.ShapeDtypeStruct((NTOK, GLA_V), F32),
            jax.ShapeDtypeStruct((nseq, 2, GLA_HEADS, GLA_DK, GLA_DV), F32),
        ],
        compiler_params=pltpu.CompilerParams(vmem_limit_bytes=VMEM_LIMIT),
        name="gla_mix",
    )(jnp.asarray(rowblk), jnp.asarray(seq[0]), jnp.asarray(first[0]), jnp.asarray(last[0]),
      bslab, bslab, bslab, lr, bslab, bslab, bslab, lr, wgk, bgk, jnp.asarray(up), jnp.asarray(lev), state_gla)


def _attn_ctx_kernel(sink_ref, q_ref, k_ref, v_ref, o_ref, ko_ref, vo_ref, *, layer):
    k = k_ref[...]
    v = v_ref[...]
    ko_ref[...] = k
    vo_ref[...] = v
    ks = (k.astype(BF16), pltpu.roll(k, 64, 1).astype(BF16))
    vs = (v.astype(BF16), pltpu.roll(v, 64, 1).astype(BF16))
    lo = lax.broadcasted_iota(jnp.int32, (SEQ, 128), 1) < HEAD_DIM
    units = []
    for t in range(ATT_HEADS // 2):
        qt = q_ref[:, t * 128:(t + 1) * 128] * (HEAD_DIM ** -0.5)
        for p in range(2):
            qm = jnp.where(lo if p == 0 else jnp.logical_not(lo), qt, 0.0).astype(BF16)
            units.append((qm, 0 if p == t // 2 else 1, sink_ref[layer, 2 * t + p]))
    scores = [_dot_nt(qm, ks[which]) for qm, which, _ in units]
    maxes = [jnp.maximum(sink, jnp.max(s, axis=-1, keepdims=True)) for s, (_, _, sink) in zip(scores, units)]
    probs = [jnp.exp(s - m) for s, m in zip(scores, maxes)]
    dens = [jnp.exp(sink - m) + jnp.sum(p, axis=-1, keepdims=True)
            for p, m, (_, _, sink) in zip(probs, maxes, units)]
    outs = [_dot(p.astype(BF16), vs[which]) / den for p, den, (_, which, _) in zip(probs, dens, units)]
    for t in range(ATT_HEADS // 2):
        o_ref[:, t * 128:(t + 1) * 128] = jnp.where(lo, outs[2 * t], outs[2 * t + 1]).astype(BF16)


def _attn_ctx(sink, cslab, layer):
    kv_out = pl.BlockSpec((None, SEQ, ATT_KV), lambda b: (b, 0, 0))
    return pl.pallas_call(
        functools.partial(_attn_ctx_kernel, layer=layer),
        grid=(BATCH,),
        in_specs=[
            pl.BlockSpec(memory_space=pltpu.SMEM),
            pl.BlockSpec((SEQ, ATT_Q), lambda b: (b, 0)),
            pl.BlockSpec((SEQ, ATT_KV), lambda b: (b, 4)),
            pl.BlockSpec((SEQ, ATT_KV), lambda b: (b, 5)),
        ],
        out_specs=[pl.BlockSpec((SEQ, ATT_Q), lambda b: (b, 0)), kv_out, kv_out],
        out_shape=[jax.ShapeDtypeStruct((NTOK_C, ATT_Q), BF16),
                   jax.ShapeDtypeStruct((BATCH, SEQ, ATT_KV), F32),
                   jax.ShapeDtypeStruct((BATCH, SEQ, ATT_KV), F32)],
        name="attn_ctx",
    )(sink, cslab, cslab, cslab)


def _attn_lat_kernel(sink_ref, q_ref, kp_ref, kc_ref, kn_ref, vp_ref, vc_ref, vn_ref,
                     ck_ref, cv_ref, cos_ref, sin_ref, bias_ref, o_ref, *, layer):
    j = pl.program_id(1)
    nb = DEC_SEQ // ATT_BLOCK
    lane = lax.broadcasted_iota(jnp.int32, (ATT_BLOCK, 128), 1)
    lo = lane < HEAD_DIM
    first16 = (lane & 31) < 16

    def rope(x, blk_idx):
        r0 = pl.multiple_of(blk_idx * ATT_BLOCK, ATT_BLOCK)
        c = cos_ref[pl.ds(r0, ATT_BLOCK), :]
        s = sin_ref[pl.ds(r0, ATT_BLOCK), :]
        xs = jnp.where(first16, pltpu.roll(x, 112, 1), pltpu.roll(x, 16, 1))
        return x * c + xs * s

    nwin = 3 * ATT_BLOCK
    keys = jnp.concatenate([rope(kp_ref[...], jnp.maximum(j - 1, 0)), rope(kc_ref[...], j),
                            rope(kn_ref[...], jnp.minimum(j + 1, nb - 1)), ck_ref[...]], axis=0)
    vals = jnp.concatenate([vp_ref[...], vc_ref[...], vn_ref[...], cv_ref[...]], axis=0)
    keys2 = (keys.astype(BF16), pltpu.roll(keys, 64, 1).astype(BF16))
    vals2 = (vals.astype(BF16), pltpu.roll(vals, 64, 1).astype(BF16))
    kcol = lax.broadcasted_iota(jnp.int32, (1, nwin + PAST_LEN), 1)
    edge = jnp.where(((j == 0) & (kcol < ATT_BLOCK)) | ((j == nb - 1) & (kcol >= 2 * ATT_BLOCK) & (kcol < nwin)),
                     -1e30, 0.0)
    bias = bias_ref[...] + edge
    top = lax.broadcasted_iota(jnp.int32, (2 * ATT_BLOCK, 1), 0) < ATT_BLOCK
    q_tiles = [rope(q_ref[:, t * 128:(t + 1) * 128], j) * (HEAD_DIM ** -0.5) for t in range(ATT_HEADS // 2)]
    lo2 = jnp.concatenate([lo, lo], axis=0)
    units = []
    for kvh in range(ATT_KV_HEADS):
        q2 = jnp.concatenate(q_tiles[2 * kvh:2 * kvh + 2], axis=0)
        for p in range(2):
            qm = jnp.where(lo2 if p == 0 else jnp.logical_not(lo2), q2, 0.0).astype(BF16)
            sink = jnp.where(top, sink_ref[layer, 4 * kvh + p], sink_ref[layer, 4 * kvh + 2 + p])
            units.append((qm, 0 if p == kvh else 1, sink))
    scores = [_dot_nt(qm, keys2[which]) + bias for qm, which, _ in units]
    maxes = [jnp.maximum(sink, jnp.max(s, axis=-1, keepdims=True)) for s, (_, _, sink) in zip(scores, units)]
    probs = [jnp.exp(s - m) for s, m in zip(scores, maxes)]
    dens = [jnp.exp(sink - m) + jnp.sum(p, axis=-1, keepdims=True)
            for p, m, (_, _, sink) in zip(probs, maxes, units)]
    outs = [_dot(p.astype(BF16), vals2[which]) / den for p, den, (_, which, _) in zip(probs, dens, units)]
    for kvh in range(ATT_KV_HEADS):
        o2 = jnp.where(lo2, outs[2 * kvh], outs[2 * kvh + 1])
        for i in range(2):
            t = 2 * kvh + i
            o_ref[:, t * 128:(t + 1) * 128] = o2[i * ATT_BLOCK:(i + 1) * ATT_BLOCK].astype(BF16)


def _attn_lat(sink, cslab, ck, cv, cos_t, sin_t, layer):
    nb = DEC_SEQ // ATT_BLOCK
    base = NTOK_C // ATT_BLOCK
    cur = lambda b, j: base + b * nb + j
    prv = lambda b, j: base + b * nb + jnp.maximum(j - 1, 0)
    nxt = lambda b, j: base + b * nb + jnp.minimum(j + 1, nb - 1)
    kv_spec = lambda row, col: pl.BlockSpec((ATT_BLOCK, ATT_KV), lambda b, j: (row(b, j), col))
    qi = np.arange(2 * ATT_BLOCK)[:, None] % ATT_BLOCK
    kc = np.arange(3 * ATT_BLOCK + PAST_LEN)[None, :]
    inside = (np.abs(kc - ATT_BLOCK - qi) <= WINDOW) | (kc >= 3 * ATT_BLOCK)
    band = np.where(inside, 0.0, -1e30).astype(np.float32)
    cache_spec = pl.BlockSpec((None, None, PAST_LEN, ATT_KV), lambda b, j: (b, layer, 0, 0))
    return pl.pallas_call(
        functools.partial(_attn_lat_kernel, layer=layer),
        grid=(DEC_BATCH, nb),
        in_specs=[
            pl.BlockSpec(memory_space=pltpu.SMEM),
            pl.BlockSpec((ATT_BLOCK, ATT_Q), lambda b, j: (cur(b, j), 0)),
            kv_spec(prv, 4), kv_spec(cur, 4), kv_spec(nxt, 4),
            kv_spec(prv, 5), kv_spec(cur, 5), kv_spec(nxt, 5),
            cache_spec, cache_spec,
            pl.BlockSpec((DEC_SEQ, 128), lambda b, j: (0, 0)),
            pl.BlockSpec((DEC_SEQ, 128), lambda b, j: (0, 0)),
            pl.BlockSpec(band.shape, lambda b, j: (0, 0)),
        ],
        out_specs=pl.BlockSpec((ATT_BLOCK, ATT_Q), lambda b, j: (b * nb + j, 0)),
        out_shape=jax.ShapeDtypeStruct((NTOK_L, ATT_Q), BF16),
        name="attn_lat",
    )(sink, cslab, cslab, cslab, cslab, cslab, cslab, cslab, ck, cv, cos_t, sin_t, jnp.asarray(band))


@functools.lru_cache(maxsize=None)
def _rope_tables():
    rows = DEC_SEQ // GRID_W
    row = np.repeat(np.arange(rows, dtype=np.float64), GRID_W)
    col = np.tile(np.arange(GRID_W, dtype=np.float64), rows)
    quarter = HEAD_DIM // 4
    inv = ROPE_BASE ** (-np.arange(quarter, dtype=np.float64) / quarter)
    lane = np.arange(128)
    use_row = (lane % HEAD_DIM) < HEAD_DIM // 2
    pos = np.where(use_row[None, :], row[:, None], col[:, None])
    ang = pos * inv[lane % quarter][None, :]
    sign = np.where((lane % 32) < 16, -1.0, 1.0)
    return np.cos(ang).astype(np.float32), (np.sin(ang) * sign[None, :]).astype(np.float32)


def _merge_kernel(*refs, split_x):
    if split_x:
        xc_ref, xl_ref, *refs = refs
    else:
        xc_ref, *refs = refs
    (mod_ref, g_ref, ys5_ref, ogf_ref, ogb_ref, gb_ref, ycc_ref, ycl_ref, gate_ref, gng_ref,
     wglu_ref, wbr_ref, wout_ref, o_ref) = refs
    is_ctx = pl.program_id(0) < NTOK_C // TM
    if split_x:
        x = jnp.where(is_ctx, xc_ref[...], xl_ref[...])
    else:
        x = xc_ref[...]
    y = jnp.concatenate([ys5_ref[j] for j in range(S5_SLABS)], axis=1)
    y = 0.5 * y * (1.0 + jnp.tanh(math.sqrt(2.0 / math.pi) * (y + 0.044715 * (y * y * y))))
    ag = _dot(y.astype(BF16), wglu_ref[...])
    y_a = ag[:, :S5_WIDTH] * _sigmoid(ag[:, S5_WIDTH:])
    gng = gng_ref[...]
    parts = []
    for h in range(GLA_HEADS):
        sl = slice(h * GLA_DV, (h + 1) * GLA_DV)
        o = ogf_ref[:, sl] + ogb_ref[:, sl]
        g = gb_ref[:, sl]
        parts.append(_rms(o, gng) * (g * _sigmoid(g)))
    y_b = jnp.concatenate(parts, axis=1)
    y_c = jnp.where(is_ctx, ycc_ref[...], ycl_ref[...])
    merged = None
    for n, yn in ((1, y_b), (2, y_c), (0, y_a)):
        proj = _dot(yn.astype(BF16), wbr_ref[n])
        term = gate_ref[:, n * D_MODEL:(n + 1) * D_MODEL].astype(F32) * proj
        merged = term if merged is None else merged + term
    mixed = _dot(merged.astype(BF16), wout_ref[...])
    g1 = mod_ref[:, 2 * D_MODEL:3 * D_MODEL]
    o_ref[...] = x + g1 * _rms(mixed, g_ref[...])


def _layer_spec(shape, layer):
    return pl.BlockSpec((None,) + shape, lambda i: (layer,) + (0,) * len(shape), pipeline_mode=pl.Buffered(1))


def _split_token_specs(n_arrays, width=D_MODEL):
    nct = NTOK_C // TM
    if n_arrays == 2:
        return [pl.BlockSpec((TM, width), lambda i: (jnp.minimum(i, nct - 1), 0)),
                pl.BlockSpec((TM, width), lambda i: (jnp.maximum(i - nct, 0), 0))]
    return [pl.BlockSpec((TM, width), lambda i: (i, 0))]


def _merge(xs, mod, g, ys5, og, bslab, yc, gates, gng, wglu, wbr, wout, layer):
    tok = lambda width, col=0: pl.BlockSpec((TM, width), lambda i: (i, col))
    full = lambda shape: _layer_spec(shape, layer)
    return pl.pallas_call(
        functools.partial(_merge_kernel, split_x=len(xs) == 2),
        grid=(NTOK // TM,),
        in_specs=_split_token_specs(len(xs)) + [
            _mod_spec(layer),
            _gain_spec(layer, 1),
            pl.BlockSpec((S5_SLABS, TM, 128), lambda i: (0, _s5_tile(i), 0)),
            tok(GLA_V),
            tok(GLA_V),
            tok(GLA_V, 2),
        ] + _split_token_specs(2, ATT_Q) + [
            tok(N_BRANCH * D_MODEL),
            pl.BlockSpec((None, 1, GLA_DV), lambda i: (layer, 0, 0)),
            full((S5_WIDTH, 2 * S5_WIDTH)),
            full((N_BRANCH, BRANCH_W, D_MODEL)),
            full((D_MODEL, D_MODEL)),
        ],
        out_specs=tok(D_MODEL),
        out_shape=jax.ShapeDtypeStruct((NTOK, D_MODEL), F32),
        compiler_params=pltpu.CompilerParams(vmem_limit_bytes=VMEM_LIMIT),
        name="merge",
    )(*xs, mod, g, ys5, *og, bslab, *yc, gates, gng, wglu, wbr, wout)


FFN_SPLIT = 1


def _ffn_kernel(x_ref, mod_ref, gin_ref, gout_ref, w1_ref, w2_ref, *o_refs):
    x = x_ref[...]
    sh = mod_ref[:, 3 * D_MODEL:4 * D_MODEL]
    sc = mod_ref[:, 4 * D_MODEL:5 * D_MODEL]
    g2 = mod_ref[:, 5 * D_MODEL:6 * D_MODEL]
    h = (_rms(x, gin_ref[...]) * (1.0 + sc) + sh).astype(BF16)
    ck = FFN_HIDDEN // FFN_SPLIT
    acc = None
    for c in range(FFN_SPLIT):
        a = _dot(h, w1_ref[:, c * ck:(c + 1) * ck])
        b = _dot(h, w1_ref[:, FFN_HIDDEN + c * ck:FFN_HIDDEN + (c + 1) * ck])
        act = (a * _sigmoid(a) * b).astype(BF16)
        part = _dot(act, w2_ref[c * ck:(c + 1) * ck, :])
        acc = part if acc is None else acc + part
    y = x + g2 * _rms(acc, gout_ref[...])
    if len(o_refs) == 1:
        o_refs[0][...] = y
    else:
        is_ctx = pl.program_id(0) < NTOK_C // TM

        @pl.when(is_ctx)
        def _():
            o_refs[0][...] = y

        @pl.when(jnp.logical_not(is_ctx))
        def _():
            o_refs[1][...] = y


def _ffn(x, mod, gains, w1, w2, layer, split_out):
    nct = NTOK_C // TM
    if split_out:
        out_specs = [pl.BlockSpec((TM, D_MODEL), lambda i: (jnp.minimum(i, nct - 1), 0)),
                     pl.BlockSpec((TM, D_MODEL), lambda i: (jnp.maximum(i - nct, 0), 0))]
        out_shape = [jax.ShapeDtypeStruct((NTOK_C, D_MODEL), F32), jax.ShapeDtypeStruct((NTOK_L, D_MODEL), F32)]
    else:
        out_specs = pl.BlockSpec((TM, D_MODEL), lambda i: (i, 0))
        out_shape = jax.ShapeDtypeStruct((NTOK, D_MODEL), F32)
    return pl.pallas_call(
        _ffn_kernel,
        grid=(NTOK // TM,),
        in_specs=[
            pl.BlockSpec((TM, D_MODEL), lambda i: (i, 0)),
            _mod_spec(layer),
            _gain_spec(layer, 2),
            _gain_spec(layer, 3),
            _layer_spec((D_MODEL, 2 * FFN_HIDDEN), layer),
            _layer_spec((FFN_HIDDEN, D_MODEL), layer),
        ],
        out_specs=out_specs,
        out_shape=out_shape,
        compiler_params=pltpu.CompilerParams(vmem_limit_bytes=VMEM_LIMIT),
        name="ffn",
    )(x, mod, gains, gains, w1, w2)


def kernel(x_prompt, x_sample, cache_k, cache_v, state_s5, state_gla, c, c_ctx, w_mod, b_mod, norm_g, w_in,
           s5_lam_re, s5_lam_im, s5_log_step, s5_b_re, s5_b_im, s5_c_re, s5_c_im, s5_d, w_glu, gla_w_gk,
           gla_b_gk, gla_norm_g, att_sink, w_branch, w_out, w_ffn_in, w_ffn_out):
    cond = jnp.concatenate([c_ctx[None, :], c, jnp.zeros((N_MOD_ROWS - 1 - DEC_BATCH, D_MODEL), F32)], axis=0)
    mod_all = _modulation(cond, w_mod, b_mod).reshape(DEPTH, N_MOD_ROWS, 1, 6 * D_MODEL)
    cos_t, sin_t = _rope_tables()
    xs = (x_prompt.reshape(NTOK_C, D_MODEL), x_sample.reshape(NTOK_L, D_MODEL))
    w_in_b = w_in.astype(BF16)
    w_in_end = jnp.pad(w_in[:, :, D_IN_TILED:].astype(BF16), ((0, 0), (0, 0), (0, W_IN_COLS - D_IN)))
    w_glu_b, w_branch_b, w_out_b = w_glu.astype(BF16), w_branch.astype(BF16), w_out.astype(BF16)
    w_ffn_in_b, w_ffn_out_b = w_ffn_in.astype(BF16), w_ffn_out.astype(BF16)
    s5_params = _s5_params(s5_lam_re, s5_lam_im, s5_log_step, s5_b_re, s5_b_im, s5_c_re, s5_c_im)
    h0_all = state_s5.astype(F32).transpose(1, 0, 3, 5, 2, 4).reshape(DEPTH, DEC_BATCH, S5_GROUPS * 256)
    wgk_all = jnp.stack([jnp.pad(gla_w_gk[:, d], ((0, 0), (d * GLA_RANK, 128 - (d + 1) * GLA_RANK), (0, 0)))
                         for d in range(2)], axis=1).astype(BF16)
    bgk_all = gla_b_gk[:, :, None, :].astype(F32)
    gains = norm_g.astype(F32).reshape(DEPTH * 4, 1, D_MODEL)
    gla_gain = gla_norm_g.astype(F32).reshape(DEPTH, 1, GLA_DV)
    sink = att_sink.astype(F32)
    cache_k2 = cache_k.astype(F32).reshape(DEC_BATCH, DEPTH, PAST_LEN, ATT_KV)
    cache_v2 = cache_v.astype(F32).reshape(DEC_BATCH, DEPTH, PAST_LEN, ATT_KV)
    state_gla = state_gla.astype(F32)
    mod = mod_all
    new_k, new_v, new_s5, new_gla = [], [], [], []
    for i in range(DEPTH):
        uj, bslab, cslab, gates, lr = _inproj(xs, mod, gains, w_in_b, w_in_end, i)

        wt, web, wca, a16, dj = _s5_prep(s5_params, s5_d[i], i)
        hin, finc = _s5_scan(_s5_state(uj, web), a16, h0_all[i])
        ys5 = _s5_out(uj, hin, wt, wca, dj)
        new_s5.append(finc)

        *og, gla_fin = _gla_mix(bslab, lr, wgk_all, bgk_all, state_gla, i)
        new_gla.append(gla_fin[:BATCH])

        yc_ctx, k_new, v_new = _attn_ctx(sink, cslab, i)
        yc = (yc_ctx, _attn_lat(sink, cslab, cache_k2, cache_v2, cos_t, sin_t, i))
        new_k.append(k_new.reshape(BATCH, SEQ, ATT_KV_HEADS, HEAD_DIM))
        new_v.append(v_new.reshape(BATCH, SEQ, ATT_KV_HEADS, HEAD_DIM))

        x = _merge(xs, mod, gains, ys5, og, bslab, yc, gates, gla_gain, w_glu_b, w_branch_b, w_out_b, i)
        last = i == DEPTH - 1
        x = _ffn(x, mod, gains, w_ffn_in_b, w_ffn_out_b, i, last)
        xs = tuple(x) if last else (x,)

    return (xs[0].reshape(BATCH, SEQ, D_MODEL), xs[1].reshape(DEC_BATCH, DEC_SEQ, D_MODEL),
            jnp.stack(new_k, axis=1), jnp.stack(new_v, axis=1),
            jnp.stack(new_s5).reshape(DEPTH, BATCH, S5_GROUPS, 2, 2, S5_STATE).transpose(1, 0, 4, 2, 5, 3),
            jnp.stack(new_gla, axis=1))
```
